```python
import jax, jax.numpy as jnp
from jax import lax
import numpy as np

D_MODEL = 2048
BATCH = 1
SEQ = 8192
DEPTH = 2

CHUNK = 64
N_META = 16
N_BRANCH = 4
MIX_W = D_MODEL // 4
FOX_HEADS = 4
FOX_HD = MIX_W // FOX_HEADS
FOX_FORGET_BIAS = 3.0
Q_BLOCK = 128
GLA_HEADS = 4
GLA_DK = MIX_W // (2 * GLA_HEADS)
GLA_DV = MIX_W // GLA_HEADS
GLA_RANK = 16
GLA_TAU = 16.0
CONV_W = MIX_W
CONV_K = 3
POOL_WINDOWS = (2, 4, 8, 16)
POOL_GROUPS = len(POOL_WINDOWS)
POOL_GW = MIX_W // POOL_GROUPS
N_EXPERTS = 32
N_GROUPS = 8
EXPERTS_PER_GROUP = N_EXPERTS // N_GROUPS
TOP_K = 2
D_EXPERT = D_MODEL // 2
EXPERT_BLOCK = 128
LN_EPS = 1e-5
DEEPNORM_ALPHA = (2 * DEPTH) ** 0.25
DEEPNORM_BETA = (8 * DEPTH) ** -0.25

IN_SPLITS = (
    MIX_W, MIX_W, MIX_W, FOX_HEADS,
    GLA_HEADS * GLA_DK, GLA_HEADS * GLA_DK, MIX_W, GLA_RANK, MIX_W,
    CONV_W, CONV_W, CONV_W,
    MIX_W,
    N_BRANCH * D_MODEL,
)
P_IN = sum(IN_SPLITS)

kernel_name = 'hybrid_fox_gla_conv_pool_grouped_moe'


def layer_norm(x, g, b):
    xf = x.astype(jnp.float32)
    mu = jnp.mean(xf, -1, keepdims=True)
    var = jnp.mean(jnp.square(xf - mu), -1, keepdims=True)
    return ((xf - mu) * lax.rsqrt(var + LN_EPS) * g + b).astype(x.dtype)


def rms_norm(x, g):
    xf = x.astype(jnp.float32)
    return xf * lax.rsqrt(jnp.mean(jnp.square(xf), -1, keepdims=True) + LN_EPS) * g


def forgetting_attention(q, k, v, log_f):
    bn, seq_len, n_h, hd = q.shape
    lp = -(-seq_len // Q_BLOCK) * Q_BLOCK
    pad = lp - seq_len
    pw = ((0, 0), (0, pad), (0, 0), (0, 0))
    q, k, v = jnp.pad(q, pw), jnp.pad(k, pw), jnp.pad(v, pw)
    c = jnp.pad(jnp.cumsum(log_f, axis=1), ((0, 0), (0, pad), (0, 0)), mode='edge')
    c = jnp.transpose(c, (0, 2, 1))
    k_pos = jnp.arange(lp)
    scale = hd ** -0.5

    def query_block(i):
        start = i * Q_BLOCK
        qb = lax.dynamic_slice_in_dim(q, start, Q_BLOCK, axis=1)
        cq = lax.dynamic_slice_in_dim(c, start, Q_BLOCK, axis=2)
        s = jnp.einsum('bqhd,bkhd->bhqk', qb, k).astype(jnp.float32) * scale
        s = s + cq[..., :, None] - c[..., None, :]
        q_pos = start + jnp.arange(Q_BLOCK)
        s = jnp.where(k_pos[None, :] <= q_pos[:, None], s, -jnp.inf)
        p = jax.nn.softmax(s, axis=-1).astype(v.dtype)
        return jnp.einsum('bhqk,bkhd->bqhd', p, v)

    out = lax.map(query_block, jnp.arange(lp // Q_BLOCK))
    out = jnp.moveaxis(out, 0, 1).reshape(bn, lp, n_h, hd)
    return out[:, :seq_len]


def gated_linear_attention(q, k, v, log_a):
    bn, seq_len, n_h, dk = q.shape
    dv = v.shape[-1]
    front = (-N_META) % CHUNK
    back = (-(front + seq_len)) % CHUNK
    pw = ((0, 0), (front, back), (0, 0), (0, 0))
    f32 = jnp.float32
    q = jnp.pad(q.astype(f32) * dk ** -0.5, pw)
    k = jnp.pad(k.astype(f32), pw)
    v = jnp.pad(v.astype(f32), pw)
    log_a = jnp.pad(log_a.astype(f32), pw)
    n_chunks = (front + seq_len + back) // CHUNK

    def to_chunks(a):
        return jnp.moveaxis(a.reshape(bn, n_chunks, CHUNK, n_h, a.shape[-1]), 1, 0)

    tri = jnp.tril(jnp.ones((CHUNK, CHUNK), dtype=bool))

    def chunk_step(state, inp):
        qc, kc, vc, gc = inp
        bc = jnp.cumsum(gc, axis=1)
        o_inter = jnp.einsum('bthk,bhkv->bthv', qc * jnp.exp(bc), state)
        diff = bc[:, :, None] - bc[:, None, :]
        decay = jnp.exp(jnp.where(tri[None, :, :, None, None], diff, -jnp.inf))
        att = jnp.einsum('bthk,bshk,btshk->bths', qc, kc, decay)
        o_intra = jnp.einsum('bths,bshv->bthv', att, vc)
        last = bc[:, -1]
        state = jnp.exp(last)[..., None] * state + jnp.einsum(
            'bshk,bshv->bhkv', kc * jnp.exp(last[:, None] - bc), vc)
        return state, o_intra + o_inter

    s0 = jnp.zeros((bn, n_h, dk, dv), f32)
    _, out = lax.scan(chunk_step, s0, (to_chunks(q), to_chunks(k), to_chunks(v), to_chunks(log_a)))
    out = jnp.moveaxis(out, 0, 1).reshape(bn, n_chunks * CHUNK, n_h, dv)
    return out[:, front:front + seq_len]


def short_gated_conv(b_gate, c_gate, val, conv_w):
    u = c_gate * val
    y = lax.conv_general_dilated(
        u, conv_w.astype(u.dtype)[:, None, :], window_strides=(1,),
        padding=[(CONV_K - 1, 0)], dimension_numbers=('NWC', 'WIO', 'NWC'),
        feature_group_count=u.shape[-1])
    return b_gate * y


def multiscale_pool(p, pool_w, pool_scale):
    seq_len = p.shape[1]
    pf = p.astype(jnp.float32)
    t1 = jnp.arange(1, seq_len + 1, dtype=jnp.float32)
    outs = []
    for g, w in enumerate(POOL_WINDOWS):
        xg = pf[..., g * POOL_GW:(g + 1) * POOL_GW]
        cs = jnp.cumsum(xg, axis=1)
        lower = jnp.pad(cs[:, :seq_len - w], ((0, 0), (w, 0), (0, 0)))
        cnt = jnp.minimum(t1, float(w))[None, :, None]
        pooled = (cs - lower) / cnt - xg
        outs.append(jnp.einsum('blc,cd->bld', pooled.astype(p.dtype), pool_w[g]))
    return jnp.concatenate(outs, axis=-1) * pool_scale


def grouped_moe(x, router_w, router_b, w_gate, w_up, w_down):
    bn, seq_len, d = x.shape
    n_tok = bn * seq_len
    xt = x.reshape(n_tok, d)
    logits = jnp.dot(xt.astype(jnp.float32), router_w.astype(jnp.float32))
    aff = jax.nn.sigmoid(logits)
    sel = (aff + router_b.astype(jnp.float32)).reshape(n_tok, N_GROUPS, EXPERTS_PER_GROUP)
    group_score = jnp.sum(lax.top_k(sel, TOP_K)[0], axis=-1)
    grp = jnp.argmax(group_score, axis=-1)
    tok_ids = jnp.arange(n_tok)
    _, idx = lax.top_k(sel[tok_ids, grp], TOP_K)
    eid = grp[:, None] * EXPERTS_PER_GROUP + idx
    wts = aff[tok_ids[:, None], eid]
    wts = wts / jnp.sum(wts, axis=-1, keepdims=True)

    flat_e = eid.reshape(-1)
    flat_tok = jnp.repeat(tok_ids, TOP_K)
    flat_w = wts.reshape(-1)
    order = jnp.argsort(flat_e)
    se = flat_e[order]
    counts = jnp.bincount(flat_e, length=N_EXPERTS)
    padded = (counts + EXPERT_BLOCK - 1) // EXPERT_BLOCK * EXPERT_BLOCK
    pad_end = jnp.cumsum(padded)
    pad_start = pad_end - padded
    start = jnp.cumsum(counts) - counts
    dest = pad_start[se] + jnp.arange(n_tok * TOP_K) - start[se]
    n_blocks = -(-(n_tok * TOP_K) // EXPERT_BLOCK) + N_EXPERTS
    n_rows = n_blocks * EXPERT_BLOCK
    row_tok = jnp.full((n_rows,), n_tok, dtype=jnp.int32).at[dest].set(flat_tok[order].astype(jnp.int32))
    row_w = jnp.zeros((n_rows,), jnp.float32).at[dest].set(flat_w[order])
    block_e = jnp.minimum(
        jnp.searchsorted(pad_end, jnp.arange(n_blocks) * EXPERT_BLOCK, side='right'), N_EXPERTS - 1)
    x_rows = jnp.concatenate([xt, jnp.zeros((1, d), xt.dtype)], axis=0)[row_tok]
    x_rows = x_rows.reshape(n_blocks, EXPERT_BLOCK, d)

    def expert_block(args):
        xb, e = args
        h = jax.nn.silu(xb @ w_gate[e]) * (xb @ w_up[e])
        return h @ w_down[e]

    y_rows = lax.map(expert_block, (x_rows, block_e)).reshape(n_rows, d)
    out = jax.ops.segment_sum(y_rows.astype(jnp.float32) * row_w[:, None], row_tok,
                              num_segments=n_tok + 1)[:n_tok]
    return out.reshape(bn, seq_len, d).astype(x.dtype)


def setup_inputs(seed: int = 0) -> dict:
    key = jax.random.key(seed)
    ks = jax.random.split(key, 24)
    f32 = jnp.float32

    def nrm(k, shape, scale):
        return jax.random.normal(k, shape, f32) * scale

    return {
        'x': nrm(ks[0], (BATCH, SEQ, D_MODEL), 1.0),
        'meta_tokens': nrm(ks[1], (N_META, D_MODEL), 1.0),
        'ln_in_g': 1.0 + nrm(ks[2], (D_MODEL,), 0.02),
        'ln_in_b': nrm(ks[3], (D_MODEL,), 0.02),
        'w_in': nrm(ks[4], (DEPTH, D_MODEL, P_IN), D_MODEL ** -0.5),
        'fox_f_bias': FOX_FORGET_BIAS + nrm(ks[5], (DEPTH, FOX_HEADS), 0.5),
        'gla_wa2': nrm(ks[6], (DEPTH, GLA_RANK, GLA_HEADS * GLA_DK), GLA_RANK ** -0.5),
        'gla_ba': nrm(ks[7], (DEPTH, GLA_HEADS * GLA_DK), 0.1),
        'gla_norm_g': 1.0 + nrm(ks[8], (DEPTH, MIX_W), 0.02),
        'conv_w': nrm(ks[9], (DEPTH, CONV_K, CONV_W), CONV_K ** -0.5),
        'pool_w': nrm(ks[10], (DEPTH, POOL_GROUPS, POOL_GW, POOL_GW), POOL_GW ** -0.5),
        'pool_scale': 1.0 + nrm(ks[11], (DEPTH, MIX_W), 0.02),
        'gate_b': nrm(ks[12], (DEPTH, N_BRANCH, D_MODEL), 0.02),
        'w_branch': nrm(ks[13], (DEPTH, N_BRANCH, MIX_W, D_MODEL), MIX_W ** -0.5),
        'w_out': nrm(ks[14], (DEPTH, D_MODEL, D_MODEL), D_MODEL ** -0.5 * DEEPNORM_BETA),
        'ln1_g': 1.0 + nrm(ks[15], (DEPTH, D_MODEL), 0.02),
        'ln1_b': nrm(ks[16], (DEPTH, D_MODEL), 0.02),
        'router_w': nrm(ks[17], (D_MODEL, N_EXPERTS), D_MODEL ** -0.5),
        'router_b': nrm(ks[18], (N_EXPERTS,), 0.01),
        'w_gate': nrm(ks[19], (DEPTH, N_EXPERTS, D_MODEL, D_EXPERT), D_MODEL ** -0.5),
        'w_up': nrm(ks[20], (DEPTH, N_EXPERTS, D_MODEL, D_EXPERT), D_MODEL ** -0.5),
        'w_down': nrm(ks[21], (DEPTH, N_EXPERTS, D_EXPERT, D_MODEL), D_EXPERT ** -0.5 * DEEPNORM_BETA),
        'ln2_g': 1.0 + nrm(ks[22], (DEPTH, D_MODEL), 0.02),
        'ln2_b': nrm(ks[23], (DEPTH, D_MODEL), 0.02),
    }


def reference(x, meta_tokens, ln_in_g, ln_in_b, w_in, fox_f_bias, gla_wa2, gla_ba, gla_norm_g,
              conv_w, pool_w, pool_scale, gate_b, w_branch, w_out, ln1_g, ln1_b,
              router_w, router_b, w_gate, w_up, w_down, ln2_g, ln2_b):
    bn = x.shape[0]
    meta = jnp.broadcast_to(meta_tokens[None].astype(x.dtype), (bn, N_META, D_MODEL))
    h = layer_norm(jnp.concatenate([meta, x], axis=1), ln_in_g, ln_in_b)
    seq_len = h.shape[1]
    offsets = [int(o) for o in np.cumsum(IN_SPLITS)[:-1]]
    for l in range(DEPTH):
        z = jnp.einsum('bld,dp->blp', h, w_in[l])
        (fq, fk, fv, ff, gq, gk, gv, ga, gr, cb, cc, cv, pz, gz) = jnp.split(z, offsets, axis=-1)
        log_f = jax.nn.log_sigmoid((ff + fox_f_bias[l]).astype(jnp.float32))
        o_a = forgetting_attention(
            fq.reshape(bn, seq_len, FOX_HEADS, FOX_HD), fk.reshape(bn, seq_len, FOX_HEADS, FOX_HD),
            fv.reshape(bn, seq_len, FOX_HEADS, FOX_HD), log_f).reshape(bn, seq_len, MIX_W)
        log_a = jax.nn.log_sigmoid((ga @ gla_wa2[l] + gla_ba[l]).astype(jnp.float32)) / GLA_TAU
        o = gated_linear_attention(
            gq.reshape(bn, seq_len, GLA_HEADS, GLA_DK), gk.reshape(bn, seq_len, GLA_HEADS, GLA_DK),
            gv.reshape(bn, seq_len, GLA_HEADS, GLA_DV), log_a.reshape(bn, seq_len, GLA_HEADS, GLA_DK))
        o = rms_norm(o, gla_norm_g[l].reshape(GLA_HEADS, GLA_DV)).reshape(bn, seq_len, MIX_W)
        o_b = (o * jax.nn.silu(gr.astype(jnp.float32))).astype(h.dtype)
        o_c = short_gated_conv(cb, cc, cv, conv_w[l])
        o_d = multiscale_pool(pz, pool_w[l], pool_scale[l])
        gates = jax.nn.sigmoid(gz.reshape(bn, seq_len, N_BRANCH, D_MODEL) + gate_b[l])
        mixed = gates[:, :, 0] * jnp.einsum('blc,cd->bld', o_a, w_branch[l, 0])
        for b, ob in ((1, o_b), (2, o_c), (3, o_d)):
            mixed = mixed + gates[:, :, b] * jnp.einsum('blc,cd->bld', ob, w_branch[l, b])
        mix_out = jnp.einsum('bld,de->ble', mixed.astype(h.dtype), w_out[l])
        h = layer_norm(DEEPNORM_ALPHA * h + mix_out, ln1_g[l], ln1_b[l])
        ffn_out = grouped_moe(h, router_w, router_b, w_gate[l], w_up[l], w_down[l])
        h = layer_norm(DEEPNORM_ALPHA * h + ffn_out, ln2_g[l], ln2_b[l])
    return h[:, N_META:]
```

```python
import functools

import jax
import jax.numpy as jnp
import numpy as np
from jax import lax
from jax.experimental import pallas as pl
from jax.experimental.pallas import tpu as pltpu

F32 = jnp.float32
BF16 = jnp.bfloat16
HIGHEST = lax.Precision.HIGHEST

D_MODEL = 2048
SEQ = 8192
DEPTH = 2
N_META = 16
N_BRANCH = 4
MIX_W = 512
FOX_HEADS = 4
FOX_HD = 128
GLA_HEADS = 4
GLA_DK = 64
GLA_DV = 128
GLA_RANK = 16
GLA_TAU = 16.0
CONV_K = 3
POOL_WINDOWS = (2, 4, 8, 16)
POOL_GW = 128
N_EXPERTS = 32
N_GROUPS = 8
EXPERTS_PER_GROUP = 4
TOP_K = 2
D_EXPERT = 1024
LN_EPS = 1e-5
DEEPNORM_ALPHA = (2 * DEPTH) ** 0.25

_SPLITS = (512, 512, 512, 4, 256, 256, 512, 16, 512, 512, 512, 512, 512, 8192)
_OFFS = np.concatenate([[0], np.cumsum(_SPLITS)]).astype(int)
(O_FQ, O_FK, O_FV, O_FF, O_GQ, O_GK, O_GV, O_GA, O_GR, O_CB, O_CC, O_CV, O_PZ, O_GZ, _) = [int(o) for o in _OFFS]

LANE = 128
PAD_ROWS = LANE - N_META
T0 = PAD_ROWS
N_TOK = N_META + SEQ
R = PAD_ROWS + N_TOK
TM = 640
TM_PROJ = 1664
HALO = 16

C_GQ, C_GK, C_GV, C_GR, C_CB, C_CC, C_CV, C_PZ, C_SM = 0, 256, 512, 1024, 1536, 2048, 2560, 3072, 3584
SM_W = 256
SM_FF, SM_GA = 0, 16
N_MIX = C_SM + SM_W

EXPERT_BLOCK = 128
N_FLAT = N_TOK * TOP_K
N_BLOCKS = -(-N_FLAT // EXPERT_BLOCK) + N_EXPERTS
N_ROWS = N_BLOCKS * EXPERT_BLOCK

NEG = -1e30
VMEM_LIMIT = 48 * 1024 * 1024


def _cparams(sem, vmem=VMEM_LIMIT):
    return pltpu.CompilerParams(dimension_semantics=sem, vmem_limit_bytes=vmem)


def _log_sigmoid(x):
    return jnp.minimum(x, 0.0) - jnp.log1p(jnp.exp(-jnp.abs(x)))


def _sigmoid(x):
    return 1.0 / (1.0 + jnp.exp(-x))


def _layer_norm_rows(x, g, b):
    mu = jnp.mean(x, axis=-1, keepdims=True)
    xc = x - mu
    var = jnp.mean(xc * xc, axis=-1, keepdims=True)
    return xc * lax.rsqrt(var + LN_EPS) * g + b


def _ln_in_kernel(x_ref, meta_ref, g_ref, b_ref, h_ref, hb_ref):
    i = pl.program_id(0)

    @pl.when(i == 0)
    def _():
        h_ref[...] = jnp.zeros_like(h_ref)
        hb_ref[...] = jnp.zeros_like(hb_ref)
        m = _layer_norm_rows(meta_ref[...], g_ref[...], b_ref[...])
        h_ref[PAD_ROWS:, :] = m
        hb_ref[PAD_ROWS:, :] = m.astype(BF16)

    @pl.when(i > 0)
    def _():
        y = _layer_norm_rows(x_ref[...], g_ref[...], b_ref[...])
        h_ref[...] = y
        hb_ref[...] = y.astype(BF16)


def _ln_in(x2d, meta, g, b):
    nb = R // LANE
    return pl.pallas_call(
        _ln_in_kernel,
        grid=(nb,),
        in_specs=[
            pl.BlockSpec((LANE, D_MODEL), lambda i: (jnp.maximum(i - 1, 0), 0)),
            pl.BlockSpec((N_META, D_MODEL), lambda i: (0, 0)),
            pl.BlockSpec((1, D_MODEL), lambda i: (0, 0)),
            pl.BlockSpec((1, D_MODEL), lambda i: (0, 0)),
        ],
        out_specs=[
            pl.BlockSpec((LANE, D_MODEL), lambda i: (i, 0)),
            pl.BlockSpec((LANE, D_MODEL), lambda i: (i, 0)),
        ],
        out_shape=[
            jax.ShapeDtypeStruct((R, D_MODEL), F32),
            jax.ShapeDtypeStruct((R, D_MODEL), BF16),
        ],
        compiler_params=_cparams(("arbitrary",)),
        name="ln_in",
    )(x2d, meta, g.reshape(1, -1), b.reshape(1, -1))


def _mm_kernel(a_ref, w_ref, o_ref):
    o_ref[...] = jnp.dot(a_ref[...], w_ref[...], preferred_element_type=F32).astype(o_ref.dtype)


def _matmul(a, w, out_dtype, tm, tn, name):
    m, k = a.shape
    n = w.shape[1]
    return pl.pallas_call(
        _mm_kernel,
        grid=(m // tm, n // tn),
        in_specs=[
            pl.BlockSpec((tm, k), lambda i, j: (i, 0)),
            pl.BlockSpec((k, tn), lambda i, j: (0, j)),
        ],
        out_specs=pl.BlockSpec((tm, tn), lambda i, j: (i, j)),
        out_shape=jax.ShapeDtypeStruct((m, n), out_dtype),
        compiler_params=_cparams(("parallel", "arbitrary")),
        name=name,
    )(a, w)


def _fox_gate_kernel(zs_ref, bias_ref, c_ref, carry_ref):
    i = pl.program_id(0)

    @pl.when(i == 0)
    def _():
        carry_ref[...] = jnp.zeros_like(carry_ref)

    rows = i * TM + lax.broadcasted_iota(jnp.int32, (TM, LANE), 0)
    lf = _log_sigmoid(zs_ref[...] + bias_ref[...])
    lf = jnp.where(rows >= T0, lf, 0.0)
    tri = (lax.broadcasted_iota(jnp.int32, (TM, TM), 0)
           >= lax.broadcasted_iota(jnp.int32, (TM, TM), 1)).astype(F32)
    c = jnp.dot(tri, lf, precision=HIGHEST, preferred_element_type=F32) + carry_ref[...]
    c_ref[...] = c
    carry_ref[...] = c[TM - 1:TM, :]


def _fox_gate(z, bias_row):
    return pl.pallas_call(
        _fox_gate_kernel,
        grid=(R // TM,),
        in_specs=[
            pl.BlockSpec((TM, LANE), lambda i: (i, C_SM // LANE)),
            pl.BlockSpec((1, LANE), lambda i: (0, 0)),
        ],
        out_specs=pl.BlockSpec((TM, LANE), lambda i: (i, 0)),
        out_shape=jax.ShapeDtypeStruct((R, LANE), F32),
        scratch_shapes=[pltpu.VMEM((1, LANE), F32)],
        compiler_params=_cparams(("arbitrary",)),
        name="fox_gate",
    )(z, bias_row)


TQ = TM
N_QB = R // TQ
_PAIRS = [(qi, kj) for qi in range(N_QB) for kj in range(qi + 1)]
N_PAIRS = len(_PAIRS)


def _fox_kernel(qi_tab, kj_tab, q_ref, k_ref, v_ref, cq_ref, ck_ref, o_ref, m_sc, l_sc, acc_sc):
    p = pl.program_id(1)
    qi = qi_tab[p]
    kj = kj_tab[p]

    @pl.when(kj == 0)
    def _():
        m_sc[...] = jnp.full_like(m_sc, NEG)
        l_sc[...] = jnp.zeros_like(l_sc)
        acc_sc[...] = jnp.zeros_like(acc_sc)

    def step(masked):
        s = lax.dot_general(q_ref[...], k_ref[...], (((1,), (1,)), ((), ())),
                            preferred_element_type=F32)
        s = s * (FOX_HD ** -0.5) + cq_ref[...] - ck_ref[...]
        if masked:
            qpos = qi * TQ + lax.broadcasted_iota(jnp.int32, (TQ, TQ), 0)
            kpos = kj * TQ + lax.broadcasted_iota(jnp.int32, (TQ, TQ), 1)
            ok = (kpos <= qpos) & ((kpos >= T0) | (qpos < T0))
            s = jnp.where(ok, s, NEG)
        m_prev = m_sc[...]
        m_new = jnp.maximum(m_prev, jnp.max(s, axis=-1, keepdims=True))
        alpha = jnp.exp(m_prev - m_new)
        pr = jnp.exp(s - m_new)
        l_sc[...] = alpha * l_sc[...] + jnp.sum(pr, axis=-1, keepdims=True)
        acc_sc[...] = alpha * acc_sc[...] + jnp.dot(pr.astype(BF16), v_ref[...],
                                                    preferred_element_type=F32)
        m_sc[...] = m_new

    needs_mask = (kj == qi) | (kj == 0)

    @pl.when(needs_mask)
    def _():
        step(True)

    @pl.when(jnp.logical_not(needs_mask))
    def _():
        step(False)

    @pl.when(kj == qi)
    def _():
        o_ref[...] = (acc_sc[...] / l_sc[...]).astype(o_ref.dtype)


def _fox_attention(zf, c_col, c_row):
    qi_tab = jnp.asarray([p[0] for p in _PAIRS], jnp.int32)
    kj_tab = jnp.asarray([p[1] for p in _PAIRS], jnp.int32)
    grid_spec = pltpu.PrefetchScalarGridSpec(
        num_scalar_prefetch=2,
        grid=(FOX_HEADS, N_PAIRS),
        in_specs=[
            pl.BlockSpec((TQ, FOX_HD), lambda h, p, qt, kt: (qt[p], h)),
            pl.BlockSpec((TQ, FOX_HD), lambda h, p, qt, kt: (kt[p], FOX_HEADS + h)),
            pl.BlockSpec((TQ, FOX_HD), lambda h, p, qt, kt: (kt[p], 2 * FOX_HEADS + h)),
            pl.BlockSpec((None, TQ, 1), lambda h, p, qt, kt: (h, qt[p], 0)),
            pl.BlockSpec((None, 1, TQ), lambda h, p, qt, kt: (h, 0, kt[p])),
        ],
        out_specs=pl.BlockSpec((TQ, FOX_HD), lambda h, p, qt, kt: (qt[p], h)),
        scratch_shapes=[
            pltpu.VMEM((TQ, 1), F32),
            pltpu.VMEM((TQ, 1), F32),
            pltpu.VMEM((TQ, FOX_HD), F32),
        ],
    )
    return pl.pallas_call(
        _fox_kernel,
        grid_spec=grid_spec,
        out_shape=jax.ShapeDtypeStruct((R, MIX_W), BF16),
        compiler_params=_cparams(("parallel", "arbitrary")),
        name="fox_attention",
    )(qi_tab, kj_tab, zf, zf, zf, c_col, c_row)


GLA_CHUNK = 64


def _gla_kernel(q_ref, k_ref, v_ref, gr_ref, zs_ref, wa2_ref, ba_ref, gn_ref, o_ref, st_ref, la_ref):
    i = pl.program_id(0)

    @pl.when(i == 0)
    def _():
        st_ref[...] = jnp.zeros_like(st_ref)

    la = jnp.dot(zs_ref[...], wa2_ref[...], precision=HIGHEST, preferred_element_type=F32)
    la_ref[...] = _log_sigmoid(la + ba_ref[...]) * (1.0 / GLA_TAU)

    c_r = lax.broadcasted_iota(jnp.int32, (GLA_CHUNK, GLA_CHUNK), 0)
    c_c = lax.broadcasted_iota(jnp.int32, (GLA_CHUNK, GLA_CHUNK), 1)
    tri_b = c_r >= c_c
    tri = tri_b.astype(F32)

    def chunk(c, carry):
        r0 = pl.multiple_of(c * GLA_CHUNK, GLA_CHUNK)
        rows = pl.ds(r0, GLA_CHUNK)
        g = la_ref[rows, :]
        b = jnp.dot(tri, g, precision=HIGHEST, preferred_element_type=F32)
        b_last = b[GLA_CHUNK - 1:GLA_CHUNK, :]
        e_last = jnp.exp(b_last)
        qt = q_ref[rows, :] * (GLA_DK ** -0.5) * jnp.exp(b)
        kt = k_ref[rows, :] * jnp.exp(-b)
        kh = kt * e_last
        for h in range(GLA_HEADS):
            ks = slice(h * GLA_DK, (h + 1) * GLA_DK)
            vs = slice(h * GLA_DV, (h + 1) * GLA_DV)
            q_h = qt[:, ks].astype(BF16)
            k_h = kt[:, ks].astype(BF16)
            kh_h = kh[:, ks].astype(BF16)
            v_h = v_ref[rows, vs]
            att = lax.dot_general(q_h, k_h, (((1,), (1,)), ((), ())), preferred_element_type=F32)
            att = jnp.where(tri_b, att, 0.0)
            st = st_ref[h]
            o = jnp.dot(att.astype(BF16), v_h.astype(BF16), preferred_element_type=F32)
            o = o + lax.dot_general(q_h, st.astype(BF16), (((1,), (1,)), ((), ())),
                                    preferred_element_type=F32)
            st_ref[h] = st * e_last[:, ks] + jnp.dot(v_h.T.astype(BF16), kh_h,
                                                     preferred_element_type=F32)
            ms = jnp.mean(o * o, axis=-1, keepdims=True)
            on = o * lax.rsqrt(ms + LN_EPS) * gn_ref[:, vs]
            gate = gr_ref[rows, vs]
            o_ref[rows, vs] = (on * (gate * _sigmoid(gate))).astype(o_ref.dtype)
        return carry

    lax.fori_loop(0, TM // GLA_CHUNK, chunk, 0)


def _gla(z, wa2p, ba, gn):
    return pl.pallas_call(
        _gla_kernel,
        grid=(R // TM,),
        in_specs=[
            pl.BlockSpec((TM, 256), lambda i: (i, C_GQ // 256)),
            pl.BlockSpec((TM, 256), lambda i: (i, C_GK // 256)),
            pl.BlockSpec((TM, 512), lambda i: (i, C_GV // 512)),
            pl.BlockSpec((TM, 512), lambda i: (i, C_GR // 512)),
            pl.BlockSpec((TM, LANE), lambda i: (i, C_SM // LANE)),
            pl.BlockSpec((LANE, 256), lambda i: (0, 0)),
            pl.BlockSpec((1, 256), lambda i: (0, 0)),
            pl.BlockSpec((1, 512), lambda i: (0, 0)),
        ],
        out_specs=pl.BlockSpec((TM, MIX_W), lambda i: (i, 0)),
        out_shape=jax.ShapeDtypeStruct((R, MIX_W), BF16),
        scratch_shapes=[
            pltpu.VMEM((GLA_HEADS, GLA_DV, GLA_DK), F32),
            pltpu.VMEM((TM, GLA_HEADS * GLA_DK), F32),
        ],
        compiler_params=_cparams(("arbitrary",)),
        name="gla",
    )(z, z, z, z, z, wa2p, ba, gn)


def _local_kernel(cb_ref, cc_ref, cv_ref, pz_ref, cw_ref, pw_ref, ps_ref, oc_ref, od_ref, u_sc, p_sc):
    i = pl.program_id(0)

    @pl.when(i == 0)
    def _():
        u_sc[0:HALO, :] = jnp.zeros((HALO, MIX_W), F32)
        p_sc[0:HALO, :] = jnp.zeros((HALO, MIX_W), F32)

    @pl.when(i > 0)
    def _():
        u_sc[0:HALO, :] = u_sc[TM:TM + HALO, :]
        p_sc[0:HALO, :] = p_sc[TM:TM + HALO, :]

    u = cc_ref[...] * cv_ref[...]
    pz = pz_ref[...]
    u_sc[HALO:, :] = u
    p_sc[HALO:, :] = pz

    y = (cw_ref[2:3, :] * u + cw_ref[1:2, :] * u_sc[HALO - 1:HALO - 1 + TM, :]
         + cw_ref[0:1, :] * u_sc[HALO - 2:HALO - 2 + TM, :])
    oc_ref[...] = (cb_ref[...] * y).astype(oc_ref.dtype)

    tok = i * TM - T0 + lax.broadcasted_iota(jnp.int32, (TM, 1), 0)
    cnt_small = jnp.maximum(tok + 1, 1).astype(F32)
    for g, w in enumerate(POOL_WINDOWS):
        cols = slice(g * POOL_GW, (g + 1) * POOL_GW)
        x = pz[:, cols]
        s = x
        for j in range(1, w):
            s = s + p_sc[HALO - j:HALO - j + TM, cols]
        inv_cnt = jnp.where(tok + 1 >= w, 1.0 / w, 1.0 / cnt_small)
        pooled = s * inv_cnt - x
        od = jnp.dot(pooled.astype(BF16), pw_ref[g], preferred_element_type=F32)
        od_ref[:, cols] = (od * ps_ref[:, cols]).astype(od_ref.dtype)


def _local_mixers(z, conv_w, pool_w_bf, pool_scale):
    cw = jnp.zeros((8, MIX_W), F32).at[:CONV_K].set(conv_w)
    blk = lambda c: pl.BlockSpec((TM, MIX_W), lambda i, c=c: (i, c // MIX_W))
    return pl.pallas_call(
        _local_kernel,
        grid=(R // TM,),
        in_specs=[
            blk(C_CB), blk(C_CC), blk(C_CV), blk(C_PZ),
            pl.BlockSpec((8, MIX_W), lambda i: (0, 0)),
            pl.BlockSpec((len(POOL_WINDOWS), POOL_GW, POOL_GW), lambda i: (0, 0, 0)),
            pl.BlockSpec((1, MIX_W), lambda i: (0, 0)),
        ],
        out_specs=[
            pl.BlockSpec((TM, MIX_W), lambda i: (i, 0)),
            pl.BlockSpec((TM, MIX_W), lambda i: (i, 0)),
        ],
        out_shape=[
            jax.ShapeDtypeStruct((R, MIX_W), BF16),
            jax.ShapeDtypeStruct((R, MIX_W), BF16),
        ],
        scratch_shapes=[
            pltpu.VMEM((TM + HALO, MIX_W), F32),
            pltpu.VMEM((TM + HALO, MIX_W), F32),
        ],
        compiler_params=_cparams(("arbitrary",)),
        name="conv_pool",
    )(z, z, z, z, cw, pool_w_bf, pool_scale.reshape(1, -1))


TN_MERGE = 256


def _merge_kernel(hb_ref, oa_ref, ob_ref, oc_ref, od_ref, wg0_ref, wg1_ref, wg2_ref, wg3_ref,
                  gb_ref, wb_ref, wo_ref, out_ref):
    j = pl.program_id(1)

    @pl.when(j == 0)
    def _():
        out_ref[...] = jnp.zeros_like(out_ref)

    hb = hb_ref[...]
    mixed = None
    for b, (o_ref, wg_ref) in enumerate(((oa_ref, wg0_ref), (ob_ref, wg1_ref),
                                         (oc_ref, wg2_ref), (od_ref, wg3_ref))):
        gate = _sigmoid(jnp.dot(hb, wg_ref[...], preferred_element_type=F32) + gb_ref[b:b + 1, :])
        proj = jnp.dot(o_ref[...], wb_ref[b], preferred_element_type=F32)
        term = gate * proj
        mixed = term if mixed is None else mixed + term
    out_ref[...] += jnp.dot(mixed.astype(BF16), wo_ref[...], preferred_element_type=F32)


def _merge(hb, o_a, o_b, o_c, o_d, wg_bf, gate_b, wb_bf, wo_bf):
    tn = TN_MERGE
    nj = D_MODEL // tn
    row = lambda w: pl.BlockSpec((TM, w), lambda i, j: (i, 0))
    wg = lambda b: pl.BlockSpec((D_MODEL, tn), lambda i, j, b=b: (0, b * nj + j))
    return pl.pallas_call(
        _merge_kernel,
        grid=(R // TM, nj),
        in_specs=[
            row(D_MODEL), row(MIX_W), row(MIX_W), row(MIX_W), row(MIX_W),
            wg(0), wg(1), wg(2), wg(3),
            pl.BlockSpec((N_BRANCH, tn), lambda i, j: (0, j)),
            pl.BlockSpec((N_BRANCH, MIX_W, tn), lambda i, j: (0, 0, j)),
            pl.BlockSpec((tn, D_MODEL), lambda i, j: (j, 0)),
        ],
        out_specs=pl.BlockSpec((TM, D_MODEL), lambda i, j: (i, 0)),
        out_shape=jax.ShapeDtypeStruct((R, D_MODEL), F32),
        compiler_params=_cparams(("parallel", "arbitrary")),
        name="merge",
    )(hb, o_a, o_b, o_c, o_d, wg_bf, wg_bf, wg_bf, wg_bf, gate_b, wb_bf, wo_bf)


def _post_ln(h, delta, g, b, row0):
    y = _layer_norm_rows(DEEPNORM_ALPHA * h + delta, g, b)
    rows = row0 + lax.broadcasted_iota(jnp.int32, (y.shape[0], 1), 0)
    return jnp.where(rows >= T0, y, 0.0)


def _ln1_kernel(h_ref, mix_ref, g_ref, b_ref, rw_ref, h1_ref, h1b_ref, lg_ref):
    y = _post_ln(h_ref[...], mix_ref[...], g_ref[...], b_ref[...], pl.program_id(0) * TM)
    h1_ref[...] = y
    h1b_ref[...] = y.astype(BF16)
    lg_ref[...] = jnp.dot(y, rw_ref[...], precision=HIGHEST, preferred_element_type=F32)


def _ln1_router(h, mix, g, b, router_wp):
    row = pl.BlockSpec((TM, D_MODEL), lambda i: (i, 0))
    vec = pl.BlockSpec((1, D_MODEL), lambda i: (0, 0))
    return pl.pallas_call(
        _ln1_kernel,
        grid=(R // TM,),
        in_specs=[row, row, vec, vec, pl.BlockSpec((D_MODEL, LANE), lambda i: (0, 0))],
        out_specs=[row, row, pl.BlockSpec((TM, LANE), lambda i: (i, 0))],
        out_shape=[
            jax.ShapeDtypeStruct((R, D_MODEL), F32),
            jax.ShapeDtypeStruct((R, D_MODEL), BF16),
            jax.ShapeDtypeStruct((R, LANE), F32),
        ],
        compiler_params=_cparams(("parallel",)),
        name="ln1_router",
    )(h, mix, g.reshape(1, -1), b.reshape(1, -1), router_wp)


def _row_copy(src_hbm, row, dst, r, sem):
    return pltpu.make_async_copy(src_hbm.at[pl.ds(row, 1), :], dst.at[pl.ds(r, 1), :], sem)


def _gather_kernel(rows_ref, h_hbm, o_ref, buf, sem):
    base = pl.program_id(0) * EXPERT_BLOCK

    def issue(r, carry):
        _row_copy(h_hbm, rows_ref[base + r], buf, r, sem).start()
        return carry

    lax.fori_loop(0, EXPERT_BLOCK, issue, 0)

    def drain(r, carry):
        _row_copy(h_hbm, 0, buf, r, sem).wait()
        return carry

    lax.fori_loop(0, EXPERT_BLOCK, drain, 0)
    o_ref[...] = buf[...].astype(o_ref.dtype)


def _gather_rows(rows, h1):
    grid_spec = pltpu.PrefetchScalarGridSpec(
        num_scalar_prefetch=1,
        grid=(N_BLOCKS,),
        in_specs=[pl.BlockSpec(memory_space=pl.ANY)],
        out_specs=pl.BlockSpec((EXPERT_BLOCK, D_MODEL), lambda i, rows: (i, 0)),
        scratch_shapes=[pltpu.VMEM((EXPERT_BLOCK, D_MODEL), F32), pltpu.SemaphoreType.DMA],
    )
    return pl.pallas_call(
        _gather_kernel,
        grid_spec=grid_spec,
        out_shape=jax.ShapeDtypeStruct((N_ROWS, D_MODEL), BF16),
        compiler_params=_cparams(("arbitrary",)),
        name="moe_gather",
    )(rows, h1)


def _expert_changed(be_ref, b):
    return (b == 0) | (be_ref[b] != be_ref[jnp.maximum(b - 1, 0)])


def _moe_up_kernel(be_ref, nu_ref, x_ref, wg_ref, wu_ref, o_ref, wg_sc, wu_sc):
    b = pl.program_id(1)

    @pl.when(_expert_changed(be_ref, b))
    def _():
        wg_sc[...] = wg_ref[...].astype(BF16)
        wu_sc[...] = wu_ref[...].astype(BF16)

    @pl.when(b < nu_ref[0])
    def _():
        x = x_ref[...]
        g = jnp.dot(x, wg_sc[...], preferred_element_type=F32)
        u = jnp.dot(x, wu_sc[...], preferred_element_type=F32)
        o_ref[...] = (g * _sigmoid(g) * u).astype(o_ref.dtype)

    @pl.when(b >= nu_ref[0])
    def _():
        o_ref[...] = jnp.zeros_like(o_ref)


def _moe_up(block_e, n_used, x_rows, w_gate, w_up):
    th = D_EXPERT // 2
    wspec = pl.BlockSpec((None, D_MODEL, th), lambda hf, b, be, nu: (be[b], 0, hf))
    grid_spec = pltpu.PrefetchScalarGridSpec(
        num_scalar_prefetch=2,
        grid=(2, N_BLOCKS),
        in_specs=[
            pl.BlockSpec((EXPERT_BLOCK, D_MODEL), lambda hf, b, be, nu: (b, 0)),
            wspec, wspec,
        ],
        out_specs=pl.BlockSpec((EXPERT_BLOCK, th), lambda hf, b, be, nu: (b, hf)),
        scratch_shapes=[pltpu.VMEM((D_MODEL, th), BF16), pltpu.VMEM((D_MODEL, th), BF16)],
    )
    return pl.pallas_call(
        _moe_up_kernel,
        grid_spec=grid_spec,
        out_shape=jax.ShapeDtypeStruct((N_ROWS, D_EXPERT), BF16),
        compiler_params=_cparams(("arbitrary", "arbitrary")),
        name="moe_up",
    )(block_e, n_used, x_rows, w_gate, w_up)


def _moe_down_kernel(be_ref, nu_ref, x_ref, wd_ref, rw_ref, o_ref, wd_sc):
    b = pl.program_id(1)

    @pl.when(_expert_changed(be_ref, b))
    def _():
        wd_sc[...] = wd_ref[...].astype(BF16)

    @pl.when(b < nu_ref[0])
    def _():
        y = jnp.dot(x_ref[...], wd_sc[...], preferred_element_type=F32)
        o_ref[...] = y * rw_ref[...]

    @pl.when(b >= nu_ref[0])
    def _():
        o_ref[...] = jnp.zeros_like(o_ref)


def _moe_down(block_e, n_used, hmid, w_down, row_w):
    th = D_MODEL // 2
    grid_spec = pltpu.PrefetchScalarGridSpec(
        num_scalar_prefetch=2,
        grid=(2, N_BLOCKS),
        in_specs=[
            pl.BlockSpec((EXPERT_BLOCK, D_EXPERT), lambda hf, b, be, nu: (b, 0)),
            pl.BlockSpec((None, D_EXPERT, th), lambda hf, b, be, nu: (be[b], 0, hf)),
            pl.BlockSpec((EXPERT_BLOCK, 1), lambda hf, b, be, nu: (b, 0)),
        ],
        out_specs=pl.BlockSpec((EXPERT_BLOCK, th), lambda hf, b, be, nu: (b, hf)),
        scratch_shapes=[pltpu.VMEM((D_EXPERT, th), BF16)],
    )
    return pl.pallas_call(
        _moe_down_kernel,
        grid_spec=grid_spec,
        out_shape=jax.ShapeDtypeStruct((N_ROWS, D_MODEL), F32),
        compiler_params=_cparams(("arbitrary", "arbitrary")),
        name="moe_down",
    )(block_e, n_used, hmid, w_down, row_w)


def _combine_kernel(d0_ref, d1_ref, y_hbm, h1_ref, g_ref, b_ref, h2_ref, h2b_ref, buf0, buf1, sem):
    i = pl.program_id(0)
    base = i * LANE

    def issue(r, carry):
        _row_copy(y_hbm, d0_ref[base + r], buf0, r, sem).start()
        _row_copy(y_hbm, d1_ref[base + r], buf1, r, sem).start()
        return carry

    lax.fori_loop(0, LANE, issue, 0)

    def drain(r, carry):
        _row_copy(y_hbm, 0, buf0, r, sem).wait()
        _row_copy(y_hbm, 0, buf1, r, sem).wait()
        return carry

    lax.fori_loop(0, LANE, drain, 0)
    y = _post_ln(h1_ref[...], buf0[...] + buf1[...], g_ref[...], b_ref[...], base)
    h2_ref[...] = y
    h2b_ref[...] = y.astype(BF16)


def _combine_ln2(d0, d1, y_rows, h1, g, b):
    row = lambda i, d0, d1: (i, 0)
    vec = pl.BlockSpec((1, D_MODEL), lambda i, d0, d1: (0, 0))
    grid_spec = pltpu.PrefetchScalarGridSpec(
        num_scalar_prefetch=2,
        grid=(R // LANE,),
        in_specs=[
            pl.BlockSpec(memory_space=pl.ANY),
            pl.BlockSpec((LANE, D_MODEL), row),
            vec, vec,
        ],
        out_specs=[pl.BlockSpec((LANE, D_MODEL), row), pl.BlockSpec((LANE, D_MODEL), row)],
        scratch_shapes=[
            pltpu.VMEM((LANE, D_MODEL), F32),
            pltpu.VMEM((LANE, D_MODEL), F32),
            pltpu.SemaphoreType.DMA,
        ],
    )
    return pl.pallas_call(
        _combine_kernel,
        grid_spec=grid_spec,
        out_shape=[
            jax.ShapeDtypeStruct((R, D_MODEL), F32),
            jax.ShapeDtypeStruct((R, D_MODEL), BF16),
        ],
        compiler_params=_cparams(("arbitrary",)),
        name="moe_combine_ln2",
    )(d0, d1, y_rows, h1, g.reshape(1, -1), b.reshape(1, -1))


def _route(logits, router_b):
    aff = jax.nn.sigmoid(logits)
    sel = (aff + router_b.astype(F32)).reshape(N_TOK, N_GROUPS, EXPERTS_PER_GROUP)
    group_score = jnp.sum(lax.top_k(sel, TOP_K)[0], axis=-1)
    grp = jnp.argmax(group_score, axis=-1)
    tok_ids = jnp.arange(N_TOK)
    _, idx = lax.top_k(sel[tok_ids, grp], TOP_K)
    eid = grp[:, None] * EXPERTS_PER_GROUP + idx
    wts = aff[tok_ids[:, None], eid]
    wts = wts / jnp.sum(wts, axis=-1, keepdims=True)

    flat_e = eid.reshape(-1)
    flat_tok = jnp.repeat(tok_ids, TOP_K)
    flat_w = wts.reshape(-1)
    order = jnp.argsort(flat_e)
    se = flat_e[order]
    counts = jnp.bincount(flat_e, length=N_EXPERTS)
    padded = (counts + EXPERT_BLOCK - 1) // EXPERT_BLOCK * EXPERT_BLOCK
    pad_end = jnp.cumsum(padded)
    pad_start = pad_end - padded
    start = jnp.cumsum(counts) - counts
    dest = (pad_start[se] + jnp.arange(N_FLAT) - start[se]).astype(jnp.int32)
    row_src = jnp.zeros((N_ROWS,), jnp.int32).at[dest].set((flat_tok[order] + T0).astype(jnp.int32))
    row_w = jnp.zeros((N_ROWS,), F32).at[dest].set(flat_w[order])
    block_e = jnp.minimum(
        jnp.searchsorted(pad_end, jnp.arange(N_BLOCKS) * EXPERT_BLOCK, side='right'),
        N_EXPERTS - 1).astype(jnp.int32)
    n_used = (pad_end[-1] // EXPERT_BLOCK).astype(jnp.int32).reshape(1)
    dest_tok = jnp.zeros((N_FLAT,), jnp.int32).at[order].set(dest).reshape(N_TOK, TOP_K)
    d = jnp.zeros((R, TOP_K), jnp.int32).at[T0:].set(dest_tok)
    return row_src, row_w, block_e, n_used, d[:, 0], d[:, 1]


def _mixer_weights(w_in_l):
    w_fox = w_in_l[:, O_FQ:O_FF].astype(BF16)
    small = jnp.zeros((D_MODEL, SM_W), F32)
    small = small.at[:, SM_FF:SM_FF + FOX_HEADS].set(w_in_l[:, O_FF:O_GQ])
    small = small.at[:, SM_GA:SM_GA + GLA_RANK].set(w_in_l[:, O_GA:O_GR])
    w_mix = jnp.concatenate([
        w_in_l[:, O_GQ:O_GA],
        w_in_l[:, O_GR:O_GZ],
        small], axis=1).astype(BF16)
    w_gates = w_in_l[:, O_GZ:].astype(BF16)
    return w_fox, w_mix, w_gates


def kernel(x, meta_tokens, ln_in_g, ln_in_b, w_in, fox_f_bias, gla_wa2, gla_ba, gla_norm_g, conv_w, pool_w, pool_scale, gate_b, w_branch, w_out, ln1_g, ln1_b, router_w, router_b, w_gate, w_up, w_down, ln2_g, ln2_b):
    assert x.shape == (1, SEQ, D_MODEL)
    h, hb = _ln_in(x.reshape(SEQ, D_MODEL), meta_tokens, ln_in_g, ln_in_b)
    router_wp = jnp.zeros((D_MODEL, LANE), F32).at[:, :N_EXPERTS].set(router_w)

    for l in range(DEPTH):
        w_fox, w_mix, w_gates = _mixer_weights(w_in[l])
        zf = _matmul(hb, w_fox, BF16, TM_PROJ, 768, "proj_fox")
        z = _matmul(hb, w_mix, F32, TM_PROJ, 768, "proj_mix")

        bias_row = jnp.zeros((1, LANE), F32).at[0, SM_FF:SM_FF + FOX_HEADS].set(fox_f_bias[l])
        c = _fox_gate(z, bias_row)[:, SM_FF:SM_FF + FOX_HEADS]
        c_t = c.T
        o_a = _fox_attention(zf, c_t.reshape(FOX_HEADS, R, 1), c_t.reshape(FOX_HEADS, 1, R))

        wa2p = jnp.zeros((LANE, GLA_HEADS * GLA_DK), F32).at[SM_GA:SM_GA + GLA_RANK].set(gla_wa2[l])
        o_b = _gla(z, wa2p, gla_ba[l].reshape(1, -1), gla_norm_g[l].reshape(1, -1))

        o_c, o_d = _local_mixers(z, conv_w[l], pool_w[l].astype(BF16), pool_scale[l])

        mix = _merge(hb, o_a, o_b, o_c, o_d, w_gates, gate_b[l], w_branch[l].astype(BF16),
                     w_out[l].astype(BF16))
        h1, h1b, logits = _ln1_router(h, mix, ln1_g[l], ln1_b[l], router_wp)

        row_src, row_w, block_e, n_used, d0, d1 = _route(logits[T0:, :N_EXPERTS], router_b)
        x_rows = _gather_rows(row_src, h1)
        hmid = _moe_up(block_e, n_used, x_rows, w_gate[l], w_up[l])
        y_rows = _moe_down(block_e, n_used, hmid, w_down[l], row_w.reshape(N_ROWS, 1))
        h, hb = _combine_ln2(d0, d1, y_rows, h1, ln2_g[l], ln2_b[l])

    return h[PAD_ROWS + N_META:].reshape(1, SEQ, D_MODEL)
```

```python
import jax
import jax.numpy as jnp
import numpy as np
from jax import lax
from jax.experimental import pallas as pl
from jax.experimental.pallas import tpu as pltpu

F32 = jnp.float32
BF16 = jnp.bfloat16
I32 = jnp.int32
HIGHEST = lax.Precision.HIGHEST

D_MODEL = 2048
SEQ = 8192
DEPTH = 2
N_META = 16
N_BRANCH = 4
MIX_W = 512
FOX_HEADS = 4
FOX_HD = 128
GLA_HEADS = 4
GLA_DK = 64
GLA_DV = 128
GLA_RANK = 16
GLA_TAU = 16.0
CONV_K = 3
POOL_WINDOWS = (2, 4, 8, 16)
POOL_GW = 128
N_EXPERTS = 32
N_GROUPS = 8
EXPERTS_PER_GROUP = 4
TOP_K = 2
D_EXPERT = 1024
LN_EPS = 1e-5
DEEPNORM_ALPHA = (2 * DEPTH) ** 0.25

_SPLITS = (512, 512, 512, 4, 256, 256, 512, 16, 512, 512, 512, 512, 512, 8192)
_OFFS = [int(o) for o in np.concatenate([[0], np.cumsum(_SPLITS)])]
(O_FQ, O_FK, O_FV, O_FF, O_GQ, O_GK, O_GV, O_GA, O_GR, O_CB, O_CC, O_CV, O_PZ, O_GZ, P_IN) = _OFFS

LANE = 128
PAD_ROWS = LANE - N_META
T0 = PAD_ROWS
N_TOK = N_META + SEQ
R = PAD_ROWS + N_TOK
TM = 640
TM_PROJ = 1664
HALO = 16

WT = 512
W_FOX, W_MIX, W_GATES = 0, 1536, 5120
N_FOX, N_MIXC, N_GATES = 1536, 3584, 8192
N_WALL = W_GATES + N_GATES
SHIFT_G = O_GQ - W_FOX - N_FOX
SHIFT_M = O_GR - (W_MIX + 1024)
C_GQ, C_GK, C_GV, C_GR, C_CB, C_CC, C_CV, C_PZ = 0, 256, 512, 1024, 1536, 2048, 2560, 3072
SM_FF_TILE, SM_GA_TILE = O_FF // LANE, O_GA // LANE
SM_FF = O_FF - SM_FF_TILE * LANE
SM_GA = O_GA - SM_GA_TILE * LANE

EXPERT_BLOCK = 128
N_FLAT = N_TOK * TOP_K
N_BLOCKS = -(-N_FLAT // EXPERT_BLOCK) + N_EXPERTS
N_ROWS = N_BLOCKS * EXPERT_BLOCK

NEG = -1e30
VMEM_LIMIT = 48 * 1024 * 1024


def _cparams(sem, vmem=VMEM_LIMIT):
    return pltpu.CompilerParams(dimension_semantics=sem, vmem_limit_bytes=vmem)


def _log_sigmoid(x):
    return jnp.minimum(x, 0.0) - jnp.log1p(jnp.exp(-jnp.abs(x)))


def _sigmoid(x):
    return 1.0 / (1.0 + jnp.exp(-x))


def _layer_norm_rows(x, g, b):
    mu = jnp.mean(x, axis=-1, keepdims=True)
    xc = x - mu
    var = jnp.mean(xc * xc, axis=-1, keepdims=True)
    return xc * lax.rsqrt(var + LN_EPS) * g + b


def _ln_in_kernel(x_ref, meta_ref, g_ref, b_ref, h_ref, hb_ref):
    i = pl.program_id(0)

    @pl.when(i == 0)
    def _():
        h_ref[...] = jnp.zeros_like(h_ref)
        hb_ref[...] = jnp.zeros_like(hb_ref)
        m = _layer_norm_rows(meta_ref[...], g_ref[...], b_ref[...])
        h_ref[PAD_ROWS:, :] = m
        hb_ref[PAD_ROWS:, :] = m.astype(BF16)

    @pl.when(i > 0)
    def _():
        y = _layer_norm_rows(x_ref[...], g_ref[...], b_ref[...])
        h_ref[...] = y
        hb_ref[...] = y.astype(BF16)


def _ln_in(x2d, meta, g, b):
    nb = R // LANE
    return pl.pallas_call(
        _ln_in_kernel,
        grid=(nb,),
        in_specs=[
            pl.BlockSpec((LANE, D_MODEL), lambda i: (jnp.maximum(i - 1, 0), 0)),
            pl.BlockSpec((N_META, D_MODEL), lambda i: (0, 0)),
            pl.BlockSpec((1, D_MODEL), lambda i: (0, 0)),
            pl.BlockSpec((1, D_MODEL), lambda i: (0, 0)),
        ],
        out_specs=[
            pl.BlockSpec((LANE, D_MODEL), lambda i: (i, 0)),
            pl.BlockSpec((LANE, D_MODEL), lambda i: (i, 0)),
        ],
        out_shape=[
            jax.ShapeDtypeStruct((R, D_MODEL), F32),
            jax.ShapeDtypeStruct((R, D_MODEL), BF16),
        ],
        compiler_params=_cparams(("arbitrary",)),
        name="ln_in",
    )(x2d, meta, g.reshape(1, -1), b.reshape(1, -1))


TR_PREP = 1024


def _wprep_kernel(a_ref, b_ref, o_ref):
    j = pl.program_id(1)

    def emit(shift):
        if shift == 0:
            o_ref[...] = a_ref[...].astype(BF16)
        else:
            x = jnp.concatenate([a_ref[...], b_ref[...]], axis=1)
            o_ref[...] = pltpu.roll(x, WT + LANE - shift, axis=1)[:, :WT].astype(BF16)

    first_g = N_FOX // WT
    first_m = (W_MIX + 1024) // WT

    @pl.when(j < first_g)
    def _():
        emit(0)

    @pl.when((j >= first_g) & (j < first_m))
    def _():
        emit(SHIFT_G)

    @pl.when(j >= first_m)
    def _():
        emit(SHIFT_M)


def _prep_w_in(w_in_l):
    return pl.pallas_call(
        _wprep_kernel,
        grid=(D_MODEL // TR_PREP, N_WALL // WT),
        in_specs=[
            pl.BlockSpec((TR_PREP, WT), lambda i, j: (i, j)),
            pl.BlockSpec((TR_PREP, LANE), lambda i, j: (i, (j + 1) * (WT // LANE))),
        ],
        out_specs=pl.BlockSpec((TR_PREP, WT), lambda i, j: (i, j)),
        out_shape=jax.ShapeDtypeStruct((D_MODEL, N_WALL), BF16),
        compiler_params=_cparams(("parallel", "arbitrary")),
        name="prep_w_in",
    )(w_in_l, w_in_l)


def _mm_kernel(a_ref, w_ref, o_ref):
    o_ref[...] = jnp.dot(a_ref[...], w_ref[...], preferred_element_type=F32).astype(o_ref.dtype)


def _matmul(a, w, col0, n, out_dtype, tm, tn, name):
    m, k = a.shape
    return pl.pallas_call(
        _mm_kernel,
        grid=(m // tm, n // tn),
        in_specs=[
            pl.BlockSpec((tm, k), lambda i, j: (i, 0)),
            pl.BlockSpec((k, tn), lambda i, j: (0, col0 // tn + j)),
        ],
        out_specs=pl.BlockSpec((tm, tn), lambda i, j: (i, j)),
        out_shape=jax.ShapeDtypeStruct((m, n), out_dtype),
        compiler_params=_cparams(("parallel", "arbitrary")),
        name=name,
    )(a, w)


def _fox_gate_kernel(zs_ref, bias_ref, c_ref, carry_ref):
    i = pl.program_id(0)

    @pl.when(i == 0)
    def _():
        carry_ref[...] = jnp.zeros_like(carry_ref)

    rows = i * TM + lax.broadcasted_iota(I32, (TM, LANE), 0)
    lf = _log_sigmoid(zs_ref[...] + bias_ref[...])
    lf = jnp.where(rows >= T0, lf, 0.0)
    tri = (lax.broadcasted_iota(I32, (TM, TM), 0)
           >= lax.broadcasted_iota(I32, (TM, TM), 1)).astype(F32)
    c = jnp.dot(tri, lf, precision=HIGHEST, preferred_element_type=F32) + carry_ref[...]
    c_ref[...] = c
    carry_ref[...] = c[TM - 1:TM, :]


def _fox_gate(zs, bias_row):
    return pl.pallas_call(
        _fox_gate_kernel,
        grid=(R // TM,),
        in_specs=[
            pl.BlockSpec((TM, LANE), lambda i: (i, 0)),
            pl.BlockSpec((1, LANE), lambda i: (0, 0)),
        ],
        out_specs=pl.BlockSpec((TM, LANE), lambda i: (i, 0)),
        out_shape=jax.ShapeDtypeStruct((R, LANE), F32),
        scratch_shapes=[pltpu.VMEM((1, LANE), F32)],
        compiler_params=_cparams(("arbitrary",)),
        name="fox_gate",
    )(zs, bias_row)


TQ = TM
N_QB = R // TQ
_PAIRS = [(qi, kj) for qi in range(N_QB) for kj in range(qi + 1)]
N_PAIRS = len(_PAIRS)


def _fox_kernel(qi_tab, kj_tab, q_ref, k_ref, v_ref, cq_ref, ck_ref, o_ref, m_sc, l_sc, acc_sc):
    p = pl.program_id(1)
    qi = qi_tab[p]
    kj = kj_tab[p]

    @pl.when(kj == 0)
    def _():
        m_sc[...] = jnp.full_like(m_sc, NEG)
        l_sc[...] = jnp.zeros_like(l_sc)
        acc_sc[...] = jnp.zeros_like(acc_sc)

    def step(masked):
        s = lax.dot_general(q_ref[...], k_ref[...], (((1,), (1,)), ((), ())),
                            preferred_element_type=F32)
        s = s * (FOX_HD ** -0.5) + cq_ref[...] - ck_ref[...]
        if masked:
            qpos = qi * TQ + lax.broadcasted_iota(I32, (TQ, TQ), 0)
            kpos = kj * TQ + lax.broadcasted_iota(I32, (TQ, TQ), 1)
            ok = (kpos <= qpos) & ((kpos >= T0) | (qpos < T0))
            s = jnp.where(ok, s, NEG)
        m_prev = m_sc[...]
        m_new = jnp.maximum(m_prev, jnp.max(s, axis=-1, keepdims=True))
        alpha = jnp.exp(m_prev - m_new)
        pr = jnp.exp(s - m_new)
        l_sc[...] = alpha * l_sc[...] + jnp.sum(pr, axis=-1, keepdims=True)
        acc_sc[...] = alpha * acc_sc[...] + jnp.dot(pr.astype(BF16), v_ref[...],
                                                    preferred_element_type=F32)
        m_sc[...] = m_new

    needs_mask = (kj == qi) | (kj == 0)

    @pl.when(needs_mask)
    def _():
        step(True)

    @pl.when(jnp.logical_not(needs_mask))
    def _():
        step(False)

    @pl.when(kj == qi)
    def _():
        o_ref[...] = (acc_sc[...] / l_sc[...]).astype(o_ref.dtype)


def _fox_attention(zf, c_col, c_row):
    qi_tab = jnp.asarray([p[0] for p in _PAIRS], I32)
    kj_tab = jnp.asarray([p[1] for p in _PAIRS], I32)
    grid_spec = pltpu.PrefetchScalarGridSpec(
        num_scalar_prefetch=2,
        grid=(FOX_HEADS, N_PAIRS),
        in_specs=[
            pl.BlockSpec((TQ, FOX_HD), lambda h, p, qt, kt: (qt[p], h)),
            pl.BlockSpec((TQ, FOX_HD), lambda h, p, qt, kt: (kt[p], FOX_HEADS + h)),
            pl.BlockSpec((TQ, FOX_HD), lambda h, p, qt, kt: (kt[p], 2 * FOX_HEADS + h)),
            pl.BlockSpec((None, TQ, 1), lambda h, p, qt, kt: (h, qt[p], 0)),
            pl.BlockSpec((None, 1, TQ), lambda h, p, qt, kt: (h, 0, kt[p])),
        ],
        out_specs=pl.BlockSpec((TQ, FOX_HD), lambda h, p, qt, kt: (qt[p], h)),
        scratch_shapes=[
            pltpu.VMEM((TQ, 1), F32),
            pltpu.VMEM((TQ, 1), F32),
            pltpu.VMEM((TQ, FOX_HD), F32),
        ],
    )
    return pl.pallas_call(
        _fox_kernel,
        grid_spec=grid_spec,
        out_shape=jax.ShapeDtypeStruct((R, MIX_W), BF16),
        compiler_params=_cparams(("parallel", "arbitrary")),
        name="fox_attention",
    )(qi_tab, kj_tab, zf, zf, zf, c_col, c_row)


GLA_CHUNK = 64


def _gla_kernel(q_ref, k_ref, v_ref, gr_ref, zs_ref, wa2_ref, ba_ref, gn_ref, o_ref, st_ref, la_ref):
    i = pl.program_id(0)

    @pl.when(i == 0)
    def _():
        st_ref[...] = jnp.zeros_like(st_ref)

    la = jnp.dot(zs_ref[...], wa2_ref[...], precision=HIGHEST, preferred_element_type=F32)
    la_ref[...] = _log_sigmoid(la + ba_ref[...]) * (1.0 / GLA_TAU)

    c_r = lax.broadcasted_iota(I32, (GLA_CHUNK, GLA_CHUNK), 0)
    c_c = lax.broadcasted_iota(I32, (GLA_CHUNK, GLA_CHUNK), 1)
    tri_b = c_r >= c_c
    tri = tri_b.astype(F32)

    def chunk(c, carry):
        r0 = pl.multiple_of(c * GLA_CHUNK, GLA_CHUNK)
        rows = pl.ds(r0, GLA_CHUNK)
        g = la_ref[rows, :]
        b = jnp.dot(tri, g, precision=HIGHEST, preferred_element_type=F32)
        b_last = b[GLA_CHUNK - 1:GLA_CHUNK, :]
        e_last = jnp.exp(b_last)
        qt = q_ref[rows, :] * (GLA_DK ** -0.5) * jnp.exp(b)
        kt = k_ref[rows, :] * jnp.exp(-b)
        kh = kt * e_last
        for h in range(GLA_HEADS):
            ks = slice(h * GLA_DK, (h + 1) * GLA_DK)
            vs = slice(h * GLA_DV, (h + 1) * GLA_DV)
            q_h = qt[:, ks].astype(BF16)
            k_h = kt[:, ks].astype(BF16)
            kh_h = kh[:, ks].astype(BF16)
            v_h = v_ref[rows, vs]
            att = lax.dot_general(q_h, k_h, (((1,), (1,)), ((), ())), preferred_element_type=F32)
            att = jnp.where(tri_b, att, 0.0)
            st = st_ref[h]
            o = jnp.dot(att.astype(BF16), v_h.astype(BF16), preferred_element_type=F32)
            o = o + lax.dot_general(q_h, st.astype(BF16), (((1,), (1,)), ((), ())),
                                    preferred_element_type=F32)
            st_ref[h] = st * e_last[:, ks] + jnp.dot(v_h.T.astype(BF16), kh_h,
                                                     preferred_element_type=F32)
            ms = jnp.mean(o * o, axis=-1, keepdims=True)
            on = o * lax.rsqrt(ms + LN_EPS) * gn_ref[:, vs]
            gate = gr_ref[rows, vs]
            o_ref[rows, vs] = (on * (gate * _sigmoid(gate))).astype(o_ref.dtype)
        return carry

    lax.fori_loop(0, TM // GLA_CHUNK, chunk, 0)


def _gla(z, zs, wa2p, ba, gn):
    return pl.pallas_call(
        _gla_kernel,
        grid=(R // TM,),
        in_specs=[
            pl.BlockSpec((TM, 256), lambda i: (i, C_GQ // 256)),
            pl.BlockSpec((TM, 256), lambda i: (i, C_GK // 256)),
            pl.BlockSpec((TM, 512), lambda i: (i, C_GV // 512)),
            pl.BlockSpec((TM, 512), lambda i: (i, C_GR // 512)),
            pl.BlockSpec((TM, LANE), lambda i: (i, 1)),
            pl.BlockSpec((LANE, 256), lambda i: (0, 0)),
            pl.BlockSpec((1, 256), lambda i: (0, 0)),
            pl.BlockSpec((1, 512), lambda i: (0, 0)),
        ],
        out_specs=pl.BlockSpec((TM, MIX_W), lambda i: (i, 0)),
        out_shape=jax.ShapeDtypeStruct((R, MIX_W), BF16),
        scratch_shapes=[
            pltpu.VMEM((GLA_HEADS, GLA_DV, GLA_DK), F32),
            pltpu.VMEM((TM, GLA_HEADS * GLA_DK), F32),
        ],
        compiler_params=_cparams(("arbitrary",)),
        name="gla",
    )(z, z, z, z, zs, wa2p, ba, gn)


def _local_kernel(cb_ref, cc_ref, cv_ref, pz_ref, cw_ref, pw_ref, ps_ref, oc_ref, od_ref, u_sc, p_sc):
    i = pl.program_id(0)

    @pl.when(i == 0)
    def _():
        u_sc[0:HALO, :] = jnp.zeros((HALO, MIX_W), F32)
        p_sc[0:HALO, :] = jnp.zeros((HALO, MIX_W), F32)

    @pl.when(i > 0)
    def _():
        u_sc[0:HALO, :] = u_sc[TM:TM + HALO, :]
        p_sc[0:HALO, :] = p_sc[TM:TM + HALO, :]

    u = cc_ref[...] * cv_ref[...]
    pz = pz_ref[...]
    u_sc[HALO:, :] = u
    p_sc[HALO:, :] = pz

    y = (cw_ref[2:3, :] * u + cw_ref[1:2, :] * u_sc[HALO - 1:HALO - 1 + TM, :]
         + cw_ref[0:1, :] * u_sc[HALO - 2:HALO - 2 + TM, :])
    oc_ref[...] = (cb_ref[...] * y).astype(oc_ref.dtype)

    tok = i * TM - T0 + lax.broadcasted_iota(I32, (TM, 1), 0)
    cnt_small = jnp.maximum(tok + 1, 1).astype(F32)
    for g, w in enumerate(POOL_WINDOWS):
        cols = slice(g * POOL_GW, (g + 1) * POOL_GW)
        x = pz[:, cols]
        s = x
        for j in range(1, w):
            s = s + p_sc[HALO - j:HALO - j + TM, cols]
        inv_cnt = jnp.where(tok + 1 >= w, 1.0 / w, 1.0 / cnt_small)
        pooled = s * inv_cnt - x
        od = jnp.dot(pooled.astype(BF16), pw_ref[g], preferred_element_type=F32)
        od_ref[:, cols] = (od * ps_ref[:, cols]).astype(od_ref.dtype)


def _local_mixers(z, conv_w, pool_w_bf, pool_scale):
    cw = jnp.zeros((8, MIX_W), F32).at[:CONV_K].set(conv_w)
    blk = lambda c: pl.BlockSpec((TM, MIX_W), lambda i, c=c: (i, c // MIX_W))
    return pl.pallas_call(
        _local_kernel,
        grid=(R // TM,),
        in_specs=[
            blk(C_CB), blk(C_CC), blk(C_CV), blk(C_PZ),
            pl.BlockSpec((8, MIX_W), lambda i: (0, 0)),
            pl.BlockSpec((len(POOL_WINDOWS), POOL_GW, POOL_GW), lambda i: (0, 0, 0)),
            pl.BlockSpec((1, MIX_W), lambda i: (0, 0)),
        ],
        out_specs=[
            pl.BlockSpec((TM, MIX_W), lambda i: (i, 0)),
            pl.BlockSpec((TM, MIX_W), lambda i: (i, 0)),
        ],
        out_shape=[
            jax.ShapeDtypeStruct((R, MIX_W), BF16),
            jax.ShapeDtypeStruct((R, MIX_W), BF16),
        ],
        scratch_shapes=[
            pltpu.VMEM((TM + HALO, MIX_W), F32),
            pltpu.VMEM((TM + HALO, MIX_W), F32),
        ],
        compiler_params=_cparams(("arbitrary",)),
        name="conv_pool",
    )(z, z, z, z, cw, pool_w_bf, pool_scale.reshape(1, -1))


TN_MERGE = 256


def _merge_kernel(hb_ref, oa_ref, ob_ref, oc_ref, od_ref, wg0_ref, wg1_ref, wg2_ref, wg3_ref,
                  gb_ref, wb_ref, wo_ref, out_ref):
    j = pl.program_id(1)

    @pl.when(j == 0)
    def _():
        out_ref[...] = jnp.zeros_like(out_ref)

    hb = hb_ref[...]
    mixed = None
    for b, (o_ref, wg_ref) in enumerate(((oa_ref, wg0_ref), (ob_ref, wg1_ref),
                                         (oc_ref, wg2_ref), (od_ref, wg3_ref))):
        gate = _sigmoid(jnp.dot(hb, wg_ref[...], preferred_element_type=F32) + gb_ref[b:b + 1, :])
        proj = jnp.dot(o_ref[...], wb_ref[b], preferred_element_type=F32)
        term = gate * proj
        mixed = term if mixed is None else mixed + term
    out_ref[...] += jnp.dot(mixed.astype(BF16), wo_ref[...], preferred_element_type=F32)


def _merge(hb, o_a, o_b, o_c, o_d, w_all, gate_b, wb_bf, wo_bf):
    tn = TN_MERGE
    nj = D_MODEL // tn
    row = lambda w: pl.BlockSpec((TM, w), lambda i, j: (i, 0))
    wg = lambda b: pl.BlockSpec((D_MODEL, tn), lambda i, j, b=b: (0, W_GATES // tn + b * nj + j))
    return pl.pallas_call(
        _merge_kernel,
        grid=(R // TM, nj),
        in_specs=[
            row(D_MODEL), row(MIX_W), row(MIX_W), row(MIX_W), row(MIX_W),
            wg(0), wg(1), wg(2), wg(3),
            pl.BlockSpec((N_BRANCH, tn), lambda i, j: (0, j)),
            pl.BlockSpec((N_BRANCH, MIX_W, tn), lambda i, j: (0, 0, j)),
            pl.BlockSpec((tn, D_MODEL), lambda i, j: (j, 0)),
        ],
        out_specs=pl.BlockSpec((TM, D_MODEL), lambda i, j: (i, 0)),
        out_shape=jax.ShapeDtypeStruct((R, D_MODEL), F32),
        compiler_params=_cparams(("parallel", "arbitrary")),
        name="merge",
    )(hb, o_a, o_b, o_c, o_d, w_all, w_all, w_all, w_all, gate_b, wb_bf, wo_bf)


def _post_ln(h, delta, g, b, row0):
    y = _layer_norm_rows(DEEPNORM_ALPHA * h + delta, g, b)
    rows = row0 + lax.broadcasted_iota(I32, (y.shape[0], 1), 0)
    return jnp.where(rows >= T0, y, 0.0)


def _first_of(cands, target):
    idx = jnp.full(target.shape, len(cands) - 1, I32)
    for j in range(len(cands) - 2, -1, -1):
        idx = jnp.where(cands[j] == target, j, idx)
    return idx


def _pick(cands, idx):
    out = cands[-1]
    for j in range(len(cands) - 2, -1, -1):
        out = jnp.where(idx == j, cands[j], out)
    return out


def _ln1_route_kernel(h_ref, mix_ref, g_ref, b_ref, rwt_ref, rb_ref,
                      h1_ref, mi_ref, mf_ref, cnt_ref, carry_sc):
    i = pl.program_id(0)

    @pl.when(i == 0)
    def _():
        carry_sc[...] = jnp.zeros_like(carry_sc)

    y = _post_ln(h_ref[...], mix_ref[...], g_ref[...], b_ref[...], i * TM)
    h1_ref[...] = y

    logits = lax.dot_general(rwt_ref[...], y, (((1,), (1,)), ((), ())), precision=HIGHEST,
                             preferred_element_type=F32)
    aff = _sigmoid(logits)
    sel = aff + rb_ref[...]
    xs = [sel[j * N_GROUPS:(j + 1) * N_GROUPS, :] for j in range(EXPERTS_PER_GROUP)]
    afs = [aff[j * N_GROUPS:(j + 1) * N_GROUPS, :] for j in range(EXPERTS_PER_GROUP)]

    score = None
    for a in range(EXPERTS_PER_GROUP):
        for bb in range(a + 1, EXPERTS_PER_GROUP):
            pair = xs[a] + xs[bb]
            score = pair if score is None else jnp.maximum(score, pair)
    giota = lax.broadcasted_iota(I32, (N_GROUPS, TM), 0)
    gmax = jnp.max(score, axis=0, keepdims=True)
    grp = jnp.min(jnp.where(score == gmax, giota, N_GROUPS), axis=0, keepdims=True)
    gsel = giota == grp
    cs = [jnp.max(jnp.where(gsel, x, -jnp.inf), axis=0, keepdims=True) for x in xs]
    acs = [jnp.sum(jnp.where(gsel, a, 0.0), axis=0, keepdims=True) for a in afs]

    m1 = jnp.maximum(jnp.maximum(cs[0], cs[1]), jnp.maximum(cs[2], cs[3]))
    i0 = _first_of(cs, m1)
    ds = [jnp.where(i0 == j, -jnp.inf, cs[j]) for j in range(EXPERTS_PER_GROUP)]
    m2 = jnp.maximum(jnp.maximum(ds[0], ds[1]), jnp.maximum(ds[2], ds[3]))
    i1 = _first_of(ds, m2)
    a0 = _pick(acs, i0)
    a1 = _pick(acs, i1)
    denom = a0 + a1

    pos = i * TM + lax.broadcasted_iota(I32, (1, TM), 1)
    valid = pos >= T0
    riota = lax.broadcasted_iota(I32, (N_EXPERTS, TM), 0)
    oh0 = (riota == i0 * N_GROUPS + grp) & valid
    oh1 = (riota == i1 * N_GROUPS + grp) & valid
    ohf = jnp.where(oh0 | oh1, 1.0, 0.0)
    before = (lax.broadcasted_iota(I32, (TM, TM), 0)
              < lax.broadcasted_iota(I32, (TM, TM), 1)).astype(BF16)
    cum = jnp.dot(ohf.astype(BF16), before, preferred_element_type=F32) + carry_sc[...]
    rank0 = jnp.sum(jnp.where(oh0, cum, 0.0), axis=0, keepdims=True)
    rank1 = jnp.sum(jnp.where(oh1, cum, 0.0), axis=0, keepdims=True)
    carry = carry_sc[...] + jnp.sum(ohf, axis=1, keepdims=True)
    carry_sc[...] = carry
    cnt_ref[...] = jnp.broadcast_to(carry, cnt_ref.shape)

    zi = jnp.zeros((1, TM), I32)
    mi_ref[...] = jnp.concatenate(
        [grp * EXPERTS_PER_GROUP + i0, grp * EXPERTS_PER_GROUP + i1,
         rank0.astype(I32), rank1.astype(I32), zi, zi, zi, zi], axis=0)
    zf = jnp.zeros((1, TM), F32)
    mf_ref[...] = jnp.concatenate([a0 / denom, a1 / denom, zf, zf, zf, zf, zf, zf], axis=0)


def _ln1_route(h, mix, g, b, router_wt, router_bc):
    row = pl.BlockSpec((TM, D_MODEL), lambda i: (i, 0))
    vec = pl.BlockSpec((1, D_MODEL), lambda i: (0, 0))
    meta = pl.BlockSpec((8, TM), lambda i: (0, i))
    return pl.pallas_call(
        _ln1_route_kernel,
        grid=(R // TM,),
        in_specs=[row, row, vec, vec,
                  pl.BlockSpec((N_EXPERTS, D_MODEL), lambda i: (0, 0)),
                  pl.BlockSpec((N_EXPERTS, 1), lambda i: (0, 0))],
        out_specs=[row, meta, meta, pl.BlockSpec((N_EXPERTS, LANE), lambda i: (0, 0))],
        out_shape=[
            jax.ShapeDtypeStruct((R, D_MODEL), F32),
            jax.ShapeDtypeStruct((8, R), I32),
            jax.ShapeDtypeStruct((8, R), F32),
            jax.ShapeDtypeStruct((N_EXPERTS, LANE), F32),
        ],
        scratch_shapes=[pltpu.VMEM((N_EXPERTS, 1), F32)],
        compiler_params=_cparams(("arbitrary",)),
        name="ln1_route",
    )(h, mix, g.reshape(1, -1), b.reshape(1, -1), router_wt, router_bc)


def _dispatch_tables(mi, counts_slot_major):
    counts = counts_slot_major.reshape(EXPERTS_PER_GROUP, N_GROUPS).T.reshape(N_EXPERTS).astype(I32)
    padded = (counts + EXPERT_BLOCK - 1) // EXPERT_BLOCK * EXPERT_BLOCK
    pad_end = jnp.cumsum(padded)
    pad_start = pad_end - padded
    e_iota = jnp.arange(N_EXPERTS, dtype=I32)
    rows_ok = jnp.arange(R) >= T0

    def dest(eid, rank):
        start = jnp.sum(jnp.where(eid[:, None] == e_iota[None, :], pad_start[None, :], 0), axis=1)
        return jnp.where(rows_ok, start + rank, 0).astype(I32)

    d0 = dest(mi[0], mi[2])
    d1 = dest(mi[1], mi[3])
    blk0 = jnp.arange(N_BLOCKS, dtype=I32) * EXPERT_BLOCK
    block_e = jnp.minimum(jnp.sum((pad_end[None, :] <= blk0[:, None]).astype(I32), axis=1),
                          N_EXPERTS - 1).astype(I32)
    n_used = (pad_end[-1] // EXPERT_BLOCK).astype(I32).reshape(1)
    return d0, d1, block_e, n_used


def _row_copy(src, src_row, dst, dst_row, sem):
    return pltpu.make_async_copy(src.at[pl.ds(src_row, 1), :], dst.at[pl.ds(dst_row, 1), :], sem)


def _dispatch_kernel(d0_ref, d1_ref, h_hbm, xz_hbm, x_hbm, sem):
    del xz_hbm
    i = pl.program_id(0)
    base = i * LANE
    lo = jnp.where(i == 0, T0, 0)

    def issue(r, carry):
        row = base + r
        _row_copy(h_hbm, row, x_hbm, d0_ref[row], sem).start()
        _row_copy(h_hbm, row, x_hbm, d1_ref[row], sem).start()
        return carry

    lax.fori_loop(lo, LANE, issue, 0)

    def drain(r, carry):
        _row_copy(h_hbm, 0, x_hbm, 0, sem).wait()
        _row_copy(h_hbm, 0, x_hbm, 0, sem).wait()
        return carry

    lax.fori_loop(lo, LANE, drain, 0)


def _dispatch_rows(d0, d1, h1):
    grid_spec = pltpu.PrefetchScalarGridSpec(
        num_scalar_prefetch=2,
        grid=(R // LANE,),
        in_specs=[pl.BlockSpec(memory_space=pl.ANY), pl.BlockSpec(memory_space=pl.ANY)],
        out_specs=pl.BlockSpec(memory_space=pl.ANY),
        scratch_shapes=[pltpu.SemaphoreType.DMA],
    )
    return pl.pallas_call(
        _dispatch_kernel,
        grid_spec=grid_spec,
        out_shape=jax.ShapeDtypeStruct((N_ROWS, D_MODEL), F32),
        input_output_aliases={3: 0},
        compiler_params=_cparams(("arbitrary",)),
        name="moe_dispatch",
    )(d0, d1, h1, jnp.zeros((N_ROWS, D_MODEL), F32))


def _expert_changed(be_ref, b):
    return (b == 0) | (be_ref[b] != be_ref[jnp.maximum(b - 1, 0)])


def _moe_up_kernel(be_ref, nu_ref, x_ref, wg_ref, wu_ref, o_ref, wg_sc, wu_sc):
    b = pl.program_id(1)

    @pl.when(_expert_changed(be_ref, b))
    def _():
        wg_sc[...] = wg_ref[...].astype(BF16)
        wu_sc[...] = wu_ref[...].astype(BF16)

    @pl.when(b < nu_ref[0])
    def _():
        x = x_ref[...].astype(BF16)
        g = jnp.dot(x, wg_sc[...], preferred_element_type=F32)
        u = jnp.dot(x, wu_sc[...], preferred_element_type=F32)
        o_ref[...] = (g * _sigmoid(g) * u).astype(o_ref.dtype)

    @pl.when(b >= nu_ref[0])
    def _():
        o_ref[...] = jnp.zeros_like(o_ref)


def _moe_up(block_e, n_used, x_rows, w_gate, w_up):
    th = D_EXPERT // 2
    wspec = pl.BlockSpec((None, D_MODEL, th), lambda hf, b, be, nu: (be[b], 0, hf))
    grid_spec = pltpu.PrefetchScalarGridSpec(
        num_scalar_prefetch=2,
        grid=(2, N_BLOCKS),
        in_specs=[
            pl.BlockSpec((EXPERT_BLOCK, D_MODEL), lambda hf, b, be, nu: (b, 0)),
            wspec, wspec,
        ],
        out_specs=pl.BlockSpec((EXPERT_BLOCK, th), lambda hf, b, be, nu: (b, hf)),
        scratch_shapes=[pltpu.VMEM((D_MODEL, th), BF16), pltpu.VMEM((D_MODEL, th), BF16)],
    )
    return pl.pallas_call(
        _moe_up_kernel,
        grid_spec=grid_spec,
        out_shape=jax.ShapeDtypeStruct((N_ROWS, D_EXPERT), BF16),
        compiler_params=_cparams(("arbitrary", "arbitrary")),
        name="moe_up",
    )(block_e, n_used, x_rows, w_gate, w_up)


def _moe_down_kernel(be_ref, nu_ref, x_ref, wd_ref, o_ref, wd_sc):
    b = pl.program_id(1)

    @pl.when(_expert_changed(be_ref, b))
    def _():
        wd_sc[...] = wd_ref[...].astype(BF16)

    @pl.when(b < nu_ref[0])
    def _():
        o_ref[...] = jnp.dot(x_ref[...], wd_sc[...], preferred_element_type=F32)

    @pl.when(b >= nu_ref[0])
    def _():
        o_ref[...] = jnp.zeros_like(o_ref)


def _moe_down(block_e, n_used, hmid, w_down):
    th = D_MODEL // 2
    grid_spec = pltpu.PrefetchScalarGridSpec(
        num_scalar_prefetch=2,
        grid=(2, N_BLOCKS),
        in_specs=[
            pl.BlockSpec((EXPERT_BLOCK, D_EXPERT), lambda hf, b, be, nu: (b, 0)),
            pl.BlockSpec((None, D_EXPERT, th), lambda hf, b, be, nu: (be[b], 0, hf)),
        ],
        out_specs=pl.BlockSpec((EXPERT_BLOCK, th), lambda hf, b, be, nu: (b, hf)),
        scratch_shapes=[pltpu.VMEM((D_EXPERT, th), BF16)],
    )
    return pl.pallas_call(
        _moe_down_kernel,
        grid_spec=grid_spec,
        out_shape=jax.ShapeDtypeStruct((N_ROWS, D_MODEL), F32),
        compiler_params=_cparams(("arbitrary", "arbitrary")),
        name="moe_down",
    )(block_e, n_used, hmid, w_down)


def _combine_kernel(d0_ref, d1_ref, y_hbm, h1_ref, w0_ref, w1_ref, g_ref, b_ref, h2_ref, h2b_ref,
                    buf0, buf1, sem):
    i = pl.program_id(0)
    base = i * LANE

    def issue(r, carry):
        _row_copy(y_hbm, d0_ref[base + r], buf0, r, sem).start()
        _row_copy(y_hbm, d1_ref[base + r], buf1, r, sem).start()
        return carry

    lax.fori_loop(0, LANE, issue, 0)

    def drain(r, carry):
        _row_copy(y_hbm, 0, buf0, r, sem).wait()
        _row_copy(y_hbm, 0, buf1, r, sem).wait()
        return carry

    lax.fori_loop(0, LANE, drain, 0)
    ffn = w0_ref[...] * buf0[...] + w1_ref[...] * buf1[...]
    y = _post_ln(h1_ref[...], ffn, g_ref[...], b_ref[...], base)
    h2_ref[...] = y
    h2b_ref[...] = y.astype(BF16)


def _combine_ln2(d0, d1, y_rows, h1, w0, w1, g, b):
    row = lambda i, d0, d1: (i, 0)
    vec = pl.BlockSpec((1, D_MODEL), lambda i, d0, d1: (0, 0))
    col = pl.BlockSpec((LANE, 1), row)
    grid_spec = pltpu.PrefetchScalarGridSpec(
        num_scalar_prefetch=2,
        grid=(R // LANE,),
        in_specs=[
            pl.BlockSpec(memory_space=pl.ANY),
            pl.BlockSpec((LANE, D_MODEL), row),
            col, col, vec, vec,
        ],
        out_specs=[pl.BlockSpec((LANE, D_MODEL), row), pl.BlockSpec((LANE, D_MODEL), row)],
        scratch_shapes=[
            pltpu.VMEM((LANE, D_MODEL), F32),
            pltpu.VMEM((LANE, D_MODEL), F32),
            pltpu.SemaphoreType.DMA,
        ],
    )
    return pl.pallas_call(
        _combine_kernel,
        grid_spec=grid_spec,
        out_shape=[
            jax.ShapeDtypeStruct((R, D_MODEL), F32),
            jax.ShapeDtypeStruct((R, D_MODEL), BF16),
        ],
        compiler_params=_cparams(("arbitrary",)),
        name="moe_combine_ln2",
    )(d0, d1, y_rows, h1, w0, w1, g.reshape(1, -1), b.reshape(1, -1))


def kernel(x, meta_tokens, ln_in_g, ln_in_b, w_in, fox_f_bias, gla_wa2, gla_ba, gla_norm_g, conv_w, pool_w, pool_scale, gate_b, w_branch, w_out, ln1_g, ln1_b, router_w, router_b, w_gate, w_up, w_down, ln2_g, ln2_b):
    assert x.shape == (1, SEQ, D_MODEL)
    h, hb = _ln_in(x.reshape(SEQ, D_MODEL), meta_tokens, ln_in_g, ln_in_b)
    router_wt = router_w.T.reshape(N_GROUPS, EXPERTS_PER_GROUP, D_MODEL).transpose(1, 0, 2).reshape(
        N_EXPERTS, D_MODEL)
    router_bc = router_b.astype(F32).reshape(N_GROUPS, EXPERTS_PER_GROUP).T.reshape(N_EXPERTS, 1)

    for l in range(DEPTH):
        w_all = _prep_w_in(w_in[l])
        w_small = jnp.concatenate(
            [w_in[l][:, SM_FF_TILE * LANE:(SM_FF_TILE + 1) * LANE],
             w_in[l][:, SM_GA_TILE * LANE:(SM_GA_TILE + 1) * LANE]], axis=1).astype(BF16)
        zf = _matmul(hb, w_all, W_FOX, N_FOX, BF16, TM_PROJ, 768, "proj_fox")
        z = _matmul(hb, w_all, W_MIX, N_MIXC, F32, TM_PROJ, 512, "proj_mix")
        zs = _matmul(hb, w_small, 0, 2 * LANE, F32, TM_PROJ, 2 * LANE, "proj_small")

        bias_row = jnp.zeros((1, LANE), F32).at[0, SM_FF:SM_FF + FOX_HEADS].set(fox_f_bias[l])
        c = _fox_gate(zs, bias_row)[:, SM_FF:SM_FF + FOX_HEADS]
        c_t = c.T
        o_a = _fox_attention(zf, c_t.reshape(FOX_HEADS, R, 1), c_t.reshape(FOX_HEADS, 1, R))

        wa2p = jnp.zeros((LANE, GLA_HEADS * GLA_DK), F32).at[SM_GA:SM_GA + GLA_RANK].set(gla_wa2[l])
        o_b = _gla(z, zs, wa2p, gla_ba[l].reshape(1, -1), gla_norm_g[l].reshape(1, -1))

        o_c, o_d = _local_mixers(z, conv_w[l], pool_w[l].astype(BF16), pool_scale[l])

        mix = _merge(hb, o_a, o_b, o_c, o_d, w_all, gate_b[l], w_branch[l].astype(BF16),
                     w_out[l].astype(BF16))
        h1, mi, mf, counts = _ln1_route(h, mix, ln1_g[l], ln1_b[l], router_wt, router_bc)

        d0, d1, block_e, n_used = _dispatch_tables(mi, counts[:, 0])
        x_rows = _dispatch_rows(d0, d1, h1)
        hmid = _moe_up(block_e, n_used, x_rows, w_gate[l], w_up[l])
        y_rows = _moe_down(block_e, n_used, hmid, w_down[l])
        h, hb = _combine_ln2(d0, d1, y_rows, h1, mf[0].reshape(R, 1), mf[1].reshape(R, 1),
                             ln2_g[l], ln2_b[l])

    return h[PAD_ROWS + N_META:].reshape(1, SEQ, D_MODEL)
```

```python
import jax
import jax.numpy as jnp
import numpy as np
from jax import lax
from jax.experimental import pallas as pl
from jax.experimental.pallas import tpu as pltpu

F32 = jnp.float32
BF16 = jnp.bfloat16
I32 = jnp.int32
HIGHEST = lax.Precision.HIGHEST

D_MODEL = 2048
SEQ = 8192
DEPTH = 2
N_META = 16
N_BRANCH = 4
MIX_W = 512
FOX_HEADS = 4
FOX_HD = 128
GLA_HEADS = 4
GLA_DK = 64
GLA_DV = 128
GLA_RANK = 16
GLA_TAU = 16.0
CONV_K = 3
POOL_WINDOWS = (2, 4, 8, 16)
POOL_GW = 128
N_EXPERTS = 32
N_GROUPS = 8
EXPERTS_PER_GROUP = 4
TOP_K = 2
D_EXPERT = 1024
LN_EPS = 1e-5
DEEPNORM_ALPHA = (2 * DEPTH) ** 0.25

_SPLITS = (512, 512, 512, 4, 256, 256, 512, 16, 512, 512, 512, 512, 512, 8192)
_OFFS = [int(o) for o in np.concatenate([[0], np.cumsum(_SPLITS)])]
(O_FQ, O_FK, O_FV, O_FF, O_GQ, O_GK, O_GV, O_GA, O_GR, O_CB, O_CC, O_CV, O_PZ, O_GZ, P_IN) = _OFFS

LANE = 128
PAD_ROWS = LANE - N_META
T0 = PAD_ROWS
N_TOK = N_META + SEQ
R = PAD_ROWS + N_TOK
TM = 640
TM_PROJ = 1664
HALO = 16

WT = 512
W_FOX, W_MIX, W_GATES = 0, 1536, 5120
N_FOX, N_MIXC, N_GATES = 1536, 3584, 8192
N_WALL = W_GATES + N_GATES
SHIFT_G = O_GQ - W_FOX - N_FOX
SHIFT_M = O_GR - (W_MIX + 1024)
C_GQ, C_GK, C_GV, C_GR, C_CB, C_CC, C_CV, C_PZ = 0, 256, 512, 1024, 1536, 2048, 2560, 3072
SM_FF_TILE, SM_GA_TILE = O_FF // LANE, O_GA // LANE
SM_FF = O_FF - SM_FF_TILE * LANE
SM_GA = O_GA - SM_GA_TILE * LANE

EXPERT_BLOCK = 128
N_FLAT = N_TOK * TOP_K
N_BLOCKS = -(-N_FLAT // EXPERT_BLOCK) + N_EXPERTS
N_ROWS = N_BLOCKS * EXPERT_BLOCK

NEG = -1e30
VMEM_LIMIT = 48 * 1024 * 1024


def _cparams(sem, vmem=VMEM_LIMIT):
    return pltpu.CompilerParams(dimension_semantics=sem, vmem_limit_bytes=vmem)


def _log_sigmoid(x):
    return jnp.minimum(x, 0.0) - jnp.log1p(jnp.exp(-jnp.abs(x)))


def _sigmoid(x):
    return 1.0 / (1.0 + jnp.exp(-x))


def _layer_norm_rows(x, g, b):
    mu = jnp.mean(x, axis=-1, keepdims=True)
    xc = x - mu
    var = jnp.mean(xc * xc, axis=-1, keepdims=True)
    return xc * lax.rsqrt(var + LN_EPS) * g + b


def _ln_in_kernel(x_ref, meta_ref, g_ref, b_ref, h_ref, hb_ref):
    i = pl.program_id(0)

    @pl.when(i == 0)
    def _():
        h_ref[...] = jnp.zeros_like(h_ref)
        hb_ref[...] = jnp.zeros_like(hb_ref)
        m = _layer_norm_rows(meta_ref[...], g_ref[...], b_ref[...])
        h_ref[PAD_ROWS:, :] = m
        hb_ref[PAD_ROWS:, :] = m.astype(BF16)

    @pl.when(i > 0)
    def _():
        y = _layer_norm_rows(x_ref[...], g_ref[...], b_ref[...])
        h_ref[...] = y
        hb_ref[...] = y.astype(BF16)


def _ln_in(x2d, meta, g, b):
    nb = R // LANE
    return pl.pallas_call(
        _ln_in_kernel,
        grid=(nb,),
        in_specs=[
            pl.BlockSpec((LANE, D_MODEL), lambda i: (jnp.maximum(i - 1, 0), 0)),
            pl.BlockSpec((N_META, D_MODEL), lambda i: (0, 0)),
            pl.BlockSpec((1, D_MODEL), lambda i: (0, 0)),
            pl.BlockSpec((1, D_MODEL), lambda i: (0, 0)),
        ],
        out_specs=[
            pl.BlockSpec((LANE, D_MODEL), lambda i: (i, 0)),
            pl.BlockSpec((LANE, D_MODEL), lambda i: (i, 0)),
        ],
        out_shape=[
            jax.ShapeDtypeStruct((R, D_MODEL), F32),
            jax.ShapeDtypeStruct((R, D_MODEL), BF16),
        ],
        compiler_params=_cparams(("arbitrary",)),
        name="ln_in",
    )(x2d, meta, g.reshape(1, -1), b.reshape(1, -1))


TR_PREP = 1024


def _wprep_kernel(a_ref, b_ref, o_ref):
    j = pl.program_id(1)

    def emit(shift):
        if shift == 0:
            o_ref[...] = a_ref[...].astype(BF16)
        else:
            x = jnp.concatenate([a_ref[...], b_ref[...]], axis=1)
            o_ref[...] = pltpu.roll(x, WT + LANE - shift, axis=1)[:, :WT].astype(BF16)

    first_g = N_FOX // WT
    first_m = (W_MIX + 1024) // WT

    @pl.when(j < first_g)
    def _():
        emit(0)

    @pl.when((j >= first_g) & (j < first_m))
    def _():
        emit(SHIFT_G)

    @pl.when(j >= first_m)
    def _():
        emit(SHIFT_M)


def _prep_w_in(w_in, l):
    return pl.pallas_call(
        _wprep_kernel,
        grid=(D_MODEL // TR_PREP, N_WALL // WT),
        in_specs=[
            pl.BlockSpec((None, TR_PREP, WT), lambda i, j: (l, i, j)),
            pl.BlockSpec((None, TR_PREP, LANE), lambda i, j: (l, i, (j + 1) * (WT // LANE))),
        ],
        out_specs=pl.BlockSpec((TR_PREP, WT), lambda i, j: (i, j)),
        out_shape=jax.ShapeDtypeStruct((D_MODEL, N_WALL), BF16),
        compiler_params=_cparams(("parallel", "arbitrary")),
        name="prep_w_in",
    )(w_in, w_in)


def _mm_kernel(a_ref, w_ref, o_ref):
    o_ref[...] = jnp.dot(a_ref[...], w_ref[...], preferred_element_type=F32).astype(o_ref.dtype)


def _matmul(a, w, col0, n, out_dtype, tm, tn, name):
    m, k = a.shape
    return pl.pallas_call(
        _mm_kernel,
        grid=(m // tm, n // tn),
        in_specs=[
            pl.BlockSpec((tm, k), lambda i, j: (i, 0)),
            pl.BlockSpec((k, tn), lambda i, j: (0, col0 // tn + j)),
        ],
        out_specs=pl.BlockSpec((tm, tn), lambda i, j: (i, j)),
        out_shape=jax.ShapeDtypeStruct((m, n), out_dtype),
        compiler_params=_cparams(("parallel", "arbitrary")),
        name=name,
    )(a, w)


def _fox_gate_kernel(zs_ref, bias_ref, c_ref, carry_ref):
    i = pl.program_id(0)

    @pl.when(i == 0)
    def _():
        carry_ref[...] = jnp.zeros_like(carry_ref)

    rows = i * TM + lax.broadcasted_iota(I32, (TM, LANE), 0)
    lf = _log_sigmoid(zs_ref[...] + bias_ref[...])
    lf = jnp.where(rows >= T0, lf, 0.0)
    tri = (lax.broadcasted_iota(I32, (TM, TM), 0)
           >= lax.broadcasted_iota(I32, (TM, TM), 1)).astype(F32)
    c = jnp.dot(tri, lf, precision=HIGHEST, preferred_element_type=F32) + carry_ref[...]
    c_ref[...] = c
    carry_ref[...] = c[TM - 1:TM, :]


def _fox_gate(zs, bias_row):
    return pl.pallas_call(
        _fox_gate_kernel,
        grid=(R // TM,),
        in_specs=[
            pl.BlockSpec((TM, LANE), lambda i: (i, 0)),
            pl.BlockSpec((1, LANE), lambda i: (0, 0)),
        ],
        out_specs=pl.BlockSpec((TM, LANE), lambda i: (i, 0)),
        out_shape=jax.ShapeDtypeStruct((R, LANE), F32),
        scratch_shapes=[pltpu.VMEM((1, LANE), F32)],
        compiler_params=_cparams(("arbitrary",)),
        name="fox_gate",
    )(zs, bias_row)


TQ = TM
N_QB = R // TQ
_PAIRS = [(qi, kj) for qi in range(N_QB) for kj in range(qi + 1)]
N_PAIRS = len(_PAIRS)


def _fox_kernel(qi_tab, kj_tab, q_ref, k_ref, v_ref, cq_ref, ck_ref, o_ref, m_sc, l_sc, acc_sc):
    p = pl.program_id(1)
    qi = qi_tab[p]
    kj = kj_tab[p]

    @pl.when(kj == 0)
    def _():
        m_sc[...] = jnp.full_like(m_sc, NEG)
        l_sc[...] = jnp.zeros_like(l_sc)
        acc_sc[...] = jnp.zeros_like(acc_sc)

    def step(masked):
        s = lax.dot_general(q_ref[...], k_ref[...], (((1,), (1,)), ((), ())),
                            preferred_element_type=F32)
        s = s * (FOX_HD ** -0.5) + cq_ref[...] - ck_ref[...]
        if masked:
            qpos = qi * TQ + lax.broadcasted_iota(I32, (TQ, TQ), 0)
            kpos = kj * TQ + lax.broadcasted_iota(I32, (TQ, TQ), 1)
            ok = (kpos <= qpos) & ((kpos >= T0) | (qpos < T0))
            s = jnp.where(ok, s, NEG)
        m_prev = m_sc[...]
        m_new = jnp.maximum(m_prev, jnp.max(s, axis=-1, keepdims=True))
        alpha = jnp.exp(m_prev - m_new)
        pr = jnp.exp(s - m_new)
        l_sc[...] = alpha * l_sc[...] + jnp.sum(pr, axis=-1, keepdims=True)
        acc_sc[...] = alpha * acc_sc[...] + jnp.dot(pr.astype(BF16), v_ref[...],
                                                    preferred_element_type=F32)
        m_sc[...] = m_new

    needs_mask = (kj == qi) | (kj == 0)

    @pl.when(needs_mask)
    def _():
        step(True)

    @pl.when(jnp.logical_not(needs_mask))
    def _():
        step(False)

    @pl.when(kj == qi)
    def _():
        o_ref[...] = (acc_sc[...] / l_sc[...]).astype(o_ref.dtype)


def _fox_attention(zf, c_col, c_row):
    qi_tab = jnp.asarray([p[0] for p in _PAIRS], I32)
    kj_tab = jnp.asarray([p[1] for p in _PAIRS], I32)
    grid_spec = pltpu.PrefetchScalarGridSpec(
        num_scalar_prefetch=2,
        grid=(FOX_HEADS, N_PAIRS),
        in_specs=[
            pl.BlockSpec((TQ, FOX_HD), lambda h, p, qt, kt: (qt[p], h)),
            pl.BlockSpec((TQ, FOX_HD), lambda h, p, qt, kt: (kt[p], FOX_HEADS + h)),
            pl.BlockSpec((TQ, FOX_HD), lambda h, p, qt, kt: (kt[p], 2 * FOX_HEADS + h)),
            pl.BlockSpec((None, TQ, 1), lambda h, p, qt, kt: (h, qt[p], 0)),
            pl.BlockSpec((None, 1, TQ), lambda h, p, qt, kt: (h, 0, kt[p])),
        ],
        out_specs=pl.BlockSpec((TQ, FOX_HD), lambda h, p, qt, kt: (qt[p], h)),
        scratch_shapes=[
            pltpu.VMEM((TQ, 1), F32),
            pltpu.VMEM((TQ, 1), F32),
            pltpu.VMEM((TQ, FOX_HD), F32),
        ],
    )
    return pl.pallas_call(
        _fox_kernel,
        grid_spec=grid_spec,
        out_shape=jax.ShapeDtypeStruct((R, MIX_W), BF16),
        compiler_params=_cparams(("parallel", "arbitrary")),
        name="fox_attention",
    )(qi_tab, kj_tab, zf, zf, zf, c_col, c_row)


GLA_CHUNK = 64


def _gla_kernel(q_ref, k_ref, v_ref, gr_ref, zs_ref, wa2_ref, ba_ref, gn_ref, o_ref, st_ref, la_ref):
    i = pl.program_id(0)

    @pl.when(i == 0)
    def _():
        st_ref[...] = jnp.zeros_like(st_ref)

    la = jnp.dot(zs_ref[...], wa2_ref[...], precision=HIGHEST, preferred_element_type=F32)
    la_ref[...] = _log_sigmoid(la + ba_ref[...]) * (1.0 / GLA_TAU)

    c_r = lax.broadcasted_iota(I32, (GLA_CHUNK, GLA_CHUNK), 0)
    c_c = lax.broadcasted_iota(I32, (GLA_CHUNK, GLA_CHUNK), 1)
    tri_b = c_r >= c_c
    tri = tri_b.astype(F32)

    def chunk(c, carry):
        r0 = pl.multiple_of(c * GLA_CHUNK, GLA_CHUNK)
        rows = pl.ds(r0, GLA_CHUNK)
        g = la_ref[rows, :]
        b = jnp.dot(tri, g, precision=HIGHEST, preferred_element_type=F32)
        b_last = b[GLA_CHUNK - 1:GLA_CHUNK, :]
        e_last = jnp.exp(b_last)
        qt = q_ref[rows, :] * (GLA_DK ** -0.5) * jnp.exp(b)
        kt = k_ref[rows, :] * jnp.exp(-b)
        kh = kt * e_last
        for h in range(GLA_HEADS):
            ks = slice(h * GLA_DK, (h + 1) * GLA_DK)
            vs = slice(h * GLA_DV, (h + 1) * GLA_DV)
            q_h = qt[:, ks].astype(BF16)
            k_h = kt[:, ks].astype(BF16)
            kh_h = kh[:, ks].astype(BF16)
            v_h = v_ref[rows, vs]
            att = lax.dot_general(q_h, k_h, (((1,), (1,)), ((), ())), preferred_element_type=F32)
            att = jnp.where(tri_b, att, 0.0)
            st = st_ref[h]
            o = jnp.dot(att.astype(BF16), v_h.astype(BF16), preferred_element_type=F32)
            o = o + lax.dot_general(q_h, st.astype(BF16), (((1,), (1,)), ((), ())),
                                    preferred_element_type=F32)
            st_ref[h] = st * e_last[:, ks] + jnp.dot(v_h.T.astype(BF16), kh_h,
                                                     preferred_element_type=F32)
            ms = jnp.mean(o * o, axis=-1, keepdims=True)
            on = o * lax.rsqrt(ms + LN_EPS) * gn_ref[:, vs]
            gate = gr_ref[rows, vs]
            o_ref[rows, vs] = (on * (gate * _sigmoid(gate))).astype(o_ref.dtype)
        return carry

    lax.fori_loop(0, TM // GLA_CHUNK, chunk, 0)


def _gla(z, zs, wa2p, ba, gn):
    return pl.pallas_call(
        _gla_kernel,
        grid=(R // TM,),
        in_specs=[
            pl.BlockSpec((TM, 256), lambda i: (i, C_GQ // 256)),
            pl.BlockSpec((TM, 256), lambda i: (i, C_GK // 256)),
            pl.BlockSpec((TM, 512), lambda i: (i, C_GV // 512)),
            pl.BlockSpec((TM, 512), lambda i: (i, C_GR // 512)),
            pl.BlockSpec((TM, LANE), lambda i: (i, 1)),
            pl.BlockSpec((LANE, 256), lambda i: (0, 0)),
            pl.BlockSpec((1, 256), lambda i: (0, 0)),
            pl.BlockSpec((1, 512), lambda i: (0, 0)),
        ],
        out_specs=pl.BlockSpec((TM, MIX_W), lambda i: (i, 0)),
        out_shape=jax.ShapeDtypeStruct((R, MIX_W), BF16),
        scratch_shapes=[
            pltpu.VMEM((GLA_HEADS, GLA_DV, GLA_DK), F32),
            pltpu.VMEM((TM, GLA_HEADS * GLA_DK), F32),
        ],
        compiler_params=_cparams(("arbitrary",)),
        name="gla",
    )(z, z, z, z, zs, wa2p, ba, gn)


def _local_kernel(cb_ref, cc_ref, cv_ref, pz_ref, cw_ref, pw_ref, ps_ref, oc_ref, od_ref, u_sc, p_sc):
    i = pl.program_id(0)

    @pl.when(i == 0)
    def _():
        u_sc[0:HALO, :] = jnp.zeros((HALO, MIX_W), F32)
        p_sc[0:HALO, :] = jnp.zeros((HALO, MIX_W), F32)

    @pl.when(i > 0)
    def _():
        u_sc[0:HALO, :] = u_sc[TM:TM + HALO, :]
        p_sc[0:HALO, :] = p_sc[TM:TM + HALO, :]

    u = cc_ref[...] * cv_ref[...]
    pz = pz_ref[...]
    u_sc[HALO:, :] = u
    p_sc[HALO:, :] = pz

    y = (cw_ref[2:3, :] * u + cw_ref[1:2, :] * u_sc[HALO - 1:HALO - 1 + TM, :]
         + cw_ref[0:1, :] * u_sc[HALO - 2:HALO - 2 + TM, :])
    oc_ref[...] = (cb_ref[...] * y).astype(oc_ref.dtype)

    tok = i * TM - T0 + lax.broadcasted_iota(I32, (TM, 1), 0)
    cnt_small = jnp.maximum(tok + 1, 1).astype(F32)
    for g, w in enumerate(POOL_WINDOWS):
        cols = slice(g * POOL_GW, (g + 1) * POOL_GW)
        x = pz[:, cols]
        s = x
        for j in range(1, w):
            s = s + p_sc[HALO - j:HALO - j + TM, cols]
        inv_cnt = jnp.where(tok + 1 >= w, 1.0 / w, 1.0 / cnt_small)
        pooled = s * inv_cnt - x
        od = jnp.dot(pooled.astype(BF16), pw_ref[g], preferred_element_type=F32)
        od_ref[:, cols] = (od * ps_ref[:, cols]).astype(od_ref.dtype)


def _local_mixers(z, conv_w, pool_w_bf, pool_scale):
    cw = jnp.zeros((8, MIX_W), F32).at[:CONV_K].set(conv_w)
    blk = lambda c: pl.BlockSpec((TM, MIX_W), lambda i, c=c: (i, c // MIX_W))
    return pl.pallas_call(
        _local_kernel,
        grid=(R // TM,),
        in_specs=[
            blk(C_CB), blk(C_CC), blk(C_CV), blk(C_PZ),
            pl.BlockSpec((8, MIX_W), lambda i: (0, 0)),
            pl.BlockSpec((len(POOL_WINDOWS), POOL_GW, POOL_GW), lambda i: (0, 0, 0)),
            pl.BlockSpec((1, MIX_W), lambda i: (0, 0)),
        ],
        out_specs=[
            pl.BlockSpec((TM, MIX_W), lambda i: (i, 0)),
            pl.BlockSpec((TM, MIX_W), lambda i: (i, 0)),
        ],
        out_shape=[
            jax.ShapeDtypeStruct((R, MIX_W), BF16),
            jax.ShapeDtypeStruct((R, MIX_W), BF16),
        ],
        scratch_shapes=[
            pltpu.VMEM((TM + HALO, MIX_W), F32),
            pltpu.VMEM((TM + HALO, MIX_W), F32),
        ],
        compiler_params=_cparams(("arbitrary",)),
        name="conv_pool",
    )(z, z, z, z, cw, pool_w_bf, pool_scale.reshape(1, -1))


TN_MERGE = 256


def _merge_kernel(hb_ref, oa_ref, ob_ref, oc_ref, od_ref, wg0_ref, wg1_ref, wg2_ref, wg3_ref,
                  gb_ref, wb_ref, wo_ref, out_ref):
    j = pl.program_id(1)

    @pl.when(j == 0)
    def _():
        out_ref[...] = jnp.zeros_like(out_ref)

    hb = hb_ref[...]
    mixed = None
    for b, (o_ref, wg_ref) in enumerate(((oa_ref, wg0_ref), (ob_ref, wg1_ref),
                                         (oc_ref, wg2_ref), (od_ref, wg3_ref))):
        gate = _sigmoid(jnp.dot(hb, wg_ref[...], preferred_element_type=F32) + gb_ref[b:b + 1, :])
        proj = jnp.dot(o_ref[...], wb_ref[b], preferred_element_type=F32)
        term = gate * proj
        mixed = term if mixed is None else mixed + term
    out_ref[...] += jnp.dot(mixed.astype(BF16), wo_ref[...], preferred_element_type=F32)


def _merge(hb, o_a, o_b, o_c, o_d, w_all, gate_b, wb_bf, wo_bf, l):
    tn = TN_MERGE
    nj = D_MODEL // tn
    row = lambda w: pl.BlockSpec((TM, w), lambda i, j: (i, 0))
    wg = lambda b: pl.BlockSpec((D_MODEL, tn), lambda i, j, b=b: (0, W_GATES // tn + b * nj + j))
    return pl.pallas_call(
        _merge_kernel,
        grid=(R // TM, nj),
        in_specs=[
            row(D_MODEL), row(MIX_W), row(MIX_W), row(MIX_W), row(MIX_W),
            wg(0), wg(1), wg(2), wg(3),
            pl.BlockSpec((N_BRANCH, tn), lambda i, j: (0, j)),
            pl.BlockSpec((None, N_BRANCH, MIX_W, tn), lambda i, j: (l, 0, 0, j)),
            pl.BlockSpec((None, tn, D_MODEL), lambda i, j: (l, j, 0)),
        ],
        out_specs=pl.BlockSpec((TM, D_MODEL), lambda i, j: (i, 0)),
        out_shape=jax.ShapeDtypeStruct((R, D_MODEL), F32),
        compiler_params=_cparams(("parallel", "arbitrary")),
        name="merge",
    )(hb, o_a, o_b, o_c, o_d, w_all, w_all, w_all, w_all, gate_b, wb_bf, wo_bf)


def _post_ln(h, delta, g, b, row0):
    y = _layer_norm_rows(DEEPNORM_ALPHA * h + delta, g, b)
    rows = row0 + lax.broadcasted_iota(I32, (y.shape[0], 1), 0)
    return jnp.where(rows >= T0, y, 0.0)


def _first_of(cands, target):
    idx = jnp.full(target.shape, len(cands) - 1, I32)
    for j in range(len(cands) - 2, -1, -1):
        idx = jnp.where(cands[j] == target, j, idx)
    return idx


def _pick(cands, idx):
    out = cands[-1]
    for j in range(len(cands) - 2, -1, -1):
        out = jnp.where(idx == j, cands[j], out)
    return out


def _ln1_route_kernel(h_ref, mix_ref, g_ref, b_ref, rwt_ref, rb_ref,
                      h1_ref, mi_ref, mf_ref, cnt_ref, carry_sc):
    i = pl.program_id(0)

    @pl.when(i == 0)
    def _():
        carry_sc[...] = jnp.zeros_like(carry_sc)

    y = _post_ln(h_ref[...], mix_ref[...], g_ref[...], b_ref[...], i * TM)
    h1_ref[...] = y

    logits = lax.dot_general(rwt_ref[...], y, (((1,), (1,)), ((), ())), precision=HIGHEST,
                             preferred_element_type=F32)
    aff = _sigmoid(logits)
    sel = aff + rb_ref[...]
    xs = [sel[j * N_GROUPS:(j + 1) * N_GROUPS, :] for j in range(EXPERTS_PER_GROUP)]
    afs = [aff[j * N_GROUPS:(j + 1) * N_GROUPS, :] for j in range(EXPERTS_PER_GROUP)]

    score = None
    for a in range(EXPERTS_PER_GROUP):
        for bb in range(a + 1, EXPERTS_PER_GROUP):
            pair = xs[a] + xs[bb]
            score = pair if score is None else jnp.maximum(score, pair)
    giota = lax.broadcasted_iota(I32, (N_GROUPS, TM), 0)
    gmax = jnp.max(score, axis=0, keepdims=True)
    grp = jnp.min(jnp.where(score == gmax, giota, N_GROUPS), axis=0, keepdims=True)
    gsel = giota == grp
    cs = [jnp.max(jnp.where(gsel, x, -jnp.inf), axis=0, keepdims=True) for x in xs]
    acs = [jnp.sum(jnp.where(gsel, a, 0.0), axis=0, keepdims=True) for a in afs]

    m1 = jnp.maximum(jnp.maximum(cs[0], cs[1]), jnp.maximum(cs[2], cs[3]))
    i0 = _first_of(cs, m1)
    ds = [jnp.where(i0 == j, -jnp.inf, cs[j]) for j in range(EXPERTS_PER_GROUP)]
    m2 = jnp.maximum(jnp.maximum(ds[0], ds[1]), jnp.maximum(ds[2], ds[3]))
    i1 = _first_of(ds, m2)
    a0 = _pick(acs, i0)
    a1 = _pick(acs, i1)
    denom = a0 + a1

    pos = i * TM + lax.broadcasted_iota(I32, (1, TM), 1)
    valid = pos >= T0
    riota = lax.broadcasted_iota(I32, (N_EXPERTS, TM), 0)
    oh0 = (riota == i0 * N_GROUPS + grp) & valid
    oh1 = (riota == i1 * N_GROUPS + grp) & valid
    ohf = jnp.where(oh0 | oh1, 1.0, 0.0)
    before = (lax.broadcasted_iota(I32, (TM, TM), 0)
              < lax.broadcasted_iota(I32, (TM, TM), 1)).astype(BF16)
    cum = jnp.dot(ohf.astype(BF16), before, preferred_element_type=F32) + carry_sc[...]
    rank0 = jnp.sum(jnp.where(oh0, cum, 0.0), axis=0, keepdims=True)
    rank1 = jnp.sum(jnp.where(oh1, cum, 0.0), axis=0, keepdims=True)
    carry = carry_sc[...] + jnp.sum(ohf, axis=1, keepdims=True)
    carry_sc[...] = carry
    cnt_ref[...] = jnp.broadcast_to(carry, cnt_ref.shape)

    zi = jnp.zeros((1, TM), I32)
    mi_ref[...] = jnp.concatenate(
        [grp * EXPERTS_PER_GROUP + i0, grp * EXPERTS_PER_GROUP + i1,
         rank0.astype(I32), rank1.astype(I32), zi, zi, zi, zi], axis=0)
    zf = jnp.zeros((1, TM), F32)
    mf_ref[...] = jnp.concatenate([a0 / denom, a1 / denom, zf, zf, zf, zf, zf, zf], axis=0)


def _ln1_route(h, mix, g, b, router_wt, router_bc):
    row = pl.BlockSpec((TM, D_MODEL), lambda i: (i, 0))
    vec = pl.BlockSpec((1, D_MODEL), lambda i: (0, 0))
    meta = pl.BlockSpec((8, TM), lambda i: (0, i))
    return pl.pallas_call(
        _ln1_route_kernel,
        grid=(R // TM,),
        in_specs=[row, row, vec, vec,
                  pl.BlockSpec((N_EXPERTS, D_MODEL), lambda i: (0, 0)),
                  pl.BlockSpec((N_EXPERTS, 1), lambda i: (0, 0))],
        out_specs=[row, meta, meta, pl.BlockSpec((N_EXPERTS, LANE), lambda i: (0, 0))],
        out_shape=[
            jax.ShapeDtypeStruct((R, D_MODEL), F32),
            jax.ShapeDtypeStruct((8, R), I32),
            jax.ShapeDtypeStruct((8, R), F32),
            jax.ShapeDtypeStruct((N_EXPERTS, LANE), F32),
        ],
        scratch_shapes=[pltpu.VMEM((N_EXPERTS, 1), F32)],
        compiler_params=_cparams(("arbitrary",)),
        name="ln1_route",
    )(h, mix, g.reshape(1, -1), b.reshape(1, -1), router_wt, router_bc)


def _dispatch_tables(mi, counts_slot_major):
    counts = counts_slot_major.reshape(EXPERTS_PER_GROUP, N_GROUPS).T.reshape(N_EXPERTS).astype(I32)
    padded = (counts + EXPERT_BLOCK - 1) // EXPERT_BLOCK * EXPERT_BLOCK
    pad_end = jnp.cumsum(padded)
    pad_start = pad_end - padded
    e_iota = jnp.arange(N_EXPERTS, dtype=I32)
    rows_ok = jnp.arange(R) >= T0

    def dest(eid, rank):
        start = jnp.sum(jnp.where(eid[:, None] == e_iota[None, :], pad_start[None, :], 0), axis=1)
        return jnp.where(rows_ok, start + rank, 0).astype(I32)

    d0 = dest(mi[0], mi[2])
    d1 = dest(mi[1], mi[3])
    blk0 = jnp.arange(N_BLOCKS, dtype=I32) * EXPERT_BLOCK
    block_e = jnp.minimum(jnp.sum((pad_end[None, :] <= blk0[:, None]).astype(I32), axis=1),
                          N_EXPERTS - 1).astype(I32)
    n_used = (pad_end[-1] // EXPERT_BLOCK).astype(I32).reshape(1)
    tok_rows = jnp.arange(T0, R, dtype=I32)
    row_src = jnp.zeros((N_ROWS,), I32).at[jnp.concatenate([d0[T0:], d1[T0:]])].set(
        jnp.concatenate([tok_rows, tok_rows]), unique_indices=True)
    return d0, d1, block_e, n_used, row_src


def _row_copy(src, src_row, dst, dst_row, sem):
    return pltpu.make_async_copy(src.at[pl.ds(src_row, 1), :], dst.at[pl.ds(dst_row, 1), :], sem)


def _gather_kernel(rows_ref, h_hbm, o_ref, buf, sem):
    base = pl.program_id(0) * EXPERT_BLOCK

    def issue(r, carry):
        _row_copy(h_hbm, rows_ref[base + r], buf, r, sem).start()
        return carry

    lax.fori_loop(0, EXPERT_BLOCK, issue, 0)

    def drain(r, carry):
        _row_copy(h_hbm, 0, buf, r, sem).wait()
        return carry

    lax.fori_loop(0, EXPERT_BLOCK, drain, 0)
    o_ref[...] = buf[...].astype(o_ref.dtype)


def _gather_rows(rows, h1):
    grid_spec = pltpu.PrefetchScalarGridSpec(
        num_scalar_prefetch=1,
        grid=(N_BLOCKS,),
        in_specs=[pl.BlockSpec(memory_space=pl.ANY)],
        out_specs=pl.BlockSpec((EXPERT_BLOCK, D_MODEL), lambda i, rows: (i, 0)),
        scratch_shapes=[pltpu.VMEM((EXPERT_BLOCK, D_MODEL), F32), pltpu.SemaphoreType.DMA],
    )
    return pl.pallas_call(
        _gather_kernel,
        grid_spec=grid_spec,
        out_shape=jax.ShapeDtypeStruct((N_ROWS, D_MODEL), BF16),
        compiler_params=_cparams(("arbitrary",)),
        name="moe_gather",
    )(rows, h1)


def _expert_changed(be_ref, b):
    return (b == 0) | (be_ref[b] != be_ref[jnp.maximum(b - 1, 0)])


def _moe_up_kernel(be_ref, nu_ref, x_ref, wg_ref, wu_ref, o_ref, wg_sc, wu_sc):
    b = pl.program_id(1)

    @pl.when(_expert_changed(be_ref, b))
    def _():
        wg_sc[...] = wg_ref[...].astype(BF16)
        wu_sc[...] = wu_ref[...].astype(BF16)

    @pl.when(b < nu_ref[0])
    def _():
        x = x_ref[...]
        g = jnp.dot(x, wg_sc[...], preferred_element_type=F32)
        u = jnp.dot(x, wu_sc[...], preferred_element_type=F32)
        o_ref[...] = (g * _sigmoid(g) * u).astype(o_ref.dtype)

    @pl.when(b >= nu_ref[0])
    def _():
        o_ref[...] = jnp.zeros_like(o_ref)


def _moe_up(block_e, n_used, x_rows, w_gate, w_up, l):
    th = D_EXPERT // 2
    wspec = pl.BlockSpec((None, None, D_MODEL, th), lambda hf, b, be, nu: (l, be[b], 0, hf))
    grid_spec = pltpu.PrefetchScalarGridSpec(
        num_scalar_prefetch=2,
        grid=(2, N_BLOCKS),
        in_specs=[
            pl.BlockSpec((EXPERT_BLOCK, D_MODEL), lambda hf, b, be, nu: (b, 0)),
            wspec, wspec,
        ],
        out_specs=pl.BlockSpec((EXPERT_BLOCK, th), lambda hf, b, be, nu: (b, hf)),
        scratch_shapes=[pltpu.VMEM((D_MODEL, th), BF16), pltpu.VMEM((D_MODEL, th), BF16)],
    )
    return pl.pallas_call(
        _moe_up_kernel,
        grid_spec=grid_spec,
        out_shape=jax.ShapeDtypeStruct((N_ROWS, D_EXPERT), BF16),
        compiler_params=_cparams(("arbitrary", "arbitrary")),
        name="moe_up",
    )(block_e, n_used, x_rows, w_gate, w_up)


def _moe_down_kernel(be_ref, nu_ref, x_ref, wd_ref, o_ref, wd_sc):
    b = pl.program_id(1)

    @pl.when(_expert_changed(be_ref, b))
    def _():
        wd_sc[...] = wd_ref[...].astype(BF16)

    @pl.when(b < nu_ref[0])
    def _():
        o_ref[...] = jnp.dot(x_ref[...], wd_sc[...], preferred_element_type=F32)

    @pl.when(b >= nu_ref[0])
    def _():
        o_ref[...] = jnp.zeros_like(o_ref)


def _moe_down(block_e, n_used, hmid, w_down, l):
    th = D_MODEL // 2
    grid_spec = pltpu.PrefetchScalarGridSpec(
        num_scalar_prefetch=2,
        grid=(2, N_BLOCKS),
        in_specs=[
            pl.BlockSpec((EXPERT_BLOCK, D_EXPERT), lambda hf, b, be, nu: (b, 0)),
            pl.BlockSpec((None, None, D_EXPERT, th), lambda hf, b, be, nu: (l, be[b], 0, hf)),
        ],
        out_specs=pl.BlockSpec((EXPERT_BLOCK, th), lambda hf, b, be, nu: (b, hf)),
        scratch_shapes=[pltpu.VMEM((D_EXPERT, th), BF16)],
    )
    return pl.pallas_call(
        _moe_down_kernel,
        grid_spec=grid_spec,
        out_shape=jax.ShapeDtypeStruct((N_ROWS, D_MODEL), F32),
        compiler_params=_cparams(("arbitrary", "arbitrary")),
        name="moe_down",
    )(block_e, n_used, hmid, w_down)


def _combine_kernel(d0_ref, d1_ref, y_hbm, h1_ref, w0_ref, w1_ref, g_ref, b_ref, h2_ref, h2b_ref,
                    buf0, buf1, sem):
    i = pl.program_id(0)
    base = i * LANE

    def issue(r, carry):
        _row_copy(y_hbm, d0_ref[base + r], buf0, r, sem).start()
        _row_copy(y_hbm, d1_ref[base + r], buf1, r, sem).start()
        return carry

    lax.fori_loop(0, LANE, issue, 0)

    def drain(r, carry):
        _row_copy(y_hbm, 0, buf0, r, sem).wait()
        _row_copy(y_hbm, 0, buf1, r, sem).wait()
        return carry

    lax.fori_loop(0, LANE, drain, 0)
    ffn = w0_ref[...] * buf0[...] + w1_ref[...] * buf1[...]
    y = _post_ln(h1_ref[...], ffn, g_ref[...], b_ref[...], base)
    h2_ref[...] = y
    h2b_ref[...] = y.astype(BF16)


def _combine_ln2(d0, d1, y_rows, h1, w0, w1, g, b):
    row = lambda i, d0, d1: (i, 0)
    vec = pl.BlockSpec((1, D_MODEL), lambda i, d0, d1: (0, 0))
    col = pl.BlockSpec((LANE, 1), row)
    grid_spec = pltpu.PrefetchScalarGridSpec(
        num_scalar_prefetch=2,
        grid=(R // LANE,),
        in_specs=[
            pl.BlockSpec(memory_space=pl.ANY),
            pl.BlockSpec((LANE, D_MODEL), row),
            col, col, vec, vec,
        ],
        out_specs=[pl.BlockSpec((LANE, D_MODEL), row), pl.BlockSpec((LANE, D_MODEL), row)],
        scratch_shapes=[
            pltpu.VMEM((LANE, D_MODEL), F32),
            pltpu.VMEM((LANE, D_MODEL), F32),
            pltpu.SemaphoreType.DMA,
        ],
    )
    return pl.pallas_call(
        _combine_kernel,
        grid_spec=grid_spec,
        out_shape=[
            jax.ShapeDtypeStruct((R, D_MODEL), F32),
            jax.ShapeDtypeStruct((R, D_MODEL), BF16),
        ],
        compiler_params=_cparams(("arbitrary",)),
        name="moe_combine_ln2",
    )(d0, d1, y_rows, h1, w0, w1, g.reshape(1, -1), b.reshape(1, -1))


def kernel(x, meta_tokens, ln_in_g, ln_in_b, w_in, fox_f_bias, gla_wa2, gla_ba, gla_norm_g, conv_w, pool_w, pool_scale, gate_b, w_branch, w_out, ln1_g, ln1_b, router_w, router_b, w_gate, w_up, w_down, ln2_g, ln2_b):
    assert x.shape == (1, SEQ, D_MODEL)
    h, hb = _ln_in(x.reshape(SEQ, D_MODEL), meta_tokens, ln_in_g, ln_in_b)
    router_wt = router_w.T.reshape(N_GROUPS, EXPERTS_PER_GROUP, D_MODEL).transpose(1, 0, 2).reshape(
        N_EXPERTS, D_MODEL)
    router_bc = router_b.astype(F32).reshape(N_GROUPS, EXPERTS_PER_GROUP).T.reshape(N_EXPERTS, 1)

    wb_bf = w_branch.astype(BF16)
    wo_bf = w_out.astype(BF16)

    for l in range(DEPTH):
        w_all = _prep_w_in(w_in, l)
        w_small = jnp.concatenate(
            [w_in[l, :, SM_FF_TILE * LANE:(SM_FF_TILE + 1) * LANE],
             w_in[l, :, SM_GA_TILE * LANE:(SM_GA_TILE + 1) * LANE]], axis=1).astype(BF16)
        zf = _matmul(hb, w_all, W_FOX, N_FOX, BF16, TM_PROJ, 768, "proj_fox")
        z = _matmul(hb, w_all, W_MIX, N_MIXC, F32, TM_PROJ, 512, "proj_mix")
        zs = _matmul(hb, w_small, 0, 2 * LANE, F32, TM_PROJ, 2 * LANE, "proj_small")

        bias_row = jnp.zeros((1, LANE), F32).at[0, SM_FF:SM_FF + FOX_HEADS].set(fox_f_bias[l])
        c = _fox_gate(zs, bias_row)[:, SM_FF:SM_FF + FOX_HEADS]
        c_t = c.T
        o_a = _fox_attention(zf, c_t.reshape(FOX_HEADS, R, 1), c_t.reshape(FOX_HEADS, 1, R))

        wa2p = jnp.zeros((LANE, GLA_HEADS * GLA_DK), F32).at[SM_GA:SM_GA + GLA_RANK].set(gla_wa2[l])
        o_b = _gla(z, zs, wa2p, gla_ba[l].reshape(1, -1), gla_norm_g[l].reshape(1, -1))

        o_c, o_d = _local_mixers(z, conv_w[l], pool_w[l].astype(BF16), pool_scale[l])

        mix = _merge(hb, o_a, o_b, o_c, o_d, w_all, gate_b[l], wb_bf, wo_bf, l)
        h1, mi, mf, counts = _ln1_route(h, mix, ln1_g[l], ln1_b[l], router_wt, router_bc)

        d0, d1, block_e, n_used, row_src = _dispatch_tables(mi, counts[:, 0])
        x_rows = _gather_rows(row_src, h1)
        hmid = _moe_up(block_e, n_used, x_rows, w_gate, w_up, l)
        y_rows = _moe_down(block_e, n_used, hmid, w_down, l)
        h, hb = _combine_ln2(d0, d1, y_rows, h1, mf[0].reshape(R, 1), mf[1].reshape(R, 1),
                             ln2_g[l], ln2_b[l])

    return h[PAD_ROWS + N_META:].reshape(1, SEQ, D_MODEL)
```

```python
import jax
import jax.numpy as jnp
import numpy as np
from jax import lax
from jax.experimental import pallas as pl
from jax.experimental.pallas import tpu as pltpu

F32 = jnp.float32
BF16 = jnp.bfloat16
I32 = jnp.int32
HIGHEST = lax.Precision.HIGHEST

D_MODEL = 2048
SEQ = 8192
DEPTH = 2
N_META = 16
N_BRANCH = 4
MIX_W = 512
FOX_HEADS = 4
FOX_HD = 128
GLA_HEADS = 4
GLA_DK = 64
GLA_DV = 128
GLA_RANK = 16
GLA_TAU = 16.0
CONV_K = 3
POOL_WINDOWS = (2, 4, 8, 16)
POOL_GW = 128
N_EXPERTS = 32
N_GROUPS = 8
EXPERTS_PER_GROUP = 4
TOP_K = 2
D_EXPERT = 1024
LN_EPS = 1e-5
DEEPNORM_ALPHA = (2 * DEPTH) ** 0.25

_SPLITS = (512, 512, 512, 4, 256, 256, 512, 16, 512, 512, 512, 512, 512, 8192)
_OFFS = [int(o) for o in np.concatenate([[0], np.cumsum(_SPLITS)])]
(O_FQ, O_FK, O_FV, O_FF, O_GQ, O_GK, O_GV, O_GA, O_GR, O_CB, O_CC, O_CV, O_PZ, O_GZ, P_IN) = _OFFS

LANE = 128
PAD_ROWS = LANE - N_META
T0 = PAD_ROWS
N_TOK = N_META + SEQ
R = PAD_ROWS + N_TOK
TM = 640
TM_PROJ = 1664
HALO = 16

WT = 512
W_FOX, W_MIX, W_GATES = 0, 1536, 5120
N_FOX, N_MIXC, N_GATES = 1536, 3584, 8192
N_WALL = W_GATES + N_GATES
SHIFT_G = O_GQ - W_FOX - N_FOX
SHIFT_M = O_GR - (W_MIX + 1024)
C_GQ, C_GK, C_GV, C_GR, C_CB, C_CC, C_CV, C_PZ = 0, 256, 512, 1024, 1536, 2048, 2560, 3072
SM_FF_TILE, SM_GA_TILE = O_FF // LANE, O_GA // LANE
SM_FF = O_FF - SM_FF_TILE * LANE
SM_GA = O_GA - SM_GA_TILE * LANE

EXPERT_BLOCK = 128
N_FLAT = N_TOK * TOP_K
N_BLOCKS = -(-N_FLAT // EXPERT_BLOCK) + N_EXPERTS
N_ROWS = N_BLOCKS * EXPERT_BLOCK

NEG = -1e30
VMEM_LIMIT = 48 * 1024 * 1024
MOE_VMEM_LIMIT = 56 * 1024 * 1024


def _cparams(sem, vmem=VMEM_LIMIT):
    return pltpu.CompilerParams(dimension_semantics=sem, vmem_limit_bytes=vmem)


def _log_sigmoid(x):
    return jnp.minimum(x, 0.0) - jnp.log1p(jnp.exp(-jnp.abs(x)))


def _sigmoid(x):
    return 1.0 / (1.0 + jnp.exp(-x))


def _layer_norm_rows(x, g, b):
    mu = jnp.mean(x, axis=-1, keepdims=True)
    xc = x - mu
    var = jnp.mean(xc * xc, axis=-1, keepdims=True)
    return xc * lax.rsqrt(var + LN_EPS) * g + b


def _ln_in_kernel(x_ref, meta_ref, g_ref, b_ref, h_ref, hb_ref):
    i = pl.program_id(0)

    @pl.when(i == 0)
    def _():
        h_ref[...] = jnp.zeros_like(h_ref)
        hb_ref[...] = jnp.zeros_like(hb_ref)
        m = _layer_norm_rows(meta_ref[...], g_ref[...], b_ref[...])
        h_ref[PAD_ROWS:, :] = m
        hb_ref[PAD_ROWS:, :] = m.astype(BF16)

    @pl.when(i > 0)
    def _():
        y = _layer_norm_rows(x_ref[...], g_ref[...], b_ref[...])
        h_ref[...] = y
        hb_ref[...] = y.astype(BF16)


def _ln_in(x2d, meta, g, b):
    nb = R // LANE
    return pl.pallas_call(
        _ln_in_kernel,
        grid=(nb,),
        in_specs=[
            pl.BlockSpec((LANE, D_MODEL), lambda i: (jnp.maximum(i - 1, 0), 0)),
            pl.BlockSpec((N_META, D_MODEL), lambda i: (0, 0)),
            pl.BlockSpec((1, D_MODEL), lambda i: (0, 0)),
            pl.BlockSpec((1, D_MODEL), lambda i: (0, 0)),
        ],
        out_specs=[
            pl.BlockSpec((LANE, D_MODEL), lambda i: (i, 0)),
            pl.BlockSpec((LANE, D_MODEL), lambda i: (i, 0)),
        ],
        out_shape=[
            jax.ShapeDtypeStruct((R, D_MODEL), F32),
            jax.ShapeDtypeStruct((R, D_MODEL), BF16),
        ],
        compiler_params=_cparams(("arbitrary",)),
        name="ln_in",
    )(x2d, meta, g.reshape(1, -1), b.reshape(1, -1))


TR_PREP = 1024


def _wprep_kernel(a_ref, b_ref, o_ref):
    j = pl.program_id(1)

    def emit(shift):
        if shift == 0:
            o_ref[...] = a_ref[...].astype(BF16)
        else:
            x = jnp.concatenate([a_ref[...], b_ref[...]], axis=1)
            o_ref[...] = pltpu.roll(x, WT + LANE - shift, axis=1)[:, :WT].astype(BF16)

    first_g = N_FOX // WT
    first_m = (W_MIX + 1024) // WT

    @pl.when(j < first_g)
    def _():
        emit(0)

    @pl.when((j >= first_g) & (j < first_m))
    def _():
        emit(SHIFT_G)

    @pl.when(j >= first_m)
    def _():
        emit(SHIFT_M)


def _prep_w_in(w_in, l):
    return pl.pallas_call(
        _wprep_kernel,
        grid=(D_MODEL // TR_PREP, N_WALL // WT),
        in_specs=[
            pl.BlockSpec((None, TR_PREP, WT), lambda i, j: (l, i, j)),
            pl.BlockSpec((None, TR_PREP, LANE), lambda i, j: (l, i, (j + 1) * (WT // LANE))),
        ],
        out_specs=pl.BlockSpec((TR_PREP, WT), lambda i, j: (i, j)),
        out_shape=jax.ShapeDtypeStruct((D_MODEL, N_WALL), BF16),
        compiler_params=_cparams(("parallel", "arbitrary")),
        name="prep_w_in",
    )(w_in, w_in)


def _mm_kernel(a_ref, w_ref, o_ref):
    o_ref[...] = jnp.dot(a_ref[...], w_ref[...], preferred_element_type=F32).astype(o_ref.dtype)


def _matmul(a, w, col0, n, out_dtype, tm, tn, name):
    m, k = a.shape
    return pl.pallas_call(
        _mm_kernel,
        grid=(m // tm, n // tn),
        in_specs=[
            pl.BlockSpec((tm, k), lambda i, j: (i, 0)),
            pl.BlockSpec((k, tn), lambda i, j: (0, col0 // tn + j)),
        ],
        out_specs=pl.BlockSpec((tm, tn), lambda i, j: (i, j)),
        out_shape=jax.ShapeDtypeStruct((m, n), out_dtype),
        compiler_params=_cparams(("parallel", "arbitrary")),
        name=name,
    )(a, w)


def _proj_small_kernel(a_ref, w_ref, o_ref):
    o_ref[...] = jnp.dot(a_ref[...], w_ref[...].astype(BF16), preferred_element_type=F32)


def _proj_small(hb, w_in, l):
    return pl.pallas_call(
        _proj_small_kernel,
        grid=(R // TM_PROJ, 2),
        in_specs=[
            pl.BlockSpec((TM_PROJ, D_MODEL), lambda i, j: (i, 0)),
            pl.BlockSpec((None, D_MODEL, LANE),
                         lambda i, j: (l, 0, SM_FF_TILE + j * (SM_GA_TILE - SM_FF_TILE))),
        ],
        out_specs=pl.BlockSpec((TM_PROJ, LANE), lambda i, j: (i, j)),
        out_shape=jax.ShapeDtypeStruct((R, 2 * LANE), F32),
        compiler_params=_cparams(("parallel", "arbitrary")),
        name="proj_small",
    )(hb, w_in)


def _fox_gate_kernel(zs_ref, bias_ref, c_ref, carry_ref):
    i = pl.program_id(0)

    @pl.when(i == 0)
    def _():
        carry_ref[...] = jnp.zeros_like(carry_ref)

    rows = i * TM + lax.broadcasted_iota(I32, (TM, LANE), 0)
    lf = _log_sigmoid(zs_ref[...] + bias_ref[...])
    lf = jnp.where(rows >= T0, lf, 0.0)
    tri = (lax.broadcasted_iota(I32, (TM, TM), 0)
           >= lax.broadcasted_iota(I32, (TM, TM), 1)).astype(F32)
    c = jnp.dot(tri, lf, precision=HIGHEST, preferred_element_type=F32) + carry_ref[...]
    c_ref[...] = c
    carry_ref[...] = c[TM - 1:TM, :]


def _fox_gate(zs, bias_row):
    return pl.pallas_call(
        _fox_gate_kernel,
        grid=(R // TM,),
        in_specs=[
            pl.BlockSpec((TM, LANE), lambda i: (i, 0)),
            pl.BlockSpec((1, LANE), lambda i: (0, 0)),
        ],
        out_specs=pl.BlockSpec((TM, LANE), lambda i: (i, 0)),
        out_shape=jax.ShapeDtypeStruct((R, LANE), F32),
        scratch_shapes=[pltpu.VMEM((1, LANE), F32)],
        compiler_params=_cparams(("arbitrary",)),
        name="fox_gate",
    )(zs, bias_row)


TQ = TM
N_QB = R // TQ
_PAIRS = [(qi, kj) for qi in range(N_QB) for kj in range(qi + 1)]
N_PAIRS = len(_PAIRS)


def _fox_kernel(qi_tab, kj_tab, q_ref, k_ref, v_ref, cq_ref, ck_ref, o_ref, m_sc, l_sc, acc_sc):
    p = pl.program_id(1)
    qi = qi_tab[p]
    kj = kj_tab[p]

    @pl.when(kj == 0)
    def _():
        m_sc[...] = jnp.full_like(m_sc, NEG)
        l_sc[...] = jnp.zeros_like(l_sc)
        acc_sc[...] = jnp.zeros_like(acc_sc)

    def step(masked):
        s = lax.dot_general(q_ref[...], k_ref[...], (((1,), (1,)), ((), ())),
                            preferred_element_type=F32)
        s = s * (FOX_HD ** -0.5) + cq_ref[...] - ck_ref[...]
        if masked:
            qpos = qi * TQ + lax.broadcasted_iota(I32, (TQ, TQ), 0)
            kpos = kj * TQ + lax.broadcasted_iota(I32, (TQ, TQ), 1)
            ok = (kpos <= qpos) & ((kpos >= T0) | (qpos < T0))
            s = jnp.where(ok, s, NEG)
        m_prev = m_sc[...]
        m_new = jnp.maximum(m_prev, jnp.max(s, axis=-1, keepdims=True))
        alpha = jnp.exp(m_prev - m_new)
        pr = jnp.exp(s - m_new)
        l_sc[...] = alpha * l_sc[...] + jnp.sum(pr, axis=-1, keepdims=True)
        acc_sc[...] = alpha * acc_sc[...] + jnp.dot(pr.astype(BF16), v_ref[...],
                                                    preferred_element_type=F32)
        m_sc[...] = m_new

    needs_mask = (kj == qi) | (kj == 0)

    @pl.when(needs_mask)
    def _():
        step(True)

    @pl.when(jnp.logical_not(needs_mask))
    def _():
        step(False)

    @pl.when(kj == qi)
    def _():
        o_ref[...] = (acc_sc[...] / l_sc[...]).astype(o_ref.dtype)


def _fox_attention(zf, c_col, c_row):
    qi_tab = jnp.asarray([p[0] for p in _PAIRS], I32)
    kj_tab = jnp.asarray([p[1] for p in _PAIRS], I32)
    grid_spec = pltpu.PrefetchScalarGridSpec(
        num_scalar_prefetch=2,
        grid=(FOX_HEADS, N_PAIRS),
        in_specs=[
            pl.BlockSpec((TQ, FOX_HD), lambda h, p, qt, kt: (qt[p], h)),
            pl.BlockSpec((TQ, FOX_HD), lambda h, p, qt, kt: (kt[p], FOX_HEADS + h)),
            pl.BlockSpec((TQ, FOX_HD), lambda h, p, qt, kt: (kt[p], 2 * FOX_HEADS + h)),
            pl.BlockSpec((None, TQ, 1), lambda h, p, qt, kt: (h, qt[p], 0)),
            pl.BlockSpec((None, 1, TQ), lambda h, p, qt, kt: (h, 0, kt[p])),
        ],
        out_specs=pl.BlockSpec((TQ, FOX_HD), lambda h, p, qt, kt: (qt[p], h)),
        scratch_shapes=[
            pltpu.VMEM((TQ, 1), F32),
            pltpu.VMEM((TQ, 1), F32),
            pltpu.VMEM((TQ, FOX_HD), F32),
        ],
    )
    return pl.pallas_call(
        _fox_kernel,
        grid_spec=grid_spec,
        out_shape=jax.ShapeDtypeStruct((R, MIX_W), BF16),
        compiler_params=_cparams(("parallel", "arbitrary")),
        name="fox_attention",
    )(qi_tab, kj_tab, zf, zf, zf, c_col, c_row)


GLA_CHUNK = 64


def _gla_kernel(q_ref, k_ref, v_ref, gr_ref, zs_ref, wa2_ref, ba_ref, gn_ref, o_ref, st_ref, la_ref):
    i = pl.program_id(0)

    @pl.when(i == 0)
    def _():
        st_ref[...] = jnp.zeros_like(st_ref)

    la = jnp.dot(zs_ref[...], wa2_ref[...], precision=HIGHEST, preferred_element_type=F32)
    la_ref[...] = _log_sigmoid(la + ba_ref[...]) * (1.0 / GLA_TAU)

    c_r = lax.broadcasted_iota(I32, (GLA_CHUNK, GLA_CHUNK), 0)
    c_c = lax.broadcasted_iota(I32, (GLA_CHUNK, GLA_CHUNK), 1)
    tri_b = c_r >= c_c
    tri = tri_b.astype(F32)

    def chunk(c, carry):
        r0 = pl.multiple_of(c * GLA_CHUNK, GLA_CHUNK)
        rows = pl.ds(r0, GLA_CHUNK)
        g = la_ref[rows, :]
        b = jnp.dot(tri, g, precision=HIGHEST, preferred_element_type=F32)
        b_last = b[GLA_CHUNK - 1:GLA_CHUNK, :]
        e_last = jnp.exp(b_last)
        qt = q_ref[rows, :] * (GLA_DK ** -0.5) * jnp.exp(b)
        kt = k_ref[rows, :] * jnp.exp(-b)
        kh = kt * e_last
        for h in range(GLA_HEADS):
            ks = slice(h * GLA_DK, (h + 1) * GLA_DK)
            vs = slice(h * GLA_DV, (h + 1) * GLA_DV)
            q_h = qt[:, ks].astype(BF16)
            k_h = kt[:, ks].astype(BF16)
            kh_h = kh[:, ks].astype(BF16)
            v_h = v_ref[rows, vs]
            att = lax.dot_general(q_h, k_h, (((1,), (1,)), ((), ())), preferred_element_type=F32)
            att = jnp.where(tri_b, att, 0.0)
            st = st_ref[h]
            o = jnp.dot(att.astype(BF16), v_h.astype(BF16), preferred_element_type=F32)
            o = o + lax.dot_general(q_h, st.astype(BF16), (((1,), (1,)), ((), ())),
                                    preferred_element_type=F32)
            st_ref[h] = st * e_last[:, ks] + jnp.dot(v_h.T.astype(BF16), kh_h,
                                                     preferred_element_type=F32)
            ms = jnp.mean(o * o, axis=-1, keepdims=True)
            on = o * lax.rsqrt(ms + LN_EPS) * gn_ref[:, vs]
            gate = gr_ref[rows, vs]
            o_ref[rows, vs] = (on * (gate * _sigmoid(gate))).astype(o_ref.dtype)
        return carry

    lax.fori_loop(0, TM // GLA_CHUNK, chunk, 0)


def _gla(z, zs, wa2p, ba, gn):
    return pl.pallas_call(
        _gla_kernel,
        grid=(R // TM,),
        in_specs=[
            pl.BlockSpec((TM, 256), lambda i: (i, C_GQ // 256)),
            pl.BlockSpec((TM, 256), lambda i: (i, C_GK // 256)),
            pl.BlockSpec((TM, 512), lambda i: (i, C_GV // 512)),
            pl.BlockSpec((TM, 512), lambda i: (i, C_GR // 512)),
            pl.BlockSpec((TM, LANE), lambda i: (i, 1)),
            pl.BlockSpec((LANE, 256), lambda i: (0, 0)),
            pl.BlockSpec((1, 256), lambda i: (0, 0)),
            pl.BlockSpec((1, 512), lambda i: (0, 0)),
        ],
        out_specs=pl.BlockSpec((TM, MIX_W), lambda i: (i, 0)),
        out_shape=jax.ShapeDtypeStruct((R, MIX_W), BF16),
        scratch_shapes=[
            pltpu.VMEM((GLA_HEADS, GLA_DV, GLA_DK), F32),
            pltpu.VMEM((TM, GLA_HEADS * GLA_DK), F32),
        ],
        compiler_params=_cparams(("arbitrary",)),
        name="gla",
    )(z, z, z, z, zs, wa2p, ba, gn)


def _local_kernel(cb_ref, cc_ref, cv_ref, pz_ref, cw_ref, pw_ref, ps_ref, oc_ref, od_ref, u_sc, p_sc):
    i = pl.program_id(0)

    @pl.when(i == 0)
    def _():
        u_sc[0:HALO, :] = jnp.zeros((HALO, MIX_W), F32)
        p_sc[0:HALO, :] = jnp.zeros((HALO, MIX_W), F32)

    @pl.when(i > 0)
    def _():
        u_sc[0:HALO, :] = u_sc[TM:TM + HALO, :]
        p_sc[0:HALO, :] = p_sc[TM:TM + HALO, :]

    u = cc_ref[...] * cv_ref[...]
    pz = pz_ref[...]
    u_sc[HALO:, :] = u
    p_sc[HALO:, :] = pz

    y = (cw_ref[2:3, :] * u + cw_ref[1:2, :] * u_sc[HALO - 1:HALO - 1 + TM, :]
         + cw_ref[0:1, :] * u_sc[HALO - 2:HALO - 2 + TM, :])
    oc_ref[...] = (cb_ref[...] * y).astype(oc_ref.dtype)

    tok = i * TM - T0 + lax.broadcasted_iota(I32, (TM, 1), 0)
    cnt_small = jnp.maximum(tok + 1, 1).astype(F32)
    for g, w in enumerate(POOL_WINDOWS):
        cols = slice(g * POOL_GW, (g + 1) * POOL_GW)
        x = pz[:, cols]
        s = x
        for j in range(1, w):
            s = s + p_sc[HALO - j:HALO - j + TM, cols]
        inv_cnt = jnp.where(tok + 1 >= w, 1.0 / w, 1.0 / cnt_small)
        pooled = s * inv_cnt - x
        od = jnp.dot(pooled.astype(BF16), pw_ref[g], preferred_element_type=F32)
        od_ref[:, cols] = (od * ps_ref[:, cols]).astype(od_ref.dtype)


def _local_mixers(z, conv_w, pool_w_bf, pool_scale):
    cw = jnp.zeros((8, MIX_W), F32).at[:CONV_K].set(conv_w)
    blk = lambda c: pl.BlockSpec((TM, MIX_W), lambda i, c=c: (i, c // MIX_W))
    return pl.pallas_call(
        _local_kernel,
        grid=(R // TM,),
        in_specs=[
            blk(C_CB), blk(C_CC), blk(C_CV), blk(C_PZ),
            pl.BlockSpec((8, MIX_W), lambda i: (0, 0)),
            pl.BlockSpec((len(POOL_WINDOWS), POOL_GW, POOL_GW), lambda i: (0, 0, 0)),
            pl.BlockSpec((1, MIX_W), lambda i: (0, 0)),
        ],
        out_specs=[
            pl.BlockSpec((TM, MIX_W), lambda i: (i, 0)),
            pl.BlockSpec((TM, MIX_W), lambda i: (i, 0)),
        ],
        out_shape=[
            jax.ShapeDtypeStruct((R, MIX_W), BF16),
            jax.ShapeDtypeStruct((R, MIX_W), BF16),
        ],
        scratch_shapes=[
            pltpu.VMEM((TM + HALO, MIX_W), F32),
            pltpu.VMEM((TM + HALO, MIX_W), F32),
        ],
        compiler_params=_cparams(("arbitrary",)),
        name="conv_pool",
    )(z, z, z, z, cw, pool_w_bf, pool_scale.reshape(1, -1))


TN_MERGE = 256


def _merge_kernel(hb_ref, oa_ref, ob_ref, oc_ref, od_ref, wg0_ref, wg1_ref, wg2_ref, wg3_ref,
                  gb_ref, wb_ref, wo_ref, out_ref):
    j = pl.program_id(1)

    @pl.when(j == 0)
    def _():
        out_ref[...] = jnp.zeros_like(out_ref)

    hb = hb_ref[...]
    mixed = None
    for b, (o_ref, wg_ref) in enumerate(((oa_ref, wg0_ref), (ob_ref, wg1_ref),
                                         (oc_ref, wg2_ref), (od_ref, wg3_ref))):
        gate = _sigmoid(jnp.dot(hb, wg_ref[...], preferred_element_type=F32) + gb_ref[b:b + 1, :])
        proj = jnp.dot(o_ref[...], wb_ref[b], preferred_element_type=F32)
        term = gate * proj
        mixed = term if mixed is None else mixed + term
    out_ref[...] += jnp.dot(mixed.astype(BF16), wo_ref[...], preferred_element_type=F32)


def _merge(hb, o_a, o_b, o_c, o_d, w_all, gate_b, wb_bf, wo_bf, l):
    tn = TN_MERGE
    nj = D_MODEL // tn
    row = lambda w: pl.BlockSpec((TM, w), lambda i, j: (i, 0))
    wg = lambda b: pl.BlockSpec((D_MODEL, tn), lambda i, j, b=b: (0, W_GATES // tn + b * nj + j))
    return pl.pallas_call(
        _merge_kernel,
        grid=(R // TM, nj),
        in_specs=[
            row(D_MODEL), row(MIX_W), row(MIX_W), row(MIX_W), row(MIX_W),
            wg(0), wg(1), wg(2), wg(3),
            pl.BlockSpec((N_BRANCH, tn), lambda i, j: (0, j)),
            pl.BlockSpec((None, N_BRANCH, MIX_W, tn), lambda i, j: (l, 0, 0, j)),
            pl.BlockSpec((None, tn, D_MODEL), lambda i, j: (l, j, 0)),
        ],
        out_specs=pl.BlockSpec((TM, D_MODEL), lambda i, j: (i, 0)),
        out_shape=jax.ShapeDtypeStruct((R, D_MODEL), F32),
        compiler_params=_cparams(("parallel", "arbitrary")),
        name="merge",
    )(hb, o_a, o_b, o_c, o_d, w_all, w_all, w_all, w_all, gate_b, wb_bf, wo_bf)


def _post_ln(h, delta, g, b, row0):
    y = _layer_norm_rows(DEEPNORM_ALPHA * h + delta, g, b)
    rows = row0 + lax.broadcasted_iota(I32, (y.shape[0], 1), 0)
    return jnp.where(rows >= T0, y, 0.0)


def _first_of(cands, target):
    idx = jnp.full(target.shape, len(cands) - 1, I32)
    for j in range(len(cands) - 2, -1, -1):
        idx = jnp.where(cands[j] == target, j, idx)
    return idx


def _pick(cands, idx):
    out = cands[-1]
    for j in range(len(cands) - 2, -1, -1):
        out = jnp.where(idx == j, cands[j], out)
    return out


def _ln1_route_kernel(h_ref, mix_ref, g_ref, b_ref, rwt_ref, rb_ref,
                      h1_ref, mi_ref, mf_ref, cnt_ref, carry_sc):
    i = pl.program_id(0)

    @pl.when(i == 0)
    def _():
        carry_sc[...] = jnp.zeros_like(carry_sc)

    y = _post_ln(h_ref[...], mix_ref[...], g_ref[...], b_ref[...], i * TM)
    h1_ref[...] = y

    logits = lax.dot_general(rwt_ref[...], y, (((1,), (1,)), ((), ())), precision=HIGHEST,
                             preferred_element_type=F32)
    aff = _sigmoid(logits)
    sel = aff + rb_ref[...]
    xs = [sel[j * N_GROUPS:(j + 1) * N_GROUPS, :] for j in range(EXPERTS_PER_GROUP)]
    afs = [aff[j * N_GROUPS:(j + 1) * N_GROUPS, :] for j in range(EXPERTS_PER_GROUP)]

    score = None
    for a in range(EXPERTS_PER_GROUP):
        for bb in range(a + 1, EXPERTS_PER_GROUP):
            pair = xs[a] + xs[bb]
            score = pair if score is None else jnp.maximum(score, pair)
    giota = lax.broadcasted_iota(I32, (N_GROUPS, TM), 0)
    gmax = jnp.max(score, axis=0, keepdims=True)
    grp = jnp.min(jnp.where(score == gmax, giota, N_GROUPS), axis=0, keepdims=True)
    gsel = giota == grp
    cs = [jnp.max(jnp.where(gsel, x, -jnp.inf), axis=0, keepdims=True) for x in xs]
    acs = [jnp.sum(jnp.where(gsel, a, 0.0), axis=0, keepdims=True) for a in afs]

    m1 = jnp.maximum(jnp.maximum(cs[0], cs[1]), jnp.maximum(cs[2], cs[3]))
    i0 = _first_of(cs, m1)
    ds = [jnp.where(i0 == j, -jnp.inf, cs[j]) for j in range(EXPERTS_PER_GROUP)]
    m2 = jnp.maximum(jnp.maximum(ds[0], ds[1]), jnp.maximum(ds[2], ds[3]))
    i1 = _first_of(ds, m2)
    a0 = _pick(acs, i0)
    a1 = _pick(acs, i1)
    denom = a0 + a1

    pos = i * TM + lax.broadcasted_iota(I32, (1, TM), 1)
    valid = pos >= T0
    riota = lax.broadcasted_iota(I32, (N_EXPERTS, TM), 0)
    oh0 = (riota == i0 * N_GROUPS + grp) & valid
    oh1 = (riota == i1 * N_GROUPS + grp) & valid
    ohf = jnp.where(oh0 | oh1, 1.0, 0.0)
    before = (lax.broadcasted_iota(I32, (TM, TM), 0)
              < lax.broadcasted_iota(I32, (TM, TM), 1)).astype(BF16)
    cum = jnp.dot(ohf.astype(BF16), before, preferred_element_type=F32) + carry_sc[...]
    rank0 = jnp.sum(jnp.where(oh0, cum, 0.0), axis=0, keepdims=True)
    rank1 = jnp.sum(jnp.where(oh1, cum, 0.0), axis=0, keepdims=True)
    carry = carry_sc[...] + jnp.sum(ohf, axis=1, keepdims=True)
    carry_sc[...] = carry
    cnt_ref[...] = jnp.broadcast_to(carry, cnt_ref.shape)

    zi = jnp.zeros((1, TM), I32)
    mi_ref[...] = jnp.concatenate(
        [grp * EXPERTS_PER_GROUP + i0, grp * EXPERTS_PER_GROUP + i1,
         rank0.astype(I32), rank1.astype(I32), zi, zi, zi, zi], axis=0)
    zf = jnp.zeros((1, TM), F32)
    mf_ref[...] = jnp.concatenate([a0 / denom, a1 / denom, zf, zf, zf, zf, zf, zf], axis=0)


def _ln1_route(h, mix, g, b, router_wt, router_bc):
    row = pl.BlockSpec((TM, D_MODEL), lambda i: (i, 0))
    vec = pl.BlockSpec((1, D_MODEL), lambda i: (0, 0))
    meta = pl.BlockSpec((8, TM), lambda i: (0, i))
    return pl.pallas_call(
        _ln1_route_kernel,
        grid=(R // TM,),
        in_specs=[row, row, vec, vec,
                  pl.BlockSpec((N_EXPERTS, D_MODEL), lambda i: (0, 0)),
                  pl.BlockSpec((N_EXPERTS, 1), lambda i: (0, 0))],
        out_specs=[row, meta, meta, pl.BlockSpec((N_EXPERTS, LANE), lambda i: (0, 0))],
        out_shape=[
            jax.ShapeDtypeStruct((R, D_MODEL), F32),
            jax.ShapeDtypeStruct((8, R), I32),
            jax.ShapeDtypeStruct((8, R), F32),
            jax.ShapeDtypeStruct((N_EXPERTS, LANE), F32),
        ],
        scratch_shapes=[pltpu.VMEM((N_EXPERTS, 1), F32)],
        compiler_params=_cparams(("arbitrary",)),
        name="ln1_route",
    )(h, mix, g.reshape(1, -1), b.reshape(1, -1), router_wt, router_bc)


def _dispatch_tables(mi, counts_slot_major):
    counts = counts_slot_major.reshape(EXPERTS_PER_GROUP, N_GROUPS).T.reshape(N_EXPERTS).astype(I32)
    padded = (counts + EXPERT_BLOCK - 1) // EXPERT_BLOCK * EXPERT_BLOCK
    pad_end = jnp.cumsum(padded)
    pad_start = pad_end - padded
    e_iota = jnp.arange(N_EXPERTS, dtype=I32)
    rows_ok = jnp.arange(R) >= T0

    def dest(eid, rank):
        start = jnp.sum(jnp.where(eid[:, None] == e_iota[None, :], pad_start[None, :], 0), axis=1)
        return jnp.where(rows_ok, start + rank, 0).astype(I32)

    d0 = dest(mi[0], mi[2])
    d1 = dest(mi[1], mi[3])
    blk_start = (pad_start // EXPERT_BLOCK).astype(I32)
    n_blk = (padded // EXPERT_BLOCK).astype(I32)
    n_used = (pad_end[-1] // EXPERT_BLOCK).astype(I32).reshape(1)
    tok_rows = jnp.arange(T0, R, dtype=I32)
    row_src = jnp.zeros((N_ROWS + EXPERT_BLOCK,), I32).at[jnp.concatenate([d0[T0:], d1[T0:]])].set(
        jnp.concatenate([tok_rows, tok_rows]), unique_indices=True)
    return d0, d1, blk_start, n_blk, n_used, row_src


def _row_copy(src, src_row, dst, dst_row, sem):
    return pltpu.make_async_copy(src.at[pl.ds(src_row, 1), :], dst.at[pl.ds(dst_row, 1), :], sem)


def _issue_row_gather(rs_ref, g, h_hbm, buf, sem):
    base = g * EXPERT_BLOCK
    for r in range(EXPERT_BLOCK):
        _row_copy(h_hbm, rs_ref[base + r], buf, r, sem).start()


def _wait_row_gather(h_hbm, buf, sem):
    for r in range(EXPERT_BLOCK):
        _row_copy(h_hbm, 0, buf, r, sem).wait()


def _block_rows(g):
    return pl.ds(pl.multiple_of(g * EXPERT_BLOCK, EXPERT_BLOCK), EXPERT_BLOCK)


def _finish_writes(out_copy, obuf, nu):
    @pl.when(nu >= 2)
    def _():
        out_copy(nu - 2, nu % 2).wait()

    @pl.when(nu >= 1)
    def _():
        out_copy(nu - 1, (nu - 1) % 2).wait()

    obuf[0] = jnp.zeros(obuf.shape[1:], obuf.dtype)

    def zero_block(g, carry):
        cp = out_copy(g, 0)
        cp.start()
        cp.wait()
        return carry

    lax.fori_loop(nu, N_BLOCKS, zero_block, 0)


def _moe_up_kernel(bs_ref, nb_ref, nu_ref, rs_ref, h_hbm, wg_ref, wu_ref, o_hbm,
                   wg_sc, wu_sc, xbuf, obuf, xsem, osem):
    e = pl.program_id(0)
    nb = nb_ref[e]
    g0 = bs_ref[e]

    def out_copy(g, slot):
        return pltpu.make_async_copy(obuf.at[slot], o_hbm.at[_block_rows(g), :], osem.at[slot])

    @pl.when(e == 0)
    def _():
        _issue_row_gather(rs_ref, 0, h_hbm, xbuf.at[0], xsem.at[0])

    @pl.when(nb > 0)
    def _():
        wg_sc[...] = wg_ref[...].astype(BF16)
        wu_sc[...] = wu_ref[...].astype(BF16)

        def block(j, carry):
            g = g0 + j
            slot = g % 2

            @pl.when(g >= 2)
            def _():
                out_copy(g - 2, slot).wait()

            _wait_row_gather(h_hbm, xbuf.at[slot], xsem.at[slot])
            _issue_row_gather(rs_ref, g + 1, h_hbm, xbuf.at[1 - slot], xsem.at[1 - slot])
            x = xbuf[slot].astype(BF16)
            gate = jnp.dot(x, wg_sc[...], preferred_element_type=F32)
            up = jnp.dot(x, wu_sc[...], preferred_element_type=F32)
            obuf[slot] = (gate * _sigmoid(gate) * up).astype(BF16)
            out_copy(g, slot).start()
            return carry

        lax.fori_loop(0, nb, block, 0)

    @pl.when(e == N_EXPERTS - 1)
    def _():
        nu = nu_ref[0]
        _wait_row_gather(h_hbm, xbuf.at[nu % 2], xsem.at[nu % 2])
        _finish_writes(out_copy, obuf, nu)


def _moe_up(blk_start, n_blk, n_used, row_src, h1, w_gate, w_up, l):
    wspec = pl.BlockSpec((None, None, D_MODEL, D_EXPERT), lambda e, *_: (l, e, 0, 0))
    grid_spec = pltpu.PrefetchScalarGridSpec(
        num_scalar_prefetch=4,
        grid=(N_EXPERTS,),
        in_specs=[pl.BlockSpec(memory_space=pl.ANY), wspec, wspec],
        out_specs=pl.BlockSpec(memory_space=pl.ANY),
        scratch_shapes=[
            pltpu.VMEM((D_MODEL, D_EXPERT), BF16),
            pltpu.VMEM((D_MODEL, D_EXPERT), BF16),
            pltpu.VMEM((2, EXPERT_BLOCK, D_MODEL), F32),
            pltpu.VMEM((2, EXPERT_BLOCK, D_EXPERT), BF16),
            pltpu.SemaphoreType.DMA((2,)),
            pltpu.SemaphoreType.DMA((2,)),
        ],
    )
    return pl.pallas_call(
        _moe_up_kernel,
        grid_spec=grid_spec,
        out_shape=jax.ShapeDtypeStruct((N_ROWS, D_EXPERT), BF16),
        compiler_params=_cparams(("arbitrary",), vmem=MOE_VMEM_LIMIT),
        name="moe_up",
    )(blk_start, n_blk, n_used, row_src, h1, w_gate, w_up)


def _moe_down_kernel(bs_ref, nb_ref, nu_ref, x_hbm, wd_ref, y_hbm, wd_sc, xbuf, obuf, xsem, osem):
    e = pl.program_id(0)
    nb = nb_ref[e]
    g0 = bs_ref[e]
    nu = nu_ref[0]

    def in_copy(g, slot):
        return pltpu.make_async_copy(x_hbm.at[_block_rows(g), :], xbuf.at[slot], xsem.at[slot])

    def out_copy(g, slot):
        return pltpu.make_async_copy(obuf.at[slot], y_hbm.at[_block_rows(g), :], osem.at[slot])

    @pl.when((e == 0) & (nu > 0))
    def _():
        in_copy(0, 0).start()

    @pl.when(nb > 0)
    def _():
        wd_sc[...] = wd_ref[...].astype(BF16)

        def block(j, carry):
            g = g0 + j
            slot = g % 2

            @pl.when(g >= 2)
            def _():
                out_copy(g - 2, slot).wait()

            in_copy(g, slot).wait()

            @pl.when(g + 1 < nu)
            def _():
                in_copy(g + 1, 1 - slot).start()

            obuf[slot] = jnp.dot(xbuf[slot], wd_sc[...], preferred_element_type=F32)
            out_copy(g, slot).start()
            return carry

        lax.fori_loop(0, nb, block, 0)

    @pl.when(e == N_EXPERTS - 1)
    def _():
        _finish_writes(out_copy, obuf, nu)


def _moe_down(blk_start, n_blk, n_used, hmid, w_down, l):
    grid_spec = pltpu.PrefetchScalarGridSpec(
        num_scalar_prefetch=3,
        grid=(N_EXPERTS,),
        in_specs=[
            pl.BlockSpec(memory_space=pl.ANY),
            pl.BlockSpec((None, None, D_EXPERT, D_MODEL), lambda e, *_: (l, e, 0, 0)),
        ],
        out_specs=pl.BlockSpec(memory_space=pl.ANY),
        scratch_shapes=[
            pltpu.VMEM((D_EXPERT, D_MODEL), BF16),
            pltpu.VMEM((2, EXPERT_BLOCK, D_EXPERT), BF16),
            pltpu.VMEM((2, EXPERT_BLOCK, D_MODEL), F32),
            pltpu.SemaphoreType.DMA((2,)),
            pltpu.SemaphoreType.DMA((2,)),
        ],
    )
    return pl.pallas_call(
        _moe_down_kernel,
        grid_spec=grid_spec,
        out_shape=jax.ShapeDtypeStruct((N_ROWS, D_MODEL), F32),
        compiler_params=_cparams(("arbitrary",), vmem=MOE_VMEM_LIMIT),
        name="moe_down",
    )(blk_start, n_blk, n_used, hmid, w_down)


def _combine_kernel(d0_ref, d1_ref, y_hbm, h1_ref, w0_ref, w1_ref, g_ref, b_ref, h2_ref, h2b_ref,
                    buf0, buf1, sem):
    i = pl.program_id(0)
    base = i * LANE

    def issue(r, carry):
        _row_copy(y_hbm, d0_ref[base + r], buf0, r, sem).start()
        _row_copy(y_hbm, d1_ref[base + r], buf1, r, sem).start()
        return carry

    lax.fori_loop(0, LANE, issue, 0)

    def drain(r, carry):
        _row_copy(y_hbm, 0, buf0, r, sem).wait()
        _row_copy(y_hbm, 0, buf1, r, sem).wait()
        return carry

    lax.fori_loop(0, LANE, drain, 0)
    ffn = w0_ref[...] * buf0[...] + w1_ref[...] * buf1[...]
    y = _post_ln(h1_ref[...], ffn, g_ref[...], b_ref[...], base)
    h2_ref[...] = y
    h2b_ref[...] = y.astype(BF16)


def _combine_ln2(d0, d1, y_rows, h1, w0, w1, g, b):
    row = lambda i, d0, d1: (i, 0)
    vec = pl.BlockSpec((1, D_MODEL), lambda i, d0, d1: (0, 0))
    col = pl.BlockSpec((LANE, 1), row)
    grid_spec = pltpu.PrefetchScalarGridSpec(
        num_scalar_prefetch=2,
        grid=(R // LANE,),
        in_specs=[
            pl.BlockSpec(memory_space=pl.ANY),
            pl.BlockSpec((LANE, D_MODEL), row),
            col, col, vec, vec,
        ],
        out_specs=[pl.BlockSpec((LANE, D_MODEL), row), pl.BlockSpec((LANE, D_MODEL), row)],
        scratch_shapes=[
            pltpu.VMEM((LANE, D_MODEL), F32),
            pltpu.VMEM((LANE, D_MODEL), F32),
            pltpu.SemaphoreType.DMA,
        ],
    )
    return pl.pallas_call(
        _combine_kernel,
        grid_spec=grid_spec,
        out_shape=[
            jax.ShapeDtypeStruct((R, D_MODEL), F32),
            jax.ShapeDtypeStruct((R, D_MODEL), BF16),
        ],
        compiler_params=_cparams(("arbitrary",)),
        name="moe_combine_ln2",
    )(d0, d1, y_rows, h1, w0, w1, g.reshape(1, -1), b.reshape(1, -1))


def kernel(x, meta_tokens, ln_in_g, ln_in_b, w_in, fox_f_bias, gla_wa2, gla_ba, gla_norm_g, conv_w, pool_w, pool_scale, gate_b, w_branch, w_out, ln1_g, ln1_b, router_w, router_b, w_gate, w_up, w_down, ln2_g, ln2_b):
    assert x.shape == (1, SEQ, D_MODEL)
    h, hb = _ln_in(x.reshape(SEQ, D_MODEL), meta_tokens, ln_in_g, ln_in_b)
    router_wt = router_w.T.reshape(N_GROUPS, EXPERTS_PER_GROUP, D_MODEL).transpose(1, 0, 2).reshape(
        N_EXPERTS, D_MODEL)
    router_bc = router_b.astype(F32).reshape(N_GROUPS, EXPERTS_PER_GROUP).T.reshape(N_EXPERTS, 1)

    wb_bf = w_branch.astype(BF16)
    wo_bf = w_out.astype(BF16)

    for l in range(DEPTH):
        w_all = _prep_w_in(w_in, l)
        zf = _matmul(hb, w_all, W_FOX, N_FOX, BF16, TM_PROJ, 768, "proj_fox")
        z = _matmul(hb, w_all, W_MIX, N_MIXC, F32, TM_PROJ, 512, "proj_mix")
        zs = _proj_small(hb, w_in, l)

        bias_row = jnp.zeros((1, LANE), F32).at[0, SM_FF:SM_FF + FOX_HEADS].set(fox_f_bias[l])
        c = _fox_gate(zs, bias_row)[:, SM_FF:SM_FF + FOX_HEADS]
        c_t = c.T
        o_a = _fox_attention(zf, c_t.reshape(FOX_HEADS, R, 1), c_t.reshape(FOX_HEADS, 1, R))

        wa2p = jnp.zeros((LANE, GLA_HEADS * GLA_DK), F32).at[SM_GA:SM_GA + GLA_RANK].set(gla_wa2[l])
        o_b = _gla(z, zs, wa2p, gla_ba[l].reshape(1, -1), gla_norm_g[l].reshape(1, -1))

        o_c, o_d = _local_mixers(z, conv_w[l], pool_w[l].astype(BF16), pool_scale[l])

        mix = _merge(hb, o_a, o_b, o_c, o_d, w_all, gate_b[l], wb_bf, wo_bf, l)
        h1, mi, mf, counts = _ln1_route(h, mix, ln1_g[l], ln1_b[l], router_wt, router_bc)

        d0, d1, blk_start, n_blk, n_used, row_src = _dispatch_tables(mi, counts[:, 0])
        hmid = _moe_up(blk_start, n_blk, n_used, row_src, h1, w_gate, w_up, l)
        y_rows = _moe_down(blk_start, n_blk, n_used, hmid, w_down, l)
        h, hb = _combine_ln2(d0, d1, y_rows, h1, mf[0].reshape(R, 1), mf[1].reshape(R, 1),
                             ln2_g[l], ln2_b[l])

    return h[PAD_ROWS + N_META:].reshape(1, SEQ, D_MODEL)
```

```python
import jax
import jax.numpy as jnp
import numpy as np
from jax import lax
from jax.experimental import pallas as pl
from jax.experimental.pallas import tpu as pltpu

F32 = jnp.float32
BF16 = jnp.bfloat16
I32 = jnp.int32
HIGHEST = lax.Precision.HIGHEST

D_MODEL = 2048
SEQ = 8192
DEPTH = 2
N_META = 16
N_BRANCH = 4
MIX_W = 512
FOX_HEADS = 4
FOX_HD = 128
GLA_HEADS = 4
GLA_DK = 64
GLA_DV = 128
GLA_RANK = 16
GLA_TAU = 16.0
CONV_K = 3
POOL_WINDOWS = (2, 4, 8, 16)
POOL_GW = 128
N_EXPERTS = 32
N_GROUPS = 8
EXPERTS_PER_GROUP = 4
TOP_K = 2
D_EXPERT = 1024
LN_EPS = 1e-5
DEEPNORM_ALPHA = (2 * DEPTH) ** 0.25

_SPLITS = (512, 512, 512, 4, 256, 256, 512, 16, 512, 512, 512, 512, 512, 8192)
_OFFS = [int(o) for o in np.concatenate([[0], np.cumsum(_SPLITS)])]
(O_FQ, O_FK, O_FV, O_FF, O_GQ, O_GK, O_GV, O_GA, O_GR, O_CB, O_CC, O_CV, O_PZ, O_GZ, P_IN) = _OFFS

LANE = 128
PAD_ROWS = LANE - N_META
T0 = PAD_ROWS
N_TOK = N_META + SEQ
R = PAD_ROWS + N_TOK
TM = 640
TM_PROJ = 1664
HALO = 16

WT = 512
W_FOX, W_MIX, W_GATES = 0, 1536, 5120
N_FOX, N_MIXC, N_GATES = 1536, 3584, 8192
N_WALL = W_GATES + N_GATES
SHIFT_G = O_GQ - W_FOX - N_FOX
SHIFT_M = O_GR - (W_MIX + 1024)
C_GQ, C_GK, C_GV, C_GR, C_CB, C_CC, C_CV, C_PZ = 0, 256, 512, 1024, 1536, 2048, 2560, 3072
SM_FF_TILE, SM_GA_TILE = O_FF // LANE, O_GA // LANE
SM_FF = O_FF - SM_FF_TILE * LANE
SM_GA = O_GA - SM_GA_TILE * LANE

EXPERT_BLOCK = 128
N_FLAT = N_TOK * TOP_K
N_BLOCKS = -(-N_FLAT // EXPERT_BLOCK) + N_EXPERTS
N_ROWS = N_BLOCKS * EXPERT_BLOCK

NEG = -1e30
VMEM_LIMIT = 48 * 1024 * 1024
MOE_VMEM_LIMIT = 56 * 1024 * 1024


def _cparams(sem, vmem=VMEM_LIMIT):
    return pltpu.CompilerParams(dimension_semantics=sem, vmem_limit_bytes=vmem)


def _log_sigmoid(x):
    return jnp.minimum(x, 0.0) - jnp.log1p(jnp.exp(-jnp.abs(x)))


def _sigmoid(x):
    return 1.0 / (1.0 + jnp.exp(-x))


def _layer_norm_rows(x, g, b):
    mu = jnp.mean(x, axis=-1, keepdims=True)
    xc = x - mu
    var = jnp.mean(xc * xc, axis=-1, keepdims=True)
    return xc * lax.rsqrt(var + LN_EPS) * g + b


def _ln_in_kernel(x_ref, meta_ref, g_ref, b_ref, h_ref, hb_ref):
    i = pl.program_id(0)

    @pl.when(i == 0)
    def _():
        h_ref[...] = jnp.zeros_like(h_ref)
        hb_ref[...] = jnp.zeros_like(hb_ref)
        m = _layer_norm_rows(meta_ref[...], g_ref[...], b_ref[...])
        h_ref[PAD_ROWS:, :] = m
        hb_ref[PAD_ROWS:, :] = m.astype(BF16)

    @pl.when(i > 0)
    def _():
        y = _layer_norm_rows(x_ref[...], g_ref[...], b_ref[...])
        h_ref[...] = y
        hb_ref[...] = y.astype(BF16)


def _ln_in(x2d, meta, g, b):
    nb = R // LANE
    return pl.pallas_call(
        _ln_in_kernel,
        grid=(nb,),
        in_specs=[
            pl.BlockSpec((LANE, D_MODEL), lambda i: (jnp.maximum(i - 1, 0), 0)),
            pl.BlockSpec((N_META, D_MODEL), lambda i: (0, 0)),
            pl.BlockSpec((1, D_MODEL), lambda i: (0, 0)),
            pl.BlockSpec((1, D_MODEL), lambda i: (0, 0)),
        ],
        out_specs=[
            pl.BlockSpec((LANE, D_MODEL), lambda i: (i, 0)),
            pl.BlockSpec((LANE, D_MODEL), lambda i: (i, 0)),
        ],
        out_shape=[
            jax.ShapeDtypeStruct((R, D_MODEL), F32),
            jax.ShapeDtypeStruct((R, D_MODEL), BF16),
        ],
        compiler_params=_cparams(("arbitrary",)),
        name="ln_in",
    )(x2d, meta, g.reshape(1, -1), b.reshape(1, -1))


TR_PREP = 1024


def _wprep_kernel(a_ref, b_ref, o_ref):
    j = pl.program_id(1)

    def emit(shift):
        if shift == 0:
            o_ref[...] = a_ref[...].astype(BF16)
        else:
            x = jnp.concatenate([a_ref[...], b_ref[...]], axis=1)
            o_ref[...] = pltpu.roll(x, WT + LANE - shift, axis=1)[:, :WT].astype(BF16)

    first_g = N_FOX // WT
    first_m = (W_MIX + 1024) // WT

    @pl.when(j < first_g)
    def _():
        emit(0)

    @pl.when((j >= first_g) & (j < first_m))
    def _():
        emit(SHIFT_G)

    @pl.when(j >= first_m)
    def _():
        emit(SHIFT_M)


def _prep_w_in(w_in, l):
    return pl.pallas_call(
        _wprep_kernel,
        grid=(D_MODEL // TR_PREP, N_WALL // WT),
        in_specs=[
            pl.BlockSpec((None, TR_PREP, WT), lambda i, j: (l, i, j)),
            pl.BlockSpec((None, TR_PREP, LANE), lambda i, j: (l, i, (j + 1) * (WT // LANE))),
        ],
        out_specs=pl.BlockSpec((TR_PREP, WT), lambda i, j: (i, j)),
        out_shape=jax.ShapeDtypeStruct((D_MODEL, N_WALL), BF16),
        compiler_params=_cparams(("parallel", "arbitrary")),
        name="prep_w_in",
    )(w_in, w_in)


def _mm_kernel(a_ref, w_ref, o_ref):
    o_ref[...] = jnp.dot(a_ref[...], w_ref[...], preferred_element_type=F32).astype(o_ref.dtype)


def _matmul(a, w, col0, n, out_dtype, tm, tn, name):
    m, k = a.shape
    return pl.pallas_call(
        _mm_kernel,
        grid=(m // tm, n // tn),
        in_specs=[
            pl.BlockSpec((tm, k), lambda i, j: (i, 0)),
            pl.BlockSpec((k, tn), lambda i, j: (0, col0 // tn + j)),
        ],
        out_specs=pl.BlockSpec((tm, tn), lambda i, j: (i, j)),
        out_shape=jax.ShapeDtypeStruct((m, n), out_dtype),
        compiler_params=_cparams(("parallel", "arbitrary")),
        name=name,
    )(a, w)


def _proj_small_kernel(a_ref, w_ref, o_ref):
    o_ref[...] = jnp.dot(a_ref[...], w_ref[...].astype(BF16), preferred_element_type=F32)


def _proj_small(hb, w_in, l):
    return pl.pallas_call(
        _proj_small_kernel,
        grid=(R // TM_PROJ, 2),
        in_specs=[
            pl.BlockSpec((TM_PROJ, D_MODEL), lambda i, j: (i, 0)),
            pl.BlockSpec((None, D_MODEL, LANE),
                         lambda i, j: (l, 0, SM_FF_TILE + j * (SM_GA_TILE - SM_FF_TILE))),
        ],
        out_specs=pl.BlockSpec((TM_PROJ, LANE), lambda i, j: (i, j)),
        out_shape=jax.ShapeDtypeStruct((R, 2 * LANE), F32),
        compiler_params=_cparams(("parallel", "arbitrary")),
        name="proj_small",
    )(hb, w_in)


def _fox_gate_kernel(zs_ref, bias_ref, c_ref, carry_ref):
    i = pl.program_id(0)

    @pl.when(i == 0)
    def _():
        carry_ref[...] = jnp.zeros_like(carry_ref)

    rows = i * TM + lax.broadcasted_iota(I32, (TM, LANE), 0)
    lf = _log_sigmoid(zs_ref[...] + bias_ref[...])
    lf = jnp.where(rows >= T0, lf, 0.0)
    tri = (lax.broadcasted_iota(I32, (TM, TM), 0)
           >= lax.broadcasted_iota(I32, (TM, TM), 1)).astype(F32)
    c = jnp.dot(tri, lf, precision=HIGHEST, preferred_element_type=F32) + carry_ref[...]
    c_ref[...] = c
    carry_ref[...] = c[TM - 1:TM, :]


def _fox_gate(zs, bias_row):
    return pl.pallas_call(
        _fox_gate_kernel,
        grid=(R // TM,),
        in_specs=[
            pl.BlockSpec((TM, LANE), lambda i: (i, 0)),
            pl.BlockSpec((1, LANE), lambda i: (0, 0)),
        ],
        out_specs=pl.BlockSpec((TM, LANE), lambda i: (i, 0)),
        out_shape=jax.ShapeDtypeStruct((R, LANE), F32),
        scratch_shapes=[pltpu.VMEM((1, LANE), F32)],
        compiler_params=_cparams(("arbitrary",)),
        name="fox_gate",
    )(zs, bias_row)


TQ = TM
N_QB = R // TQ
_PAIRS = [(qi, kj) for qi in range(N_QB) for kj in range(qi + 1)]
N_PAIRS = len(_PAIRS)


def _fox_kernel(qi_tab, kj_tab, q_ref, k_ref, v_ref, cq_ref, ck_ref, o_ref, m_sc, l_sc, acc_sc):
    p = pl.program_id(1)
    qi = qi_tab[p]
    kj = kj_tab[p]

    @pl.when(kj == 0)
    def _():
        m_sc[...] = jnp.full_like(m_sc, NEG)
        l_sc[...] = jnp.zeros_like(l_sc)
        acc_sc[...] = jnp.zeros_like(acc_sc)

    def step(masked):
        s = lax.dot_general(q_ref[...], k_ref[...], (((1,), (1,)), ((), ())),
                            preferred_element_type=F32)
        s = s * (FOX_HD ** -0.5) + cq_ref[...] - ck_ref[...]
        if masked:
            qpos = qi * TQ + lax.broadcasted_iota(I32, (TQ, TQ), 0)
            kpos = kj * TQ + lax.broadcasted_iota(I32, (TQ, TQ), 1)
            ok = (kpos <= qpos) & ((kpos >= T0) | (qpos < T0))
            s = jnp.where(ok, s, NEG)
        m_prev = m_sc[...]
        m_new = jnp.maximum(m_prev, jnp.max(s, axis=-1, keepdims=True))
        alpha = jnp.exp(m_prev - m_new)
        pr = jnp.exp(s - m_new)
        l_sc[...] = alpha * l_sc[...] + jnp.sum(pr, axis=-1, keepdims=True)
        acc_sc[...] = alpha * acc_sc[...] + jnp.dot(pr.astype(BF16), v_ref[...],
                                                    preferred_element_type=F32)
        m_sc[...] = m_new

    needs_mask = (kj == qi) | (kj == 0)

    @pl.when(needs_mask)
    def _():
        step(True)

    @pl.when(jnp.logical_not(needs_mask))
    def _():
        step(False)

    @pl.when(kj == qi)
    def _():
        o_ref[...] = (acc_sc[...] / l_sc[...]).astype(o_ref.dtype)


def _fox_attention(zf, c_col, c_row):
    qi_tab = jnp.asarray([p[0] for p in _PAIRS], I32)
    kj_tab = jnp.asarray([p[1] for p in _PAIRS], I32)
    grid_spec = pltpu.PrefetchScalarGridSpec(
        num_scalar_prefetch=2,
        grid=(FOX_HEADS, N_PAIRS),
        in_specs=[
            pl.BlockSpec((TQ, FOX_HD), lambda h, p, qt, kt: (qt[p], h)),
            pl.BlockSpec((TQ, FOX_HD), lambda h, p, qt, kt: (kt[p], FOX_HEADS + h)),
            pl.BlockSpec((TQ, FOX_HD), lambda h, p, qt, kt: (kt[p], 2 * FOX_HEADS + h)),
            pl.BlockSpec((None, TQ, 1), lambda h, p, qt, kt: (h, qt[p], 0)),
            pl.BlockSpec((None, 1, TQ), lambda h, p, qt, kt: (h, 0, kt[p])),
        ],
        out_specs=pl.BlockSpec((TQ, FOX_HD), lambda h, p, qt, kt: (qt[p], h)),
        scratch_shapes=[
            pltpu.VMEM((TQ, 1), F32),
            pltpu.VMEM((TQ, 1), F32),
            pltpu.VMEM((TQ, FOX_HD), F32),
        ],
    )
    return pl.pallas_call(
        _fox_kernel,
        grid_spec=grid_spec,
        out_shape=jax.ShapeDtypeStruct((R, MIX_W), BF16),
        compiler_params=_cparams(("parallel", "arbitrary")),
        name="fox_attention",
    )(qi_tab, kj_tab, zf, zf, zf, c_col, c_row)


GLA_CHUNK = 64


def _gla_kernel(q_ref, k_ref, v_ref, gr_ref, zs_ref, wa2_ref, ba_ref, gn_ref, o_ref, st_ref, la_ref):
    i = pl.program_id(0)

    @pl.when(i == 0)
    def _():
        st_ref[...] = jnp.zeros_like(st_ref)

    la = jnp.dot(zs_ref[...], wa2_ref[...], precision=HIGHEST, preferred_element_type=F32)
    la_ref[...] = _log_sigmoid(la + ba_ref[...]) * (1.0 / GLA_TAU)

    c_r = lax.broadcasted_iota(I32, (GLA_CHUNK, GLA_CHUNK), 0)
    c_c = lax.broadcasted_iota(I32, (GLA_CHUNK, GLA_CHUNK), 1)
    tri_b = c_r >= c_c
    tri = tri_b.astype(F32)

    def chunk(c, carry):
        r0 = pl.multiple_of(c * GLA_CHUNK, GLA_CHUNK)
        rows = pl.ds(r0, GLA_CHUNK)
        g = la_ref[rows, :]
        b = jnp.dot(tri, g, precision=HIGHEST, preferred_element_type=F32)
        b_last = b[GLA_CHUNK - 1:GLA_CHUNK, :]
        e_last = jnp.exp(b_last)
        qt = q_ref[rows, :] * (GLA_DK ** -0.5) * jnp.exp(b)
        kt = k_ref[rows, :] * jnp.exp(-b)
        kh = kt * e_last
        for h in range(GLA_HEADS):
            ks = slice(h * GLA_DK, (h + 1) * GLA_DK)
            vs = slice(h * GLA_DV, (h + 1) * GLA_DV)
            q_h = qt[:, ks].astype(BF16)
            k_h = kt[:, ks].astype(BF16)
            kh_h = kh[:, ks].astype(BF16)
            v_h = v_ref[rows, vs]
            att = lax.dot_general(q_h, k_h, (((1,), (1,)), ((), ())), preferred_element_type=F32)
            att = jnp.where(tri_b, att, 0.0)
            st = st_ref[h]
            o = jnp.dot(att.astype(BF16), v_h.astype(BF16), preferred_element_type=F32)
            o = o + lax.dot_general(q_h, st.astype(BF16), (((1,), (1,)), ((), ())),
                                    preferred_element_type=F32)
            st_ref[h] = st * e_last[:, ks] + jnp.dot(v_h.T.astype(BF16), kh_h,
                                                     preferred_element_type=F32)
            ms = jnp.mean(o * o, axis=-1, keepdims=True)
            on = o * lax.rsqrt(ms + LN_EPS) * gn_ref[:, vs]
            gate = gr_ref[rows, vs]
            o_ref[rows, vs] = (on * (gate * _sigmoid(gate))).astype(o_ref.dtype)
        return carry

    lax.fori_loop(0, TM // GLA_CHUNK, chunk, 0)


def _gla(z, zs, wa2p, ba, gn):
    return pl.pallas_call(
        _gla_kernel,
        grid=(R // TM,),
        in_specs=[
            pl.BlockSpec((TM, 256), lambda i: (i, C_GQ // 256)),
            pl.BlockSpec((TM, 256), lambda i: (i, C_GK // 256)),
            pl.BlockSpec((TM, 512), lambda i: (i, C_GV // 512)),
            pl.BlockSpec((TM, 512), lambda i: (i, C_GR // 512)),
            pl.BlockSpec((TM, LANE), lambda i: (i, 1)),
            pl.BlockSpec((LANE, 256), lambda i: (0, 0)),
            pl.BlockSpec((1, 256), lambda i: (0, 0)),
            pl.BlockSpec((1, 512), lambda i: (0, 0)),
        ],
        out_specs=pl.BlockSpec((TM, MIX_W), lambda i: (i, 0)),
        out_shape=jax.ShapeDtypeStruct((R, MIX_W), BF16),
        scratch_shapes=[
            pltpu.VMEM((GLA_HEADS, GLA_DV, GLA_DK), F32),
            pltpu.VMEM((TM, GLA_HEADS * GLA_DK), F32),
        ],
        compiler_params=_cparams(("arbitrary",)),
        name="gla",
    )(z, z, z, z, zs, wa2p, ba, gn)


def _local_kernel(cb_ref, cc_ref, cv_ref, pz_ref, cw_ref, pw_ref, ps_ref, oc_ref, od_ref, u_sc, p_sc):
    i = pl.program_id(0)

    @pl.when(i == 0)
    def _():
        u_sc[0:HALO, :] = jnp.zeros((HALO, MIX_W), F32)
        p_sc[0:HALO, :] = jnp.zeros((HALO, MIX_W), F32)

    @pl.when(i > 0)
    def _():
        u_sc[0:HALO, :] = u_sc[TM:TM + HALO, :]
        p_sc[0:HALO, :] = p_sc[TM:TM + HALO, :]

    u = cc_ref[...] * cv_ref[...]
    pz = pz_ref[...]
    u_sc[HALO:, :] = u
    p_sc[HALO:, :] = pz

    y = (cw_ref[2:3, :] * u + cw_ref[1:2, :] * u_sc[HALO - 1:HALO - 1 + TM, :]
         + cw_ref[0:1, :] * u_sc[HALO - 2:HALO - 2 + TM, :])
    oc_ref[...] = (cb_ref[...] * y).astype(oc_ref.dtype)

    tok = i * TM - T0 + lax.broadcasted_iota(I32, (TM, 1), 0)
    cnt_small = jnp.maximum(tok + 1, 1).astype(F32)
    for g, w in enumerate(POOL_WINDOWS):
        cols = slice(g * POOL_GW, (g + 1) * POOL_GW)
        x = pz[:, cols]
        s = x
        for j in range(1, w):
            s = s + p_sc[HALO - j:HALO - j + TM, cols]
        inv_cnt = jnp.where(tok + 1 >= w, 1.0 / w, 1.0 / cnt_small)
        pooled = s * inv_cnt - x
        od = jnp.dot(pooled.astype(BF16), pw_ref[g], preferred_element_type=F32)
        od_ref[:, cols] = (od * ps_ref[:, cols]).astype(od_ref.dtype)


def _local_mixers(z, conv_w, pool_w_bf, pool_scale):
    cw = jnp.zeros((8, MIX_W), F32).at[:CONV_K].set(conv_w)
    blk = lambda c: pl.BlockSpec((TM, MIX_W), lambda i, c=c: (i, c // MIX_W))
    return pl.pallas_call(
        _local_kernel,
        grid=(R // TM,),
        in_specs=[
            blk(C_CB), blk(C_CC), blk(C_CV), blk(C_PZ),
            pl.BlockSpec((8, MIX_W), lambda i: (0, 0)),
            pl.BlockSpec((len(POOL_WINDOWS), POOL_GW, POOL_GW), lambda i: (0, 0, 0)),
            pl.BlockSpec((1, MIX_W), lambda i: (0, 0)),
        ],
        out_specs=[
            pl.BlockSpec((TM, MIX_W), lambda i: (i, 0)),
            pl.BlockSpec((TM, MIX_W), lambda i: (i, 0)),
        ],
        out_shape=[
            jax.ShapeDtypeStruct((R, MIX_W), BF16),
            jax.ShapeDtypeStruct((R, MIX_W), BF16),
        ],
        scratch_shapes=[
            pltpu.VMEM((TM + HALO, MIX_W), F32),
            pltpu.VMEM((TM + HALO, MIX_W), F32),
        ],
        compiler_params=_cparams(("arbitrary",)),
        name="conv_pool",
    )(z, z, z, z, cw, pool_w_bf, pool_scale.reshape(1, -1))


TN_MERGE = 256


def _merge_kernel(hb_ref, oa_ref, ob_ref, oc_ref, od_ref, wg0_ref, wg1_ref, wg2_ref, wg3_ref,
                  gb_ref, wb_ref, wo_ref, out_ref):
    j = pl.program_id(1)

    @pl.when(j == 0)
    def _():
        out_ref[...] = jnp.zeros_like(out_ref)

    hb = hb_ref[...]
    mixed = None
    for b, (o_ref, wg_ref) in enumerate(((oa_ref, wg0_ref), (ob_ref, wg1_ref),
                                         (oc_ref, wg2_ref), (od_ref, wg3_ref))):
        gate = _sigmoid(jnp.dot(hb, wg_ref[...], preferred_element_type=F32) + gb_ref[b:b + 1, :])
        proj = jnp.dot(o_ref[...], wb_ref[b], preferred_element_type=F32)
        term = gate * proj
        mixed = term if mixed is None else mixed + term
    out_ref[...] += jnp.dot(mixed.astype(BF16), wo_ref[...], preferred_element_type=F32)


def _merge(hb, o_a, o_b, o_c, o_d, w_all, gate_b, wb_bf, wo_bf, l):
    tn = TN_MERGE
    nj = D_MODEL // tn
    row = lambda w: pl.BlockSpec((TM, w), lambda i, j: (i, 0))
    wg = lambda b: pl.BlockSpec((D_MODEL, tn), lambda i, j, b=b: (0, W_GATES // tn + b * nj + j))
    return pl.pallas_call(
        _merge_kernel,
        grid=(R // TM, nj),
        in_specs=[
            row(D_MODEL), row(MIX_W), row(MIX_W), row(MIX_W), row(MIX_W),
            wg(0), wg(1), wg(2), wg(3),
            pl.BlockSpec((N_BRANCH, tn), lambda i, j: (0, j)),
            pl.BlockSpec((None, N_BRANCH, MIX_W, tn), lambda i, j: (l, 0, 0, j)),
            pl.BlockSpec((None, tn, D_MODEL), lambda i, j: (l, j, 0)),
        ],
        out_specs=pl.BlockSpec((TM, D_MODEL), lambda i, j: (i, 0)),
        out_shape=jax.ShapeDtypeStruct((R, D_MODEL), F32),
        compiler_params=_cparams(("parallel", "arbitrary")),
        name="merge",
    )(hb, o_a, o_b, o_c, o_d, w_all, w_all, w_all, w_all, gate_b, wb_bf, wo_bf)


def _post_ln(h, delta, g, b, row0):
    y = _layer_norm_rows(DEEPNORM_ALPHA * h + delta, g, b)
    rows = row0 + lax.broadcasted_iota(I32, (y.shape[0], 1), 0)
    return jnp.where(rows >= T0, y, 0.0)


def _first_of(cands, target):
    idx = jnp.full(target.shape, len(cands) - 1, I32)
    for j in range(len(cands) - 2, -1, -1):
        idx = jnp.where(cands[j] == target, j, idx)
    return idx


def _pick(cands, idx):
    out = cands[-1]
    for j in range(len(cands) - 2, -1, -1):
        out = jnp.where(idx == j, cands[j], out)
    return out


def _ln1_route_kernel(h_ref, mix_ref, g_ref, b_ref, rwt_ref, rb_ref,
                      h1_ref, mi_ref, mf_ref, cnt_ref, carry_sc):
    i = pl.program_id(0)

    @pl.when(i == 0)
    def _():
        carry_sc[...] = jnp.zeros_like(carry_sc)

    y = _post_ln(h_ref[...], mix_ref[...], g_ref[...], b_ref[...], i * TM)
    h1_ref[...] = y

    logits = lax.dot_general(rwt_ref[...], y, (((1,), (1,)), ((), ())), precision=HIGHEST,
                             preferred_element_type=F32)
    aff = _sigmoid(logits)
    sel = aff + rb_ref[...]
    xs = [sel[j * N_GROUPS:(j + 1) * N_GROUPS, :] for j in range(EXPERTS_PER_GROUP)]
    afs = [aff[j * N_GROUPS:(j + 1) * N_GROUPS, :] for j in range(EXPERTS_PER_GROUP)]

    score = None
    for a in range(EXPERTS_PER_GROUP):
        for bb in range(a + 1, EXPERTS_PER_GROUP):
            pair = xs[a] + xs[bb]
            score = pair if score is None else jnp.maximum(score, pair)
    giota = lax.broadcasted_iota(I32, (N_GROUPS, TM), 0)
    gmax = jnp.max(score, axis=0, keepdims=True)
    grp = jnp.min(jnp.where(score == gmax, giota, N_GROUPS), axis=0, keepdims=True)
    gsel = giota == grp
    cs = [jnp.max(jnp.where(gsel, x, -jnp.inf), axis=0, keepdims=True) for x in xs]
    acs = [jnp.sum(jnp.where(gsel, a, 0.0), axis=0, keepdims=True) for a in afs]

    m1 = jnp.maximum(jnp.maximum(cs[0], cs[1]), jnp.maximum(cs[2], cs[3]))
    i0 = _first_of(cs, m1)
    ds = [jnp.where(i0 == j, -jnp.inf, cs[j]) for j in range(EXPERTS_PER_GROUP)]
    m2 = jnp.maximum(jnp.maximum(ds[0], ds[1]), jnp.maximum(ds[2], ds[3]))
    i1 = _first_of(ds, m2)
    a0 = _pick(acs, i0)
    a1 = _pick(acs, i1)
    denom = a0 + a1

    pos = i * TM + lax.broadcasted_iota(I32, (1, TM), 1)
    valid = pos >= T0
    riota = lax.broadcasted_iota(I32, (N_EXPERTS, TM), 0)
    oh0 = (riota == i0 * N_GROUPS + grp) & valid
    oh1 = (riota == i1 * N_GROUPS + grp) & valid
    ohf = jnp.where(oh0 | oh1, 1.0, 0.0)
    before = (lax.broadcasted_iota(I32, (TM, TM), 0)
              < lax.broadcasted_iota(I32, (TM, TM), 1)).astype(BF16)
    cum = jnp.dot(ohf.astype(BF16), before, preferred_element_type=F32) + carry_sc[...]
    rank0 = jnp.sum(jnp.where(oh0, cum, 0.0), axis=0, keepdims=True)
    rank1 = jnp.sum(jnp.where(oh1, cum, 0.0), axis=0, keepdims=True)
    carry = carry_sc[...] + jnp.sum(ohf, axis=1, keepdims=True)
    carry_sc[...] = carry
    cnt_ref[...] = jnp.broadcast_to(carry, cnt_ref.shape)

    zi = jnp.zeros((1, TM), I32)
    mi_ref[...] = jnp.concatenate(
        [grp * EXPERTS_PER_GROUP + i0, grp * EXPERTS_PER_GROUP + i1,
         rank0.astype(I32), rank1.astype(I32), zi, zi, zi, zi], axis=0)
    zf = jnp.zeros((1, TM), F32)
    mf_ref[...] = jnp.concatenate([a0 / denom, a1 / denom, zf, zf, zf, zf, zf, zf], axis=0)


def _ln1_route(h, mix, g, b, router_wt, router_bc):
    row = pl.BlockSpec((TM, D_MODEL), lambda i: (i, 0))
    vec = pl.BlockSpec((1, D_MODEL), lambda i: (0, 0))
    meta = pl.BlockSpec((8, TM), lambda i: (0, i))
    return pl.pallas_call(
        _ln1_route_kernel,
        grid=(R // TM,),
        in_specs=[row, row, vec, vec,
                  pl.BlockSpec((N_EXPERTS, D_MODEL), lambda i: (0, 0)),
                  pl.BlockSpec((N_EXPERTS, 1), lambda i: (0, 0))],
        out_specs=[row, meta, meta, pl.BlockSpec((N_EXPERTS, LANE), lambda i: (0, 0))],
        out_shape=[
            jax.ShapeDtypeStruct((R, D_MODEL), F32),
            jax.ShapeDtypeStruct((8, R), I32),
            jax.ShapeDtypeStruct((8, R), F32),
            jax.ShapeDtypeStruct((N_EXPERTS, LANE), F32),
        ],
        scratch_shapes=[pltpu.VMEM((N_EXPERTS, 1), F32)],
        compiler_params=_cparams(("arbitrary",)),
        name="ln1_route",
    )(h, mix, g.reshape(1, -1), b.reshape(1, -1), router_wt, router_bc)


def _dispatch_tables(mi, counts_slot_major):
    counts = counts_slot_major.reshape(EXPERTS_PER_GROUP, N_GROUPS).T.reshape(N_EXPERTS).astype(I32)
    padded = (counts + EXPERT_BLOCK - 1) // EXPERT_BLOCK * EXPERT_BLOCK
    pad_end = jnp.cumsum(padded)
    pad_start = pad_end - padded
    e_iota = jnp.arange(N_EXPERTS, dtype=I32)
    rows_ok = jnp.arange(R) >= T0

    def dest(eid, rank):
        start = jnp.sum(jnp.where(eid[:, None] == e_iota[None, :], pad_start[None, :], 0), axis=1)
        return jnp.where(rows_ok, start + rank, 0).astype(I32)

    d0 = dest(mi[0], mi[2])
    d1 = dest(mi[1], mi[3])
    blk_start = (pad_start // EXPERT_BLOCK).astype(I32)
    n_blk = (padded // EXPERT_BLOCK).astype(I32)
    n_used = (pad_end[-1] // EXPERT_BLOCK).astype(I32).reshape(1)
    tok_rows = jnp.arange(T0, R, dtype=I32)
    row_src = jnp.zeros((N_ROWS + EXPERT_BLOCK,), I32).at[jnp.concatenate([d0[T0:], d1[T0:]])].set(
        jnp.concatenate([tok_rows, tok_rows]), unique_indices=True)
    return d0, d1, blk_start, n_blk, n_used, row_src


def _row_copy(src, src_row, dst, dst_row, sem):
    return pltpu.make_async_copy(src.at[pl.ds(src_row, 1), :], dst.at[pl.ds(dst_row, 1), :], sem)


def _issue_row_gather(rs_ref, g, h_hbm, buf, sem):
    base = g * EXPERT_BLOCK
    for r in range(EXPERT_BLOCK):
        _row_copy(h_hbm, rs_ref[base + r], buf, r, sem).start(priority=1)


def _wait_row_gather(h_hbm, buf, sem):
    for r in range(EXPERT_BLOCK):
        _row_copy(h_hbm, 0, buf, r, sem).wait()


def _block_rows(g):
    return pl.ds(pl.multiple_of(g * EXPERT_BLOCK, EXPERT_BLOCK), EXPERT_BLOCK)


def _finish_writes(out_copy, obuf, nu):
    @pl.when(nu >= 2)
    def _():
        out_copy(nu - 2, nu % 2).wait()

    @pl.when(nu >= 1)
    def _():
        out_copy(nu - 1, (nu - 1) % 2).wait()

    obuf[0] = jnp.zeros(obuf.shape[1:], obuf.dtype)

    def zero_block(g, carry):
        cp = out_copy(g, 0)
        cp.start()
        cp.wait()
        return carry

    lax.fori_loop(nu, N_BLOCKS, zero_block, 0)


def _moe_up_kernel(bs_ref, nb_ref, nu_ref, rs_ref, h_hbm, wg_ref, wu_ref, o_hbm,
                   wg_sc, wu_sc, xbuf, obuf, xsem, osem):
    e = pl.program_id(0)
    nb = nb_ref[e]
    g0 = bs_ref[e]

    def out_copy(g, slot):
        return pltpu.make_async_copy(obuf.at[slot], o_hbm.at[_block_rows(g), :], osem.at[slot])

    @pl.when(e == 0)
    def _():
        _issue_row_gather(rs_ref, 0, h_hbm, xbuf.at[0], xsem.at[0])

    @pl.when(nb > 0)
    def _():
        wg_sc[...] = wg_ref[...].astype(BF16)
        wu_sc[...] = wu_ref[...].astype(BF16)

        def block(j, carry):
            g = g0 + j
            slot = g % 2

            @pl.when(g >= 2)
            def _():
                out_copy(g - 2, slot).wait()

            _wait_row_gather(h_hbm, xbuf.at[slot], xsem.at[slot])
            _issue_row_gather(rs_ref, g + 1, h_hbm, xbuf.at[1 - slot], xsem.at[1 - slot])
            x = xbuf[slot].astype(BF16)
            gate = jnp.dot(x, wg_sc[...], preferred_element_type=F32)
            up = jnp.dot(x, wu_sc[...], preferred_element_type=F32)
            obuf[slot] = (gate * _sigmoid(gate) * up).astype(BF16)
            out_copy(g, slot).start()
            return carry

        lax.fori_loop(0, nb, block, 0)

    @pl.when(e == N_EXPERTS - 1)
    def _():
        nu = nu_ref[0]
        _wait_row_gather(h_hbm, xbuf.at[nu % 2], xsem.at[nu % 2])
        _finish_writes(out_copy, obuf, nu)


def _moe_up(blk_start, n_blk, n_used, row_src, h1, w_gate, w_up, l):
    wspec = pl.BlockSpec((None, None, D_MODEL, D_EXPERT), lambda e, *_: (l, e, 0, 0))
    grid_spec = pltpu.PrefetchScalarGridSpec(
        num_scalar_prefetch=4,
        grid=(N_EXPERTS,),
        in_specs=[pl.BlockSpec(memory_space=pl.ANY), wspec, wspec],
        out_specs=pl.BlockSpec(memory_space=pl.ANY),
        scratch_shapes=[
            pltpu.VMEM((D_MODEL, D_EXPERT), BF16),
            pltpu.VMEM((D_MODEL, D_EXPERT), BF16),
            pltpu.VMEM((2, EXPERT_BLOCK, D_MODEL), F32),
            pltpu.VMEM((2, EXPERT_BLOCK, D_EXPERT), BF16),
            pltpu.SemaphoreType.DMA((2,)),
            pltpu.SemaphoreType.DMA((2,)),
        ],
    )
    return pl.pallas_call(
        _moe_up_kernel,
        grid_spec=grid_spec,
        out_shape=jax.ShapeDtypeStruct((N_ROWS, D_EXPERT), BF16),
        compiler_params=_cparams(("arbitrary",), vmem=MOE_VMEM_LIMIT),
        name="moe_up",
    )(blk_start, n_blk, n_used, row_src, h1, w_gate, w_up)


def _moe_down_kernel(bs_ref, nb_ref, nu_ref, x_hbm, wd_ref, y_hbm, wd_sc, xbuf, obuf, xsem, osem):
    e = pl.program_id(0)
    nb = nb_ref[e]
    g0 = bs_ref[e]
    nu = nu_ref[0]

    def in_copy(g, slot):
        return pltpu.make_async_copy(x_hbm.at[_block_rows(g), :], xbuf.at[slot], xsem.at[slot])

    def out_copy(g, slot):
        return pltpu.make_async_copy(obuf.at[slot], y_hbm.at[_block_rows(g), :], osem.at[slot])

    @pl.when((e == 0) & (nu > 0))
    def _():
        in_copy(0, 0).start()

    @pl.when(nb > 0)
    def _():
        wd_sc[...] = wd_ref[...].astype(BF16)

        def block(j, carry):
            g = g0 + j
            slot = g % 2

            @pl.when(g >= 2)
            def _():
                out_copy(g - 2, slot).wait()

            in_copy(g, slot).wait()

            @pl.when(g + 1 < nu)
            def _():
                in_copy(g + 1, 1 - slot).start()

            obuf[slot] = jnp.dot(xbuf[slot], wd_sc[...], preferred_element_type=F32)
            out_copy(g, slot).start()
            return carry

        lax.fori_loop(0, nb, block, 0)

    @pl.when(e == N_EXPERTS - 1)
    def _():
        _finish_writes(out_copy, obuf, nu)


def _moe_down(blk_start, n_blk, n_used, hmid, w_down, l):
    grid_spec = pltpu.PrefetchScalarGridSpec(
        num_scalar_prefetch=3,
        grid=(N_EXPERTS,),
        in_specs=[
            pl.BlockSpec(memory_space=pl.ANY),
            pl.BlockSpec((None, None, D_EXPERT, D_MODEL), lambda e, *_: (l, e, 0, 0)),
        ],
        out_specs=pl.BlockSpec(memory_space=pl.ANY),
        scratch_shapes=[
            pltpu.VMEM((D_EXPERT, D_MODEL), BF16),
            pltpu.VMEM((2, EXPERT_BLOCK, D_EXPERT), BF16),
            pltpu.VMEM((2, EXPERT_BLOCK, D_MODEL), F32),
            pltpu.SemaphoreType.DMA((2,)),
            pltpu.SemaphoreType.DMA((2,)),
        ],
    )
    return pl.pallas_call(
        _moe_down_kernel,
        grid_spec=grid_spec,
        out_shape=jax.ShapeDtypeStruct((N_ROWS, D_MODEL), F32),
        compiler_params=_cparams(("arbitrary",), vmem=MOE_VMEM_LIMIT),
        name="moe_down",
    )(blk_start, n_blk, n_used, hmid, w_down)


def _combine_kernel(d0_ref, d1_ref, y_hbm, h1_ref, w0_ref, w1_ref, g_ref, b_ref, h2_ref, h2b_ref,
                    buf0, buf1, sem):
    i = pl.program_id(0)
    base = i * LANE

    def issue(r, carry):
        _row_copy(y_hbm, d0_ref[base + r], buf0, r, sem).start(priority=0)
        _row_copy(y_hbm, d1_ref[base + r], buf1, r, sem).start(priority=1)
        return carry

    lax.fori_loop(0, LANE, issue, 0)

    def drain(r, carry):
        _row_copy(y_hbm, 0, buf0, r, sem).wait()
        _row_copy(y_hbm, 0, buf1, r, sem).wait()
        return carry

    lax.fori_loop(0, LANE, drain, 0)
    ffn = w0_ref[...] * buf0[...] + w1_ref[...] * buf1[...]
    y = _post_ln(h1_ref[...], ffn, g_ref[...], b_ref[...], base)
    h2_ref[...] = y
    h2b_ref[...] = y.astype(BF16)


def _combine_ln2(d0, d1, y_rows, h1, w0, w1, g, b):
    row = lambda i, d0, d1: (i, 0)
    vec = pl.BlockSpec((1, D_MODEL), lambda i, d0, d1: (0, 0))
    col = pl.BlockSpec((LANE, 1), row)
    grid_spec = pltpu.PrefetchScalarGridSpec(
        num_scalar_prefetch=2,
        grid=(R // LANE,),
        in_specs=[
            pl.BlockSpec(memory_space=pl.ANY),
            pl.BlockSpec((LANE, D_MODEL), row),
            col, col, vec, vec,
        ],
        out_specs=[pl.BlockSpec((LANE, D_MODEL), row), pl.BlockSpec((LANE, D_MODEL), row)],
        scratch_shapes=[
            pltpu.VMEM((LANE, D_MODEL), F32),
            pltpu.VMEM((LANE, D_MODEL), F32),
            pltpu.SemaphoreType.DMA,
        ],
    )
    return pl.pallas_call(
        _combine_kernel,
        grid_spec=grid_spec,
        out_shape=[
            jax.ShapeDtypeStruct((R, D_MODEL), F32),
            jax.ShapeDtypeStruct((R, D_MODEL), BF16),
        ],
        compiler_params=_cparams(("arbitrary",)),
        name="moe_combine_ln2",
    )(d0, d1, y_rows, h1, w0, w1, g.reshape(1, -1), b.reshape(1, -1))


def kernel(x, meta_tokens, ln_in_g, ln_in_b, w_in, fox_f_bias, gla_wa2, gla_ba, gla_norm_g, conv_w, pool_w, pool_scale, gate_b, w_branch, w_out, ln1_g, ln1_b, router_w, router_b, w_gate, w_up, w_down, ln2_g, ln2_b):
    assert x.shape == (1, SEQ, D_MODEL)
    h, hb = _ln_in(x.reshape(SEQ, D_MODEL), meta_tokens, ln_in_g, ln_in_b)
    router_wt = router_w.T.reshape(N_GROUPS, EXPERTS_PER_GROUP, D_MODEL).transpose(1, 0, 2).reshape(
        N_EXPERTS, D_MODEL)
    router_bc = router_b.astype(F32).reshape(N_GROUPS, EXPERTS_PER_GROUP).T.reshape(N_EXPERTS, 1)

    wb_bf = w_branch.astype(BF16)
    wo_bf = w_out.astype(BF16)

    for l in range(DEPTH):
        w_all = _prep_w_in(w_in, l)
        zf = _matmul(hb, w_all, W_FOX, N_FOX, BF16, TM_PROJ, 768, "proj_fox")
        z = _matmul(hb, w_all, W_MIX, N_MIXC, F32, TM_PROJ, 512, "proj_mix")
        zs = _proj_small(hb, w_in, l)

        bias_row = jnp.zeros((1, LANE), F32).at[0, SM_FF:SM_FF + FOX_HEADS].set(fox_f_bias[l])
        c = _fox_gate(zs, bias_row)[:, SM_FF:SM_FF + FOX_HEADS]
        c_t = c.T
        o_a = _fox_attention(zf, c_t.reshape(FOX_HEADS, R, 1), c_t.reshape(FOX_HEADS, 1, R))

        wa2p = jnp.zeros((LANE, GLA_HEADS * GLA_DK), F32).at[SM_GA:SM_GA + GLA_RANK].set(gla_wa2[l])
        o_b = _gla(z, zs, wa2p, gla_ba[l].reshape(1, -1), gla_norm_g[l].reshape(1, -1))

        o_c, o_d = _local_mixers(z, conv_w[l], pool_w[l].astype(BF16), pool_scale[l])

        mix = _merge(hb, o_a, o_b, o_c, o_d, w_all, gate_b[l], wb_bf, wo_bf, l)
        h1, mi, mf, counts = _ln1_route(h, mix, ln1_g[l], ln1_b[l], router_wt, router_bc)

        d0, d1, blk_start, n_blk, n_used, row_src = _dispatch_tables(mi, counts[:, 0])
        hmid = _moe_up(blk_start, n_blk, n_used, row_src, h1, w_gate, w_up, l)
        y_rows = _moe_down(blk_start, n_blk, n_used, hmid, w_down, l)
        h, hb = _combine_ln2(d0, d1, y_rows, h1, mf[0].reshape(R, 1), mf[1].reshape(R, 1),
                             ln2_g[l], ln2_b[l])

    return h[PAD_ROWS + N_META:].reshape(1, SEQ, D_MODEL)
```

```python
import jax
import jax.numpy as jnp
import numpy as np
from jax import lax
from jax.experimental import pallas as pl
from jax.experimental.pallas import tpu as pltpu

F32 = jnp.float32
BF16 = jnp.bfloat16
I32 = jnp.int32
HIGHEST = lax.Precision.HIGHEST

D_MODEL = 2048
SEQ = 8192
DEPTH = 2
N_META = 16
N_BRANCH = 4
MIX_W = 512
FOX_HEADS = 4
FOX_HD = 128
GLA_HEADS = 4
GLA_DK = 64
GLA_DV = 128
GLA_RANK = 16
GLA_TAU = 16.0
CONV_K = 3
POOL_WINDOWS = (2, 4, 8, 16)
POOL_GW = 128
N_EXPERTS = 32
N_GROUPS = 8
EXPERTS_PER_GROUP = 4
TOP_K = 2
D_EXPERT = 1024
LN_EPS = 1e-5
DEEPNORM_ALPHA = (2 * DEPTH) ** 0.25

_SPLITS = (512, 512, 512, 4, 256, 256, 512, 16, 512, 512, 512, 512, 512, 8192)
_OFFS = [int(o) for o in np.concatenate([[0], np.cumsum(_SPLITS)])]
(O_FQ, O_FK, O_FV, O_FF, O_GQ, O_GK, O_GV, O_GA, O_GR, O_CB, O_CC, O_CV, O_PZ, O_GZ, P_IN) = _OFFS

LANE = 128
PAD_ROWS = LANE - N_META
T0 = PAD_ROWS
N_TOK = N_META + SEQ
R = PAD_ROWS + N_TOK
TM = 640
TM_PROJ = 1664
HALO = 16

WT = 512
W_FOX, W_MIX, W_GATES = 0, 1536, 5120
N_FOX, N_MIXC, N_GATES = 1536, 3584, 8192
N_WALL = W_GATES + N_GATES
SHIFT_G = O_GQ - W_FOX - N_FOX
SHIFT_M = O_GR - (W_MIX + 1024)
C_GQ, C_GK, C_GV, C_GR, C_CB, C_CC, C_CV, C_PZ = 0, 256, 512, 1024, 1536, 2048, 2560, 3072
SM_FF_TILE, SM_GA_TILE = O_FF // LANE, O_GA // LANE
SM_FF = O_FF - SM_FF_TILE * LANE
SM_GA = O_GA - SM_GA_TILE * LANE

EXPERT_BLOCK = 128
N_FLAT = N_TOK * TOP_K
N_BLOCKS = -(-N_FLAT // EXPERT_BLOCK) + N_EXPERTS
N_ROWS = N_BLOCKS * EXPERT_BLOCK

NEG = -1e30
VMEM_LIMIT = 48 * 1024 * 1024
MOE_VMEM_LIMIT = 56 * 1024 * 1024


def _cparams(sem, vmem=VMEM_LIMIT):
    return pltpu.CompilerParams(dimension_semantics=sem, vmem_limit_bytes=vmem)


def _log_sigmoid(x):
    return jnp.minimum(x, 0.0) - jnp.log1p(jnp.exp(-jnp.abs(x)))


def _sigmoid(x):
    return 1.0 / (1.0 + jnp.exp(-x))


def _layer_norm_rows(x, g, b):
    mu = jnp.mean(x, axis=-1, keepdims=True)
    xc = x - mu
    var = jnp.mean(xc * xc, axis=-1, keepdims=True)
    return xc * lax.rsqrt(var + LN_EPS) * g + b


def _ln_in_kernel(x_ref, meta_ref, g_ref, b_ref, h_ref, hb_ref):
    i = pl.program_id(0)

    @pl.when(i == 0)
    def _():
        h_ref[...] = jnp.zeros_like(h_ref)
        hb_ref[...] = jnp.zeros_like(hb_ref)
        m = _layer_norm_rows(meta_ref[...], g_ref[...], b_ref[...])
        h_ref[PAD_ROWS:, :] = m
        hb_ref[PAD_ROWS:, :] = m.astype(BF16)

    @pl.when(i > 0)
    def _():
        y = _layer_norm_rows(x_ref[...], g_ref[...], b_ref[...])
        h_ref[...] = y
        hb_ref[...] = y.astype(BF16)


def _ln_in(x2d, meta, g, b):
    nb = R // LANE
    return pl.pallas_call(
        _ln_in_kernel,
        grid=(nb,),
        in_specs=[
            pl.BlockSpec((LANE, D_MODEL), lambda i: (jnp.maximum(i - 1, 0), 0)),
            pl.BlockSpec((N_META, D_MODEL), lambda i: (0, 0)),
            pl.BlockSpec((1, D_MODEL), lambda i: (0, 0)),
            pl.BlockSpec((1, D_MODEL), lambda i: (0, 0)),
        ],
        out_specs=[
            pl.BlockSpec((LANE, D_MODEL), lambda i: (i, 0)),
            pl.BlockSpec((LANE, D_MODEL), lambda i: (i, 0)),
        ],
        out_shape=[
            jax.ShapeDtypeStruct((R, D_MODEL), F32),
            jax.ShapeDtypeStruct((R, D_MODEL), BF16),
        ],
        compiler_params=_cparams(("arbitrary",)),
        name="ln_in",
    )(x2d, meta, g.reshape(1, -1), b.reshape(1, -1))


TR_PREP = 1024


def _wprep_kernel(a_ref, b_ref, o_ref):
    j = pl.program_id(1)

    def emit(shift):
        if shift == 0:
            o_ref[...] = a_ref[...].astype(BF16)
        else:
            x = jnp.concatenate([a_ref[...], b_ref[...]], axis=1)
            o_ref[...] = pltpu.roll(x, WT + LANE - shift, axis=1)[:, :WT].astype(BF16)

    first_g = N_FOX // WT
    first_m = (W_MIX + 1024) // WT

    @pl.when(j < first_g)
    def _():
        emit(0)

    @pl.when((j >= first_g) & (j < first_m))
    def _():
        emit(SHIFT_G)

    @pl.when(j >= first_m)
    def _():
        emit(SHIFT_M)


def _prep_w_in(w_in, l):
    return pl.pallas_call(
        _wprep_kernel,
        grid=(D_MODEL // TR_PREP, N_WALL // WT),
        in_specs=[
            pl.BlockSpec((None, TR_PREP, WT), lambda i, j: (l, i, j)),
            pl.BlockSpec((None, TR_PREP, LANE), lambda i, j: (l, i, (j + 1) * (WT // LANE))),
        ],
        out_specs=pl.BlockSpec((TR_PREP, WT), lambda i, j: (i, j)),
        out_shape=jax.ShapeDtypeStruct((D_MODEL, N_WALL), BF16),
        compiler_params=_cparams(("parallel", "arbitrary")),
        name="prep_w_in",
    )(w_in, w_in)


def _mm_kernel(a_ref, w_ref, o_ref):
    o_ref[...] = jnp.dot(a_ref[...], w_ref[...], preferred_element_type=F32).astype(o_ref.dtype)


def _matmul(a, w, col0, n, out_dtype, tm, tn, name):
    m, k = a.shape
    return pl.pallas_call(
        _mm_kernel,
        grid=(m // tm, n // tn),
        in_specs=[
            pl.BlockSpec((tm, k), lambda i, j: (i, 0)),
            pl.BlockSpec((k, tn), lambda i, j: (0, col0 // tn + j)),
        ],
        out_specs=pl.BlockSpec((tm, tn), lambda i, j: (i, j)),
        out_shape=jax.ShapeDtypeStruct((m, n), out_dtype),
        compiler_params=_cparams(("parallel", "arbitrary")),
        name=name,
    )(a, w)


def _proj_small_kernel(a_ref, w_ref, o_ref):
    o_ref[...] = jnp.dot(a_ref[...], w_ref[...].astype(BF16), preferred_element_type=F32)


def _proj_small(hb, w_in, l):
    return pl.pallas_call(
        _proj_small_kernel,
        grid=(R // TM_PROJ, 2),
        in_specs=[
            pl.BlockSpec((TM_PROJ, D_MODEL), lambda i, j: (i, 0)),
            pl.BlockSpec((None, D_MODEL, LANE),
                         lambda i, j: (l, 0, SM_FF_TILE + j * (SM_GA_TILE - SM_FF_TILE))),
        ],
        out_specs=pl.BlockSpec((TM_PROJ, LANE), lambda i, j: (i, j)),
        out_shape=jax.ShapeDtypeStruct((R, 2 * LANE), F32),
        compiler_params=_cparams(("parallel", "arbitrary")),
        name="proj_small",
    )(hb, w_in)


def _fox_gate_kernel(zs_ref, bias_ref, c_ref, carry_ref):
    i = pl.program_id(0)

    @pl.when(i == 0)
    def _():
        carry_ref[...] = jnp.zeros_like(carry_ref)

    rows = i * TM + lax.broadcasted_iota(I32, (TM, LANE), 0)
    lf = _log_sigmoid(zs_ref[...] + bias_ref[...])
    lf = jnp.where(rows >= T0, lf, 0.0)
    tri = (lax.broadcasted_iota(I32, (TM, TM), 0)
           >= lax.broadcasted_iota(I32, (TM, TM), 1)).astype(F32)
    c = jnp.dot(tri, lf, precision=HIGHEST, preferred_element_type=F32) + carry_ref[...]
    c_ref[...] = c
    carry_ref[...] = c[TM - 1:TM, :]


def _fox_gate(zs, bias_row):
    return pl.pallas_call(
        _fox_gate_kernel,
        grid=(R // TM,),
        in_specs=[
            pl.BlockSpec((TM, LANE), lambda i: (i, 0)),
            pl.BlockSpec((1, LANE), lambda i: (0, 0)),
        ],
        out_specs=pl.BlockSpec((TM, LANE), lambda i: (i, 0)),
        out_shape=jax.ShapeDtypeStruct((R, LANE), F32),
        scratch_shapes=[pltpu.VMEM((1, LANE), F32)],
        compiler_params=_cparams(("arbitrary",)),
        name="fox_gate",
    )(zs, bias_row)


TQ = TM
N_QB = R // TQ
_PAIRS = [(qi, kj) for qi in range(N_QB) for kj in range(qi + 1)]
N_PAIRS = len(_PAIRS)


def _fox_kernel(qi_tab, kj_tab, q_ref, k_ref, v_ref, cq_ref, ck_ref, o_ref, m_sc, l_sc, acc_sc):
    p = pl.program_id(1)
    qi = qi_tab[p]
    kj = kj_tab[p]

    @pl.when(kj == 0)
    def _():
        m_sc[...] = jnp.full_like(m_sc, NEG)
        l_sc[...] = jnp.zeros_like(l_sc)
        acc_sc[...] = jnp.zeros_like(acc_sc)

    def step(masked):
        s = lax.dot_general(q_ref[...], k_ref[...], (((1,), (1,)), ((), ())),
                            preferred_element_type=F32)
        s = s * (FOX_HD ** -0.5) + cq_ref[...] - ck_ref[...]
        if masked:
            qpos = qi * TQ + lax.broadcasted_iota(I32, (TQ, TQ), 0)
            kpos = kj * TQ + lax.broadcasted_iota(I32, (TQ, TQ), 1)
            ok = (kpos <= qpos) & ((kpos >= T0) | (qpos < T0))
            s = jnp.where(ok, s, NEG)
        m_prev = m_sc[...]
        m_new = jnp.maximum(m_prev, jnp.max(s, axis=-1, keepdims=True))
        alpha = jnp.exp(m_prev - m_new)
        pr = jnp.exp(s - m_new)
        l_sc[...] = alpha * l_sc[...] + jnp.sum(pr, axis=-1, keepdims=True)
        acc_sc[...] = alpha * acc_sc[...] + jnp.dot(pr.astype(BF16), v_ref[...],
                                                    preferred_element_type=F32)
        m_sc[...] = m_new

    needs_mask = (kj == qi) | (kj == 0)

    @pl.when(needs_mask)
    def _():
        step(True)

    @pl.when(jnp.logical_not(needs_mask))
    def _():
        step(False)

    @pl.when(kj == qi)
    def _():
        o_ref[...] = (acc_sc[...] / l_sc[...]).astype(o_ref.dtype)


def _fox_attention(zf, c_col, c_row):
    qi_tab = jnp.asarray([p[0] for p in _PAIRS], I32)
    kj_tab = jnp.asarray([p[1] for p in _PAIRS], I32)
    grid_spec = pltpu.PrefetchScalarGridSpec(
        num_scalar_prefetch=2,
        grid=(FOX_HEADS, N_PAIRS),
        in_specs=[
            pl.BlockSpec((TQ, FOX_HD), lambda h, p, qt, kt: (qt[p], h)),
            pl.BlockSpec((TQ, FOX_HD), lambda h, p, qt, kt: (kt[p], FOX_HEADS + h)),
            pl.BlockSpec((TQ, FOX_HD), lambda h, p, qt, kt: (kt[p], 2 * FOX_HEADS + h)),
            pl.BlockSpec((None, TQ, 1), lambda h, p, qt, kt: (h, qt[p], 0)),
            pl.BlockSpec((None, 1, TQ), lambda h, p, qt, kt: (h, 0, kt[p])),
        ],
        out_specs=pl.BlockSpec((TQ, FOX_HD), lambda h, p, qt, kt: (qt[p], h)),
        scratch_shapes=[
            pltpu.VMEM((TQ, 1), F32),
            pltpu.VMEM((TQ, 1), F32),
            pltpu.VMEM((TQ, FOX_HD), F32),
        ],
    )
    return pl.pallas_call(
        _fox_kernel,
        grid_spec=grid_spec,
        out_shape=jax.ShapeDtypeStruct((R, MIX_W), BF16),
        compiler_params=_cparams(("parallel", "arbitrary")),
        name="fox_attention",
    )(qi_tab, kj_tab, zf, zf, zf, c_col, c_row)


GLA_CHUNK = 64


def _gla_kernel(q_ref, k_ref, v_ref, gr_ref, zs_ref, wa2_ref, ba_ref, gn_ref, o_ref, st_ref, la_ref):
    i = pl.program_id(0)

    @pl.when(i == 0)
    def _():
        st_ref[...] = jnp.zeros_like(st_ref)

    la = jnp.dot(zs_ref[...], wa2_ref[...], precision=HIGHEST, preferred_element_type=F32)
    la_ref[...] = _log_sigmoid(la + ba_ref[...]) * (1.0 / GLA_TAU)

    c_r = lax.broadcasted_iota(I32, (GLA_CHUNK, GLA_CHUNK), 0)
    c_c = lax.broadcasted_iota(I32, (GLA_CHUNK, GLA_CHUNK), 1)
    tri_b = c_r >= c_c
    tri = tri_b.astype(F32)

    def chunk(c, carry):
        r0 = pl.multiple_of(c * GLA_CHUNK, GLA_CHUNK)
        rows = pl.ds(r0, GLA_CHUNK)
        g = la_ref[rows, :]
        b = jnp.dot(tri, g, precision=HIGHEST, preferred_element_type=F32)
        b_last = b[GLA_CHUNK - 1:GLA_CHUNK, :]
        e_last = jnp.exp(b_last)
        qt = q_ref[rows, :] * (GLA_DK ** -0.5) * jnp.exp(b)
        kt = k_ref[rows, :] * jnp.exp(-b)
        kh = kt * e_last
        for h in range(GLA_HEADS):
            ks = slice(h * GLA_DK, (h + 1) * GLA_DK)
            vs = slice(h * GLA_DV, (h + 1) * GLA_DV)
            q_h = qt[:, ks].astype(BF16)
            k_h = kt[:, ks].astype(BF16)
            kh_h = kh[:, ks].astype(BF16)
            v_h = v_ref[rows, vs]
            att = lax.dot_general(q_h, k_h, (((1,), (1,)), ((), ())), preferred_element_type=F32)
            att = jnp.where(tri_b, att, 0.0)
            st = st_ref[h]
            o = jnp.dot(att.astype(BF16), v_h.astype(BF16), preferred_element_type=F32)
            o = o + lax.dot_general(q_h, st.astype(BF16), (((1,), (1,)), ((), ())),
                                    preferred_element_type=F32)
            st_ref[h] = st * e_last[:, ks] + jnp.dot(v_h.T.astype(BF16), kh_h,
                                                     preferred_element_type=F32)
            ms = jnp.mean(o * o, axis=-1, keepdims=True)
            on = o * lax.rsqrt(ms + LN_EPS) * gn_ref[:, vs]
            gate = gr_ref[rows, vs]
            o_ref[rows, vs] = (on * (gate * _sigmoid(gate))).astype(o_ref.dtype)
        return carry

    lax.fori_loop(0, TM // GLA_CHUNK, chunk, 0)


def _gla(z, zs, wa2p, ba, gn):
    return pl.pallas_call(
        _gla_kernel,
        grid=(R // TM,),
        in_specs=[
            pl.BlockSpec((TM, 256), lambda i: (i, C_GQ // 256)),
            pl.BlockSpec((TM, 256), lambda i: (i, C_GK // 256)),
            pl.BlockSpec((TM, 512), lambda i: (i, C_GV // 512)),
            pl.BlockSpec((TM, 512), lambda i: (i, C_GR // 512)),
            pl.BlockSpec((TM, LANE), lambda i: (i, 1)),
            pl.BlockSpec((LANE, 256), lambda i: (0, 0)),
            pl.BlockSpec((1, 256), lambda i: (0, 0)),
            pl.BlockSpec((1, 512), lambda i: (0, 0)),
        ],
        out_specs=pl.BlockSpec((TM, MIX_W), lambda i: (i, 0)),
        out_shape=jax.ShapeDtypeStruct((R, MIX_W), BF16),
        scratch_shapes=[
            pltpu.VMEM((GLA_HEADS, GLA_DV, GLA_DK), F32),
            pltpu.VMEM((TM, GLA_HEADS * GLA_DK), F32),
        ],
        compiler_params=_cparams(("arbitrary",)),
        name="gla",
    )(z, z, z, z, zs, wa2p, ba, gn)


def _local_kernel(cb_ref, cc_ref, cv_ref, pz_ref, cw_ref, pw_ref, ps_ref, oc_ref, od_ref, u_sc, p_sc):
    i = pl.program_id(0)

    @pl.when(i == 0)
    def _():
        u_sc[0:HALO, :] = jnp.zeros((HALO, MIX_W), F32)
        p_sc[0:HALO, :] = jnp.zeros((HALO, MIX_W), F32)

    @pl.when(i > 0)
    def _():
        u_sc[0:HALO, :] = u_sc[TM:TM + HALO, :]
        p_sc[0:HALO, :] = p_sc[TM:TM + HALO, :]

    u = cc_ref[...] * cv_ref[...]
    pz = pz_ref[...]
    u_sc[HALO:, :] = u
    p_sc[HALO:, :] = pz

    y = (cw_ref[2:3, :] * u + cw_ref[1:2, :] * u_sc[HALO - 1:HALO - 1 + TM, :]
         + cw_ref[0:1, :] * u_sc[HALO - 2:HALO - 2 + TM, :])
    oc_ref[...] = (cb_ref[...] * y).astype(oc_ref.dtype)

    tok = i * TM - T0 + lax.broadcasted_iota(I32, (TM, 1), 0)
    cnt_small = jnp.maximum(tok + 1, 1).astype(F32)
    for g, w in enumerate(POOL_WINDOWS):
        cols = slice(g * POOL_GW, (g + 1) * POOL_GW)
        x = pz[:, cols]
        s = x
        for j in range(1, w):
            s = s + p_sc[HALO - j:HALO - j + TM, cols]
        inv_cnt = jnp.where(tok + 1 >= w, 1.0 / w, 1.0 / cnt_small)
        pooled = s * inv_cnt - x
        od = jnp.dot(pooled.astype(BF16), pw_ref[g], preferred_element_type=F32)
        od_ref[:, cols] = (od * ps_ref[:, cols]).astype(od_ref.dtype)


def _local_mixers(z, conv_w, pool_w_bf, pool_scale):
    cw = jnp.zeros((8, MIX_W), F32).at[:CONV_K].set(conv_w)
    blk = lambda c: pl.BlockSpec((TM, MIX_W), lambda i, c=c: (i, c // MIX_W))
    return pl.pallas_call(
        _local_kernel,
        grid=(R // TM,),
        in_specs=[
            blk(C_CB), blk(C_CC), blk(C_CV), blk(C_PZ),
            pl.BlockSpec((8, MIX_W), lambda i: (0, 0)),
            pl.BlockSpec((len(POOL_WINDOWS), POOL_GW, POOL_GW), lambda i: (0, 0, 0)),
            pl.BlockSpec((1, MIX_W), lambda i: (0, 0)),
        ],
        out_specs=[
            pl.BlockSpec((TM, MIX_W), lambda i: (i, 0)),
            pl.BlockSpec((TM, MIX_W), lambda i: (i, 0)),
        ],
        out_shape=[
            jax.ShapeDtypeStruct((R, MIX_W), BF16),
            jax.ShapeDtypeStruct((R, MIX_W), BF16),
        ],
        scratch_shapes=[
            pltpu.VMEM((TM + HALO, MIX_W), F32),
            pltpu.VMEM((TM + HALO, MIX_W), F32),
        ],
        compiler_params=_cparams(("arbitrary",)),
        name="conv_pool",
    )(z, z, z, z, cw, pool_w_bf, pool_scale.reshape(1, -1))


TN_MERGE = 256


def _merge_kernel(hb_ref, oa_ref, ob_ref, oc_ref, od_ref, wg0_ref, wg1_ref, wg2_ref, wg3_ref,
                  gb_ref, wb_ref, wo_ref, out_ref):
    j = pl.program_id(1)

    @pl.when(j == 0)
    def _():
        out_ref[...] = jnp.zeros_like(out_ref)

    hb = hb_ref[...]
    mixed = None
    for b, (o_ref, wg_ref) in enumerate(((oa_ref, wg0_ref), (ob_ref, wg1_ref),
                                         (oc_ref, wg2_ref), (od_ref, wg3_ref))):
        gate = _sigmoid(jnp.dot(hb, wg_ref[...], preferred_element_type=F32) + gb_ref[b:b + 1, :])
        proj = jnp.dot(o_ref[...], wb_ref[b], preferred_element_type=F32)
        term = gate * proj
        mixed = term if mixed is None else mixed + term
    out_ref[...] += jnp.dot(mixed.astype(BF16), wo_ref[...], preferred_element_type=F32)


def _merge(hb, o_a, o_b, o_c, o_d, w_all, gate_b, wb_bf, wo_bf, l):
    tn = TN_MERGE
    nj = D_MODEL // tn
    row = lambda w: pl.BlockSpec((TM, w), lambda i, j: (i, 0))
    wg = lambda b: pl.BlockSpec((D_MODEL, tn), lambda i, j, b=b: (0, W_GATES // tn + b * nj + j))
    return pl.pallas_call(
        _merge_kernel,
        grid=(R // TM, nj),
        in_specs=[
            row(D_MODEL), row(MIX_W), row(MIX_W), row(MIX_W), row(MIX_W),
            wg(0), wg(1), wg(2), wg(3),
            pl.BlockSpec((N_BRANCH, tn), lambda i, j: (0, j)),
            pl.BlockSpec((None, N_BRANCH, MIX_W, tn), lambda i, j: (l, 0, 0, j)),
            pl.BlockSpec((None, tn, D_MODEL), lambda i, j: (l, j, 0)),
        ],
        out_specs=pl.BlockSpec((TM, D_MODEL), lambda i, j: (i, 0)),
        out_shape=jax.ShapeDtypeStruct((R, D_MODEL), F32),
        compiler_params=_cparams(("parallel", "arbitrary")),
        name="merge",
    )(hb, o_a, o_b, o_c, o_d, w_all, w_all, w_all, w_all, gate_b, wb_bf, wo_bf)


TOK_ROWS = D_MODEL // LANE
TOK_PITCH = 20


def _store_token_major(ref, x):
    n = x.shape[0]
    for s in range(TOK_ROWS):
        ref[pl.ds(s, n, stride=TOK_ROWS), :] = x[:, s * LANE:(s + 1) * LANE]


def _load_token_major(ref, n, pitch):
    return jnp.concatenate(
        [ref[pl.ds(s, n, stride=pitch), :] for s in range(TOK_ROWS)], axis=1)


def _post_ln(h, delta, g, b, row0):
    y = _layer_norm_rows(DEEPNORM_ALPHA * h + delta, g, b)
    rows = row0 + lax.broadcasted_iota(I32, (y.shape[0], 1), 0)
    return jnp.where(rows >= T0, y, 0.0)


def _first_of(cands, target):
    idx = jnp.full(target.shape, len(cands) - 1, I32)
    for j in range(len(cands) - 2, -1, -1):
        idx = jnp.where(cands[j] == target, j, idx)
    return idx


def _pick(cands, idx):
    out = cands[-1]
    for j in range(len(cands) - 2, -1, -1):
        out = jnp.where(idx == j, cands[j], out)
    return out


def _ln1_route_kernel(h_ref, mix_ref, g_ref, b_ref, rwt_ref, rb_ref,
                      h1_ref, h1t_ref, mi_ref, mf_ref, cnt_ref, carry_sc):
    i = pl.program_id(0)
    TM = h_ref.shape[0]

    @pl.when(i == 0)
    def _():
        carry_sc[...] = jnp.zeros_like(carry_sc)

    y = _post_ln(h_ref[...], mix_ref[...], g_ref[...], b_ref[...], i * TM)
    h1_ref[...] = y
    _store_token_major(h1t_ref, y)

    logits = lax.dot_general(rwt_ref[...], y, (((1,), (1,)), ((), ())), precision=HIGHEST,
                             preferred_element_type=F32)
    aff = _sigmoid(logits)
    sel = aff + rb_ref[...]
    xs = [sel[j * N_GROUPS:(j + 1) * N_GROUPS, :] for j in range(EXPERTS_PER_GROUP)]
    afs = [aff[j * N_GROUPS:(j + 1) * N_GROUPS, :] for j in range(EXPERTS_PER_GROUP)]

    score = None
    for a in range(EXPERTS_PER_GROUP):
        for bb in range(a + 1, EXPERTS_PER_GROUP):
            pair = xs[a] + xs[bb]
            score = pair if score is None else jnp.maximum(score, pair)
    giota = lax.broadcasted_iota(I32, (N_GROUPS, TM), 0)
    gmax = jnp.max(score, axis=0, keepdims=True)
    grp = jnp.min(jnp.where(score == gmax, giota, N_GROUPS), axis=0, keepdims=True)
    gsel = giota == grp
    cs = [jnp.max(jnp.where(gsel, x, -jnp.inf), axis=0, keepdims=True) for x in xs]
    acs = [jnp.sum(jnp.where(gsel, a, 0.0), axis=0, keepdims=True) for a in afs]

    m1 = jnp.maximum(jnp.maximum(cs[0], cs[1]), jnp.maximum(cs[2], cs[3]))
    i0 = _first_of(cs, m1)
    ds = [jnp.where(i0 == j, -jnp.inf, cs[j]) for j in range(EXPERTS_PER_GROUP)]
    m2 = jnp.maximum(jnp.maximum(ds[0], ds[1]), jnp.maximum(ds[2], ds[3]))
    i1 = _first_of(ds, m2)
    a0 = _pick(acs, i0)
    a1 = _pick(acs, i1)
    denom = a0 + a1

    pos = i * TM + lax.broadcasted_iota(I32, (1, TM), 1)
    valid = pos >= T0
    riota = lax.broadcasted_iota(I32, (N_EXPERTS, TM), 0)
    oh0 = (riota == i0 * N_GROUPS + grp) & valid
    oh1 = (riota == i1 * N_GROUPS + grp) & valid
    ohf = jnp.where(oh0 | oh1, 1.0, 0.0)
    before = (lax.broadcasted_iota(I32, (TM, TM), 0)
              < lax.broadcasted_iota(I32, (TM, TM), 1)).astype(BF16)
    cum = jnp.dot(ohf.astype(BF16), before, preferred_element_type=F32) + carry_sc[...]
    rank0 = jnp.sum(jnp.where(oh0, cum, 0.0), axis=0, keepdims=True)
    rank1 = jnp.sum(jnp.where(oh1, cum, 0.0), axis=0, keepdims=True)
    carry = carry_sc[...] + jnp.sum(ohf, axis=1, keepdims=True)
    carry_sc[...] = carry
    cnt_ref[...] = jnp.broadcast_to(carry, cnt_ref.shape)

    zi = jnp.zeros((1, TM), I32)
    mi_ref[...] = jnp.concatenate(
        [grp * EXPERTS_PER_GROUP + i0, grp * EXPERTS_PER_GROUP + i1,
         rank0.astype(I32), rank1.astype(I32), zi, zi, zi, zi], axis=0)
    zf = jnp.zeros((1, TM), F32)
    mf_ref[...] = jnp.concatenate([a0 / denom, a1 / denom, zf, zf, zf, zf, zf, zf], axis=0)


def _ln1_route(h, mix, g, b, router_wt, router_bc):
    tm = LANE
    row = pl.BlockSpec((tm, D_MODEL), lambda i: (i, 0))
    vec = pl.BlockSpec((1, D_MODEL), lambda i: (0, 0))
    meta = pl.BlockSpec((8, tm), lambda i: (0, i))
    return pl.pallas_call(
        _ln1_route_kernel,
        grid=(R // tm,),
        in_specs=[row, row, vec, vec,
                  pl.BlockSpec((N_EXPERTS, D_MODEL), lambda i: (0, 0)),
                  pl.BlockSpec((N_EXPERTS, 1), lambda i: (0, 0))],
        out_specs=[row, pl.BlockSpec((tm * TOK_ROWS, LANE), lambda i: (i, 0)), meta, meta,
                   pl.BlockSpec((N_EXPERTS, LANE), lambda i: (0, 0))],
        out_shape=[
            jax.ShapeDtypeStruct((R, D_MODEL), F32),
            jax.ShapeDtypeStruct((R * TOK_ROWS, LANE), F32),
            jax.ShapeDtypeStruct((8, R), I32),
            jax.ShapeDtypeStruct((8, R), F32),
            jax.ShapeDtypeStruct((N_EXPERTS, LANE), F32),
        ],
        scratch_shapes=[pltpu.VMEM((N_EXPERTS, 1), F32)],
        compiler_params=_cparams(("arbitrary",)),
        name="ln1_route",
    )(h, mix, g.reshape(1, -1), b.reshape(1, -1), router_wt, router_bc)


def _dispatch_tables(mi, counts_slot_major):
    counts = counts_slot_major.reshape(EXPERTS_PER_GROUP, N_GROUPS).T.reshape(N_EXPERTS).astype(I32)
    padded = (counts + EXPERT_BLOCK - 1) // EXPERT_BLOCK * EXPERT_BLOCK
    pad_end = jnp.cumsum(padded)
    pad_start = pad_end - padded
    e_iota = jnp.arange(N_EXPERTS, dtype=I32)
    rows_ok = jnp.arange(R) >= T0

    def dest(eid, rank):
        start = jnp.sum(jnp.where(eid[:, None] == e_iota[None, :], pad_start[None, :], 0), axis=1)
        return jnp.where(rows_ok, start + rank, 0).astype(I32)

    d0 = dest(mi[0], mi[2])
    d1 = dest(mi[1], mi[3])
    blk_start = (pad_start // EXPERT_BLOCK).astype(I32)
    n_blk = (padded // EXPERT_BLOCK).astype(I32)
    n_used = (pad_end[-1] // EXPERT_BLOCK).astype(I32).reshape(1)
    tok_rows = jnp.arange(T0, R, dtype=I32) * TOK_ROWS
    row_src = jnp.zeros((N_ROWS + EXPERT_BLOCK,), I32).at[jnp.concatenate([d0[T0:], d1[T0:]])].set(
        jnp.concatenate([tok_rows, tok_rows]), unique_indices=True)
    return d0 * TOK_ROWS, d1 * TOK_ROWS, blk_start, n_blk, n_used, row_src


def _row_copy(src, src_row0, dst, r, sem):
    return pltpu.make_async_copy(src.at[pl.ds(pl.multiple_of(src_row0, TOK_ROWS), TOK_ROWS), :],
                                 dst.at[pl.ds(r * TOK_PITCH, TOK_ROWS), :], sem)


def _issue_row_gather(rs_ref, g, h_hbm, buf, sem):
    base = g * EXPERT_BLOCK
    for r in range(EXPERT_BLOCK):
        _row_copy(h_hbm, rs_ref[base + r], buf, r, sem).start(priority=1)


def _wait_row_gather(h_hbm, buf, sem):
    for r in range(EXPERT_BLOCK):
        _row_copy(h_hbm, 0, buf, r, sem).wait()


def _block_rows(g, rows_per_token=1):
    n = EXPERT_BLOCK * rows_per_token
    return pl.ds(pl.multiple_of(g * n, n), n)


def _finish_writes(out_copy, obuf, nu):
    @pl.when(nu >= 2)
    def _():
        out_copy(nu - 2, nu % 2).wait()

    @pl.when(nu >= 1)
    def _():
        out_copy(nu - 1, (nu - 1) % 2).wait()

    obuf[0] = jnp.zeros(obuf.shape[1:], obuf.dtype)

    def zero_block(g, carry):
        cp = out_copy(g, 0)
        cp.start()
        cp.wait()
        return carry

    lax.fori_loop(nu, N_BLOCKS, zero_block, 0)


def _moe_up_kernel(bs_ref, nb_ref, nu_ref, rs_ref, h_hbm, wg_ref, wu_ref, o_hbm,
                   wg_sc, wu_sc, xbuf, obuf, xsem, osem):
    e = pl.program_id(0)
    nb = nb_ref[e]
    g0 = bs_ref[e]

    def out_copy(g, slot):
        return pltpu.make_async_copy(obuf.at[slot], o_hbm.at[_block_rows(g), :], osem.at[slot])

    @pl.when(e == 0)
    def _():
        _issue_row_gather(rs_ref, 0, h_hbm, xbuf.at[0], xsem.at[0])

    @pl.when(nb > 0)
    def _():
        wg_sc[...] = wg_ref[...].astype(BF16)
        wu_sc[...] = wu_ref[...].astype(BF16)

        def block(j, carry):
            g = g0 + j
            slot = g % 2

            @pl.when(g >= 2)
            def _():
                out_copy(g - 2, slot).wait()

            _wait_row_gather(h_hbm, xbuf.at[slot], xsem.at[slot])
            _issue_row_gather(rs_ref, g + 1, h_hbm, xbuf.at[1 - slot], xsem.at[1 - slot])
            x = _load_token_major(xbuf.at[slot], EXPERT_BLOCK, TOK_PITCH).astype(BF16)
            gate = jnp.dot(x, wg_sc[...], preferred_element_type=F32)
            up = jnp.dot(x, wu_sc[...], preferred_element_type=F32)
            obuf[slot] = (gate * _sigmoid(gate) * up).astype(BF16)
            out_copy(g, slot).start()
            return carry

        lax.fori_loop(0, nb, block, 0)

    @pl.when(e == N_EXPERTS - 1)
    def _():
        nu = nu_ref[0]
        _wait_row_gather(h_hbm, xbuf.at[nu % 2], xsem.at[nu % 2])
        _finish_writes(out_copy, obuf, nu)


def _moe_up(blk_start, n_blk, n_used, row_src, h1, w_gate, w_up, l):
    wspec = pl.BlockSpec((None, None, D_MODEL, D_EXPERT), lambda e, *_: (l, e, 0, 0))
    grid_spec = pltpu.PrefetchScalarGridSpec(
        num_scalar_prefetch=4,
        grid=(N_EXPERTS,),
        in_specs=[pl.BlockSpec(memory_space=pl.ANY), wspec, wspec],
        out_specs=pl.BlockSpec(memory_space=pl.ANY),
        scratch_shapes=[
            pltpu.VMEM((D_MODEL, D_EXPERT), BF16),
            pltpu.VMEM((D_MODEL, D_EXPERT), BF16),
            pltpu.VMEM((2, EXPERT_BLOCK * TOK_PITCH, LANE), F32),
            pltpu.VMEM((2, EXPERT_BLOCK, D_EXPERT), BF16),
            pltpu.SemaphoreType.DMA((2,)),
            pltpu.SemaphoreType.DMA((2,)),
        ],
    )
    return pl.pallas_call(
        _moe_up_kernel,
        grid_spec=grid_spec,
        out_shape=jax.ShapeDtypeStruct((N_ROWS, D_EXPERT), BF16),
        compiler_params=_cparams(("arbitrary",), vmem=MOE_VMEM_LIMIT),
        name="moe_up",
    )(blk_start, n_blk, n_used, row_src, h1, w_gate, w_up)


def _moe_down_kernel(bs_ref, nb_ref, nu_ref, x_hbm, wd_ref, y_hbm, wd_sc, xbuf, obuf, xsem, osem):
    e = pl.program_id(0)
    nb = nb_ref[e]
    g0 = bs_ref[e]
    nu = nu_ref[0]

    def in_copy(g, slot):
        return pltpu.make_async_copy(x_hbm.at[_block_rows(g), :], xbuf.at[slot], xsem.at[slot])

    def out_copy(g, slot):
        return pltpu.make_async_copy(obuf.at[slot], y_hbm.at[_block_rows(g, TOK_ROWS), :],
                                     osem.at[slot])

    @pl.when((e == 0) & (nu > 0))
    def _():
        in_copy(0, 0).start()

    @pl.when(nb > 0)
    def _():
        wd_sc[...] = wd_ref[...].astype(BF16)

        def block(j, carry):
            g = g0 + j
            slot = g % 2

            @pl.when(g >= 2)
            def _():
                out_copy(g - 2, slot).wait()

            in_copy(g, slot).wait()

            @pl.when(g + 1 < nu)
            def _():
                in_copy(g + 1, 1 - slot).start()

            y = jnp.dot(xbuf[slot], wd_sc[...], preferred_element_type=F32)
            _store_token_major(obuf.at[slot], y)
            out_copy(g, slot).start()
            return carry

        lax.fori_loop(0, nb, block, 0)

    @pl.when(e == N_EXPERTS - 1)
    def _():
        _finish_writes(out_copy, obuf, nu)


def _moe_down(blk_start, n_blk, n_used, hmid, w_down, l):
    grid_spec = pltpu.PrefetchScalarGridSpec(
        num_scalar_prefetch=3,
        grid=(N_EXPERTS,),
        in_specs=[
            pl.BlockSpec(memory_space=pl.ANY),
            pl.BlockSpec((None, None, D_EXPERT, D_MODEL), lambda e, *_: (l, e, 0, 0)),
        ],
        out_specs=pl.BlockSpec(memory_space=pl.ANY),
        scratch_shapes=[
            pltpu.VMEM((D_EXPERT, D_MODEL), BF16),
            pltpu.VMEM((2, EXPERT_BLOCK, D_EXPERT), BF16),
            pltpu.VMEM((2, EXPERT_BLOCK * TOK_ROWS, LANE), F32),
            pltpu.SemaphoreType.DMA((2,)),
            pltpu.SemaphoreType.DMA((2,)),
        ],
    )
    return pl.pallas_call(
        _moe_down_kernel,
        grid_spec=grid_spec,
        out_shape=jax.ShapeDtypeStruct((N_ROWS * TOK_ROWS, LANE), F32),
        compiler_params=_cparams(("arbitrary",), vmem=MOE_VMEM_LIMIT),
        name="moe_down",
    )(blk_start, n_blk, n_used, hmid, w_down)


def _combine_kernel(d0_ref, d1_ref, y_hbm, h1_ref, w0_ref, w1_ref, g_ref, b_ref, h2_ref, h2b_ref,
                    buf0, buf1, sem):
    i = pl.program_id(0)
    base = i * LANE

    def issue(r, carry):
        _row_copy(y_hbm, d0_ref[base + r], buf0, r, sem).start(priority=0)
        _row_copy(y_hbm, d1_ref[base + r], buf1, r, sem).start(priority=1)
        return carry

    lax.fori_loop(0, LANE, issue, 0)

    def drain(r, carry):
        _row_copy(y_hbm, 0, buf0, r, sem).wait()
        _row_copy(y_hbm, 0, buf1, r, sem).wait()
        return carry

    lax.fori_loop(0, LANE, drain, 0)
    ffn = (w0_ref[...] * _load_token_major(buf0, LANE, TOK_PITCH)
           + w1_ref[...] * _load_token_major(buf1, LANE, TOK_PITCH))
    y = _post_ln(h1_ref[...], ffn, g_ref[...], b_ref[...], base)
    h2_ref[...] = y
    h2b_ref[...] = y.astype(BF16)


def _combine_ln2(d0, d1, y_rows, h1, w0, w1, g, b):
    row = lambda i, d0, d1: (i, 0)
    vec = pl.BlockSpec((1, D_MODEL), lambda i, d0, d1: (0, 0))
    col = pl.BlockSpec((LANE, 1), row)
    grid_spec = pltpu.PrefetchScalarGridSpec(
        num_scalar_prefetch=2,
        grid=(R // LANE,),
        in_specs=[
            pl.BlockSpec(memory_space=pl.ANY),
            pl.BlockSpec((LANE, D_MODEL), row),
            col, col, vec, vec,
        ],
        out_specs=[pl.BlockSpec((LANE, D_MODEL), row), pl.BlockSpec((LANE, D_MODEL), row)],
        scratch_shapes=[
            pltpu.VMEM((LANE * TOK_PITCH, LANE), F32),
            pltpu.VMEM((LANE * TOK_PITCH, LANE), F32),
            pltpu.SemaphoreType.DMA,
        ],
    )
    return pl.pallas_call(
        _combine_kernel,
        grid_spec=grid_spec,
        out_shape=[
            jax.ShapeDtypeStruct((R, D_MODEL), F32),
            jax.ShapeDtypeStruct((R, D_MODEL), BF16),
        ],
        compiler_params=_cparams(("arbitrary",)),
        name="moe_combine_ln2",
    )(d0, d1, y_rows, h1, w0, w1, g.reshape(1, -1), b.reshape(1, -1))


def kernel(x, meta_tokens, ln_in_g, ln_in_b, w_in, fox_f_bias, gla_wa2, gla_ba, gla_norm_g, conv_w, pool_w, pool_scale, gate_b, w_branch, w_out, ln1_g, ln1_b, router_w, router_b, w_gate, w_up, w_down, ln2_g, ln2_b):
    assert x.shape == (1, SEQ, D_MODEL)
    h, hb = _ln_in(x.reshape(SEQ, D_MODEL), meta_tokens, ln_in_g, ln_in_b)
    router_wt = router_w.T.reshape(N_GROUPS, EXPERTS_PER_GROUP, D_MODEL).transpose(1, 0, 2).reshape(
        N_EXPERTS, D_MODEL)
    router_bc = router_b.astype(F32).reshape(N_GROUPS, EXPERTS_PER_GROUP).T.reshape(N_EXPERTS, 1)

    wb_bf = w_branch.astype(BF16)
    wo_bf = w_out.astype(BF16)

    for l in range(DEPTH):
        w_all = _prep_w_in(w_in, l)
        zf = _matmul(hb, w_all, W_FOX, N_FOX, BF16, TM_PROJ, 768, "proj_fox")
        z = _matmul(hb, w_all, W_MIX, N_MIXC, F32, TM_PROJ, 512, "proj_mix")
        zs = _proj_small(hb, w_in, l)

        bias_row = jnp.zeros((1, LANE), F32).at[0, SM_FF:SM_FF + FOX_HEADS].set(fox_f_bias[l])
        c = _fox_gate(zs, bias_row)[:, SM_FF:SM_FF + FOX_HEADS]
        c_t = c.T
        o_a = _fox_attention(zf, c_t.reshape(FOX_HEADS, R, 1), c_t.reshape(FOX_HEADS, 1, R))

        wa2p = jnp.zeros((LANE, GLA_HEADS * GLA_DK), F32).at[SM_GA:SM_GA + GLA_RANK].set(gla_wa2[l])
        o_b = _gla(z, zs, wa2p, gla_ba[l].reshape(1, -1), gla_norm_g[l].reshape(1, -1))

        o_c, o_d = _local_mixers(z, conv_w[l], pool_w[l].astype(BF16), pool_scale[l])

        mix = _merge(hb, o_a, o_b, o_c, o_d, w_all, gate_b[l], wb_bf, wo_bf, l)
        h1, h1t, mi, mf, counts = _ln1_route(h, mix, ln1_g[l], ln1_b[l], router_wt, router_bc)

        d0, d1, blk_start, n_blk, n_used, row_src = _dispatch_tables(mi, counts[:, 0])
        hmid = _moe_up(blk_start, n_blk, n_used, row_src, h1t, w_gate, w_up, l)
        y_rows = _moe_down(blk_start, n_blk, n_used, hmid, w_down, l)
        h, hb = _combine_ln2(d0, d1, y_rows, h1, mf[0].reshape(R, 1), mf[1].reshape(R, 1),
                             ln2_g[l], ln2_b[l])

    return h[PAD_ROWS + N_META:].reshape(1, SEQ, D_MODEL)
```

```python
import jax
import jax.numpy as jnp
import numpy as np
from jax import lax
from jax.experimental import pallas as pl
from jax.experimental.pallas import tpu as pltpu

F32 = jnp.float32
BF16 = jnp.bfloat16
I32 = jnp.int32
HIGHEST = lax.Precision.HIGHEST

D_MODEL = 2048
SEQ = 8192
DEPTH = 2
N_META = 16
N_BRANCH = 4
MIX_W = 512
FOX_HEADS = 4
FOX_HD = 128
GLA_HEADS = 4
GLA_DK = 64
GLA_DV = 128
GLA_RANK = 16
GLA_TAU = 16.0
CONV_K = 3
POOL_WINDOWS = (2, 4, 8, 16)
POOL_GW = 128
N_EXPERTS = 32
N_GROUPS = 8
EXPERTS_PER_GROUP = 4
TOP_K = 2
D_EXPERT = 1024
LN_EPS = 1e-5
DEEPNORM_ALPHA = (2 * DEPTH) ** 0.25

_SPLITS = (512, 512, 512, 4, 256, 256, 512, 16, 512, 512, 512, 512, 512, 8192)
_OFFS = [int(o) for o in np.concatenate([[0], np.cumsum(_SPLITS)])]
(O_FQ, O_FK, O_FV, O_FF, O_GQ, O_GK, O_GV, O_GA, O_GR, O_CB, O_CC, O_CV, O_PZ, O_GZ, P_IN) = _OFFS

LANE = 128
PAD_ROWS = LANE - N_META
T0 = PAD_ROWS
N_TOK = N_META + SEQ
R = PAD_ROWS + N_TOK
TM = 640
TM_PROJ = 1664
HALO = 16

WT = 512
W_FOX, W_MIX, W_GATES = 0, 1536, 5120
N_FOX, N_MIXC, N_GATES = 1536, 3584, 8192
N_WALL = W_GATES + N_GATES
SHIFT_G = O_GQ - W_FOX - N_FOX
SHIFT_M = O_GR - (W_MIX + 1024)
C_GQ, C_GK, C_GV, C_GR, C_CB, C_CC, C_CV, C_PZ = 0, 256, 512, 1024, 1536, 2048, 2560, 3072
SM_FF_TILE, SM_GA_TILE = O_FF // LANE, O_GA // LANE
SM_FF = O_FF - SM_FF_TILE * LANE
SM_GA = O_GA - SM_GA_TILE * LANE

EXPERT_BLOCK = 128
N_FLAT = N_TOK * TOP_K
N_BLOCKS = -(-N_FLAT // EXPERT_BLOCK) + N_EXPERTS
N_ROWS = N_BLOCKS * EXPERT_BLOCK

NEG = -1e30
VMEM_LIMIT = 48 * 1024 * 1024
MOE_VMEM_LIMIT = 56 * 1024 * 1024


def _cparams(sem, vmem=VMEM_LIMIT):
    return pltpu.CompilerParams(dimension_semantics=sem, vmem_limit_bytes=vmem)


def _log_sigmoid(x):
    return jnp.minimum(x, 0.0) - jnp.log1p(jnp.exp(-jnp.abs(x)))


def _sigmoid(x):
    return 1.0 / (1.0 + jnp.exp(-x))


def _layer_norm_rows(x, g, b):
    mu = jnp.mean(x, axis=-1, keepdims=True)
    xc = x - mu
    var = jnp.mean(xc * xc, axis=-1, keepdims=True)
    return xc * lax.rsqrt(var + LN_EPS) * g + b


def _ln_in_kernel(x_ref, meta_ref, g_ref, b_ref, h_ref, hb_ref):
    i = pl.program_id(0)

    @pl.when(i == 0)
    def _():
        h_ref[...] = jnp.zeros_like(h_ref)
        hb_ref[...] = jnp.zeros_like(hb_ref)
        m = _layer_norm_rows(meta_ref[...], g_ref[...], b_ref[...])
        h_ref[PAD_ROWS:, :] = m
        hb_ref[PAD_ROWS:, :] = m.astype(BF16)

    @pl.when(i > 0)
    def _():
        y = _layer_norm_rows(x_ref[...], g_ref[...], b_ref[...])
        h_ref[...] = y
        hb_ref[...] = y.astype(BF16)


def _ln_in(x2d, meta, g, b):
    nb = R // LANE
    return pl.pallas_call(
        _ln_in_kernel,
        grid=(nb,),
        in_specs=[
            pl.BlockSpec((LANE, D_MODEL), lambda i: (jnp.maximum(i - 1, 0), 0)),
            pl.BlockSpec((N_META, D_MODEL), lambda i: (0, 0)),
            pl.BlockSpec((1, D_MODEL), lambda i: (0, 0)),
            pl.BlockSpec((1, D_MODEL), lambda i: (0, 0)),
        ],
        out_specs=[
            pl.BlockSpec((LANE, D_MODEL), lambda i: (i, 0)),
            pl.BlockSpec((LANE, D_MODEL), lambda i: (i, 0)),
        ],
        out_shape=[
            jax.ShapeDtypeStruct((R, D_MODEL), F32),
            jax.ShapeDtypeStruct((R, D_MODEL), BF16),
        ],
        compiler_params=_cparams(("arbitrary",)),
        name="ln_in",
    )(x2d, meta, g.reshape(1, -1), b.reshape(1, -1))


TR_PREP = 1024


def _wprep_kernel(a_ref, b_ref, o_ref):
    j = pl.program_id(1)

    def emit(shift):
        if shift == 0:
            o_ref[...] = a_ref[...].astype(BF16)
        else:
            x = jnp.concatenate([a_ref[...], b_ref[...]], axis=1)
            o_ref[...] = pltpu.roll(x, WT + LANE - shift, axis=1)[:, :WT].astype(BF16)

    first_g = N_FOX // WT
    first_m = (W_MIX + 1024) // WT

    @pl.when(j < first_g)
    def _():
        emit(0)

    @pl.when((j >= first_g) & (j < first_m))
    def _():
        emit(SHIFT_G)

    @pl.when(j >= first_m)
    def _():
        emit(SHIFT_M)


def _prep_w_in(w_in, l):
    return pl.pallas_call(
        _wprep_kernel,
        grid=(D_MODEL // TR_PREP, N_WALL // WT),
        in_specs=[
            pl.BlockSpec((None, TR_PREP, WT), lambda i, j: (l, i, j)),
            pl.BlockSpec((None, TR_PREP, LANE), lambda i, j: (l, i, (j + 1) * (WT // LANE))),
        ],
        out_specs=pl.BlockSpec((TR_PREP, WT), lambda i, j: (i, j)),
        out_shape=jax.ShapeDtypeStruct((D_MODEL, N_WALL), BF16),
        compiler_params=_cparams(("parallel", "arbitrary")),
        name="prep_w_in",
    )(w_in, w_in)


def _mm_kernel(a_ref, w_ref, o_ref):
    o_ref[...] = jnp.dot(a_ref[...], w_ref[...], preferred_element_type=F32).astype(o_ref.dtype)


def _matmul(a, w, col0, n, out_dtype, tm, tn, name):
    m, k = a.shape
    return pl.pallas_call(
        _mm_kernel,
        grid=(m // tm, n // tn),
        in_specs=[
            pl.BlockSpec((tm, k), lambda i, j: (i, 0)),
            pl.BlockSpec((k, tn), lambda i, j: (0, col0 // tn + j)),
        ],
        out_specs=pl.BlockSpec((tm, tn), lambda i, j: (i, j)),
        out_shape=jax.ShapeDtypeStruct((m, n), out_dtype),
        compiler_params=_cparams(("parallel", "arbitrary")),
        name=name,
    )(a, w)


def _proj_small_kernel(a_ref, w_ref, o_ref):
    o_ref[...] = jnp.dot(a_ref[...], w_ref[...].astype(BF16), preferred_element_type=F32)


def _proj_small(hb, w_in, l):
    return pl.pallas_call(
        _proj_small_kernel,
        grid=(R // TM_PROJ, 2),
        in_specs=[
            pl.BlockSpec((TM_PROJ, D_MODEL), lambda i, j: (i, 0)),
            pl.BlockSpec((None, D_MODEL, LANE),
                         lambda i, j: (l, 0, SM_FF_TILE + j * (SM_GA_TILE - SM_FF_TILE))),
        ],
        out_specs=pl.BlockSpec((TM_PROJ, LANE), lambda i, j: (i, j)),
        out_shape=jax.ShapeDtypeStruct((R, 2 * LANE), F32),
        compiler_params=_cparams(("parallel", "arbitrary")),
        name="proj_small",
    )(hb, w_in)


def _fox_gate_kernel(zs_ref, bias_ref, c_ref, carry_ref):
    i = pl.program_id(0)

    @pl.when(i == 0)
    def _():
        carry_ref[...] = jnp.zeros_like(carry_ref)

    rows = i * TM + lax.broadcasted_iota(I32, (TM, LANE), 0)
    lf = _log_sigmoid(zs_ref[...] + bias_ref[...])
    lf = jnp.where(rows >= T0, lf, 0.0)
    tri = (lax.broadcasted_iota(I32, (TM, TM), 0)
           >= lax.broadcasted_iota(I32, (TM, TM), 1)).astype(F32)
    c = jnp.dot(tri, lf, precision=HIGHEST, preferred_element_type=F32) + carry_ref[...]
    c_ref[...] = c
    carry_ref[...] = c[TM - 1:TM, :]


def _fox_gate(zs, bias_row):
    return pl.pallas_call(
        _fox_gate_kernel,
        grid=(R // TM,),
        in_specs=[
            pl.BlockSpec((TM, LANE), lambda i: (i, 0)),
            pl.BlockSpec((1, LANE), lambda i: (0, 0)),
        ],
        out_specs=pl.BlockSpec((TM, LANE), lambda i: (i, 0)),
        out_shape=jax.ShapeDtypeStruct((R, LANE), F32),
        scratch_shapes=[pltpu.VMEM((1, LANE), F32)],
        compiler_params=_cparams(("arbitrary",)),
        name="fox_gate",
    )(zs, bias_row)


TQ = TM
N_QB = R // TQ
_PAIRS = [(qi, kj) for qi in range(N_QB) for kj in range(qi + 1)]
N_PAIRS = len(_PAIRS)


LOG2E = 1.4426950408889634


def _fox_kernel(qi_tab, kj_tab, q_ref, k_ref, v_ref, ck_ref, o_ref, m_sc, l_sc, acc_sc):
    p = pl.program_id(1)
    qi = qi_tab[p]
    kj = kj_tab[p]

    @pl.when(kj == 0)
    def _():
        m_sc[...] = jnp.full_like(m_sc, NEG)
        l_sc[...] = jnp.zeros_like(l_sc)
        acc_sc[...] = jnp.zeros_like(acc_sc)

    kpos = kj * TQ + lax.broadcasted_iota(I32, (1, TQ), 1)
    ckl = jnp.where(kpos >= T0, ck_ref[...] * LOG2E, -NEG)
    c1 = FOX_HD ** -0.5 * LOG2E

    def step(causal):
        t = lax.dot_general(q_ref[...], k_ref[...], (((1,), (1,)), ((), ())),
                            preferred_element_type=F32) * c1 - ckl
        if causal:
            ahead = (lax.broadcasted_iota(I32, (TQ, TQ), 1)
                     - lax.broadcasted_iota(I32, (TQ, TQ), 0))
            t = jnp.where(ahead <= 0, t, NEG)
        m_prev = m_sc[...]
        m_new = jnp.maximum(m_prev, jnp.max(t, axis=-1, keepdims=True))
        alpha = jnp.exp2(m_prev - m_new)
        pr = jnp.exp2(t - m_new)
        l_sc[...] = alpha * l_sc[...] + jnp.sum(pr, axis=-1, keepdims=True)
        acc_sc[...] = alpha * acc_sc[...] + jnp.dot(pr.astype(BF16), v_ref[...],
                                                    preferred_element_type=F32)
        m_sc[...] = m_new

    @pl.when(kj == qi)
    def _():
        step(True)

    @pl.when(kj != qi)
    def _():
        step(False)

    @pl.when(kj == qi)
    def _():
        o_ref[...] = (acc_sc[...] / l_sc[...]).astype(o_ref.dtype)


def _fox_attention(zf, c_row):
    qi_tab = jnp.asarray([p[0] for p in _PAIRS], I32)
    kj_tab = jnp.asarray([p[1] for p in _PAIRS], I32)
    grid_spec = pltpu.PrefetchScalarGridSpec(
        num_scalar_prefetch=2,
        grid=(FOX_HEADS, N_PAIRS),
        in_specs=[
            pl.BlockSpec((TQ, FOX_HD), lambda h, p, qt, kt: (qt[p], h)),
            pl.BlockSpec((TQ, FOX_HD), lambda h, p, qt, kt: (kt[p], FOX_HEADS + h)),
            pl.BlockSpec((TQ, FOX_HD), lambda h, p, qt, kt: (kt[p], 2 * FOX_HEADS + h)),
            pl.BlockSpec((None, 1, TQ), lambda h, p, qt, kt: (h, 0, kt[p])),
        ],
        out_specs=pl.BlockSpec((TQ, FOX_HD), lambda h, p, qt, kt: (qt[p], h)),
        scratch_shapes=[
            pltpu.VMEM((TQ, 1), F32),
            pltpu.VMEM((TQ, 1), F32),
            pltpu.VMEM((TQ, FOX_HD), F32),
        ],
    )
    return pl.pallas_call(
        _fox_kernel,
        grid_spec=grid_spec,
        out_shape=jax.ShapeDtypeStruct((R, MIX_W), BF16),
        compiler_params=_cparams(("parallel", "arbitrary")),
        name="fox_attention",
    )(qi_tab, kj_tab, zf, zf, zf, c_row)


GLA_CHUNK = 64


def _gla_kernel(q_ref, k_ref, v_ref, gr_ref, zs_ref, wa2_ref, ba_ref, gn_ref, o_ref, st_ref, la_ref):
    i = pl.program_id(0)

    @pl.when(i == 0)
    def _():
        st_ref[...] = jnp.zeros_like(st_ref)

    la = jnp.dot(zs_ref[...], wa2_ref[...], precision=HIGHEST, preferred_element_type=F32)
    la_ref[...] = _log_sigmoid(la + ba_ref[...]) * (1.0 / GLA_TAU)

    c_r = lax.broadcasted_iota(I32, (GLA_CHUNK, GLA_CHUNK), 0)
    c_c = lax.broadcasted_iota(I32, (GLA_CHUNK, GLA_CHUNK), 1)
    tri_b = c_r >= c_c
    tri = tri_b.astype(F32)

    def chunk(c, carry):
        r0 = pl.multiple_of(c * GLA_CHUNK, GLA_CHUNK)
        rows = pl.ds(r0, GLA_CHUNK)
        g = la_ref[rows, :]
        b = jnp.dot(tri, g, precision=HIGHEST, preferred_element_type=F32)
        b_last = b[GLA_CHUNK - 1:GLA_CHUNK, :]
        e_last = jnp.exp(b_last)
        qt = q_ref[rows, :] * (GLA_DK ** -0.5) * jnp.exp(b)
        kt = k_ref[rows, :] * jnp.exp(-b)
        kh = kt * e_last
        for h in range(GLA_HEADS):
            ks = slice(h * GLA_DK, (h + 1) * GLA_DK)
            vs = slice(h * GLA_DV, (h + 1) * GLA_DV)
            q_h = qt[:, ks].astype(BF16)
            k_h = kt[:, ks].astype(BF16)
            kh_h = kh[:, ks].astype(BF16)
            v_h = v_ref[rows, vs]
            att = lax.dot_general(q_h, k_h, (((1,), (1,)), ((), ())), preferred_element_type=F32)
            att = jnp.where(tri_b, att, 0.0)
            st = st_ref[h]
            o = jnp.dot(att.astype(BF16), v_h.astype(BF16), preferred_element_type=F32)
            o = o + lax.dot_general(q_h, st.astype(BF16), (((1,), (1,)), ((), ())),
                                    preferred_element_type=F32)
            st_ref[h] = st * e_last[:, ks] + jnp.dot(v_h.T.astype(BF16), kh_h,
                                                     preferred_element_type=F32)
            ms = jnp.mean(o * o, axis=-1, keepdims=True)
            on = o * lax.rsqrt(ms + LN_EPS) * gn_ref[:, vs]
            gate = gr_ref[rows, vs]
            o_ref[rows, vs] = (on * (gate * _sigmoid(gate))).astype(o_ref.dtype)
        return carry

    lax.fori_loop(0, TM // GLA_CHUNK, chunk, 0)


def _gla(z, zs, wa2p, ba, gn):
    return pl.pallas_call(
        _gla_kernel,
        grid=(R // TM,),
        in_specs=[
            pl.BlockSpec((TM, 256), lambda i: (i, C_GQ // 256)),
            pl.BlockSpec((TM, 256), lambda i: (i, C_GK // 256)),
            pl.BlockSpec((TM, 512), lambda i: (i, C_GV // 512)),
            pl.BlockSpec((TM, 512), lambda i: (i, C_GR // 512)),
            pl.BlockSpec((TM, LANE), lambda i: (i, 1)),
            pl.BlockSpec((LANE, 256), lambda i: (0, 0)),
            pl.BlockSpec((1, 256), lambda i: (0, 0)),
            pl.BlockSpec((1, 512), lambda i: (0, 0)),
        ],
        out_specs=pl.BlockSpec((TM, MIX_W), lambda i: (i, 0)),
        out_shape=jax.ShapeDtypeStruct((R, MIX_W), BF16),
        scratch_shapes=[
            pltpu.VMEM((GLA_HEADS, GLA_DV, GLA_DK), F32),
            pltpu.VMEM((TM, GLA_HEADS * GLA_DK), F32),
        ],
        compiler_params=_cparams(("arbitrary",)),
        name="gla",
    )(z, z, z, z, zs, wa2p, ba, gn)


def _local_kernel(cb_ref, cc_ref, cv_ref, pz_ref, cw_ref, pw_ref, ps_ref, oc_ref, od_ref, u_sc, p_sc):
    i = pl.program_id(0)

    @pl.when(i == 0)
    def _():
        u_sc[0:HALO, :] = jnp.zeros((HALO, MIX_W), F32)
        p_sc[0:HALO, :] = jnp.zeros((HALO, MIX_W), F32)

    @pl.when(i > 0)
    def _():
        u_sc[0:HALO, :] = u_sc[TM:TM + HALO, :]
        p_sc[0:HALO, :] = p_sc[TM:TM + HALO, :]

    u = cc_ref[...] * cv_ref[...]
    pz = pz_ref[...]
    u_sc[HALO:, :] = u
    p_sc[HALO:, :] = pz

    y = (cw_ref[2:3, :] * u + cw_ref[1:2, :] * u_sc[HALO - 1:HALO - 1 + TM, :]
         + cw_ref[0:1, :] * u_sc[HALO - 2:HALO - 2 + TM, :])
    oc_ref[...] = (cb_ref[...] * y).astype(oc_ref.dtype)

    tok = i * TM - T0 + lax.broadcasted_iota(I32, (TM, 1), 0)
    cnt_small = jnp.maximum(tok + 1, 1).astype(F32)
    for g, w in enumerate(POOL_WINDOWS):
        cols = slice(g * POOL_GW, (g + 1) * POOL_GW)
        x = pz[:, cols]
        s = x
        for j in range(1, w):
            s = s + p_sc[HALO - j:HALO - j + TM, cols]
        inv_cnt = jnp.where(tok + 1 >= w, 1.0 / w, 1.0 / cnt_small)
        pooled = s * inv_cnt - x
        od = jnp.dot(pooled.astype(BF16), pw_ref[g], preferred_element_type=F32)
        od_ref[:, cols] = (od * ps_ref[:, cols]).astype(od_ref.dtype)


def _local_mixers(z, conv_w, pool_w_bf, pool_scale):
    cw = jnp.zeros((8, MIX_W), F32).at[:CONV_K].set(conv_w)
    blk = lambda c: pl.BlockSpec((TM, MIX_W), lambda i, c=c: (i, c // MIX_W))
    return pl.pallas_call(
        _local_kernel,
        grid=(R // TM,),
        in_specs=[
            blk(C_CB), blk(C_CC), blk(C_CV), blk(C_PZ),
            pl.BlockSpec((8, MIX_W), lambda i: (0, 0)),
            pl.BlockSpec((len(POOL_WINDOWS), POOL_GW, POOL_GW), lambda i: (0, 0, 0)),
            pl.BlockSpec((1, MIX_W), lambda i: (0, 0)),
        ],
        out_specs=[
            pl.BlockSpec((TM, MIX_W), lambda i: (i, 0)),
            pl.BlockSpec((TM, MIX_W), lambda i: (i, 0)),
        ],
        out_shape=[
            jax.ShapeDtypeStruct((R, MIX_W), BF16),
            jax.ShapeDtypeStruct((R, MIX_W), BF16),
        ],
        scratch_shapes=[
            pltpu.VMEM((TM + HALO, MIX_W), F32),
            pltpu.VMEM((TM + HALO, MIX_W), F32),
        ],
        compiler_params=_cparams(("arbitrary",)),
        name="conv_pool",
    )(z, z, z, z, cw, pool_w_bf, pool_scale.reshape(1, -1))


TN_MERGE = 256


def _merge_kernel(hb_ref, oa_ref, ob_ref, oc_ref, od_ref, wg0_ref, wg1_ref, wg2_ref, wg3_ref,
                  gb_ref, wb_ref, wo_ref, out_ref):
    j = pl.program_id(1)

    @pl.when(j == 0)
    def _():
        out_ref[...] = jnp.zeros_like(out_ref)

    hb = hb_ref[...]
    mixed = None
    for b, (o_ref, wg_ref) in enumerate(((oa_ref, wg0_ref), (ob_ref, wg1_ref),
                                         (oc_ref, wg2_ref), (od_ref, wg3_ref))):
        gate = _sigmoid(jnp.dot(hb, wg_ref[...], preferred_element_type=F32) + gb_ref[b:b + 1, :])
        proj = jnp.dot(o_ref[...], wb_ref[b], preferred_element_type=F32)
        term = gate * proj
        mixed = term if mixed is None else mixed + term
    out_ref[...] += jnp.dot(mixed.astype(BF16), wo_ref[...], preferred_element_type=F32)


def _merge(hb, o_a, o_b, o_c, o_d, w_all, gate_b, wb_bf, wo_bf, l):
    tn = TN_MERGE
    nj = D_MODEL // tn
    row = lambda w: pl.BlockSpec((TM, w), lambda i, j: (i, 0))
    wg = lambda b: pl.BlockSpec((D_MODEL, tn), lambda i, j, b=b: (0, W_GATES // tn + b * nj + j))
    return pl.pallas_call(
        _merge_kernel,
        grid=(R // TM, nj),
        in_specs=[
            row(D_MODEL), row(MIX_W), row(MIX_W), row(MIX_W), row(MIX_W),
            wg(0), wg(1), wg(2), wg(3),
            pl.BlockSpec((N_BRANCH, tn), lambda i, j: (0, j)),
            pl.BlockSpec((None, N_BRANCH, MIX_W, tn), lambda i, j: (l, 0, 0, j)),
            pl.BlockSpec((None, tn, D_MODEL), lambda i, j: (l, j, 0)),
        ],
        out_specs=pl.BlockSpec((TM, D_MODEL), lambda i, j: (i, 0)),
        out_shape=jax.ShapeDtypeStruct((R, D_MODEL), F32),
        compiler_params=_cparams(("parallel", "arbitrary")),
        name="merge",
    )(hb, o_a, o_b, o_c, o_d, w_all, w_all, w_all, w_all, gate_b, wb_bf, wo_bf)


def _post_ln(h, delta, g, b, row0):
    y = _layer_norm_rows(DEEPNORM_ALPHA * h + delta, g, b)
    rows = row0 + lax.broadcasted_iota(I32, (y.shape[0], 1), 0)
    return jnp.where(rows >= T0, y, 0.0)


def _first_of(cands, target):
    idx = jnp.full(target.shape, len(cands) - 1, I32)
    for j in range(len(cands) - 2, -1, -1):
        idx = jnp.where(cands[j] == target, j, idx)
    return idx


def _pick(cands, idx):
    out = cands[-1]
    for j in range(len(cands) - 2, -1, -1):
        out = jnp.where(idx == j, cands[j], out)
    return out


def _ln1_route_kernel(h_ref, mix_ref, g_ref, b_ref, rwt_ref, rb_ref,
                      h1_ref, mi_ref, mf_ref, cnt_ref, carry_sc):
    i = pl.program_id(0)

    @pl.when(i == 0)
    def _():
        carry_sc[...] = jnp.zeros_like(carry_sc)

    y = _post_ln(h_ref[...], mix_ref[...], g_ref[...], b_ref[...], i * TM)
    h1_ref[...] = y

    logits = lax.dot_general(rwt_ref[...], y, (((1,), (1,)), ((), ())), precision=HIGHEST,
                             preferred_element_type=F32)
    aff = _sigmoid(logits)
    sel = aff + rb_ref[...]
    xs = [sel[j * N_GROUPS:(j + 1) * N_GROUPS, :] for j in range(EXPERTS_PER_GROUP)]
    afs = [aff[j * N_GROUPS:(j + 1) * N_GROUPS, :] for j in range(EXPERTS_PER_GROUP)]

    score = None
    for a in range(EXPERTS_PER_GROUP):
        for bb in range(a + 1, EXPERTS_PER_GROUP):
            pair = xs[a] + xs[bb]
            score = pair if score is None else jnp.maximum(score, pair)
    giota = lax.broadcasted_iota(I32, (N_GROUPS, TM), 0)
    gmax = jnp.max(score, axis=0, keepdims=True)
    grp = jnp.min(jnp.where(score == gmax, giota, N_GROUPS), axis=0, keepdims=True)
    gsel = giota == grp
    cs = [jnp.max(jnp.where(gsel, x, -jnp.inf), axis=0, keepdims=True) for x in xs]
    acs = [jnp.sum(jnp.where(gsel, a, 0.0), axis=0, keepdims=True) for a in afs]

    m1 = jnp.maximum(jnp.maximum(cs[0], cs[1]), jnp.maximum(cs[2], cs[3]))
    i0 = _first_of(cs, m1)
    ds = [jnp.where(i0 == j, -jnp.inf, cs[j]) for j in range(EXPERTS_PER_GROUP)]
    m2 = jnp.maximum(jnp.maximum(ds[0], ds[1]), jnp.maximum(ds[2], ds[3]))
    i1 = _first_of(ds, m2)
    a0 = _pick(acs, i0)
    a1 = _pick(acs, i1)
    denom = a0 + a1

    pos = i * TM + lax.broadcasted_iota(I32, (1, TM), 1)
    valid = pos >= T0
    riota = lax.broadcasted_iota(I32, (N_EXPERTS, TM), 0)
    oh0 = (riota == i0 * N_GROUPS + grp) & valid
    oh1 = (riota == i1 * N_GROUPS + grp) & valid
    ohf = jnp.where(oh0 | oh1, 1.0, 0.0)
    before = (lax.broadcasted_iota(I32, (TM, TM), 0)
              < lax.broadcasted_iota(I32, (TM, TM), 1)).astype(BF16)
    cum = jnp.dot(ohf.astype(BF16), before, preferred_element_type=F32) + carry_sc[...]
    rank0 = jnp.sum(jnp.where(oh0, cum, 0.0), axis=0, keepdims=True)
    rank1 = jnp.sum(jnp.where(oh1, cum, 0.0), axis=0, keepdims=True)
    carry = carry_sc[...] + jnp.sum(ohf, axis=1, keepdims=True)
    carry_sc[...] = carry
    cnt_ref[...] = jnp.broadcast_to(carry, cnt_ref.shape)

    zi = jnp.zeros((1, TM), I32)
    mi_ref[...] = jnp.concatenate(
        [grp * EXPERTS_PER_GROUP + i0, grp * EXPERTS_PER_GROUP + i1,
         rank0.astype(I32), rank1.astype(I32), zi, zi, zi, zi], axis=0)
    zf = jnp.zeros((1, TM), F32)
    mf_ref[...] = jnp.concatenate([a0 / denom, a1 / denom, zf, zf, zf, zf, zf, zf], axis=0)


def _ln1_route(h, mix, g, b, router_wt, router_bc):
    row = pl.BlockSpec((TM, D_MODEL), lambda i: (i, 0))
    vec = pl.BlockSpec((1, D_MODEL), lambda i: (0, 0))
    meta = pl.BlockSpec((8, TM), lambda i: (0, i))
    return pl.pallas_call(
        _ln1_route_kernel,
        grid=(R // TM,),
        in_specs=[row, row, vec, vec,
                  pl.BlockSpec((N_EXPERTS, D_MODEL), lambda i: (0, 0)),
                  pl.BlockSpec((N_EXPERTS, 1), lambda i: (0, 0))],
        out_specs=[row, meta, meta, pl.BlockSpec((N_EXPERTS, LANE), lambda i: (0, 0))],
        out_shape=[
            jax.ShapeDtypeStruct((R, D_MODEL), F32),
            jax.ShapeDtypeStruct((8, R), I32),
            jax.ShapeDtypeStruct((8, R), F32),
            jax.ShapeDtypeStruct((N_EXPERTS, LANE), F32),
        ],
        scratch_shapes=[pltpu.VMEM((N_EXPERTS, 1), F32)],
        compiler_params=_cparams(("arbitrary",)),
        name="ln1_route",
    )(h, mix, g.reshape(1, -1), b.reshape(1, -1), router_wt, router_bc)


def _dispatch_tables(mi, counts_slot_major):
    counts = counts_slot_major.reshape(EXPERTS_PER_GROUP, N_GROUPS).T.reshape(N_EXPERTS).astype(I32)
    padded = (counts + EXPERT_BLOCK - 1) // EXPERT_BLOCK * EXPERT_BLOCK
    pad_end = jnp.cumsum(padded)
    pad_start = pad_end - padded
    e_iota = jnp.arange(N_EXPERTS, dtype=I32)
    rows_ok = jnp.arange(R) >= T0

    def dest(eid, rank):
        start = jnp.sum(jnp.where(eid[:, None] == e_iota[None, :], pad_start[None, :], 0), axis=1)
        return jnp.where(rows_ok, start + rank, 0).astype(I32)

    d0 = dest(mi[0], mi[2])
    d1 = dest(mi[1], mi[3])
    blk_start = (pad_start // EXPERT_BLOCK).astype(I32)
    n_blk = (padded // EXPERT_BLOCK).astype(I32)
    n_used = (pad_end[-1] // EXPERT_BLOCK).astype(I32).reshape(1)
    tok_rows = jnp.arange(T0, R, dtype=I32)
    row_src = jnp.zeros((N_ROWS + EXPERT_BLOCK,), I32).at[jnp.concatenate([d0[T0:], d1[T0:]])].set(
        jnp.concatenate([tok_rows, tok_rows]), unique_indices=True)
    return d0, d1, blk_start, n_blk, n_used, row_src


def _row_copy(src, src_row, dst, dst_row, sem):
    return pltpu.make_async_copy(src.at[pl.ds(src_row, 1), :], dst.at[pl.ds(dst_row, 1), :], sem)


def _issue_row_gather(rs_ref, g, h_hbm, buf, sem):
    base = g * EXPERT_BLOCK
    for r in range(EXPERT_BLOCK):
        _row_copy(h_hbm, rs_ref[base + r], buf, r, sem).start(priority=1)


def _wait_row_gather(h_hbm, buf, sem):
    for r in range(EXPERT_BLOCK):
        _row_copy(h_hbm, 0, buf, r, sem).wait()


def _block_rows(g):
    return pl.ds(pl.multiple_of(g * EXPERT_BLOCK, EXPERT_BLOCK), EXPERT_BLOCK)


def _finish_writes(out_copy, obuf, nu):
    @pl.when(nu >= 2)
    def _():
        out_copy(nu - 2, nu % 2).wait()

    @pl.when(nu >= 1)
    def _():
        out_copy(nu - 1, (nu - 1) % 2).wait()

    obuf[0] = jnp.zeros(obuf.shape[1:], obuf.dtype)

    def zero_block(g, carry):
        cp = out_copy(g, 0)
        cp.start()
        cp.wait()
        return carry

    lax.fori_loop(nu, N_BLOCKS, zero_block, 0)


def _moe_up_kernel(bs_ref, nb_ref, nu_ref, rs_ref, h_hbm, wg_ref, wu_ref, o_hbm,
                   wg_sc, wu_sc, xbuf, obuf, xsem, osem):
    e = pl.program_id(0)
    nb = nb_ref[e]
    g0 = bs_ref[e]

    def out_copy(g, slot):
        return pltpu.make_async_copy(obuf.at[slot], o_hbm.at[_block_rows(g), :], osem.at[slot])

    @pl.when(e == 0)
    def _():
        _issue_row_gather(rs_ref, 0, h_hbm, xbuf.at[0], xsem.at[0])

    @pl.when(nb > 0)
    def _():
        wg_sc[...] = wg_ref[...].astype(BF16)
        wu_sc[...] = wu_ref[...].astype(BF16)

        def block(j, carry):
            g = g0 + j
            slot = g % 2

            @pl.when(g >= 2)
            def _():
                out_copy(g - 2, slot).wait()

            _wait_row_gather(h_hbm, xbuf.at[slot], xsem.at[slot])
            _issue_row_gather(rs_ref, g + 1, h_hbm, xbuf.at[1 - slot], xsem.at[1 - slot])
            x = xbuf[slot].astype(BF16)
            gate = jnp.dot(x, wg_sc[...], preferred_element_type=F32)
            up = jnp.dot(x, wu_sc[...], preferred_element_type=F32)
            obuf[slot] = (gate * _sigmoid(gate) * up).astype(BF16)
            out_copy(g, slot).start()
            return carry

        lax.fori_loop(0, nb, block, 0)

    @pl.when(e == N_EXPERTS - 1)
    def _():
        nu = nu_ref[0]
        _wait_row_gather(h_hbm, xbuf.at[nu % 2], xsem.at[nu % 2])
        _finish_writes(out_copy, obuf, nu)


def _moe_up(blk_start, n_blk, n_used, row_src, h1, w_gate, w_up, l):
    wspec = pl.BlockSpec((None, None, D_MODEL, D_EXPERT), lambda e, *_: (l, e, 0, 0))
    grid_spec = pltpu.PrefetchScalarGridSpec(
        num_scalar_prefetch=4,
        grid=(N_EXPERTS,),
        in_specs=[pl.BlockSpec(memory_space=pl.ANY), wspec, wspec],
        out_specs=pl.BlockSpec(memory_space=pl.ANY),
        scratch_shapes=[
            pltpu.VMEM((D_MODEL, D_EXPERT), BF16),
            pltpu.VMEM((D_MODEL, D_EXPERT), BF16),
            pltpu.VMEM((2, EXPERT_BLOCK, D_MODEL), F32),
            pltpu.VMEM((2, EXPERT_BLOCK, D_EXPERT), BF16),
            pltpu.SemaphoreType.DMA((2,)),
            pltpu.SemaphoreType.DMA((2,)),
        ],
    )
    return pl.pallas_call(
        _moe_up_kernel,
        grid_spec=grid_spec,
        out_shape=jax.ShapeDtypeStruct((N_ROWS, D_EXPERT), BF16),
        compiler_params=_cparams(("arbitrary",), vmem=MOE_VMEM_LIMIT),
        name="moe_up",
    )(blk_start, n_blk, n_used, row_src, h1, w_gate, w_up)


def _moe_down_kernel(bs_ref, nb_ref, nu_ref, x_hbm, wd_ref, y_hbm, wd_sc, xbuf, obuf, xsem, osem):
    e = pl.program_id(0)
    nb = nb_ref[e]
    g0 = bs_ref[e]
    nu = nu_ref[0]

    def in_copy(g, slot):
        return pltpu.make_async_copy(x_hbm.at[_block_rows(g), :], xbuf.at[slot], xsem.at[slot])

    def out_copy(g, slot):
        return pltpu.make_async_copy(obuf.at[slot], y_hbm.at[_block_rows(g), :], osem.at[slot])

    @pl.when((e == 0) & (nu > 0))
    def _():
        in_copy(0, 0).start()

    @pl.when(nb > 0)
    def _():
        wd_sc[...] = wd_ref[...].astype(BF16)

        def block(j, carry):
            g = g0 + j
            slot = g % 2

            @pl.when(g >= 2)
            def _():
                out_copy(g - 2, slot).wait()

            in_copy(g, slot).wait()

            @pl.when(g + 1 < nu)
            def _():
                in_copy(g + 1, 1 - slot).start()

            obuf[slot] = jnp.dot(xbuf[slot], wd_sc[...], preferred_element_type=F32)
            out_copy(g, slot).start()
            return carry

        lax.fori_loop(0, nb, block, 0)

    @pl.when(e == N_EXPERTS - 1)
    def _():
        _finish_writes(out_copy, obuf, nu)


def _moe_down(blk_start, n_blk, n_used, hmid, w_down, l):
    grid_spec = pltpu.PrefetchScalarGridSpec(
        num_scalar_prefetch=3,
        grid=(N_EXPERTS,),
        in_specs=[
            pl.BlockSpec(memory_space=pl.ANY),
            pl.BlockSpec((None, None, D_EXPERT, D_MODEL), lambda e, *_: (l, e, 0, 0)),
        ],
        out_specs=pl.BlockSpec(memory_space=pl.ANY),
        scratch_shapes=[
            pltpu.VMEM((D_EXPERT, D_MODEL), BF16),
            pltpu.VMEM((2, EXPERT_BLOCK, D_EXPERT), BF16),
            pltpu.VMEM((2, EXPERT_BLOCK, D_MODEL), F32),
            pltpu.SemaphoreType.DMA((2,)),
            pltpu.SemaphoreType.DMA((2,)),
        ],
    )
    return pl.pallas_call(
        _moe_down_kernel,
        grid_spec=grid_spec,
        out_shape=jax.ShapeDtypeStruct((N_ROWS, D_MODEL), F32),
        compiler_params=_cparams(("arbitrary",), vmem=MOE_VMEM_LIMIT),
        name="moe_down",
    )(blk_start, n_blk, n_used, hmid, w_down)


def _combine_kernel(d0_ref, d1_ref, y_hbm, h1_ref, mf_ref, g_ref, b_ref, h2_ref, h2b_ref,
                    buf0, buf1, sem):
    i = pl.program_id(0)
    slot = i % 2

    def issue(tile, s):
        base = tile * LANE

        def body(r, carry):
            _row_copy(y_hbm, d0_ref[base + r], buf0.at[s], r, sem.at[s]).start(priority=0)
            _row_copy(y_hbm, d1_ref[base + r], buf1.at[s], r, sem.at[s]).start(priority=1)
            return carry

        lax.fori_loop(0, LANE, body, 0)

    @pl.when(i == 0)
    def _():
        issue(0, 0)

    @pl.when(i + 1 < pl.num_programs(0))
    def _():
        issue(i + 1, 1 - slot)

    def drain(r, carry):
        _row_copy(y_hbm, 0, buf0.at[slot], r, sem.at[slot]).wait()
        _row_copy(y_hbm, 0, buf1.at[slot], r, sem.at[slot]).wait()
        return carry

    lax.fori_loop(0, LANE, drain, 0)
    wt = mf_ref[...].T
    ffn = wt[:, 0:1] * buf0[slot] + wt[:, 1:2] * buf1[slot]
    y = _post_ln(h1_ref[...], ffn, g_ref[...], b_ref[...], i * LANE)
    h2_ref[...] = y
    h2b_ref[...] = y.astype(BF16)


def _combine_ln2(d0, d1, y_rows, h1, mf, g, b):
    row = lambda i, d0, d1: (i, 0)
    vec = pl.BlockSpec((1, D_MODEL), lambda i, d0, d1: (0, 0))
    grid_spec = pltpu.PrefetchScalarGridSpec(
        num_scalar_prefetch=2,
        grid=(R // LANE,),
        in_specs=[
            pl.BlockSpec(memory_space=pl.ANY),
            pl.BlockSpec((LANE, D_MODEL), row),
            pl.BlockSpec((8, LANE), lambda i, d0, d1: (0, i)),
            vec, vec,
        ],
        out_specs=[pl.BlockSpec((LANE, D_MODEL), row), pl.BlockSpec((LANE, D_MODEL), row)],
        scratch_shapes=[
            pltpu.VMEM((2, LANE, D_MODEL), F32),
            pltpu.VMEM((2, LANE, D_MODEL), F32),
            pltpu.SemaphoreType.DMA((2,)),
        ],
    )
    return pl.pallas_call(
        _combine_kernel,
        grid_spec=grid_spec,
        out_shape=[
            jax.ShapeDtypeStruct((R, D_MODEL), F32),
            jax.ShapeDtypeStruct((R, D_MODEL), BF16),
        ],
        compiler_params=_cparams(("arbitrary",)),
        name="moe_combine_ln2",
    )(d0, d1, y_rows, h1, mf, g.reshape(1, -1), b.reshape(1, -1))


def kernel(x, meta_tokens, ln_in_g, ln_in_b, w_in, fox_f_bias, gla_wa2, gla_ba, gla_norm_g, conv_w, pool_w, pool_scale, gate_b, w_branch, w_out, ln1_g, ln1_b, router_w, router_b, w_gate, w_up, w_down, ln2_g, ln2_b):
    assert x.shape == (1, SEQ, D_MODEL)
    h, hb = _ln_in(x.reshape(SEQ, D_MODEL), meta_tokens, ln_in_g, ln_in_b)
    router_wt = router_w.T.reshape(N_GROUPS, EXPERTS_PER_GROUP, D_MODEL).transpose(1, 0, 2).reshape(
        N_EXPERTS, D_MODEL)
    router_bc = router_b.astype(F32).reshape(N_GROUPS, EXPERTS_PER_GROUP).T.reshape(N_EXPERTS, 1)

    wb_bf = w_branch.astype(BF16)
    wo_bf = w_out.astype(BF16)

    for l in range(DEPTH):
        w_all = _prep_w_in(w_in, l)
        zf = _matmul(hb, w_all, W_FOX, N_FOX, BF16, TM_PROJ, 768, "proj_fox")
        z = _matmul(hb, w_all, W_MIX, N_MIXC, F32, TM_PROJ, 512, "proj_mix")
        zs = _proj_small(hb, w_in, l)

        bias_row = jnp.zeros((1, LANE), F32).at[0, SM_FF:SM_FF + FOX_HEADS].set(fox_f_bias[l])
        c = _fox_gate(zs, bias_row)[:, SM_FF:SM_FF + FOX_HEADS]
        o_a = _fox_attention(zf, c.T.reshape(FOX_HEADS, 1, R))

        wa2p = jnp.zeros((LANE, GLA_HEADS * GLA_DK), F32).at[SM_GA:SM_GA + GLA_RANK].set(gla_wa2[l])
        o_b = _gla(z, zs, wa2p, gla_ba[l].reshape(1, -1), gla_norm_g[l].reshape(1, -1))

        o_c, o_d = _local_mixers(z, conv_w[l], pool_w[l].astype(BF16), pool_scale[l])

        mix = _merge(hb, o_a, o_b, o_c, o_d, w_all, gate_b[l], wb_bf, wo_bf, l)
        h1, mi, mf, counts = _ln1_route(h, mix, ln1_g[l], ln1_b[l], router_wt, router_bc)

        d0, d1, blk_start, n_blk, n_used, row_src = _dispatch_tables(mi, counts[:, 0])
        hmid = _moe_up(blk_start, n_blk, n_used, row_src, h1, w_gate, w_up, l)
        y_rows = _moe_down(blk_start, n_blk, n_used, hmid, w_down, l)
        h, hb = _combine_ln2(d0, d1, y_rows, h1, mf, ln2_g[l], ln2_b[l])

    return h[PAD_ROWS + N_META:].reshape(1, SEQ, D_MODEL)
```

```python
import jax
import jax.numpy as jnp
import numpy as np
from jax import lax
from jax.experimental import pallas as pl
from jax.experimental.pallas import tpu as pltpu

F32 = jnp.float32
BF16 = jnp.bfloat16
I32 = jnp.int32
HIGHEST = lax.Precision.HIGHEST

D_MODEL = 2048
SEQ = 8192
DEPTH = 2
N_META = 16
N_BRANCH = 4
MIX_W = 512
FOX_HEADS = 4
FOX_HD = 128
GLA_HEADS = 4
GLA_DK = 64
GLA_DV = 128
GLA_RANK = 16
GLA_TAU = 16.0
CONV_K = 3
POOL_WINDOWS = (2, 4, 8, 16)
POOL_GW = 128
N_EXPERTS = 32
N_GROUPS = 8
EXPERTS_PER_GROUP = 4
TOP_K = 2
D_EXPERT = 1024
LN_EPS = 1e-5
DEEPNORM_ALPHA = (2 * DEPTH) ** 0.25

_SPLITS = (512, 512, 512, 4, 256, 256, 512, 16, 512, 512, 512, 512, 512, 8192)
_OFFS = [int(o) for o in np.concatenate([[0], np.cumsum(_SPLITS)])]
(O_FQ, O_FK, O_FV, O_FF, O_GQ, O_GK, O_GV, O_GA, O_GR, O_CB, O_CC, O_CV, O_PZ, O_GZ, P_IN) = _OFFS

LANE = 128
PAD_ROWS = LANE - N_META
T0 = PAD_ROWS
N_TOK = N_META + SEQ
R = PAD_ROWS + N_TOK
TM = 640
TM_PROJ = 1664
HALO = 16

WT = 512
W_FOX, W_MIX, W_GATES = 0, 1536, 5120
N_FOX, N_MIXC, N_GATES = 1536, 3584, 8192
N_WALL = W_GATES + N_GATES
SHIFT_G = O_GQ - W_FOX - N_FOX
SHIFT_M = O_GR - (W_MIX + 1024)
C_GQ, C_GK, C_GV, C_GR, C_CB, C_CC, C_CV, C_PZ = 0, 256, 512, 1024, 1536, 2048, 2560, 3072
SM_FF_TILE, SM_GA_TILE = O_FF // LANE, O_GA // LANE
SM_FF = O_FF - SM_FF_TILE * LANE
SM_GA = O_GA - SM_GA_TILE * LANE

EXPERT_BLOCK = 128
N_FLAT = N_TOK * TOP_K
N_BLOCKS = -(-N_FLAT // EXPERT_BLOCK) + N_EXPERTS
N_ROWS = N_BLOCKS * EXPERT_BLOCK

NEG = -1e30
VMEM_LIMIT = 48 * 1024 * 1024
MOE_VMEM_LIMIT = 56 * 1024 * 1024


def _cparams(sem, vmem=VMEM_LIMIT):
    return pltpu.CompilerParams(dimension_semantics=sem, vmem_limit_bytes=vmem)


def _log_sigmoid(x):
    return jnp.minimum(x, 0.0) - jnp.log1p(jnp.exp(-jnp.abs(x)))


def _sigmoid(x):
    return 1.0 / (1.0 + jnp.exp(-x))


def _layer_norm_rows(x, g, b):
    mu = jnp.mean(x, axis=-1, keepdims=True)
    xc = x - mu
    var = jnp.mean(xc * xc, axis=-1, keepdims=True)
    return xc * lax.rsqrt(var + LN_EPS) * g + b


def _ln_in_kernel(x_ref, meta_ref, g_ref, b_ref, h_ref, hb_ref):
    i = pl.program_id(0)

    @pl.when(i == 0)
    def _():
        h_ref[...] = jnp.zeros_like(h_ref)
        hb_ref[...] = jnp.zeros_like(hb_ref)
        m = _layer_norm_rows(meta_ref[...], g_ref[...], b_ref[...])
        h_ref[PAD_ROWS:, :] = m
        hb_ref[PAD_ROWS:, :] = m.astype(BF16)

    @pl.when(i > 0)
    def _():
        y = _layer_norm_rows(x_ref[...], g_ref[...], b_ref[...])
        h_ref[...] = y
        hb_ref[...] = y.astype(BF16)


def _ln_in(x2d, meta, g, b):
    nb = R // LANE
    return pl.pallas_call(
        _ln_in_kernel,
        grid=(nb,),
        in_specs=[
            pl.BlockSpec((LANE, D_MODEL), lambda i: (jnp.maximum(i - 1, 0), 0)),
            pl.BlockSpec((N_META, D_MODEL), lambda i: (0, 0)),
            pl.BlockSpec((1, D_MODEL), lambda i: (0, 0)),
            pl.BlockSpec((1, D_MODEL), lambda i: (0, 0)),
        ],
        out_specs=[
            pl.BlockSpec((LANE, D_MODEL), lambda i: (i, 0)),
            pl.BlockSpec((LANE, D_MODEL), lambda i: (i, 0)),
        ],
        out_shape=[
            jax.ShapeDtypeStruct((R, D_MODEL), F32),
            jax.ShapeDtypeStruct((R, D_MODEL), BF16),
        ],
        compiler_params=_cparams(("arbitrary",)),
        name="ln_in",
    )(x2d, meta, g.reshape(1, -1), b.reshape(1, -1))


TR_PREP = 1024


def _wprep_kernel(a_ref, b_ref, o_ref):
    j = pl.program_id(1)

    def emit(shift):
        if shift == 0:
            o_ref[...] = a_ref[...].astype(BF16)
        else:
            x = jnp.concatenate([a_ref[...], b_ref[...]], axis=1)
            o_ref[...] = pltpu.roll(x, WT + LANE - shift, axis=1)[:, :WT].astype(BF16)

    first_g = N_FOX // WT
    first_m = (W_MIX + 1024) // WT

    @pl.when(j < first_g)
    def _():
        emit(0)

    @pl.when((j >= first_g) & (j < first_m))
    def _():
        emit(SHIFT_G)

    @pl.when(j >= first_m)
    def _():
        emit(SHIFT_M)


def _prep_w_in(w_in, l):
    return pl.pallas_call(
        _wprep_kernel,
        grid=(D_MODEL // TR_PREP, N_WALL // WT),
        in_specs=[
            pl.BlockSpec((None, TR_PREP, WT), lambda i, j: (l, i, j)),
            pl.BlockSpec((None, TR_PREP, LANE), lambda i, j: (l, i, (j + 1) * (WT // LANE))),
        ],
        out_specs=pl.BlockSpec((TR_PREP, WT), lambda i, j: (i, j)),
        out_shape=jax.ShapeDtypeStruct((D_MODEL, N_WALL), BF16),
        compiler_params=_cparams(("parallel", "arbitrary")),
        name="prep_w_in",
    )(w_in, w_in)


def _mm_kernel(a_ref, w_ref, o_ref):
    o_ref[...] = jnp.dot(a_ref[...], w_ref[...], preferred_element_type=F32).astype(o_ref.dtype)


def _matmul(a, w, col0, n, out_dtype, tm, tn, name):
    m, k = a.shape
    return pl.pallas_call(
        _mm_kernel,
        grid=(m // tm, n // tn),
        in_specs=[
            pl.BlockSpec((tm, k), lambda i, j: (i, 0)),
            pl.BlockSpec((k, tn), lambda i, j: (0, col0 // tn + j)),
        ],
        out_specs=pl.BlockSpec((tm, tn), lambda i, j: (i, j)),
        out_shape=jax.ShapeDtypeStruct((m, n), out_dtype),
        compiler_params=_cparams(("parallel", "arbitrary")),
        name=name,
    )(a, w)


def _mm_nt_kernel(w_ref, a_ref, o_ref):
    o_ref[...] = lax.dot_general(w_ref[...], a_ref[...], (((1,), (1,)), ((), ())),
                                 preferred_element_type=F32).astype(o_ref.dtype)


def _matmul_nt(w_t, a, out_dtype, tm, name):
    n, k = w_t.shape
    m = a.shape[0]
    return pl.pallas_call(
        _mm_nt_kernel,
        grid=(m // tm,),
        in_specs=[
            pl.BlockSpec((n, k), lambda i: (0, 0)),
            pl.BlockSpec((tm, k), lambda i: (i, 0)),
        ],
        out_specs=pl.BlockSpec((n, tm), lambda i: (0, i)),
        out_shape=jax.ShapeDtypeStruct((n, m), out_dtype),
        compiler_params=_cparams(("parallel",)),
        name=name,
    )(w_t, a)


def _proj_small_kernel(a_ref, w_ref, o_ref):
    o_ref[...] = jnp.dot(a_ref[...], w_ref[...].astype(BF16), preferred_element_type=F32)


def _proj_small(hb, w_in, l):
    return pl.pallas_call(
        _proj_small_kernel,
        grid=(R // TM_PROJ, 2),
        in_specs=[
            pl.BlockSpec((TM_PROJ, D_MODEL), lambda i, j: (i, 0)),
            pl.BlockSpec((None, D_MODEL, LANE),
                         lambda i, j: (l, 0, SM_FF_TILE + j * (SM_GA_TILE - SM_FF_TILE))),
        ],
        out_specs=pl.BlockSpec((TM_PROJ, LANE), lambda i, j: (i, j)),
        out_shape=jax.ShapeDtypeStruct((R, 2 * LANE), F32),
        compiler_params=_cparams(("parallel", "arbitrary")),
        name="proj_small",
    )(hb, w_in)


def _fox_gate_kernel(zs_ref, bias_ref, c_ref, carry_ref):
    i = pl.program_id(0)

    @pl.when(i == 0)
    def _():
        carry_ref[...] = jnp.zeros_like(carry_ref)

    rows = i * TM + lax.broadcasted_iota(I32, (TM, LANE), 0)
    lf = _log_sigmoid(zs_ref[...] + bias_ref[...])
    lf = jnp.where(rows >= T0, lf, 0.0)
    tri = (lax.broadcasted_iota(I32, (TM, TM), 0)
           >= lax.broadcasted_iota(I32, (TM, TM), 1)).astype(F32)
    c = jnp.dot(tri, lf, precision=HIGHEST, preferred_element_type=F32) + carry_ref[...]
    c_ref[...] = c
    carry_ref[...] = c[TM - 1:TM, :]


def _fox_gate(zs, bias_row):
    return pl.pallas_call(
        _fox_gate_kernel,
        grid=(R // TM,),
        in_specs=[
            pl.BlockSpec((TM, LANE), lambda i: (i, 0)),
            pl.BlockSpec((1, LANE), lambda i: (0, 0)),
        ],
        out_specs=pl.BlockSpec((TM, LANE), lambda i: (i, 0)),
        out_shape=jax.ShapeDtypeStruct((R, LANE), F32),
        scratch_shapes=[pltpu.VMEM((1, LANE), F32)],
        compiler_params=_cparams(("arbitrary",)),
        name="fox_gate",
    )(zs, bias_row)


TQ = TM
N_QB = R // TQ
_PAIRS = [(qi, kj) for qi in range(N_QB) for kj in range(qi + 1)]
N_PAIRS = len(_PAIRS)


LOG2E = 1.4426950408889634


def _fox_kernel(qi_tab, kj_tab, q_ref, k_ref, vt_ref, ck_ref, o_ref, m_sc, l_sc, acc_sc):
    p = pl.program_id(1)
    qi = qi_tab[p]
    kj = kj_tab[p]

    @pl.when(kj == 0)
    def _():
        m_sc[...] = jnp.full_like(m_sc, NEG)
        l_sc[...] = jnp.zeros_like(l_sc)
        acc_sc[...] = jnp.zeros_like(acc_sc)

    kpos = kj * TQ + lax.broadcasted_iota(I32, (TQ, 1), 0)
    ckl = jnp.where(kpos >= T0, ck_ref[...] * LOG2E, -NEG)
    c1 = FOX_HD ** -0.5 * LOG2E

    def step(causal):
        t = lax.dot_general(k_ref[...], q_ref[...], (((1,), (1,)), ((), ())),
                            preferred_element_type=F32) * c1 - ckl
        if causal:
            ahead = (lax.broadcasted_iota(I32, (TQ, TQ), 0)
                     - lax.broadcasted_iota(I32, (TQ, TQ), 1))
            t = jnp.where(ahead <= 0, t, NEG)
        m_prev = m_sc[...]
        m_new = jnp.maximum(m_prev, jnp.max(t, axis=0, keepdims=True))
        alpha = jnp.exp2(m_prev - m_new)
        pr = jnp.exp2(t - m_new)
        l_sc[...] = alpha * l_sc[...] + jnp.sum(pr, axis=0, keepdims=True)
        acc_sc[...] = alpha * acc_sc[...] + jnp.dot(vt_ref[...], pr.astype(BF16),
                                                    preferred_element_type=F32)
        m_sc[...] = m_new

    @pl.when(kj == qi)
    def _():
        step(True)

    @pl.when(kj != qi)
    def _():
        step(False)

    @pl.when(kj == qi)
    def _():
        o_ref[...] = (acc_sc[...] / l_sc[...]).T.astype(o_ref.dtype)


def _fox_attention(zf, vt, c_col):
    qi_tab = jnp.asarray([p[0] for p in _PAIRS], I32)
    kj_tab = jnp.asarray([p[1] for p in _PAIRS], I32)
    grid_spec = pltpu.PrefetchScalarGridSpec(
        num_scalar_prefetch=2,
        grid=(FOX_HEADS, N_PAIRS),
        in_specs=[
            pl.BlockSpec((TQ, FOX_HD), lambda h, p, qt, kt: (qt[p], h)),
            pl.BlockSpec((TQ, FOX_HD), lambda h, p, qt, kt: (kt[p], FOX_HEADS + h)),
            pl.BlockSpec((FOX_HD, TQ), lambda h, p, qt, kt: (h, kt[p])),
            pl.BlockSpec((None, TQ, 1), lambda h, p, qt, kt: (h, kt[p], 0)),
        ],
        out_specs=pl.BlockSpec((TQ, FOX_HD), lambda h, p, qt, kt: (qt[p], h)),
        scratch_shapes=[
            pltpu.VMEM((1, TQ), F32),
            pltpu.VMEM((1, TQ), F32),
            pltpu.VMEM((FOX_HD, TQ), F32),
        ],
    )
    return pl.pallas_call(
        _fox_kernel,
        grid_spec=grid_spec,
        out_shape=jax.ShapeDtypeStruct((R, MIX_W), BF16),
        compiler_params=_cparams(("parallel", "arbitrary")),
        name="fox_attention",
    )(qi_tab, kj_tab, zf, zf, vt, c_col)


GLA_CHUNK = 64


def _gla_kernel(q_ref, k_ref, v_ref, gr_ref, zs_ref, wa2_ref, ba_ref, gn_ref, o_ref, st_ref, la_ref):
    i = pl.program_id(0)

    @pl.when(i == 0)
    def _():
        st_ref[...] = jnp.zeros_like(st_ref)

    la = jnp.dot(zs_ref[...], wa2_ref[...], precision=HIGHEST, preferred_element_type=F32)
    la_ref[...] = _log_sigmoid(la + ba_ref[...]) * (1.0 / GLA_TAU)

    c_r = lax.broadcasted_iota(I32, (GLA_CHUNK, GLA_CHUNK), 0)
    c_c = lax.broadcasted_iota(I32, (GLA_CHUNK, GLA_CHUNK), 1)
    tri_b = c_r >= c_c
    tri = tri_b.astype(F32)

    def chunk(c, carry):
        r0 = pl.multiple_of(c * GLA_CHUNK, GLA_CHUNK)
        rows = pl.ds(r0, GLA_CHUNK)
        g = la_ref[rows, :]
        b = jnp.dot(tri, g, precision=HIGHEST, preferred_element_type=F32)
        b_last = b[GLA_CHUNK - 1:GLA_CHUNK, :]
        e_last = jnp.exp(b_last)
        qt = q_ref[rows, :] * (GLA_DK ** -0.5) * jnp.exp(b)
        kt = k_ref[rows, :] * jnp.exp(-b)
        kh = kt * e_last
        for h in range(GLA_HEADS):
            ks = slice(h * GLA_DK, (h + 1) * GLA_DK)
            vs = slice(h * GLA_DV, (h + 1) * GLA_DV)
            q_h = qt[:, ks].astype(BF16)
            k_h = kt[:, ks].astype(BF16)
            kh_h = kh[:, ks].astype(BF16)
            v_h = v_ref[rows, vs]
            att = lax.dot_general(q_h, k_h, (((1,), (1,)), ((), ())), preferred_element_type=F32)
            att = jnp.where(tri_b, att, 0.0)
            st = st_ref[h]
            o = jnp.dot(att.astype(BF16), v_h.astype(BF16), preferred_element_type=F32)
            o = o + lax.dot_general(q_h, st.astype(BF16), (((1,), (1,)), ((), ())),
                                    preferred_element_type=F32)
            st_ref[h] = st * e_last[:, ks] + jnp.dot(v_h.T.astype(BF16), kh_h,
                                                     preferred_element_type=F32)
            ms = jnp.mean(o * o, axis=-1, keepdims=True)
            on = o * lax.rsqrt(ms + LN_EPS) * gn_ref[:, vs]
            gate = gr_ref[rows, vs]
            o_ref[rows, vs] = (on * (gate * _sigmoid(gate))).astype(o_ref.dtype)
        return carry

    lax.fori_loop(0, TM // GLA_CHUNK, chunk, 0)


def _gla(z, zs, wa2p, ba, gn):
    return pl.pallas_call(
        _gla_kernel,
        grid=(R // TM,),
        in_specs=[
            pl.BlockSpec((TM, 256), lambda i: (i, C_GQ // 256)),
            pl.BlockSpec((TM, 256), lambda i: (i, C_GK // 256)),
            pl.BlockSpec((TM, 512), lambda i: (i, C_GV // 512)),
            pl.BlockSpec((TM, 512), lambda i: (i, C_GR // 512)),
            pl.BlockSpec((TM, LANE), lambda i: (i, 1)),
            pl.BlockSpec((LANE, 256), lambda i: (0, 0)),
            pl.BlockSpec((1, 256), lambda i: (0, 0)),
            pl.BlockSpec((1, 512), lambda i: (0, 0)),
        ],
        out_specs=pl.BlockSpec((TM, MIX_W), lambda i: (i, 0)),
        out_shape=jax.ShapeDtypeStruct((R, MIX_W), BF16),
        scratch_shapes=[
            pltpu.VMEM((GLA_HEADS, GLA_DV, GLA_DK), F32),
            pltpu.VMEM((TM, GLA_HEADS * GLA_DK), F32),
        ],
        compiler_params=_cparams(("arbitrary",)),
        name="gla",
    )(z, z, z, z, zs, wa2p, ba, gn)


def _local_kernel(cb_ref, cc_ref, cv_ref, pz_ref, cw_ref, pw_ref, ps_ref, oc_ref, od_ref, u_sc, p_sc):
    i = pl.program_id(0)

    @pl.when(i == 0)
    def _():
        u_sc[0:HALO, :] = jnp.zeros((HALO, MIX_W), F32)
        p_sc[0:HALO, :] = jnp.zeros((HALO, MIX_W), F32)

    @pl.when(i > 0)
    def _():
        u_sc[0:HALO, :] = u_sc[TM:TM + HALO, :]
        p_sc[0:HALO, :] = p_sc[TM:TM + HALO, :]

    u = cc_ref[...] * cv_ref[...]
    pz = pz_ref[...]
    u_sc[HALO:, :] = u
    p_sc[HALO:, :] = pz

    y = (cw_ref[2:3, :] * u + cw_ref[1:2, :] * u_sc[HALO - 1:HALO - 1 + TM, :]
         + cw_ref[0:1, :] * u_sc[HALO - 2:HALO - 2 + TM, :])
    oc_ref[...] = (cb_ref[...] * y).astype(oc_ref.dtype)

    tok = i * TM - T0 + lax.broadcasted_iota(I32, (TM, 1), 0)
    cnt_small = jnp.maximum(tok + 1, 1).astype(F32)
    for g, w in enumerate(POOL_WINDOWS):
        cols = slice(g * POOL_GW, (g + 1) * POOL_GW)
        x = pz[:, cols]
        s = x
        for j in range(1, w):
            s = s + p_sc[HALO - j:HALO - j + TM, cols]
        inv_cnt = jnp.where(tok + 1 >= w, 1.0 / w, 1.0 / cnt_small)
        pooled = s * inv_cnt - x
        od = jnp.dot(pooled.astype(BF16), pw_ref[g], preferred_element_type=F32)
        od_ref[:, cols] = (od * ps_ref[:, cols]).astype(od_ref.dtype)


def _local_mixers(z, conv_w, pool_w_bf, pool_scale):
    cw = jnp.zeros((8, MIX_W), F32).at[:CONV_K].set(conv_w)
    blk = lambda c: pl.BlockSpec((TM, MIX_W), lambda i, c=c: (i, c // MIX_W))
    return pl.pallas_call(
        _local_kernel,
        grid=(R // TM,),
        in_specs=[
            blk(C_CB), blk(C_CC), blk(C_CV), blk(C_PZ),
            pl.BlockSpec((8, MIX_W), lambda i: (0, 0)),
            pl.BlockSpec((len(POOL_WINDOWS), POOL_GW, POOL_GW), lambda i: (0, 0, 0)),
            pl.BlockSpec((1, MIX_W), lambda i: (0, 0)),
        ],
        out_specs=[
            pl.BlockSpec((TM, MIX_W), lambda i: (i, 0)),
            pl.BlockSpec((TM, MIX_W), lambda i: (i, 0)),
        ],
        out_shape=[
            jax.ShapeDtypeStruct((R, MIX_W), BF16),
            jax.ShapeDtypeStruct((R, MIX_W), BF16),
        ],
        scratch_shapes=[
            pltpu.VMEM((TM + HALO, MIX_W), F32),
            pltpu.VMEM((TM + HALO, MIX_W), F32),
        ],
        compiler_params=_cparams(("arbitrary",)),
        name="conv_pool",
    )(z, z, z, z, cw, pool_w_bf, pool_scale.reshape(1, -1))


TN_MERGE = 256


def _merge_kernel(hb_ref, oa_ref, ob_ref, oc_ref, od_ref, wg0_ref, wg1_ref, wg2_ref, wg3_ref,
                  gb_ref, wb_ref, wo_ref, out_ref):
    j = pl.program_id(1)

    @pl.when(j == 0)
    def _():
        out_ref[...] = jnp.zeros_like(out_ref)

    hb = hb_ref[...]
    mixed = None
    for b, (o_ref, wg_ref) in enumerate(((oa_ref, wg0_ref), (ob_ref, wg1_ref),
                                         (oc_ref, wg2_ref), (od_ref, wg3_ref))):
        gate = _sigmoid(jnp.dot(hb, wg_ref[...], preferred_element_type=F32) + gb_ref[b:b + 1, :])
        proj = jnp.dot(o_ref[...], wb_ref[b], preferred_element_type=F32)
        term = gate * proj
        mixed = term if mixed is None else mixed + term
    out_ref[...] += jnp.dot(mixed.astype(BF16), wo_ref[...], preferred_element_type=F32)


def _merge(hb, o_a, o_b, o_c, o_d, w_all, gate_b, wb_bf, wo_bf, l):
    tn = TN_MERGE
    nj = D_MODEL // tn
    row = lambda w: pl.BlockSpec((TM, w), lambda i, j: (i, 0))
    wg = lambda b: pl.BlockSpec((D_MODEL, tn), lambda i, j, b=b: (0, W_GATES // tn + b * nj + j))
    return pl.pallas_call(
        _merge_kernel,
        grid=(R // TM, nj),
        in_specs=[
            row(D_MODEL), row(MIX_W), row(MIX_W), row(MIX_W), row(MIX_W),
            wg(0), wg(1), wg(2), wg(3),
            pl.BlockSpec((N_BRANCH, tn), lambda i, j: (0, j)),
            pl.BlockSpec((None, N_BRANCH, MIX_W, tn), lambda i, j: (l, 0, 0, j)),
            pl.BlockSpec((None, tn, D_MODEL), lambda i, j: (l, j, 0)),
        ],
        out_specs=pl.BlockSpec((TM, D_MODEL), lambda i, j: (i, 0)),
        out_shape=jax.ShapeDtypeStruct((R, D_MODEL), F32),
        compiler_params=_cparams(("parallel", "arbitrary")),
        name="merge",
    )(hb, o_a, o_b, o_c, o_d, w_all, w_all, w_all, w_all, gate_b, wb_bf, wo_bf)


def _post_ln(h, delta, g, b, row0):
    y = _layer_norm_rows(DEEPNORM_ALPHA * h + delta, g, b)
    rows = row0 + lax.broadcasted_iota(I32, (y.shape[0], 1), 0)
    return jnp.where(rows >= T0, y, 0.0)


def _first_of(cands, target):
    idx = jnp.full(target.shape, len(cands) - 1, I32)
    for j in range(len(cands) - 2, -1, -1):
        idx = jnp.where(cands[j] == target, j, idx)
    return idx


def _pick(cands, idx):
    out = cands[-1]
    for j in range(len(cands) - 2, -1, -1):
        out = jnp.where(idx == j, cands[j], out)
    return out


def _ln1_route_kernel(h_ref, mix_ref, g_ref, b_ref, rwt_ref, rb_ref,
                      h1_ref, mi_ref, mf_ref, cnt_ref, carry_sc):
    i = pl.program_id(0)

    @pl.when(i == 0)
    def _():
        carry_sc[...] = jnp.zeros_like(carry_sc)

    y = _post_ln(h_ref[...], mix_ref[...], g_ref[...], b_ref[...], i * TM)
    h1_ref[...] = y

    logits = lax.dot_general(rwt_ref[...], y, (((1,), (1,)), ((), ())), precision=HIGHEST,
                             preferred_element_type=F32)
    aff = _sigmoid(logits)
    sel = aff + rb_ref[...]
    xs = [sel[j * N_GROUPS:(j + 1) * N_GROUPS, :] for j in range(EXPERTS_PER_GROUP)]
    afs = [aff[j * N_GROUPS:(j + 1) * N_GROUPS, :] for j in range(EXPERTS_PER_GROUP)]

    score = None
    for a in range(EXPERTS_PER_GROUP):
        for bb in range(a + 1, EXPERTS_PER_GROUP):
            pair = xs[a] + xs[bb]
            score = pair if score is None else jnp.maximum(score, pair)
    giota = lax.broadcasted_iota(I32, (N_GROUPS, TM), 0)
    gmax = jnp.max(score, axis=0, keepdims=True)
    grp = jnp.min(jnp.where(score == gmax, giota, N_GROUPS), axis=0, keepdims=True)
    gsel = giota == grp
    cs = [jnp.max(jnp.where(gsel, x, -jnp.inf), axis=0, keepdims=True) for x in xs]
    acs = [jnp.sum(jnp.where(gsel, a, 0.0), axis=0, keepdims=True) for a in afs]

    m1 = jnp.maximum(jnp.maximum(cs[0], cs[1]), jnp.maximum(cs[2], cs[3]))
    i0 = _first_of(cs, m1)
    ds = [jnp.where(i0 == j, -jnp.inf, cs[j]) for j in range(EXPERTS_PER_GROUP)]
    m2 = jnp.maximum(jnp.maximum(ds[0], ds[1]), jnp.maximum(ds[2], ds[3]))
    i1 = _first_of(ds, m2)
    a0 = _pick(acs, i0)
    a1 = _pick(acs, i1)
    denom = a0 + a1

    pos = i * TM + lax.broadcasted_iota(I32, (1, TM), 1)
    valid = pos >= T0
    riota = lax.broadcasted_iota(I32, (N_EXPERTS, TM), 0)
    oh0 = (riota == i0 * N_GROUPS + grp) & valid
    oh1 = (riota == i1 * N_GROUPS + grp) & valid
    ohf = jnp.where(oh0 | oh1, 1.0, 0.0)
    before = (lax.broadcasted_iota(I32, (TM, TM), 0)
              < lax.broadcasted_iota(I32, (TM, TM), 1)).astype(BF16)
    cum = jnp.dot(ohf.astype(BF16), before, preferred_element_type=F32) + carry_sc[...]
    rank0 = jnp.sum(jnp.where(oh0, cum, 0.0), axis=0, keepdims=True)
    rank1 = jnp.sum(jnp.where(oh1, cum, 0.0), axis=0, keepdims=True)
    carry = carry_sc[...] + jnp.sum(ohf, axis=1, keepdims=True)
    carry_sc[...] = carry
    cnt_ref[...] = jnp.broadcast_to(carry, cnt_ref.shape)

    zi = jnp.zeros((1, TM), I32)
    mi_ref[...] = jnp.concatenate(
        [grp * EXPERTS_PER_GROUP + i0, grp * EXPERTS_PER_GROUP + i1,
         rank0.astype(I32), rank1.astype(I32), zi, zi, zi, zi], axis=0)
    zf = jnp.zeros((1, TM), F32)
    mf_ref[...] = jnp.concatenate([a0 / denom, a1 / denom, zf, zf, zf, zf, zf, zf], axis=0)


def _ln1_route(h, mix, g, b, router_wt, router_bc):
    row = pl.BlockSpec((TM, D_MODEL), lambda i: (i, 0))
    vec = pl.BlockSpec((1, D_MODEL), lambda i: (0, 0))
    meta = pl.BlockSpec((8, TM), lambda i: (0, i))
    return pl.pallas_call(
        _ln1_route_kernel,
        grid=(R // TM,),
        in_specs=[row, row, vec, vec,
                  pl.BlockSpec((N_EXPERTS, D_MODEL), lambda i: (0, 0)),
                  pl.BlockSpec((N_EXPERTS, 1), lambda i: (0, 0))],
        out_specs=[row, meta, meta, pl.BlockSpec((N_EXPERTS, LANE), lambda i: (0, 0))],
        out_shape=[
            jax.ShapeDtypeStruct((R, D_MODEL), F32),
            jax.ShapeDtypeStruct((8, R), I32),
            jax.ShapeDtypeStruct((8, R), F32),
            jax.ShapeDtypeStruct((N_EXPERTS, LANE), F32),
        ],
        scratch_shapes=[pltpu.VMEM((N_EXPERTS, 1), F32)],
        compiler_params=_cparams(("arbitrary",)),
        name="ln1_route",
    )(h, mix, g.reshape(1, -1), b.reshape(1, -1), router_wt, router_bc)


def _dispatch_tables(mi, counts_slot_major):
    counts = counts_slot_major.reshape(EXPERTS_PER_GROUP, N_GROUPS).T.reshape(N_EXPERTS).astype(I32)
    padded = (counts + EXPERT_BLOCK - 1) // EXPERT_BLOCK * EXPERT_BLOCK
    pad_end = jnp.cumsum(padded)
    pad_start = pad_end - padded
    e_iota = jnp.arange(N_EXPERTS, dtype=I32)
    rows_ok = jnp.arange(R) >= T0

    def dest(eid, rank):
        start = jnp.sum(jnp.where(eid[:, None] == e_iota[None, :], pad_start[None, :], 0), axis=1)
        return jnp.where(rows_ok, start + rank, 0).astype(I32)

    d0 = dest(mi[0], mi[2])
    d1 = dest(mi[1], mi[3])
    blk_start = (pad_start // EXPERT_BLOCK).astype(I32)
    n_blk = (padded // EXPERT_BLOCK).astype(I32)
    n_used = (pad_end[-1] // EXPERT_BLOCK).astype(I32).reshape(1)
    tok_rows = jnp.arange(T0, R, dtype=I32)
    row_src = jnp.zeros((N_ROWS + EXPERT_BLOCK,), I32).at[jnp.concatenate([d0[T0:], d1[T0:]])].set(
        jnp.concatenate([tok_rows, tok_rows]), unique_indices=True)
    return d0, d1, blk_start, n_blk, n_used, row_src


def _row_copy(src, src_row, dst, dst_row, sem):
    return pltpu.make_async_copy(src.at[pl.ds(src_row, 1), :], dst.at[pl.ds(dst_row, 1), :], sem)


def _issue_row_gather(rs_ref, g, h_hbm, buf, sem):
    base = g * EXPERT_BLOCK
    for r in range(EXPERT_BLOCK):
        _row_copy(h_hbm, rs_ref[base + r], buf, r, sem).start(priority=1)


def _wait_row_gather(h_hbm, buf, sem):
    for r in range(EXPERT_BLOCK):
        _row_copy(h_hbm, 0, buf, r, sem).wait()


def _block_rows(g):
    return pl.ds(pl.multiple_of(g * EXPERT_BLOCK, EXPERT_BLOCK), EXPERT_BLOCK)


def _finish_writes(out_copy, obuf, nu):
    @pl.when(nu >= 2)
    def _():
        out_copy(nu - 2, nu % 2).wait()

    @pl.when(nu >= 1)
    def _():
        out_copy(nu - 1, (nu - 1) % 2).wait()

    obuf[0] = jnp.zeros(obuf.shape[1:], obuf.dtype)

    def zero_block(g, carry):
        cp = out_copy(g, 0)
        cp.start()
        cp.wait()
        return carry

    lax.fori_loop(nu, N_BLOCKS, zero_block, 0)


def _moe_up_kernel(bs_ref, nb_ref, nu_ref, rs_ref, h_hbm, wg_ref, wu_ref, o_hbm,
                   wg_sc, wu_sc, xbuf, obuf, xsem, osem):
    e = pl.program_id(0)
    nb = nb_ref[e]
    g0 = bs_ref[e]

    def out_copy(g, slot):
        return pltpu.make_async_copy(obuf.at[slot], o_hbm.at[_block_rows(g), :], osem.at[slot])

    @pl.when(e == 0)
    def _():
        _issue_row_gather(rs_ref, 0, h_hbm, xbuf.at[0], xsem.at[0])

    @pl.when(nb > 0)
    def _():
        wg_sc[...] = wg_ref[...].astype(BF16)
        wu_sc[...] = wu_ref[...].astype(BF16)

        def block(j, carry):
            g = g0 + j
            slot = g % 2

            @pl.when(g >= 2)
            def _():
                out_copy(g - 2, slot).wait()

            _wait_row_gather(h_hbm, xbuf.at[slot], xsem.at[slot])
            _issue_row_gather(rs_ref, g + 1, h_hbm, xbuf.at[1 - slot], xsem.at[1 - slot])
            x = xbuf[slot].astype(BF16)
            gate = jnp.dot(x, wg_sc[...], preferred_element_type=F32)
            up = jnp.dot(x, wu_sc[...], preferred_element_type=F32)
            obuf[slot] = (gate * _sigmoid(gate) * up).astype(BF16)
            out_copy(g, slot).start()
            return carry

        lax.fori_loop(0, nb, block, 0)

    @pl.when(e == N_EXPERTS - 1)
    def _():
        nu = nu_ref[0]
        _wait_row_gather(h_hbm, xbuf.at[nu % 2], xsem.at[nu % 2])
        _finish_writes(out_copy, obuf, nu)


def _moe_up(blk_start, n_blk, n_used, row_src, h1, w_gate, w_up, l):
    wspec = pl.BlockSpec((None, None, D_MODEL, D_EXPERT), lambda e, *_: (l, e, 0, 0))
    grid_spec = pltpu.PrefetchScalarGridSpec(
        num_scalar_prefetch=4,
        grid=(N_EXPERTS,),
        in_specs=[pl.BlockSpec(memory_space=pl.ANY), wspec, wspec],
        out_specs=pl.BlockSpec(memory_space=pl.ANY),
        scratch_shapes=[
            pltpu.VMEM((D_MODEL, D_EXPERT), BF16),
            pltpu.VMEM((D_MODEL, D_EXPERT), BF16),
            pltpu.VMEM((2, EXPERT_BLOCK, D_MODEL), F32),
            pltpu.VMEM((2, EXPERT_BLOCK, D_EXPERT), BF16),
            pltpu.SemaphoreType.DMA((2,)),
            pltpu.SemaphoreType.DMA((2,)),
        ],
    )
    return pl.pallas_call(
        _moe_up_kernel,
        grid_spec=grid_spec,
        out_shape=jax.ShapeDtypeStruct((N_ROWS, D_EXPERT), BF16),
        compiler_params=_cparams(("arbitrary",), vmem=MOE_VMEM_LIMIT),
        name="moe_up",
    )(blk_start, n_blk, n_used, row_src, h1, w_gate, w_up)


def _moe_down_kernel(bs_ref, nb_ref, nu_ref, x_hbm, wd_ref, y_hbm, wd_sc, xbuf, obuf, xsem, osem):
    e = pl.program_id(0)
    nb = nb_ref[e]
    g0 = bs_ref[e]
    nu = nu_ref[0]

    def in_copy(g, slot):
        return pltpu.make_async_copy(x_hbm.at[_block_rows(g), :], xbuf.at[slot], xsem.at[slot])

    def out_copy(g, slot):
        return pltpu.make_async_copy(obuf.at[slot], y_hbm.at[_block_rows(g), :], osem.at[slot])

    @pl.when((e == 0) & (nu > 0))
    def _():
        in_copy(0, 0).start()

    @pl.when(nb > 0)
    def _():
        wd_sc[...] = wd_ref[...].astype(BF16)

        def block(j, carry):
            g = g0 + j
            slot = g % 2

            @pl.when(g >= 2)
            def _():
                out_copy(g - 2, slot).wait()

            in_copy(g, slot).wait()

            @pl.when(g + 1 < nu)
            def _():
                in_copy(g + 1, 1 - slot).start()

            obuf[slot] = jnp.dot(xbuf[slot], wd_sc[...], preferred_element_type=F32)
            out_copy(g, slot).start()
            return carry

        lax.fori_loop(0, nb, block, 0)

    @pl.when(e == N_EXPERTS - 1)
    def _():
        _finish_writes(out_copy, obuf, nu)


def _moe_down(blk_start, n_blk, n_used, hmid, w_down, l):
    grid_spec = pltpu.PrefetchScalarGridSpec(
        num_scalar_prefetch=3,
        grid=(N_EXPERTS,),
        in_specs=[
            pl.BlockSpec(memory_space=pl.ANY),
            pl.BlockSpec((None, None, D_EXPERT, D_MODEL), lambda e, *_: (l, e, 0, 0)),
        ],
        out_specs=pl.BlockSpec(memory_space=pl.ANY),
        scratch_shapes=[
            pltpu.VMEM((D_EXPERT, D_MODEL), BF16),
            pltpu.VMEM((2, EXPERT_BLOCK, D_EXPERT), BF16),
            pltpu.VMEM((2, EXPERT_BLOCK, D_MODEL), F32),
            pltpu.SemaphoreType.DMA((2,)),
            pltpu.SemaphoreType.DMA((2,)),
        ],
    )
    return pl.pallas_call(
        _moe_down_kernel,
        grid_spec=grid_spec,
        out_shape=jax.ShapeDtypeStruct((N_ROWS, D_MODEL), F32),
        compiler_params=_cparams(("arbitrary",), vmem=MOE_VMEM_LIMIT),
        name="moe_down",
    )(blk_start, n_blk, n_used, hmid, w_down)


def _combine_kernel(d0_ref, d1_ref, y_hbm, h1_ref, mf_ref, g_ref, b_ref, h2_ref, h2b_ref,
                    buf0, buf1, sem):
    i = pl.program_id(0)
    slot = i % 2

    def issue(tile, s):
        base = tile * LANE

        def body(r, carry):
            _row_copy(y_hbm, d0_ref[base + r], buf0.at[s], r, sem.at[s]).start(priority=0)
            _row_copy(y_hbm, d1_ref[base + r], buf1.at[s], r, sem.at[s]).start(priority=1)
            return carry

        lax.fori_loop(0, LANE, body, 0)

    @pl.when(i == 0)
    def _():
        issue(0, 0)

    @pl.when(i + 1 < pl.num_programs(0))
    def _():
        issue(i + 1, 1 - slot)

    def drain(r, carry):
        _row_copy(y_hbm, 0, buf0.at[slot], r, sem.at[slot]).wait()
        _row_copy(y_hbm, 0, buf1.at[slot], r, sem.at[slot]).wait()
        return carry

    lax.fori_loop(0, LANE, drain, 0)
    wt = mf_ref[...].T
    ffn = wt[:, 0:1] * buf0[slot] + wt[:, 1:2] * buf1[slot]
    y = _post_ln(h1_ref[...], ffn, g_ref[...], b_ref[...], i * LANE)
    h2_ref[...] = y
    h2b_ref[...] = y.astype(BF16)


def _combine_ln2(d0, d1, y_rows, h1, mf, g, b):
    row = lambda i, d0, d1: (i, 0)
    vec = pl.BlockSpec((1, D_MODEL), lambda i, d0, d1: (0, 0))
    grid_spec = pltpu.PrefetchScalarGridSpec(
        num_scalar_prefetch=2,
        grid=(R // LANE,),
        in_specs=[
            pl.BlockSpec(memory_space=pl.ANY),
            pl.BlockSpec((LANE, D_MODEL), row),
            pl.BlockSpec((8, LANE), lambda i, d0, d1: (0, i)),
            vec, vec,
        ],
        out_specs=[pl.BlockSpec((LANE, D_MODEL), row), pl.BlockSpec((LANE, D_MODEL), row)],
        scratch_shapes=[
            pltpu.VMEM((2, LANE, D_MODEL), F32),
            pltpu.VMEM((2, LANE, D_MODEL), F32),
            pltpu.SemaphoreType.DMA((2,)),
        ],
    )
    return pl.pallas_call(
        _combine_kernel,
        grid_spec=grid_spec,
        out_shape=[
            jax.ShapeDtypeStruct((R, D_MODEL), F32),
            jax.ShapeDtypeStruct((R, D_MODEL), BF16),
        ],
        compiler_params=_cparams(("arbitrary",)),
        name="moe_combine_ln2",
    )(d0, d1, y_rows, h1, mf, g.reshape(1, -1), b.reshape(1, -1))


def kernel(x, meta_tokens, ln_in_g, ln_in_b, w_in, fox_f_bias, gla_wa2, gla_ba, gla_norm_g, conv_w, pool_w, pool_scale, gate_b, w_branch, w_out, ln1_g, ln1_b, router_w, router_b, w_gate, w_up, w_down, ln2_g, ln2_b):
    assert x.shape == (1, SEQ, D_MODEL)
    h, hb = _ln_in(x.reshape(SEQ, D_MODEL), meta_tokens, ln_in_g, ln_in_b)
    router_wt = router_w.T.reshape(N_GROUPS, EXPERTS_PER_GROUP, D_MODEL).transpose(1, 0, 2).reshape(
        N_EXPERTS, D_MODEL)
    router_bc = router_b.astype(F32).reshape(N_GROUPS, EXPERTS_PER_GROUP).T.reshape(N_EXPERTS, 1)

    wb_bf = w_branch.astype(BF16)
    wo_bf = w_out.astype(BF16)

    for l in range(DEPTH):
        w_all = _prep_w_in(w_in, l)
        zf = _matmul(hb, w_all, W_FOX, 2 * MIX_W, BF16, TM_PROJ, 512, "proj_fox")
        wv_t = w_all[:, W_FOX + 2 * MIX_W:W_FOX + N_FOX].T
        vt = _matmul_nt(wv_t, hb, BF16, TM_PROJ, "proj_fox_vt")
        z = _matmul(hb, w_all, W_MIX, N_MIXC, F32, TM_PROJ, 512, "proj_mix")
        zs = _proj_small(hb, w_in, l)

        bias_row = jnp.zeros((1, LANE), F32).at[0, SM_FF:SM_FF + FOX_HEADS].set(fox_f_bias[l])
        c = _fox_gate(zs, bias_row)[:, SM_FF:SM_FF + FOX_HEADS]
        o_a = _fox_attention(zf, vt, c.T.reshape(FOX_HEADS, R, 1))

        wa2p = jnp.zeros((LANE, GLA_HEADS * GLA_DK), F32).at[SM_GA:SM_GA + GLA_RANK].set(gla_wa2[l])
        o_b = _gla(z, zs, wa2p, gla_ba[l].reshape(1, -1), gla_norm_g[l].reshape(1, -1))

        o_c, o_d = _local_mixers(z, conv_w[l], pool_w[l].astype(BF16), pool_scale[l])

        mix = _merge(hb, o_a, o_b, o_c, o_d, w_all, gate_b[l], wb_bf, wo_bf, l)
        h1, mi, mf, counts = _ln1_route(h, mix, ln1_g[l], ln1_b[l], router_wt, router_bc)

        d0, d1, blk_start, n_blk, n_used, row_src = _dispatch_tables(mi, counts[:, 0])
        hmid = _moe_up(blk_start, n_blk, n_used, row_src, h1, w_gate, w_up, l)
        y_rows = _moe_down(blk_start, n_blk, n_used, hmid, w_down, l)
        h, hb = _combine_ln2(d0, d1, y_rows, h1, mf, ln2_g[l], ln2_b[l])

    return h[PAD_ROWS + N_META:].reshape(1, SEQ, D_MODEL)
```

```python
import jax
import jax.numpy as jnp
import numpy as np
from jax import lax
from jax.experimental import pallas as pl
from jax.experimental.pallas import tpu as pltpu

F32 = jnp.float32
BF16 = jnp.bfloat16
I32 = jnp.int32
HIGHEST = lax.Precision.HIGHEST

D_MODEL = 2048
SEQ = 8192
DEPTH = 2
N_META = 16
N_BRANCH = 4
MIX_W = 512
FOX_HEADS = 4
FOX_HD = 128
GLA_HEADS = 4
GLA_DK = 64
GLA_DV = 128
GLA_RANK = 16
GLA_TAU = 16.0
CONV_K = 3
POOL_WINDOWS = (2, 4, 8, 16)
POOL_GW = 128
N_EXPERTS = 32
N_GROUPS = 8
EXPERTS_PER_GROUP = 4
TOP_K = 2
D_EXPERT = 1024
LN_EPS = 1e-5
DEEPNORM_ALPHA = (2 * DEPTH) ** 0.25

_SPLITS = (512, 512, 512, 4, 256, 256, 512, 16, 512, 512, 512, 512, 512, 8192)
_OFFS = [int(o) for o in np.concatenate([[0], np.cumsum(_SPLITS)])]
(O_FQ, O_FK, O_FV, O_FF, O_GQ, O_GK, O_GV, O_GA, O_GR, O_CB, O_CC, O_CV, O_PZ, O_GZ, P_IN) = _OFFS

LANE = 128
PAD_ROWS = LANE - N_META
T0 = PAD_ROWS
N_TOK = N_META + SEQ
R = PAD_ROWS + N_TOK
TM = 640
TM_PROJ = 1664
HALO = 16

WT = 512
W_FOX, W_MIX, W_GATES = 0, 1536, 5120
N_FOX, N_MIXC, N_GATES = 1536, 3584, 8192
N_WALL = W_GATES + N_GATES
SHIFT_G = O_GQ - W_FOX - N_FOX
SHIFT_M = O_GR - (W_MIX + 1024)
C_GQ, C_GK, C_GV, C_GR, C_CB, C_CC, C_CV, C_PZ = 0, 256, 512, 1024, 1536, 2048, 2560, 3072
SM_FF_TILE, SM_GA_TILE = O_FF // LANE, O_GA // LANE
SM_FF = O_FF - SM_FF_TILE * LANE
SM_GA = O_GA - SM_GA_TILE * LANE

EXPERT_BLOCK = 128
N_FLAT = N_TOK * TOP_K
N_BLOCKS = -(-N_FLAT // EXPERT_BLOCK) + N_EXPERTS
N_ROWS = N_BLOCKS * EXPERT_BLOCK

NEG = -1e30
VMEM_LIMIT = 48 * 1024 * 1024
MOE_VMEM_LIMIT = 56 * 1024 * 1024


def _cparams(sem, vmem=VMEM_LIMIT):
    return pltpu.CompilerParams(dimension_semantics=sem, vmem_limit_bytes=vmem)


def _log_sigmoid(x):
    return jnp.minimum(x, 0.0) - jnp.log1p(jnp.exp(-jnp.abs(x)))


def _sigmoid(x):
    return 1.0 / (1.0 + jnp.exp(-x))


def _layer_norm_rows(x, g, b):
    mu = jnp.mean(x, axis=-1, keepdims=True)
    xc = x - mu
    var = jnp.mean(xc * xc, axis=-1, keepdims=True)
    return xc * lax.rsqrt(var + LN_EPS) * g + b


def _ln_in_kernel(x_ref, meta_ref, g_ref, b_ref, h_ref, hb_ref):
    i = pl.program_id(0)

    @pl.when(i == 0)
    def _():
        h_ref[...] = jnp.zeros_like(h_ref)
        hb_ref[...] = jnp.zeros_like(hb_ref)
        m = _layer_norm_rows(meta_ref[...], g_ref[...], b_ref[...])
        h_ref[PAD_ROWS:, :] = m
        hb_ref[PAD_ROWS:, :] = m.astype(BF16)

    @pl.when(i > 0)
    def _():
        y = _layer_norm_rows(x_ref[...], g_ref[...], b_ref[...])
        h_ref[...] = y
        hb_ref[...] = y.astype(BF16)


def _ln_in(x2d, meta, g, b):
    nb = R // LANE
    return pl.pallas_call(
        _ln_in_kernel,
        grid=(nb,),
        in_specs=[
            pl.BlockSpec((LANE, D_MODEL), lambda i: (jnp.maximum(i - 1, 0), 0)),
            pl.BlockSpec((N_META, D_MODEL), lambda i: (0, 0)),
            pl.BlockSpec((1, D_MODEL), lambda i: (0, 0)),
            pl.BlockSpec((1, D_MODEL), lambda i: (0, 0)),
        ],
        out_specs=[
            pl.BlockSpec((LANE, D_MODEL), lambda i: (i, 0)),
            pl.BlockSpec((LANE, D_MODEL), lambda i: (i, 0)),
        ],
        out_shape=[
            jax.ShapeDtypeStruct((R, D_MODEL), F32),
            jax.ShapeDtypeStruct((R, D_MODEL), BF16),
        ],
        compiler_params=_cparams(("arbitrary",)),
        name="ln_in",
    )(x2d, meta, g.reshape(1, -1), b.reshape(1, -1))


TR_PREP = 1024


def _wprep_kernel(a_ref, b_ref, o_ref):
    j = pl.program_id(1)

    def emit(shift):
        if shift == 0:
            o_ref[...] = a_ref[...].astype(BF16)
        else:
            x = jnp.concatenate([a_ref[...], b_ref[...]], axis=1)
            o_ref[...] = pltpu.roll(x, WT + LANE - shift, axis=1)[:, :WT].astype(BF16)

    first_g = N_FOX // WT
    first_m = (W_MIX + 1024) // WT

    @pl.when(j < first_g)
    def _():
        emit(0)

    @pl.when((j >= first_g) & (j < first_m))
    def _():
        emit(SHIFT_G)

    @pl.when(j >= first_m)
    def _():
        emit(SHIFT_M)


def _prep_w_in(w_in, l):
    return pl.pallas_call(
        _wprep_kernel,
        grid=(D_MODEL // TR_PREP, N_WALL // WT),
        in_specs=[
            pl.BlockSpec((None, TR_PREP, WT), lambda i, j: (l, i, j)),
            pl.BlockSpec((None, TR_PREP, LANE), lambda i, j: (l, i, (j + 1) * (WT // LANE))),
        ],
        out_specs=pl.BlockSpec((TR_PREP, WT), lambda i, j: (i, j)),
        out_shape=jax.ShapeDtypeStruct((D_MODEL, N_WALL), BF16),
        compiler_params=_cparams(("parallel", "arbitrary")),
        name="prep_w_in",
    )(w_in, w_in)


def _mm_kernel(a_ref, w_ref, o_ref):
    o_ref[...] = jnp.dot(a_ref[...], w_ref[...], preferred_element_type=F32).astype(o_ref.dtype)


def _matmul(a, w, col0, n, out_dtype, tm, tn, name):
    m, k = a.shape
    return pl.pallas_call(
        _mm_kernel,
        grid=(m // tm, n // tn),
        in_specs=[
            pl.BlockSpec((tm, k), lambda i, j: (i, 0)),
            pl.BlockSpec((k, tn), lambda i, j: (0, col0 // tn + j)),
        ],
        out_specs=pl.BlockSpec((tm, tn), lambda i, j: (i, j)),
        out_shape=jax.ShapeDtypeStruct((m, n), out_dtype),
        compiler_params=_cparams(("parallel", "arbitrary")),
        name=name,
    )(a, w)


def _mm_tn_kernel(w_ref, a_ref, o_ref):
    o_ref[...] = lax.dot_general(w_ref[...], a_ref[...], (((0,), (1,)), ((), ())),
                                 preferred_element_type=F32).astype(o_ref.dtype)


def _matmul_tn(w, col0, n, a, out_dtype, tm, name):
    k = w.shape[0]
    m = a.shape[0]
    return pl.pallas_call(
        _mm_tn_kernel,
        grid=(m // tm,),
        in_specs=[
            pl.BlockSpec((k, n), lambda i: (0, col0 // n)),
            pl.BlockSpec((tm, k), lambda i: (i, 0)),
        ],
        out_specs=pl.BlockSpec((n, tm), lambda i: (0, i)),
        out_shape=jax.ShapeDtypeStruct((n, m), out_dtype),
        compiler_params=_cparams(("parallel",)),
        name=name,
    )(w, a)


def _proj_small_kernel(a_ref, w_ref, o_ref):
    o_ref[...] = jnp.dot(a_ref[...], w_ref[...].astype(BF16), preferred_element_type=F32)


def _proj_small(hb, w_in, l):
    return pl.pallas_call(
        _proj_small_kernel,
        grid=(R // TM_PROJ, 2),
        in_specs=[
            pl.BlockSpec((TM_PROJ, D_MODEL), lambda i, j: (i, 0)),
            pl.BlockSpec((None, D_MODEL, LANE),
                         lambda i, j: (l, 0, SM_FF_TILE + j * (SM_GA_TILE - SM_FF_TILE))),
        ],
        out_specs=pl.BlockSpec((TM_PROJ, LANE), lambda i, j: (i, j)),
        out_shape=jax.ShapeDtypeStruct((R, 2 * LANE), F32),
        compiler_params=_cparams(("parallel", "arbitrary")),
        name="proj_small",
    )(hb, w_in)


def _fox_gate_kernel(zs_ref, bias_ref, c_ref, carry_ref):
    i = pl.program_id(0)

    @pl.when(i == 0)
    def _():
        carry_ref[...] = jnp.zeros_like(carry_ref)

    rows = i * TM + lax.broadcasted_iota(I32, (TM, LANE), 0)
    lf = _log_sigmoid(zs_ref[...] + bias_ref[...])
    lf = jnp.where(rows >= T0, lf, 0.0)
    tri = (lax.broadcasted_iota(I32, (TM, TM), 0)
           >= lax.broadcasted_iota(I32, (TM, TM), 1)).astype(F32)
    c = jnp.dot(tri, lf, precision=HIGHEST, preferred_element_type=F32) + carry_ref[...]
    c_ref[...] = c
    carry_ref[...] = c[TM - 1:TM, :]


def _fox_gate(zs, bias_row):
    return pl.pallas_call(
        _fox_gate_kernel,
        grid=(R // TM,),
        in_specs=[
            pl.BlockSpec((TM, LANE), lambda i: (i, 0)),
            pl.BlockSpec((1, LANE), lambda i: (0, 0)),
        ],
        out_specs=pl.BlockSpec((TM, LANE), lambda i: (i, 0)),
        out_shape=jax.ShapeDtypeStruct((R, LANE), F32),
        scratch_shapes=[pltpu.VMEM((1, LANE), F32)],
        compiler_params=_cparams(("arbitrary",)),
        name="fox_gate",
    )(zs, bias_row)


TQ = TM
N_QB = R // TQ
_PAIRS = [(qi, kj) for qi in range(N_QB) for kj in range(qi + 1)]
N_PAIRS = len(_PAIRS)


LOG2E = 1.4426950408889634
FOX_HPS = 4


def _fox_kernel(qi_tab, kj_tab, q_ref, k_ref, vt_ref, ck_ref, o_ref, m_sc, l_sc, acc_sc):
    p = pl.program_id(1)
    qi = qi_tab[p]
    kj = kj_tab[p]

    @pl.when(kj == 0)
    def _():
        m_sc[...] = jnp.full_like(m_sc, NEG)
        l_sc[...] = jnp.zeros_like(l_sc)
        acc_sc[...] = jnp.zeros_like(acc_sc)

    kpos = kj * TQ + lax.broadcasted_iota(I32, (TQ, 1), 0)
    c1 = FOX_HD ** -0.5 * LOG2E

    def step(causal):
        for hh in range(FOX_HPS):
            lanes = slice(hh * FOX_HD, (hh + 1) * FOX_HD)
            ckl = jnp.where(kpos >= T0, ck_ref[hh] * LOG2E, -NEG)
            t = lax.dot_general(k_ref[:, lanes], q_ref[:, lanes], (((1,), (1,)), ((), ())),
                                preferred_element_type=F32) * c1 - ckl
            if causal:
                ahead = (lax.broadcasted_iota(I32, (TQ, TQ), 0)
                         - lax.broadcasted_iota(I32, (TQ, TQ), 1))
                t = jnp.where(ahead <= 0, t, NEG)
            m_prev = m_sc[hh]
            m_new = jnp.maximum(m_prev, jnp.max(t, axis=0, keepdims=True))
            alpha = jnp.exp2(m_prev - m_new)
            pr = jnp.exp2(t - m_new)
            l_sc[hh] = alpha * l_sc[hh] + jnp.sum(pr, axis=0, keepdims=True)
            acc_sc[hh] = alpha * acc_sc[hh] + jnp.dot(vt_ref[lanes, :], pr.astype(BF16),
                                                      preferred_element_type=F32)
            m_sc[hh] = m_new

    @pl.when(kj == qi)
    def _():
        step(True)

    @pl.when(kj != qi)
    def _():
        step(False)

    @pl.when(kj == qi)
    def _():
        for hh in range(FOX_HPS):
            o_ref[:, hh * FOX_HD:(hh + 1) * FOX_HD] = (acc_sc[hh] / l_sc[hh]).T.astype(o_ref.dtype)


def _fox_attention(zf, vt, c_col):
    qi_tab = jnp.asarray([p[0] for p in _PAIRS], I32)
    kj_tab = jnp.asarray([p[1] for p in _PAIRS], I32)
    hw = FOX_HPS * FOX_HD
    grid_spec = pltpu.PrefetchScalarGridSpec(
        num_scalar_prefetch=2,
        grid=(FOX_HEADS // FOX_HPS, N_PAIRS),
        in_specs=[
            pl.BlockSpec((TQ, hw), lambda h, p, qt, kt: (qt[p], h)),
            pl.BlockSpec((TQ, hw), lambda h, p, qt, kt: (kt[p], FOX_HEADS // FOX_HPS + h)),
            pl.BlockSpec((hw, TQ), lambda h, p, qt, kt: (h, kt[p])),
            pl.BlockSpec((FOX_HPS, TQ, 1), lambda h, p, qt, kt: (h, kt[p], 0)),
        ],
        out_specs=pl.BlockSpec((TQ, hw), lambda h, p, qt, kt: (qt[p], h)),
        scratch_shapes=[
            pltpu.VMEM((FOX_HPS, 1, TQ), F32),
            pltpu.VMEM((FOX_HPS, 1, TQ), F32),
            pltpu.VMEM((FOX_HPS, FOX_HD, TQ), F32),
        ],
    )
    return pl.pallas_call(
        _fox_kernel,
        grid_spec=grid_spec,
        out_shape=jax.ShapeDtypeStruct((R, MIX_W), BF16),
        compiler_params=_cparams(("parallel", "arbitrary")),
        name="fox_attention",
    )(qi_tab, kj_tab, zf, zf, vt, c_col)


GLA_CHUNK = 64


def _gla_kernel(q_ref, k_ref, v_ref, gr_ref, zs_ref, wa2_ref, ba_ref, gn_ref, o_ref, st_ref, la_ref):
    i = pl.program_id(0)

    @pl.when(i == 0)
    def _():
        st_ref[...] = jnp.zeros_like(st_ref)

    la = jnp.dot(zs_ref[...], wa2_ref[...], precision=HIGHEST, preferred_element_type=F32)
    la_ref[...] = _log_sigmoid(la + ba_ref[...]) * (1.0 / GLA_TAU)

    c_r = lax.broadcasted_iota(I32, (GLA_CHUNK, GLA_CHUNK), 0)
    c_c = lax.broadcasted_iota(I32, (GLA_CHUNK, GLA_CHUNK), 1)
    tri_b = c_r >= c_c
    tri = tri_b.astype(F32)

    def chunk(c, carry):
        r0 = pl.multiple_of(c * GLA_CHUNK, GLA_CHUNK)
        rows = pl.ds(r0, GLA_CHUNK)
        g = la_ref[rows, :]
        b = jnp.dot(tri, g, precision=HIGHEST, preferred_element_type=F32)
        b_last = b[GLA_CHUNK - 1:GLA_CHUNK, :]
        e_last = jnp.exp(b_last)
        qt = q_ref[rows, :] * (GLA_DK ** -0.5) * jnp.exp(b)
        kt = k_ref[rows, :] * jnp.exp(-b)
        kh = kt * e_last
        for h in range(GLA_HEADS):
            ks = slice(h * GLA_DK, (h + 1) * GLA_DK)
            vs = slice(h * GLA_DV, (h + 1) * GLA_DV)
            q_h = qt[:, ks].astype(BF16)
            k_h = kt[:, ks].astype(BF16)
            kh_h = kh[:, ks].astype(BF16)
            v_h = v_ref[rows, vs]
            att = lax.dot_general(q_h, k_h, (((1,), (1,)), ((), ())), preferred_element_type=F32)
            att = jnp.where(tri_b, att, 0.0)
            st = st_ref[h]
            o = jnp.dot(att.astype(BF16), v_h.astype(BF16), preferred_element_type=F32)
            o = o + lax.dot_general(q_h, st.astype(BF16), (((1,), (1,)), ((), ())),
                                    preferred_element_type=F32)
            st_ref[h] = st * e_last[:, ks] + jnp.dot(v_h.T.astype(BF16), kh_h,
                                                     preferred_element_type=F32)
            ms = jnp.mean(o * o, axis=-1, keepdims=True)
            on = o * lax.rsqrt(ms + LN_EPS) * gn_ref[:, vs]
            gate = gr_ref[rows, vs]
            o_ref[rows, vs] = (on * (gate * _sigmoid(gate))).astype(o_ref.dtype)
        return carry

    lax.fori_loop(0, TM // GLA_CHUNK, chunk, 0)


def _gla(z, zs, wa2p, ba, gn):
    return pl.pallas_call(
        _gla_kernel,
        grid=(R // TM,),
        in_specs=[
            pl.BlockSpec((TM, 256), lambda i: (i, C_GQ // 256)),
            pl.BlockSpec((TM, 256), lambda i: (i, C_GK // 256)),
            pl.BlockSpec((TM, 512), lambda i: (i, C_GV // 512)),
            pl.BlockSpec((TM, 512), lambda i: (i, C_GR // 512)),
            pl.BlockSpec((TM, LANE), lambda i: (i, 1)),
            pl.BlockSpec((LANE, 256), lambda i: (0, 0)),
            pl.BlockSpec((1, 256), lambda i: (0, 0)),
            pl.BlockSpec((1, 512), lambda i: (0, 0)),
        ],
        out_specs=pl.BlockSpec((TM, MIX_W), lambda i: (i, 0)),
        out_shape=jax.ShapeDtypeStruct((R, MIX_W), BF16),
        scratch_shapes=[
            pltpu.VMEM((GLA_HEADS, GLA_DV, GLA_DK), F32),
            pltpu.VMEM((TM, GLA_HEADS * GLA_DK), F32),
        ],
        compiler_params=_cparams(("arbitrary",)),
        name="gla",
    )(z, z, z, z, zs, wa2p, ba, gn)


def _local_kernel(cb_ref, cc_ref, cv_ref, pz_ref, cw_ref, pw_ref, ps_ref, oc_ref, od_ref, u_sc, p_sc):
    i = pl.program_id(0)

    @pl.when(i == 0)
    def _():
        u_sc[0:HALO, :] = jnp.zeros((HALO, MIX_W), F32)
        p_sc[0:HALO, :] = jnp.zeros((HALO, MIX_W), F32)

    @pl.when(i > 0)
    def _():
        u_sc[0:HALO, :] = u_sc[TM:TM + HALO, :]
        p_sc[0:HALO, :] = p_sc[TM:TM + HALO, :]

    u = cc_ref[...] * cv_ref[...]
    pz = pz_ref[...]
    u_sc[HALO:, :] = u
    p_sc[HALO:, :] = pz

    y = (cw_ref[2:3, :] * u + cw_ref[1:2, :] * u_sc[HALO - 1:HALO - 1 + TM, :]
         + cw_ref[0:1, :] * u_sc[HALO - 2:HALO - 2 + TM, :])
    oc_ref[...] = (cb_ref[...] * y).astype(oc_ref.dtype)

    tok = i * TM - T0 + lax.broadcasted_iota(I32, (TM, 1), 0)
    cnt_small = jnp.maximum(tok + 1, 1).astype(F32)
    for g, w in enumerate(POOL_WINDOWS):
        cols = slice(g * POOL_GW, (g + 1) * POOL_GW)
        x = pz[:, cols]
        s = x
        for j in range(1, w):
            s = s + p_sc[HALO - j:HALO - j + TM, cols]
        inv_cnt = jnp.where(tok + 1 >= w, 1.0 / w, 1.0 / cnt_small)
        pooled = s * inv_cnt - x
        od = jnp.dot(pooled.astype(BF16), pw_ref[g], preferred_element_type=F32)
        od_ref[:, cols] = (od * ps_ref[:, cols]).astype(od_ref.dtype)


def _local_mixers(z, conv_w, pool_w_bf, pool_scale):
    cw = jnp.zeros((8, MIX_W), F32).at[:CONV_K].set(conv_w)
    blk = lambda c: pl.BlockSpec((TM, MIX_W), lambda i, c=c: (i, c // MIX_W))
    return pl.pallas_call(
        _local_kernel,
        grid=(R // TM,),
        in_specs=[
            blk(C_CB), blk(C_CC), blk(C_CV), blk(C_PZ),
            pl.BlockSpec((8, MIX_W), lambda i: (0, 0)),
            pl.BlockSpec((len(POOL_WINDOWS), POOL_GW, POOL_GW), lambda i: (0, 0, 0)),
            pl.BlockSpec((1, MIX_W), lambda i: (0, 0)),
        ],
        out_specs=[
            pl.BlockSpec((TM, MIX_W), lambda i: (i, 0)),
            pl.BlockSpec((TM, MIX_W), lambda i: (i, 0)),
        ],
        out_shape=[
            jax.ShapeDtypeStruct((R, MIX_W), BF16),
            jax.ShapeDtypeStruct((R, MIX_W), BF16),
        ],
        scratch_shapes=[
            pltpu.VMEM((TM + HALO, MIX_W), F32),
            pltpu.VMEM((TM + HALO, MIX_W), F32),
        ],
        compiler_params=_cparams(("arbitrary",)),
        name="conv_pool",
    )(z, z, z, z, cw, pool_w_bf, pool_scale.reshape(1, -1))


TN_MERGE = 256


def _merge_kernel(hb_ref, oa_ref, ob_ref, oc_ref, od_ref, wg0_ref, wg1_ref, wg2_ref, wg3_ref,
                  gb_ref, wb_ref, wo_ref, out_ref):
    j = pl.program_id(1)

    @pl.when(j == 0)
    def _():
        out_ref[...] = jnp.zeros_like(out_ref)

    hb = hb_ref[...]
    mixed = None
    for b, (o_ref, wg_ref) in enumerate(((oa_ref, wg0_ref), (ob_ref, wg1_ref),
                                         (oc_ref, wg2_ref), (od_ref, wg3_ref))):
        gate = _sigmoid(jnp.dot(hb, wg_ref[...], preferred_element_type=F32) + gb_ref[b:b + 1, :])
        proj = jnp.dot(o_ref[...], wb_ref[b], preferred_element_type=F32)
        term = gate * proj
        mixed = term if mixed is None else mixed + term
    out_ref[...] += jnp.dot(mixed.astype(BF16), wo_ref[...], preferred_element_type=F32)


def _merge(hb, o_a, o_b, o_c, o_d, w_all, gate_b, wb_bf, wo_bf, l):
    tn = TN_MERGE
    nj = D_MODEL // tn
    row = lambda w: pl.BlockSpec((TM, w), lambda i, j: (i, 0))
    wg = lambda b: pl.BlockSpec((D_MODEL, tn), lambda i, j, b=b: (0, W_GATES // tn + b * nj + j))
    return pl.pallas_call(
        _merge_kernel,
        grid=(R // TM, nj),
        in_specs=[
            row(D_MODEL), row(MIX_W), row(MIX_W), row(MIX_W), row(MIX_W),
            wg(0), wg(1), wg(2), wg(3),
            pl.BlockSpec((N_BRANCH, tn), lambda i, j: (0, j)),
            pl.BlockSpec((None, N_BRANCH, MIX_W, tn), lambda i, j: (l, 0, 0, j)),
            pl.BlockSpec((None, tn, D_MODEL), lambda i, j: (l, j, 0)),
        ],
        out_specs=pl.BlockSpec((TM, D_MODEL), lambda i, j: (i, 0)),
        out_shape=jax.ShapeDtypeStruct((R, D_MODEL), F32),
        compiler_params=_cparams(("parallel", "arbitrary")),
        name="merge",
    )(hb, o_a, o_b, o_c, o_d, w_all, w_all, w_all, w_all, gate_b, wb_bf, wo_bf)


def _post_ln(h, delta, g, b, row0):
    y = _layer_norm_rows(DEEPNORM_ALPHA * h + delta, g, b)
    rows = row0 + lax.broadcasted_iota(I32, (y.shape[0], 1), 0)
    return jnp.where(rows >= T0, y, 0.0)


def _first_of(cands, target):
    idx = jnp.full(target.shape, len(cands) - 1, I32)
    for j in range(len(cands) - 2, -1, -1):
        idx = jnp.where(cands[j] == target, j, idx)
    return idx


def _pick(cands, idx):
    out = cands[-1]
    for j in range(len(cands) - 2, -1, -1):
        out = jnp.where(idx == j, cands[j], out)
    return out


def _ln1_route_kernel(h_ref, mix_ref, g_ref, b_ref, rwt_ref, rb_ref,
                      h1_ref, mi_ref, mf_ref, cnt_ref, carry_sc):
    i = pl.program_id(0)

    @pl.when(i == 0)
    def _():
        carry_sc[...] = jnp.zeros_like(carry_sc)

    y = _post_ln(h_ref[...], mix_ref[...], g_ref[...], b_ref[...], i * TM)
    h1_ref[...] = y

    logits = lax.dot_general(rwt_ref[...], y, (((1,), (1,)), ((), ())), precision=HIGHEST,
                             preferred_element_type=F32)
    aff = _sigmoid(logits)
    sel = aff + rb_ref[...]
    xs = [sel[j * N_GROUPS:(j + 1) * N_GROUPS, :] for j in range(EXPERTS_PER_GROUP)]
    afs = [aff[j * N_GROUPS:(j + 1) * N_GROUPS, :] for j in range(EXPERTS_PER_GROUP)]

    score = None
    for a in range(EXPERTS_PER_GROUP):
        for bb in range(a + 1, EXPERTS_PER_GROUP):
            pair = xs[a] + xs[bb]
            score = pair if score is None else jnp.maximum(score, pair)
    giota = lax.broadcasted_iota(I32, (N_GROUPS, TM), 0)
    gmax = jnp.max(score, axis=0, keepdims=True)
    grp = jnp.min(jnp.where(score == gmax, giota, N_GROUPS), axis=0, keepdims=True)
    gsel = giota == grp
    cs = [jnp.max(jnp.where(gsel, x, -jnp.inf), axis=0, keepdims=True) for x in xs]
    acs = [jnp.sum(jnp.where(gsel, a, 0.0), axis=0, keepdims=True) for a in afs]

    m1 = jnp.maximum(jnp.maximum(cs[0], cs[1]), jnp.maximum(cs[2], cs[3]))
    i0 = _first_of(cs, m1)
    ds = [jnp.where(i0 == j, -jnp.inf, cs[j]) for j in range(EXPERTS_PER_GROUP)]
    m2 = jnp.maximum(jnp.maximum(ds[0], ds[1]), jnp.maximum(ds[2], ds[3]))
    i1 = _first_of(ds, m2)
    a0 = _pick(acs, i0)
    a1 = _pick(acs, i1)
    denom = a0 + a1

    pos = i * TM + lax.broadcasted_iota(I32, (1, TM), 1)
    valid = pos >= T0
    riota = lax.broadcasted_iota(I32, (N_EXPERTS, TM), 0)
    oh0 = (riota == i0 * N_GROUPS + grp) & valid
    oh1 = (riota == i1 * N_GROUPS + grp) & valid
    ohf = jnp.where(oh0 | oh1, 1.0, 0.0)
    before = (lax.broadcasted_iota(I32, (TM, TM), 0)
              < lax.broadcasted_iota(I32, (TM, TM), 1)).astype(BF16)
    cum = jnp.dot(ohf.astype(BF16), before, preferred_element_type=F32) + carry_sc[...]
    rank0 = jnp.sum(jnp.where(oh0, cum, 0.0), axis=0, keepdims=True)
    rank1 = jnp.sum(jnp.where(oh1, cum, 0.0), axis=0, keepdims=True)
    carry = carry_sc[...] + jnp.sum(ohf, axis=1, keepdims=True)
    carry_sc[...] = carry
    cnt_ref[...] = jnp.broadcast_to(carry, cnt_ref.shape)

    zi = jnp.zeros((1, TM), I32)
    mi_ref[...] = jnp.concatenate(
        [grp * EXPERTS_PER_GROUP + i0, grp * EXPERTS_PER_GROUP + i1,
         rank0.astype(I32), rank1.astype(I32), zi, zi, zi, zi], axis=0)
    zf = jnp.zeros((1, TM), F32)
    mf_ref[...] = jnp.concatenate([a0 / denom, a1 / denom, zf, zf, zf, zf, zf, zf], axis=0)


def _ln1_route(h, mix, g, b, router_wt, router_bc):
    row = pl.BlockSpec((TM, D_MODEL), lambda i: (i, 0))
    vec = pl.BlockSpec((1, D_MODEL), lambda i: (0, 0))
    meta = pl.BlockSpec((8, TM), lambda i: (0, i))
    return pl.pallas_call(
        _ln1_route_kernel,
        grid=(R // TM,),
        in_specs=[row, row, vec, vec,
                  pl.BlockSpec((N_EXPERTS, D_MODEL), lambda i: (0, 0)),
                  pl.BlockSpec((N_EXPERTS, 1), lambda i: (0, 0))],
        out_specs=[row, meta, meta, pl.BlockSpec((N_EXPERTS, LANE), lambda i: (0, 0))],
        out_shape=[
            jax.ShapeDtypeStruct((R, D_MODEL), F32),
            jax.ShapeDtypeStruct((8, R), I32),
            jax.ShapeDtypeStruct((8, R), F32),
            jax.ShapeDtypeStruct((N_EXPERTS, LANE), F32),
        ],
        scratch_shapes=[pltpu.VMEM((N_EXPERTS, 1), F32)],
        compiler_params=_cparams(("arbitrary",)),
        name="ln1_route",
    )(h, mix, g.reshape(1, -1), b.reshape(1, -1), router_wt, router_bc)


def _dispatch_tables(mi, counts_slot_major):
    counts = counts_slot_major.reshape(EXPERTS_PER_GROUP, N_GROUPS).T.reshape(N_EXPERTS).astype(I32)
    padded = (counts + EXPERT_BLOCK - 1) // EXPERT_BLOCK * EXPERT_BLOCK
    pad_end = jnp.cumsum(padded)
    pad_start = pad_end - padded
    e_iota = jnp.arange(N_EXPERTS, dtype=I32)
    rows_ok = jnp.arange(R) >= T0

    def dest(eid, rank):
        start = jnp.sum(jnp.where(eid[:, None] == e_iota[None, :], pad_start[None, :], 0), axis=1)
        return jnp.where(rows_ok, start + rank, 0).astype(I32)

    d0 = dest(mi[0], mi[2])
    d1 = dest(mi[1], mi[3])
    blk_start = (pad_start // EXPERT_BLOCK).astype(I32)
    n_blk = (padded // EXPERT_BLOCK).astype(I32)
    n_used = (pad_end[-1] // EXPERT_BLOCK).astype(I32).reshape(1)
    tok_rows = jnp.arange(T0, R, dtype=I32)
    row_src = jnp.zeros((N_ROWS + EXPERT_BLOCK,), I32).at[jnp.concatenate([d0[T0:], d1[T0:]])].set(
        jnp.concatenate([tok_rows, tok_rows]), unique_indices=True)
    return d0, d1, blk_start, n_blk, n_used, row_src


def _row_copy(src, src_row, dst, dst_row, sem):
    return pltpu.make_async_copy(src.at[pl.ds(src_row, 1), :], dst.at[pl.ds(dst_row, 1), :], sem)


def _issue_row_gather(rs_ref, g, h_hbm, buf, sem):
    base = g * EXPERT_BLOCK
    for r in range(EXPERT_BLOCK):
        _row_copy(h_hbm, rs_ref[base + r], buf, r, sem).start(priority=1)


def _wait_row_gather(h_hbm, buf, sem):
    for r in range(EXPERT_BLOCK):
        _row_copy(h_hbm, 0, buf, r, sem).wait()


def _block_rows(g):
    return pl.ds(pl.multiple_of(g * EXPERT_BLOCK, EXPERT_BLOCK), EXPERT_BLOCK)


def _finish_writes(out_copy, obuf, nu):
    @pl.when(nu >= 2)
    def _():
        out_copy(nu - 2, nu % 2).wait()

    @pl.when(nu >= 1)
    def _():
        out_copy(nu - 1, (nu - 1) % 2).wait()

    obuf[0] = jnp.zeros(obuf.shape[1:], obuf.dtype)

    def zero_block(g, carry):
        cp = out_copy(g, 0)
        cp.start()
        cp.wait()
        return carry

    lax.fori_loop(nu, N_BLOCKS, zero_block, 0)


def _moe_up_kernel(bs_ref, nb_ref, nu_ref, rs_ref, h_hbm, wg_ref, wu_ref, o_hbm,
                   wg_sc, wu_sc, xbuf, obuf, xsem, osem):
    e = pl.program_id(0)
    nb = nb_ref[e]
    g0 = bs_ref[e]

    def out_copy(g, slot):
        return pltpu.make_async_copy(obuf.at[slot], o_hbm.at[_block_rows(g), :], osem.at[slot])

    @pl.when(e == 0)
    def _():
        _issue_row_gather(rs_ref, 0, h_hbm, xbuf.at[0], xsem.at[0])

    @pl.when(nb > 0)
    def _():
        wg_sc[...] = wg_ref[...].astype(BF16)
        wu_sc[...] = wu_ref[...].astype(BF16)

        def block(j, carry):
            g = g0 + j
            slot = g % 2

            @pl.when(g >= 2)
            def _():
                out_copy(g - 2, slot).wait()

            _wait_row_gather(h_hbm, xbuf.at[slot], xsem.at[slot])
            _issue_row_gather(rs_ref, g + 1, h_hbm, xbuf.at[1 - slot], xsem.at[1 - slot])
            x = xbuf[slot].astype(BF16)
            gate = jnp.dot(x, wg_sc[...], preferred_element_type=F32)
            up = jnp.dot(x, wu_sc[...], preferred_element_type=F32)
            obuf[slot] = (gate * _sigmoid(gate) * up).astype(BF16)
            out_copy(g, slot).start()
            return carry

        lax.fori_loop(0, nb, block, 0)

    @pl.when(e == N_EXPERTS - 1)
    def _():
        nu = nu_ref[0]
        _wait_row_gather(h_hbm, xbuf.at[nu % 2], xsem.at[nu % 2])
        _finish_writes(out_copy, obuf, nu)


def _moe_up(blk_start, n_blk, n_used, row_src, h1, w_gate, w_up, l):
    wspec = pl.BlockSpec((None, None, D_MODEL, D_EXPERT), lambda e, *_: (l, e, 0, 0))
    grid_spec = pltpu.PrefetchScalarGridSpec(
        num_scalar_prefetch=4,
        grid=(N_EXPERTS,),
        in_specs=[pl.BlockSpec(memory_space=pl.ANY), wspec, wspec],
        out_specs=pl.BlockSpec(memory_space=pl.ANY),
        scratch_shapes=[
            pltpu.VMEM((D_MODEL, D_EXPERT), BF16),
            pltpu.VMEM((D_MODEL, D_EXPERT), BF16),
            pltpu.VMEM((2, EXPERT_BLOCK, D_MODEL), F32),
            pltpu.VMEM((2, EXPERT_BLOCK, D_EXPERT), BF16),
            pltpu.SemaphoreType.DMA((2,)),
            pltpu.SemaphoreType.DMA((2,)),
        ],
    )
    return pl.pallas_call(
        _moe_up_kernel,
        grid_spec=grid_spec,
        out_shape=jax.ShapeDtypeStruct((N_ROWS, D_EXPERT), BF16),
        compiler_params=_cparams(("arbitrary",), vmem=MOE_VMEM_LIMIT),
        name="moe_up",
    )(blk_start, n_blk, n_used, row_src, h1, w_gate, w_up)


def _moe_down_kernel(bs_ref, nb_ref, nu_ref, x_hbm, wd_ref, y_hbm, wd_sc, xbuf, obuf, xsem, osem):
    e = pl.program_id(0)
    nb = nb_ref[e]
    g0 = bs_ref[e]
    nu = nu_ref[0]

    def in_copy(g, slot):
        return pltpu.make_async_copy(x_hbm.at[_block_rows(g), :], xbuf.at[slot], xsem.at[slot])

    def out_copy(g, slot):
        return pltpu.make_async_copy(obuf.at[slot], y_hbm.at[_block_rows(g), :], osem.at[slot])

    @pl.when((e == 0) & (nu > 0))
    def _():
        in_copy(0, 0).start()

    @pl.when(nb > 0)
    def _():
        wd_sc[...] = wd_ref[...].astype(BF16)

        def block(j, carry):
            g = g0 + j
            slot = g % 2

            @pl.when(g >= 2)
            def _():
                out_copy(g - 2, slot).wait()

            in_copy(g, slot).wait()

            @pl.when(g + 1 < nu)
            def _():
                in_copy(g + 1, 1 - slot).start()

            obuf[slot] = jnp.dot(xbuf[slot], wd_sc[...], preferred_element_type=F32)
            out_copy(g, slot).start()
            return carry

        lax.fori_loop(0, nb, block, 0)

    @pl.when(e == N_EXPERTS - 1)
    def _():
        _finish_writes(out_copy, obuf, nu)


def _moe_down(blk_start, n_blk, n_used, hmid, w_down, l):
    grid_spec = pltpu.PrefetchScalarGridSpec(
        num_scalar_prefetch=3,
        grid=(N_EXPERTS,),
        in_specs=[
            pl.BlockSpec(memory_space=pl.ANY),
            pl.BlockSpec((None, None, D_EXPERT, D_MODEL), lambda e, *_: (l, e, 0, 0)),
        ],
        out_specs=pl.BlockSpec(memory_space=pl.ANY),
        scratch_shapes=[
            pltpu.VMEM((D_EXPERT, D_MODEL), BF16),
            pltpu.VMEM((2, EXPERT_BLOCK, D_EXPERT), BF16),
            pltpu.VMEM((2, EXPERT_BLOCK, D_MODEL), F32),
            pltpu.SemaphoreType.DMA((2,)),
            pltpu.SemaphoreType.DMA((2,)),
        ],
    )
    return pl.pallas_call(
        _moe_down_kernel,
        grid_spec=grid_spec,
        out_shape=jax.ShapeDtypeStruct((N_ROWS, D_MODEL), F32),
        compiler_params=_cparams(("arbitrary",), vmem=MOE_VMEM_LIMIT),
        name="moe_down",
    )(blk_start, n_blk, n_used, hmid, w_down)


def _combine_kernel(d0_ref, d1_ref, y_hbm, h1_ref, mf_ref, g_ref, b_ref, h2_ref, h2b_ref,
                    buf0, buf1, sem):
    i = pl.program_id(0)
    slot = i % 2

    def issue(tile, s):
        base = tile * LANE

        def body(r, carry):
            _row_copy(y_hbm, d0_ref[base + r], buf0.at[s], r, sem.at[s]).start(priority=0)
            _row_copy(y_hbm, d1_ref[base + r], buf1.at[s], r, sem.at[s]).start(priority=1)
            return carry

        lax.fori_loop(0, LANE, body, 0)

    @pl.when(i == 0)
    def _():
        issue(0, 0)

    @pl.when(i + 1 < pl.num_programs(0))
    def _():
        issue(i + 1, 1 - slot)

    def drain(r, carry):
        _row_copy(y_hbm, 0, buf0.at[slot], r, sem.at[slot]).wait()
        _row_copy(y_hbm, 0, buf1.at[slot], r, sem.at[slot]).wait()
        return carry

    lax.fori_loop(0, LANE, drain, 0)
    wt = mf_ref[...].T
    ffn = wt[:, 0:1] * buf0[slot] + wt[:, 1:2] * buf1[slot]
    y = _post_ln(h1_ref[...], ffn, g_ref[...], b_ref[...], i * LANE)
    h2_ref[...] = y
    h2b_ref[...] = y.astype(BF16)


def _combine_ln2(d0, d1, y_rows, h1, mf, g, b):
    row = lambda i, d0, d1: (i, 0)
    vec = pl.BlockSpec((1, D_MODEL), lambda i, d0, d1: (0, 0))
    grid_spec = pltpu.PrefetchScalarGridSpec(
        num_scalar_prefetch=2,
        grid=(R // LANE,),
        in_specs=[
            pl.BlockSpec(memory_space=pl.ANY),
            pl.BlockSpec((LANE, D_MODEL), row),
            pl.BlockSpec((8, LANE), lambda i, d0, d1: (0, i)),
            vec, vec,
        ],
        out_specs=[pl.BlockSpec((LANE, D_MODEL), row), pl.BlockSpec((LANE, D_MODEL), row)],
        scratch_shapes=[
            pltpu.VMEM((2, LANE, D_MODEL), F32),
            pltpu.VMEM((2, LANE, D_MODEL), F32),
            pltpu.SemaphoreType.DMA((2,)),
        ],
    )
    return pl.pallas_call(
        _combine_kernel,
        grid_spec=grid_spec,
        out_shape=[
            jax.ShapeDtypeStruct((R, D_MODEL), F32),
            jax.ShapeDtypeStruct((R, D_MODEL), BF16),
        ],
        compiler_params=_cparams(("arbitrary",)),
        name="moe_combine_ln2",
    )(d0, d1, y_rows, h1, mf, g.reshape(1, -1), b.reshape(1, -1))


def kernel(x, meta_tokens, ln_in_g, ln_in_b, w_in, fox_f_bias, gla_wa2, gla_ba, gla_norm_g, conv_w, pool_w, pool_scale, gate_b, w_branch, w_out, ln1_g, ln1_b, router_w, router_b, w_gate, w_up, w_down, ln2_g, ln2_b):
    assert x.shape == (1, SEQ, D_MODEL)
    h, hb = _ln_in(x.reshape(SEQ, D_MODEL), meta_tokens, ln_in_g, ln_in_b)
    router_wt = router_w.T.reshape(N_GROUPS, EXPERTS_PER_GROUP, D_MODEL).transpose(1, 0, 2).reshape(
        N_EXPERTS, D_MODEL)
    router_bc = router_b.astype(F32).reshape(N_GROUPS, EXPERTS_PER_GROUP).T.reshape(N_EXPERTS, 1)

    wb_bf = w_branch.astype(BF16)
    wo_bf = w_out.astype(BF16)

    for l in range(DEPTH):
        w_all = _prep_w_in(w_in, l)
        zf = _matmul(hb, w_all, W_FOX, 2 * MIX_W, BF16, TM_PROJ, 512, "proj_fox")
        vt = _matmul_tn(w_all, W_FOX + 2 * MIX_W, MIX_W, hb, BF16, TM_PROJ, "proj_fox_vt")
        z = _matmul(hb, w_all, W_MIX, N_MIXC, F32, TM_PROJ, 512, "proj_mix")
        zs = _proj_small(hb, w_in, l)

        bias_row = jnp.zeros((1, LANE), F32).at[0, SM_FF:SM_FF + FOX_HEADS].set(fox_f_bias[l])
        c = _fox_gate(zs, bias_row)[:, SM_FF:SM_FF + FOX_HEADS]
        o_a = _fox_attention(zf, vt, c.T.reshape(FOX_HEADS, R, 1))

        wa2p = jnp.zeros((LANE, GLA_HEADS * GLA_DK), F32).at[SM_GA:SM_GA + GLA_RANK].set(gla_wa2[l])
        o_b = _gla(z, zs, wa2p, gla_ba[l].reshape(1, -1), gla_norm_g[l].reshape(1, -1))

        o_c, o_d = _local_mixers(z, conv_w[l], pool_w[l].astype(BF16), pool_scale[l])

        mix = _merge(hb, o_a, o_b, o_c, o_d, w_all, gate_b[l], wb_bf, wo_bf, l)
        h1, mi, mf, counts = _ln1_route(h, mix, ln1_g[l], ln1_b[l], router_wt, router_bc)

        d0, d1, blk_start, n_blk, n_used, row_src = _dispatch_tables(mi, counts[:, 0])
        hmid = _moe_up(blk_start, n_blk, n_used, row_src, h1, w_gate, w_up, l)
        y_rows = _moe_down(blk_start, n_blk, n_used, hmid, w_down, l)
        h, hb = _combine_ln2(d0, d1, y_rows, h1, mf, ln2_g[l], ln2_b[l])

    return h[PAD_ROWS + N_META:].reshape(1, SEQ, D_MODEL)
```

```python
import jax
import jax.numpy as jnp
import numpy as np
from jax import lax
from jax.experimental import pallas as pl
from jax.experimental.pallas import tpu as pltpu

F32 = jnp.float32
BF16 = jnp.bfloat16
I32 = jnp.int32
HIGHEST = lax.Precision.HIGHEST

D_MODEL = 2048
SEQ = 8192
DEPTH = 2
N_META = 16
N_BRANCH = 4
MIX_W = 512
FOX_HEADS = 4
FOX_HD = 128
GLA_HEADS = 4
GLA_DK = 64
GLA_DV = 128
GLA_RANK = 16
GLA_TAU = 16.0
CONV_K = 3
POOL_WINDOWS = (2, 4, 8, 16)
POOL_GW = 128
N_EXPERTS = 32
N_GROUPS = 8
EXPERTS_PER_GROUP = 4
TOP_K = 2
D_EXPERT = 1024
LN_EPS = 1e-5
DEEPNORM_ALPHA = (2 * DEPTH) ** 0.25

_SPLITS = (512, 512, 512, 4, 256, 256, 512, 16, 512, 512, 512, 512, 512, 8192)
_OFFS = [int(o) for o in np.concatenate([[0], np.cumsum(_SPLITS)])]
(O_FQ, O_FK, O_FV, O_FF, O_GQ, O_GK, O_GV, O_GA, O_GR, O_CB, O_CC, O_CV, O_PZ, O_GZ, P_IN) = _OFFS

LANE = 128
PAD_ROWS = LANE - N_META
T0 = PAD_ROWS
N_TOK = N_META + SEQ
R = PAD_ROWS + N_TOK
TM = 640
TM_PROJ = 1664
HALO = 16

WT_TILE = 128
W_FOX, W_MIX, W_GATES = 0, 1536, 5120
N_FOX, N_MIXC, N_GATES = 1536, 3584, 8192
W_SMALL = W_GATES + N_GATES
N_WT = W_SMALL + 2 * LANE
C_GQ, C_GK, C_GV, C_GR, C_CB, C_CC, C_CV, C_PZ = 0, 256, 512, 1024, 1536, 2048, 2560, 3072
SM_FF_TILE, SM_GA_TILE = O_FF // LANE, O_GA // LANE
SM_FF = O_FF - SM_FF_TILE * LANE
SM_GA = O_GA - SM_GA_TILE * LANE


def _wt_sources():
    src = []
    for r in range(0, W_SMALL, WT_TILE):
        if r < N_FOX:
            src.append(O_FQ + r)
        elif r < W_MIX + 1024:
            src.append(O_GQ + r - W_MIX)
        else:
            src.append(O_GR + r - (W_MIX + 1024))
    src += [SM_FF_TILE * LANE, SM_GA_TILE * LANE]
    assert all(0 <= s and s + WT_TILE <= P_IN for s in src)
    return src


WT_SRC = _wt_sources()

EXPERT_BLOCK = 128
N_FLAT = N_TOK * TOP_K
N_BLOCKS = -(-N_FLAT // EXPERT_BLOCK) + N_EXPERTS
N_ROWS = N_BLOCKS * EXPERT_BLOCK

NEG = -1e30
VMEM_LIMIT = 48 * 1024 * 1024
MOE_VMEM_LIMIT = 56 * 1024 * 1024


def _cparams(sem, vmem=VMEM_LIMIT):
    return pltpu.CompilerParams(dimension_semantics=sem, vmem_limit_bytes=vmem)


def _log_sigmoid(x):
    return jnp.minimum(x, 0.0) - jnp.log1p(jnp.exp(-jnp.abs(x)))


def _sigmoid(x):
    return 1.0 / (1.0 + jnp.exp(-x))


def _layer_norm_rows(x, g, b):
    mu = jnp.mean(x, axis=-1, keepdims=True)
    xc = x - mu
    var = jnp.mean(xc * xc, axis=-1, keepdims=True)
    return xc * lax.rsqrt(var + LN_EPS) * g + b


def _ln_in_kernel(x_ref, meta_ref, g_ref, b_ref, h_ref, hb_ref):
    i = pl.program_id(0)

    @pl.when(i == 0)
    def _():
        h_ref[...] = jnp.zeros_like(h_ref)
        hb_ref[...] = jnp.zeros_like(hb_ref)
        m = _layer_norm_rows(meta_ref[...], g_ref[...], b_ref[...])
        h_ref[PAD_ROWS:, :] = m
        hb_ref[PAD_ROWS:, :] = m.astype(BF16)

    @pl.when(i > 0)
    def _():
        y = _layer_norm_rows(x_ref[...], g_ref[...], b_ref[...])
        h_ref[...] = y
        hb_ref[...] = y.astype(BF16)


def _ln_in(x2d, meta, g, b):
    nb = R // LANE
    return pl.pallas_call(
        _ln_in_kernel,
        grid=(nb,),
        in_specs=[
            pl.BlockSpec((LANE, D_MODEL), lambda i: (jnp.maximum(i - 1, 0), 0)),
            pl.BlockSpec((N_META, D_MODEL), lambda i: (0, 0)),
            pl.BlockSpec((1, D_MODEL), lambda i: (0, 0)),
            pl.BlockSpec((1, D_MODEL), lambda i: (0, 0)),
        ],
        out_specs=[
            pl.BlockSpec((LANE, D_MODEL), lambda i: (i, 0)),
            pl.BlockSpec((LANE, D_MODEL), lambda i: (i, 0)),
        ],
        out_shape=[
            jax.ShapeDtypeStruct((R, D_MODEL), F32),
            jax.ShapeDtypeStruct((R, D_MODEL), BF16),
        ],
        compiler_params=_cparams(("arbitrary",)),
        name="ln_in",
    )(x2d, meta, g.reshape(1, -1), b.reshape(1, -1))


D_CHUNKS = D_MODEL // LANE
FLAT_ROWS = D_CHUNKS * DEPTH


def _wt_kernel(src_ref, w_hbm, o_ref, buf, sem):
    j = pl.program_id(0)
    slot = j % 2

    def tile_copy(t, s):
        row0 = pl.multiple_of(src_ref[t] * FLAT_ROWS, FLAT_ROWS)
        return pltpu.make_async_copy(w_hbm.at[pl.ds(row0, WT_TILE * FLAT_ROWS), :], buf.at[s],
                                     sem.at[s])

    @pl.when(j == 0)
    def _():
        tile_copy(0, 0).start()

    @pl.when(j + 1 < pl.num_programs(0))
    def _():
        tile_copy(j + 1, 1 - slot).start()

    tile_copy(j, slot).wait()
    for l in range(DEPTH):
        for c in range(D_CHUNKS):
            o_ref[l, :, c * LANE:(c + 1) * LANE] = buf[
                slot, pl.ds(c * DEPTH + l, WT_TILE, stride=FLAT_ROWS), :].astype(BF16)


def _transposed_weights(w_in):
    flat = w_in.reshape(DEPTH, D_CHUNKS, LANE, P_IN).transpose(3, 1, 0, 2).reshape(
        P_IN * FLAT_ROWS, LANE)
    grid_spec = pltpu.PrefetchScalarGridSpec(
        num_scalar_prefetch=1,
        grid=(N_WT // WT_TILE,),
        in_specs=[pl.BlockSpec(memory_space=pl.ANY)],
        out_specs=pl.BlockSpec((DEPTH, WT_TILE, D_MODEL), lambda j, src: (0, j, 0)),
        scratch_shapes=[
            pltpu.VMEM((2, WT_TILE * FLAT_ROWS, LANE), F32),
            pltpu.SemaphoreType.DMA((2,)),
        ],
    )
    return pl.pallas_call(
        _wt_kernel,
        grid_spec=grid_spec,
        out_shape=jax.ShapeDtypeStruct((DEPTH, N_WT, D_MODEL), BF16),
        compiler_params=_cparams(("arbitrary",)),
        name="transposed_weights",
    )(jnp.asarray(WT_SRC, I32), flat)


_NT = (((1,), (1,)), ((), ()))


def _mm_nt_kernel(a_ref, wt_ref, o_ref):
    o_ref[...] = lax.dot_general(a_ref[...], wt_ref[...], _NT,
                                 preferred_element_type=F32).astype(o_ref.dtype)


def _matmul_nt(a, wt, l, row0, n, out_dtype, tm, tn, name):
    m, k = a.shape
    return pl.pallas_call(
        _mm_nt_kernel,
        grid=(m // tm, n // tn),
        in_specs=[
            pl.BlockSpec((tm, k), lambda i, j: (i, 0)),
            pl.BlockSpec((None, tn, k), lambda i, j: (l, row0 // tn + j, 0)),
        ],
        out_specs=pl.BlockSpec((tm, tn), lambda i, j: (i, j)),
        out_shape=jax.ShapeDtypeStruct((m, n), out_dtype),
        compiler_params=_cparams(("parallel", "arbitrary")),
        name=name,
    )(a, wt)


def _matmul_tt(wt, l, row0, n, a, out_dtype, tm, name):
    m, k = a.shape
    return pl.pallas_call(
        _mm_nt_kernel,
        grid=(m // tm,),
        in_specs=[
            pl.BlockSpec((None, n, k), lambda i: (l, row0 // n, 0)),
            pl.BlockSpec((tm, k), lambda i: (i, 0)),
        ],
        out_specs=pl.BlockSpec((n, tm), lambda i: (0, i)),
        out_shape=jax.ShapeDtypeStruct((n, m), out_dtype),
        compiler_params=_cparams(("parallel",)),
        name=name,
    )(wt, a)


def _fox_gate_kernel(zs_ref, bias_ref, c_ref, carry_ref):
    i = pl.program_id(0)

    @pl.when(i == 0)
    def _():
        carry_ref[...] = jnp.zeros_like(carry_ref)

    rows = i * TM + lax.broadcasted_iota(I32, (TM, LANE), 0)
    lf = _log_sigmoid(zs_ref[...] + bias_ref[...])
    lf = jnp.where(rows >= T0, lf, 0.0)
    tri = (lax.broadcasted_iota(I32, (TM, TM), 0)
           >= lax.broadcasted_iota(I32, (TM, TM), 1)).astype(F32)
    c = jnp.dot(tri, lf, precision=HIGHEST, preferred_element_type=F32) + carry_ref[...]
    c_ref[...] = c
    carry_ref[...] = c[TM - 1:TM, :]


def _fox_gate(zs, bias_row):
    return pl.pallas_call(
        _fox_gate_kernel,
        grid=(R // TM,),
        in_specs=[
            pl.BlockSpec((TM, LANE), lambda i: (i, 0)),
            pl.BlockSpec((1, LANE), lambda i: (0, 0)),
        ],
        out_specs=pl.BlockSpec((TM, LANE), lambda i: (i, 0)),
        out_shape=jax.ShapeDtypeStruct((R, LANE), F32),
        scratch_shapes=[pltpu.VMEM((1, LANE), F32)],
        compiler_params=_cparams(("arbitrary",)),
        name="fox_gate",
    )(zs, bias_row)


TQ = TM
N_QB = R // TQ
_PAIRS = [(qi, kj) for qi in range(N_QB) for kj in range(qi + 1)]
N_PAIRS = len(_PAIRS)


LOG2E = 1.4426950408889634
FOX_HPS = 4


def _fox_kernel(qi_tab, kj_tab, q_ref, k_ref, vt_ref, ck_ref, o_ref, m_sc, l_sc, acc_sc):
    p = pl.program_id(1)
    qi = qi_tab[p]
    kj = kj_tab[p]

    @pl.when(kj == 0)
    def _():
        m_sc[...] = jnp.full_like(m_sc, NEG)
        l_sc[...] = jnp.zeros_like(l_sc)
        acc_sc[...] = jnp.zeros_like(acc_sc)

    kpos = kj * TQ + lax.broadcasted_iota(I32, (TQ, 1), 0)
    c1 = FOX_HD ** -0.5 * LOG2E

    def step(causal):
        for hh in range(FOX_HPS):
            lanes = slice(hh * FOX_HD, (hh + 1) * FOX_HD)
            ckl = jnp.where(kpos >= T0, ck_ref[hh] * LOG2E, -NEG)
            t = lax.dot_general(k_ref[:, lanes], q_ref[:, lanes], (((1,), (1,)), ((), ())),
                                preferred_element_type=F32) * c1 - ckl
            if causal:
                ahead = (lax.broadcasted_iota(I32, (TQ, TQ), 0)
                         - lax.broadcasted_iota(I32, (TQ, TQ), 1))
                t = jnp.where(ahead <= 0, t, NEG)
            m_prev = m_sc[hh]
            m_new = jnp.maximum(m_prev, jnp.max(t, axis=0, keepdims=True))
            alpha = jnp.exp2(m_prev - m_new)
            pr = jnp.exp2(t - m_new)
            l_sc[hh] = alpha * l_sc[hh] + jnp.sum(pr, axis=0, keepdims=True)
            acc_sc[hh] = alpha * acc_sc[hh] + jnp.dot(vt_ref[lanes, :], pr.astype(BF16),
                                                      preferred_element_type=F32)
            m_sc[hh] = m_new

    @pl.when(kj == qi)
    def _():
        step(True)

    @pl.when(kj != qi)
    def _():
        step(False)

    @pl.when(kj == qi)
    def _():
        for hh in range(FOX_HPS):
            o_ref[:, hh * FOX_HD:(hh + 1) * FOX_HD] = (acc_sc[hh] / l_sc[hh]).T.astype(o_ref.dtype)


def _fox_attention(zf, vt, c_col):
    qi_tab = jnp.asarray([p[0] for p in _PAIRS], I32)
    kj_tab = jnp.asarray([p[1] for p in _PAIRS], I32)
    hw = FOX_HPS * FOX_HD
    grid_spec = pltpu.PrefetchScalarGridSpec(
        num_scalar_prefetch=2,
        grid=(FOX_HEADS // FOX_HPS, N_PAIRS),
        in_specs=[
            pl.BlockSpec((TQ, hw), lambda h, p, qt, kt: (qt[p], h)),
            pl.BlockSpec((TQ, hw), lambda h, p, qt, kt: (kt[p], FOX_HEADS // FOX_HPS + h)),
            pl.BlockSpec((hw, TQ), lambda h, p, qt, kt: (h, kt[p])),
            pl.BlockSpec((FOX_HPS, TQ, 1), lambda h, p, qt, kt: (h, kt[p], 0)),
        ],
        out_specs=pl.BlockSpec((TQ, hw), lambda h, p, qt, kt: (qt[p], h)),
        scratch_shapes=[
            pltpu.VMEM((FOX_HPS, 1, TQ), F32),
            pltpu.VMEM((FOX_HPS, 1, TQ), F32),
            pltpu.VMEM((FOX_HPS, FOX_HD, TQ), F32),
        ],
    )
    return pl.pallas_call(
        _fox_kernel,
        grid_spec=grid_spec,
        out_shape=jax.ShapeDtypeStruct((R, MIX_W), BF16),
        compiler_params=_cparams(("parallel", "arbitrary")),
        name="fox_attention",
    )(qi_tab, kj_tab, zf, zf, vt, c_col)


GLA_CHUNK = 64


def _gla_kernel(q_ref, k_ref, v_ref, gr_ref, zs_ref, wa2_ref, ba_ref, gn_ref, o_ref, st_ref, la_ref):
    i = pl.program_id(0)

    @pl.when(i == 0)
    def _():
        st_ref[...] = jnp.zeros_like(st_ref)

    la = jnp.dot(zs_ref[...], wa2_ref[...], precision=HIGHEST, preferred_element_type=F32)
    la_ref[...] = _log_sigmoid(la + ba_ref[...]) * (1.0 / GLA_TAU)

    c_r = lax.broadcasted_iota(I32, (GLA_CHUNK, GLA_CHUNK), 0)
    c_c = lax.broadcasted_iota(I32, (GLA_CHUNK, GLA_CHUNK), 1)
    tri_b = c_r >= c_c
    tri = tri_b.astype(F32)

    def chunk(c, carry):
        r0 = pl.multiple_of(c * GLA_CHUNK, GLA_CHUNK)
        rows = pl.ds(r0, GLA_CHUNK)
        g = la_ref[rows, :]
        b = jnp.dot(tri, g, precision=HIGHEST, preferred_element_type=F32)
        b_last = b[GLA_CHUNK - 1:GLA_CHUNK, :]
        e_last = jnp.exp(b_last)
        qt = q_ref[rows, :] * (GLA_DK ** -0.5) * jnp.exp(b)
        kt = k_ref[rows, :] * jnp.exp(-b)
        kh = kt * e_last
        for h in range(GLA_HEADS):
            ks = slice(h * GLA_DK, (h + 1) * GLA_DK)
            vs = slice(h * GLA_DV, (h + 1) * GLA_DV)
            q_h = qt[:, ks].astype(BF16)
            k_h = kt[:, ks].astype(BF16)
            kh_h = kh[:, ks].astype(BF16)
            v_h = v_ref[rows, vs]
            att = lax.dot_general(q_h, k_h, (((1,), (1,)), ((), ())), preferred_element_type=F32)
            att = jnp.where(tri_b, att, 0.0)
            st = st_ref[h]
            o = jnp.dot(att.astype(BF16), v_h.astype(BF16), preferred_element_type=F32)
            o = o + lax.dot_general(q_h, st.astype(BF16), (((1,), (1,)), ((), ())),
                                    preferred_element_type=F32)
            st_ref[h] = st * e_last[:, ks] + jnp.dot(v_h.T.astype(BF16), kh_h,
                                                     preferred_element_type=F32)
            ms = jnp.mean(o * o, axis=-1, keepdims=True)
            on = o * lax.rsqrt(ms + LN_EPS) * gn_ref[:, vs]
            gate = gr_ref[rows, vs]
            o_ref[rows, vs] = (on * (gate * _sigmoid(gate))).astype(o_ref.dtype)
        return carry

    lax.fori_loop(0, TM // GLA_CHUNK, chunk, 0)


def _gla(z, zs, wa2p, ba, gn):
    return pl.pallas_call(
        _gla_kernel,
        grid=(R // TM,),
        in_specs=[
            pl.BlockSpec((TM, 256), lambda i: (i, C_GQ // 256)),
            pl.BlockSpec((TM, 256), lambda i: (i, C_GK // 256)),
            pl.BlockSpec((TM, 512), lambda i: (i, C_GV // 512)),
            pl.BlockSpec((TM, 512), lambda i: (i, C_GR // 512)),
            pl.BlockSpec((TM, LANE), lambda i: (i, 1)),
            pl.BlockSpec((LANE, 256), lambda i: (0, 0)),
            pl.BlockSpec((1, 256), lambda i: (0, 0)),
            pl.BlockSpec((1, 512), lambda i: (0, 0)),
        ],
        out_specs=pl.BlockSpec((TM, MIX_W), lambda i: (i, 0)),
        out_shape=jax.ShapeDtypeStruct((R, MIX_W), BF16),
        scratch_shapes=[
            pltpu.VMEM((GLA_HEADS, GLA_DV, GLA_DK), F32),
            pltpu.VMEM((TM, GLA_HEADS * GLA_DK), F32),
        ],
        compiler_params=_cparams(("arbitrary",)),
        name="gla",
    )(z, z, z, z, zs, wa2p, ba, gn)


def _local_kernel(cb_ref, cc_ref, cv_ref, pz_ref, cw_ref, pw_ref, ps_ref, oc_ref, od_ref, u_sc, p_sc):
    i = pl.program_id(0)

    @pl.when(i == 0)
    def _():
        u_sc[0:HALO, :] = jnp.zeros((HALO, MIX_W), F32)
        p_sc[0:HALO, :] = jnp.zeros((HALO, MIX_W), F32)

    @pl.when(i > 0)
    def _():
        u_sc[0:HALO, :] = u_sc[TM:TM + HALO, :]
        p_sc[0:HALO, :] = p_sc[TM:TM + HALO, :]

    u = cc_ref[...] * cv_ref[...]
    pz = pz_ref[...]
    u_sc[HALO:, :] = u
    p_sc[HALO:, :] = pz

    y = (cw_ref[2:3, :] * u + cw_ref[1:2, :] * u_sc[HALO - 1:HALO - 1 + TM, :]
         + cw_ref[0:1, :] * u_sc[HALO - 2:HALO - 2 + TM, :])
    oc_ref[...] = (cb_ref[...] * y).astype(oc_ref.dtype)

    tok = i * TM - T0 + lax.broadcasted_iota(I32, (TM, 1), 0)
    cnt_small = jnp.maximum(tok + 1, 1).astype(F32)
    for g, w in enumerate(POOL_WINDOWS):
        cols = slice(g * POOL_GW, (g + 1) * POOL_GW)
        x = pz[:, cols]
        s = x
        for j in range(1, w):
            s = s + p_sc[HALO - j:HALO - j + TM, cols]
        inv_cnt = jnp.where(tok + 1 >= w, 1.0 / w, 1.0 / cnt_small)
        pooled = s * inv_cnt - x
        od = jnp.dot(pooled.astype(BF16), pw_ref[g], preferred_element_type=F32)
        od_ref[:, cols] = (od * ps_ref[:, cols]).astype(od_ref.dtype)


def _local_mixers(z, conv_w, pool_w_bf, pool_scale):
    cw = jnp.zeros((8, MIX_W), F32).at[:CONV_K].set(conv_w)
    blk = lambda c: pl.BlockSpec((TM, MIX_W), lambda i, c=c: (i, c // MIX_W))
    return pl.pallas_call(
        _local_kernel,
        grid=(R // TM,),
        in_specs=[
            blk(C_CB), blk(C_CC), blk(C_CV), blk(C_PZ),
            pl.BlockSpec((8, MIX_W), lambda i: (0, 0)),
            pl.BlockSpec((len(POOL_WINDOWS), POOL_GW, POOL_GW), lambda i: (0, 0, 0)),
            pl.BlockSpec((1, MIX_W), lambda i: (0, 0)),
        ],
        out_specs=[
            pl.BlockSpec((TM, MIX_W), lambda i: (i, 0)),
            pl.BlockSpec((TM, MIX_W), lambda i: (i, 0)),
        ],
        out_shape=[
            jax.ShapeDtypeStruct((R, MIX_W), BF16),
            jax.ShapeDtypeStruct((R, MIX_W), BF16),
        ],
        scratch_shapes=[
            pltpu.VMEM((TM + HALO, MIX_W), F32),
            pltpu.VMEM((TM + HALO, MIX_W), F32),
        ],
        compiler_params=_cparams(("arbitrary",)),
        name="conv_pool",
    )(z, z, z, z, cw, pool_w_bf, pool_scale.reshape(1, -1))


TN_MERGE = 256


def _merge_kernel(hb_ref, oa_ref, ob_ref, oc_ref, od_ref, wg0_ref, wg1_ref, wg2_ref, wg3_ref,
                  gb_ref, wb_ref, wo_ref, out_ref):
    j = pl.program_id(1)

    @pl.when(j == 0)
    def _():
        out_ref[...] = jnp.zeros_like(out_ref)

    hb = hb_ref[...]
    mixed = None
    for b, (o_ref, wg_ref) in enumerate(((oa_ref, wg0_ref), (ob_ref, wg1_ref),
                                         (oc_ref, wg2_ref), (od_ref, wg3_ref))):
        gate = _sigmoid(lax.dot_general(hb, wg_ref[...], _NT, preferred_element_type=F32)
                        + gb_ref[b:b + 1, :])
        proj = jnp.dot(o_ref[...], wb_ref[b], preferred_element_type=F32)
        term = gate * proj
        mixed = term if mixed is None else mixed + term
    out_ref[...] += jnp.dot(mixed.astype(BF16), wo_ref[...], preferred_element_type=F32)


def _merge(hb, o_a, o_b, o_c, o_d, w_all, gate_b, wb_bf, wo_bf, l):
    tn = TN_MERGE
    nj = D_MODEL // tn
    row = lambda w: pl.BlockSpec((TM, w), lambda i, j: (i, 0))
    wg = lambda b: pl.BlockSpec((None, tn, D_MODEL),
                                lambda i, j, b=b: (l, W_GATES // tn + b * nj + j, 0))
    return pl.pallas_call(
        _merge_kernel,
        grid=(R // TM, nj),
        in_specs=[
            row(D_MODEL), row(MIX_W), row(MIX_W), row(MIX_W), row(MIX_W),
            wg(0), wg(1), wg(2), wg(3),
            pl.BlockSpec((N_BRANCH, tn), lambda i, j: (0, j)),
            pl.BlockSpec((None, N_BRANCH, MIX_W, tn), lambda i, j: (l, 0, 0, j)),
            pl.BlockSpec((None, tn, D_MODEL), lambda i, j: (l, j, 0)),
        ],
        out_specs=pl.BlockSpec((TM, D_MODEL), lambda i, j: (i, 0)),
        out_shape=jax.ShapeDtypeStruct((R, D_MODEL), F32),
        compiler_params=_cparams(("parallel", "arbitrary")),
        name="merge",
    )(hb, o_a, o_b, o_c, o_d, w_all, w_all, w_all, w_all, gate_b, wb_bf, wo_bf)


def _post_ln(h, delta, g, b, row0):
    y = _layer_norm_rows(DEEPNORM_ALPHA * h + delta, g, b)
    rows = row0 + lax.broadcasted_iota(I32, (y.shape[0], 1), 0)
    return jnp.where(rows >= T0, y, 0.0)


def _first_of(cands, target):
    idx = jnp.full(target.shape, len(cands) - 1, I32)
    for j in range(len(cands) - 2, -1, -1):
        idx = jnp.where(cands[j] == target, j, idx)
    return idx


def _pick(cands, idx):
    out = cands[-1]
    for j in range(len(cands) - 2, -1, -1):
        out = jnp.where(idx == j, cands[j], out)
    return out


def _ln1_route_kernel(h_ref, mix_ref, g_ref, b_ref, rwt_ref, rb_ref,
                      h1_ref, mi_ref, mf_ref, cnt_ref, carry_sc):
    i = pl.program_id(0)

    @pl.when(i == 0)
    def _():
        carry_sc[...] = jnp.zeros_like(carry_sc)

    y = _post_ln(h_ref[...], mix_ref[...], g_ref[...], b_ref[...], i * TM)
    h1_ref[...] = y

    logits = lax.dot_general(rwt_ref[...], y, (((1,), (1,)), ((), ())), precision=HIGHEST,
                             preferred_element_type=F32)
    aff = _sigmoid(logits)
    sel = aff + rb_ref[...]
    xs = [sel[j * N_GROUPS:(j + 1) * N_GROUPS, :] for j in range(EXPERTS_PER_GROUP)]
    afs = [aff[j * N_GROUPS:(j + 1) * N_GROUPS, :] for j in range(EXPERTS_PER_GROUP)]

    score = None
    for a in range(EXPERTS_PER_GROUP):
        for bb in range(a + 1, EXPERTS_PER_GROUP):
            pair = xs[a] + xs[bb]
            score = pair if score is None else jnp.maximum(score, pair)
    giota = lax.broadcasted_iota(I32, (N_GROUPS, TM), 0)
    gmax = jnp.max(score, axis=0, keepdims=True)
    grp = jnp.min(jnp.where(score == gmax, giota, N_GROUPS), axis=0, keepdims=True)
    gsel = giota == grp
    cs = [jnp.max(jnp.where(gsel, x, -jnp.inf), axis=0, keepdims=True) for x in xs]
    acs = [jnp.sum(jnp.where(gsel, a, 0.0), axis=0, keepdims=True) for a in afs]

    m1 = jnp.maximum(jnp.maximum(cs[0], cs[1]), jnp.maximum(cs[2], cs[3]))
    i0 = _first_of(cs, m1)
    ds = [jnp.where(i0 == j, -jnp.inf, cs[j]) for j in range(EXPERTS_PER_GROUP)]
    m2 = jnp.maximum(jnp.maximum(ds[0], ds[1]), jnp.maximum(ds[2], ds[3]))
    i1 = _first_of(ds, m2)
    a0 = _pick(acs, i0)
    a1 = _pick(acs, i1)
    denom = a0 + a1

    pos = i * TM + lax.broadcasted_iota(I32, (1, TM), 1)
    valid = pos >= T0
    riota = lax.broadcasted_iota(I32, (N_EXPERTS, TM), 0)
    oh0 = (riota == i0 * N_GROUPS + grp) & valid
    oh1 = (riota == i1 * N_GROUPS + grp) & valid
    ohf = jnp.where(oh0 | oh1, 1.0, 0.0)
    before = (lax.broadcasted_iota(I32, (TM, TM), 0)
              < lax.broadcasted_iota(I32, (TM, TM), 1)).astype(BF16)
    cum = jnp.dot(ohf.astype(BF16), before, preferred_element_type=F32) + carry_sc[...]
    rank0 = jnp.sum(jnp.where(oh0, cum, 0.0), axis=0, keepdims=True)
    rank1 = jnp.sum(jnp.where(oh1, cum, 0.0), axis=0, keepdims=True)
    carry = carry_sc[...] + jnp.sum(ohf, axis=1, keepdims=True)
    carry_sc[...] = carry
    cnt_ref[...] = jnp.broadcast_to(carry, cnt_ref.shape)

    zi = jnp.zeros((1, TM), I32)
    mi_ref[...] = jnp.concatenate(
        [grp * EXPERTS_PER_GROUP + i0, grp * EXPERTS_PER_GROUP + i1,
         rank0.astype(I32), rank1.astype(I32), zi, zi, zi, zi], axis=0)
    zf = jnp.zeros((1, TM), F32)
    mf_ref[...] = jnp.concatenate([a0 / denom, a1 / denom, zf, zf, zf, zf, zf, zf], axis=0)


def _ln1_route(h, mix, g, b, router_wt, router_bc):
    row = pl.BlockSpec((TM, D_MODEL), lambda i: (i, 0))
    vec = pl.BlockSpec((1, D_MODEL), lambda i: (0, 0))
    meta = pl.BlockSpec((8, TM), lambda i: (0, i))
    return pl.pallas_call(
        _ln1_route_kernel,
        grid=(R // TM,),
        in_specs=[row, row, vec, vec,
                  pl.BlockSpec((N_EXPERTS, D_MODEL), lambda i: (0, 0)),
                  pl.BlockSpec((N_EXPERTS, 1), lambda i: (0, 0))],
        out_specs=[row, meta, meta, pl.BlockSpec((N_EXPERTS, LANE), lambda i: (0, 0))],
        out_shape=[
            jax.ShapeDtypeStruct((R, D_MODEL), F32),
            jax.ShapeDtypeStruct((8, R), I32),
            jax.ShapeDtypeStruct((8, R), F32),
            jax.ShapeDtypeStruct((N_EXPERTS, LANE), F32),
        ],
        scratch_shapes=[pltpu.VMEM((N_EXPERTS, 1), F32)],
        compiler_params=_cparams(("arbitrary",)),
        name="ln1_route",
    )(h, mix, g.reshape(1, -1), b.reshape(1, -1), router_wt, router_bc)


def _dispatch_tables(mi, counts_slot_major):
    counts = counts_slot_major.reshape(EXPERTS_PER_GROUP, N_GROUPS).T.reshape(N_EXPERTS).astype(I32)
    padded = (counts + EXPERT_BLOCK - 1) // EXPERT_BLOCK * EXPERT_BLOCK
    pad_end = jnp.cumsum(padded)
    pad_start = pad_end - padded
    e_iota = jnp.arange(N_EXPERTS, dtype=I32)
    rows_ok = jnp.arange(R) >= T0

    def dest(eid, rank):
        start = jnp.sum(jnp.where(eid[:, None] == e_iota[None, :], pad_start[None, :], 0), axis=1)
        return jnp.where(rows_ok, start + rank, 0).astype(I32)

    d0 = dest(mi[0], mi[2])
    d1 = dest(mi[1], mi[3])
    blk_start = (pad_start // EXPERT_BLOCK).astype(I32)
    n_blk = (padded // EXPERT_BLOCK).astype(I32)
    n_used = (pad_end[-1] // EXPERT_BLOCK).astype(I32).reshape(1)
    tok_rows = jnp.arange(T0, R, dtype=I32)
    row_src = jnp.zeros((N_ROWS + EXPERT_BLOCK,), I32).at[jnp.concatenate([d0[T0:], d1[T0:]])].set(
        jnp.concatenate([tok_rows, tok_rows]), unique_indices=True)
    return d0, d1, blk_start, n_blk, n_used, row_src


def _row_copy(src, src_row, dst, dst_row, sem):
    return pltpu.make_async_copy(src.at[pl.ds(src_row, 1), :], dst.at[pl.ds(dst_row, 1), :], sem)


def _issue_row_gather(rs_ref, g, h_hbm, buf, sem):
    base = g * EXPERT_BLOCK
    for r in range(EXPERT_BLOCK):
        _row_copy(h_hbm, rs_ref[base + r], buf, r, sem).start(priority=1)


def _wait_row_gather(h_hbm, buf, sem):
    for r in range(EXPERT_BLOCK):
        _row_copy(h_hbm, 0, buf, r, sem).wait()


def _block_rows(g):
    return pl.ds(pl.multiple_of(g * EXPERT_BLOCK, EXPERT_BLOCK), EXPERT_BLOCK)


def _finish_writes(out_copy, obuf, nu):
    @pl.when(nu >= 2)
    def _():
        out_copy(nu - 2, nu % 2).wait()

    @pl.when(nu >= 1)
    def _():
        out_copy(nu - 1, (nu - 1) % 2).wait()

    obuf[0] = jnp.zeros(obuf.shape[1:], obuf.dtype)

    def zero_block(g, carry):
        cp = out_copy(g, 0)
        cp.start()
        cp.wait()
        return carry

    lax.fori_loop(nu, N_BLOCKS, zero_block, 0)


def _moe_up_kernel(bs_ref, nb_ref, nu_ref, rs_ref, h_hbm, wg_ref, wu_ref, o_hbm,
                   wg_sc, wu_sc, xbuf, obuf, xsem, osem):
    e = pl.program_id(0)
    nb = nb_ref[e]
    g0 = bs_ref[e]

    def out_copy(g, slot):
        return pltpu.make_async_copy(obuf.at[slot], o_hbm.at[_block_rows(g), :], osem.at[slot])

    @pl.when(e == 0)
    def _():
        _issue_row_gather(rs_ref, 0, h_hbm, xbuf.at[0], xsem.at[0])

    @pl.when(nb > 0)
    def _():
        wg_sc[...] = wg_ref[...].astype(BF16)
        wu_sc[...] = wu_ref[...].astype(BF16)

        def block(j, carry):
            g = g0 + j
            slot = g % 2

            @pl.when(g >= 2)
            def _():
                out_copy(g - 2, slot).wait()

            _wait_row_gather(h_hbm, xbuf.at[slot], xsem.at[slot])
            _issue_row_gather(rs_ref, g + 1, h_hbm, xbuf.at[1 - slot], xsem.at[1 - slot])
            x = xbuf[slot].astype(BF16)
            gate = jnp.dot(x, wg_sc[...], preferred_element_type=F32)
            up = jnp.dot(x, wu_sc[...], preferred_element_type=F32)
            obuf[slot] = (gate * _sigmoid(gate) * up).astype(BF16)
            out_copy(g, slot).start()
            return carry

        lax.fori_loop(0, nb, block, 0)

    @pl.when(e == N_EXPERTS - 1)
    def _():
        nu = nu_ref[0]
        _wait_row_gather(h_hbm, xbuf.at[nu % 2], xsem.at[nu % 2])
        _finish_writes(out_copy, obuf, nu)


def _moe_up(blk_start, n_blk, n_used, row_src, h1, w_gate, w_up, l):
    wspec = pl.BlockSpec((None, None, D_MODEL, D_EXPERT), lambda e, *_: (l, e, 0, 0))
    grid_spec = pltpu.PrefetchScalarGridSpec(
        num_scalar_prefetch=4,
        grid=(N_EXPERTS,),
        in_specs=[pl.BlockSpec(memory_space=pl.ANY), wspec, wspec],
        out_specs=pl.BlockSpec(memory_space=pl.ANY),
        scratch_shapes=[
            pltpu.VMEM((D_MODEL, D_EXPERT), BF16),
            pltpu.VMEM((D_MODEL, D_EXPERT), BF16),
            pltpu.VMEM((2, EXPERT_BLOCK, D_MODEL), F32),
            pltpu.VMEM((2, EXPERT_BLOCK, D_EXPERT), BF16),
            pltpu.SemaphoreType.DMA((2,)),
            pltpu.SemaphoreType.DMA((2,)),
        ],
    )
    return pl.pallas_call(
        _moe_up_kernel,
        grid_spec=grid_spec,
        out_shape=jax.ShapeDtypeStruct((N_ROWS, D_EXPERT), BF16),
        compiler_params=_cparams(("arbitrary",), vmem=MOE_VMEM_LIMIT),
        name="moe_up",
    )(blk_start, n_blk, n_used, row_src, h1, w_gate, w_up)


def _moe_down_kernel(bs_ref, nb_ref, nu_ref, x_hbm, wd_ref, y_hbm, wd_sc, xbuf, obuf, xsem, osem):
    e = pl.program_id(0)
    nb = nb_ref[e]
    g0 = bs_ref[e]
    nu = nu_ref[0]

    def in_copy(g, slot):
        return pltpu.make_async_copy(x_hbm.at[_block_rows(g), :], xbuf.at[slot], xsem.at[slot])

    def out_copy(g, slot):
        return pltpu.make_async_copy(obuf.at[slot], y_hbm.at[_block_rows(g), :], osem.at[slot])

    @pl.when((e == 0) & (nu > 0))
    def _():
        in_copy(0, 0).start()

    @pl.when(nb > 0)
    def _():
        wd_sc[...] = wd_ref[...].astype(BF16)

        def block(j, carry):
            g = g0 + j
            slot = g % 2

            @pl.when(g >= 2)
            def _():
                out_copy(g - 2, slot).wait()

            in_copy(g, slot).wait()

            @pl.when(g + 1 < nu)
            def _():
                in_copy(g + 1, 1 - slot).start()

            obuf[slot] = jnp.dot(xbuf[slot], wd_sc[...], preferred_element_type=F32)
            out_copy(g, slot).start()
            return carry

        lax.fori_loop(0, nb, block, 0)

    @pl.when(e == N_EXPERTS - 1)
    def _():
        _finish_writes(out_copy, obuf, nu)


def _moe_down(blk_start, n_blk, n_used, hmid, w_down, l):
    grid_spec = pltpu.PrefetchScalarGridSpec(
        num_scalar_prefetch=3,
        grid=(N_EXPERTS,),
        in_specs=[
            pl.BlockSpec(memory_space=pl.ANY),
            pl.BlockSpec((None, None, D_EXPERT, D_MODEL), lambda e, *_: (l, e, 0, 0)),
        ],
        out_specs=pl.BlockSpec(memory_space=pl.ANY),
        scratch_shapes=[
            pltpu.VMEM((D_EXPERT, D_MODEL), BF16),
            pltpu.VMEM((2, EXPERT_BLOCK, D_EXPERT), BF16),
            pltpu.VMEM((2, EXPERT_BLOCK, D_MODEL), F32),
            pltpu.SemaphoreType.DMA((2,)),
            pltpu.SemaphoreType.DMA((2,)),
        ],
    )
    return pl.pallas_call(
        _moe_down_kernel,
        grid_spec=grid_spec,
        out_shape=jax.ShapeDtypeStruct((N_ROWS, D_MODEL), F32),
        compiler_params=_cparams(("arbitrary",), vmem=MOE_VMEM_LIMIT),
        name="moe_down",
    )(blk_start, n_blk, n_used, hmid, w_down)


def _combine_kernel(d0_ref, d1_ref, y_hbm, h1_ref, mf_ref, g_ref, b_ref, h2_ref, h2b_ref,
                    buf0, buf1, sem):
    i = pl.program_id(0)
    slot = i % 2

    def issue(tile, s):
        base = tile * LANE

        def body(r, carry):
            _row_copy(y_hbm, d0_ref[base + r], buf0.at[s], r, sem.at[s]).start(priority=0)
            _row_copy(y_hbm, d1_ref[base + r], buf1.at[s], r, sem.at[s]).start(priority=1)
            return carry

        lax.fori_loop(0, LANE, body, 0)

    @pl.when(i == 0)
    def _():
        issue(0, 0)

    @pl.when(i + 1 < pl.num_programs(0))
    def _():
        issue(i + 1, 1 - slot)

    def drain(r, carry):
        _row_copy(y_hbm, 0, buf0.at[slot], r, sem.at[slot]).wait()
        _row_copy(y_hbm, 0, buf1.at[slot], r, sem.at[slot]).wait()
        return carry

    lax.fori_loop(0, LANE, drain, 0)
    wt = mf_ref[...].T
    ffn = wt[:, 0:1] * buf0[slot] + wt[:, 1:2] * buf1[slot]
    y = _post_ln(h1_ref[...], ffn, g_ref[...], b_ref[...], i * LANE)
    h2_ref[...] = y
    h2b_ref[...] = y.astype(BF16)


def _combine_ln2(d0, d1, y_rows, h1, mf, g, b):
    row = lambda i, d0, d1: (i, 0)
    vec = pl.BlockSpec((1, D_MODEL), lambda i, d0, d1: (0, 0))
    grid_spec = pltpu.PrefetchScalarGridSpec(
        num_scalar_prefetch=2,
        grid=(R // LANE,),
        in_specs=[
            pl.BlockSpec(memory_space=pl.ANY),
            pl.BlockSpec((LANE, D_MODEL), row),
            pl.BlockSpec((8, LANE), lambda i, d0, d1: (0, i)),
            vec, vec,
        ],
        out_specs=[pl.BlockSpec((LANE, D_MODEL), row), pl.BlockSpec((LANE, D_MODEL), row)],
        scratch_shapes=[
            pltpu.VMEM((2, LANE, D_MODEL), F32),
            pltpu.VMEM((2, LANE, D_MODEL), F32),
            pltpu.SemaphoreType.DMA((2,)),
        ],
    )
    return pl.pallas_call(
        _combine_kernel,
        grid_spec=grid_spec,
        out_shape=[
            jax.ShapeDtypeStruct((R, D_MODEL), F32),
            jax.ShapeDtypeStruct((R, D_MODEL), BF16),
        ],
        compiler_params=_cparams(("arbitrary",)),
        name="moe_combine_ln2",
    )(d0, d1, y_rows, h1, mf, g.reshape(1, -1), b.reshape(1, -1))


def kernel(x, meta_tokens, ln_in_g, ln_in_b, w_in, fox_f_bias, gla_wa2, gla_ba, gla_norm_g, conv_w, pool_w, pool_scale, gate_b, w_branch, w_out, ln1_g, ln1_b, router_w, router_b, w_gate, w_up, w_down, ln2_g, ln2_b):
    assert x.shape == (1, SEQ, D_MODEL)
    h, hb = _ln_in(x.reshape(SEQ, D_MODEL), meta_tokens, ln_in_g, ln_in_b)
    router_wt = router_w.T.reshape(N_GROUPS, EXPERTS_PER_GROUP, D_MODEL).transpose(1, 0, 2).reshape(
        N_EXPERTS, D_MODEL)
    router_bc = router_b.astype(F32).reshape(N_GROUPS, EXPERTS_PER_GROUP).T.reshape(N_EXPERTS, 1)

    wb_bf = w_branch.astype(BF16)
    wo_bf = w_out.astype(BF16)

    w_all = _transposed_weights(w_in)

    for l in range(DEPTH):
        zf = _matmul_nt(hb, w_all, l, W_FOX, 2 * MIX_W, BF16, TM_PROJ, 512, "proj_fox")
        vt = _matmul_tt(w_all, l, W_FOX + 2 * MIX_W, MIX_W, hb, BF16, TM_PROJ, "proj_fox_vt")
        z = _matmul_nt(hb, w_all, l, W_MIX, N_MIXC, F32, TM_PROJ, 512, "proj_mix")
        zs = _matmul_nt(hb, w_all, l, W_SMALL, 2 * LANE, F32, TM_PROJ, 2 * LANE, "proj_small")

        bias_row = jnp.zeros((1, LANE), F32).at[0, SM_FF:SM_FF + FOX_HEADS].set(fox_f_bias[l])
        c = _fox_gate(zs, bias_row)[:, SM_FF:SM_FF + FOX_HEADS]
        o_a = _fox_attention(zf, vt, c.T.reshape(FOX_HEADS, R, 1))

        wa2p = jnp.zeros((LANE, GLA_HEADS * GLA_DK), F32).at[SM_GA:SM_GA + GLA_RANK].set(gla_wa2[l])
        o_b = _gla(z, zs, wa2p, gla_ba[l].reshape(1, -1), gla_norm_g[l].reshape(1, -1))

        o_c, o_d = _local_mixers(z, conv_w[l], pool_w[l].astype(BF16), pool_scale[l])

        mix = _merge(hb, o_a, o_b, o_c, o_d, w_all, gate_b[l], wb_bf, wo_bf, l)
        h1, mi, mf, counts = _ln1_route(h, mix, ln1_g[l], ln1_b[l], router_wt, router_bc)

        d0, d1, blk_start, n_blk, n_used, row_src = _dispatch_tables(mi, counts[:, 0])
        hmid = _moe_up(blk_start, n_blk, n_used, row_src, h1, w_gate, w_up, l)
        y_rows = _moe_down(blk_start, n_blk, n_used, hmid, w_down, l)
        h, hb = _combine_ln2(d0, d1, y_rows, h1, mf, ln2_g[l], ln2_b[l])

    return h[PAD_ROWS + N_META:].reshape(1, SEQ, D_MODEL)
```

```python
import jax
import jax.numpy as jnp
import numpy as np
from jax import lax
from jax.experimental import pallas as pl
from jax.experimental.pallas import tpu as pltpu

F32 = jnp.float32
BF16 = jnp.bfloat16
I32 = jnp.int32
HIGHEST = lax.Precision.HIGHEST

D_MODEL = 2048
SEQ = 8192
DEPTH = 2
N_META = 16
N_BRANCH = 4
MIX_W = 512
FOX_HEADS = 4
FOX_HD = 128
GLA_HEADS = 4
GLA_DK = 64
GLA_DV = 128
GLA_RANK = 16
GLA_TAU = 16.0
CONV_K = 3
POOL_WINDOWS = (2, 4, 8, 16)
POOL_GW = 128
N_EXPERTS = 32
N_GROUPS = 8
EXPERTS_PER_GROUP = 4
TOP_K = 2
D_EXPERT = 1024
LN_EPS = 1e-5
DEEPNORM_ALPHA = (2 * DEPTH) ** 0.25

_SPLITS = (512, 512, 512, 4, 256, 256, 512, 16, 512, 512, 512, 512, 512, 8192)
_OFFS = [int(o) for o in np.concatenate([[0], np.cumsum(_SPLITS)])]
(O_FQ, O_FK, O_FV, O_FF, O_GQ, O_GK, O_GV, O_GA, O_GR, O_CB, O_CC, O_CV, O_PZ, O_GZ, P_IN) = _OFFS

LANE = 128
PAD_ROWS = LANE - N_META
T0 = PAD_ROWS
N_TOK = N_META + SEQ
R = PAD_ROWS + N_TOK
TM = 640
TM_PROJ = 1664
HALO = 16

WT_TILE = 128
W_FOX, W_MIX, W_GATES = 0, 1536, 5120
N_FOX, N_MIXC, N_GATES = 1536, 3584, 8192
W_SMALL = W_GATES + N_GATES
N_WT = W_SMALL + 2 * LANE
C_GQ, C_GK, C_GV, C_GR, C_CB, C_CC, C_CV, C_PZ = 0, 256, 512, 1024, 1536, 2048, 2560, 3072
SM_FF_TILE, SM_GA_TILE = O_FF // LANE, O_GA // LANE
SM_FF = O_FF - SM_FF_TILE * LANE
SM_GA = O_GA - SM_GA_TILE * LANE


def _wt_sources():
    src = []
    for r in range(0, W_SMALL, WT_TILE):
        if r < N_FOX:
            src.append(O_FQ + r)
        elif r < W_MIX + 1024:
            src.append(O_GQ + r - W_MIX)
        else:
            src.append(O_GR + r - (W_MIX + 1024))
    src += [SM_FF_TILE * LANE, SM_GA_TILE * LANE]
    assert all(0 <= s and s + WT_TILE <= P_IN for s in src)
    return src


WT_SRC = _wt_sources()

EXPERT_BLOCK = 128
N_FLAT = N_TOK * TOP_K
N_BLOCKS = -(-N_FLAT // EXPERT_BLOCK) + N_EXPERTS
N_ROWS = N_BLOCKS * EXPERT_BLOCK

NEG = -1e30
VMEM_LIMIT = 48 * 1024 * 1024
MOE_VMEM_LIMIT = 56 * 1024 * 1024


def _cparams(sem, vmem=VMEM_LIMIT):
    return pltpu.CompilerParams(dimension_semantics=sem, vmem_limit_bytes=vmem)


def _log_sigmoid(x):
    return jnp.minimum(x, 0.0) - jnp.log1p(jnp.exp(-jnp.abs(x)))


def _sigmoid(x):
    return 1.0 / (1.0 + jnp.exp(-x))


def _layer_norm_rows(x, g, b):
    mu = jnp.mean(x, axis=-1, keepdims=True)
    xc = x - mu
    var = jnp.mean(xc * xc, axis=-1, keepdims=True)
    return xc * lax.rsqrt(var + LN_EPS) * g + b


def _ln_in_kernel(x_ref, meta_ref, g_ref, b_ref, h_ref, hb_ref):
    i = pl.program_id(0)

    @pl.when(i == 0)
    def _():
        h_ref[...] = jnp.zeros_like(h_ref)
        hb_ref[...] = jnp.zeros_like(hb_ref)
        m = _layer_norm_rows(meta_ref[...], g_ref[...], b_ref[...])
        h_ref[PAD_ROWS:, :] = m
        hb_ref[PAD_ROWS:, :] = m.astype(BF16)

    @pl.when(i > 0)
    def _():
        y = _layer_norm_rows(x_ref[...], g_ref[...], b_ref[...])
        h_ref[...] = y
        hb_ref[...] = y.astype(BF16)


def _ln_in(x2d, meta, g, b):
    nb = R // LANE
    return pl.pallas_call(
        _ln_in_kernel,
        grid=(nb,),
        in_specs=[
            pl.BlockSpec((LANE, D_MODEL), lambda i: (jnp.maximum(i - 1, 0), 0)),
            pl.BlockSpec((N_META, D_MODEL), lambda i: (0, 0)),
            pl.BlockSpec((1, D_MODEL), lambda i: (0, 0)),
            pl.BlockSpec((1, D_MODEL), lambda i: (0, 0)),
        ],
        out_specs=[
            pl.BlockSpec((LANE, D_MODEL), lambda i: (i, 0)),
            pl.BlockSpec((LANE, D_MODEL), lambda i: (i, 0)),
        ],
        out_shape=[
            jax.ShapeDtypeStruct((R, D_MODEL), F32),
            jax.ShapeDtypeStruct((R, D_MODEL), BF16),
        ],
        compiler_params=_cparams(("arbitrary",)),
        name="ln_in",
    )(x2d, meta, g.reshape(1, -1), b.reshape(1, -1))


D_CHUNKS = D_MODEL // LANE
FLAT_ROWS = D_CHUNKS * DEPTH


def _wt_kernel(src_ref, w_hbm, o_ref, buf, sem):
    j = pl.program_id(0)
    slot = j % 2

    def tile_copy(t, s):
        row0 = pl.multiple_of(src_ref[t] * FLAT_ROWS, FLAT_ROWS)
        return pltpu.make_async_copy(w_hbm.at[pl.ds(row0, WT_TILE * FLAT_ROWS), :], buf.at[s],
                                     sem.at[s])

    @pl.when(j == 0)
    def _():
        tile_copy(0, 0).start()

    @pl.when(j + 1 < pl.num_programs(0))
    def _():
        tile_copy(j + 1, 1 - slot).start()

    tile_copy(j, slot).wait()
    for l in range(DEPTH):
        for c in range(D_CHUNKS):
            o_ref[l, :, c * LANE:(c + 1) * LANE] = buf[
                slot, pl.ds(c * DEPTH + l, WT_TILE, stride=FLAT_ROWS), :].astype(BF16)


def _transposed_weights(w_in):
    flat = w_in.reshape(DEPTH, D_CHUNKS, LANE, P_IN).transpose(3, 1, 0, 2).reshape(
        P_IN * FLAT_ROWS, LANE)
    grid_spec = pltpu.PrefetchScalarGridSpec(
        num_scalar_prefetch=1,
        grid=(N_WT // WT_TILE,),
        in_specs=[pl.BlockSpec(memory_space=pl.ANY)],
        out_specs=pl.BlockSpec((DEPTH, WT_TILE, D_MODEL), lambda j, src: (0, j, 0)),
        scratch_shapes=[
            pltpu.VMEM((2, WT_TILE * FLAT_ROWS, LANE), F32),
            pltpu.SemaphoreType.DMA((2,)),
        ],
    )
    return pl.pallas_call(
        _wt_kernel,
        grid_spec=grid_spec,
        out_shape=jax.ShapeDtypeStruct((DEPTH, N_WT, D_MODEL), BF16),
        compiler_params=_cparams(("arbitrary",)),
        name="transposed_weights",
    )(jnp.asarray(WT_SRC, I32), flat)


_NT = (((1,), (1,)), ((), ()))


def _mm_nt_kernel(a_ref, wt_ref, o_ref):
    o_ref[...] = lax.dot_general(a_ref[...], wt_ref[...], _NT,
                                 preferred_element_type=F32).astype(o_ref.dtype)


def _matmul_nt(a, wt, l, row0, n, out_dtype, tm, tn, name):
    m, k = a.shape
    return pl.pallas_call(
        _mm_nt_kernel,
        grid=(m // tm, n // tn),
        in_specs=[
            pl.BlockSpec((tm, k), lambda i, j: (i, 0)),
            pl.BlockSpec((None, tn, k), lambda i, j: (l, row0 // tn + j, 0)),
        ],
        out_specs=pl.BlockSpec((tm, tn), lambda i, j: (i, j)),
        out_shape=jax.ShapeDtypeStruct((m, n), out_dtype),
        compiler_params=_cparams(("parallel", "arbitrary")),
        name=name,
    )(a, wt)


def _matmul_tt(wt, l, row0, n, a, out_dtype, tm, name):
    m, k = a.shape
    return pl.pallas_call(
        _mm_nt_kernel,
        grid=(m // tm,),
        in_specs=[
            pl.BlockSpec((None, n, k), lambda i: (l, row0 // n, 0)),
            pl.BlockSpec((tm, k), lambda i: (i, 0)),
        ],
        out_specs=pl.BlockSpec((n, tm), lambda i: (0, i)),
        out_shape=jax.ShapeDtypeStruct((n, m), out_dtype),
        compiler_params=_cparams(("parallel",)),
        name=name,
    )(wt, a)


def _fox_gate_kernel(zs_ref, bias_ref, c_ref, carry_ref):
    i = pl.program_id(0)

    @pl.when(i == 0)
    def _():
        carry_ref[...] = jnp.zeros_like(carry_ref)

    rows = i * TM + lax.broadcasted_iota(I32, (TM, LANE), 0)
    lf = _log_sigmoid(zs_ref[...] + bias_ref[...])
    lf = jnp.where(rows >= T0, lf, 0.0)
    tri = (lax.broadcasted_iota(I32, (TM, TM), 0)
           >= lax.broadcasted_iota(I32, (TM, TM), 1)).astype(F32)
    c = jnp.dot(tri, lf, precision=HIGHEST, preferred_element_type=F32) + carry_ref[...]
    c_ref[...] = c
    carry_ref[...] = c[TM - 1:TM, :]


def _fox_gate(zs, bias_row):
    return pl.pallas_call(
        _fox_gate_kernel,
        grid=(R // TM,),
        in_specs=[
            pl.BlockSpec((TM, LANE), lambda i: (i, 0)),
            pl.BlockSpec((1, LANE), lambda i: (0, 0)),
        ],
        out_specs=pl.BlockSpec((TM, LANE), lambda i: (i, 0)),
        out_shape=jax.ShapeDtypeStruct((R, LANE), F32),
        scratch_shapes=[pltpu.VMEM((1, LANE), F32)],
        compiler_params=_cparams(("arbitrary",)),
        name="fox_gate",
    )(zs, bias_row)


TQ = TM
N_QB = R // TQ
_PAIRS = [(qi, kj) for qi in range(N_QB) for kj in range(qi + 1)]
N_PAIRS = len(_PAIRS)


LOG2E = 1.4426950408889634
FOX_HPS = FOX_HEADS


def _fox_kernel(qi_tab, kj_tab, q_ref, k_ref, vt_ref, ck_ref, o_ref, m_sc, l_sc, acc_sc):
    p = pl.program_id(1)
    qi = qi_tab[p]
    kj = kj_tab[p]

    @pl.when(kj == 0)
    def _():
        m_sc[...] = jnp.full_like(m_sc, NEG)
        l_sc[...] = jnp.zeros_like(l_sc)
        acc_sc[...] = jnp.zeros_like(acc_sc)

    kpos = kj * TQ + lax.broadcasted_iota(I32, (TQ, 1), 0)
    c1 = FOX_HD ** -0.5 * LOG2E

    def step(causal):
        for hh in range(FOX_HPS):
            lanes = slice(hh * FOX_HD, (hh + 1) * FOX_HD)
            ck = ck_ref[:, SM_FF + hh:SM_FF + hh + 1]
            ckl = jnp.where(kpos >= T0, ck * LOG2E, -NEG)
            t = lax.dot_general(k_ref[:, lanes], q_ref[:, lanes], (((1,), (1,)), ((), ())),
                                preferred_element_type=F32) * c1 - ckl
            if causal:
                ahead = (lax.broadcasted_iota(I32, (TQ, TQ), 0)
                         - lax.broadcasted_iota(I32, (TQ, TQ), 1))
                t = jnp.where(ahead <= 0, t, NEG)
            m_prev = m_sc[hh]
            m_new = jnp.maximum(m_prev, jnp.max(t, axis=0, keepdims=True))
            alpha = jnp.exp2(m_prev - m_new)
            pr = jnp.exp2(t - m_new)
            l_sc[hh] = alpha * l_sc[hh] + jnp.sum(pr, axis=0, keepdims=True)
            acc_sc[hh] = alpha * acc_sc[hh] + jnp.dot(vt_ref[lanes, :], pr.astype(BF16),
                                                      preferred_element_type=F32)
            m_sc[hh] = m_new

    @pl.when(kj == qi)
    def _():
        step(True)

    @pl.when(kj != qi)
    def _():
        step(False)

    @pl.when(kj == qi)
    def _():
        for hh in range(FOX_HPS):
            o_ref[:, hh * FOX_HD:(hh + 1) * FOX_HD] = (acc_sc[hh] / l_sc[hh]).T.astype(o_ref.dtype)


def _fox_attention(zf, vt, c_col):
    qi_tab = jnp.asarray([p[0] for p in _PAIRS], I32)
    kj_tab = jnp.asarray([p[1] for p in _PAIRS], I32)
    hw = FOX_HPS * FOX_HD
    grid_spec = pltpu.PrefetchScalarGridSpec(
        num_scalar_prefetch=2,
        grid=(FOX_HEADS // FOX_HPS, N_PAIRS),
        in_specs=[
            pl.BlockSpec((TQ, hw), lambda h, p, qt, kt: (qt[p], h)),
            pl.BlockSpec((TQ, hw), lambda h, p, qt, kt: (kt[p], FOX_HEADS // FOX_HPS + h)),
            pl.BlockSpec((hw, TQ), lambda h, p, qt, kt: (h, kt[p])),
            pl.BlockSpec((TQ, LANE), lambda h, p, qt, kt: (kt[p], 0)),
        ],
        out_specs=pl.BlockSpec((TQ, hw), lambda h, p, qt, kt: (qt[p], h)),
        scratch_shapes=[
            pltpu.VMEM((FOX_HPS, 1, TQ), F32),
            pltpu.VMEM((FOX_HPS, 1, TQ), F32),
            pltpu.VMEM((FOX_HPS, FOX_HD, TQ), F32),
        ],
    )
    return pl.pallas_call(
        _fox_kernel,
        grid_spec=grid_spec,
        out_shape=jax.ShapeDtypeStruct((R, MIX_W), BF16),
        compiler_params=_cparams(("parallel", "arbitrary")),
        name="fox_attention",
    )(qi_tab, kj_tab, zf, zf, vt, c_col)


GLA_CHUNK = 64


def _gla_kernel(q_ref, k_ref, v_ref, gr_ref, zs_ref, wa2_ref, ba_ref, gn_ref, o_ref, st_ref, la_ref):
    i = pl.program_id(0)

    @pl.when(i == 0)
    def _():
        st_ref[...] = jnp.zeros_like(st_ref)

    la = jnp.dot(zs_ref[...], wa2_ref[...], precision=HIGHEST, preferred_element_type=F32)
    la_ref[...] = _log_sigmoid(la + ba_ref[...]) * (1.0 / GLA_TAU)

    c_r = lax.broadcasted_iota(I32, (GLA_CHUNK, GLA_CHUNK), 0)
    c_c = lax.broadcasted_iota(I32, (GLA_CHUNK, GLA_CHUNK), 1)
    tri_b = c_r >= c_c
    tri = tri_b.astype(F32)

    def chunk(c, carry):
        r0 = pl.multiple_of(c * GLA_CHUNK, GLA_CHUNK)
        rows = pl.ds(r0, GLA_CHUNK)
        g = la_ref[rows, :]
        b = jnp.dot(tri, g, precision=HIGHEST, preferred_element_type=F32)
        b_last = b[GLA_CHUNK - 1:GLA_CHUNK, :]
        e_last = jnp.exp(b_last)
        qt = q_ref[rows, :] * (GLA_DK ** -0.5) * jnp.exp(b)
        kt = k_ref[rows, :] * jnp.exp(-b)
        kh = kt * e_last
        for h in range(GLA_HEADS):
            ks = slice(h * GLA_DK, (h + 1) * GLA_DK)
            vs = slice(h * GLA_DV, (h + 1) * GLA_DV)
            q_h = qt[:, ks].astype(BF16)
            k_h = kt[:, ks].astype(BF16)
            kh_h = kh[:, ks].astype(BF16)
            v_h = v_ref[rows, vs]
            att = lax.dot_general(q_h, k_h, (((1,), (1,)), ((), ())), preferred_element_type=F32)
            att = jnp.where(tri_b, att, 0.0)
            st = st_ref[h]
            o = jnp.dot(att.astype(BF16), v_h.astype(BF16), preferred_element_type=F32)
            o = o + lax.dot_general(q_h, st.astype(BF16), (((1,), (1,)), ((), ())),
                                    preferred_element_type=F32)
            st_ref[h] = st * e_last[:, ks] + jnp.dot(v_h.T.astype(BF16), kh_h,
                                                     preferred_element_type=F32)
            ms = jnp.mean(o * o, axis=-1, keepdims=True)
            on = o * lax.rsqrt(ms + LN_EPS) * gn_ref[:, vs]
            gate = gr_ref[rows, vs]
            o_ref[rows, vs] = (on * (gate * _sigmoid(gate))).astype(o_ref.dtype)
        return carry

    lax.fori_loop(0, TM // GLA_CHUNK, chunk, 0)


def _gla(z, zs, wa2p, ba, gn):
    return pl.pallas_call(
        _gla_kernel,
        grid=(R // TM,),
        in_specs=[
            pl.BlockSpec((TM, 256), lambda i: (i, C_GQ // 256)),
            pl.BlockSpec((TM, 256), lambda i: (i, C_GK // 256)),
            pl.BlockSpec((TM, 512), lambda i: (i, C_GV // 512)),
            pl.BlockSpec((TM, 512), lambda i: (i, C_GR // 512)),
            pl.BlockSpec((TM, LANE), lambda i: (i, 1)),
            pl.BlockSpec((LANE, 256), lambda i: (0, 0)),
            pl.BlockSpec((1, 256), lambda i: (0, 0)),
            pl.BlockSpec((1, 512), lambda i: (0, 0)),
        ],
        out_specs=pl.BlockSpec((TM, MIX_W), lambda i: (i, 0)),
        out_shape=jax.ShapeDtypeStruct((R, MIX_W), BF16),
        scratch_shapes=[
            pltpu.VMEM((GLA_HEADS, GLA_DV, GLA_DK), F32),
            pltpu.VMEM((TM, GLA_HEADS * GLA_DK), F32),
        ],
        compiler_params=_cparams(("arbitrary",)),
        name="gla",
    )(z, z, z, z, zs, wa2p, ba, gn)


def _local_kernel(cb_ref, cc_ref, cv_ref, pz_ref, cw_ref, pw_ref, ps_ref, oc_ref, od_ref, u_sc, p_sc):
    i = pl.program_id(0)

    @pl.when(i == 0)
    def _():
        u_sc[0:HALO, :] = jnp.zeros((HALO, MIX_W), F32)
        p_sc[0:HALO, :] = jnp.zeros((HALO, MIX_W), F32)

    @pl.when(i > 0)
    def _():
        u_sc[0:HALO, :] = u_sc[TM:TM + HALO, :]
        p_sc[0:HALO, :] = p_sc[TM:TM + HALO, :]

    u = cc_ref[...] * cv_ref[...]
    pz = pz_ref[...]
    u_sc[HALO:, :] = u
    p_sc[HALO:, :] = pz

    y = (cw_ref[2:3, :] * u + cw_ref[1:2, :] * u_sc[HALO - 1:HALO - 1 + TM, :]
         + cw_ref[0:1, :] * u_sc[HALO - 2:HALO - 2 + TM, :])
    oc_ref[...] = (cb_ref[...] * y).astype(oc_ref.dtype)

    tok = i * TM - T0 + lax.broadcasted_iota(I32, (TM, 1), 0)
    cnt_small = jnp.maximum(tok + 1, 1).astype(F32)
    for g, w in enumerate(POOL_WINDOWS):
        cols = slice(g * POOL_GW, (g + 1) * POOL_GW)
        x = pz[:, cols]
        s = x
        for j in range(1, w):
            s = s + p_sc[HALO - j:HALO - j + TM, cols]
        inv_cnt = jnp.where(tok + 1 >= w, 1.0 / w, 1.0 / cnt_small)
        pooled = s * inv_cnt - x
        od = jnp.dot(pooled.astype(BF16), pw_ref[g], preferred_element_type=F32)
        od_ref[:, cols] = (od * ps_ref[:, cols]).astype(od_ref.dtype)


def _local_mixers(z, conv_w, pool_w_bf, pool_scale):
    cw = jnp.zeros((8, MIX_W), F32).at[:CONV_K].set(conv_w)
    blk = lambda c: pl.BlockSpec((TM, MIX_W), lambda i, c=c: (i, c // MIX_W))
    return pl.pallas_call(
        _local_kernel,
        grid=(R // TM,),
        in_specs=[
            blk(C_CB), blk(C_CC), blk(C_CV), blk(C_PZ),
            pl.BlockSpec((8, MIX_W), lambda i: (0, 0)),
            pl.BlockSpec((len(POOL_WINDOWS), POOL_GW, POOL_GW), lambda i: (0, 0, 0)),
            pl.BlockSpec((1, MIX_W), lambda i: (0, 0)),
        ],
        out_specs=[
            pl.BlockSpec((TM, MIX_W), lambda i: (i, 0)),
            pl.BlockSpec((TM, MIX_W), lambda i: (i, 0)),
        ],
        out_shape=[
            jax.ShapeDtypeStruct((R, MIX_W), BF16),
            jax.ShapeDtypeStruct((R, MIX_W), BF16),
        ],
        scratch_shapes=[
            pltpu.VMEM((TM + HALO, MIX_W), F32),
            pltpu.VMEM((TM + HALO, MIX_W), F32),
        ],
        compiler_params=_cparams(("arbitrary",)),
        name="conv_pool",
    )(z, z, z, z, cw, pool_w_bf, pool_scale.reshape(1, -1))


TN_MERGE = 256


def _merge_kernel(hb_ref, oa_ref, ob_ref, oc_ref, od_ref, wg0_ref, wg1_ref, wg2_ref, wg3_ref,
                  gb_ref, wb_ref, wo_ref, out_ref):
    j = pl.program_id(1)

    @pl.when(j == 0)
    def _():
        out_ref[...] = jnp.zeros_like(out_ref)

    hb = hb_ref[...]
    mixed = None
    for b, (o_ref, wg_ref) in enumerate(((oa_ref, wg0_ref), (ob_ref, wg1_ref),
                                         (oc_ref, wg2_ref), (od_ref, wg3_ref))):
        gate = _sigmoid(lax.dot_general(hb, wg_ref[...], _NT, preferred_element_type=F32)
                        + gb_ref[b:b + 1, :])
        proj = jnp.dot(o_ref[...], wb_ref[b], preferred_element_type=F32)
        term = gate * proj
        mixed = term if mixed is None else mixed + term
    out_ref[...] += jnp.dot(mixed.astype(BF16), wo_ref[...], preferred_element_type=F32)


def _merge(hb, o_a, o_b, o_c, o_d, w_all, gate_b, wb_bf, wo_bf, l):
    tn = TN_MERGE
    nj = D_MODEL // tn
    row = lambda w: pl.BlockSpec((TM, w), lambda i, j: (i, 0))
    wg = lambda b: pl.BlockSpec((None, tn, D_MODEL),
                                lambda i, j, b=b: (l, W_GATES // tn + b * nj + j, 0))
    return pl.pallas_call(
        _merge_kernel,
        grid=(R // TM, nj),
        in_specs=[
            row(D_MODEL), row(MIX_W), row(MIX_W), row(MIX_W), row(MIX_W),
            wg(0), wg(1), wg(2), wg(3),
            pl.BlockSpec((N_BRANCH, tn), lambda i, j: (0, j)),
            pl.BlockSpec((None, N_BRANCH, MIX_W, tn), lambda i, j: (l, 0, 0, j)),
            pl.BlockSpec((None, tn, D_MODEL), lambda i, j: (l, j, 0)),
        ],
        out_specs=pl.BlockSpec((TM, D_MODEL), lambda i, j: (i, 0)),
        out_shape=jax.ShapeDtypeStruct((R, D_MODEL), F32),
        compiler_params=_cparams(("parallel", "arbitrary")),
        name="merge",
    )(hb, o_a, o_b, o_c, o_d, w_all, w_all, w_all, w_all, gate_b, wb_bf, wo_bf)


def _post_ln(h, delta, g, b, row0):
    y = _layer_norm_rows(DEEPNORM_ALPHA * h + delta, g, b)
    rows = row0 + lax.broadcasted_iota(I32, (y.shape[0], 1), 0)
    return jnp.where(rows >= T0, y, 0.0)


def _first_of(cands, target):
    idx = jnp.full(target.shape, len(cands) - 1, I32)
    for j in range(len(cands) - 2, -1, -1):
        idx = jnp.where(cands[j] == target, j, idx)
    return idx


def _pick(cands, idx):
    out = cands[-1]
    for j in range(len(cands) - 2, -1, -1):
        out = jnp.where(idx == j, cands[j], out)
    return out


def _ln1_route_kernel(h_ref, mix_ref, g_ref, b_ref, rwt_ref, rb_ref,
                      h1_ref, mi_ref, mf_ref, cnt_ref, carry_sc):
    i = pl.program_id(0)

    @pl.when(i == 0)
    def _():
        carry_sc[...] = jnp.zeros_like(carry_sc)

    y = _post_ln(h_ref[...], mix_ref[...], g_ref[...], b_ref[...], i * TM)
    h1_ref[...] = y

    logits = lax.dot_general(rwt_ref[...], y, (((1,), (1,)), ((), ())), precision=HIGHEST,
                             preferred_element_type=F32)
    aff = _sigmoid(logits)
    sel = aff + rb_ref[...]
    xs = [sel[j * N_GROUPS:(j + 1) * N_GROUPS, :] for j in range(EXPERTS_PER_GROUP)]
    afs = [aff[j * N_GROUPS:(j + 1) * N_GROUPS, :] for j in range(EXPERTS_PER_GROUP)]

    score = None
    for a in range(EXPERTS_PER_GROUP):
        for bb in range(a + 1, EXPERTS_PER_GROUP):
            pair = xs[a] + xs[bb]
            score = pair if score is None else jnp.maximum(score, pair)
    giota = lax.broadcasted_iota(I32, (N_GROUPS, TM), 0)
    gmax = jnp.max(score, axis=0, keepdims=True)
    grp = jnp.min(jnp.where(score == gmax, giota, N_GROUPS), axis=0, keepdims=True)
    gsel = giota == grp
    cs = [jnp.max(jnp.where(gsel, x, -jnp.inf), axis=0, keepdims=True) for x in xs]
    acs = [jnp.sum(jnp.where(gsel, a, 0.0), axis=0, keepdims=True) for a in afs]

    m1 = jnp.maximum(jnp.maximum(cs[0], cs[1]), jnp.maximum(cs[2], cs[3]))
    i0 = _first_of(cs, m1)
    ds = [jnp.where(i0 == j, -jnp.inf, cs[j]) for j in range(EXPERTS_PER_GROUP)]
    m2 = jnp.maximum(jnp.maximum(ds[0], ds[1]), jnp.maximum(ds[2], ds[3]))
    i1 = _first_of(ds, m2)
    a0 = _pick(acs, i0)
    a1 = _pick(acs, i1)
    denom = a0 + a1

    pos = i * TM + lax.broadcasted_iota(I32, (1, TM), 1)
    valid = pos >= T0
    riota = lax.broadcasted_iota(I32, (N_EXPERTS, TM), 0)
    oh0 = (riota == i0 * N_GROUPS + grp) & valid
    oh1 = (riota == i1 * N_GROUPS + grp) & valid
    ohf = jnp.where(oh0 | oh1, 1.0, 0.0)
    before = (lax.broadcasted_iota(I32, (TM, TM), 0)
              < lax.broadcasted_iota(I32, (TM, TM), 1)).astype(BF16)
    cum = jnp.dot(ohf.astype(BF16), before, preferred_element_type=F32) + carry_sc[...]
    rank0 = jnp.sum(jnp.where(oh0, cum, 0.0), axis=0, keepdims=True)
    rank1 = jnp.sum(jnp.where(oh1, cum, 0.0), axis=0, keepdims=True)
    carry = carry_sc[...] + jnp.sum(ohf, axis=1, keepdims=True)
    carry_sc[...] = carry
    cnt_ref[...] = jnp.broadcast_to(carry, cnt_ref.shape)

    zi = jnp.zeros((1, TM), I32)
    mi_ref[...] = jnp.concatenate(
        [grp * EXPERTS_PER_GROUP + i0, grp * EXPERTS_PER_GROUP + i1,
         rank0.astype(I32), rank1.astype(I32), zi, zi, zi, zi], axis=0)
    zf = jnp.zeros((1, TM), F32)
    mf_ref[...] = jnp.concatenate([a0 / denom, a1 / denom, zf, zf, zf, zf, zf, zf], axis=0)


def _ln1_route(h, mix, g, b, router_wt, router_bc):
    row = pl.BlockSpec((TM, D_MODEL), lambda i: (i, 0))
    vec = pl.BlockSpec((1, D_MODEL), lambda i: (0, 0))
    meta = pl.BlockSpec((8, TM), lambda i: (0, i))
    return pl.pallas_call(
        _ln1_route_kernel,
        grid=(R // TM,),
        in_specs=[row, row, vec, vec,
                  pl.BlockSpec((N_EXPERTS, D_MODEL), lambda i: (0, 0)),
                  pl.BlockSpec((N_EXPERTS, 1), lambda i: (0, 0))],
        out_specs=[row, meta, meta, pl.BlockSpec((N_EXPERTS, LANE), lambda i: (0, 0))],
        out_shape=[
            jax.ShapeDtypeStruct((R, D_MODEL), F32),
            jax.ShapeDtypeStruct((8, R), I32),
            jax.ShapeDtypeStruct((8, R), F32),
            jax.ShapeDtypeStruct((N_EXPERTS, LANE), F32),
        ],
        scratch_shapes=[pltpu.VMEM((N_EXPERTS, 1), F32)],
        compiler_params=_cparams(("arbitrary",)),
        name="ln1_route",
    )(h, mix, g.reshape(1, -1), b.reshape(1, -1), router_wt, router_bc)


def _dispatch_tables(mi, counts_slot_major):
    counts = counts_slot_major.reshape(EXPERTS_PER_GROUP, N_GROUPS).T.reshape(N_EXPERTS).astype(I32)
    padded = (counts + EXPERT_BLOCK - 1) // EXPERT_BLOCK * EXPERT_BLOCK
    pad_end = jnp.cumsum(padded)
    pad_start = pad_end - padded
    e_iota = jnp.arange(N_EXPERTS, dtype=I32)
    rows_ok = jnp.arange(R) >= T0

    def dest(eid, rank):
        start = jnp.sum(jnp.where(eid[:, None] == e_iota[None, :], pad_start[None, :], 0), axis=1)
        return jnp.where(rows_ok, start + rank, 0).astype(I32)

    d0 = dest(mi[0], mi[2])
    d1 = dest(mi[1], mi[3])
    blk_start = (pad_start // EXPERT_BLOCK).astype(I32)
    n_blk = (padded // EXPERT_BLOCK).astype(I32)
    n_used = (pad_end[-1] // EXPERT_BLOCK).astype(I32).reshape(1)
    tok_rows = jnp.arange(T0, R, dtype=I32)
    row_src = jnp.zeros((N_ROWS + EXPERT_BLOCK,), I32).at[jnp.concatenate([d0[T0:], d1[T0:]])].set(
        jnp.concatenate([tok_rows, tok_rows]), unique_indices=True)
    blk = jnp.arange(N_BLOCKS + 1, dtype=I32)
    owner = (blk[:, None] >= blk_start[None, :]) & (blk[:, None] < (blk_start + n_blk)[None, :])
    left = counts[None, :] - (blk[:, None] - blk_start[None, :]) * EXPERT_BLOCK
    n_valid = jnp.sum(jnp.where(owner, jnp.clip(left, 0, EXPERT_BLOCK), 0), axis=1).astype(I32)
    return d0, d1, blk_start, n_blk, n_used, row_src, n_valid


def _row_copy(src, src_row, dst, dst_row, sem):
    return pltpu.make_async_copy(src.at[pl.ds(src_row, 1), :], dst.at[pl.ds(dst_row, 1), :], sem)


def _issue_row_gather(rs_ref, nv_ref, g, h_hbm, buf, sem):
    base = g * EXPERT_BLOCK
    nv = nv_ref[g]
    for r in range(EXPERT_BLOCK):
        @pl.when(r < nv)
        def _():
            _row_copy(h_hbm, rs_ref[base + r], buf, r, sem).start(priority=1)


def _wait_row_gather(nv_ref, g, h_hbm, buf, sem):
    nv = nv_ref[g]
    k = EXPERT_BLOCK
    while k >= 1:
        @pl.when((nv & k) != 0)
        def _():
            pltpu.make_async_copy(h_hbm.at[pl.ds(0, k), :], buf.at[pl.ds(0, k), :], sem).wait()
        k //= 2


def _block_rows(g):
    return pl.ds(pl.multiple_of(g * EXPERT_BLOCK, EXPERT_BLOCK), EXPERT_BLOCK)


def _finish_writes(out_copy, obuf, nu):
    @pl.when(nu >= 2)
    def _():
        out_copy(nu - 2, nu % 2).wait()

    @pl.when(nu >= 1)
    def _():
        out_copy(nu - 1, (nu - 1) % 2).wait()

    obuf[0] = jnp.zeros(obuf.shape[1:], obuf.dtype)

    def zero_block(g, carry):
        cp = out_copy(g, 0)
        cp.start()
        cp.wait()
        return carry

    lax.fori_loop(nu, N_BLOCKS, zero_block, 0)


def _moe_up_kernel(bs_ref, nb_ref, nu_ref, rs_ref, nv_ref, h_hbm, wg_ref, wu_ref, o_hbm,
                   wg_sc, wu_sc, xbuf, obuf, xsem, osem):
    e = pl.program_id(0)
    nb = nb_ref[e]
    g0 = bs_ref[e]

    def out_copy(g, slot):
        return pltpu.make_async_copy(obuf.at[slot], o_hbm.at[_block_rows(g), :], osem.at[slot])

    @pl.when(e == 0)
    def _():
        xbuf[...] = jnp.zeros_like(xbuf)
        _issue_row_gather(rs_ref, nv_ref, 0, h_hbm, xbuf.at[0], xsem.at[0])

    @pl.when(nb > 0)
    def _():
        wg_sc[...] = wg_ref[...].astype(BF16)
        wu_sc[...] = wu_ref[...].astype(BF16)

        def block(j, carry):
            g = g0 + j
            slot = g % 2

            @pl.when(g >= 2)
            def _():
                out_copy(g - 2, slot).wait()

            _wait_row_gather(nv_ref, g, h_hbm, xbuf.at[slot], xsem.at[slot])
            _issue_row_gather(rs_ref, nv_ref, g + 1, h_hbm, xbuf.at[1 - slot], xsem.at[1 - slot])
            x = xbuf[slot].astype(BF16)
            gate = jnp.dot(x, wg_sc[...], preferred_element_type=F32)
            up = jnp.dot(x, wu_sc[...], preferred_element_type=F32)
            obuf[slot] = (gate * _sigmoid(gate) * up).astype(BF16)
            out_copy(g, slot).start()
            return carry

        lax.fori_loop(0, nb, block, 0)

    @pl.when(e == N_EXPERTS - 1)
    def _():
        _finish_writes(out_copy, obuf, nu_ref[0])


def _moe_up(blk_start, n_blk, n_used, row_src, n_valid, h1, w_gate, w_up, l):
    wspec = pl.BlockSpec((None, None, D_MODEL, D_EXPERT), lambda e, *_: (l, e, 0, 0))
    grid_spec = pltpu.PrefetchScalarGridSpec(
        num_scalar_prefetch=5,
        grid=(N_EXPERTS,),
        in_specs=[pl.BlockSpec(memory_space=pl.ANY), wspec, wspec],
        out_specs=pl.BlockSpec(memory_space=pl.ANY),
        scratch_shapes=[
            pltpu.VMEM((D_MODEL, D_EXPERT), BF16),
            pltpu.VMEM((D_MODEL, D_EXPERT), BF16),
            pltpu.VMEM((2, EXPERT_BLOCK, D_MODEL), F32),
            pltpu.VMEM((2, EXPERT_BLOCK, D_EXPERT), BF16),
            pltpu.SemaphoreType.DMA((2,)),
            pltpu.SemaphoreType.DMA((2,)),
        ],
    )
    return pl.pallas_call(
        _moe_up_kernel,
        grid_spec=grid_spec,
        out_shape=jax.ShapeDtypeStruct((N_ROWS, D_EXPERT), BF16),
        compiler_params=_cparams(("arbitrary",), vmem=MOE_VMEM_LIMIT),
        name="moe_up",
    )(blk_start, n_blk, n_used, row_src, n_valid, h1, w_gate, w_up)


def _moe_down_kernel(bs_ref, nb_ref, nu_ref, x_hbm, wd_ref, y_hbm, wd_sc, xbuf, obuf, xsem, osem):
    e = pl.program_id(0)
    nb = nb_ref[e]
    g0 = bs_ref[e]
    nu = nu_ref[0]

    def in_copy(g, slot):
        return pltpu.make_async_copy(x_hbm.at[_block_rows(g), :], xbuf.at[slot], xsem.at[slot])

    def out_copy(g, slot):
        return pltpu.make_async_copy(obuf.at[slot], y_hbm.at[_block_rows(g), :], osem.at[slot])

    @pl.when((e == 0) & (nu > 0))
    def _():
        in_copy(0, 0).start()

    @pl.when(nb > 0)
    def _():
        wd_sc[...] = wd_ref[...].astype(BF16)

        def block(j, carry):
            g = g0 + j
            slot = g % 2

            @pl.when(g >= 2)
            def _():
                out_copy(g - 2, slot).wait()

            in_copy(g, slot).wait()

            @pl.when(g + 1 < nu)
            def _():
                in_copy(g + 1, 1 - slot).start()

            obuf[slot] = jnp.dot(xbuf[slot], wd_sc[...], preferred_element_type=F32)
            out_copy(g, slot).start()
            return carry

        lax.fori_loop(0, nb, block, 0)

    @pl.when(e == N_EXPERTS - 1)
    def _():
        _finish_writes(out_copy, obuf, nu)


def _moe_down(blk_start, n_blk, n_used, hmid, w_down, l):
    grid_spec = pltpu.PrefetchScalarGridSpec(
        num_scalar_prefetch=3,
        grid=(N_EXPERTS,),
        in_specs=[
            pl.BlockSpec(memory_space=pl.ANY),
            pl.BlockSpec((None, None, D_EXPERT, D_MODEL), lambda e, *_: (l, e, 0, 0)),
        ],
        out_specs=pl.BlockSpec(memory_space=pl.ANY),
        scratch_shapes=[
            pltpu.VMEM((D_EXPERT, D_MODEL), BF16),
            pltpu.VMEM((2, EXPERT_BLOCK, D_EXPERT), BF16),
            pltpu.VMEM((2, EXPERT_BLOCK, D_MODEL), F32),
            pltpu.SemaphoreType.DMA((2,)),
            pltpu.SemaphoreType.DMA((2,)),
        ],
    )
    return pl.pallas_call(
        _moe_down_kernel,
        grid_spec=grid_spec,
        out_shape=jax.ShapeDtypeStruct((N_ROWS, D_MODEL), F32),
        compiler_params=_cparams(("arbitrary",), vmem=MOE_VMEM_LIMIT),
        name="moe_down",
    )(blk_start, n_blk, n_used, hmid, w_down)


def _combine_kernel(d0_ref, d1_ref, y_hbm, h1_ref, mf_ref, g_ref, b_ref, h2_ref, h2b_ref,
                    buf0, buf1, sem):
    i = pl.program_id(0)
    slot = i % 2

    def issue(tile, s):
        base = tile * LANE

        def body(r, carry):
            _row_copy(y_hbm, d0_ref[base + r], buf0.at[s], r, sem.at[s]).start(priority=0)
            _row_copy(y_hbm, d1_ref[base + r], buf1.at[s], r, sem.at[s]).start(priority=1)
            return carry

        lax.fori_loop(0, LANE, body, 0)

    @pl.when(i == 0)
    def _():
        issue(0, 0)

    @pl.when(i + 1 < pl.num_programs(0))
    def _():
        issue(i + 1, 1 - slot)

    def drain(r, carry):
        _row_copy(y_hbm, 0, buf0.at[slot], r, sem.at[slot]).wait()
        _row_copy(y_hbm, 0, buf1.at[slot], r, sem.at[slot]).wait()
        return carry

    lax.fori_loop(0, LANE, drain, 0)
    wt = mf_ref[...].T
    ffn = wt[:, 0:1] * buf0[slot] + wt[:, 1:2] * buf1[slot]
    y = _post_ln(h1_ref[...], ffn, g_ref[...], b_ref[...], i * LANE)
    h2_ref[...] = y
    h2b_ref[...] = y.astype(BF16)


def _combine_ln2(d0, d1, y_rows, h1, mf, g, b):
    row = lambda i, d0, d1: (i, 0)
    vec = pl.BlockSpec((1, D_MODEL), lambda i, d0, d1: (0, 0))
    grid_spec = pltpu.PrefetchScalarGridSpec(
        num_scalar_prefetch=2,
        grid=(R // LANE,),
        in_specs=[
            pl.BlockSpec(memory_space=pl.ANY),
            pl.BlockSpec((LANE, D_MODEL), row),
            pl.BlockSpec((8, LANE), lambda i, d0, d1: (0, i)),
            vec, vec,
        ],
        out_specs=[pl.BlockSpec((LANE, D_MODEL), row), pl.BlockSpec((LANE, D_MODEL), row)],
        scratch_shapes=[
            pltpu.VMEM((2, LANE, D_MODEL), F32),
            pltpu.VMEM((2, LANE, D_MODEL), F32),
            pltpu.SemaphoreType.DMA((2,)),
        ],
    )
    return pl.pallas_call(
        _combine_kernel,
        grid_spec=grid_spec,
        out_shape=[
            jax.ShapeDtypeStruct((R, D_MODEL), F32),
            jax.ShapeDtypeStruct((R, D_MODEL), BF16),
        ],
        compiler_params=_cparams(("arbitrary",)),
        name="moe_combine_ln2",
    )(d0, d1, y_rows, h1, mf, g.reshape(1, -1), b.reshape(1, -1))


def kernel(x, meta_tokens, ln_in_g, ln_in_b, w_in, fox_f_bias, gla_wa2, gla_ba, gla_norm_g, conv_w, pool_w, pool_scale, gate_b, w_branch, w_out, ln1_g, ln1_b, router_w, router_b, w_gate, w_up, w_down, ln2_g, ln2_b):
    assert x.shape == (1, SEQ, D_MODEL)
    h, hb = _ln_in(x.reshape(SEQ, D_MODEL), meta_tokens, ln_in_g, ln_in_b)
    router_wt = router_w.T.reshape(N_GROUPS, EXPERTS_PER_GROUP, D_MODEL).transpose(1, 0, 2).reshape(
        N_EXPERTS, D_MODEL)
    router_bc = router_b.astype(F32).reshape(N_GROUPS, EXPERTS_PER_GROUP).T.reshape(N_EXPERTS, 1)

    wb_bf = w_branch.astype(BF16)
    wo_bf = w_out.astype(BF16)

    w_all = _transposed_weights(w_in)

    for l in range(DEPTH):
        zf = _matmul_nt(hb, w_all, l, W_FOX, 2 * MIX_W, BF16, TM_PROJ, 512, "proj_fox")
        vt = _matmul_tt(w_all, l, W_FOX + 2 * MIX_W, MIX_W, hb, BF16, TM_PROJ, "proj_fox_vt")
        z = _matmul_nt(hb, w_all, l, W_MIX, N_MIXC, F32, TM_PROJ, 512, "proj_mix")
        zs = _matmul_nt(hb, w_all, l, W_SMALL, 2 * LANE, F32, TM_PROJ, 2 * LANE, "proj_small")

        bias_row = jnp.zeros((1, LANE), F32).at[0, SM_FF:SM_FF + FOX_HEADS].set(fox_f_bias[l])
        c = _fox_gate(zs, bias_row)
        o_a = _fox_attention(zf, vt, c)

        wa2p = jnp.zeros((LANE, GLA_HEADS * GLA_DK), F32).at[SM_GA:SM_GA + GLA_RANK].set(gla_wa2[l])
        o_b = _gla(z, zs, wa2p, gla_ba[l].reshape(1, -1), gla_norm_g[l].reshape(1, -1))

        o_c, o_d = _local_mixers(z, conv_w[l], pool_w[l].astype(BF16), pool_scale[l])

        mix = _merge(hb, o_a, o_b, o_c, o_d, w_all, gate_b[l], wb_bf, wo_bf, l)
        h1, mi, mf, counts = _ln1_route(h, mix, ln1_g[l], ln1_b[l], router_wt, router_bc)

        d0, d1, blk_start, n_blk, n_used, row_src, n_valid = _dispatch_tables(mi, counts[:, 0])
        hmid = _moe_up(blk_start, n_blk, n_used, row_src, n_valid, h1, w_gate, w_up, l)
        y_rows = _moe_down(blk_start, n_blk, n_used, hmid, w_down, l)
        h, hb = _combine_ln2(d0, d1, y_rows, h1, mf, ln2_g[l], ln2_b[l])

    return h[PAD_ROWS + N_META:].reshape(1, SEQ, D_MODEL)
```

```python
import jax
import jax.numpy as jnp
import numpy as np
from jax import lax
from jax.experimental import pallas as pl
from jax.experimental.pallas import tpu as pltpu

F32 = jnp.float32
BF16 = jnp.bfloat16
I32 = jnp.int32
HIGHEST = lax.Precision.HIGHEST

D_MODEL = 2048
SEQ = 8192
DEPTH = 2
N_META = 16
N_BRANCH = 4
MIX_W = 512
FOX_HEADS = 4
FOX_HD = 128
GLA_HEADS = 4
GLA_DK = 64
GLA_DV = 128
GLA_RANK = 16
GLA_TAU = 16.0
CONV_K = 3
POOL_WINDOWS = (2, 4, 8, 16)
POOL_GW = 128
N_EXPERTS = 32
N_GROUPS = 8
EXPERTS_PER_GROUP = 4
TOP_K = 2
D_EXPERT = 1024
LN_EPS = 1e-5
DEEPNORM_ALPHA = (2 * DEPTH) ** 0.25

_SPLITS = (512, 512, 512, 4, 256, 256, 512, 16, 512, 512, 512, 512, 512, 8192)
_OFFS = [int(o) for o in np.concatenate([[0], np.cumsum(_SPLITS)])]
(O_FQ, O_FK, O_FV, O_FF, O_GQ, O_GK, O_GV, O_GA, O_GR, O_CB, O_CC, O_CV, O_PZ, O_GZ, P_IN) = _OFFS

LANE = 128
PAD_ROWS = LANE - N_META
T0 = PAD_ROWS
N_TOK = N_META + SEQ
R = PAD_ROWS + N_TOK
TM = 640
TM_PROJ = 1664
HALO = 16

WT_TILE = 128
W_FOX, W_MIX, W_GATES = 0, 1536, 5120
N_FOX, N_MIXC, N_GATES = 1536, 3584, 8192
W_SMALL = W_GATES + N_GATES
N_WT = W_SMALL + 2 * LANE
C_GQ, C_GK, C_GV, C_GR, C_CB, C_CC, C_CV, C_PZ = 0, 256, 512, 1024, 1536, 2048, 2560, 3072
SM_FF_TILE, SM_GA_TILE = O_FF // LANE, O_GA // LANE
SM_FF = O_FF - SM_FF_TILE * LANE
SM_GA = O_GA - SM_GA_TILE * LANE


def _wt_sources():
    src = []
    for r in range(0, W_SMALL, WT_TILE):
        if r < N_FOX:
            src.append(O_FQ + r)
        elif r < W_MIX + 1024:
            src.append(O_GQ + r - W_MIX)
        else:
            src.append(O_GR + r - (W_MIX + 1024))
    src += [SM_FF_TILE * LANE, SM_GA_TILE * LANE]
    assert all(0 <= s and s + WT_TILE <= P_IN for s in src)
    return src


WT_SRC = _wt_sources()

EXPERT_BLOCK = 128
N_FLAT = N_TOK * TOP_K
N_BLOCKS = -(-N_FLAT // EXPERT_BLOCK) + N_EXPERTS
N_ROWS = N_BLOCKS * EXPERT_BLOCK

NEG = -1e30
VMEM_LIMIT = 48 * 1024 * 1024
MOE_VMEM_LIMIT = 56 * 1024 * 1024


def _cparams(sem, vmem=VMEM_LIMIT):
    return pltpu.CompilerParams(dimension_semantics=sem, vmem_limit_bytes=vmem)


def _log_sigmoid(x):
    return jnp.minimum(x, 0.0) - jnp.log1p(jnp.exp(-jnp.abs(x)))


def _sigmoid(x):
    return 1.0 / (1.0 + jnp.exp(-x))


def _layer_norm_rows(x, g, b):
    mu = jnp.mean(x, axis=-1, keepdims=True)
    xc = x - mu
    var = jnp.mean(xc * xc, axis=-1, keepdims=True)
    return xc * lax.rsqrt(var + LN_EPS) * g + b


def _ln_in_kernel(x_ref, meta_ref, g_ref, b_ref, h_ref, hb_ref):
    i = pl.program_id(0)

    @pl.when(i == 0)
    def _():
        h_ref[...] = jnp.zeros_like(h_ref)
        hb_ref[...] = jnp.zeros_like(hb_ref)
        m = _layer_norm_rows(meta_ref[...], g_ref[...], b_ref[...])
        h_ref[PAD_ROWS:, :] = m
        hb_ref[PAD_ROWS:, :] = m.astype(BF16)

    @pl.when(i > 0)
    def _():
        y = _layer_norm_rows(x_ref[...], g_ref[...], b_ref[...])
        h_ref[...] = y
        hb_ref[...] = y.astype(BF16)


def _ln_in(x2d, meta, g, b):
    nb = R // LANE
    return pl.pallas_call(
        _ln_in_kernel,
        grid=(nb,),
        in_specs=[
            pl.BlockSpec((LANE, D_MODEL), lambda i: (jnp.maximum(i - 1, 0), 0)),
            pl.BlockSpec((N_META, D_MODEL), lambda i: (0, 0)),
            pl.BlockSpec((1, D_MODEL), lambda i: (0, 0)),
            pl.BlockSpec((1, D_MODEL), lambda i: (0, 0)),
        ],
        out_specs=[
            pl.BlockSpec((LANE, D_MODEL), lambda i: (i, 0)),
            pl.BlockSpec((LANE, D_MODEL), lambda i: (i, 0)),
        ],
        out_shape=[
            jax.ShapeDtypeStruct((R, D_MODEL), F32),
            jax.ShapeDtypeStruct((R, D_MODEL), BF16),
        ],
        compiler_params=_cparams(("arbitrary",)),
        name="ln_in",
    )(x2d, meta, g.reshape(1, -1), b.reshape(1, -1))


D_CHUNKS = D_MODEL // LANE
FLAT_ROWS = D_CHUNKS * DEPTH


def _wt_kernel(src_ref, w_hbm, o_ref, buf, sem):
    j = pl.program_id(0)
    slot = j % 2

    def tile_copy(t, s):
        row0 = pl.multiple_of(src_ref[t] * FLAT_ROWS, FLAT_ROWS)
        return pltpu.make_async_copy(w_hbm.at[pl.ds(row0, WT_TILE * FLAT_ROWS), :], buf.at[s],
                                     sem.at[s])

    @pl.when(j == 0)
    def _():
        tile_copy(0, 0).start()

    @pl.when(j + 1 < pl.num_programs(0))
    def _():
        tile_copy(j + 1, 1 - slot).start()

    tile_copy(j, slot).wait()
    for l in range(DEPTH):
        for c in range(D_CHUNKS):
            o_ref[l, :, c * LANE:(c + 1) * LANE] = buf[
                slot, pl.ds(c * DEPTH + l, WT_TILE, stride=FLAT_ROWS), :].astype(BF16)


def _transposed_weights(w_in):
    flat = w_in.reshape(DEPTH, D_CHUNKS, LANE, P_IN).transpose(3, 1, 0, 2).reshape(
        P_IN * FLAT_ROWS, LANE)
    grid_spec = pltpu.PrefetchScalarGridSpec(
        num_scalar_prefetch=1,
        grid=(N_WT // WT_TILE,),
        in_specs=[pl.BlockSpec(memory_space=pl.ANY)],
        out_specs=pl.BlockSpec((DEPTH, WT_TILE, D_MODEL), lambda j, src: (0, j, 0)),
        scratch_shapes=[
            pltpu.VMEM((2, WT_TILE * FLAT_ROWS, LANE), F32),
            pltpu.SemaphoreType.DMA((2,)),
        ],
    )
    return pl.pallas_call(
        _wt_kernel,
        grid_spec=grid_spec,
        out_shape=jax.ShapeDtypeStruct((DEPTH, N_WT, D_MODEL), BF16),
        compiler_params=_cparams(("arbitrary",)),
        name="transposed_weights",
    )(jnp.asarray(WT_SRC, I32), flat)


_NT = (((1,), (1,)), ((), ()))


def _mm_nt_kernel(a_ref, wt_ref, o_ref):
    o_ref[...] = lax.dot_general(a_ref[...], wt_ref[...], _NT,
                                 preferred_element_type=F32).astype(o_ref.dtype)


def _matmul_nt(a, wt, l, row0, n, out_dtype, tm, tn, name):
    m, k = a.shape
    return pl.pallas_call(
        _mm_nt_kernel,
        grid=(m // tm, n // tn),
        in_specs=[
            pl.BlockSpec((tm, k), lambda i, j: (i, 0)),
            pl.BlockSpec((None, tn, k), lambda i, j: (l, row0 // tn + j, 0)),
        ],
        out_specs=pl.BlockSpec((tm, tn), lambda i, j: (i, j)),
        out_shape=jax.ShapeDtypeStruct((m, n), out_dtype),
        compiler_params=_cparams(("parallel", "arbitrary")),
        name=name,
    )(a, wt)


def _matmul_tt(wt, l, row0, n, a, out_dtype, tm, name):
    m, k = a.shape
    return pl.pallas_call(
        _mm_nt_kernel,
        grid=(m // tm,),
        in_specs=[
            pl.BlockSpec((None, n, k), lambda i: (l, row0 // n, 0)),
            pl.BlockSpec((tm, k), lambda i: (i, 0)),
        ],
        out_specs=pl.BlockSpec((n, tm), lambda i: (0, i)),
        out_shape=jax.ShapeDtypeStruct((n, m), out_dtype),
        compiler_params=_cparams(("parallel",)),
        name=name,
    )(wt, a)


def _fox_gate_kernel(zs_ref, bias_ref, c_ref, carry_ref):
    i = pl.program_id(0)

    @pl.when(i == 0)
    def _():
        carry_ref[...] = jnp.zeros_like(carry_ref)

    rows = i * TM + lax.broadcasted_iota(I32, (TM, LANE), 0)
    lf = _log_sigmoid(zs_ref[...] + bias_ref[...])
    lf = jnp.where(rows >= T0, lf, 0.0)
    tri = (lax.broadcasted_iota(I32, (TM, TM), 0)
           >= lax.broadcasted_iota(I32, (TM, TM), 1)).astype(F32)
    c = jnp.dot(tri, lf, precision=HIGHEST, preferred_element_type=F32) + carry_ref[...]
    c_ref[...] = c
    carry_ref[...] = c[TM - 1:TM, :]


def _fox_gate(zs, bias_row):
    return pl.pallas_call(
        _fox_gate_kernel,
        grid=(R // TM,),
        in_specs=[
            pl.BlockSpec((TM, LANE), lambda i: (i, 0)),
            pl.BlockSpec((1, LANE), lambda i: (0, 0)),
        ],
        out_specs=pl.BlockSpec((TM, LANE), lambda i: (i, 0)),
        out_shape=jax.ShapeDtypeStruct((R, LANE), F32),
        scratch_shapes=[pltpu.VMEM((1, LANE), F32)],
        compiler_params=_cparams(("arbitrary",)),
        name="fox_gate",
    )(zs, bias_row)


TQ = TM
N_QB = R // TQ
_PAIRS = [(qi, kj) for qi in range(N_QB) for kj in range(qi + 1)]
N_PAIRS = len(_PAIRS)


LOG2E = 1.4426950408889634
FOX_HPS = FOX_HEADS


def _fox_kernel(qi_tab, kj_tab, q_ref, k_ref, vt_ref, ck_ref, o_ref, m_sc, l_sc, acc_sc):
    p = pl.program_id(1)
    qi = qi_tab[p]
    kj = kj_tab[p]

    @pl.when(kj == 0)
    def _():
        m_sc[...] = jnp.full_like(m_sc, NEG)
        l_sc[...] = jnp.zeros_like(l_sc)
        acc_sc[...] = jnp.zeros_like(acc_sc)

    kpos = kj * TQ + lax.broadcasted_iota(I32, (TQ, 1), 0)
    c1 = FOX_HD ** -0.5 * LOG2E

    def step(causal):
        for hh in range(FOX_HPS):
            lanes = slice(hh * FOX_HD, (hh + 1) * FOX_HD)
            ck = ck_ref[:, SM_FF + hh:SM_FF + hh + 1]
            ckl = jnp.where(kpos >= T0, ck * LOG2E, -NEG)
            t = lax.dot_general(k_ref[:, lanes], q_ref[:, lanes], (((1,), (1,)), ((), ())),
                                preferred_element_type=F32) * c1 - ckl
            if causal:
                ahead = (lax.broadcasted_iota(I32, (TQ, TQ), 0)
                         - lax.broadcasted_iota(I32, (TQ, TQ), 1))
                t = jnp.where(ahead <= 0, t, NEG)
            m_prev = m_sc[hh]
            m_new = jnp.maximum(m_prev, jnp.max(t, axis=0, keepdims=True))
            alpha = jnp.exp2(m_prev - m_new)
            pr = jnp.exp2(t - m_new)
            l_sc[hh] = alpha * l_sc[hh] + jnp.sum(pr, axis=0, keepdims=True)
            acc_sc[hh] = alpha * acc_sc[hh] + jnp.dot(vt_ref[lanes, :], pr.astype(BF16),
                                                      preferred_element_type=F32)
            m_sc[hh] = m_new

    @pl.when(kj == qi)
    def _():
        step(True)

    @pl.when(kj != qi)
    def _():
        step(False)

    @pl.when(kj == qi)
    def _():
        for hh in range(FOX_HPS):
            o_ref[:, hh * FOX_HD:(hh + 1) * FOX_HD] = (acc_sc[hh] / l_sc[hh]).T.astype(o_ref.dtype)


def _fox_attention(zf, vt, c_col):
    qi_tab = jnp.asarray([p[0] for p in _PAIRS], I32)
    kj_tab = jnp.asarray([p[1] for p in _PAIRS], I32)
    hw = FOX_HPS * FOX_HD
    grid_spec = pltpu.PrefetchScalarGridSpec(
        num_scalar_prefetch=2,
        grid=(FOX_HEADS // FOX_HPS, N_PAIRS),
        in_specs=[
            pl.BlockSpec((TQ, hw), lambda h, p, qt, kt: (qt[p], h)),
            pl.BlockSpec((TQ, hw), lambda h, p, qt, kt: (kt[p], FOX_HEADS // FOX_HPS + h)),
            pl.BlockSpec((hw, TQ), lambda h, p, qt, kt: (h, kt[p])),
            pl.BlockSpec((TQ, LANE), lambda h, p, qt, kt: (kt[p], 0)),
        ],
        out_specs=pl.BlockSpec((TQ, hw), lambda h, p, qt, kt: (qt[p], h)),
        scratch_shapes=[
            pltpu.VMEM((FOX_HPS, 1, TQ), F32),
            pltpu.VMEM((FOX_HPS, 1, TQ), F32),
            pltpu.VMEM((FOX_HPS, FOX_HD, TQ), F32),
        ],
    )
    return pl.pallas_call(
        _fox_kernel,
        grid_spec=grid_spec,
        out_shape=jax.ShapeDtypeStruct((R, MIX_W), BF16),
        compiler_params=_cparams(("parallel", "arbitrary")),
        name="fox_attention",
    )(qi_tab, kj_tab, zf, zf, vt, c_col)


GLA_CHUNK = 64
GLA_UNROLL = 5


def _gla_kernel(q_ref, k_ref, v_ref, gr_ref, zs_ref, wa2_ref, ba_ref, gn_ref, o_ref, st_ref, la_ref):
    i = pl.program_id(0)

    @pl.when(i == 0)
    def _():
        st_ref[...] = jnp.zeros_like(st_ref)

    la = jnp.dot(zs_ref[...], wa2_ref[...], precision=HIGHEST, preferred_element_type=F32)
    la_ref[...] = _log_sigmoid(la + ba_ref[...]) * (1.0 / GLA_TAU)

    c_r = lax.broadcasted_iota(I32, (GLA_CHUNK, GLA_CHUNK), 0)
    c_c = lax.broadcasted_iota(I32, (GLA_CHUNK, GLA_CHUNK), 1)
    tri_b = c_r >= c_c
    tri = jnp.where(tri_b, 1.0, 0.0).astype(BF16)

    def chunk(c, carry):
        r0 = pl.multiple_of(c * GLA_CHUNK, GLA_CHUNK)
        rows = pl.ds(r0, GLA_CHUNK)
        g = la_ref[rows, :]
        g_hi = g.astype(BF16)
        g_lo = (g - g_hi.astype(F32)).astype(BF16)
        b = (jnp.dot(tri, g_hi, preferred_element_type=F32)
             + jnp.dot(tri, g_lo, preferred_element_type=F32))
        b_last = b[GLA_CHUNK - 1:GLA_CHUNK, :]
        e_last = jnp.exp(b_last)
        qt = q_ref[rows, :] * (GLA_DK ** -0.5) * jnp.exp(b)
        kt = k_ref[rows, :] * jnp.exp(-b)
        kh = kt * e_last
        for h in range(GLA_HEADS):
            ks = slice(h * GLA_DK, (h + 1) * GLA_DK)
            vs = slice(h * GLA_DV, (h + 1) * GLA_DV)
            q_h = qt[:, ks].astype(BF16)
            k_h = kt[:, ks].astype(BF16)
            kh_h = kh[:, ks].astype(BF16)
            v_h = v_ref[rows, vs]
            att = lax.dot_general(q_h, k_h, (((1,), (1,)), ((), ())), preferred_element_type=F32)
            att = jnp.where(tri_b, att, 0.0)
            st = st_ref[h]
            o = jnp.dot(att.astype(BF16), v_h.astype(BF16), preferred_element_type=F32)
            o = o + lax.dot_general(q_h, st.astype(BF16), (((1,), (1,)), ((), ())),
                                    preferred_element_type=F32)
            st_ref[h] = st * e_last[:, ks] + jnp.dot(v_h.T.astype(BF16), kh_h,
                                                     preferred_element_type=F32)
            ms = jnp.mean(o * o, axis=-1, keepdims=True)
            on = o * lax.rsqrt(ms + LN_EPS) * gn_ref[:, vs]
            gate = gr_ref[rows, vs]
            o_ref[rows, vs] = (on * (gate * _sigmoid(gate))).astype(o_ref.dtype)
        return carry

    lax.fori_loop(0, TM // GLA_CHUNK, chunk, 0, unroll=GLA_UNROLL)


def _gla(z, zs, wa2p, ba, gn):
    return pl.pallas_call(
        _gla_kernel,
        grid=(R // TM,),
        in_specs=[
            pl.BlockSpec((TM, 256), lambda i: (i, C_GQ // 256)),
            pl.BlockSpec((TM, 256), lambda i: (i, C_GK // 256)),
            pl.BlockSpec((TM, 512), lambda i: (i, C_GV // 512)),
            pl.BlockSpec((TM, 512), lambda i: (i, C_GR // 512)),
            pl.BlockSpec((TM, LANE), lambda i: (i, 1)),
            pl.BlockSpec((LANE, 256), lambda i: (0, 0)),
            pl.BlockSpec((1, 256), lambda i: (0, 0)),
            pl.BlockSpec((1, 512), lambda i: (0, 0)),
        ],
        out_specs=pl.BlockSpec((TM, MIX_W), lambda i: (i, 0)),
        out_shape=jax.ShapeDtypeStruct((R, MIX_W), BF16),
        scratch_shapes=[
            pltpu.VMEM((GLA_HEADS, GLA_DV, GLA_DK), F32),
            pltpu.VMEM((TM, GLA_HEADS * GLA_DK), F32),
        ],
        compiler_params=_cparams(("arbitrary",)),
        name="gla",
    )(z, z, z, z, zs, wa2p, ba, gn)


def _local_kernel(cb_ref, cc_ref, cv_ref, pz_ref, cw_ref, pw_ref, ps_ref, oc_ref, od_ref, u_sc, p_sc):
    i = pl.program_id(0)

    @pl.when(i == 0)
    def _():
        u_sc[0:HALO, :] = jnp.zeros((HALO, MIX_W), F32)
        p_sc[0:HALO, :] = jnp.zeros((HALO, MIX_W), F32)

    @pl.when(i > 0)
    def _():
        u_sc[0:HALO, :] = u_sc[TM:TM + HALO, :]
        p_sc[0:HALO, :] = p_sc[TM:TM + HALO, :]

    u = cc_ref[...] * cv_ref[...]
    pz = pz_ref[...]
    u_sc[HALO:, :] = u
    p_sc[HALO:, :] = pz

    y = (cw_ref[2:3, :] * u + cw_ref[1:2, :] * u_sc[HALO - 1:HALO - 1 + TM, :]
         + cw_ref[0:1, :] * u_sc[HALO - 2:HALO - 2 + TM, :])
    oc_ref[...] = (cb_ref[...] * y).astype(oc_ref.dtype)

    tok = i * TM - T0 + lax.broadcasted_iota(I32, (TM, 1), 0)
    cnt_small = jnp.maximum(tok + 1, 1).astype(F32)
    for g, w in enumerate(POOL_WINDOWS):
        cols = slice(g * POOL_GW, (g + 1) * POOL_GW)
        x = pz[:, cols]
        s = x
        for j in range(1, w):
            s = s + p_sc[HALO - j:HALO - j + TM, cols]
        inv_cnt = jnp.where(tok + 1 >= w, 1.0 / w, 1.0 / cnt_small)
        pooled = s * inv_cnt - x
        od = jnp.dot(pooled.astype(BF16), pw_ref[g], preferred_element_type=F32)
        od_ref[:, cols] = (od * ps_ref[:, cols]).astype(od_ref.dtype)


def _local_mixers(z, conv_w, pool_w_bf, pool_scale):
    cw = jnp.zeros((8, MIX_W), F32).at[:CONV_K].set(conv_w)
    blk = lambda c: pl.BlockSpec((TM, MIX_W), lambda i, c=c: (i, c // MIX_W))
    return pl.pallas_call(
        _local_kernel,
        grid=(R // TM,),
        in_specs=[
            blk(C_CB), blk(C_CC), blk(C_CV), blk(C_PZ),
            pl.BlockSpec((8, MIX_W), lambda i: (0, 0)),
            pl.BlockSpec((len(POOL_WINDOWS), POOL_GW, POOL_GW), lambda i: (0, 0, 0)),
            pl.BlockSpec((1, MIX_W), lambda i: (0, 0)),
        ],
        out_specs=[
            pl.BlockSpec((TM, MIX_W), lambda i: (i, 0)),
            pl.BlockSpec((TM, MIX_W), lambda i: (i, 0)),
        ],
        out_shape=[
            jax.ShapeDtypeStruct((R, MIX_W), BF16),
            jax.ShapeDtypeStruct((R, MIX_W), BF16),
        ],
        scratch_shapes=[
            pltpu.VMEM((TM + HALO, MIX_W), F32),
            pltpu.VMEM((TM + HALO, MIX_W), F32),
        ],
        compiler_params=_cparams(("arbitrary",)),
        name="conv_pool",
    )(z, z, z, z, cw, pool_w_bf, pool_scale.reshape(1, -1))


TN_MERGE = 256


def _merge_kernel(hb_ref, oa_ref, ob_ref, oc_ref, od_ref, wg0_ref, wg1_ref, wg2_ref, wg3_ref,
                  gb_ref, wb_ref, wo_ref, out_ref):
    j = pl.program_id(1)

    @pl.when(j == 0)
    def _():
        out_ref[...] = jnp.zeros_like(out_ref)

    hb = hb_ref[...]
    mixed = None
    for b, (o_ref, wg_ref) in enumerate(((oa_ref, wg0_ref), (ob_ref, wg1_ref),
                                         (oc_ref, wg2_ref), (od_ref, wg3_ref))):
        gate = _sigmoid(lax.dot_general(hb, wg_ref[...], _NT, preferred_element_type=F32)
                        + gb_ref[b:b + 1, :])
        proj = jnp.dot(o_ref[...], wb_ref[b], preferred_element_type=F32)
        term = gate * proj
        mixed = term if mixed is None else mixed + term
    out_ref[...] += jnp.dot(mixed.astype(BF16), wo_ref[...], preferred_element_type=F32)


def _merge(hb, o_a, o_b, o_c, o_d, w_all, gate_b, wb_bf, wo_bf, l):
    tn = TN_MERGE
    nj = D_MODEL // tn
    row = lambda w: pl.BlockSpec((TM, w), lambda i, j: (i, 0))
    wg = lambda b: pl.BlockSpec((None, tn, D_MODEL),
                                lambda i, j, b=b: (l, W_GATES // tn + b * nj + j, 0))
    return pl.pallas_call(
        _merge_kernel,
        grid=(R // TM, nj),
        in_specs=[
            row(D_MODEL), row(MIX_W), row(MIX_W), row(MIX_W), row(MIX_W),
            wg(0), wg(1), wg(2), wg(3),
            pl.BlockSpec((N_BRANCH, tn), lambda i, j: (0, j)),
            pl.BlockSpec((None, N_BRANCH, MIX_W, tn), lambda i, j: (l, 0, 0, j)),
            pl.BlockSpec((None, tn, D_MODEL), lambda i, j: (l, j, 0)),
        ],
        out_specs=pl.BlockSpec((TM, D_MODEL), lambda i, j: (i, 0)),
        out_shape=jax.ShapeDtypeStruct((R, D_MODEL), F32),
        compiler_params=_cparams(("parallel", "arbitrary")),
        name="merge",
    )(hb, o_a, o_b, o_c, o_d, w_all, w_all, w_all, w_all, gate_b, wb_bf, wo_bf)


def _post_ln(h, delta, g, b, row0):
    y = _layer_norm_rows(DEEPNORM_ALPHA * h + delta, g, b)
    rows = row0 + lax.broadcasted_iota(I32, (y.shape[0], 1), 0)
    return jnp.where(rows >= T0, y, 0.0)


def _first_of(cands, target):
    idx = jnp.full(target.shape, len(cands) - 1, I32)
    for j in range(len(cands) - 2, -1, -1):
        idx = jnp.where(cands[j] == target, j, idx)
    return idx


def _pick(cands, idx):
    out = cands[-1]
    for j in range(len(cands) - 2, -1, -1):
        out = jnp.where(idx == j, cands[j], out)
    return out


def _ln1_route_kernel(h_ref, mix_ref, g_ref, b_ref, rwt_ref, rb_ref,
                      h1_ref, mi_ref, mf_ref, cnt_ref, carry_sc):
    i = pl.program_id(0)

    @pl.when(i == 0)
    def _():
        carry_sc[...] = jnp.zeros_like(carry_sc)

    y = _post_ln(h_ref[...], mix_ref[...], g_ref[...], b_ref[...], i * TM)
    h1_ref[...] = y

    logits = lax.dot_general(rwt_ref[...], y, (((1,), (1,)), ((), ())), precision=HIGHEST,
                             preferred_element_type=F32)
    aff = _sigmoid(logits)
    sel = aff + rb_ref[...]
    xs = [sel[j * N_GROUPS:(j + 1) * N_GROUPS, :] for j in range(EXPERTS_PER_GROUP)]
    afs = [aff[j * N_GROUPS:(j + 1) * N_GROUPS, :] for j in range(EXPERTS_PER_GROUP)]

    score = None
    for a in range(EXPERTS_PER_GROUP):
        for bb in range(a + 1, EXPERTS_PER_GROUP):
            pair = xs[a] + xs[bb]
            score = pair if score is None else jnp.maximum(score, pair)
    giota = lax.broadcasted_iota(I32, (N_GROUPS, TM), 0)
    gmax = jnp.max(score, axis=0, keepdims=True)
    grp = jnp.min(jnp.where(score == gmax, giota, N_GROUPS), axis=0, keepdims=True)
    gsel = giota == grp
    cs = [jnp.max(jnp.where(gsel, x, -jnp.inf), axis=0, keepdims=True) for x in xs]
    acs = [jnp.sum(jnp.where(gsel, a, 0.0), axis=0, keepdims=True) for a in afs]

    m1 = jnp.maximum(jnp.maximum(cs[0], cs[1]), jnp.maximum(cs[2], cs[3]))
    i0 = _first_of(cs, m1)
    ds = [jnp.where(i0 == j, -jnp.inf, cs[j]) for j in range(EXPERTS_PER_GROUP)]
    m2 = jnp.maximum(jnp.maximum(ds[0], ds[1]), jnp.maximum(ds[2], ds[3]))
    i1 = _first_of(ds, m2)
    a0 = _pick(acs, i0)
    a1 = _pick(acs, i1)
    denom = a0 + a1

    pos = i * TM + lax.broadcasted_iota(I32, (1, TM), 1)
    valid = pos >= T0
    riota = lax.broadcasted_iota(I32, (N_EXPERTS, TM), 0)
    oh0 = (riota == i0 * N_GROUPS + grp) & valid
    oh1 = (riota == i1 * N_GROUPS + grp) & valid
    ohf = jnp.where(oh0 | oh1, 1.0, 0.0)
    before = (lax.broadcasted_iota(I32, (TM, TM), 0)
              < lax.broadcasted_iota(I32, (TM, TM), 1)).astype(BF16)
    cum = jnp.dot(ohf.astype(BF16), before, preferred_element_type=F32) + carry_sc[...]
    rank0 = jnp.sum(jnp.where(oh0, cum, 0.0), axis=0, keepdims=True)
    rank1 = jnp.sum(jnp.where(oh1, cum, 0.0), axis=0, keepdims=True)
    carry = carry_sc[...] + jnp.sum(ohf, axis=1, keepdims=True)
    carry_sc[...] = carry
    cnt_ref[...] = jnp.broadcast_to(carry, cnt_ref.shape)

    zi = jnp.zeros((1, TM), I32)
    mi_ref[...] = jnp.concatenate(
        [grp * EXPERTS_PER_GROUP + i0, grp * EXPERTS_PER_GROUP + i1,
         rank0.astype(I32), rank1.astype(I32), zi, zi, zi, zi], axis=0)
    zf = jnp.zeros((1, TM), F32)
    mf_ref[...] = jnp.concatenate([a0 / denom, a1 / denom, zf, zf, zf, zf, zf, zf], axis=0)


def _ln1_route(h, mix, g, b, router_wt, router_bc):
    row = pl.BlockSpec((TM, D_MODEL), lambda i: (i, 0))
    vec = pl.BlockSpec((1, D_MODEL), lambda i: (0, 0))
    meta = pl.BlockSpec((8, TM), lambda i: (0, i))
    return pl.pallas_call(
        _ln1_route_kernel,
        grid=(R // TM,),
        in_specs=[row, row, vec, vec,
                  pl.BlockSpec((N_EXPERTS, D_MODEL), lambda i: (0, 0)),
                  pl.BlockSpec((N_EXPERTS, 1), lambda i: (0, 0))],
        out_specs=[row, meta, meta, pl.BlockSpec((N_EXPERTS, LANE), lambda i: (0, 0))],
        out_shape=[
            jax.ShapeDtypeStruct((R, D_MODEL), F32),
            jax.ShapeDtypeStruct((8, R), I32),
            jax.ShapeDtypeStruct((8, R), F32),
            jax.ShapeDtypeStruct((N_EXPERTS, LANE), F32),
        ],
        scratch_shapes=[pltpu.VMEM((N_EXPERTS, 1), F32)],
        compiler_params=_cparams(("arbitrary",)),
        name="ln1_route",
    )(h, mix, g.reshape(1, -1), b.reshape(1, -1), router_wt, router_bc)


def _dispatch_tables(mi, counts_slot_major):
    counts = counts_slot_major.reshape(EXPERTS_PER_GROUP, N_GROUPS).T.reshape(N_EXPERTS).astype(I32)
    padded = (counts + EXPERT_BLOCK - 1) // EXPERT_BLOCK * EXPERT_BLOCK
    pad_end = jnp.cumsum(padded)
    pad_start = pad_end - padded
    e_iota = jnp.arange(N_EXPERTS, dtype=I32)
    rows_ok = jnp.arange(R) >= T0

    def dest(eid, rank):
        start = jnp.sum(jnp.where(eid[:, None] == e_iota[None, :], pad_start[None, :], 0), axis=1)
        return jnp.where(rows_ok, start + rank, 0).astype(I32)

    d0 = dest(mi[0], mi[2])
    d1 = dest(mi[1], mi[3])
    blk_start = (pad_start // EXPERT_BLOCK).astype(I32)
    n_blk = (padded // EXPERT_BLOCK).astype(I32)
    n_used = (pad_end[-1] // EXPERT_BLOCK).astype(I32).reshape(1)
    tok_rows = jnp.arange(T0, R, dtype=I32)
    row_src = jnp.zeros((N_ROWS + EXPERT_BLOCK,), I32).at[jnp.concatenate([d0[T0:], d1[T0:]])].set(
        jnp.concatenate([tok_rows, tok_rows]), unique_indices=True)
    blk = jnp.arange(N_BLOCKS + 1, dtype=I32)
    owner = (blk[:, None] >= blk_start[None, :]) & (blk[:, None] < (blk_start + n_blk)[None, :])
    left = counts[None, :] - (blk[:, None] - blk_start[None, :]) * EXPERT_BLOCK
    n_valid = jnp.sum(jnp.where(owner, jnp.clip(left, 0, EXPERT_BLOCK), 0), axis=1).astype(I32)
    return d0, d1, blk_start, n_blk, n_used, row_src, n_valid


def _row_copy(src, src_row, dst, dst_row, sem):
    return pltpu.make_async_copy(src.at[pl.ds(src_row, 1), :], dst.at[pl.ds(dst_row, 1), :], sem)


def _issue_row_gather(rs_ref, nv_ref, g, h_hbm, buf, sem):
    base = g * EXPERT_BLOCK
    nv = nv_ref[g]
    for r in range(EXPERT_BLOCK):
        @pl.when(r < nv)
        def _():
            _row_copy(h_hbm, rs_ref[base + r], buf, r, sem).start(priority=1)


def _wait_row_gather(nv_ref, g, h_hbm, buf, sem):
    nv = nv_ref[g]
    k = EXPERT_BLOCK
    while k >= 1:
        @pl.when((nv & k) != 0)
        def _():
            pltpu.make_async_copy(h_hbm.at[pl.ds(0, k), :], buf.at[pl.ds(0, k), :], sem).wait()
        k //= 2


def _block_rows(g):
    return pl.ds(pl.multiple_of(g * EXPERT_BLOCK, EXPERT_BLOCK), EXPERT_BLOCK)


def _finish_writes(out_copy, obuf, nu):
    @pl.when(nu >= 2)
    def _():
        out_copy(nu - 2, nu % 2).wait()

    @pl.when(nu >= 1)
    def _():
        out_copy(nu - 1, (nu - 1) % 2).wait()

    obuf[0] = jnp.zeros(obuf.shape[1:], obuf.dtype)

    def zero_block(g, carry):
        cp = out_copy(g, 0)
        cp.start()
        cp.wait()
        return carry

    lax.fori_loop(nu, N_BLOCKS, zero_block, 0)


def _moe_up_kernel(bs_ref, nb_ref, nu_ref, rs_ref, nv_ref, h_hbm, wg_ref, wu_ref, o_hbm,
                   wg_sc, wu_sc, xbuf, obuf, xsem, osem):
    e = pl.program_id(0)
    nb = nb_ref[e]
    g0 = bs_ref[e]

    def out_copy(g, slot):
        return pltpu.make_async_copy(obuf.at[slot], o_hbm.at[_block_rows(g), :], osem.at[slot])

    @pl.when(e == 0)
    def _():
        xbuf[...] = jnp.zeros_like(xbuf)
        _issue_row_gather(rs_ref, nv_ref, 0, h_hbm, xbuf.at[0], xsem.at[0])

    @pl.when(nb > 0)
    def _():
        wg_sc[...] = wg_ref[...].astype(BF16)
        wu_sc[...] = wu_ref[...].astype(BF16)

        def block(j, carry):
            g = g0 + j
            slot = g % 2

            @pl.when(g >= 2)
            def _():
                out_copy(g - 2, slot).wait()

            _wait_row_gather(nv_ref, g, h_hbm, xbuf.at[slot], xsem.at[slot])
            _issue_row_gather(rs_ref, nv_ref, g + 1, h_hbm, xbuf.at[1 - slot], xsem.at[1 - slot])
            x = xbuf[slot].astype(BF16)
            gate = jnp.dot(x, wg_sc[...], preferred_element_type=F32)
            up = jnp.dot(x, wu_sc[...], preferred_element_type=F32)
            obuf[slot] = (gate * _sigmoid(gate) * up).astype(BF16)
            out_copy(g, slot).start()
            return carry

        lax.fori_loop(0, nb, block, 0)

    @pl.when(e == N_EXPERTS - 1)
    def _():
        _finish_writes(out_copy, obuf, nu_ref[0])


def _moe_up(blk_start, n_blk, n_used, row_src, n_valid, h1, w_gate, w_up, l):
    wspec = pl.BlockSpec((None, None, D_MODEL, D_EXPERT), lambda e, *_: (l, e, 0, 0))
    grid_spec = pltpu.PrefetchScalarGridSpec(
        num_scalar_prefetch=5,
        grid=(N_EXPERTS,),
        in_specs=[pl.BlockSpec(memory_space=pl.ANY), wspec, wspec],
        out_specs=pl.BlockSpec(memory_space=pl.ANY),
        scratch_shapes=[
            pltpu.VMEM((D_MODEL, D_EXPERT), BF16),
            pltpu.VMEM((D_MODEL, D_EXPERT), BF16),
            pltpu.VMEM((2, EXPERT_BLOCK, D_MODEL), F32),
            pltpu.VMEM((2, EXPERT_BLOCK, D_EXPERT), BF16),
            pltpu.SemaphoreType.DMA((2,)),
            pltpu.SemaphoreType.DMA((2,)),
        ],
    )
    return pl.pallas_call(
        _moe_up_kernel,
        grid_spec=grid_spec,
        out_shape=jax.ShapeDtypeStruct((N_ROWS, D_EXPERT), BF16),
        compiler_params=_cparams(("arbitrary",), vmem=MOE_VMEM_LIMIT),
        name="moe_up",
    )(blk_start, n_blk, n_used, row_src, n_valid, h1, w_gate, w_up)


def _moe_down_kernel(bs_ref, nb_ref, nu_ref, x_hbm, wd_ref, y_hbm, wd_sc, xbuf, obuf, xsem, osem):
    e = pl.program_id(0)
    nb = nb_ref[e]
    g0 = bs_ref[e]
    nu = nu_ref[0]

    def in_copy(g, slot):
        return pltpu.make_async_copy(x_hbm.at[_block_rows(g), :], xbuf.at[slot], xsem.at[slot])

    def out_copy(g, slot):
        return pltpu.make_async_copy(obuf.at[slot], y_hbm.at[_block_rows(g), :], osem.at[slot])

    @pl.when((e == 0) & (nu > 0))
    def _():
        in_copy(0, 0).start()

    @pl.when(nb > 0)
    def _():
        wd_sc[...] = wd_ref[...].astype(BF16)

        def block(j, carry):
            g = g0 + j
            slot = g % 2

            @pl.when(g >= 2)
            def _():
                out_copy(g - 2, slot).wait()

            in_copy(g, slot).wait()

            @pl.when(g + 1 < nu)
            def _():
                in_copy(g + 1, 1 - slot).start()

            obuf[slot] = jnp.dot(xbuf[slot], wd_sc[...], preferred_element_type=F32)
            out_copy(g, slot).start()
            return carry

        lax.fori_loop(0, nb, block, 0)

    @pl.when(e == N_EXPERTS - 1)
    def _():
        _finish_writes(out_copy, obuf, nu)


def _moe_down(blk_start, n_blk, n_used, hmid, w_down, l):
    grid_spec = pltpu.PrefetchScalarGridSpec(
        num_scalar_prefetch=3,
        grid=(N_EXPERTS,),
        in_specs=[
            pl.BlockSpec(memory_space=pl.ANY),
            pl.BlockSpec((None, None, D_EXPERT, D_MODEL), lambda e, *_: (l, e, 0, 0)),
        ],
        out_specs=pl.BlockSpec(memory_space=pl.ANY),
        scratch_shapes=[
            pltpu.VMEM((D_EXPERT, D_MODEL), BF16),
            pltpu.VMEM((2, EXPERT_BLOCK, D_EXPERT), BF16),
            pltpu.VMEM((2, EXPERT_BLOCK, D_MODEL), F32),
            pltpu.SemaphoreType.DMA((2,)),
            pltpu.SemaphoreType.DMA((2,)),
        ],
    )
    return pl.pallas_call(
        _moe_down_kernel,
        grid_spec=grid_spec,
        out_shape=jax.ShapeDtypeStruct((N_ROWS, D_MODEL), F32),
        compiler_params=_cparams(("arbitrary",), vmem=MOE_VMEM_LIMIT),
        name="moe_down",
    )(blk_start, n_blk, n_used, hmid, w_down)


def _combine_kernel(d0_ref, d1_ref, y_hbm, h1_ref, mf_ref, g_ref, b_ref, h2_ref, h2b_ref,
                    buf0, buf1, sem):
    i = pl.program_id(0)
    slot = i % 2

    def issue(tile, s):
        base = tile * LANE

        def body(r, carry):
            _row_copy(y_hbm, d0_ref[base + r], buf0.at[s], r, sem.at[s]).start(priority=0)
            _row_copy(y_hbm, d1_ref[base + r], buf1.at[s], r, sem.at[s]).start(priority=1)
            return carry

        lax.fori_loop(0, LANE, body, 0)

    @pl.when(i == 0)
    def _():
        issue(0, 0)

    @pl.when(i + 1 < pl.num_programs(0))
    def _():
        issue(i + 1, 1 - slot)

    def drain(r, carry):
        _row_copy(y_hbm, 0, buf0.at[slot], r, sem.at[slot]).wait()
        _row_copy(y_hbm, 0, buf1.at[slot], r, sem.at[slot]).wait()
        return carry

    lax.fori_loop(0, LANE, drain, 0)
    wt = mf_ref[...].T
    ffn = wt[:, 0:1] * buf0[slot] + wt[:, 1:2] * buf1[slot]
    y = _post_ln(h1_ref[...], ffn, g_ref[...], b_ref[...], i * LANE)
    h2_ref[...] = y
    h2b_ref[...] = y.astype(BF16)


def _combine_ln2(d0, d1, y_rows, h1, mf, g, b):
    row = lambda i, d0, d1: (i, 0)
    vec = pl.BlockSpec((1, D_MODEL), lambda i, d0, d1: (0, 0))
    grid_spec = pltpu.PrefetchScalarGridSpec(
        num_scalar_prefetch=2,
        grid=(R // LANE,),
        in_specs=[
            pl.BlockSpec(memory_space=pl.ANY),
            pl.BlockSpec((LANE, D_MODEL), row),
            pl.BlockSpec((8, LANE), lambda i, d0, d1: (0, i)),
            vec, vec,
        ],
        out_specs=[pl.BlockSpec((LANE, D_MODEL), row), pl.BlockSpec((LANE, D_MODEL), row)],
        scratch_shapes=[
            pltpu.VMEM((2, LANE, D_MODEL), F32),
            pltpu.VMEM((2, LANE, D_MODEL), F32),
            pltpu.SemaphoreType.DMA((2,)),
        ],
    )
    return pl.pallas_call(
        _combine_kernel,
        grid_spec=grid_spec,
        out_shape=[
            jax.ShapeDtypeStruct((R, D_MODEL), F32),
            jax.ShapeDtypeStruct((R, D_MODEL), BF16),
        ],
        compiler_params=_cparams(("arbitrary",)),
        name="moe_combine_ln2",
    )(d0, d1, y_rows, h1, mf, g.reshape(1, -1), b.reshape(1, -1))


def kernel(x, meta_tokens, ln_in_g, ln_in_b, w_in, fox_f_bias, gla_wa2, gla_ba, gla_norm_g, conv_w, pool_w, pool_scale, gate_b, w_branch, w_out, ln1_g, ln1_b, router_w, router_b, w_gate, w_up, w_down, ln2_g, ln2_b):
    assert x.shape == (1, SEQ, D_MODEL)
    h, hb = _ln_in(x.reshape(SEQ, D_MODEL), meta_tokens, ln_in_g, ln_in_b)
    router_wt = router_w.T.reshape(N_GROUPS, EXPERTS_PER_GROUP, D_MODEL).transpose(1, 0, 2).reshape(
        N_EXPERTS, D_MODEL)
    router_bc = router_b.astype(F32).reshape(N_GROUPS, EXPERTS_PER_GROUP).T.reshape(N_EXPERTS, 1)

    wb_bf = w_branch.astype(BF16)
    wo_bf = w_out.astype(BF16)

    w_all = _transposed_weights(w_in)

    for l in range(DEPTH):
        zf = _matmul_nt(hb, w_all, l, W_FOX, 2 * MIX_W, BF16, TM_PROJ, 512, "proj_fox")
        vt = _matmul_tt(w_all, l, W_FOX + 2 * MIX_W, MIX_W, hb, BF16, TM_PROJ, "proj_fox_vt")
        z = _matmul_nt(hb, w_all, l, W_MIX, N_MIXC, F32, TM_PROJ, 512, "proj_mix")
        zs = _matmul_nt(hb, w_all, l, W_SMALL, 2 * LANE, F32, TM_PROJ, 2 * LANE, "proj_small")

        bias_row = jnp.zeros((1, LANE), F32).at[0, SM_FF:SM_FF + FOX_HEADS].set(fox_f_bias[l])
        c = _fox_gate(zs, bias_row)
        o_a = _fox_attention(zf, vt, c)

        wa2p = jnp.zeros((LANE, GLA_HEADS * GLA_DK), F32).at[SM_GA:SM_GA + GLA_RANK].set(gla_wa2[l])
        o_b = _gla(z, zs, wa2p, gla_ba[l].reshape(1, -1), gla_norm_g[l].reshape(1, -1))

        o_c, o_d = _local_mixers(z, conv_w[l], pool_w[l].astype(BF16), pool_scale[l])

        mix = _merge(hb, o_a, o_b, o_c, o_d, w_all, gate_b[l], wb_bf, wo_bf, l)
        h1, mi, mf, counts = _ln1_route(h, mix, ln1_g[l], ln1_b[l], router_wt, router_bc)

        d0, d1, blk_start, n_blk, n_used, row_src, n_valid = _dispatch_tables(mi, counts[:, 0])
        hmid = _moe_up(blk_start, n_blk, n_used, row_src, n_valid, h1, w_gate, w_up, l)
        y_rows = _moe_down(blk_start, n_blk, n_used, hmid, w_down, l)
        h, hb = _combine_ln2(d0, d1, y_rows, h1, mf, ln2_g[l], ln2_b[l])

    return h[PAD_ROWS + N_META:].reshape(1, SEQ, D_MODEL)
```

```python
import jax
import jax.numpy as jnp
import numpy as np
from jax import lax
from jax.experimental import pallas as pl
from jax.experimental.pallas import tpu as pltpu

F32 = jnp.float32
BF16 = jnp.bfloat16
I32 = jnp.int32
HIGHEST = lax.Precision.HIGHEST

D_MODEL = 2048
SEQ = 8192
DEPTH = 2
N_META = 16
N_BRANCH = 4
MIX_W = 512
FOX_HEADS = 4
FOX_HD = 128
GLA_HEADS = 4
GLA_DK = 64
GLA_DV = 128
GLA_RANK = 16
GLA_TAU = 16.0
CONV_K = 3
POOL_WINDOWS = (2, 4, 8, 16)
POOL_GW = 128
N_EXPERTS = 32
N_GROUPS = 8
EXPERTS_PER_GROUP = 4
TOP_K = 2
D_EXPERT = 1024
LN_EPS = 1e-5
DEEPNORM_ALPHA = (2 * DEPTH) ** 0.25

_SPLITS = (512, 512, 512, 4, 256, 256, 512, 16, 512, 512, 512, 512, 512, 8192)
_OFFS = [int(o) for o in np.concatenate([[0], np.cumsum(_SPLITS)])]
(O_FQ, O_FK, O_FV, O_FF, O_GQ, O_GK, O_GV, O_GA, O_GR, O_CB, O_CC, O_CV, O_PZ, O_GZ, P_IN) = _OFFS

LANE = 128
PAD_ROWS = LANE - N_META
T0 = PAD_ROWS
N_TOK = N_META + SEQ
R = PAD_ROWS + N_TOK
TM = 640
TM_PROJ = 1664
HALO = 16

WT_TILE = 128
W_FOX, W_MIX, W_GATES = 0, 1536, 5120
N_FOX, N_MIXC, N_GATES = 1536, 3584, 8192
W_SMALL = W_GATES + N_GATES
N_WT = W_SMALL + 2 * LANE
C_GQ, C_GK, C_GV, C_GR, C_CB, C_CC, C_CV, C_PZ = 0, 256, 512, 1024, 1536, 2048, 2560, 3072
SM_FF_TILE, SM_GA_TILE = O_FF // LANE, O_GA // LANE
SM_FF = O_FF - SM_FF_TILE * LANE
SM_GA = O_GA - SM_GA_TILE * LANE


def _wt_sources():
    src = []
    for r in range(0, W_SMALL, WT_TILE):
        if r < N_FOX:
            src.append(O_FQ + r)
        elif r < W_MIX + 1024:
            src.append(O_GQ + r - W_MIX)
        else:
            src.append(O_GR + r - (W_MIX + 1024))
    src += [SM_FF_TILE * LANE, SM_GA_TILE * LANE]
    assert all(0 <= s and s + WT_TILE <= P_IN for s in src)
    return src


WT_SRC = _wt_sources()

EXPERT_BLOCK = 128
N_FLAT = N_TOK * TOP_K
N_BLOCKS = -(-N_FLAT // EXPERT_BLOCK) + N_EXPERTS
N_ROWS = N_BLOCKS * EXPERT_BLOCK

NEG = -1e30
VMEM_LIMIT = 48 * 1024 * 1024
BIG_VMEM_LIMIT = 56 * 1024 * 1024
MOE_VMEM_LIMIT = BIG_VMEM_LIMIT


def _cparams(sem, vmem=VMEM_LIMIT):
    return pltpu.CompilerParams(dimension_semantics=sem, vmem_limit_bytes=vmem)


def _log_sigmoid(x):
    return jnp.minimum(x, 0.0) - jnp.log1p(jnp.exp(-jnp.abs(x)))


def _sigmoid(x):
    return 1.0 / (1.0 + jnp.exp(-x))


def _layer_norm_rows(x, g, b):
    mu = jnp.mean(x, axis=-1, keepdims=True)
    xc = x - mu
    var = jnp.mean(xc * xc, axis=-1, keepdims=True)
    return xc * lax.rsqrt(var + LN_EPS) * g + b


def _ln_in_kernel(x_ref, meta_ref, g_ref, b_ref, h_ref, hb_ref):
    i = pl.program_id(0)

    @pl.when(i == 0)
    def _():
        h_ref[...] = jnp.zeros_like(h_ref)
        hb_ref[...] = jnp.zeros_like(hb_ref)
        m = _layer_norm_rows(meta_ref[...], g_ref[...], b_ref[...])
        h_ref[PAD_ROWS:, :] = m
        hb_ref[PAD_ROWS:, :] = m.astype(BF16)

    @pl.when(i > 0)
    def _():
        y = _layer_norm_rows(x_ref[...], g_ref[...], b_ref[...])
        h_ref[...] = y
        hb_ref[...] = y.astype(BF16)


def _ln_in(x2d, meta, g, b):
    nb = R // LANE
    return pl.pallas_call(
        _ln_in_kernel,
        grid=(nb,),
        in_specs=[
            pl.BlockSpec((LANE, D_MODEL), lambda i: (jnp.maximum(i - 1, 0), 0)),
            pl.BlockSpec((N_META, D_MODEL), lambda i: (0, 0)),
            pl.BlockSpec((1, D_MODEL), lambda i: (0, 0)),
            pl.BlockSpec((1, D_MODEL), lambda i: (0, 0)),
        ],
        out_specs=[
            pl.BlockSpec((LANE, D_MODEL), lambda i: (i, 0)),
            pl.BlockSpec((LANE, D_MODEL), lambda i: (i, 0)),
        ],
        out_shape=[
            jax.ShapeDtypeStruct((R, D_MODEL), F32),
            jax.ShapeDtypeStruct((R, D_MODEL), BF16),
        ],
        compiler_params=_cparams(("arbitrary",)),
        name="ln_in",
    )(x2d, meta, g.reshape(1, -1), b.reshape(1, -1))


D_CHUNKS = D_MODEL // LANE
FLAT_ROWS = D_CHUNKS * DEPTH


def _wt_kernel(src_ref, w_hbm, o_ref, buf, sem):
    j = pl.program_id(0)
    slot = j % 2

    def tile_copy(t, s):
        row0 = pl.multiple_of(src_ref[t] * FLAT_ROWS, FLAT_ROWS)
        return pltpu.make_async_copy(w_hbm.at[pl.ds(row0, WT_TILE * FLAT_ROWS), :], buf.at[s],
                                     sem.at[s])

    @pl.when(j == 0)
    def _():
        tile_copy(0, 0).start()

    @pl.when(j + 1 < pl.num_programs(0))
    def _():
        tile_copy(j + 1, 1 - slot).start()

    tile_copy(j, slot).wait()
    for l in range(DEPTH):
        for c in range(D_CHUNKS):
            o_ref[l, :, c * LANE:(c + 1) * LANE] = buf[
                slot, pl.ds(c * DEPTH + l, WT_TILE, stride=FLAT_ROWS), :].astype(BF16)


def _transposed_weights(w_in):
    flat = w_in.reshape(DEPTH, D_CHUNKS, LANE, P_IN).transpose(3, 1, 0, 2).reshape(
        P_IN * FLAT_ROWS, LANE)
    grid_spec = pltpu.PrefetchScalarGridSpec(
        num_scalar_prefetch=1,
        grid=(N_WT // WT_TILE,),
        in_specs=[pl.BlockSpec(memory_space=pl.ANY)],
        out_specs=pl.BlockSpec((DEPTH, WT_TILE, D_MODEL), lambda j, src: (0, j, 0)),
        scratch_shapes=[
            pltpu.VMEM((2, WT_TILE * FLAT_ROWS, LANE), F32),
            pltpu.SemaphoreType.DMA((2,)),
        ],
    )
    return pl.pallas_call(
        _wt_kernel,
        grid_spec=grid_spec,
        out_shape=jax.ShapeDtypeStruct((DEPTH, N_WT, D_MODEL), BF16),
        compiler_params=_cparams(("arbitrary",)),
        name="transposed_weights",
    )(jnp.asarray(WT_SRC, I32), flat)


_NT = (((1,), (1,)), ((), ()))


def _mm_nt_kernel(a_ref, wt_ref, o_ref):
    o_ref[...] = lax.dot_general(a_ref[...], wt_ref[...], _NT,
                                 preferred_element_type=F32).astype(o_ref.dtype)


def _matmul_nt(a, wt, l, row0, n, out_dtype, tm, tn, name):
    m, k = a.shape
    return pl.pallas_call(
        _mm_nt_kernel,
        grid=(m // tm, n // tn),
        in_specs=[
            pl.BlockSpec((tm, k), lambda i, j: (i, 0)),
            pl.BlockSpec((None, tn, k), lambda i, j: (l, row0 // tn + j, 0)),
        ],
        out_specs=pl.BlockSpec((tm, tn), lambda i, j: (i, j)),
        out_shape=jax.ShapeDtypeStruct((m, n), out_dtype),
        compiler_params=_cparams(("parallel", "arbitrary")),
        name=name,
    )(a, wt)


def _matmul_tt(wt, l, row0, n, a, out_dtype, tm, name):
    m, k = a.shape
    return pl.pallas_call(
        _mm_nt_kernel,
        grid=(m // tm,),
        in_specs=[
            pl.BlockSpec((None, n, k), lambda i: (l, row0 // n, 0)),
            pl.BlockSpec((tm, k), lambda i: (i, 0)),
        ],
        out_specs=pl.BlockSpec((n, tm), lambda i: (0, i)),
        out_shape=jax.ShapeDtypeStruct((n, m), out_dtype),
        compiler_params=_cparams(("parallel",)),
        name=name,
    )(wt, a)


def _fox_gate_kernel(zs_ref, bias_ref, c_ref, carry_ref):
    i = pl.program_id(0)

    @pl.when(i == 0)
    def _():
        carry_ref[...] = jnp.zeros_like(carry_ref)

    rows = i * TM + lax.broadcasted_iota(I32, (TM, LANE), 0)
    lf = _log_sigmoid(zs_ref[...] + bias_ref[...])
    lf = jnp.where(rows >= T0, lf, 0.0)
    tri = (lax.broadcasted_iota(I32, (TM, TM), 0)
           >= lax.broadcasted_iota(I32, (TM, TM), 1)).astype(F32)
    c = jnp.dot(tri, lf, precision=HIGHEST, preferred_element_type=F32) + carry_ref[...]
    c_ref[...] = c
    carry_ref[...] = c[TM - 1:TM, :]


def _fox_gate(zs, bias_row):
    return pl.pallas_call(
        _fox_gate_kernel,
        grid=(R // TM,),
        in_specs=[
            pl.BlockSpec((TM, LANE), lambda i: (i, 0)),
            pl.BlockSpec((1, LANE), lambda i: (0, 0)),
        ],
        out_specs=pl.BlockSpec((TM, LANE), lambda i: (i, 0)),
        out_shape=jax.ShapeDtypeStruct((R, LANE), F32),
        scratch_shapes=[pltpu.VMEM((1, LANE), F32)],
        compiler_params=_cparams(("arbitrary",)),
        name="fox_gate",
    )(zs, bias_row)


TQ = TM
N_QB = R // TQ
_PAIRS = [(qi, kj) for qi in range(N_QB) for kj in range(qi + 1)]
N_PAIRS = len(_PAIRS)


LOG2E = 1.4426950408889634
FOX_HPS = FOX_HEADS


def _fox_kernel(qi_tab, kj_tab, q_ref, k_ref, vt_ref, ck_ref, o_ref, m_sc, l_sc, acc_sc):
    p = pl.program_id(1)
    qi = qi_tab[p]
    kj = kj_tab[p]

    @pl.when(kj == 0)
    def _():
        m_sc[...] = jnp.full_like(m_sc, NEG)
        l_sc[...] = jnp.zeros_like(l_sc)
        acc_sc[...] = jnp.zeros_like(acc_sc)

    kpos = kj * TQ + lax.broadcasted_iota(I32, (TQ, 1), 0)
    c1 = FOX_HD ** -0.5 * LOG2E

    def step(causal):
        for hh in range(FOX_HPS):
            lanes = slice(hh * FOX_HD, (hh + 1) * FOX_HD)
            ck = ck_ref[:, SM_FF + hh:SM_FF + hh + 1]
            ckl = jnp.where(kpos >= T0, ck * LOG2E, -NEG)
            t = lax.dot_general(k_ref[:, lanes], q_ref[:, lanes], (((1,), (1,)), ((), ())),
                                preferred_element_type=F32) * c1 - ckl
            if causal:
                ahead = (lax.broadcasted_iota(I32, (TQ, TQ), 0)
                         - lax.broadcasted_iota(I32, (TQ, TQ), 1))
                t = jnp.where(ahead <= 0, t, NEG)
            m_prev = m_sc[hh]
            m_new = jnp.maximum(m_prev, jnp.max(t, axis=0, keepdims=True))
            alpha = jnp.exp2(m_prev - m_new)
            pr = jnp.exp2(t - m_new)
            l_sc[hh] = alpha * l_sc[hh] + jnp.sum(pr, axis=0, keepdims=True)
            acc_sc[hh] = alpha * acc_sc[hh] + jnp.dot(vt_ref[lanes, :], pr.astype(BF16),
                                                      preferred_element_type=F32)
            m_sc[hh] = m_new

    @pl.when(kj == qi)
    def _():
        step(True)

    @pl.when(kj != qi)
    def _():
        step(False)

    @pl.when(kj == qi)
    def _():
        for hh in range(FOX_HPS):
            o_ref[:, hh * FOX_HD:(hh + 1) * FOX_HD] = (acc_sc[hh] / l_sc[hh]).T.astype(o_ref.dtype)


def _fox_attention(zf, vt, c_col):
    qi_tab = jnp.asarray([p[0] for p in _PAIRS], I32)
    kj_tab = jnp.asarray([p[1] for p in _PAIRS], I32)
    hw = FOX_HPS * FOX_HD
    grid_spec = pltpu.PrefetchScalarGridSpec(
        num_scalar_prefetch=2,
        grid=(FOX_HEADS // FOX_HPS, N_PAIRS),
        in_specs=[
            pl.BlockSpec((TQ, hw), lambda h, p, qt, kt: (qt[p], h)),
            pl.BlockSpec((TQ, hw), lambda h, p, qt, kt: (kt[p], FOX_HEADS // FOX_HPS + h)),
            pl.BlockSpec((hw, TQ), lambda h, p, qt, kt: (h, kt[p])),
            pl.BlockSpec((TQ, LANE), lambda h, p, qt, kt: (kt[p], 0)),
        ],
        out_specs=pl.BlockSpec((TQ, hw), lambda h, p, qt, kt: (qt[p], h)),
        scratch_shapes=[
            pltpu.VMEM((FOX_HPS, 1, TQ), F32),
            pltpu.VMEM((FOX_HPS, 1, TQ), F32),
            pltpu.VMEM((FOX_HPS, FOX_HD, TQ), F32),
        ],
    )
    return pl.pallas_call(
        _fox_kernel,
        grid_spec=grid_spec,
        out_shape=jax.ShapeDtypeStruct((R, MIX_W), BF16),
        compiler_params=_cparams(("parallel", "arbitrary")),
        name="fox_attention",
    )(qi_tab, kj_tab, zf, zf, vt, c_col)


GLA_CHUNK = 64
GLA_UNROLL = 5


def _gla_kernel(q_ref, k_ref, v_ref, gr_ref, zs_ref, wa2_ref, ba_ref, gn_ref, o_ref, st_ref, la_ref):
    i = pl.program_id(0)

    @pl.when(i == 0)
    def _():
        st_ref[...] = jnp.zeros_like(st_ref)

    la = jnp.dot(zs_ref[...], wa2_ref[...], precision=HIGHEST, preferred_element_type=F32)
    la_ref[...] = _log_sigmoid(la + ba_ref[...]) * (1.0 / GLA_TAU)

    c_r = lax.broadcasted_iota(I32, (GLA_CHUNK, GLA_CHUNK), 0)
    c_c = lax.broadcasted_iota(I32, (GLA_CHUNK, GLA_CHUNK), 1)
    tri_b = c_r >= c_c
    tri = jnp.where(tri_b, 1.0, 0.0).astype(BF16)

    def chunk(c, carry):
        r0 = pl.multiple_of(c * GLA_CHUNK, GLA_CHUNK)
        rows = pl.ds(r0, GLA_CHUNK)
        g = la_ref[rows, :]
        g_hi = g.astype(BF16)
        g_lo = (g - g_hi.astype(F32)).astype(BF16)
        b = (jnp.dot(tri, g_hi, preferred_element_type=F32)
             + jnp.dot(tri, g_lo, preferred_element_type=F32))
        b_last = b[GLA_CHUNK - 1:GLA_CHUNK, :]
        e_last = jnp.exp(b_last)
        qt = q_ref[rows, :] * (GLA_DK ** -0.5) * jnp.exp(b)
        kt = k_ref[rows, :] * jnp.exp(-b)
        kh = kt * e_last
        for h in range(GLA_HEADS):
            ks = slice(h * GLA_DK, (h + 1) * GLA_DK)
            vs = slice(h * GLA_DV, (h + 1) * GLA_DV)
            q_h = qt[:, ks].astype(BF16)
            k_h = kt[:, ks].astype(BF16)
            kh_h = kh[:, ks].astype(BF16)
            v_h = v_ref[rows, vs]
            att = lax.dot_general(q_h, k_h, (((1,), (1,)), ((), ())), preferred_element_type=F32)
            att = jnp.where(tri_b, att, 0.0)
            st = st_ref[h]
            o = jnp.dot(att.astype(BF16), v_h.astype(BF16), preferred_element_type=F32)
            o = o + lax.dot_general(q_h, st.astype(BF16), (((1,), (1,)), ((), ())),
                                    preferred_element_type=F32)
            st_ref[h] = st * e_last[:, ks] + jnp.dot(v_h.T.astype(BF16), kh_h,
                                                     preferred_element_type=F32)
            ms = jnp.mean(o * o, axis=-1, keepdims=True)
            on = o * lax.rsqrt(ms + LN_EPS) * gn_ref[:, vs]
            gate = gr_ref[rows, vs]
            o_ref[rows, vs] = (on * (gate * _sigmoid(gate))).astype(o_ref.dtype)
        return carry

    lax.fori_loop(0, TM // GLA_CHUNK, chunk, 0, unroll=GLA_UNROLL)


def _gla(z, zs, wa2p, ba, gn):
    return pl.pallas_call(
        _gla_kernel,
        grid=(R // TM,),
        in_specs=[
            pl.BlockSpec((TM, 256), lambda i: (i, C_GQ // 256)),
            pl.BlockSpec((TM, 256), lambda i: (i, C_GK // 256)),
            pl.BlockSpec((TM, 512), lambda i: (i, C_GV // 512)),
            pl.BlockSpec((TM, 512), lambda i: (i, C_GR // 512)),
            pl.BlockSpec((TM, LANE), lambda i: (i, 1)),
            pl.BlockSpec((LANE, 256), lambda i: (0, 0)),
            pl.BlockSpec((1, 256), lambda i: (0, 0)),
            pl.BlockSpec((1, 512), lambda i: (0, 0)),
        ],
        out_specs=pl.BlockSpec((TM, MIX_W), lambda i: (i, 0)),
        out_shape=jax.ShapeDtypeStruct((R, MIX_W), BF16),
        scratch_shapes=[
            pltpu.VMEM((GLA_HEADS, GLA_DV, GLA_DK), F32),
            pltpu.VMEM((TM, GLA_HEADS * GLA_DK), F32),
        ],
        compiler_params=_cparams(("arbitrary",)),
        name="gla",
    )(z, z, z, z, zs, wa2p, ba, gn)


def _local_kernel(cb_ref, cc_ref, cv_ref, pz_ref, cw_ref, pw_ref, ps_ref, oc_ref, od_ref, u_sc, p_sc):
    i = pl.program_id(0)

    @pl.when(i == 0)
    def _():
        u_sc[0:HALO, :] = jnp.zeros((HALO, MIX_W), F32)
        p_sc[0:HALO, :] = jnp.zeros((HALO, MIX_W), F32)

    @pl.when(i > 0)
    def _():
        u_sc[0:HALO, :] = u_sc[TM:TM + HALO, :]
        p_sc[0:HALO, :] = p_sc[TM:TM + HALO, :]

    u = cc_ref[...] * cv_ref[...]
    pz = pz_ref[...]
    u_sc[HALO:, :] = u
    p_sc[HALO:, :] = pz

    y = (cw_ref[2:3, :] * u + cw_ref[1:2, :] * u_sc[HALO - 1:HALO - 1 + TM, :]
         + cw_ref[0:1, :] * u_sc[HALO - 2:HALO - 2 + TM, :])
    oc_ref[...] = (cb_ref[...] * y).astype(oc_ref.dtype)

    tok = i * TM - T0 + lax.broadcasted_iota(I32, (TM, 1), 0)
    cnt_small = jnp.maximum(tok + 1, 1).astype(F32)
    for g, w in enumerate(POOL_WINDOWS):
        cols = slice(g * POOL_GW, (g + 1) * POOL_GW)
        x = pz[:, cols]
        s = x
        for j in range(1, w):
            s = s + p_sc[HALO - j:HALO - j + TM, cols]
        inv_cnt = jnp.where(tok + 1 >= w, 1.0 / w, 1.0 / cnt_small)
        pooled = s * inv_cnt - x
        od = jnp.dot(pooled.astype(BF16), pw_ref[g], preferred_element_type=F32)
        od_ref[:, cols] = (od * ps_ref[:, cols]).astype(od_ref.dtype)


def _local_mixers(z, conv_w, pool_w_bf, pool_scale):
    cw = jnp.zeros((8, MIX_W), F32).at[:CONV_K].set(conv_w)
    blk = lambda c: pl.BlockSpec((TM, MIX_W), lambda i, c=c: (i, c // MIX_W))
    return pl.pallas_call(
        _local_kernel,
        grid=(R // TM,),
        in_specs=[
            blk(C_CB), blk(C_CC), blk(C_CV), blk(C_PZ),
            pl.BlockSpec((8, MIX_W), lambda i: (0, 0)),
            pl.BlockSpec((len(POOL_WINDOWS), POOL_GW, POOL_GW), lambda i: (0, 0, 0)),
            pl.BlockSpec((1, MIX_W), lambda i: (0, 0)),
        ],
        out_specs=[
            pl.BlockSpec((TM, MIX_W), lambda i: (i, 0)),
            pl.BlockSpec((TM, MIX_W), lambda i: (i, 0)),
        ],
        out_shape=[
            jax.ShapeDtypeStruct((R, MIX_W), BF16),
            jax.ShapeDtypeStruct((R, MIX_W), BF16),
        ],
        scratch_shapes=[
            pltpu.VMEM((TM + HALO, MIX_W), F32),
            pltpu.VMEM((TM + HALO, MIX_W), F32),
        ],
        compiler_params=_cparams(("arbitrary",)),
        name="conv_pool",
    )(z, z, z, z, cw, pool_w_bf, pool_scale.reshape(1, -1))


TN_MERGE = 512


def _merge_kernel(hb_ref, oa_ref, ob_ref, oc_ref, od_ref, wg0_ref, wg1_ref, wg2_ref, wg3_ref,
                  gb_ref, wb_ref, wo_ref, out_ref):
    j = pl.program_id(1)

    @pl.when(j == 0)
    def _():
        out_ref[...] = jnp.zeros_like(out_ref)

    hb = hb_ref[...]
    mixed = None
    for b, (o_ref, wg_ref) in enumerate(((oa_ref, wg0_ref), (ob_ref, wg1_ref),
                                         (oc_ref, wg2_ref), (od_ref, wg3_ref))):
        gate = _sigmoid(lax.dot_general(hb, wg_ref[...], _NT, preferred_element_type=F32)
                        + gb_ref[b:b + 1, :])
        proj = jnp.dot(o_ref[...], wb_ref[b], preferred_element_type=F32)
        term = gate * proj
        mixed = term if mixed is None else mixed + term
    out_ref[...] += jnp.dot(mixed.astype(BF16), wo_ref[...], preferred_element_type=F32)


def _merge(hb, o_a, o_b, o_c, o_d, w_all, gate_b, wb_bf, wo_bf, l):
    tn = TN_MERGE
    nj = D_MODEL // tn
    row = lambda w: pl.BlockSpec((TM, w), lambda i, j: (i, 0))
    wg = lambda b: pl.BlockSpec((None, tn, D_MODEL),
                                lambda i, j, b=b: (l, W_GATES // tn + b * nj + j, 0))
    return pl.pallas_call(
        _merge_kernel,
        grid=(R // TM, nj),
        in_specs=[
            row(D_MODEL), row(MIX_W), row(MIX_W), row(MIX_W), row(MIX_W),
            wg(0), wg(1), wg(2), wg(3),
            pl.BlockSpec((N_BRANCH, tn), lambda i, j: (0, j)),
            pl.BlockSpec((None, N_BRANCH, MIX_W, tn), lambda i, j: (l, 0, 0, j)),
            pl.BlockSpec((None, tn, D_MODEL), lambda i, j: (l, j, 0)),
        ],
        out_specs=pl.BlockSpec((TM, D_MODEL), lambda i, j: (i, 0)),
        out_shape=jax.ShapeDtypeStruct((R, D_MODEL), F32),
        compiler_params=_cparams(("parallel", "arbitrary"), vmem=BIG_VMEM_LIMIT),
        name="merge",
    )(hb, o_a, o_b, o_c, o_d, w_all, w_all, w_all, w_all, gate_b, wb_bf, wo_bf)


def _post_ln(h, delta, g, b, row0):
    y = _layer_norm_rows(DEEPNORM_ALPHA * h + delta, g, b)
    rows = row0 + lax.broadcasted_iota(I32, (y.shape[0], 1), 0)
    return jnp.where(rows >= T0, y, 0.0)


def _first_of(cands, target):
    idx = jnp.full(target.shape, len(cands) - 1, I32)
    for j in range(len(cands) - 2, -1, -1):
        idx = jnp.where(cands[j] == target, j, idx)
    return idx


def _pick(cands, idx):
    out = cands[-1]
    for j in range(len(cands) - 2, -1, -1):
        out = jnp.where(idx == j, cands[j], out)
    return out


def _ln1_route_kernel(h_ref, mix_ref, g_ref, b_ref, rwt_ref, rb_ref,
                      h1_ref, mi_ref, mf_ref, cnt_ref, carry_sc):
    i = pl.program_id(0)

    @pl.when(i == 0)
    def _():
        carry_sc[...] = jnp.zeros_like(carry_sc)

    y = _post_ln(h_ref[...], mix_ref[...], g_ref[...], b_ref[...], i * TM)
    h1_ref[...] = y

    logits = lax.dot_general(rwt_ref[...], y, (((1,), (1,)), ((), ())), precision=HIGHEST,
                             preferred_element_type=F32)
    aff = _sigmoid(logits)
    sel = aff + rb_ref[...]
    xs = [sel[j * N_GROUPS:(j + 1) * N_GROUPS, :] for j in range(EXPERTS_PER_GROUP)]
    afs = [aff[j * N_GROUPS:(j + 1) * N_GROUPS, :] for j in range(EXPERTS_PER_GROUP)]

    score = None
    for a in range(EXPERTS_PER_GROUP):
        for bb in range(a + 1, EXPERTS_PER_GROUP):
            pair = xs[a] + xs[bb]
            score = pair if score is None else jnp.maximum(score, pair)
    giota = lax.broadcasted_iota(I32, (N_GROUPS, TM), 0)
    gmax = jnp.max(score, axis=0, keepdims=True)
    grp = jnp.min(jnp.where(score == gmax, giota, N_GROUPS), axis=0, keepdims=True)
    gsel = giota == grp
    cs = [jnp.max(jnp.where(gsel, x, -jnp.inf), axis=0, keepdims=True) for x in xs]
    acs = [jnp.sum(jnp.where(gsel, a, 0.0), axis=0, keepdims=True) for a in afs]

    m1 = jnp.maximum(jnp.maximum(cs[0], cs[1]), jnp.maximum(cs[2], cs[3]))
    i0 = _first_of(cs, m1)
    ds = [jnp.where(i0 == j, -jnp.inf, cs[j]) for j in range(EXPERTS_PER_GROUP)]
    m2 = jnp.maximum(jnp.maximum(ds[0], ds[1]), jnp.maximum(ds[2], ds[3]))
    i1 = _first_of(ds, m2)
    a0 = _pick(acs, i0)
    a1 = _pick(acs, i1)
    denom = a0 + a1

    pos = i * TM + lax.broadcasted_iota(I32, (1, TM), 1)
    valid = pos >= T0
    riota = lax.broadcasted_iota(I32, (N_EXPERTS, TM), 0)
    oh0 = (riota == i0 * N_GROUPS + grp) & valid
    oh1 = (riota == i1 * N_GROUPS + grp) & valid
    ohf = jnp.where(oh0 | oh1, 1.0, 0.0)
    before = (lax.broadcasted_iota(I32, (TM, TM), 0)
              < lax.broadcasted_iota(I32, (TM, TM), 1)).astype(BF16)
    cum = jnp.dot(ohf.astype(BF16), before, preferred_element_type=F32) + carry_sc[...]
    rank0 = jnp.sum(jnp.where(oh0, cum, 0.0), axis=0, keepdims=True)
    rank1 = jnp.sum(jnp.where(oh1, cum, 0.0), axis=0, keepdims=True)
    carry = carry_sc[...] + jnp.sum(ohf, axis=1, keepdims=True)
    carry_sc[...] = carry
    cnt_ref[...] = jnp.broadcast_to(carry, cnt_ref.shape)

    zi = jnp.zeros((1, TM), I32)
    mi_ref[...] = jnp.concatenate(
        [grp * EXPERTS_PER_GROUP + i0, grp * EXPERTS_PER_GROUP + i1,
         rank0.astype(I32), rank1.astype(I32), zi, zi, zi, zi], axis=0)
    zf = jnp.zeros((1, TM), F32)
    mf_ref[...] = jnp.concatenate([a0 / denom, a1 / denom, zf, zf, zf, zf, zf, zf], axis=0)


def _ln1_route(h, mix, g, b, router_wt, router_bc):
    row = pl.BlockSpec((TM, D_MODEL), lambda i: (i, 0))
    vec = pl.BlockSpec((1, D_MODEL), lambda i: (0, 0))
    meta = pl.BlockSpec((8, TM), lambda i: (0, i))
    return pl.pallas_call(
        _ln1_route_kernel,
        grid=(R // TM,),
        in_specs=[row, row, vec, vec,
                  pl.BlockSpec((N_EXPERTS, D_MODEL), lambda i: (0, 0)),
                  pl.BlockSpec((N_EXPERTS, 1), lambda i: (0, 0))],
        out_specs=[row, meta, meta, pl.BlockSpec((N_EXPERTS, LANE), lambda i: (0, 0))],
        out_shape=[
            jax.ShapeDtypeStruct((R, D_MODEL), F32),
            jax.ShapeDtypeStruct((8, R), I32),
            jax.ShapeDtypeStruct((8, R), F32),
            jax.ShapeDtypeStruct((N_EXPERTS, LANE), F32),
        ],
        scratch_shapes=[pltpu.VMEM((N_EXPERTS, 1), F32)],
        compiler_params=_cparams(("arbitrary",)),
        name="ln1_route",
    )(h, mix, g.reshape(1, -1), b.reshape(1, -1), router_wt, router_bc)


def _dispatch_tables(mi, counts_slot_major):
    counts = counts_slot_major.reshape(EXPERTS_PER_GROUP, N_GROUPS).T.reshape(N_EXPERTS).astype(I32)
    padded = (counts + EXPERT_BLOCK - 1) // EXPERT_BLOCK * EXPERT_BLOCK
    pad_end = jnp.cumsum(padded)
    pad_start = pad_end - padded
    e_iota = jnp.arange(N_EXPERTS, dtype=I32)
    rows_ok = jnp.arange(R) >= T0

    def dest(eid, rank):
        start = jnp.sum(jnp.where(eid[:, None] == e_iota[None, :], pad_start[None, :], 0), axis=1)
        return jnp.where(rows_ok, start + rank, 0).astype(I32)

    d0 = dest(mi[0], mi[2])
    d1 = dest(mi[1], mi[3])
    blk_start = (pad_start // EXPERT_BLOCK).astype(I32)
    n_blk = (padded // EXPERT_BLOCK).astype(I32)
    n_used = (pad_end[-1] // EXPERT_BLOCK).astype(I32).reshape(1)
    tok_rows = jnp.arange(T0, R, dtype=I32)
    row_src = jnp.zeros((N_ROWS + EXPERT_BLOCK,), I32).at[jnp.concatenate([d0[T0:], d1[T0:]])].set(
        jnp.concatenate([tok_rows, tok_rows]), unique_indices=True)
    blk = jnp.arange(N_BLOCKS + 1, dtype=I32)
    owner = (blk[:, None] >= blk_start[None, :]) & (blk[:, None] < (blk_start + n_blk)[None, :])
    left = counts[None, :] - (blk[:, None] - blk_start[None, :]) * EXPERT_BLOCK
    n_valid = jnp.sum(jnp.where(owner, jnp.clip(left, 0, EXPERT_BLOCK), 0), axis=1).astype(I32)
    return d0, d1, blk_start, n_blk, n_used, row_src, n_valid


def _row_copy(src, src_row, dst, dst_row, sem):
    return pltpu.make_async_copy(src.at[pl.ds(src_row, 1), :], dst.at[pl.ds(dst_row, 1), :], sem)


def _issue_row_gather(rs_ref, nv_ref, g, h_hbm, buf, sem):
    base = g * EXPERT_BLOCK
    nv = nv_ref[g]
    for r in range(EXPERT_BLOCK):
        @pl.when(r < nv)
        def _():
            _row_copy(h_hbm, rs_ref[base + r], buf, r, sem).start(priority=1)


def _wait_row_gather(nv_ref, g, h_hbm, buf, sem):
    nv = nv_ref[g]
    k = EXPERT_BLOCK
    while k >= 1:
        @pl.when((nv & k) != 0)
        def _():
            pltpu.make_async_copy(h_hbm.at[pl.ds(0, k), :], buf.at[pl.ds(0, k), :], sem).wait()
        k //= 2


def _block_rows(g):
    return pl.ds(pl.multiple_of(g * EXPERT_BLOCK, EXPERT_BLOCK), EXPERT_BLOCK)


def _finish_writes(out_copy, obuf, nu):
    @pl.when(nu >= 2)
    def _():
        out_copy(nu - 2, nu % 2).wait()

    @pl.when(nu >= 1)
    def _():
        out_copy(nu - 1, (nu - 1) % 2).wait()

    obuf[0] = jnp.zeros(obuf.shape[1:], obuf.dtype)

    def zero_block(g, carry):
        cp = out_copy(g, 0)
        cp.start()
        cp.wait()
        return carry

    lax.fori_loop(nu, N_BLOCKS, zero_block, 0)


def _moe_up_kernel(bs_ref, nb_ref, nu_ref, rs_ref, nv_ref, h_hbm, wg_ref, wu_ref, o_hbm,
                   wg_sc, wu_sc, xbuf, obuf, xsem, osem):
    e = pl.program_id(0)
    nb = nb_ref[e]
    g0 = bs_ref[e]

    def out_copy(g, slot):
        return pltpu.make_async_copy(obuf.at[slot], o_hbm.at[_block_rows(g), :], osem.at[slot])

    @pl.when(e == 0)
    def _():
        xbuf[...] = jnp.zeros_like(xbuf)
        _issue_row_gather(rs_ref, nv_ref, 0, h_hbm, xbuf.at[0], xsem.at[0])

    @pl.when(nb > 0)
    def _():
        wg_sc[...] = wg_ref[...].astype(BF16)
        wu_sc[...] = wu_ref[...].astype(BF16)

        def block(j, carry):
            g = g0 + j
            slot = g % 2

            @pl.when(g >= 2)
            def _():
                out_copy(g - 2, slot).wait()

            _wait_row_gather(nv_ref, g, h_hbm, xbuf.at[slot], xsem.at[slot])
            _issue_row_gather(rs_ref, nv_ref, g + 1, h_hbm, xbuf.at[1 - slot], xsem.at[1 - slot])
            x = xbuf[slot].astype(BF16)
            gate = jnp.dot(x, wg_sc[...], preferred_element_type=F32)
            up = jnp.dot(x, wu_sc[...], preferred_element_type=F32)
            obuf[slot] = (gate * _sigmoid(gate) * up).astype(BF16)
            out_copy(g, slot).start()
            return carry

        lax.fori_loop(0, nb, block, 0)

    @pl.when(e == N_EXPERTS - 1)
    def _():
        _finish_writes(out_copy, obuf, nu_ref[0])


def _moe_up(blk_start, n_blk, n_used, row_src, n_valid, h1, w_gate, w_up, l):
    wspec = pl.BlockSpec((None, None, D_MODEL, D_EXPERT), lambda e, *_: (l, e, 0, 0))
    grid_spec = pltpu.PrefetchScalarGridSpec(
        num_scalar_prefetch=5,
        grid=(N_EXPERTS,),
        in_specs=[pl.BlockSpec(memory_space=pl.ANY), wspec, wspec],
        out_specs=pl.BlockSpec(memory_space=pl.ANY),
        scratch_shapes=[
            pltpu.VMEM((D_MODEL, D_EXPERT), BF16),
            pltpu.VMEM((D_MODEL, D_EXPERT), BF16),
            pltpu.VMEM((2, EXPERT_BLOCK, D_MODEL), F32),
            pltpu.VMEM((2, EXPERT_BLOCK, D_EXPERT), BF16),
            pltpu.SemaphoreType.DMA((2,)),
            pltpu.SemaphoreType.DMA((2,)),
        ],
    )
    return pl.pallas_call(
        _moe_up_kernel,
        grid_spec=grid_spec,
        out_shape=jax.ShapeDtypeStruct((N_ROWS, D_EXPERT), BF16),
        compiler_params=_cparams(("arbitrary",), vmem=MOE_VMEM_LIMIT),
        name="moe_up",
    )(blk_start, n_blk, n_used, row_src, n_valid, h1, w_gate, w_up)


DOWN_IN_SLOTS = 4


def _moe_down_kernel(bs_ref, nb_ref, nu_ref, x_hbm, wd_ref, y_hbm, wd_sc, xbuf, obuf, xsem, osem):
    e = pl.program_id(0)
    nb = nb_ref[e]
    g0 = bs_ref[e]
    nu = nu_ref[0]

    def in_copy(g):
        s = g % DOWN_IN_SLOTS
        return pltpu.make_async_copy(x_hbm.at[_block_rows(g), :], xbuf.at[s], xsem.at[s])

    def out_copy(g, slot):
        return pltpu.make_async_copy(obuf.at[slot], y_hbm.at[_block_rows(g), :], osem.at[slot])

    @pl.when(e == 0)
    def _():
        for g in range(DOWN_IN_SLOTS - 1):
            @pl.when(g < nu)
            def _():
                in_copy(g).start()

    @pl.when(nb > 0)
    def _():
        wd_sc[...] = wd_ref[...].astype(BF16)

        def block(j, carry):
            g = g0 + j
            slot = g % 2

            @pl.when(g >= 2)
            def _():
                out_copy(g - 2, slot).wait()

            in_copy(g).wait()

            @pl.when(g + DOWN_IN_SLOTS - 1 < nu)
            def _():
                in_copy(g + DOWN_IN_SLOTS - 1).start()

            obuf[slot] = jnp.dot(xbuf[g % DOWN_IN_SLOTS], wd_sc[...], preferred_element_type=F32)
            out_copy(g, slot).start()
            return carry

        lax.fori_loop(0, nb, block, 0)

    @pl.when(e == N_EXPERTS - 1)
    def _():
        _finish_writes(out_copy, obuf, nu)


def _moe_down(blk_start, n_blk, n_used, hmid, w_down, l):
    grid_spec = pltpu.PrefetchScalarGridSpec(
        num_scalar_prefetch=3,
        grid=(N_EXPERTS,),
        in_specs=[
            pl.BlockSpec(memory_space=pl.ANY),
            pl.BlockSpec((None, None, D_EXPERT, D_MODEL), lambda e, *_: (l, e, 0, 0)),
        ],
        out_specs=pl.BlockSpec(memory_space=pl.ANY),
        scratch_shapes=[
            pltpu.VMEM((D_EXPERT, D_MODEL), BF16),
            pltpu.VMEM((DOWN_IN_SLOTS, EXPERT_BLOCK, D_EXPERT), BF16),
            pltpu.VMEM((2, EXPERT_BLOCK, D_MODEL), F32),
            pltpu.SemaphoreType.DMA((DOWN_IN_SLOTS,)),
            pltpu.SemaphoreType.DMA((2,)),
        ],
    )
    return pl.pallas_call(
        _moe_down_kernel,
        grid_spec=grid_spec,
        out_shape=jax.ShapeDtypeStruct((N_ROWS, D_MODEL), F32),
        compiler_params=_cparams(("arbitrary",), vmem=MOE_VMEM_LIMIT),
        name="moe_down",
    )(blk_start, n_blk, n_used, hmid, w_down)


def _combine_kernel(d0_ref, d1_ref, y_hbm, h1_ref, mf_ref, g_ref, b_ref, h2_ref, h2b_ref,
                    buf0, buf1, sem):
    i = pl.program_id(0)
    slot = i % 2

    def issue(tile, s):
        base = tile * LANE

        def body(r, carry):
            _row_copy(y_hbm, d0_ref[base + r], buf0.at[s], r, sem.at[s]).start(priority=0)
            _row_copy(y_hbm, d1_ref[base + r], buf1.at[s], r, sem.at[s]).start(priority=1)
            return carry

        lax.fori_loop(0, LANE, body, 0)

    @pl.when(i == 0)
    def _():
        issue(0, 0)

    @pl.when(i + 1 < pl.num_programs(0))
    def _():
        issue(i + 1, 1 - slot)

    def drain(r, carry):
        _row_copy(y_hbm, 0, buf0.at[slot], r, sem.at[slot]).wait()
        _row_copy(y_hbm, 0, buf1.at[slot], r, sem.at[slot]).wait()
        return carry

    lax.fori_loop(0, LANE, drain, 0)
    wt = mf_ref[...].T
    ffn = wt[:, 0:1] * buf0[slot] + wt[:, 1:2] * buf1[slot]
    y = _post_ln(h1_ref[...], ffn, g_ref[...], b_ref[...], i * LANE)
    h2_ref[...] = y
    h2b_ref[...] = y.astype(BF16)


def _combine_ln2(d0, d1, y_rows, h1, mf, g, b):
    row = lambda i, d0, d1: (i, 0)
    vec = pl.BlockSpec((1, D_MODEL), lambda i, d0, d1: (0, 0))
    grid_spec = pltpu.PrefetchScalarGridSpec(
        num_scalar_prefetch=2,
        grid=(R // LANE,),
        in_specs=[
            pl.BlockSpec(memory_space=pl.ANY),
            pl.BlockSpec((LANE, D_MODEL), row),
            pl.BlockSpec((8, LANE), lambda i, d0, d1: (0, i)),
            vec, vec,
        ],
        out_specs=[pl.BlockSpec((LANE, D_MODEL), row), pl.BlockSpec((LANE, D_MODEL), row)],
        scratch_shapes=[
            pltpu.VMEM((2, LANE, D_MODEL), F32),
            pltpu.VMEM((2, LANE, D_MODEL), F32),
            pltpu.SemaphoreType.DMA((2,)),
        ],
    )
    return pl.pallas_call(
        _combine_kernel,
        grid_spec=grid_spec,
        out_shape=[
            jax.ShapeDtypeStruct((R, D_MODEL), F32),
            jax.ShapeDtypeStruct((R, D_MODEL), BF16),
        ],
        compiler_params=_cparams(("arbitrary",)),
        name="moe_combine_ln2",
    )(d0, d1, y_rows, h1, mf, g.reshape(1, -1), b.reshape(1, -1))


def kernel(x, meta_tokens, ln_in_g, ln_in_b, w_in, fox_f_bias, gla_wa2, gla_ba, gla_norm_g, conv_w, pool_w, pool_scale, gate_b, w_branch, w_out, ln1_g, ln1_b, router_w, router_b, w_gate, w_up, w_down, ln2_g, ln2_b):
    assert x.shape == (1, SEQ, D_MODEL)
    h, hb = _ln_in(x.reshape(SEQ, D_MODEL), meta_tokens, ln_in_g, ln_in_b)
    router_wt = router_w.T.reshape(N_GROUPS, EXPERTS_PER_GROUP, D_MODEL).transpose(1, 0, 2).reshape(
        N_EXPERTS, D_MODEL)
    router_bc = router_b.astype(F32).reshape(N_GROUPS, EXPERTS_PER_GROUP).T.reshape(N_EXPERTS, 1)

    wb_bf = w_branch.astype(BF16)
    wo_bf = w_out.astype(BF16)

    w_all = _transposed_weights(w_in)

    for l in range(DEPTH):
        zf = _matmul_nt(hb, w_all, l, W_FOX, 2 * MIX_W, BF16, TM_PROJ, 512, "proj_fox")
        vt = _matmul_tt(w_all, l, W_FOX + 2 * MIX_W, MIX_W, hb, BF16, TM_PROJ, "proj_fox_vt")
        z = _matmul_nt(hb, w_all, l, W_MIX, N_MIXC, F32, TM_PROJ, 512, "proj_mix")
        zs = _matmul_nt(hb, w_all, l, W_SMALL, 2 * LANE, F32, TM_PROJ, 2 * LANE, "proj_small")

        bias_row = jnp.zeros((1, LANE), F32).at[0, SM_FF:SM_FF + FOX_HEADS].set(fox_f_bias[l])
        c = _fox_gate(zs, bias_row)
        o_a = _fox_attention(zf, vt, c)

        wa2p = jnp.zeros((LANE, GLA_HEADS * GLA_DK), F32).at[SM_GA:SM_GA + GLA_RANK].set(gla_wa2[l])
        o_b = _gla(z, zs, wa2p, gla_ba[l].reshape(1, -1), gla_norm_g[l].reshape(1, -1))

        o_c, o_d = _local_mixers(z, conv_w[l], pool_w[l].astype(BF16), pool_scale[l])

        mix = _merge(hb, o_a, o_b, o_c, o_d, w_all, gate_b[l], wb_bf, wo_bf, l)
        h1, mi, mf, counts = _ln1_route(h, mix, ln1_g[l], ln1_b[l], router_wt, router_bc)

        d0, d1, blk_start, n_blk, n_used, row_src, n_valid = _dispatch_tables(mi, counts[:, 0])
        hmid = _moe_up(blk_start, n_blk, n_used, row_src, n_valid, h1, w_gate, w_up, l)
        y_rows = _moe_down(blk_start, n_blk, n_used, hmid, w_down, l)
        h, hb = _combine_ln2(d0, d1, y_rows, h1, mf, ln2_g[l], ln2_b[l])

    return h[PAD_ROWS + N_META:].reshape(1, SEQ, D_MODEL)
```

```python
import jax
import jax.numpy as jnp
import numpy as np
from jax import lax
from jax.experimental import pallas as pl
from jax.experimental.pallas import tpu as pltpu

F32 = jnp.float32
BF16 = jnp.bfloat16
I32 = jnp.int32
HIGHEST = lax.Precision.HIGHEST

D_MODEL = 2048
SEQ = 8192
DEPTH = 2
N_META = 16
N_BRANCH = 4
MIX_W = 512
FOX_HEADS = 4
FOX_HD = 128
GLA_HEADS = 4
GLA_DK = 64
GLA_DV = 128
GLA_RANK = 16
GLA_TAU = 16.0
CONV_K = 3
POOL_WINDOWS = (2, 4, 8, 16)
POOL_GW = 128
N_EXPERTS = 32
N_GROUPS = 8
EXPERTS_PER_GROUP = 4
TOP_K = 2
D_EXPERT = 1024
LN_EPS = 1e-5
DEEPNORM_ALPHA = (2 * DEPTH) ** 0.25

_SPLITS = (512, 512, 512, 4, 256, 256, 512, 16, 512, 512, 512, 512, 512, 8192)
_OFFS = [int(o) for o in np.concatenate([[0], np.cumsum(_SPLITS)])]
(O_FQ, O_FK, O_FV, O_FF, O_GQ, O_GK, O_GV, O_GA, O_GR, O_CB, O_CC, O_CV, O_PZ, O_GZ, P_IN) = _OFFS

LANE = 128
PAD_ROWS = LANE - N_META
T0 = PAD_ROWS
N_TOK = N_META + SEQ
R = PAD_ROWS + N_TOK
TM = 640
TM_PROJ = 1664
HALO = 16

WT_TILE = 128
W_FOX, W_MIX, W_GATES = 0, 1536, 5120
N_FOX, N_MIXC, N_GATES = 1536, 3584, 8192
W_SMALL = W_GATES + N_GATES
N_WT = W_SMALL + 2 * LANE
C_GQ, C_GK, C_GV, C_GR, C_CB, C_CC, C_CV, C_PZ = 0, 256, 512, 1024, 1536, 2048, 2560, 3072
SM_FF_TILE, SM_GA_TILE = O_FF // LANE, O_GA // LANE
SM_FF = O_FF - SM_FF_TILE * LANE
SM_GA = O_GA - SM_GA_TILE * LANE


def _wt_sources():
    src = []
    for r in range(0, W_SMALL, WT_TILE):
        if r < N_FOX:
            src.append(O_FQ + r)
        elif r < W_MIX + 1024:
            src.append(O_GQ + r - W_MIX)
        else:
            src.append(O_GR + r - (W_MIX + 1024))
    src += [SM_FF_TILE * LANE, SM_GA_TILE * LANE]
    assert all(0 <= s and s + WT_TILE <= P_IN for s in src)
    return src


WT_SRC = _wt_sources()

EXPERT_BLOCK = 128
N_FLAT = N_TOK * TOP_K
N_BLOCKS = -(-N_FLAT // EXPERT_BLOCK) + N_EXPERTS
N_ROWS = N_BLOCKS * EXPERT_BLOCK

NEG = -1e30
VMEM_LIMIT = 48 * 1024 * 1024
BIG_VMEM_LIMIT = 56 * 1024 * 1024
MOE_VMEM_LIMIT = BIG_VMEM_LIMIT


def _cparams(sem, vmem=VMEM_LIMIT):
    return pltpu.CompilerParams(dimension_semantics=sem, vmem_limit_bytes=vmem)


def _log_sigmoid(x):
    return jnp.minimum(x, 0.0) - jnp.log1p(jnp.exp(-jnp.abs(x)))


def _sigmoid(x):
    return 1.0 / (1.0 + jnp.exp(-x))


def _layer_norm_rows(x, g, b):
    mu = jnp.mean(x, axis=-1, keepdims=True)
    xc = x - mu
    var = jnp.mean(xc * xc, axis=-1, keepdims=True)
    return xc * lax.rsqrt(var + LN_EPS) * g + b


def _ln_in_kernel(x_ref, meta_ref, g_ref, b_ref, h_ref, hb_ref):
    i = pl.program_id(0)

    @pl.when(i == 0)
    def _():
        h_ref[...] = jnp.zeros_like(h_ref)
        hb_ref[...] = jnp.zeros_like(hb_ref)
        m = _layer_norm_rows(meta_ref[...], g_ref[...], b_ref[...])
        h_ref[PAD_ROWS:, :] = m
        hb_ref[PAD_ROWS:, :] = m.astype(BF16)

    @pl.when(i > 0)
    def _():
        y = _layer_norm_rows(x_ref[...], g_ref[...], b_ref[...])
        h_ref[...] = y
        hb_ref[...] = y.astype(BF16)


def _ln_in(x2d, meta, g, b):
    nb = R // LANE
    return pl.pallas_call(
        _ln_in_kernel,
        grid=(nb,),
        in_specs=[
            pl.BlockSpec((LANE, D_MODEL), lambda i: (jnp.maximum(i - 1, 0), 0)),
            pl.BlockSpec((N_META, D_MODEL), lambda i: (0, 0)),
            pl.BlockSpec((1, D_MODEL), lambda i: (0, 0)),
            pl.BlockSpec((1, D_MODEL), lambda i: (0, 0)),
        ],
        out_specs=[
            pl.BlockSpec((LANE, D_MODEL), lambda i: (i, 0)),
            pl.BlockSpec((LANE, D_MODEL), lambda i: (i, 0)),
        ],
        out_shape=[
            jax.ShapeDtypeStruct((R, D_MODEL), F32),
            jax.ShapeDtypeStruct((R, D_MODEL), BF16),
        ],
        compiler_params=_cparams(("arbitrary",)),
        name="ln_in",
    )(x2d, meta, g.reshape(1, -1), b.reshape(1, -1))


D_CHUNKS = D_MODEL // LANE
FLAT_ROWS = D_CHUNKS * DEPTH


def _wt_kernel(src_ref, w_hbm, o_ref, buf, sem):
    j = pl.program_id(0)
    slot = j % 2

    def tile_copy(t, s):
        row0 = pl.multiple_of(src_ref[t] * FLAT_ROWS, FLAT_ROWS)
        return pltpu.make_async_copy(w_hbm.at[pl.ds(row0, WT_TILE * FLAT_ROWS), :], buf.at[s],
                                     sem.at[s])

    @pl.when(j == 0)
    def _():
        tile_copy(0, 0).start()

    @pl.when(j + 1 < pl.num_programs(0))
    def _():
        tile_copy(j + 1, 1 - slot).start()

    tile_copy(j, slot).wait()
    for l in range(DEPTH):
        for c in range(D_CHUNKS):
            o_ref[l, :, c * LANE:(c + 1) * LANE] = buf[
                slot, pl.ds(c * DEPTH + l, WT_TILE, stride=FLAT_ROWS), :].astype(BF16)


def _transposed_weights(w_in):
    flat = w_in.reshape(DEPTH, D_CHUNKS, LANE, P_IN).transpose(3, 1, 0, 2).reshape(
        P_IN * FLAT_ROWS, LANE)
    grid_spec = pltpu.PrefetchScalarGridSpec(
        num_scalar_prefetch=1,
        grid=(N_WT // WT_TILE,),
        in_specs=[pl.BlockSpec(memory_space=pl.ANY)],
        out_specs=pl.BlockSpec((DEPTH, WT_TILE, D_MODEL), lambda j, src: (0, j, 0)),
        scratch_shapes=[
            pltpu.VMEM((2, WT_TILE * FLAT_ROWS, LANE), F32),
            pltpu.SemaphoreType.DMA((2,)),
        ],
    )
    return pl.pallas_call(
        _wt_kernel,
        grid_spec=grid_spec,
        out_shape=jax.ShapeDtypeStruct((DEPTH, N_WT, D_MODEL), BF16),
        compiler_params=_cparams(("arbitrary",)),
        name="transposed_weights",
    )(jnp.asarray(WT_SRC, I32), flat)


_NT = (((1,), (1,)), ((), ()))


def _mm_nt_kernel(a_ref, wt_ref, o_ref):
    o_ref[...] = lax.dot_general(a_ref[...], wt_ref[...], _NT,
                                 preferred_element_type=F32).astype(o_ref.dtype)


def _matmul_nt(a, wt, l, row0, n, out_dtype, tm, tn, name):
    m, k = a.shape
    return pl.pallas_call(
        _mm_nt_kernel,
        grid=(m // tm, n // tn),
        in_specs=[
            pl.BlockSpec((tm, k), lambda i, j: (i, 0)),
            pl.BlockSpec((None, tn, k), lambda i, j: (l, row0 // tn + j, 0)),
        ],
        out_specs=pl.BlockSpec((tm, tn), lambda i, j: (i, j)),
        out_shape=jax.ShapeDtypeStruct((m, n), out_dtype),
        compiler_params=_cparams(("parallel", "arbitrary")),
        name=name,
    )(a, wt)


def _matmul_tt(wt, l, row0, n, a, out_dtype, tm, name):
    m, k = a.shape
    return pl.pallas_call(
        _mm_nt_kernel,
        grid=(m // tm,),
        in_specs=[
            pl.BlockSpec((None, n, k), lambda i: (l, row0 // n, 0)),
            pl.BlockSpec((tm, k), lambda i: (i, 0)),
        ],
        out_specs=pl.BlockSpec((n, tm), lambda i: (0, i)),
        out_shape=jax.ShapeDtypeStruct((n, m), out_dtype),
        compiler_params=_cparams(("parallel",)),
        name=name,
    )(wt, a)


def _fox_gate_kernel(zs_ref, bias_ref, c_ref, carry_ref):
    i = pl.program_id(0)

    @pl.when(i == 0)
    def _():
        carry_ref[...] = jnp.zeros_like(carry_ref)

    rows = i * TM + lax.broadcasted_iota(I32, (TM, LANE), 0)
    lf = _log_sigmoid(zs_ref[...] + bias_ref[...])
    lf = jnp.where(rows >= T0, lf, 0.0)
    tri = (lax.broadcasted_iota(I32, (TM, TM), 0)
           >= lax.broadcasted_iota(I32, (TM, TM), 1)).astype(F32)
    c = jnp.dot(tri, lf, precision=HIGHEST, preferred_element_type=F32) + carry_ref[...]
    c_ref[...] = c
    carry_ref[...] = c[TM - 1:TM, :]


def _fox_gate(zs, bias_row):
    return pl.pallas_call(
        _fox_gate_kernel,
        grid=(R // TM,),
        in_specs=[
            pl.BlockSpec((TM, LANE), lambda i: (i, 0)),
            pl.BlockSpec((1, LANE), lambda i: (0, 0)),
        ],
        out_specs=pl.BlockSpec((TM, LANE), lambda i: (i, 0)),
        out_shape=jax.ShapeDtypeStruct((R, LANE), F32),
        scratch_shapes=[pltpu.VMEM((1, LANE), F32)],
        compiler_params=_cparams(("arbitrary",)),
        name="fox_gate",
    )(zs, bias_row)


TQ = TM
N_QB = R // TQ
_PAIRS = [(qi, kj) for qi in range(N_QB) for kj in range(qi + 1)]
N_PAIRS = len(_PAIRS)


LOG2E = 1.4426950408889634
FOX_HPS = FOX_HEADS


def _fox_kernel(qi_tab, kj_tab, q_ref, k_ref, vt_ref, ck_ref, o_ref, m_sc, l_sc, acc_sc):
    p = pl.program_id(1)
    qi = qi_tab[p]
    kj = kj_tab[p]

    @pl.when(kj == 0)
    def _():
        m_sc[...] = jnp.full_like(m_sc, NEG)
        l_sc[...] = jnp.zeros_like(l_sc)
        acc_sc[...] = jnp.zeros_like(acc_sc)

    kpos = kj * TQ + lax.broadcasted_iota(I32, (TQ, 1), 0)
    c1 = FOX_HD ** -0.5 * LOG2E

    def step(causal):
        for hh in range(FOX_HPS):
            lanes = slice(hh * FOX_HD, (hh + 1) * FOX_HD)
            ck = ck_ref[:, SM_FF + hh:SM_FF + hh + 1]
            ckl = jnp.where(kpos >= T0, ck * LOG2E, -NEG)
            t = lax.dot_general(k_ref[:, lanes], q_ref[:, lanes], (((1,), (1,)), ((), ())),
                                preferred_element_type=F32) * c1 - ckl
            if causal:
                ahead = (lax.broadcasted_iota(I32, (TQ, TQ), 0)
                         - lax.broadcasted_iota(I32, (TQ, TQ), 1))
                t = jnp.where(ahead <= 0, t, NEG)
            m_prev = m_sc[hh]
            m_new = jnp.maximum(m_prev, jnp.max(t, axis=0, keepdims=True))
            alpha = jnp.exp2(m_prev - m_new)
            pr = jnp.exp2(t - m_new)
            l_sc[hh] = alpha * l_sc[hh] + jnp.sum(pr, axis=0, keepdims=True)
            acc_sc[hh] = alpha * acc_sc[hh] + jnp.dot(vt_ref[lanes, :], pr.astype(BF16),
                                                      preferred_element_type=F32)
            m_sc[hh] = m_new

    @pl.when(kj == qi)
    def _():
        step(True)

    @pl.when(kj != qi)
    def _():
        step(False)

    @pl.when(kj == qi)
    def _():
        for hh in range(FOX_HPS):
            o_ref[:, hh * FOX_HD:(hh + 1) * FOX_HD] = (acc_sc[hh] / l_sc[hh]).T.astype(o_ref.dtype)


def _fox_attention(zf, vt, c_col):
    qi_tab = jnp.asarray([p[0] for p in _PAIRS], I32)
    kj_tab = jnp.asarray([p[1] for p in _PAIRS], I32)
    hw = FOX_HPS * FOX_HD
    grid_spec = pltpu.PrefetchScalarGridSpec(
        num_scalar_prefetch=2,
        grid=(FOX_HEADS // FOX_HPS, N_PAIRS),
        in_specs=[
            pl.BlockSpec((TQ, hw), lambda h, p, qt, kt: (qt[p], h)),
            pl.BlockSpec((TQ, hw), lambda h, p, qt, kt: (kt[p], FOX_HEADS // FOX_HPS + h)),
            pl.BlockSpec((hw, TQ), lambda h, p, qt, kt: (h, kt[p])),
            pl.BlockSpec((TQ, LANE), lambda h, p, qt, kt: (kt[p], 0)),
        ],
        out_specs=pl.BlockSpec((TQ, hw), lambda h, p, qt, kt: (qt[p], h)),
        scratch_shapes=[
            pltpu.VMEM((FOX_HPS, 1, TQ), F32),
            pltpu.VMEM((FOX_HPS, 1, TQ), F32),
            pltpu.VMEM((FOX_HPS, FOX_HD, TQ), F32),
        ],
    )
    return pl.pallas_call(
        _fox_kernel,
        grid_spec=grid_spec,
        out_shape=jax.ShapeDtypeStruct((R, MIX_W), BF16),
        compiler_params=_cparams(("parallel", "arbitrary")),
        name="fox_attention",
    )(qi_tab, kj_tab, zf, zf, vt, c_col)


GLA_CHUNK = 64
GLA_UNROLL = 5


def _gla_kernel(q_ref, k_ref, v_ref, gr_ref, zs_ref, wa2_ref, ba_ref, gn_ref, o_ref, st_ref, la_ref):
    i = pl.program_id(0)

    @pl.when(i == 0)
    def _():
        st_ref[...] = jnp.zeros_like(st_ref)

    la = jnp.dot(zs_ref[...], wa2_ref[...], precision=HIGHEST, preferred_element_type=F32)
    la_ref[...] = _log_sigmoid(la + ba_ref[...]) * (1.0 / GLA_TAU)

    c_r = lax.broadcasted_iota(I32, (GLA_CHUNK, GLA_CHUNK), 0)
    c_c = lax.broadcasted_iota(I32, (GLA_CHUNK, GLA_CHUNK), 1)
    tri_b = c_r >= c_c
    tri = jnp.where(tri_b, 1.0, 0.0).astype(BF16)

    def chunk(c, carry):
        r0 = pl.multiple_of(c * GLA_CHUNK, GLA_CHUNK)
        rows = pl.ds(r0, GLA_CHUNK)
        g = la_ref[rows, :]
        g_hi = g.astype(BF16)
        g_lo = (g - g_hi.astype(F32)).astype(BF16)
        b = (jnp.dot(tri, g_hi, preferred_element_type=F32)
             + jnp.dot(tri, g_lo, preferred_element_type=F32))
        b_last = b[GLA_CHUNK - 1:GLA_CHUNK, :]
        e_last = jnp.exp(b_last)
        qt = q_ref[rows, :] * (GLA_DK ** -0.5) * jnp.exp(b)
        kt = k_ref[rows, :] * jnp.exp(-b)
        kh = kt * e_last
        for h in range(GLA_HEADS):
            ks = slice(h * GLA_DK, (h + 1) * GLA_DK)
            vs = slice(h * GLA_DV, (h + 1) * GLA_DV)
            q_h = qt[:, ks].astype(BF16)
            k_h = kt[:, ks].astype(BF16)
            kh_h = kh[:, ks].astype(BF16)
            v_h = v_ref[rows, vs]
            att = lax.dot_general(q_h, k_h, (((1,), (1,)), ((), ())), preferred_element_type=F32)
            att = jnp.where(tri_b, att, 0.0)
            st = st_ref[h]
            o = jnp.dot(att.astype(BF16), v_h.astype(BF16), preferred_element_type=F32)
            o = o + lax.dot_general(q_h, st.astype(BF16), (((1,), (1,)), ((), ())),
                                    preferred_element_type=F32)
            st_ref[h] = st * e_last[:, ks] + jnp.dot(v_h.T.astype(BF16), kh_h,
                                                     preferred_element_type=F32)
            ms = jnp.mean(o * o, axis=-1, keepdims=True)
            on = o * lax.rsqrt(ms + LN_EPS) * gn_ref[:, vs]
            gate = gr_ref[rows, vs]
            o_ref[rows, vs] = (on * (gate * _sigmoid(gate))).astype(o_ref.dtype)
        return carry

    lax.fori_loop(0, TM // GLA_CHUNK, chunk, 0, unroll=GLA_UNROLL)


def _gla(z, zs, wa2p, ba, gn):
    return pl.pallas_call(
        _gla_kernel,
        grid=(R // TM,),
        in_specs=[
            pl.BlockSpec((TM, 256), lambda i: (i, C_GQ // 256)),
            pl.BlockSpec((TM, 256), lambda i: (i, C_GK // 256)),
            pl.BlockSpec((TM, 512), lambda i: (i, C_GV // 512)),
            pl.BlockSpec((TM, 512), lambda i: (i, C_GR // 512)),
            pl.BlockSpec((TM, LANE), lambda i: (i, 1)),
            pl.BlockSpec((LANE, 256), lambda i: (0, 0)),
            pl.BlockSpec((1, 256), lambda i: (0, 0)),
            pl.BlockSpec((1, 512), lambda i: (0, 0)),
        ],
        out_specs=pl.BlockSpec((TM, MIX_W), lambda i: (i, 0)),
        out_shape=jax.ShapeDtypeStruct((R, MIX_W), BF16),
        scratch_shapes=[
            pltpu.VMEM((GLA_HEADS, GLA_DV, GLA_DK), F32),
            pltpu.VMEM((TM, GLA_HEADS * GLA_DK), F32),
        ],
        compiler_params=_cparams(("arbitrary",)),
        name="gla",
    )(z, z, z, z, zs, wa2p, ba, gn)


def _local_kernel(cb_ref, cc_ref, cv_ref, pz_ref, cw_ref, pw_ref, ps_ref, oc_ref, od_ref, u_sc, p_sc):
    i = pl.program_id(0)

    @pl.when(i == 0)
    def _():
        u_sc[0:HALO, :] = jnp.zeros((HALO, MIX_W), F32)
        p_sc[0:HALO, :] = jnp.zeros((HALO, MIX_W), F32)

    @pl.when(i > 0)
    def _():
        u_sc[0:HALO, :] = u_sc[TM:TM + HALO, :]
        p_sc[0:HALO, :] = p_sc[TM:TM + HALO, :]

    u = cc_ref[...] * cv_ref[...]
    pz = pz_ref[...]
    u_sc[HALO:, :] = u
    p_sc[HALO:, :] = pz

    y = (cw_ref[2:3, :] * u + cw_ref[1:2, :] * u_sc[HALO - 1:HALO - 1 + TM, :]
         + cw_ref[0:1, :] * u_sc[HALO - 2:HALO - 2 + TM, :])
    oc_ref[...] = (cb_ref[...] * y).astype(oc_ref.dtype)

    tok = i * TM - T0 + lax.broadcasted_iota(I32, (TM, 1), 0)
    cnt_small = jnp.maximum(tok + 1, 1).astype(F32)
    for g, w in enumerate(POOL_WINDOWS):
        cols = slice(g * POOL_GW, (g + 1) * POOL_GW)
        x = pz[:, cols]
        s = x
        for j in range(1, w):
            s = s + p_sc[HALO - j:HALO - j + TM, cols]
        inv_cnt = jnp.where(tok + 1 >= w, 1.0 / w, 1.0 / cnt_small)
        pooled = s * inv_cnt - x
        od = jnp.dot(pooled.astype(BF16), pw_ref[g], preferred_element_type=F32)
        od_ref[:, cols] = (od * ps_ref[:, cols]).astype(od_ref.dtype)


def _local_mixers(z, conv_w, pool_w_bf, pool_scale):
    cw = jnp.zeros((8, MIX_W), F32).at[:CONV_K].set(conv_w)
    blk = lambda c: pl.BlockSpec((TM, MIX_W), lambda i, c=c: (i, c // MIX_W))
    return pl.pallas_call(
        _local_kernel,
        grid=(R // TM,),
        in_specs=[
            blk(C_CB), blk(C_CC), blk(C_CV), blk(C_PZ),
            pl.BlockSpec((8, MIX_W), lambda i: (0, 0)),
            pl.BlockSpec((len(POOL_WINDOWS), POOL_GW, POOL_GW), lambda i: (0, 0, 0)),
            pl.BlockSpec((1, MIX_W), lambda i: (0, 0)),
        ],
        out_specs=[
            pl.BlockSpec((TM, MIX_W), lambda i: (i, 0)),
            pl.BlockSpec((TM, MIX_W), lambda i: (i, 0)),
        ],
        out_shape=[
            jax.ShapeDtypeStruct((R, MIX_W), BF16),
            jax.ShapeDtypeStruct((R, MIX_W), BF16),
        ],
        scratch_shapes=[
            pltpu.VMEM((TM + HALO, MIX_W), F32),
            pltpu.VMEM((TM + HALO, MIX_W), F32),
        ],
        compiler_params=_cparams(("arbitrary",)),
        name="conv_pool",
    )(z, z, z, z, cw, pool_w_bf, pool_scale.reshape(1, -1))


TN_MERGE = 512


def _merge_kernel(hb_ref, oa_ref, ob_ref, oc_ref, od_ref, wg0_ref, wg1_ref, wg2_ref, wg3_ref,
                  gb_ref, wb_ref, wo_ref, out_ref):
    j = pl.program_id(1)

    @pl.when(j == 0)
    def _():
        out_ref[...] = jnp.zeros_like(out_ref)

    hb = hb_ref[...]
    mixed = None
    for b, (o_ref, wg_ref) in enumerate(((oa_ref, wg0_ref), (ob_ref, wg1_ref),
                                         (oc_ref, wg2_ref), (od_ref, wg3_ref))):
        gate = _sigmoid(lax.dot_general(hb, wg_ref[...], _NT, preferred_element_type=F32)
                        + gb_ref[b:b + 1, :])
        proj = jnp.dot(o_ref[...], wb_ref[b], preferred_element_type=F32)
        term = gate * proj
        mixed = term if mixed is None else mixed + term
    out_ref[...] += jnp.dot(mixed.astype(BF16), wo_ref[...], preferred_element_type=F32)


def _merge(hb, o_a, o_b, o_c, o_d, w_all, gate_b, wb_bf, wo_bf, l):
    tn = TN_MERGE
    nj = D_MODEL // tn
    row = lambda w: pl.BlockSpec((TM, w), lambda i, j: (i, 0))
    wg = lambda b: pl.BlockSpec((None, tn, D_MODEL),
                                lambda i, j, b=b: (l, W_GATES // tn + b * nj + j, 0))
    return pl.pallas_call(
        _merge_kernel,
        grid=(R // TM, nj),
        in_specs=[
            row(D_MODEL), row(MIX_W), row(MIX_W), row(MIX_W), row(MIX_W),
            wg(0), wg(1), wg(2), wg(3),
            pl.BlockSpec((N_BRANCH, tn), lambda i, j: (0, j)),
            pl.BlockSpec((None, N_BRANCH, MIX_W, tn), lambda i, j: (l, 0, 0, j)),
            pl.BlockSpec((None, tn, D_MODEL), lambda i, j: (l, j, 0)),
        ],
        out_specs=pl.BlockSpec((TM, D_MODEL), lambda i, j: (i, 0)),
        out_shape=jax.ShapeDtypeStruct((R, D_MODEL), F32),
        compiler_params=_cparams(("parallel", "arbitrary"), vmem=BIG_VMEM_LIMIT),
        name="merge",
    )(hb, o_a, o_b, o_c, o_d, w_all, w_all, w_all, w_all, gate_b, wb_bf, wo_bf)


def _post_ln(h, delta, g, b, row0):
    y = _layer_norm_rows(DEEPNORM_ALPHA * h + delta, g, b)
    rows = row0 + lax.broadcasted_iota(I32, (y.shape[0], 1), 0)
    return jnp.where(rows >= T0, y, 0.0)


def _first_of(cands, target):
    idx = jnp.full(target.shape, len(cands) - 1, I32)
    for j in range(len(cands) - 2, -1, -1):
        idx = jnp.where(cands[j] == target, j, idx)
    return idx


def _pick(cands, idx):
    out = cands[-1]
    for j in range(len(cands) - 2, -1, -1):
        out = jnp.where(idx == j, cands[j], out)
    return out


def _ln1_route_kernel(h_ref, mix_ref, g_ref, b_ref, rwt_ref, rb_ref,
                      h1_ref, mi_ref, mf_ref, cnt_ref, carry_sc):
    i = pl.program_id(0)

    @pl.when(i == 0)
    def _():
        carry_sc[...] = jnp.zeros_like(carry_sc)

    y = _post_ln(h_ref[...], mix_ref[...], g_ref[...], b_ref[...], i * TM)
    h1_ref[...] = y

    logits = lax.dot_general(rwt_ref[...], y, (((1,), (1,)), ((), ())), precision=HIGHEST,
                             preferred_element_type=F32)
    aff = _sigmoid(logits)
    sel = aff + rb_ref[...]
    xs = [sel[j * N_GROUPS:(j + 1) * N_GROUPS, :] for j in range(EXPERTS_PER_GROUP)]
    afs = [aff[j * N_GROUPS:(j + 1) * N_GROUPS, :] for j in range(EXPERTS_PER_GROUP)]

    score = None
    for a in range(EXPERTS_PER_GROUP):
        for bb in range(a + 1, EXPERTS_PER_GROUP):
            pair = xs[a] + xs[bb]
            score = pair if score is None else jnp.maximum(score, pair)
    giota = lax.broadcasted_iota(I32, (N_GROUPS, TM), 0)
    gmax = jnp.max(score, axis=0, keepdims=True)
    grp = jnp.min(jnp.where(score == gmax, giota, N_GROUPS), axis=0, keepdims=True)
    gsel = giota == grp
    cs = [jnp.max(jnp.where(gsel, x, -jnp.inf), axis=0, keepdims=True) for x in xs]
    acs = [jnp.sum(jnp.where(gsel, a, 0.0), axis=0, keepdims=True) for a in afs]

    m1 = jnp.maximum(jnp.maximum(cs[0], cs[1]), jnp.maximum(cs[2], cs[3]))
    i0 = _first_of(cs, m1)
    ds = [jnp.where(i0 == j, -jnp.inf, cs[j]) for j in range(EXPERTS_PER_GROUP)]
    m2 = jnp.maximum(jnp.maximum(ds[0], ds[1]), jnp.maximum(ds[2], ds[3]))
    i1 = _first_of(ds, m2)
    a0 = _pick(acs, i0)
    a1 = _pick(acs, i1)
    denom = a0 + a1

    pos = i * TM + lax.broadcasted_iota(I32, (1, TM), 1)
    valid = pos >= T0
    riota = lax.broadcasted_iota(I32, (N_EXPERTS, TM), 0)
    oh0 = (riota == i0 * N_GROUPS + grp) & valid
    oh1 = (riota == i1 * N_GROUPS + grp) & valid
    ohf = jnp.where(oh0 | oh1, 1.0, 0.0)
    before = (lax.broadcasted_iota(I32, (TM, TM), 0)
              < lax.broadcasted_iota(I32, (TM, TM), 1)).astype(BF16)
    cum = jnp.dot(ohf.astype(BF16), before, preferred_element_type=F32) + carry_sc[...]
    rank0 = jnp.sum(jnp.where(oh0, cum, 0.0), axis=0, keepdims=True)
    rank1 = jnp.sum(jnp.where(oh1, cum, 0.0), axis=0, keepdims=True)
    carry = carry_sc[...] + jnp.sum(ohf, axis=1, keepdims=True)
    carry_sc[...] = carry
    cnt_ref[...] = jnp.broadcast_to(carry, cnt_ref.shape)

    zi = jnp.zeros((1, TM), I32)
    mi_ref[...] = jnp.concatenate(
        [grp * EXPERTS_PER_GROUP + i0, grp * EXPERTS_PER_GROUP + i1,
         rank0.astype(I32), rank1.astype(I32), zi, zi, zi, zi], axis=0)
    zf = jnp.zeros((1, TM), F32)
    mf_ref[...] = jnp.concatenate([a0 / denom, a1 / denom, zf, zf, zf, zf, zf, zf], axis=0)


def _ln1_route(h, mix, g, b, router_wt, router_bc):
    row = pl.BlockSpec((TM, D_MODEL), lambda i: (i, 0))
    vec = pl.BlockSpec((1, D_MODEL), lambda i: (0, 0))
    meta = pl.BlockSpec((8, TM), lambda i: (0, i))
    return pl.pallas_call(
        _ln1_route_kernel,
        grid=(R // TM,),
        in_specs=[row, row, vec, vec,
                  pl.BlockSpec((N_EXPERTS, D_MODEL), lambda i: (0, 0)),
                  pl.BlockSpec((N_EXPERTS, 1), lambda i: (0, 0))],
        out_specs=[row, meta, meta, pl.BlockSpec((N_EXPERTS, LANE), lambda i: (0, 0))],
        out_shape=[
            jax.ShapeDtypeStruct((R, D_MODEL), F32),
            jax.ShapeDtypeStruct((8, R), I32),
            jax.ShapeDtypeStruct((8, R), F32),
            jax.ShapeDtypeStruct((N_EXPERTS, LANE), F32),
        ],
        scratch_shapes=[pltpu.VMEM((N_EXPERTS, 1), F32)],
        compiler_params=_cparams(("arbitrary",)),
        name="ln1_route",
    )(h, mix, g.reshape(1, -1), b.reshape(1, -1), router_wt, router_bc)


def _dispatch_tables(mi, counts_slot_major):
    counts = counts_slot_major.reshape(EXPERTS_PER_GROUP, N_GROUPS).T.reshape(N_EXPERTS).astype(I32)
    padded = (counts + EXPERT_BLOCK - 1) // EXPERT_BLOCK * EXPERT_BLOCK
    pad_end = jnp.cumsum(padded)
    pad_start = pad_end - padded
    e_iota = jnp.arange(N_EXPERTS, dtype=I32)
    rows_ok = jnp.arange(R) >= T0

    def dest(eid, rank):
        start = jnp.sum(jnp.where(eid[:, None] == e_iota[None, :], pad_start[None, :], 0), axis=1)
        return jnp.where(rows_ok, start + rank, 0).astype(I32)

    d0 = dest(mi[0], mi[2])
    d1 = dest(mi[1], mi[3])
    blk_start = (pad_start // EXPERT_BLOCK).astype(I32)
    n_blk = (padded // EXPERT_BLOCK).astype(I32)
    n_used = (pad_end[-1] // EXPERT_BLOCK).astype(I32).reshape(1)
    tok_rows = jnp.arange(T0, R, dtype=I32)
    row_src = jnp.zeros((N_ROWS + EXPERT_BLOCK,), I32).at[jnp.concatenate([d0[T0:], d1[T0:]])].set(
        jnp.concatenate([tok_rows, tok_rows]), unique_indices=True)
    blk = jnp.arange(N_BLOCKS + 1, dtype=I32)
    owner = (blk[:, None] >= blk_start[None, :]) & (blk[:, None] < (blk_start + n_blk)[None, :])
    left = counts[None, :] - (blk[:, None] - blk_start[None, :]) * EXPERT_BLOCK
    n_valid = jnp.sum(jnp.where(owner, jnp.clip(left, 0, EXPERT_BLOCK), 0), axis=1).astype(I32)
    return d0, d1, blk_start, n_blk, n_used, row_src, n_valid


def _row_copy(src, src_row, dst, dst_row, sem):
    return pltpu.make_async_copy(src.at[pl.ds(src_row, 1), :], dst.at[pl.ds(dst_row, 1), :], sem)


def _issue_row_gather(rs_ref, nv_ref, g, h_hbm, buf, sem, full=False):
    base = g * EXPERT_BLOCK
    nv = nv_ref[g]
    for r in range(EXPERT_BLOCK):
        if full:
            _row_copy(h_hbm, rs_ref[base + r], buf, r, sem).start(priority=1)
        else:
            @pl.when(r < nv)
            def _():
                _row_copy(h_hbm, rs_ref[base + r], buf, r, sem).start(priority=1)


def _wait_row_gather(nv_ref, g, h_hbm, buf, sem, full=False):
    if full:
        pltpu.make_async_copy(h_hbm.at[pl.ds(0, EXPERT_BLOCK), :], buf, sem).wait()
        return
    nv = nv_ref[g]
    k = EXPERT_BLOCK
    while k >= 1:
        @pl.when((nv & k) != 0)
        def _():
            pltpu.make_async_copy(h_hbm.at[pl.ds(0, k), :], buf.at[pl.ds(0, k), :], sem).wait()
        k //= 2


def _block_rows(g):
    return pl.ds(pl.multiple_of(g * EXPERT_BLOCK, EXPERT_BLOCK), EXPERT_BLOCK)


def _finish_writes(out_copy, obuf, nu):
    n_slots = obuf.shape[0]
    for back in range(1, n_slots + 1):
        @pl.when(nu >= back)
        def _():
            out_copy(nu - back, (nu - back) % n_slots).wait()

    obuf[0] = jnp.zeros(obuf.shape[1:], obuf.dtype)

    def zero_block(g, carry):
        cp = out_copy(g, 0)
        cp.start()
        cp.wait()
        return carry

    lax.fori_loop(nu, N_BLOCKS, zero_block, 0)


def _moe_up_kernel(bs_ref, nb_ref, nu_ref, rs_ref, nv_ref, h_hbm, wg_ref, wu_ref, o_hbm,
                   wg_sc, wu_sc, xbuf, obuf, xsem, osem):
    e = pl.program_id(0)
    nb = nb_ref[e]
    g0 = bs_ref[e]

    def out_copy(g, slot):
        return pltpu.make_async_copy(obuf.at[slot], o_hbm.at[_block_rows(g), :], osem.at[slot])

    @pl.when(e == 0)
    def _():
        xbuf[...] = jnp.zeros_like(xbuf)
        _issue_row_gather(rs_ref, nv_ref, 0, h_hbm, xbuf.at[0], xsem.at[0])

    @pl.when(nb > 0)
    def _():
        wg_sc[...] = wg_ref[...].astype(BF16)
        wu_sc[...] = wu_ref[...].astype(BF16)

        def block(j, carry, full):
            g = g0 + j
            slot = g % 2

            @pl.when(g >= 2)
            def _():
                out_copy(g - 2, slot).wait()

            _wait_row_gather(nv_ref, g, h_hbm, xbuf.at[slot], xsem.at[slot], full)
            _issue_row_gather(rs_ref, nv_ref, g + 1, h_hbm, xbuf.at[1 - slot], xsem.at[1 - slot],
                              full)
            x = xbuf[slot].astype(BF16)
            gate = jnp.dot(x, wg_sc[...], preferred_element_type=F32)
            up = jnp.dot(x, wu_sc[...], preferred_element_type=F32)
            obuf[slot] = (gate * _sigmoid(gate) * up).astype(BF16)
            out_copy(g, slot).start()
            return carry

        n_full = jnp.maximum(nb - 2, 0)
        lax.fori_loop(0, n_full, lambda j, c: block(j, c, True), 0)
        lax.fori_loop(n_full, nb, lambda j, c: block(j, c, False), 0)

    @pl.when(e == N_EXPERTS - 1)
    def _():
        _finish_writes(out_copy, obuf, nu_ref[0])


def _moe_up(blk_start, n_blk, n_used, row_src, n_valid, h1, w_gate, w_up, l):
    wspec = pl.BlockSpec((None, None, D_MODEL, D_EXPERT), lambda e, *_: (l, e, 0, 0))
    grid_spec = pltpu.PrefetchScalarGridSpec(
        num_scalar_prefetch=5,
        grid=(N_EXPERTS,),
        in_specs=[pl.BlockSpec(memory_space=pl.ANY), wspec, wspec],
        out_specs=pl.BlockSpec(memory_space=pl.ANY),
        scratch_shapes=[
            pltpu.VMEM((D_MODEL, D_EXPERT), BF16),
            pltpu.VMEM((D_MODEL, D_EXPERT), BF16),
            pltpu.VMEM((2, EXPERT_BLOCK, D_MODEL), F32),
            pltpu.VMEM((2, EXPERT_BLOCK, D_EXPERT), BF16),
            pltpu.SemaphoreType.DMA((2,)),
            pltpu.SemaphoreType.DMA((2,)),
        ],
    )
    return pl.pallas_call(
        _moe_up_kernel,
        grid_spec=grid_spec,
        out_shape=jax.ShapeDtypeStruct((N_ROWS, D_EXPERT), BF16),
        compiler_params=_cparams(("arbitrary",), vmem=MOE_VMEM_LIMIT),
        name="moe_up",
    )(blk_start, n_blk, n_used, row_src, n_valid, h1, w_gate, w_up)


DOWN_IN_SLOTS = 4
DOWN_OUT_SLOTS = 3


def _moe_down_kernel(bs_ref, nb_ref, nu_ref, x_hbm, wd_ref, y_hbm, wd_sc, xbuf, obuf, xsem, osem):
    e = pl.program_id(0)
    nb = nb_ref[e]
    g0 = bs_ref[e]
    nu = nu_ref[0]

    def in_copy(g):
        s = g % DOWN_IN_SLOTS
        return pltpu.make_async_copy(x_hbm.at[_block_rows(g), :], xbuf.at[s], xsem.at[s])

    def out_copy(g, slot):
        return pltpu.make_async_copy(obuf.at[slot], y_hbm.at[_block_rows(g), :], osem.at[slot])

    @pl.when(e == 0)
    def _():
        for g in range(DOWN_IN_SLOTS - 1):
            @pl.when(g < nu)
            def _():
                in_copy(g).start()

    @pl.when(nb > 0)
    def _():
        wd_sc[...] = wd_ref[...].astype(BF16)

        def block(j, carry):
            g = g0 + j
            slot = g % DOWN_OUT_SLOTS

            @pl.when(g >= DOWN_OUT_SLOTS)
            def _():
                out_copy(g - DOWN_OUT_SLOTS, slot).wait()

            in_copy(g).wait()

            @pl.when(g + DOWN_IN_SLOTS - 1 < nu)
            def _():
                in_copy(g + DOWN_IN_SLOTS - 1).start()

            obuf[slot] = jnp.dot(xbuf[g % DOWN_IN_SLOTS], wd_sc[...], preferred_element_type=F32)
            out_copy(g, slot).start()
            return carry

        lax.fori_loop(0, nb, block, 0)

    @pl.when(e == N_EXPERTS - 1)
    def _():
        _finish_writes(out_copy, obuf, nu)


def _moe_down(blk_start, n_blk, n_used, hmid, w_down, l):
    grid_spec = pltpu.PrefetchScalarGridSpec(
        num_scalar_prefetch=3,
        grid=(N_EXPERTS,),
        in_specs=[
            pl.BlockSpec(memory_space=pl.ANY),
            pl.BlockSpec((None, None, D_EXPERT, D_MODEL), lambda e, *_: (l, e, 0, 0)),
        ],
        out_specs=pl.BlockSpec(memory_space=pl.ANY),
        scratch_shapes=[
            pltpu.VMEM((D_EXPERT, D_MODEL), BF16),
            pltpu.VMEM((DOWN_IN_SLOTS, EXPERT_BLOCK, D_EXPERT), BF16),
            pltpu.VMEM((DOWN_OUT_SLOTS, EXPERT_BLOCK, D_MODEL), F32),
            pltpu.SemaphoreType.DMA((DOWN_IN_SLOTS,)),
            pltpu.SemaphoreType.DMA((DOWN_OUT_SLOTS,)),
        ],
    )
    return pl.pallas_call(
        _moe_down_kernel,
        grid_spec=grid_spec,
        out_shape=jax.ShapeDtypeStruct((N_ROWS, D_MODEL), F32),
        compiler_params=_cparams(("arbitrary",), vmem=MOE_VMEM_LIMIT),
        name="moe_down",
    )(blk_start, n_blk, n_used, hmid, w_down)


def _combine_kernel(d0_ref, d1_ref, y_hbm, h1_ref, mf_ref, g_ref, b_ref, h2_ref, h2b_ref,
                    buf0, buf1, sem):
    i = pl.program_id(0)
    slot = i % 2

    def issue(tile, s):
        base = tile * LANE

        def body(r, carry):
            _row_copy(y_hbm, d0_ref[base + r], buf0.at[s], r, sem.at[s]).start(priority=0)
            _row_copy(y_hbm, d1_ref[base + r], buf1.at[s], r, sem.at[s]).start(priority=1)
            return carry

        lax.fori_loop(0, LANE, body, 0)

    @pl.when(i == 0)
    def _():
        issue(0, 0)

    @pl.when(i + 1 < pl.num_programs(0))
    def _():
        issue(i + 1, 1 - slot)

    def drain(r, carry):
        _row_copy(y_hbm, 0, buf0.at[slot], r, sem.at[slot]).wait()
        _row_copy(y_hbm, 0, buf1.at[slot], r, sem.at[slot]).wait()
        return carry

    lax.fori_loop(0, LANE, drain, 0)
    wt = mf_ref[...].T
    ffn = wt[:, 0:1] * buf0[slot] + wt[:, 1:2] * buf1[slot]
    y = _post_ln(h1_ref[...], ffn, g_ref[...], b_ref[...], i * LANE)
    h2_ref[...] = y
    if h2b_ref is not None:
        h2b_ref[...] = y.astype(BF16)


def _combine_last_kernel(d0_ref, d1_ref, y_hbm, h1_ref, mf_ref, g_ref, b_ref, out_ref, buf0, buf1, sem):
    _combine_kernel(d0_ref, d1_ref, y_hbm, h1_ref, mf_ref, g_ref, b_ref, out_ref, None, buf0, buf1, sem)


def _combine_ln2(d0, d1, y_rows, h1, mf, g, b, last):
    row = lambda i, d0, d1: (i, 0)
    vec = pl.BlockSpec((1, D_MODEL), lambda i, d0, d1: (0, 0))
    if last:
        frames = lambda i, d0, d1: (jnp.maximum(i - 1, 0), 0)
        out_specs = [pl.BlockSpec((LANE, D_MODEL), frames)]
        out_shape = [jax.ShapeDtypeStruct((SEQ, D_MODEL), F32)]
    else:
        out_specs = [pl.BlockSpec((LANE, D_MODEL), row), pl.BlockSpec((LANE, D_MODEL), row)]
        out_shape = [jax.ShapeDtypeStruct((R, D_MODEL), F32), jax.ShapeDtypeStruct((R, D_MODEL), BF16)]
    grid_spec = pltpu.PrefetchScalarGridSpec(
        num_scalar_prefetch=2,
        grid=(R // LANE,),
        in_specs=[
            pl.BlockSpec(memory_space=pl.ANY),
            pl.BlockSpec((LANE, D_MODEL), row),
            pl.BlockSpec((8, LANE), lambda i, d0, d1: (0, i)),
            vec, vec,
        ],
        out_specs=out_specs,
        scratch_shapes=[
            pltpu.VMEM((2, LANE, D_MODEL), F32),
            pltpu.VMEM((2, LANE, D_MODEL), F32),
            pltpu.SemaphoreType.DMA((2,)),
        ],
    )
    return pl.pallas_call(
        _combine_last_kernel if last else _combine_kernel,
        grid_spec=grid_spec,
        out_shape=out_shape,
        compiler_params=_cparams(("arbitrary",)),
        name="moe_combine_ln2",
    )(d0, d1, y_rows, h1, mf, g.reshape(1, -1), b.reshape(1, -1))


def kernel(x, meta_tokens, ln_in_g, ln_in_b, w_in, fox_f_bias, gla_wa2, gla_ba, gla_norm_g, conv_w, pool_w, pool_scale, gate_b, w_branch, w_out, ln1_g, ln1_b, router_w, router_b, w_gate, w_up, w_down, ln2_g, ln2_b):
    assert x.shape == (1, SEQ, D_MODEL)
    h, hb = _ln_in(x.reshape(SEQ, D_MODEL), meta_tokens, ln_in_g, ln_in_b)
    router_wt = router_w.T.reshape(N_GROUPS, EXPERTS_PER_GROUP, D_MODEL).transpose(1, 0, 2).reshape(
        N_EXPERTS, D_MODEL)
    router_bc = router_b.astype(F32).reshape(N_GROUPS, EXPERTS_PER_GROUP).T.reshape(N_EXPERTS, 1)

    wb_bf = w_branch.astype(BF16)
    wo_bf = w_out.astype(BF16)

    w_all = _transposed_weights(w_in)

    for l in range(DEPTH):
        zf = _matmul_nt(hb, w_all, l, W_FOX, 2 * MIX_W, BF16, TM_PROJ, 512, "proj_fox")
        vt = _matmul_tt(w_all, l, W_FOX + 2 * MIX_W, MIX_W, hb, BF16, TM_PROJ, "proj_fox_vt")
        z = _matmul_nt(hb, w_all, l, W_MIX, N_MIXC, F32, TM_PROJ, 512, "proj_mix")
        zs = _matmul_nt(hb, w_all, l, W_SMALL, 2 * LANE, F32, TM_PROJ, 2 * LANE, "proj_small")

        bias_row = jnp.zeros((1, LANE), F32).at[0, SM_FF:SM_FF + FOX_HEADS].set(fox_f_bias[l])
        c = _fox_gate(zs, bias_row)
        o_a = _fox_attention(zf, vt, c)

        wa2p = jnp.zeros((LANE, GLA_HEADS * GLA_DK), F32).at[SM_GA:SM_GA + GLA_RANK].set(gla_wa2[l])
        o_b = _gla(z, zs, wa2p, gla_ba[l].reshape(1, -1), gla_norm_g[l].reshape(1, -1))

        o_c, o_d = _local_mixers(z, conv_w[l], pool_w[l].astype(BF16), pool_scale[l])

        mix = _merge(hb, o_a, o_b, o_c, o_d, w_all, gate_b[l], wb_bf, wo_bf, l)
        h1, mi, mf, counts = _ln1_route(h, mix, ln1_g[l], ln1_b[l], router_wt, router_bc)

        d0, d1, blk_start, n_blk, n_used, row_src, n_valid = _dispatch_tables(mi, counts[:, 0])
        hmid = _moe_up(blk_start, n_blk, n_used, row_src, n_valid, h1, w_gate, w_up, l)
        y_rows = _moe_down(blk_start, n_blk, n_used, hmid, w_down, l)
        if l + 1 < DEPTH:
            h, hb = _combine_ln2(d0, d1, y_rows, h1, mf, ln2_g[l], ln2_b[l], last=False)
        else:
            (out,) = _combine_ln2(d0, d1, y_rows, h1, mf, ln2_g[l], ln2_b[l], last=True)

    return out.reshape(1, SEQ, D_MODEL)
```

```python
import jax
import jax.numpy as jnp
import numpy as np
from jax import lax
from jax.experimental import pallas as pl
from jax.experimental.pallas import tpu as pltpu

F32 = jnp.float32
BF16 = jnp.bfloat16
I32 = jnp.int32
HIGHEST = lax.Precision.HIGHEST

D_MODEL = 2048
SEQ = 8192
DEPTH = 2
N_META = 16
N_BRANCH = 4
MIX_W = 512
FOX_HEADS = 4
FOX_HD = 128
GLA_HEADS = 4
GLA_DK = 64
GLA_DV = 128
GLA_RANK = 16
GLA_TAU = 16.0
CONV_K = 3
POOL_WINDOWS = (2, 4, 8, 16)
POOL_GW = 128
N_EXPERTS = 32
N_GROUPS = 8
EXPERTS_PER_GROUP = 4
TOP_K = 2
D_EXPERT = 1024
LN_EPS = 1e-5
DEEPNORM_ALPHA = (2 * DEPTH) ** 0.25

_SPLITS = (512, 512, 512, 4, 256, 256, 512, 16, 512, 512, 512, 512, 512, 8192)
_OFFS = [int(o) for o in np.concatenate([[0], np.cumsum(_SPLITS)])]
(O_FQ, O_FK, O_FV, O_FF, O_GQ, O_GK, O_GV, O_GA, O_GR, O_CB, O_CC, O_CV, O_PZ, O_GZ, P_IN) = _OFFS

LANE = 128
PAD_ROWS = LANE - N_META
T0 = PAD_ROWS
N_TOK = N_META + SEQ
R = PAD_ROWS + N_TOK
TM = 640
TM_PROJ = 1664
HALO = 16

WT_TILE = 128
W_FOX, W_MIX, W_GATES = 0, 1536, 5120
N_FOX, N_MIXC, N_GATES = 1536, 3584, 8192
W_SMALL = W_GATES + N_GATES
N_WT = W_SMALL + 2 * LANE
C_GQ, C_GK, C_GV, C_GR, C_CB, C_CC, C_CV, C_PZ = 0, 256, 512, 1024, 1536, 2048, 2560, 3072
SM_FF_TILE, SM_GA_TILE = O_FF // LANE, O_GA // LANE
SM_FF = O_FF - SM_FF_TILE * LANE
SM_GA = O_GA - SM_GA_TILE * LANE


def _wt_sources():
    src = []
    for r in range(0, W_SMALL, WT_TILE):
        if r < N_FOX:
            src.append(O_FQ + r)
        elif r < W_MIX + 1024:
            src.append(O_GQ + r - W_MIX)
        else:
            src.append(O_GR + r - (W_MIX + 1024))
    src += [SM_FF_TILE * LANE, SM_GA_TILE * LANE]
    assert all(0 <= s and s + WT_TILE <= P_IN for s in src)
    return src


WT_SRC = _wt_sources()

EXPERT_BLOCK = 128
N_FLAT = N_TOK * TOP_K
N_BLOCKS = -(-N_FLAT // EXPERT_BLOCK) + N_EXPERTS
N_ROWS = N_BLOCKS * EXPERT_BLOCK

NEG = -1e30
VMEM_LIMIT = 48 * 1024 * 1024
BIG_VMEM_LIMIT = 56 * 1024 * 1024
MOE_VMEM_LIMIT = BIG_VMEM_LIMIT


def _cparams(sem, vmem=VMEM_LIMIT):
    return pltpu.CompilerParams(dimension_semantics=sem, vmem_limit_bytes=vmem)


def _log_sigmoid(x):
    return jnp.minimum(x, 0.0) - jnp.log1p(jnp.exp(-jnp.abs(x)))


def _sigmoid(x):
    return 1.0 / (1.0 + jnp.exp(-x))


def _layer_norm_rows(x, g, b):
    mu = jnp.mean(x, axis=-1, keepdims=True)
    xc = x - mu
    var = jnp.mean(xc * xc, axis=-1, keepdims=True)
    return xc * lax.rsqrt(var + LN_EPS) * g + b


def _ln_in_kernel(x_ref, meta_ref, g_ref, b_ref, h_ref, hb_ref):
    i = pl.program_id(0)

    @pl.when(i == 0)
    def _():
        h_ref[...] = jnp.zeros_like(h_ref)
        hb_ref[...] = jnp.zeros_like(hb_ref)
        m = _layer_norm_rows(meta_ref[...], g_ref[...], b_ref[...])
        h_ref[PAD_ROWS:, :] = m
        hb_ref[PAD_ROWS:, :] = m.astype(BF16)

    @pl.when(i > 0)
    def _():
        y = _layer_norm_rows(x_ref[...], g_ref[...], b_ref[...])
        h_ref[...] = y
        hb_ref[...] = y.astype(BF16)


def _ln_in(x2d, meta, g, b):
    nb = R // LANE
    return pl.pallas_call(
        _ln_in_kernel,
        grid=(nb,),
        in_specs=[
            pl.BlockSpec((LANE, D_MODEL), lambda i: (jnp.maximum(i - 1, 0), 0)),
            pl.BlockSpec((N_META, D_MODEL), lambda i: (0, 0)),
            pl.BlockSpec((1, D_MODEL), lambda i: (0, 0)),
            pl.BlockSpec((1, D_MODEL), lambda i: (0, 0)),
        ],
        out_specs=[
            pl.BlockSpec((LANE, D_MODEL), lambda i: (i, 0)),
            pl.BlockSpec((LANE, D_MODEL), lambda i: (i, 0)),
        ],
        out_shape=[
            jax.ShapeDtypeStruct((R, D_MODEL), F32),
            jax.ShapeDtypeStruct((R, D_MODEL), BF16),
        ],
        compiler_params=_cparams(("arbitrary",)),
        name="ln_in",
    )(x2d, meta, g.reshape(1, -1), b.reshape(1, -1))


D_CHUNKS = D_MODEL // LANE
FLAT_ROWS = D_CHUNKS * DEPTH


def _wt_kernel(src_ref, w_hbm, o_ref, buf, sem):
    j = pl.program_id(0)
    slot = j % 2

    def tile_copy(t, s):
        row0 = pl.multiple_of(src_ref[t] * FLAT_ROWS, FLAT_ROWS)
        return pltpu.make_async_copy(w_hbm.at[pl.ds(row0, WT_TILE * FLAT_ROWS), :], buf.at[s],
                                     sem.at[s])

    @pl.when(j == 0)
    def _():
        tile_copy(0, 0).start()

    @pl.when(j + 1 < pl.num_programs(0))
    def _():
        tile_copy(j + 1, 1 - slot).start()

    tile_copy(j, slot).wait()
    for l in range(DEPTH):
        for c in range(D_CHUNKS):
            o_ref[l, :, c * LANE:(c + 1) * LANE] = buf[
                slot, pl.ds(c * DEPTH + l, WT_TILE, stride=FLAT_ROWS), :].astype(BF16)


def _transposed_weights(w_in):
    flat = w_in.reshape(DEPTH, D_CHUNKS, LANE, P_IN).transpose(3, 1, 0, 2).reshape(
        P_IN * FLAT_ROWS, LANE)
    grid_spec = pltpu.PrefetchScalarGridSpec(
        num_scalar_prefetch=1,
        grid=(N_WT // WT_TILE,),
        in_specs=[pl.BlockSpec(memory_space=pl.ANY)],
        out_specs=pl.BlockSpec((DEPTH, WT_TILE, D_MODEL), lambda j, src: (0, j, 0)),
        scratch_shapes=[
            pltpu.VMEM((2, WT_TILE * FLAT_ROWS, LANE), F32),
            pltpu.SemaphoreType.DMA((2,)),
        ],
    )
    return pl.pallas_call(
        _wt_kernel,
        grid_spec=grid_spec,
        out_shape=jax.ShapeDtypeStruct((DEPTH, N_WT, D_MODEL), BF16),
        compiler_params=_cparams(("arbitrary",)),
        name="transposed_weights",
    )(jnp.asarray(WT_SRC, I32), flat)


_NT = (((1,), (1,)), ((), ()))


def _mm_nt_kernel(a_ref, wt_ref, o_ref):
    o_ref[...] = lax.dot_general(a_ref[...], wt_ref[...], _NT,
                                 preferred_element_type=F32).astype(o_ref.dtype)


def _matmul_nt(a, wt, l, row0, n, out_dtype, tm, tn, name):
    m, k = a.shape
    return pl.pallas_call(
        _mm_nt_kernel,
        grid=(m // tm, n // tn),
        in_specs=[
            pl.BlockSpec((tm, k), lambda i, j: (i, 0)),
            pl.BlockSpec((None, tn, k), lambda i, j: (l, row0 // tn + j, 0)),
        ],
        out_specs=pl.BlockSpec((tm, tn), lambda i, j: (i, j)),
        out_shape=jax.ShapeDtypeStruct((m, n), out_dtype),
        compiler_params=_cparams(("parallel", "arbitrary")),
        name=name,
    )(a, wt)


def _matmul_tt(wt, l, row0, n, a, out_dtype, tm, name):
    m, k = a.shape
    return pl.pallas_call(
        _mm_nt_kernel,
        grid=(m // tm,),
        in_specs=[
            pl.BlockSpec((None, n, k), lambda i: (l, row0 // n, 0)),
            pl.BlockSpec((tm, k), lambda i: (i, 0)),
        ],
        out_specs=pl.BlockSpec((n, tm), lambda i: (0, i)),
        out_shape=jax.ShapeDtypeStruct((n, m), out_dtype),
        compiler_params=_cparams(("parallel",)),
        name=name,
    )(wt, a)


def _fox_gate_kernel(zs_ref, bias_ref, c_ref, carry_ref):
    i = pl.program_id(0)

    @pl.when(i == 0)
    def _():
        carry_ref[...] = jnp.zeros_like(carry_ref)

    rows = i * TM + lax.broadcasted_iota(I32, (TM, LANE), 0)
    lf = _log_sigmoid(zs_ref[...] + bias_ref[...])
    lf = jnp.where(rows >= T0, lf, 0.0)
    tri = (lax.broadcasted_iota(I32, (TM, TM), 0)
           >= lax.broadcasted_iota(I32, (TM, TM), 1)).astype(F32)
    c = jnp.dot(tri, lf, precision=HIGHEST, preferred_element_type=F32) + carry_ref[...]
    c_ref[...] = c
    carry_ref[...] = c[TM - 1:TM, :]


def _fox_gate(zs, bias_row):
    return pl.pallas_call(
        _fox_gate_kernel,
        grid=(R // TM,),
        in_specs=[
            pl.BlockSpec((TM, LANE), lambda i: (i, 0)),
            pl.BlockSpec((1, LANE), lambda i: (0, 0)),
        ],
        out_specs=pl.BlockSpec((TM, LANE), lambda i: (i, 0)),
        out_shape=jax.ShapeDtypeStruct((R, LANE), F32),
        scratch_shapes=[pltpu.VMEM((1, LANE), F32)],
        compiler_params=_cparams(("arbitrary",)),
        name="fox_gate",
    )(zs, bias_row)


TQ = TM
N_QB = R // TQ
_PAIRS = [(qi, kj) for qi in range(N_QB) for kj in range(qi + 1)]
N_PAIRS = len(_PAIRS)


LOG2E = 1.4426950408889634
FOX_HPS = FOX_HEADS


def _fox_kernel(qi_tab, kj_tab, q_ref, k_ref, vt_ref, ck_ref, o_ref, m_sc, l_sc, acc_sc):
    p = pl.program_id(1)
    qi = qi_tab[p]
    kj = kj_tab[p]

    @pl.when(kj == 0)
    def _():
        m_sc[...] = jnp.full_like(m_sc, NEG)
        l_sc[...] = jnp.zeros_like(l_sc)
        acc_sc[...] = jnp.zeros_like(acc_sc)

    kpos = kj * TQ + lax.broadcasted_iota(I32, (TQ, 1), 0)
    c1 = FOX_HD ** -0.5 * LOG2E

    def step(causal):
        for hh in range(FOX_HPS):
            lanes = slice(hh * FOX_HD, (hh + 1) * FOX_HD)
            ck = ck_ref[:, SM_FF + hh:SM_FF + hh + 1]
            ckl = jnp.where(kpos >= T0, ck * LOG2E, -NEG)
            t = lax.dot_general(k_ref[:, lanes], q_ref[:, lanes], (((1,), (1,)), ((), ())),
                                preferred_element_type=F32) * c1 - ckl
            if causal:
                ahead = (lax.broadcasted_iota(I32, (TQ, TQ), 0)
                         - lax.broadcasted_iota(I32, (TQ, TQ), 1))
                t = jnp.where(ahead <= 0, t, NEG)
            m_prev = m_sc[hh]
            m_new = jnp.maximum(m_prev, jnp.max(t, axis=0, keepdims=True))
            alpha = jnp.exp2(m_prev - m_new)
            pr = jnp.exp2(t - m_new)
            l_sc[hh] = alpha * l_sc[hh] + jnp.sum(pr, axis=0, keepdims=True)
            acc_sc[hh] = alpha * acc_sc[hh] + jnp.dot(vt_ref[lanes, :], pr.astype(BF16),
                                                      preferred_element_type=F32)
            m_sc[hh] = m_new

    @pl.when(kj == qi)
    def _():
        step(True)

    @pl.when(kj != qi)
    def _():
        step(False)

    @pl.when(kj == qi)
    def _():
        for hh in range(FOX_HPS):
            o_ref[:, hh * FOX_HD:(hh + 1) * FOX_HD] = (acc_sc[hh] / l_sc[hh]).T.astype(o_ref.dtype)


def _fox_attention(zf, vt, c_col):
    qi_tab = jnp.asarray([p[0] for p in _PAIRS], I32)
    kj_tab = jnp.asarray([p[1] for p in _PAIRS], I32)
    hw = FOX_HPS * FOX_HD
    grid_spec = pltpu.PrefetchScalarGridSpec(
        num_scalar_prefetch=2,
        grid=(FOX_HEADS // FOX_HPS, N_PAIRS),
        in_specs=[
            pl.BlockSpec((TQ, hw), lambda h, p, qt, kt: (qt[p], h)),
            pl.BlockSpec((TQ, hw), lambda h, p, qt, kt: (kt[p], FOX_HEADS // FOX_HPS + h)),
            pl.BlockSpec((hw, TQ), lambda h, p, qt, kt: (h, kt[p])),
            pl.BlockSpec((TQ, LANE), lambda h, p, qt, kt: (kt[p], 0)),
        ],
        out_specs=pl.BlockSpec((TQ, hw), lambda h, p, qt, kt: (qt[p], h)),
        scratch_shapes=[
            pltpu.VMEM((FOX_HPS, 1, TQ), F32),
            pltpu.VMEM((FOX_HPS, 1, TQ), F32),
            pltpu.VMEM((FOX_HPS, FOX_HD, TQ), F32),
        ],
    )
    return pl.pallas_call(
        _fox_kernel,
        grid_spec=grid_spec,
        out_shape=jax.ShapeDtypeStruct((R, MIX_W), BF16),
        compiler_params=_cparams(("parallel", "arbitrary")),
        name="fox_attention",
    )(qi_tab, kj_tab, zf, zf, vt, c_col)


GLA_CHUNK = 64
GLA_UNROLL = 5


def _gla_kernel(q_ref, k_ref, v_ref, gr_ref, zs_ref, wa2_ref, ba_ref, gn_ref, o_ref, st_ref, la_ref):
    i = pl.program_id(0)

    @pl.when(i == 0)
    def _():
        st_ref[...] = jnp.zeros_like(st_ref)

    la = jnp.dot(zs_ref[...], wa2_ref[...], precision=HIGHEST, preferred_element_type=F32)
    la_ref[...] = _log_sigmoid(la + ba_ref[...]) * (1.0 / GLA_TAU)

    c_r = lax.broadcasted_iota(I32, (GLA_CHUNK, GLA_CHUNK), 0)
    c_c = lax.broadcasted_iota(I32, (GLA_CHUNK, GLA_CHUNK), 1)
    tri_b = c_r >= c_c
    tri = jnp.where(tri_b, 1.0, 0.0).astype(BF16)

    def chunk(c, carry):
        r0 = pl.multiple_of(c * GLA_CHUNK, GLA_CHUNK)
        rows = pl.ds(r0, GLA_CHUNK)
        g = la_ref[rows, :]
        g_hi = g.astype(BF16)
        g_lo = (g - g_hi.astype(F32)).astype(BF16)
        b = (jnp.dot(tri, g_hi, preferred_element_type=F32)
             + jnp.dot(tri, g_lo, preferred_element_type=F32))
        b_last = b[GLA_CHUNK - 1:GLA_CHUNK, :]
        e_last = jnp.exp(b_last)
        qt = q_ref[rows, :] * (GLA_DK ** -0.5) * jnp.exp(b)
        kt = k_ref[rows, :] * jnp.exp(-b)
        kh = kt * e_last
        for h in range(GLA_HEADS):
            ks = slice(h * GLA_DK, (h + 1) * GLA_DK)
            vs = slice(h * GLA_DV, (h + 1) * GLA_DV)
            q_h = qt[:, ks].astype(BF16)
            k_h = kt[:, ks].astype(BF16)
            kh_h = kh[:, ks].astype(BF16)
            v_h = v_ref[rows, vs]
            att = lax.dot_general(q_h, k_h, (((1,), (1,)), ((), ())), preferred_element_type=F32)
            att = jnp.where(tri_b, att, 0.0)
            st = st_ref[h]
            o = jnp.dot(att.astype(BF16), v_h.astype(BF16), preferred_element_type=F32)
            o = o + lax.dot_general(q_h, st.astype(BF16), (((1,), (1,)), ((), ())),
                                    preferred_element_type=F32)
            st_ref[h] = st * e_last[:, ks] + jnp.dot(v_h.T.astype(BF16), kh_h,
                                                     preferred_element_type=F32)
            ms = jnp.mean(o * o, axis=-1, keepdims=True)
            on = o * lax.rsqrt(ms + LN_EPS) * gn_ref[:, vs]
            gate = gr_ref[rows, vs]
            o_ref[rows, vs] = (on * (gate * _sigmoid(gate))).astype(o_ref.dtype)
        return carry

    lax.fori_loop(0, TM // GLA_CHUNK, chunk, 0, unroll=GLA_UNROLL)


def _gla(z, zs, wa2p, ba, gn):
    return pl.pallas_call(
        _gla_kernel,
        grid=(R // TM,),
        in_specs=[
            pl.BlockSpec((TM, 256), lambda i: (i, C_GQ // 256)),
            pl.BlockSpec((TM, 256), lambda i: (i, C_GK // 256)),
            pl.BlockSpec((TM, 512), lambda i: (i, C_GV // 512)),
            pl.BlockSpec((TM, 512), lambda i: (i, C_GR // 512)),
            pl.BlockSpec((TM, LANE), lambda i: (i, 1)),
            pl.BlockSpec((LANE, 256), lambda i: (0, 0)),
            pl.BlockSpec((1, 256), lambda i: (0, 0)),
            pl.BlockSpec((1, 512), lambda i: (0, 0)),
        ],
        out_specs=pl.BlockSpec((TM, MIX_W), lambda i: (i, 0)),
        out_shape=jax.ShapeDtypeStruct((R, MIX_W), BF16),
        scratch_shapes=[
            pltpu.VMEM((GLA_HEADS, GLA_DV, GLA_DK), F32),
            pltpu.VMEM((TM, GLA_HEADS * GLA_DK), F32),
        ],
        compiler_params=_cparams(("arbitrary",)),
        name="gla",
    )(z, z, z, z, zs, wa2p, ba, gn)


def _local_kernel(cb_ref, cc_ref, cv_ref, pz_ref, cw_ref, pw_ref, ps_ref, oc_ref, od_ref, u_sc, p_sc):
    i = pl.program_id(0)

    @pl.when(i == 0)
    def _():
        u_sc[0:HALO, :] = jnp.zeros((HALO, MIX_W), F32)
        p_sc[0:HALO, :] = jnp.zeros((HALO, MIX_W), F32)

    @pl.when(i > 0)
    def _():
        u_sc[0:HALO, :] = u_sc[TM:TM + HALO, :]
        p_sc[0:HALO, :] = p_sc[TM:TM + HALO, :]

    u = cc_ref[...] * cv_ref[...]
    pz = pz_ref[...]
    u_sc[HALO:, :] = u
    p_sc[HALO:, :] = pz

    y = (cw_ref[2:3, :] * u + cw_ref[1:2, :] * u_sc[HALO - 1:HALO - 1 + TM, :]
         + cw_ref[0:1, :] * u_sc[HALO - 2:HALO - 2 + TM, :])
    oc_ref[...] = (cb_ref[...] * y).astype(oc_ref.dtype)

    tok = i * TM - T0 + lax.broadcasted_iota(I32, (TM, 1), 0)
    cnt_small = jnp.maximum(tok + 1, 1).astype(F32)
    for g, w in enumerate(POOL_WINDOWS):
        cols = slice(g * POOL_GW, (g + 1) * POOL_GW)
        x = pz[:, cols]
        s = x
        for j in range(1, w):
            s = s + p_sc[HALO - j:HALO - j + TM, cols]
        inv_cnt = jnp.where(tok + 1 >= w, 1.0 / w, 1.0 / cnt_small)
        pooled = s * inv_cnt - x
        od = jnp.dot(pooled.astype(BF16), pw_ref[g], preferred_element_type=F32)
        od_ref[:, cols] = (od * ps_ref[:, cols]).astype(od_ref.dtype)


def _local_mixers(z, conv_w, pool_w_bf, pool_scale):
    cw = jnp.zeros((8, MIX_W), F32).at[:CONV_K].set(conv_w)
    blk = lambda c: pl.BlockSpec((TM, MIX_W), lambda i, c=c: (i, c // MIX_W))
    return pl.pallas_call(
        _local_kernel,
        grid=(R // TM,),
        in_specs=[
            blk(C_CB), blk(C_CC), blk(C_CV), blk(C_PZ),
            pl.BlockSpec((8, MIX_W), lambda i: (0, 0)),
            pl.BlockSpec((len(POOL_WINDOWS), POOL_GW, POOL_GW), lambda i: (0, 0, 0)),
            pl.BlockSpec((1, MIX_W), lambda i: (0, 0)),
        ],
        out_specs=[
            pl.BlockSpec((TM, MIX_W), lambda i: (i, 0)),
            pl.BlockSpec((TM, MIX_W), lambda i: (i, 0)),
        ],
        out_shape=[
            jax.ShapeDtypeStruct((R, MIX_W), BF16),
            jax.ShapeDtypeStruct((R, MIX_W), BF16),
        ],
        scratch_shapes=[
            pltpu.VMEM((TM + HALO, MIX_W), F32),
            pltpu.VMEM((TM + HALO, MIX_W), F32),
        ],
        compiler_params=_cparams(("arbitrary",)),
        name="conv_pool",
    )(z, z, z, z, cw, pool_w_bf, pool_scale.reshape(1, -1))


TN_MERGE = 512


def _merge_kernel(hb_ref, oa_ref, ob_ref, oc_ref, od_ref, wg0_ref, wg1_ref, wg2_ref, wg3_ref,
                  gb_ref, wb_ref, wo_ref, out_ref):
    j = pl.program_id(1)

    @pl.when(j == 0)
    def _():
        out_ref[...] = jnp.zeros_like(out_ref)

    hb = hb_ref[...]
    mixed = None
    for b, (o_ref, wg_ref) in enumerate(((oa_ref, wg0_ref), (ob_ref, wg1_ref),
                                         (oc_ref, wg2_ref), (od_ref, wg3_ref))):
        gate = _sigmoid(lax.dot_general(hb, wg_ref[...], _NT, preferred_element_type=F32)
                        + gb_ref[b:b + 1, :])
        proj = jnp.dot(o_ref[...], wb_ref[b], preferred_element_type=F32)
        term = gate * proj
        mixed = term if mixed is None else mixed + term
    out_ref[...] += jnp.dot(mixed.astype(BF16), wo_ref[...], preferred_element_type=F32)


def _merge(hb, o_a, o_b, o_c, o_d, w_all, gate_b, wb_bf, wo_bf, l):
    tn = TN_MERGE
    nj = D_MODEL // tn
    row = lambda w: pl.BlockSpec((TM, w), lambda i, j: (i, 0))
    wg = lambda b: pl.BlockSpec((None, tn, D_MODEL),
                                lambda i, j, b=b: (l, W_GATES // tn + b * nj + j, 0))
    return pl.pallas_call(
        _merge_kernel,
        grid=(R // TM, nj),
        in_specs=[
            row(D_MODEL), row(MIX_W), row(MIX_W), row(MIX_W), row(MIX_W),
            wg(0), wg(1), wg(2), wg(3),
            pl.BlockSpec((N_BRANCH, tn), lambda i, j: (0, j)),
            pl.BlockSpec((None, N_BRANCH, MIX_W, tn), lambda i, j: (l, 0, 0, j)),
            pl.BlockSpec((None, tn, D_MODEL), lambda i, j: (l, j, 0)),
        ],
        out_specs=pl.BlockSpec((TM, D_MODEL), lambda i, j: (i, 0)),
        out_shape=jax.ShapeDtypeStruct((R, D_MODEL), F32),
        compiler_params=_cparams(("parallel", "arbitrary"), vmem=BIG_VMEM_LIMIT),
        name="merge",
    )(hb, o_a, o_b, o_c, o_d, w_all, w_all, w_all, w_all, gate_b, wb_bf, wo_bf)


def _post_ln(h, delta, g, b, row0):
    y = _layer_norm_rows(DEEPNORM_ALPHA * h + delta, g, b)
    rows = row0 + lax.broadcasted_iota(I32, (y.shape[0], 1), 0)
    return jnp.where(rows >= T0, y, 0.0)


def _first_of(cands, target):
    idx = jnp.full(target.shape, len(cands) - 1, I32)
    for j in range(len(cands) - 2, -1, -1):
        idx = jnp.where(cands[j] == target, j, idx)
    return idx


def _pick(cands, idx):
    out = cands[-1]
    for j in range(len(cands) - 2, -1, -1):
        out = jnp.where(idx == j, cands[j], out)
    return out


def _ln1_route_kernel(h_ref, mix_ref, g_ref, b_ref, rwt_ref, rb_ref,
                      h1_ref, mi_ref, mf_ref, cnt_ref, carry_sc):
    i = pl.program_id(0)

    @pl.when(i == 0)
    def _():
        carry_sc[...] = jnp.zeros_like(carry_sc)

    y = _post_ln(h_ref[...], mix_ref[...], g_ref[...], b_ref[...], i * TM)
    h1_ref[...] = y

    logits = lax.dot_general(rwt_ref[...], y, (((1,), (1,)), ((), ())), precision=HIGHEST,
                             preferred_element_type=F32)
    aff = _sigmoid(logits)
    sel = aff + rb_ref[...]
    xs = [sel[j * N_GROUPS:(j + 1) * N_GROUPS, :] for j in range(EXPERTS_PER_GROUP)]
    afs = [aff[j * N_GROUPS:(j + 1) * N_GROUPS, :] for j in range(EXPERTS_PER_GROUP)]

    score = None
    for a in range(EXPERTS_PER_GROUP):
        for bb in range(a + 1, EXPERTS_PER_GROUP):
            pair = xs[a] + xs[bb]
            score = pair if score is None else jnp.maximum(score, pair)
    giota = lax.broadcasted_iota(I32, (N_GROUPS, TM), 0)
    gmax = jnp.max(score, axis=0, keepdims=True)
    grp = jnp.min(jnp.where(score == gmax, giota, N_GROUPS), axis=0, keepdims=True)
    gsel = giota == grp
    cs = [jnp.max(jnp.where(gsel, x, -jnp.inf), axis=0, keepdims=True) for x in xs]
    acs = [jnp.sum(jnp.where(gsel, a, 0.0), axis=0, keepdims=True) for a in afs]

    m1 = jnp.maximum(jnp.maximum(cs[0], cs[1]), jnp.maximum(cs[2], cs[3]))
    i0 = _first_of(cs, m1)
    ds = [jnp.where(i0 == j, -jnp.inf, cs[j]) for j in range(EXPERTS_PER_GROUP)]
    m2 = jnp.maximum(jnp.maximum(ds[0], ds[1]), jnp.maximum(ds[2], ds[3]))
    i1 = _first_of(ds, m2)
    a0 = _pick(acs, i0)
    a1 = _pick(acs, i1)
    denom = a0 + a1

    pos = i * TM + lax.broadcasted_iota(I32, (1, TM), 1)
    valid = pos >= T0
    riota = lax.broadcasted_iota(I32, (N_EXPERTS, TM), 0)
    oh0 = (riota == i0 * N_GROUPS + grp) & valid
    oh1 = (riota == i1 * N_GROUPS + grp) & valid
    ohf = jnp.where(oh0 | oh1, 1.0, 0.0)
    before = (lax.broadcasted_iota(I32, (TM, TM), 0)
              < lax.broadcasted_iota(I32, (TM, TM), 1)).astype(BF16)
    cum = jnp.dot(ohf.astype(BF16), before, preferred_element_type=F32) + carry_sc[...]
    rank0 = jnp.sum(jnp.where(oh0, cum, 0.0), axis=0, keepdims=True)
    rank1 = jnp.sum(jnp.where(oh1, cum, 0.0), axis=0, keepdims=True)
    carry = carry_sc[...] + jnp.sum(ohf, axis=1, keepdims=True)
    carry_sc[...] = carry
    cnt_ref[...] = jnp.broadcast_to(carry, cnt_ref.shape)

    zi = jnp.zeros((1, TM), I32)
    mi_ref[...] = jnp.concatenate(
        [grp * EXPERTS_PER_GROUP + i0, grp * EXPERTS_PER_GROUP + i1,
         rank0.astype(I32), rank1.astype(I32), zi, zi, zi, zi], axis=0)
    zf = jnp.zeros((1, TM), F32)
    mf_ref[...] = jnp.concatenate([a0 / denom, a1 / denom, zf, zf, zf, zf, zf, zf], axis=0)


def _ln1_route(h, mix, g, b, router_wt, router_bc):
    row = pl.BlockSpec((TM, D_MODEL), lambda i: (i, 0))
    vec = pl.BlockSpec((1, D_MODEL), lambda i: (0, 0))
    meta = pl.BlockSpec((8, TM), lambda i: (0, i))
    return pl.pallas_call(
        _ln1_route_kernel,
        grid=(R // TM,),
        in_specs=[row, row, vec, vec,
                  pl.BlockSpec((N_EXPERTS, D_MODEL), lambda i: (0, 0)),
                  pl.BlockSpec((N_EXPERTS, 1), lambda i: (0, 0))],
        out_specs=[row, meta, meta, pl.BlockSpec((N_EXPERTS, LANE), lambda i: (0, 0))],
        out_shape=[
            jax.ShapeDtypeStruct((R, D_MODEL), F32),
            jax.ShapeDtypeStruct((8, R), I32),
            jax.ShapeDtypeStruct((8, R), F32),
            jax.ShapeDtypeStruct((N_EXPERTS, LANE), F32),
        ],
        scratch_shapes=[pltpu.VMEM((N_EXPERTS, 1), F32)],
        compiler_params=_cparams(("arbitrary",)),
        name="ln1_route",
    )(h, mix, g.reshape(1, -1), b.reshape(1, -1), router_wt, router_bc)


def _dispatch_tables(mi, counts_slot_major):
    counts = counts_slot_major.reshape(EXPERTS_PER_GROUP, N_GROUPS).T.reshape(N_EXPERTS).astype(I32)
    padded = (counts + EXPERT_BLOCK - 1) // EXPERT_BLOCK * EXPERT_BLOCK
    pad_end = jnp.cumsum(padded)
    pad_start = pad_end - padded
    e_iota = jnp.arange(N_EXPERTS, dtype=I32)
    rows_ok = jnp.arange(R) >= T0

    def dest(eid, rank):
        start = jnp.sum(jnp.where(eid[:, None] == e_iota[None, :], pad_start[None, :], 0), axis=1)
        return jnp.where(rows_ok, start + rank, 0).astype(I32)

    d0 = dest(mi[0], mi[2])
    d1 = dest(mi[1], mi[3])
    blk_start = (pad_start // EXPERT_BLOCK).astype(I32)
    n_blk = (padded // EXPERT_BLOCK).astype(I32)
    n_used = (pad_end[-1] // EXPERT_BLOCK).astype(I32).reshape(1)
    tok_rows = jnp.arange(T0, R, dtype=I32)
    row_src = jnp.zeros((N_ROWS + UP_AHEAD * EXPERT_BLOCK,), I32).at[
        jnp.concatenate([d0[T0:], d1[T0:]])].set(jnp.concatenate([tok_rows, tok_rows]),
                                                 unique_indices=True)
    blk = jnp.arange(N_BLOCKS + UP_AHEAD, dtype=I32)
    owner = (blk[:, None] >= blk_start[None, :]) & (blk[:, None] < (blk_start + n_blk)[None, :])
    left = counts[None, :] - (blk[:, None] - blk_start[None, :]) * EXPERT_BLOCK
    n_valid = jnp.sum(jnp.where(owner, jnp.clip(left, 0, EXPERT_BLOCK), 0), axis=1).astype(I32)
    return d0, d1, blk_start, n_blk, n_used, row_src, n_valid


def _row_copy(src, src_row, dst, dst_row, sem):
    return pltpu.make_async_copy(src.at[pl.ds(src_row, 1), :], dst.at[pl.ds(dst_row, 1), :], sem)


def _issue_row_gather(rs_ref, nv_ref, g, h_hbm, buf, sem, full=False):
    base = g * EXPERT_BLOCK
    nv = nv_ref[g]
    for r in range(EXPERT_BLOCK):
        if full:
            _row_copy(h_hbm, rs_ref[base + r], buf, r, sem).start(priority=r % 2)
        else:
            @pl.when(r < nv)
            def _():
                _row_copy(h_hbm, rs_ref[base + r], buf, r, sem).start(priority=r % 2)


def _wait_row_gather(nv_ref, g, h_hbm, buf, sem, full=False):
    if full:
        pltpu.make_async_copy(h_hbm.at[pl.ds(0, EXPERT_BLOCK), :], buf, sem).wait()
        return
    nv = nv_ref[g]
    k = EXPERT_BLOCK
    while k >= 1:
        @pl.when((nv & k) != 0)
        def _():
            pltpu.make_async_copy(h_hbm.at[pl.ds(0, k), :], buf.at[pl.ds(0, k), :], sem).wait()
        k //= 2


def _block_rows(g):
    return pl.ds(pl.multiple_of(g * EXPERT_BLOCK, EXPERT_BLOCK), EXPERT_BLOCK)


def _finish_writes(out_copy, obuf, nu):
    n_slots = obuf.shape[0]
    for back in range(1, n_slots + 1):
        @pl.when(nu >= back)
        def _():
            out_copy(nu - back, (nu - back) % n_slots).wait()

    obuf[0] = jnp.zeros(obuf.shape[1:], obuf.dtype)

    def zero_block(g, carry):
        cp = out_copy(g, 0)
        cp.start()
        cp.wait()
        return carry

    lax.fori_loop(nu, N_BLOCKS, zero_block, 0)


UP_AHEAD = 2
UP_X_SLOTS = UP_AHEAD + 1


def _moe_up_kernel(bs_ref, nb_ref, nu_ref, rs_ref, nv_ref, h_hbm, wg_ref, wu_ref, o_hbm,
                   wg_sc, wu_sc, xbuf, obuf, xsem, osem):
    e = pl.program_id(0)
    nb = nb_ref[e]
    g0 = bs_ref[e]

    def out_copy(g, slot):
        return pltpu.make_async_copy(obuf.at[slot], o_hbm.at[_block_rows(g), :], osem.at[slot])

    @pl.when(e == 0)
    def _():
        xbuf[...] = jnp.zeros_like(xbuf)
        for g in range(UP_AHEAD):
            _issue_row_gather(rs_ref, nv_ref, g, h_hbm, xbuf.at[g], xsem.at[g])

    @pl.when(nb > 0)
    def _():
        wg_sc[...] = wg_ref[...].astype(BF16)
        wu_sc[...] = wu_ref[...].astype(BF16)

        def block(j, carry, full):
            g = g0 + j
            slot = g % 2
            xs = g % UP_X_SLOTS
            xs_next = (g + UP_AHEAD) % UP_X_SLOTS

            @pl.when(g >= 2)
            def _():
                out_copy(g - 2, slot).wait()

            _wait_row_gather(nv_ref, g, h_hbm, xbuf.at[xs], xsem.at[xs], full)
            _issue_row_gather(rs_ref, nv_ref, g + UP_AHEAD, h_hbm, xbuf.at[xs_next],
                              xsem.at[xs_next], full)
            x = xbuf[xs].astype(BF16)
            gate = jnp.dot(x, wg_sc[...], preferred_element_type=F32)
            up = jnp.dot(x, wu_sc[...], preferred_element_type=F32)
            obuf[slot] = (gate * _sigmoid(gate) * up).astype(BF16)
            out_copy(g, slot).start()
            return carry

        n_full = jnp.maximum(nb - 1 - UP_AHEAD, 0)
        lax.fori_loop(0, n_full, lambda j, c: block(j, c, True), 0)
        lax.fori_loop(n_full, nb, lambda j, c: block(j, c, False), 0)

    @pl.when(e == N_EXPERTS - 1)
    def _():
        _finish_writes(out_copy, obuf, nu_ref[0])


def _moe_up(blk_start, n_blk, n_used, row_src, n_valid, h1, w_gate, w_up, l):
    wspec = pl.BlockSpec((None, None, D_MODEL, D_EXPERT), lambda e, *_: (l, e, 0, 0))
    grid_spec = pltpu.PrefetchScalarGridSpec(
        num_scalar_prefetch=5,
        grid=(N_EXPERTS,),
        in_specs=[pl.BlockSpec(memory_space=pl.ANY), wspec, wspec],
        out_specs=pl.BlockSpec(memory_space=pl.ANY),
        scratch_shapes=[
            pltpu.VMEM((D_MODEL, D_EXPERT), BF16),
            pltpu.VMEM((D_MODEL, D_EXPERT), BF16),
            pltpu.VMEM((UP_X_SLOTS, EXPERT_BLOCK, D_MODEL), F32),
            pltpu.VMEM((2, EXPERT_BLOCK, D_EXPERT), BF16),
            pltpu.SemaphoreType.DMA((UP_X_SLOTS,)),
            pltpu.SemaphoreType.DMA((2,)),
        ],
    )
    return pl.pallas_call(
        _moe_up_kernel,
        grid_spec=grid_spec,
        out_shape=jax.ShapeDtypeStruct((N_ROWS, D_EXPERT), BF16),
        compiler_params=_cparams(("arbitrary",), vmem=MOE_VMEM_LIMIT),
        name="moe_up",
    )(blk_start, n_blk, n_used, row_src, n_valid, h1, w_gate, w_up)


DOWN_IN_SLOTS = 4
DOWN_OUT_SLOTS = 3


def _moe_down_kernel(bs_ref, nb_ref, nu_ref, x_hbm, wd_ref, y_hbm, wd_sc, xbuf, obuf, xsem, osem):
    e = pl.program_id(0)
    nb = nb_ref[e]
    g0 = bs_ref[e]
    nu = nu_ref[0]

    def in_copy(g):
        s = g % DOWN_IN_SLOTS
        return pltpu.make_async_copy(x_hbm.at[_block_rows(g), :], xbuf.at[s], xsem.at[s])

    def out_copy(g, slot):
        return pltpu.make_async_copy(obuf.at[slot], y_hbm.at[_block_rows(g), :], osem.at[slot])

    @pl.when(e == 0)
    def _():
        for g in range(DOWN_IN_SLOTS - 1):
            @pl.when(g < nu)
            def _():
                in_copy(g).start()

    @pl.when(nb > 0)
    def _():
        wd_sc[...] = wd_ref[...].astype(BF16)

        def block(j, carry):
            g = g0 + j
            slot = g % DOWN_OUT_SLOTS

            @pl.when(g >= DOWN_OUT_SLOTS)
            def _():
                out_copy(g - DOWN_OUT_SLOTS, slot).wait()

            in_copy(g).wait()

            @pl.when(g + DOWN_IN_SLOTS - 1 < nu)
            def _():
                in_copy(g + DOWN_IN_SLOTS - 1).start()

            obuf[slot] = jnp.dot(xbuf[g % DOWN_IN_SLOTS], wd_sc[...], preferred_element_type=F32)
            out_copy(g, slot).start()
            return carry

        lax.fori_loop(0, nb, block, 0)

    @pl.when(e == N_EXPERTS - 1)
    def _():
        _finish_writes(out_copy, obuf, nu)


def _moe_down(blk_start, n_blk, n_used, hmid, w_down, l):
    grid_spec = pltpu.PrefetchScalarGridSpec(
        num_scalar_prefetch=3,
        grid=(N_EXPERTS,),
        in_specs=[
            pl.BlockSpec(memory_space=pl.ANY),
            pl.BlockSpec((None, None, D_EXPERT, D_MODEL), lambda e, *_: (l, e, 0, 0)),
        ],
        out_specs=pl.BlockSpec(memory_space=pl.ANY),
        scratch_shapes=[
            pltpu.VMEM((D_EXPERT, D_MODEL), BF16),
            pltpu.VMEM((DOWN_IN_SLOTS, EXPERT_BLOCK, D_EXPERT), BF16),
            pltpu.VMEM((DOWN_OUT_SLOTS, EXPERT_BLOCK, D_MODEL), F32),
            pltpu.SemaphoreType.DMA((DOWN_IN_SLOTS,)),
            pltpu.SemaphoreType.DMA((DOWN_OUT_SLOTS,)),
        ],
    )
    return pl.pallas_call(
        _moe_down_kernel,
        grid_spec=grid_spec,
        out_shape=jax.ShapeDtypeStruct((N_ROWS, D_MODEL), F32),
        compiler_params=_cparams(("arbitrary",), vmem=MOE_VMEM_LIMIT),
        name="moe_down",
    )(blk_start, n_blk, n_used, hmid, w_down)


def _combine_kernel(d0_ref, d1_ref, y_hbm, h1_ref, mf_ref, g_ref, b_ref, h2_ref, h2b_ref,
                    buf0, buf1, sem):
    i = pl.program_id(0)
    slot = i % 2

    def issue(tile, s):
        base = tile * LANE

        def body(r, carry):
            _row_copy(y_hbm, d0_ref[base + r], buf0.at[s], r, sem.at[s]).start(priority=0)
            _row_copy(y_hbm, d1_ref[base + r], buf1.at[s], r, sem.at[s]).start(priority=1)
            return carry

        lax.fori_loop(0, LANE, body, 0)

    @pl.when(i == 0)
    def _():
        issue(0, 0)

    @pl.when(i + 1 < pl.num_programs(0))
    def _():
        issue(i + 1, 1 - slot)

    def drain(r, carry):
        _row_copy(y_hbm, 0, buf0.at[slot], r, sem.at[slot]).wait()
        _row_copy(y_hbm, 0, buf1.at[slot], r, sem.at[slot]).wait()
        return carry

    lax.fori_loop(0, LANE, drain, 0)
    wt = mf_ref[...].T
    ffn = wt[:, 0:1] * buf0[slot] + wt[:, 1:2] * buf1[slot]
    y = _post_ln(h1_ref[...], ffn, g_ref[...], b_ref[...], i * LANE)
    h2_ref[...] = y
    if h2b_ref is not None:
        h2b_ref[...] = y.astype(BF16)


def _combine_last_kernel(d0_ref, d1_ref, y_hbm, h1_ref, mf_ref, g_ref, b_ref, out_ref, buf0, buf1, sem):
    _combine_kernel(d0_ref, d1_ref, y_hbm, h1_ref, mf_ref, g_ref, b_ref, out_ref, None, buf0, buf1, sem)


def _combine_ln2(d0, d1, y_rows, h1, mf, g, b, last):
    row = lambda i, d0, d1: (i, 0)
    vec = pl.BlockSpec((1, D_MODEL), lambda i, d0, d1: (0, 0))
    if last:
        frames = lambda i, d0, d1: (jnp.maximum(i - 1, 0), 0)
        out_specs = [pl.BlockSpec((LANE, D_MODEL), frames)]
        out_shape = [jax.ShapeDtypeStruct((SEQ, D_MODEL), F32)]
    else:
        out_specs = [pl.BlockSpec((LANE, D_MODEL), row), pl.BlockSpec((LANE, D_MODEL), row)]
        out_shape = [jax.ShapeDtypeStruct((R, D_MODEL), F32), jax.ShapeDtypeStruct((R, D_MODEL), BF16)]
    grid_spec = pltpu.PrefetchScalarGridSpec(
        num_scalar_prefetch=2,
        grid=(R // LANE,),
        in_specs=[
            pl.BlockSpec(memory_space=pl.ANY),
            pl.BlockSpec((LANE, D_MODEL), row),
            pl.BlockSpec((8, LANE), lambda i, d0, d1: (0, i)),
            vec, vec,
        ],
        out_specs=out_specs,
        scratch_shapes=[
            pltpu.VMEM((2, LANE, D_MODEL), F32),
            pltpu.VMEM((2, LANE, D_MODEL), F32),
            pltpu.SemaphoreType.DMA((2,)),
        ],
    )
    return pl.pallas_call(
        _combine_last_kernel if last else _combine_kernel,
        grid_spec=grid_spec,
        out_shape=out_shape,
        compiler_params=_cparams(("arbitrary",)),
        name="moe_combine_ln2",
    )(d0, d1, y_rows, h1, mf, g.reshape(1, -1), b.reshape(1, -1))


def kernel(x, meta_tokens, ln_in_g, ln_in_b, w_in, fox_f_bias, gla_wa2, gla_ba, gla_norm_g, conv_w, pool_w, pool_scale, gate_b, w_branch, w_out, ln1_g, ln1_b, router_w, router_b, w_gate, w_up, w_down, ln2_g, ln2_b):
    assert x.shape == (1, SEQ, D_MODEL)
    h, hb = _ln_in(x.reshape(SEQ, D_MODEL), meta_tokens, ln_in_g, ln_in_b)
    router_wt = router_w.T.reshape(N_GROUPS, EXPERTS_PER_GROUP, D_MODEL).transpose(1, 0, 2).reshape(
        N_EXPERTS, D_MODEL)
    router_bc = router_b.astype(F32).reshape(N_GROUPS, EXPERTS_PER_GROUP).T.reshape(N_EXPERTS, 1)

    wb_bf = w_branch.astype(BF16)
    wo_bf = w_out.astype(BF16)

    w_all = _transposed_weights(w_in)

    for l in range(DEPTH):
        zf = _matmul_nt(hb, w_all, l, W_FOX, 2 * MIX_W, BF16, TM_PROJ, 512, "proj_fox")
        vt = _matmul_tt(w_all, l, W_FOX + 2 * MIX_W, MIX_W, hb, BF16, TM_PROJ, "proj_fox_vt")
        z = _matmul_nt(hb, w_all, l, W_MIX, N_MIXC, F32, TM_PROJ, 512, "proj_mix")
        zs = _matmul_nt(hb, w_all, l, W_SMALL, 2 * LANE, F32, TM_PROJ, 2 * LANE, "proj_small")

        bias_row = jnp.zeros((1, LANE), F32).at[0, SM_FF:SM_FF + FOX_HEADS].set(fox_f_bias[l])
        c = _fox_gate(zs, bias_row)
        o_a = _fox_attention(zf, vt, c)

        wa2p = jnp.zeros((LANE, GLA_HEADS * GLA_DK), F32).at[SM_GA:SM_GA + GLA_RANK].set(gla_wa2[l])
        o_b = _gla(z, zs, wa2p, gla_ba[l].reshape(1, -1), gla_norm_g[l].reshape(1, -1))

        o_c, o_d = _local_mixers(z, conv_w[l], pool_w[l].astype(BF16), pool_scale[l])

        mix = _merge(hb, o_a, o_b, o_c, o_d, w_all, gate_b[l], wb_bf, wo_bf, l)
        h1, mi, mf, counts = _ln1_route(h, mix, ln1_g[l], ln1_b[l], router_wt, router_bc)

        d0, d1, blk_start, n_blk, n_used, row_src, n_valid = _dispatch_tables(mi, counts[:, 0])
        hmid = _moe_up(blk_start, n_blk, n_used, row_src, n_valid, h1, w_gate, w_up, l)
        y_rows = _moe_down(blk_start, n_blk, n_used, hmid, w_down, l)
        if l + 1 < DEPTH:
            h, hb = _combine_ln2(d0, d1, y_rows, h1, mf, ln2_g[l], ln2_b[l], last=False)
        else:
            (out,) = _combine_ln2(d0, d1, y_rows, h1, mf, ln2_g[l], ln2_b[l], last=True)

    return out.reshape(1, SEQ, D_MODEL)
```

```python
import jax
import jax.numpy as jnp
import numpy as np
from jax import lax
from jax.experimental import pallas as pl
from jax.experimental.pallas import tpu as pltpu

F32 = jnp.float32
BF16 = jnp.bfloat16
I32 = jnp.int32
HIGHEST = lax.Precision.HIGHEST

D_MODEL = 2048
SEQ = 8192
DEPTH = 2
N_META = 16
N_BRANCH = 4
MIX_W = 512
FOX_HEADS = 4
FOX_HD = 128
GLA_HEADS = 4
GLA_DK = 64
GLA_DV = 128
GLA_RANK = 16
GLA_TAU = 16.0
CONV_K = 3
POOL_WINDOWS = (2, 4, 8, 16)
POOL_GW = 128
N_EXPERTS = 32
N_GROUPS = 8
EXPERTS_PER_GROUP = 4
TOP_K = 2
D_EXPERT = 1024
LN_EPS = 1e-5
DEEPNORM_ALPHA = (2 * DEPTH) ** 0.25

_SPLITS = (512, 512, 512, 4, 256, 256, 512, 16, 512, 512, 512, 512, 512, 8192)
_OFFS = [int(o) for o in np.concatenate([[0], np.cumsum(_SPLITS)])]
(O_FQ, O_FK, O_FV, O_FF, O_GQ, O_GK, O_GV, O_GA, O_GR, O_CB, O_CC, O_CV, O_PZ, O_GZ, P_IN) = _OFFS

LANE = 128
PAD_ROWS = LANE - N_META
T0 = PAD_ROWS
N_TOK = N_META + SEQ
R = PAD_ROWS + N_TOK
TM = 640
TM_PROJ = 1664
HALO = 16

WT_TILE = 128
W_FOX, W_MIX, W_GATES = 0, 1536, 5120
N_FOX, N_MIXC, N_GATES = 1536, 3584, 8192
W_SMALL = W_GATES + N_GATES
N_WT = W_SMALL + 2 * LANE
C_GQ, C_GK, C_GV, C_GR, C_CB, C_CC, C_CV, C_PZ = 0, 256, 512, 1024, 1536, 2048, 2560, 3072
SM_FF_TILE, SM_GA_TILE = O_FF // LANE, O_GA // LANE
SM_FF = O_FF - SM_FF_TILE * LANE
SM_GA = O_GA - SM_GA_TILE * LANE


def _wt_sources():
    src = []
    for r in range(0, W_SMALL, WT_TILE):
        if r < N_FOX:
            src.append(O_FQ + r)
        elif r < W_MIX + 1024:
            src.append(O_GQ + r - W_MIX)
        else:
            src.append(O_GR + r - (W_MIX + 1024))
    src += [SM_FF_TILE * LANE, SM_GA_TILE * LANE]
    assert all(0 <= s and s + WT_TILE <= P_IN for s in src)
    return src


WT_SRC = _wt_sources()

EXPERT_BLOCK = 128
N_FLAT = N_TOK * TOP_K
N_BLOCKS = -(-N_FLAT // EXPERT_BLOCK) + N_EXPERTS
N_ROWS = N_BLOCKS * EXPERT_BLOCK

NEG = -1e30
VMEM_LIMIT = 48 * 1024 * 1024
BIG_VMEM_LIMIT = 56 * 1024 * 1024
MOE_VMEM_LIMIT = BIG_VMEM_LIMIT


def _cparams(sem, vmem=VMEM_LIMIT):
    return pltpu.CompilerParams(dimension_semantics=sem, vmem_limit_bytes=vmem)


def _log_sigmoid(x):
    return jnp.minimum(x, 0.0) - jnp.log1p(jnp.exp(-jnp.abs(x)))


def _sigmoid(x):
    return 1.0 / (1.0 + jnp.exp(-x))


def _layer_norm_rows(x, g, b):
    mu = jnp.mean(x, axis=-1, keepdims=True)
    xc = x - mu
    var = jnp.mean(xc * xc, axis=-1, keepdims=True)
    return xc * lax.rsqrt(var + LN_EPS) * g + b


def _ln_in_kernel(x_ref, meta_ref, g_ref, b_ref, h_ref, hb_ref):
    i = pl.program_id(0)

    @pl.when(i == 0)
    def _():
        h_ref[...] = jnp.zeros_like(h_ref)
        hb_ref[...] = jnp.zeros_like(hb_ref)
        m = _layer_norm_rows(meta_ref[...], g_ref[...], b_ref[...])
        h_ref[PAD_ROWS:, :] = m
        hb_ref[PAD_ROWS:, :] = m.astype(BF16)

    @pl.when(i > 0)
    def _():
        y = _layer_norm_rows(x_ref[...], g_ref[...], b_ref[...])
        h_ref[...] = y
        hb_ref[...] = y.astype(BF16)


def _ln_in(x2d, meta, g, b):
    nb = R // LANE
    return pl.pallas_call(
        _ln_in_kernel,
        grid=(nb,),
        in_specs=[
            pl.BlockSpec((LANE, D_MODEL), lambda i: (jnp.maximum(i - 1, 0), 0)),
            pl.BlockSpec((N_META, D_MODEL), lambda i: (0, 0)),
            pl.BlockSpec((1, D_MODEL), lambda i: (0, 0)),
            pl.BlockSpec((1, D_MODEL), lambda i: (0, 0)),
        ],
        out_specs=[
            pl.BlockSpec((LANE, D_MODEL), lambda i: (i, 0)),
            pl.BlockSpec((LANE, D_MODEL), lambda i: (i, 0)),
        ],
        out_shape=[
            jax.ShapeDtypeStruct((R, D_MODEL), F32),
            jax.ShapeDtypeStruct((R, D_MODEL), BF16),
        ],
        compiler_params=_cparams(("arbitrary",)),
        name="ln_in",
    )(x2d, meta, g.reshape(1, -1), b.reshape(1, -1))


D_CHUNKS = D_MODEL // LANE
FLAT_ROWS = D_CHUNKS * DEPTH


def _wt_kernel(src_ref, w_hbm, o_ref, buf, sem):
    j = pl.program_id(0)
    slot = j % 2

    def tile_copy(t, s):
        row0 = pl.multiple_of(src_ref[t] * FLAT_ROWS, FLAT_ROWS)
        return pltpu.make_async_copy(w_hbm.at[pl.ds(row0, WT_TILE * FLAT_ROWS), :], buf.at[s],
                                     sem.at[s])

    @pl.when(j == 0)
    def _():
        tile_copy(0, 0).start()

    @pl.when(j + 1 < pl.num_programs(0))
    def _():
        tile_copy(j + 1, 1 - slot).start()

    tile_copy(j, slot).wait()
    for l in range(DEPTH):
        for c in range(D_CHUNKS):
            o_ref[l, :, c * LANE:(c + 1) * LANE] = buf[
                slot, pl.ds(c * DEPTH + l, WT_TILE, stride=FLAT_ROWS), :].astype(BF16)


def _transposed_weights(w_in):
    flat = w_in.reshape(DEPTH, D_CHUNKS, LANE, P_IN).transpose(3, 1, 0, 2).reshape(
        P_IN * FLAT_ROWS, LANE)
    grid_spec = pltpu.PrefetchScalarGridSpec(
        num_scalar_prefetch=1,
        grid=(N_WT // WT_TILE,),
        in_specs=[pl.BlockSpec(memory_space=pl.ANY)],
        out_specs=pl.BlockSpec((DEPTH, WT_TILE, D_MODEL), lambda j, src: (0, j, 0)),
        scratch_shapes=[
            pltpu.VMEM((2, WT_TILE * FLAT_ROWS, LANE), F32),
            pltpu.SemaphoreType.DMA((2,)),
        ],
    )
    return pl.pallas_call(
        _wt_kernel,
        grid_spec=grid_spec,
        out_shape=jax.ShapeDtypeStruct((DEPTH, N_WT, D_MODEL), BF16),
        compiler_params=_cparams(("arbitrary",)),
        name="transposed_weights",
    )(jnp.asarray(WT_SRC, I32), flat)


_NT = (((1,), (1,)), ((), ()))


def _mm_nt_kernel(a_ref, wt_ref, o_ref):
    o_ref[...] = lax.dot_general(a_ref[...], wt_ref[...], _NT,
                                 preferred_element_type=F32).astype(o_ref.dtype)


def _matmul_nt(a, wt, l, row0, n, out_dtype, tm, tn, name):
    m, k = a.shape
    return pl.pallas_call(
        _mm_nt_kernel,
        grid=(m // tm, n // tn),
        in_specs=[
            pl.BlockSpec((tm, k), lambda i, j: (i, 0)),
            pl.BlockSpec((None, tn, k), lambda i, j: (l, row0 // tn + j, 0)),
        ],
        out_specs=pl.BlockSpec((tm, tn), lambda i, j: (i, j)),
        out_shape=jax.ShapeDtypeStruct((m, n), out_dtype),
        compiler_params=_cparams(("parallel", "arbitrary")),
        name=name,
    )(a, wt)


def _matmul_tt(wt, l, row0, n, a, out_dtype, tm, name):
    m, k = a.shape
    return pl.pallas_call(
        _mm_nt_kernel,
        grid=(m // tm,),
        in_specs=[
            pl.BlockSpec((None, n, k), lambda i: (l, row0 // n, 0)),
            pl.BlockSpec((tm, k), lambda i: (i, 0)),
        ],
        out_specs=pl.BlockSpec((n, tm), lambda i: (0, i)),
        out_shape=jax.ShapeDtypeStruct((n, m), out_dtype),
        compiler_params=_cparams(("parallel",)),
        name=name,
    )(wt, a)


def _fox_gate_kernel(zs_ref, bias_ref, c_ref, carry_ref):
    i = pl.program_id(0)

    @pl.when(i == 0)
    def _():
        carry_ref[...] = jnp.zeros_like(carry_ref)

    rows = i * TM + lax.broadcasted_iota(I32, (TM, LANE), 0)
    lf = _log_sigmoid(zs_ref[...] + bias_ref[...])
    lf = jnp.where(rows >= T0, lf, 0.0)
    tri = (lax.broadcasted_iota(I32, (TM, TM), 0)
           >= lax.broadcasted_iota(I32, (TM, TM), 1)).astype(F32)
    c = jnp.dot(tri, lf, precision=HIGHEST, preferred_element_type=F32) + carry_ref[...]
    c_ref[...] = c
    carry_ref[...] = c[TM - 1:TM, :]


def _fox_gate(zs, bias_row):
    return pl.pallas_call(
        _fox_gate_kernel,
        grid=(R // TM,),
        in_specs=[
            pl.BlockSpec((TM, LANE), lambda i: (i, 0)),
            pl.BlockSpec((1, LANE), lambda i: (0, 0)),
        ],
        out_specs=pl.BlockSpec((TM, LANE), lambda i: (i, 0)),
        out_shape=jax.ShapeDtypeStruct((R, LANE), F32),
        scratch_shapes=[pltpu.VMEM((1, LANE), F32)],
        compiler_params=_cparams(("arbitrary",)),
        name="fox_gate",
    )(zs, bias_row)


TQ = TM
N_QB = R // TQ
_PAIRS = [(qi, kj) for qi in range(N_QB) for kj in range(qi + 1)]
N_PAIRS = len(_PAIRS)


LOG2E = 1.4426950408889634
FOX_HPS = FOX_HEADS


def _fox_kernel(qi_tab, kj_tab, q_ref, k_ref, vt_ref, ck_ref, o_ref, m_sc, l_sc, acc_sc):
    p = pl.program_id(1)
    qi = qi_tab[p]
    kj = kj_tab[p]

    @pl.when(kj == 0)
    def _():
        m_sc[...] = jnp.full_like(m_sc, NEG)
        l_sc[...] = jnp.zeros_like(l_sc)
        acc_sc[...] = jnp.zeros_like(acc_sc)

    kpos = kj * TQ + lax.broadcasted_iota(I32, (TQ, 1), 0)
    c1 = FOX_HD ** -0.5 * LOG2E

    def step(causal):
        for hh in range(FOX_HPS):
            lanes = slice(hh * FOX_HD, (hh + 1) * FOX_HD)
            ck = ck_ref[:, SM_FF + hh:SM_FF + hh + 1]
            ckl = jnp.where(kpos >= T0, ck * LOG2E, -NEG)
            t = lax.dot_general(k_ref[:, lanes], q_ref[:, lanes], (((1,), (1,)), ((), ())),
                                preferred_element_type=F32) * c1 - ckl
            if causal:
                ahead = (lax.broadcasted_iota(I32, (TQ, TQ), 0)
                         - lax.broadcasted_iota(I32, (TQ, TQ), 1))
                t = jnp.where(ahead <= 0, t, NEG)
            m_prev = m_sc[hh]
            m_new = jnp.maximum(m_prev, jnp.max(t, axis=0, keepdims=True))
            alpha = jnp.exp2(m_prev - m_new)
            pr = jnp.exp2(t - m_new)
            l_sc[hh] = alpha * l_sc[hh] + jnp.sum(pr, axis=0, keepdims=True)
            acc_sc[hh] = alpha * acc_sc[hh] + jnp.dot(vt_ref[lanes, :], pr.astype(BF16),
                                                      preferred_element_type=F32)
            m_sc[hh] = m_new

    @pl.when(kj == qi)
    def _():
        step(True)

    @pl.when(kj != qi)
    def _():
        step(False)

    @pl.when(kj == qi)
    def _():
        for hh in range(FOX_HPS):
            o_ref[:, hh * FOX_HD:(hh + 1) * FOX_HD] = (acc_sc[hh] / l_sc[hh]).T.astype(o_ref.dtype)


def _fox_attention(zf, vt, c_col):
    qi_tab = jnp.asarray([p[0] for p in _PAIRS], I32)
    kj_tab = jnp.asarray([p[1] for p in _PAIRS], I32)
    hw = FOX_HPS * FOX_HD
    grid_spec = pltpu.PrefetchScalarGridSpec(
        num_scalar_prefetch=2,
        grid=(FOX_HEADS // FOX_HPS, N_PAIRS),
        in_specs=[
            pl.BlockSpec((TQ, hw), lambda h, p, qt, kt: (qt[p], h)),
            pl.BlockSpec((TQ, hw), lambda h, p, qt, kt: (kt[p], FOX_HEADS // FOX_HPS + h)),
            pl.BlockSpec((hw, TQ), lambda h, p, qt, kt: (h, kt[p])),
            pl.BlockSpec((TQ, LANE), lambda h, p, qt, kt: (kt[p], 0)),
        ],
        out_specs=pl.BlockSpec((TQ, hw), lambda h, p, qt, kt: (qt[p], h)),
        scratch_shapes=[
            pltpu.VMEM((FOX_HPS, 1, TQ), F32),
            pltpu.VMEM((FOX_HPS, 1, TQ), F32),
            pltpu.VMEM((FOX_HPS, FOX_HD, TQ), F32),
        ],
    )
    return pl.pallas_call(
        _fox_kernel,
        grid_spec=grid_spec,
        out_shape=jax.ShapeDtypeStruct((R, MIX_W), BF16),
        compiler_params=_cparams(("parallel", "arbitrary")),
        name="fox_attention",
    )(qi_tab, kj_tab, zf, zf, vt, c_col)


GLA_CHUNK = 64
GLA_UNROLL = 5


def _gla_kernel(q_ref, k_ref, v_ref, gr_ref, zs_ref, wa2_ref, ba_ref, gn_ref, o_ref, st_ref, la_ref):
    i = pl.program_id(0)

    @pl.when(i == 0)
    def _():
        st_ref[...] = jnp.zeros_like(st_ref)

    la = jnp.dot(zs_ref[...], wa2_ref[...], precision=HIGHEST, preferred_element_type=F32)
    la_ref[...] = _log_sigmoid(la + ba_ref[...]) * (1.0 / GLA_TAU)

    c_r = lax.broadcasted_iota(I32, (GLA_CHUNK, GLA_CHUNK), 0)
    c_c = lax.broadcasted_iota(I32, (GLA_CHUNK, GLA_CHUNK), 1)
    tri_b = c_r >= c_c
    tri = jnp.where(tri_b, 1.0, 0.0).astype(BF16)

    def chunk(c, carry):
        r0 = pl.multiple_of(c * GLA_CHUNK, GLA_CHUNK)
        rows = pl.ds(r0, GLA_CHUNK)
        g = la_ref[rows, :]
        g_hi = g.astype(BF16)
        g_lo = (g - g_hi.astype(F32)).astype(BF16)
        b = (jnp.dot(tri, g_hi, preferred_element_type=F32)
             + jnp.dot(tri, g_lo, preferred_element_type=F32))
        b_last = b[GLA_CHUNK - 1:GLA_CHUNK, :]
        e_last = jnp.exp(b_last)
        qt = q_ref[rows, :] * (GLA_DK ** -0.5) * jnp.exp(b)
        kt = k_ref[rows, :] * jnp.exp(-b)
        kh = kt * e_last
        for h in range(GLA_HEADS):
            ks = slice(h * GLA_DK, (h + 1) * GLA_DK)
            vs = slice(h * GLA_DV, (h + 1) * GLA_DV)
            q_h = qt[:, ks].astype(BF16)
            k_h = kt[:, ks].astype(BF16)
            kh_h = kh[:, ks].astype(BF16)
            v_h = v_ref[rows, vs]
            att = lax.dot_general(q_h, k_h, (((1,), (1,)), ((), ())), preferred_element_type=F32)
            att = jnp.where(tri_b, att, 0.0)
            st = st_ref[h]
            o = jnp.dot(att.astype(BF16), v_h.astype(BF16), preferred_element_type=F32)
            o = o + lax.dot_general(q_h, st.astype(BF16), (((1,), (1,)), ((), ())),
                                    preferred_element_type=F32)
            st_ref[h] = st * e_last[:, ks] + jnp.dot(v_h.T.astype(BF16), kh_h,
                                                     preferred_element_type=F32)
            ms = jnp.mean(o * o, axis=-1, keepdims=True)
            on = o * lax.rsqrt(ms + LN_EPS) * gn_ref[:, vs]
            gate = gr_ref[rows, vs]
            o_ref[rows, vs] = (on * (gate * _sigmoid(gate))).astype(o_ref.dtype)
        return carry

    lax.fori_loop(0, TM // GLA_CHUNK, chunk, 0, unroll=GLA_UNROLL)


def _gla(z, zs, wa2p, ba, gn):
    return pl.pallas_call(
        _gla_kernel,
        grid=(R // TM,),
        in_specs=[
            pl.BlockSpec((TM, 256), lambda i: (i, C_GQ // 256)),
            pl.BlockSpec((TM, 256), lambda i: (i, C_GK // 256)),
            pl.BlockSpec((TM, 512), lambda i: (i, C_GV // 512)),
            pl.BlockSpec((TM, 512), lambda i: (i, C_GR // 512)),
            pl.BlockSpec((TM, LANE), lambda i: (i, 1)),
            pl.BlockSpec((LANE, 256), lambda i: (0, 0)),
            pl.BlockSpec((1, 256), lambda i: (0, 0)),
            pl.BlockSpec((1, 512), lambda i: (0, 0)),
        ],
        out_specs=pl.BlockSpec((TM, MIX_W), lambda i: (i, 0)),
        out_shape=jax.ShapeDtypeStruct((R, MIX_W), BF16),
        scratch_shapes=[
            pltpu.VMEM((GLA_HEADS, GLA_DV, GLA_DK), F32),
            pltpu.VMEM((TM, GLA_HEADS * GLA_DK), F32),
        ],
        compiler_params=_cparams(("arbitrary",)),
        name="gla",
    )(z, z, z, z, zs, wa2p, ba, gn)


def _local_kernel(cb_ref, cc_ref, cv_ref, pz_ref, cw_ref, pw_ref, ps_ref, oc_ref, od_ref, u_sc, p_sc):
    i = pl.program_id(0)

    @pl.when(i == 0)
    def _():
        u_sc[0:HALO, :] = jnp.zeros((HALO, MIX_W), F32)
        p_sc[0:HALO, :] = jnp.zeros((HALO, MIX_W), F32)

    @pl.when(i > 0)
    def _():
        u_sc[0:HALO, :] = u_sc[TM:TM + HALO, :]
        p_sc[0:HALO, :] = p_sc[TM:TM + HALO, :]

    u = cc_ref[...] * cv_ref[...]
    pz = pz_ref[...]
    u_sc[HALO:, :] = u
    p_sc[HALO:, :] = pz

    y = (cw_ref[2:3, :] * u + cw_ref[1:2, :] * u_sc[HALO - 1:HALO - 1 + TM, :]
         + cw_ref[0:1, :] * u_sc[HALO - 2:HALO - 2 + TM, :])
    oc_ref[...] = (cb_ref[...] * y).astype(oc_ref.dtype)

    tok = i * TM - T0 + lax.broadcasted_iota(I32, (TM, 1), 0)
    cnt_small = jnp.maximum(tok + 1, 1).astype(F32)
    for g, w in enumerate(POOL_WINDOWS):
        cols = slice(g * POOL_GW, (g + 1) * POOL_GW)
        x = pz[:, cols]
        s = x
        for j in range(1, w):
            s = s + p_sc[HALO - j:HALO - j + TM, cols]
        inv_cnt = jnp.where(tok + 1 >= w, 1.0 / w, 1.0 / cnt_small)
        pooled = s * inv_cnt - x
        od = jnp.dot(pooled.astype(BF16), pw_ref[g], preferred_element_type=F32)
        od_ref[:, cols] = (od * ps_ref[:, cols]).astype(od_ref.dtype)


def _local_mixers(z, conv_w, pool_w_bf, pool_scale):
    cw = jnp.zeros((8, MIX_W), F32).at[:CONV_K].set(conv_w)
    blk = lambda c: pl.BlockSpec((TM, MIX_W), lambda i, c=c: (i, c // MIX_W))
    return pl.pallas_call(
        _local_kernel,
        grid=(R // TM,),
        in_specs=[
            blk(C_CB), blk(C_CC), blk(C_CV), blk(C_PZ),
            pl.BlockSpec((8, MIX_W), lambda i: (0, 0)),
            pl.BlockSpec((len(POOL_WINDOWS), POOL_GW, POOL_GW), lambda i: (0, 0, 0)),
            pl.BlockSpec((1, MIX_W), lambda i: (0, 0)),
        ],
        out_specs=[
            pl.BlockSpec((TM, MIX_W), lambda i: (i, 0)),
            pl.BlockSpec((TM, MIX_W), lambda i: (i, 0)),
        ],
        out_shape=[
            jax.ShapeDtypeStruct((R, MIX_W), BF16),
            jax.ShapeDtypeStruct((R, MIX_W), BF16),
        ],
        scratch_shapes=[
            pltpu.VMEM((TM + HALO, MIX_W), F32),
            pltpu.VMEM((TM + HALO, MIX_W), F32),
        ],
        compiler_params=_cparams(("arbitrary",)),
        name="conv_pool",
    )(z, z, z, z, cw, pool_w_bf, pool_scale.reshape(1, -1))


TN_MERGE = 512


def _merge_kernel(hb_ref, oa_ref, ob_ref, oc_ref, od_ref, wg0_ref, wg1_ref, wg2_ref, wg3_ref,
                  gb_ref, wb_ref, wo_ref, out_ref):
    j = pl.program_id(1)

    @pl.when(j == 0)
    def _():
        out_ref[...] = jnp.zeros_like(out_ref)

    hb = hb_ref[...]
    mixed = None
    for b, (o_ref, wg_ref) in enumerate(((oa_ref, wg0_ref), (ob_ref, wg1_ref),
                                         (oc_ref, wg2_ref), (od_ref, wg3_ref))):
        gate = _sigmoid(lax.dot_general(hb, wg_ref[...], _NT, preferred_element_type=F32)
                        + gb_ref[b:b + 1, :])
        proj = jnp.dot(o_ref[...], wb_ref[b], preferred_element_type=F32)
        term = gate * proj
        mixed = term if mixed is None else mixed + term
    out_ref[...] += jnp.dot(mixed.astype(BF16), wo_ref[...], preferred_element_type=F32)


def _merge(hb, o_a, o_b, o_c, o_d, w_all, gate_b, wb_bf, wo_bf, l):
    tn = TN_MERGE
    nj = D_MODEL // tn
    row = lambda w: pl.BlockSpec((TM, w), lambda i, j: (i, 0))
    wg = lambda b: pl.BlockSpec((None, tn, D_MODEL),
                                lambda i, j, b=b: (l, W_GATES // tn + b * nj + j, 0))
    return pl.pallas_call(
        _merge_kernel,
        grid=(R // TM, nj),
        in_specs=[
            row(D_MODEL), row(MIX_W), row(MIX_W), row(MIX_W), row(MIX_W),
            wg(0), wg(1), wg(2), wg(3),
            pl.BlockSpec((N_BRANCH, tn), lambda i, j: (0, j)),
            pl.BlockSpec((None, N_BRANCH, MIX_W, tn), lambda i, j: (l, 0, 0, j)),
            pl.BlockSpec((None, tn, D_MODEL), lambda i, j: (l, j, 0)),
        ],
        out_specs=pl.BlockSpec((TM, D_MODEL), lambda i, j: (i, 0)),
        out_shape=jax.ShapeDtypeStruct((R, D_MODEL), F32),
        compiler_params=_cparams(("parallel", "arbitrary"), vmem=BIG_VMEM_LIMIT),
        name="merge",
    )(hb, o_a, o_b, o_c, o_d, w_all, w_all, w_all, w_all, gate_b, wb_bf, wo_bf)


def _post_ln(h, delta, g, b, row0):
    y = _layer_norm_rows(DEEPNORM_ALPHA * h + delta, g, b)
    rows = row0 + lax.broadcasted_iota(I32, (y.shape[0], 1), 0)
    return jnp.where(rows >= T0, y, 0.0)


def _first_of(cands, target):
    idx = jnp.full(target.shape, len(cands) - 1, I32)
    for j in range(len(cands) - 2, -1, -1):
        idx = jnp.where(cands[j] == target, j, idx)
    return idx


def _pick(cands, idx):
    out = cands[-1]
    for j in range(len(cands) - 2, -1, -1):
        out = jnp.where(idx == j, cands[j], out)
    return out


def _ln1_route_kernel(h_ref, mix_ref, g_ref, b_ref, rwt_ref, rb_ref,
                      h1_ref, mi_ref, mf_ref, cnt_ref, carry_sc):
    i = pl.program_id(0)

    @pl.when(i == 0)
    def _():
        carry_sc[...] = jnp.zeros_like(carry_sc)

    y = _post_ln(h_ref[...], mix_ref[...], g_ref[...], b_ref[...], i * TM)
    h1_ref[...] = y

    logits = lax.dot_general(rwt_ref[...], y, (((1,), (1,)), ((), ())), precision=HIGHEST,
                             preferred_element_type=F32)
    aff = _sigmoid(logits)
    sel = aff + rb_ref[...]
    xs = [sel[j * N_GROUPS:(j + 1) * N_GROUPS, :] for j in range(EXPERTS_PER_GROUP)]
    afs = [aff[j * N_GROUPS:(j + 1) * N_GROUPS, :] for j in range(EXPERTS_PER_GROUP)]

    score = None
    for a in range(EXPERTS_PER_GROUP):
        for bb in range(a + 1, EXPERTS_PER_GROUP):
            pair = xs[a] + xs[bb]
            score = pair if score is None else jnp.maximum(score, pair)
    giota = lax.broadcasted_iota(I32, (N_GROUPS, TM), 0)
    gmax = jnp.max(score, axis=0, keepdims=True)
    grp = jnp.min(jnp.where(score == gmax, giota, N_GROUPS), axis=0, keepdims=True)
    gsel = giota == grp
    cs = [jnp.max(jnp.where(gsel, x, -jnp.inf), axis=0, keepdims=True) for x in xs]
    acs = [jnp.sum(jnp.where(gsel, a, 0.0), axis=0, keepdims=True) for a in afs]

    m1 = jnp.maximum(jnp.maximum(cs[0], cs[1]), jnp.maximum(cs[2], cs[3]))
    i0 = _first_of(cs, m1)
    ds = [jnp.where(i0 == j, -jnp.inf, cs[j]) for j in range(EXPERTS_PER_GROUP)]
    m2 = jnp.maximum(jnp.maximum(ds[0], ds[1]), jnp.maximum(ds[2], ds[3]))
    i1 = _first_of(ds, m2)
    a0 = _pick(acs, i0)
    a1 = _pick(acs, i1)
    denom = a0 + a1

    pos = i * TM + lax.broadcasted_iota(I32, (1, TM), 1)
    valid = pos >= T0
    riota = lax.broadcasted_iota(I32, (N_EXPERTS, TM), 0)
    oh0 = (riota == i0 * N_GROUPS + grp) & valid
    oh1 = (riota == i1 * N_GROUPS + grp) & valid
    ohf = jnp.where(oh0 | oh1, 1.0, 0.0)
    before = (lax.broadcasted_iota(I32, (TM, TM), 0)
              < lax.broadcasted_iota(I32, (TM, TM), 1)).astype(BF16)
    cum = jnp.dot(ohf.astype(BF16), before, preferred_element_type=F32) + carry_sc[...]
    rank0 = jnp.sum(jnp.where(oh0, cum, 0.0), axis=0, keepdims=True)
    rank1 = jnp.sum(jnp.where(oh1, cum, 0.0), axis=0, keepdims=True)
    carry = carry_sc[...] + jnp.sum(ohf, axis=1, keepdims=True)
    carry_sc[...] = carry
    cnt_ref[...] = jnp.broadcast_to(carry, cnt_ref.shape)

    zi = jnp.zeros((1, TM), I32)
    mi_ref[...] = jnp.concatenate(
        [grp * EXPERTS_PER_GROUP + i0, grp * EXPERTS_PER_GROUP + i1,
         rank0.astype(I32), rank1.astype(I32), zi, zi, zi, zi], axis=0)
    zf = jnp.zeros((1, TM), F32)
    mf_ref[...] = jnp.concatenate([a0 / denom, a1 / denom, zf, zf, zf, zf, zf, zf], axis=0)


def _ln1_route(h, mix, g, b, router_wt, router_bc):
    row = pl.BlockSpec((TM, D_MODEL), lambda i: (i, 0))
    vec = pl.BlockSpec((1, D_MODEL), lambda i: (0, 0))
    meta = pl.BlockSpec((8, TM), lambda i: (0, i))
    return pl.pallas_call(
        _ln1_route_kernel,
        grid=(R // TM,),
        in_specs=[row, row, vec, vec,
                  pl.BlockSpec((N_EXPERTS, D_MODEL), lambda i: (0, 0)),
                  pl.BlockSpec((N_EXPERTS, 1), lambda i: (0, 0))],
        out_specs=[row, meta, meta, pl.BlockSpec((N_EXPERTS, LANE), lambda i: (0, 0))],
        out_shape=[
            jax.ShapeDtypeStruct((R, D_MODEL), F32),
            jax.ShapeDtypeStruct((8, R), I32),
            jax.ShapeDtypeStruct((8, R), F32),
            jax.ShapeDtypeStruct((N_EXPERTS, LANE), F32),
        ],
        scratch_shapes=[pltpu.VMEM((N_EXPERTS, 1), F32)],
        compiler_params=_cparams(("arbitrary",)),
        name="ln1_route",
    )(h, mix, g.reshape(1, -1), b.reshape(1, -1), router_wt, router_bc)


def _dispatch_tables(mi, counts_slot_major):
    counts = counts_slot_major.reshape(EXPERTS_PER_GROUP, N_GROUPS).T.reshape(N_EXPERTS).astype(I32)
    padded = (counts + EXPERT_BLOCK - 1) // EXPERT_BLOCK * EXPERT_BLOCK
    pad_end = jnp.cumsum(padded)
    pad_start = pad_end - padded
    e_iota = jnp.arange(N_EXPERTS, dtype=I32)
    rows_ok = jnp.arange(R) >= T0

    def dest(eid, rank):
        start = jnp.sum(jnp.where(eid[:, None] == e_iota[None, :], pad_start[None, :], 0), axis=1)
        return jnp.where(rows_ok, start + rank, 0).astype(I32)

    d0 = dest(mi[0], mi[2])
    d1 = dest(mi[1], mi[3])
    blk_start = (pad_start // EXPERT_BLOCK).astype(I32)
    n_blk = (padded // EXPERT_BLOCK).astype(I32)
    n_used = (pad_end[-1] // EXPERT_BLOCK).astype(I32).reshape(1)
    tok_rows = jnp.arange(T0, R, dtype=I32)
    row_src = jnp.zeros((N_ROWS + UP_AHEAD * EXPERT_BLOCK,), I32).at[
        jnp.concatenate([d0[T0:], d1[T0:]])].set(jnp.concatenate([tok_rows, tok_rows]),
                                                 unique_indices=True)
    blk = jnp.arange(N_BLOCKS + UP_AHEAD, dtype=I32)
    owner = (blk[:, None] >= blk_start[None, :]) & (blk[:, None] < (blk_start + n_blk)[None, :])
    left = counts[None, :] - (blk[:, None] - blk_start[None, :]) * EXPERT_BLOCK
    n_valid = jnp.sum(jnp.where(owner, jnp.clip(left, 0, EXPERT_BLOCK), 0), axis=1).astype(I32)
    return d0, d1, blk_start, n_blk, n_used, row_src, n_valid


def _row_copy(src, src_row, dst, dst_row, sem):
    return pltpu.make_async_copy(src.at[pl.ds(src_row, 1), :], dst.at[pl.ds(dst_row, 1), :], sem)


def _issue_row_gather(rs_ref, nv_ref, g, h_hbm, buf, sem, full=False):
    base = g * EXPERT_BLOCK
    nv = nv_ref[g]
    for r in range(EXPERT_BLOCK):
        if full:
            _row_copy(h_hbm, rs_ref[base + r], buf, r, sem).start(priority=r % 2)
        else:
            @pl.when(r < nv)
            def _():
                _row_copy(h_hbm, rs_ref[base + r], buf, r, sem).start(priority=r % 2)


def _wait_row_gather(nv_ref, g, h_hbm, buf, sem, full=False):
    if full:
        pltpu.make_async_copy(h_hbm.at[pl.ds(0, EXPERT_BLOCK), :], buf, sem).wait()
        return
    nv = nv_ref[g]
    k = EXPERT_BLOCK
    while k >= 1:
        @pl.when((nv & k) != 0)
        def _():
            pltpu.make_async_copy(h_hbm.at[pl.ds(0, k), :], buf.at[pl.ds(0, k), :], sem).wait()
        k //= 2


def _block_rows(g):
    return pl.ds(pl.multiple_of(g * EXPERT_BLOCK, EXPERT_BLOCK), EXPERT_BLOCK)


def _finish_writes(out_copy, obuf, nu):
    n_slots = obuf.shape[0]
    for back in range(1, n_slots + 1):
        @pl.when(nu >= back)
        def _():
            out_copy(nu - back, (nu - back) % n_slots).wait()

    obuf[0] = jnp.zeros(obuf.shape[1:], obuf.dtype)

    def zero_block(g, carry):
        cp = out_copy(g, 0)
        cp.start()
        cp.wait()
        return carry

    lax.fori_loop(nu, N_BLOCKS, zero_block, 0)


UP_AHEAD = 2
UP_X_SLOTS = UP_AHEAD + 1


def _moe_up_kernel(bs_ref, nb_ref, nu_ref, rs_ref, nv_ref, h_hbm, wg_ref, wu_ref, o_hbm,
                   wg_sc, wu_sc, xbuf, xb_sc, obuf, xsem, osem):
    e = pl.program_id(0)
    nb = nb_ref[e]
    g0 = bs_ref[e]

    def out_copy(g, slot):
        return pltpu.make_async_copy(obuf.at[slot], o_hbm.at[_block_rows(g), :], osem.at[slot])

    @pl.when(e == 0)
    def _():
        xbuf[...] = jnp.zeros_like(xbuf)
        for g in range(UP_AHEAD):
            _issue_row_gather(rs_ref, nv_ref, g, h_hbm, xbuf.at[g], xsem.at[g])

    @pl.when(nb > 0)
    def _():
        wg_sc[...] = wg_ref[...].astype(BF16)
        wu_sc[...] = wu_ref[...].astype(BF16)

        def block(j, carry, full):
            g = g0 + j
            slot = g % 2
            xs = g % UP_X_SLOTS
            xs_next = (g + UP_AHEAD) % UP_X_SLOTS

            @pl.when(g >= 2)
            def _():
                out_copy(g - 2, slot).wait()

            _wait_row_gather(nv_ref, g, h_hbm, xbuf.at[xs], xsem.at[xs], full)
            xb_sc[...] = xbuf[xs].astype(BF16)
            _issue_row_gather(rs_ref, nv_ref, g + UP_AHEAD, h_hbm, xbuf.at[xs_next],
                              xsem.at[xs_next], full)
            x = xb_sc[...]
            gate = jnp.dot(x, wg_sc[...], preferred_element_type=F32)
            up = jnp.dot(x, wu_sc[...], preferred_element_type=F32)
            obuf[slot] = (gate * _sigmoid(gate) * up).astype(BF16)
            out_copy(g, slot).start()
            return carry

        n_full = jnp.maximum(nb - 1 - UP_AHEAD, 0)
        lax.fori_loop(0, n_full, lambda j, c: block(j, c, True), 0)
        lax.fori_loop(n_full, nb, lambda j, c: block(j, c, False), 0)

    @pl.when(e == N_EXPERTS - 1)
    def _():
        _finish_writes(out_copy, obuf, nu_ref[0])


def _moe_up(blk_start, n_blk, n_used, row_src, n_valid, h1, w_gate, w_up, l):
    wspec = pl.BlockSpec((None, None, D_MODEL, D_EXPERT), lambda e, *_: (l, e, 0, 0))
    grid_spec = pltpu.PrefetchScalarGridSpec(
        num_scalar_prefetch=5,
        grid=(N_EXPERTS,),
        in_specs=[pl.BlockSpec(memory_space=pl.ANY), wspec, wspec],
        out_specs=pl.BlockSpec(memory_space=pl.ANY),
        scratch_shapes=[
            pltpu.VMEM((D_MODEL, D_EXPERT), BF16),
            pltpu.VMEM((D_MODEL, D_EXPERT), BF16),
            pltpu.VMEM((UP_X_SLOTS, EXPERT_BLOCK, D_MODEL), F32),
            pltpu.VMEM((EXPERT_BLOCK, D_MODEL), BF16),
            pltpu.VMEM((2, EXPERT_BLOCK, D_EXPERT), BF16),
            pltpu.SemaphoreType.DMA((UP_X_SLOTS,)),
            pltpu.SemaphoreType.DMA((2,)),
        ],
    )
    return pl.pallas_call(
        _moe_up_kernel,
        grid_spec=grid_spec,
        out_shape=jax.ShapeDtypeStruct((N_ROWS, D_EXPERT), BF16),
        compiler_params=_cparams(("arbitrary",), vmem=MOE_VMEM_LIMIT),
        name="moe_up",
    )(blk_start, n_blk, n_used, row_src, n_valid, h1, w_gate, w_up)


DOWN_IN_SLOTS = 4
DOWN_OUT_SLOTS = 3


def _moe_down_kernel(bs_ref, nb_ref, nu_ref, x_hbm, wd_ref, y_hbm, wd_sc, xbuf, obuf, xsem, osem):
    e = pl.program_id(0)
    nb = nb_ref[e]
    g0 = bs_ref[e]
    nu = nu_ref[0]

    def in_copy(g):
        s = g % DOWN_IN_SLOTS
        return pltpu.make_async_copy(x_hbm.at[_block_rows(g), :], xbuf.at[s], xsem.at[s])

    def out_copy(g, slot):
        return pltpu.make_async_copy(obuf.at[slot], y_hbm.at[_block_rows(g), :], osem.at[slot])

    @pl.when(e == 0)
    def _():
        for g in range(DOWN_IN_SLOTS - 1):
            @pl.when(g < nu)
            def _():
                in_copy(g).start()

    @pl.when(nb > 0)
    def _():
        wd_sc[...] = wd_ref[...].astype(BF16)

        def block(j, carry):
            g = g0 + j
            slot = g % DOWN_OUT_SLOTS

            @pl.when(g >= DOWN_OUT_SLOTS)
            def _():
                out_copy(g - DOWN_OUT_SLOTS, slot).wait()

            in_copy(g).wait()

            @pl.when(g + DOWN_IN_SLOTS - 1 < nu)
            def _():
                in_copy(g + DOWN_IN_SLOTS - 1).start()

            obuf[slot] = jnp.dot(xbuf[g % DOWN_IN_SLOTS], wd_sc[...], preferred_element_type=F32)
            out_copy(g, slot).start()
            return carry

        lax.fori_loop(0, nb, block, 0)

    @pl.when(e == N_EXPERTS - 1)
    def _():
        _finish_writes(out_copy, obuf, nu)


def _moe_down(blk_start, n_blk, n_used, hmid, w_down, l):
    grid_spec = pltpu.PrefetchScalarGridSpec(
        num_scalar_prefetch=3,
        grid=(N_EXPERTS,),
        in_specs=[
            pl.BlockSpec(memory_space=pl.ANY),
            pl.BlockSpec((None, None, D_EXPERT, D_MODEL), lambda e, *_: (l, e, 0, 0)),
        ],
        out_specs=pl.BlockSpec(memory_space=pl.ANY),
        scratch_shapes=[
            pltpu.VMEM((D_EXPERT, D_MODEL), BF16),
            pltpu.VMEM((DOWN_IN_SLOTS, EXPERT_BLOCK, D_EXPERT), BF16),
            pltpu.VMEM((DOWN_OUT_SLOTS, EXPERT_BLOCK, D_MODEL), F32),
            pltpu.SemaphoreType.DMA((DOWN_IN_SLOTS,)),
            pltpu.SemaphoreType.DMA((DOWN_OUT_SLOTS,)),
        ],
    )
    return pl.pallas_call(
        _moe_down_kernel,
        grid_spec=grid_spec,
        out_shape=jax.ShapeDtypeStruct((N_ROWS, D_MODEL), F32),
        compiler_params=_cparams(("arbitrary",), vmem=MOE_VMEM_LIMIT),
        name="moe_down",
    )(blk_start, n_blk, n_used, hmid, w_down)


def _combine_kernel(d0_ref, d1_ref, y_hbm, h1_ref, mf_ref, g_ref, b_ref, h2_ref, h2b_ref,
                    buf0, buf1, sem):
    i = pl.program_id(0)
    slot = i % 2

    def issue(tile, s):
        base = tile * LANE

        def body(r, carry):
            _row_copy(y_hbm, d0_ref[base + r], buf0.at[s], r, sem.at[s]).start(priority=0)
            _row_copy(y_hbm, d1_ref[base + r], buf1.at[s], r, sem.at[s]).start(priority=1)
            return carry

        lax.fori_loop(0, LANE, body, 0)

    @pl.when(i == 0)
    def _():
        issue(0, 0)

    @pl.when(i + 1 < pl.num_programs(0))
    def _():
        issue(i + 1, 1 - slot)

    def drain(r, carry):
        _row_copy(y_hbm, 0, buf0.at[slot], r, sem.at[slot]).wait()
        _row_copy(y_hbm, 0, buf1.at[slot], r, sem.at[slot]).wait()
        return carry

    lax.fori_loop(0, LANE, drain, 0)
    wt = mf_ref[...].T
    ffn = wt[:, 0:1] * buf0[slot] + wt[:, 1:2] * buf1[slot]
    y = _post_ln(h1_ref[...], ffn, g_ref[...], b_ref[...], i * LANE)
    h2_ref[...] = y
    if h2b_ref is not None:
        h2b_ref[...] = y.astype(BF16)


def _combine_last_kernel(d0_ref, d1_ref, y_hbm, h1_ref, mf_ref, g_ref, b_ref, out_ref, buf0, buf1, sem):
    _combine_kernel(d0_ref, d1_ref, y_hbm, h1_ref, mf_ref, g_ref, b_ref, out_ref, None, buf0, buf1, sem)


def _combine_ln2(d0, d1, y_rows, h1, mf, g, b, last):
    row = lambda i, d0, d1: (i, 0)
    vec = pl.BlockSpec((1, D_MODEL), lambda i, d0, d1: (0, 0))
    if last:
        frames = lambda i, d0, d1: (jnp.maximum(i - 1, 0), 0)
        out_specs = [pl.BlockSpec((LANE, D_MODEL), frames)]
        out_shape = [jax.ShapeDtypeStruct((SEQ, D_MODEL), F32)]
    else:
        out_specs = [pl.BlockSpec((LANE, D_MODEL), row), pl.BlockSpec((LANE, D_MODEL), row)]
        out_shape = [jax.ShapeDtypeStruct((R, D_MODEL), F32), jax.ShapeDtypeStruct((R, D_MODEL), BF16)]
    grid_spec = pltpu.PrefetchScalarGridSpec(
        num_scalar_prefetch=2,
        grid=(R // LANE,),
        in_specs=[
            pl.BlockSpec(memory_space=pl.ANY),
            pl.BlockSpec((LANE, D_MODEL), row),
            pl.BlockSpec((8, LANE), lambda i, d0, d1: (0, i)),
            vec, vec,
        ],
        out_specs=out_specs,
        scratch_shapes=[
            pltpu.VMEM((2, LANE, D_MODEL), F32),
            pltpu.VMEM((2, LANE, D_MODEL), F32),
            pltpu.SemaphoreType.DMA((2,)),
        ],
    )
    return pl.pallas_call(
        _combine_last_kernel if last else _combine_kernel,
        grid_spec=grid_spec,
        out_shape=out_shape,
        compiler_params=_cparams(("arbitrary",)),
        name="moe_combine_ln2",
    )(d0, d1, y_rows, h1, mf, g.reshape(1, -1), b.reshape(1, -1))


def kernel(x, meta_tokens, ln_in_g, ln_in_b, w_in, fox_f_bias, gla_wa2, gla_ba, gla_norm_g, conv_w, pool_w, pool_scale, gate_b, w_branch, w_out, ln1_g, ln1_b, router_w, router_b, w_gate, w_up, w_down, ln2_g, ln2_b):
    assert x.shape == (1, SEQ, D_MODEL)
    h, hb = _ln_in(x.reshape(SEQ, D_MODEL), meta_tokens, ln_in_g, ln_in_b)
    router_wt = router_w.T.reshape(N_GROUPS, EXPERTS_PER_GROUP, D_MODEL).transpose(1, 0, 2).reshape(
        N_EXPERTS, D_MODEL)
    router_bc = router_b.astype(F32).reshape(N_GROUPS, EXPERTS_PER_GROUP).T.reshape(N_EXPERTS, 1)

    wb_bf = w_branch.astype(BF16)
    wo_bf = w_out.astype(BF16)

    w_all = _transposed_weights(w_in)

    for l in range(DEPTH):
        zf = _matmul_nt(hb, w_all, l, W_FOX, 2 * MIX_W, BF16, TM_PROJ, 512, "proj_fox")
        vt = _matmul_tt(w_all, l, W_FOX + 2 * MIX_W, MIX_W, hb, BF16, TM_PROJ, "proj_fox_vt")
        z = _matmul_nt(hb, w_all, l, W_MIX, N_MIXC, F32, TM_PROJ, 512, "proj_mix")
        zs = _matmul_nt(hb, w_all, l, W_SMALL, 2 * LANE, F32, TM_PROJ, 2 * LANE, "proj_small")

        bias_row = jnp.zeros((1, LANE), F32).at[0, SM_FF:SM_FF + FOX_HEADS].set(fox_f_bias[l])
        c = _fox_gate(zs, bias_row)
        o_a = _fox_attention(zf, vt, c)

        wa2p = jnp.zeros((LANE, GLA_HEADS * GLA_DK), F32).at[SM_GA:SM_GA + GLA_RANK].set(gla_wa2[l])
        o_b = _gla(z, zs, wa2p, gla_ba[l].reshape(1, -1), gla_norm_g[l].reshape(1, -1))

        o_c, o_d = _local_mixers(z, conv_w[l], pool_w[l].astype(BF16), pool_scale[l])

        mix = _merge(hb, o_a, o_b, o_c, o_d, w_all, gate_b[l], wb_bf, wo_bf, l)
        h1, mi, mf, counts = _ln1_route(h, mix, ln1_g[l], ln1_b[l], router_wt, router_bc)

        d0, d1, blk_start, n_blk, n_used, row_src, n_valid = _dispatch_tables(mi, counts[:, 0])
        hmid = _moe_up(blk_start, n_blk, n_used, row_src, n_valid, h1, w_gate, w_up, l)
        y_rows = _moe_down(blk_start, n_blk, n_used, hmid, w_down, l)
        if l + 1 < DEPTH:
            h, hb = _combine_ln2(d0, d1, y_rows, h1, mf, ln2_g[l], ln2_b[l], last=False)
        else:
            (out,) = _combine_ln2(d0, d1, y_rows, h1, mf, ln2_g[l], ln2_b[l], last=True)

    return out.reshape(1, SEQ, D_MODEL)
```

```python
import jax
import jax.numpy as jnp
import numpy as np
from jax import lax
from jax.experimental import pallas as pl
from jax.experimental.pallas import tpu as pltpu

F32 = jnp.float32
BF16 = jnp.bfloat16
I32 = jnp.int32
HIGHEST = lax.Precision.HIGHEST

D_MODEL = 2048
SEQ = 8192
DEPTH = 2
N_META = 16
N_BRANCH = 4
MIX_W = 512
FOX_HEADS = 4
FOX_HD = 128
GLA_HEADS = 4
GLA_DK = 64
GLA_DV = 128
GLA_RANK = 16
GLA_TAU = 16.0
CONV_K = 3
POOL_WINDOWS = (2, 4, 8, 16)
POOL_GW = 128
N_EXPERTS = 32
N_GROUPS = 8
EXPERTS_PER_GROUP = 4
TOP_K = 2
D_EXPERT = 1024
LN_EPS = 1e-5
DEEPNORM_ALPHA = (2 * DEPTH) ** 0.25

_SPLITS = (512, 512, 512, 4, 256, 256, 512, 16, 512, 512, 512, 512, 512, 8192)
_OFFS = [int(o) for o in np.concatenate([[0], np.cumsum(_SPLITS)])]
(O_FQ, O_FK, O_FV, O_FF, O_GQ, O_GK, O_GV, O_GA, O_GR, O_CB, O_CC, O_CV, O_PZ, O_GZ, P_IN) = _OFFS

LANE = 128
PAD_ROWS = LANE - N_META
T0 = PAD_ROWS
N_TOK = N_META + SEQ
R = PAD_ROWS + N_TOK
TM = 640
TM_PROJ = 1664
HALO = 16

WT_TILE = 128
W_FOX, W_MIX, W_SMALL, W_GATES = 0, 1536, 5120, 5376
N_FOX, N_MIXC, N_SMALL, N_GATES = 1536, 3584, 2 * LANE, 8192
N_WT = W_GATES + N_GATES
C_GQ, C_GK, C_GV, C_GR, C_CB, C_CC, C_CV, C_PZ = 0, 256, 512, 1024, 1536, 2048, 2560, 3072
C_SM = N_MIXC
SM_FF_TILE, SM_GA_TILE = O_FF // LANE, O_GA // LANE
SM_FF = O_FF - SM_FF_TILE * LANE
SM_GA = O_GA - SM_GA_TILE * LANE


def _wt_sources():
    src = []
    for r in range(0, N_WT, WT_TILE):
        if r < N_FOX:
            src.append(O_FQ + r)
        elif r < W_MIX + 1024:
            src.append(O_GQ + r - W_MIX)
        elif r < W_SMALL:
            src.append(O_GR + r - (W_MIX + 1024))
        elif r < W_GATES:
            src.append((SM_FF_TILE, SM_GA_TILE)[(r - W_SMALL) // LANE] * LANE)
        else:
            src.append(O_GZ + r - W_GATES)
    assert all(0 <= s and s + WT_TILE <= P_IN for s in src)
    return src


WT_SRC = _wt_sources()

EXPERT_BLOCK = 128
N_FLAT = N_TOK * TOP_K
N_BLOCKS = -(-N_FLAT // EXPERT_BLOCK) + N_EXPERTS
N_ROWS = N_BLOCKS * EXPERT_BLOCK

NEG = -1e30
VMEM_LIMIT = 48 * 1024 * 1024
BIG_VMEM_LIMIT = 56 * 1024 * 1024
MOE_VMEM_LIMIT = BIG_VMEM_LIMIT


def _cparams(sem, vmem=VMEM_LIMIT):
    return pltpu.CompilerParams(dimension_semantics=sem, vmem_limit_bytes=vmem)


def _log_sigmoid(x):
    return jnp.minimum(x, 0.0) - jnp.log1p(jnp.exp(-jnp.abs(x)))


def _sigmoid(x):
    return 1.0 / (1.0 + jnp.exp(-x))


def _layer_norm_rows(x, g, b):
    mu = jnp.mean(x, axis=-1, keepdims=True)
    xc = x - mu
    var = jnp.mean(xc * xc, axis=-1, keepdims=True)
    return xc * lax.rsqrt(var + LN_EPS) * g + b


def _ln_in_kernel(x_ref, meta_ref, g_ref, b_ref, h_ref, hb_ref):
    i = pl.program_id(0)

    @pl.when(i == 0)
    def _():
        h_ref[...] = jnp.zeros_like(h_ref)
        hb_ref[...] = jnp.zeros_like(hb_ref)
        m = _layer_norm_rows(meta_ref[...], g_ref[...], b_ref[...])
        h_ref[PAD_ROWS:, :] = m
        hb_ref[PAD_ROWS:, :] = m.astype(BF16)

    @pl.when(i > 0)
    def _():
        y = _layer_norm_rows(x_ref[...], g_ref[...], b_ref[...])
        h_ref[...] = y
        hb_ref[...] = y.astype(BF16)


def _ln_in(x2d, meta, g, b):
    nb = R // LANE
    return pl.pallas_call(
        _ln_in_kernel,
        grid=(nb,),
        in_specs=[
            pl.BlockSpec((LANE, D_MODEL), lambda i: (jnp.maximum(i - 1, 0), 0)),
            pl.BlockSpec((N_META, D_MODEL), lambda i: (0, 0)),
            pl.BlockSpec((1, D_MODEL), lambda i: (0, 0)),
            pl.BlockSpec((1, D_MODEL), lambda i: (0, 0)),
        ],
        out_specs=[
            pl.BlockSpec((LANE, D_MODEL), lambda i: (i, 0)),
            pl.BlockSpec((LANE, D_MODEL), lambda i: (i, 0)),
        ],
        out_shape=[
            jax.ShapeDtypeStruct((R, D_MODEL), F32),
            jax.ShapeDtypeStruct((R, D_MODEL), BF16),
        ],
        compiler_params=_cparams(("arbitrary",)),
        name="ln_in",
    )(x2d, meta, g.reshape(1, -1), b.reshape(1, -1))


D_CHUNKS = D_MODEL // LANE
FLAT_ROWS = D_CHUNKS * DEPTH


def _wt_kernel(src_ref, w_hbm, o_ref, buf, sem):
    j = pl.program_id(0)
    slot = j % 2

    def tile_copy(t, s):
        row0 = pl.multiple_of(src_ref[t] * FLAT_ROWS, FLAT_ROWS)
        return pltpu.make_async_copy(w_hbm.at[pl.ds(row0, WT_TILE * FLAT_ROWS), :], buf.at[s],
                                     sem.at[s])

    @pl.when(j == 0)
    def _():
        tile_copy(0, 0).start()

    @pl.when(j + 1 < pl.num_programs(0))
    def _():
        tile_copy(j + 1, 1 - slot).start()

    tile_copy(j, slot).wait()
    for l in range(DEPTH):
        for c in range(D_CHUNKS):
            o_ref[l, :, c * LANE:(c + 1) * LANE] = buf[
                slot, pl.ds(c * DEPTH + l, WT_TILE, stride=FLAT_ROWS), :].astype(BF16)


def _transposed_weights(w_in):
    flat = w_in.reshape(DEPTH, D_CHUNKS, LANE, P_IN).transpose(3, 1, 0, 2).reshape(
        P_IN * FLAT_ROWS, LANE)
    grid_spec = pltpu.PrefetchScalarGridSpec(
        num_scalar_prefetch=1,
        grid=(N_WT // WT_TILE,),
        in_specs=[pl.BlockSpec(memory_space=pl.ANY)],
        out_specs=pl.BlockSpec((DEPTH, WT_TILE, D_MODEL), lambda j, src: (0, j, 0)),
        scratch_shapes=[
            pltpu.VMEM((2, WT_TILE * FLAT_ROWS, LANE), F32),
            pltpu.SemaphoreType.DMA((2,)),
        ],
    )
    return pl.pallas_call(
        _wt_kernel,
        grid_spec=grid_spec,
        out_shape=jax.ShapeDtypeStruct((DEPTH, N_WT, D_MODEL), BF16),
        compiler_params=_cparams(("arbitrary",)),
        name="transposed_weights",
    )(jnp.asarray(WT_SRC, I32), flat)


_NT = (((1,), (1,)), ((), ()))


def _mm_nt_kernel(a_ref, wt_ref, o_ref):
    o_ref[...] = lax.dot_general(a_ref[...], wt_ref[...], _NT,
                                 preferred_element_type=F32).astype(o_ref.dtype)


def _matmul_nt(a, wt, l, row0, n, out_dtype, tm, tn, name):
    m, k = a.shape
    return pl.pallas_call(
        _mm_nt_kernel,
        grid=(m // tm, n // tn),
        in_specs=[
            pl.BlockSpec((tm, k), lambda i, j: (i, 0)),
            pl.BlockSpec((None, tn, k), lambda i, j: (l, row0 // tn + j, 0)),
        ],
        out_specs=pl.BlockSpec((tm, tn), lambda i, j: (i, j)),
        out_shape=jax.ShapeDtypeStruct((m, n), out_dtype),
        compiler_params=_cparams(("parallel", "arbitrary")),
        name=name,
    )(a, wt)


def _matmul_tt(wt, l, row0, n, a, out_dtype, tm, name):
    m, k = a.shape
    return pl.pallas_call(
        _mm_nt_kernel,
        grid=(m // tm,),
        in_specs=[
            pl.BlockSpec((None, n, k), lambda i: (l, row0 // n, 0)),
            pl.BlockSpec((tm, k), lambda i: (i, 0)),
        ],
        out_specs=pl.BlockSpec((n, tm), lambda i: (0, i)),
        out_shape=jax.ShapeDtypeStruct((n, m), out_dtype),
        compiler_params=_cparams(("parallel",)),
        name=name,
    )(wt, a)


def _fox_gate_kernel(zs_ref, bias_ref, c_ref, carry_ref):
    i = pl.program_id(0)

    @pl.when(i == 0)
    def _():
        carry_ref[...] = jnp.zeros_like(carry_ref)

    rows = i * TM + lax.broadcasted_iota(I32, (TM, LANE), 0)
    lf = _log_sigmoid(zs_ref[...] + bias_ref[...])
    lf = jnp.where(rows >= T0, lf, 0.0)
    tri = (lax.broadcasted_iota(I32, (TM, TM), 0)
           >= lax.broadcasted_iota(I32, (TM, TM), 1)).astype(F32)
    c = jnp.dot(tri, lf, precision=HIGHEST, preferred_element_type=F32) + carry_ref[...]
    c_ref[...] = c
    carry_ref[...] = c[TM - 1:TM, :]


def _fox_gate(z, bias_row):
    return pl.pallas_call(
        _fox_gate_kernel,
        grid=(R // TM,),
        in_specs=[
            pl.BlockSpec((TM, LANE), lambda i: (i, C_SM // LANE)),
            pl.BlockSpec((1, LANE), lambda i: (0, 0)),
        ],
        out_specs=pl.BlockSpec((TM, LANE), lambda i: (i, 0)),
        out_shape=jax.ShapeDtypeStruct((R, LANE), F32),
        scratch_shapes=[pltpu.VMEM((1, LANE), F32)],
        compiler_params=_cparams(("arbitrary",)),
        name="fox_gate",
    )(z, bias_row)


TQ = TM
N_QB = R // TQ
_PAIRS = [(qi, kj) for qi in range(N_QB) for kj in range(qi + 1)]
N_PAIRS = len(_PAIRS)


LOG2E = 1.4426950408889634
FOX_HPS = FOX_HEADS


def _fox_kernel(qi_tab, kj_tab, q_ref, k_ref, vt_ref, ck_ref, o_ref, m_sc, l_sc, acc_sc):
    p = pl.program_id(1)
    qi = qi_tab[p]
    kj = kj_tab[p]

    @pl.when(kj == 0)
    def _():
        m_sc[...] = jnp.full_like(m_sc, NEG)
        l_sc[...] = jnp.zeros_like(l_sc)
        acc_sc[...] = jnp.zeros_like(acc_sc)

    kpos = kj * TQ + lax.broadcasted_iota(I32, (TQ, 1), 0)
    c1 = FOX_HD ** -0.5 * LOG2E

    def step(causal):
        for hh in range(FOX_HPS):
            lanes = slice(hh * FOX_HD, (hh + 1) * FOX_HD)
            ck = ck_ref[:, SM_FF + hh:SM_FF + hh + 1]
            ckl = jnp.where(kpos >= T0, ck * LOG2E, -NEG)
            t = lax.dot_general(k_ref[:, lanes], q_ref[:, lanes], (((1,), (1,)), ((), ())),
                                preferred_element_type=F32) * c1 - ckl
            if causal:
                ahead = (lax.broadcasted_iota(I32, (TQ, TQ), 0)
                         - lax.broadcasted_iota(I32, (TQ, TQ), 1))
                t = jnp.where(ahead <= 0, t, NEG)
            m_prev = m_sc[hh]
            m_new = jnp.maximum(m_prev, jnp.max(t, axis=0, keepdims=True))
            alpha = jnp.exp2(m_prev - m_new)
            pr = jnp.exp2(t - m_new)
            l_sc[hh] = alpha * l_sc[hh] + jnp.sum(pr, axis=0, keepdims=True)
            acc_sc[hh] = alpha * acc_sc[hh] + jnp.dot(vt_ref[lanes, :], pr.astype(BF16),
                                                      preferred_element_type=F32)
            m_sc[hh] = m_new

    @pl.when(kj == qi)
    def _():
        step(True)

    @pl.when(kj != qi)
    def _():
        step(False)

    @pl.when(kj == qi)
    def _():
        for hh in range(FOX_HPS):
            o_ref[:, hh * FOX_HD:(hh + 1) * FOX_HD] = (acc_sc[hh] / l_sc[hh]).T.astype(o_ref.dtype)


def _fox_attention(zf, vt, c_col):
    qi_tab = jnp.asarray([p[0] for p in _PAIRS], I32)
    kj_tab = jnp.asarray([p[1] for p in _PAIRS], I32)
    hw = FOX_HPS * FOX_HD
    grid_spec = pltpu.PrefetchScalarGridSpec(
        num_scalar_prefetch=2,
        grid=(FOX_HEADS // FOX_HPS, N_PAIRS),
        in_specs=[
            pl.BlockSpec((TQ, hw), lambda h, p, qt, kt: (qt[p], h)),
            pl.BlockSpec((TQ, hw), lambda h, p, qt, kt: (kt[p], FOX_HEADS // FOX_HPS + h)),
            pl.BlockSpec((hw, TQ), lambda h, p, qt, kt: (h, kt[p])),
            pl.BlockSpec((TQ, LANE), lambda h, p, qt, kt: (kt[p], 0)),
        ],
        out_specs=pl.BlockSpec((TQ, hw), lambda h, p, qt, kt: (qt[p], h)),
        scratch_shapes=[
            pltpu.VMEM((FOX_HPS, 1, TQ), F32),
            pltpu.VMEM((FOX_HPS, 1, TQ), F32),
            pltpu.VMEM((FOX_HPS, FOX_HD, TQ), F32),
        ],
    )
    return pl.pallas_call(
        _fox_kernel,
        grid_spec=grid_spec,
        out_shape=jax.ShapeDtypeStruct((R, MIX_W), BF16),
        compiler_params=_cparams(("parallel", "arbitrary")),
        name="fox_attention",
    )(qi_tab, kj_tab, zf, zf, vt, c_col)


GLA_CHUNK = 64
GLA_UNROLL = 5


def _gla_kernel(q_ref, k_ref, v_ref, gr_ref, zs_ref, wa2_ref, ba_ref, gn_ref, o_ref, st_ref, la_ref):
    i = pl.program_id(0)

    @pl.when(i == 0)
    def _():
        st_ref[...] = jnp.zeros_like(st_ref)

    la = jnp.dot(zs_ref[...], wa2_ref[...], precision=HIGHEST, preferred_element_type=F32)
    la_ref[...] = _log_sigmoid(la + ba_ref[...]) * (1.0 / GLA_TAU)

    c_r = lax.broadcasted_iota(I32, (GLA_CHUNK, GLA_CHUNK), 0)
    c_c = lax.broadcasted_iota(I32, (GLA_CHUNK, GLA_CHUNK), 1)
    tri_b = c_r >= c_c
    tri = jnp.where(tri_b, 1.0, 0.0).astype(BF16)

    def chunk(c, carry):
        r0 = pl.multiple_of(c * GLA_CHUNK, GLA_CHUNK)
        rows = pl.ds(r0, GLA_CHUNK)
        g = la_ref[rows, :]
        g_hi = g.astype(BF16)
        g_lo = (g - g_hi.astype(F32)).astype(BF16)
        b = (jnp.dot(tri, g_hi, preferred_element_type=F32)
             + jnp.dot(tri, g_lo, preferred_element_type=F32))
        b_last = b[GLA_CHUNK - 1:GLA_CHUNK, :]
        e_last = jnp.exp(b_last)
        qt = q_ref[rows, :] * (GLA_DK ** -0.5) * jnp.exp(b)
        kt = k_ref[rows, :] * jnp.exp(-b)
        kh = kt * e_last
        for h in range(GLA_HEADS):
            ks = slice(h * GLA_DK, (h + 1) * GLA_DK)
            vs = slice(h * GLA_DV, (h + 1) * GLA_DV)
            q_h = qt[:, ks].astype(BF16)
            k_h = kt[:, ks].astype(BF16)
            kh_h = kh[:, ks].astype(BF16)
            v_h = v_ref[rows, vs]
            att = lax.dot_general(q_h, k_h, (((1,), (1,)), ((), ())), preferred_element_type=F32)
            att = jnp.where(tri_b, att, 0.0)
            st = st_ref[h]
            o = jnp.dot(att.astype(BF16), v_h.astype(BF16), preferred_element_type=F32)
            o = o + lax.dot_general(q_h, st.astype(BF16), (((1,), (1,)), ((), ())),
                                    preferred_element_type=F32)
            st_ref[h] = st * e_last[:, ks] + jnp.dot(v_h.T.astype(BF16), kh_h,
                                                     preferred_element_type=F32)
            ms = jnp.mean(o * o, axis=-1, keepdims=True)
            on = o * lax.rsqrt(ms + LN_EPS) * gn_ref[:, vs]
            gate = gr_ref[rows, vs]
            o_ref[rows, vs] = (on * (gate * _sigmoid(gate))).astype(o_ref.dtype)
        return carry

    lax.fori_loop(0, TM // GLA_CHUNK, chunk, 0, unroll=GLA_UNROLL)


def _gla(z, wa2p, ba, gn):
    return pl.pallas_call(
        _gla_kernel,
        grid=(R // TM,),
        in_specs=[
            pl.BlockSpec((TM, 256), lambda i: (i, C_GQ // 256)),
            pl.BlockSpec((TM, 256), lambda i: (i, C_GK // 256)),
            pl.BlockSpec((TM, 512), lambda i: (i, C_GV // 512)),
            pl.BlockSpec((TM, 512), lambda i: (i, C_GR // 512)),
            pl.BlockSpec((TM, LANE), lambda i: (i, C_SM // LANE + 1)),
            pl.BlockSpec((LANE, 256), lambda i: (0, 0)),
            pl.BlockSpec((1, 256), lambda i: (0, 0)),
            pl.BlockSpec((1, 512), lambda i: (0, 0)),
        ],
        out_specs=pl.BlockSpec((TM, MIX_W), lambda i: (i, 0)),
        out_shape=jax.ShapeDtypeStruct((R, MIX_W), BF16),
        scratch_shapes=[
            pltpu.VMEM((GLA_HEADS, GLA_DV, GLA_DK), F32),
            pltpu.VMEM((TM, GLA_HEADS * GLA_DK), F32),
        ],
        compiler_params=_cparams(("arbitrary",)),
        name="gla",
    )(z, z, z, z, z, wa2p, ba, gn)


def _local_kernel(cb_ref, cc_ref, cv_ref, pz_ref, cw_ref, pw_ref, ps_ref, oc_ref, od_ref, u_sc, p_sc):
    i = pl.program_id(0)

    @pl.when(i == 0)
    def _():
        u_sc[0:HALO, :] = jnp.zeros((HALO, MIX_W), F32)
        p_sc[0:HALO, :] = jnp.zeros((HALO, MIX_W), F32)

    @pl.when(i > 0)
    def _():
        u_sc[0:HALO, :] = u_sc[TM:TM + HALO, :]
        p_sc[0:HALO, :] = p_sc[TM:TM + HALO, :]

    u = cc_ref[...] * cv_ref[...]
    pz = pz_ref[...]
    u_sc[HALO:, :] = u
    p_sc[HALO:, :] = pz

    y = (cw_ref[2:3, :] * u + cw_ref[1:2, :] * u_sc[HALO - 1:HALO - 1 + TM, :]
         + cw_ref[0:1, :] * u_sc[HALO - 2:HALO - 2 + TM, :])
    oc_ref[...] = (cb_ref[...] * y).astype(oc_ref.dtype)

    tok = i * TM - T0 + lax.broadcasted_iota(I32, (TM, 1), 0)
    cnt_small = jnp.maximum(tok + 1, 1).astype(F32)
    for g, w in enumerate(POOL_WINDOWS):
        cols = slice(g * POOL_GW, (g + 1) * POOL_GW)
        x = pz[:, cols]
        s = x
        for j in range(1, w):
            s = s + p_sc[HALO - j:HALO - j + TM, cols]
        inv_cnt = jnp.where(tok + 1 >= w, 1.0 / w, 1.0 / cnt_small)
        pooled = s * inv_cnt - x
        od = jnp.dot(pooled.astype(BF16), pw_ref[g], preferred_element_type=F32)
        od_ref[:, cols] = (od * ps_ref[:, cols]).astype(od_ref.dtype)


def _local_mixers(z, conv_w, pool_w_bf, pool_scale):
    cw = jnp.zeros((8, MIX_W), F32).at[:CONV_K].set(conv_w)
    blk = lambda c: pl.BlockSpec((TM, MIX_W), lambda i, c=c: (i, c // MIX_W))
    return pl.pallas_call(
        _local_kernel,
        grid=(R // TM,),
        in_specs=[
            blk(C_CB), blk(C_CC), blk(C_CV), blk(C_PZ),
            pl.BlockSpec((8, MIX_W), lambda i: (0, 0)),
            pl.BlockSpec((len(POOL_WINDOWS), POOL_GW, POOL_GW), lambda i: (0, 0, 0)),
            pl.BlockSpec((1, MIX_W), lambda i: (0, 0)),
        ],
        out_specs=[
            pl.BlockSpec((TM, MIX_W), lambda i: (i, 0)),
            pl.BlockSpec((TM, MIX_W), lambda i: (i, 0)),
        ],
        out_shape=[
            jax.ShapeDtypeStruct((R, MIX_W), BF16),
            jax.ShapeDtypeStruct((R, MIX_W), BF16),
        ],
        scratch_shapes=[
            pltpu.VMEM((TM + HALO, MIX_W), F32),
            pltpu.VMEM((TM + HALO, MIX_W), F32),
        ],
        compiler_params=_cparams(("arbitrary",)),
        name="conv_pool",
    )(z, z, z, z, cw, pool_w_bf, pool_scale.reshape(1, -1))


TN_MERGE = 256


def _merge_kernel(hb_ref, oa_ref, ob_ref, oc_ref, od_ref, wg0_ref, wg1_ref, wg2_ref, wg3_ref,
                  gb_ref, wb_ref, wo_ref, out_ref):
    j = pl.program_id(1)

    @pl.when(j == 0)
    def _():
        out_ref[...] = jnp.zeros_like(out_ref)

    hb = hb_ref[...]
    mixed = None
    for b, (o_ref, wg_ref) in enumerate(((oa_ref, wg0_ref), (ob_ref, wg1_ref),
                                         (oc_ref, wg2_ref), (od_ref, wg3_ref))):
        gate = _sigmoid(lax.dot_general(hb, wg_ref[...], _NT, preferred_element_type=F32)
                        + gb_ref[b:b + 1, :])
        proj = jnp.dot(o_ref[...], wb_ref[b], preferred_element_type=F32)
        term = gate * proj
        mixed = term if mixed is None else mixed + term
    out_ref[...] += jnp.dot(mixed.astype(BF16), wo_ref[...], preferred_element_type=F32)


def _merge(hb, o_a, o_b, o_c, o_d, w_all, gate_b, wb_bf, wo_bf, l):
    tn = TN_MERGE
    nj = D_MODEL // tn
    row = lambda w: pl.BlockSpec((TM, w), lambda i, j: (i, 0))
    wg = lambda b: pl.BlockSpec((None, tn, D_MODEL),
                                lambda i, j, b=b: (l, W_GATES // tn + b * nj + j, 0))
    return pl.pallas_call(
        _merge_kernel,
        grid=(R // TM, nj),
        in_specs=[
            row(D_MODEL), row(MIX_W), row(MIX_W), row(MIX_W), row(MIX_W),
            wg(0), wg(1), wg(2), wg(3),
            pl.BlockSpec((N_BRANCH, tn), lambda i, j: (0, j)),
            pl.BlockSpec((None, N_BRANCH, MIX_W, tn), lambda i, j: (l, 0, 0, j)),
            pl.BlockSpec((None, tn, D_MODEL), lambda i, j: (l, j, 0)),
        ],
        out_specs=pl.BlockSpec((TM, D_MODEL), lambda i, j: (i, 0)),
        out_shape=jax.ShapeDtypeStruct((R, D_MODEL), F32),
        compiler_params=_cparams(("parallel", "arbitrary"), vmem=BIG_VMEM_LIMIT),
        name="merge",
    )(hb, o_a, o_b, o_c, o_d, w_all, w_all, w_all, w_all, gate_b, wb_bf, wo_bf)


def _post_ln(h, delta, g, b, row0):
    y = _layer_norm_rows(DEEPNORM_ALPHA * h + delta, g, b)
    rows = row0 + lax.broadcasted_iota(I32, (y.shape[0], 1), 0)
    return jnp.where(rows >= T0, y, 0.0)


def _first_of(cands, target):
    idx = jnp.full(target.shape, len(cands) - 1, I32)
    for j in range(len(cands) - 2, -1, -1):
        idx = jnp.where(cands[j] == target, j, idx)
    return idx


def _pick(cands, idx):
    out = cands[-1]
    for j in range(len(cands) - 2, -1, -1):
        out = jnp.where(idx == j, cands[j], out)
    return out


def _ln1_route_kernel(h_ref, mix_ref, g_ref, b_ref, rwt_ref, rb_ref,
                      h1_ref, mi_ref, mf_ref, cnt_ref, carry_sc):
    i = pl.program_id(0)

    @pl.when(i == 0)
    def _():
        carry_sc[...] = jnp.zeros_like(carry_sc)

    y = _post_ln(h_ref[...], mix_ref[...], g_ref[...], b_ref[...], i * TM)
    h1_ref[...] = y

    logits = lax.dot_general(rwt_ref[...], y, (((1,), (1,)), ((), ())), precision=HIGHEST,
                             preferred_element_type=F32)
    aff = _sigmoid(logits)
    sel = aff + rb_ref[...]
    xs = [sel[j * N_GROUPS:(j + 1) * N_GROUPS, :] for j in range(EXPERTS_PER_GROUP)]
    afs = [aff[j * N_GROUPS:(j + 1) * N_GROUPS, :] for j in range(EXPERTS_PER_GROUP)]

    score = None
    for a in range(EXPERTS_PER_GROUP):
        for bb in range(a + 1, EXPERTS_PER_GROUP):
            pair = xs[a] + xs[bb]
            score = pair if score is None else jnp.maximum(score, pair)
    giota = lax.broadcasted_iota(I32, (N_GROUPS, TM), 0)
    gmax = jnp.max(score, axis=0, keepdims=True)
    grp = jnp.min(jnp.where(score == gmax, giota, N_GROUPS), axis=0, keepdims=True)
    gsel = giota == grp
    cs = [jnp.max(jnp.where(gsel, x, -jnp.inf), axis=0, keepdims=True) for x in xs]
    acs = [jnp.sum(jnp.where(gsel, a, 0.0), axis=0, keepdims=True) for a in afs]

    m1 = jnp.maximum(jnp.maximum(cs[0], cs[1]), jnp.maximum(cs[2], cs[3]))
    i0 = _first_of(cs, m1)
    ds = [jnp.where(i0 == j, -jnp.inf, cs[j]) for j in range(EXPERTS_PER_GROUP)]
    m2 = jnp.maximum(jnp.maximum(ds[0], ds[1]), jnp.maximum(ds[2], ds[3]))
    i1 = _first_of(ds, m2)
    a0 = _pick(acs, i0)
    a1 = _pick(acs, i1)
    denom = a0 + a1

    pos = i * TM + lax.broadcasted_iota(I32, (1, TM), 1)
    valid = pos >= T0
    riota = lax.broadcasted_iota(I32, (N_EXPERTS, TM), 0)
    oh0 = (riota == i0 * N_GROUPS + grp) & valid
    oh1 = (riota == i1 * N_GROUPS + grp) & valid
    ohf = jnp.where(oh0 | oh1, 1.0, 0.0)
    before = (lax.broadcasted_iota(I32, (TM, TM), 0)
              < lax.broadcasted_iota(I32, (TM, TM), 1)).astype(BF16)
    cum = jnp.dot(ohf.astype(BF16), before, preferred_element_type=F32) + carry_sc[...]
    rank0 = jnp.sum(jnp.where(oh0, cum, 0.0), axis=0, keepdims=True)
    rank1 = jnp.sum(jnp.where(oh1, cum, 0.0), axis=0, keepdims=True)
    carry = carry_sc[...] + jnp.sum(ohf, axis=1, keepdims=True)
    carry_sc[...] = carry
    cnt_ref[...] = jnp.broadcast_to(carry, cnt_ref.shape)

    zi = jnp.zeros((1, TM), I32)
    mi_ref[...] = jnp.concatenate(
        [grp * EXPERTS_PER_GROUP + i0, grp * EXPERTS_PER_GROUP + i1,
         rank0.astype(I32), rank1.astype(I32), zi, zi, zi, zi], axis=0)
    zf = jnp.zeros((1, TM), F32)
    mf_ref[...] = jnp.concatenate([a0 / denom, a1 / denom, zf, zf, zf, zf, zf, zf], axis=0)


def _ln1_route(h, mix, g, b, router_wt, router_bc):
    row = pl.BlockSpec((TM, D_MODEL), lambda i: (i, 0))
    vec = pl.BlockSpec((1, D_MODEL), lambda i: (0, 0))
    meta = pl.BlockSpec((8, TM), lambda i: (0, i))
    return pl.pallas_call(
        _ln1_route_kernel,
        grid=(R // TM,),
        in_specs=[row, row, vec, vec,
                  pl.BlockSpec((N_EXPERTS, D_MODEL), lambda i: (0, 0)),
                  pl.BlockSpec((N_EXPERTS, 1), lambda i: (0, 0))],
        out_specs=[row, meta, meta, pl.BlockSpec((N_EXPERTS, LANE), lambda i: (0, 0))],
        out_shape=[
            jax.ShapeDtypeStruct((R, D_MODEL), F32),
            jax.ShapeDtypeStruct((8, R), I32),
            jax.ShapeDtypeStruct((8, R), F32),
            jax.ShapeDtypeStruct((N_EXPERTS, LANE), F32),
        ],
        scratch_shapes=[pltpu.VMEM((N_EXPERTS, 1), F32)],
        compiler_params=_cparams(("arbitrary",)),
        name="ln1_route",
    )(h, mix, g.reshape(1, -1), b.reshape(1, -1), router_wt, router_bc)


def _dispatch_tables(mi, counts_slot_major):
    counts = counts_slot_major.reshape(EXPERTS_PER_GROUP, N_GROUPS).T.reshape(N_EXPERTS).astype(I32)
    padded = (counts + EXPERT_BLOCK - 1) // EXPERT_BLOCK * EXPERT_BLOCK
    pad_end = jnp.cumsum(padded)
    pad_start = pad_end - padded
    e_iota = jnp.arange(N_EXPERTS, dtype=I32)
    rows_ok = jnp.arange(R) >= T0

    def dest(eid, rank):
        start = jnp.sum(jnp.where(eid[:, None] == e_iota[None, :], pad_start[None, :], 0), axis=1)
        return jnp.where(rows_ok, start + rank, 0).astype(I32)

    d0 = dest(mi[0], mi[2])
    d1 = dest(mi[1], mi[3])
    blk_start = (pad_start // EXPERT_BLOCK).astype(I32)
    n_blk = (padded // EXPERT_BLOCK).astype(I32)
    n_used = (pad_end[-1] // EXPERT_BLOCK).astype(I32).reshape(1)
    tok_rows = jnp.arange(T0, R, dtype=I32)
    row_src = jnp.zeros((N_ROWS + UP_AHEAD * EXPERT_BLOCK,), I32).at[
        jnp.concatenate([d0[T0:], d1[T0:]])].set(jnp.concatenate([tok_rows, tok_rows]),
                                                 unique_indices=True)
    blk = jnp.arange(N_BLOCKS + UP_AHEAD, dtype=I32)
    owner = (blk[:, None] >= blk_start[None, :]) & (blk[:, None] < (blk_start + n_blk)[None, :])
    left = counts[None, :] - (blk[:, None] - blk_start[None, :]) * EXPERT_BLOCK
    n_valid = jnp.sum(jnp.where(owner, jnp.clip(left, 0, EXPERT_BLOCK), 0), axis=1).astype(I32)
    return d0, d1, blk_start, n_blk, n_used, row_src, n_valid


def _row_copy(src, src_row, dst, dst_row, sem):
    return pltpu.make_async_copy(src.at[pl.ds(src_row, 1), :], dst.at[pl.ds(dst_row, 1), :], sem)


def _issue_row_gather(rs_ref, nv_ref, g, h_hbm, buf, sem, full=False):
    base = g * EXPERT_BLOCK
    nv = nv_ref[g]
    for r in range(EXPERT_BLOCK):
        if full:
            _row_copy(h_hbm, rs_ref[base + r], buf, r, sem).start(priority=r % 2)
        else:
            @pl.when(r < nv)
            def _():
                _row_copy(h_hbm, rs_ref[base + r], buf, r, sem).start(priority=r % 2)


def _wait_row_gather(nv_ref, g, h_hbm, buf, sem, full=False):
    if full:
        pltpu.make_async_copy(h_hbm.at[pl.ds(0, EXPERT_BLOCK), :], buf, sem).wait()
        return
    nv = nv_ref[g]
    k = EXPERT_BLOCK
    while k >= 1:
        @pl.when((nv & k) != 0)
        def _():
            pltpu.make_async_copy(h_hbm.at[pl.ds(0, k), :], buf.at[pl.ds(0, k), :], sem).wait()
        k //= 2


def _block_rows(g):
    return pl.ds(pl.multiple_of(g * EXPERT_BLOCK, EXPERT_BLOCK), EXPERT_BLOCK)


def _finish_writes(out_copy, obuf, nu):
    n_slots = obuf.shape[0]
    for back in range(1, n_slots + 1):
        @pl.when(nu >= back)
        def _():
            out_copy(nu - back, (nu - back) % n_slots).wait()

    obuf[0] = jnp.zeros(obuf.shape[1:], obuf.dtype)

    def zero_block(g, carry):
        cp = out_copy(g, 0)
        cp.start()
        cp.wait()
        return carry

    lax.fori_loop(nu, N_BLOCKS, zero_block, 0)


UP_AHEAD = 2
UP_X_SLOTS = UP_AHEAD + 1


def _moe_up_kernel(bs_ref, nb_ref, nu_ref, rs_ref, nv_ref, h_hbm, wg_ref, wu_ref, o_hbm,
                   wg_sc, wu_sc, xbuf, xb_sc, obuf, xsem, osem):
    e = pl.program_id(0)
    nb = nb_ref[e]
    g0 = bs_ref[e]

    def out_copy(g, slot):
        return pltpu.make_async_copy(obuf.at[slot], o_hbm.at[_block_rows(g), :], osem.at[slot])

    @pl.when(e == 0)
    def _():
        xbuf[...] = jnp.zeros_like(xbuf)
        for g in range(UP_AHEAD):
            _issue_row_gather(rs_ref, nv_ref, g, h_hbm, xbuf.at[g], xsem.at[g])

    @pl.when(nb > 0)
    def _():
        wg_sc[...] = wg_ref[...].astype(BF16)
        wu_sc[...] = wu_ref[...].astype(BF16)

        def block(j, carry, full):
            g = g0 + j
            slot = g % 2
            xs = g % UP_X_SLOTS
            xs_next = (g + UP_AHEAD) % UP_X_SLOTS

            @pl.when(g >= 2)
            def _():
                out_copy(g - 2, slot).wait()

            _wait_row_gather(nv_ref, g, h_hbm, xbuf.at[xs], xsem.at[xs], full)
            xb_sc[...] = xbuf[xs].astype(BF16)
            _issue_row_gather(rs_ref, nv_ref, g + UP_AHEAD, h_hbm, xbuf.at[xs_next],
                              xsem.at[xs_next], full)
            x = xb_sc[...]
            gate = jnp.dot(x, wg_sc[...], preferred_element_type=F32)
            up = jnp.dot(x, wu_sc[...], preferred_element_type=F32)
            obuf[slot] = (gate * _sigmoid(gate) * up).astype(BF16)
            out_copy(g, slot).start()
            return carry

        n_full = jnp.maximum(nb - 1 - UP_AHEAD, 0)
        lax.fori_loop(0, n_full, lambda j, c: block(j, c, True), 0)
        lax.fori_loop(n_full, nb, lambda j, c: block(j, c, False), 0)

    @pl.when(e == N_EXPERTS - 1)
    def _():
        _finish_writes(out_copy, obuf, nu_ref[0])


def _moe_up(blk_start, n_blk, n_used, row_src, n_valid, h1, w_gate, w_up, l):
    wspec = pl.BlockSpec((None, None, D_MODEL, D_EXPERT), lambda e, *_: (l, e, 0, 0))
    grid_spec = pltpu.PrefetchScalarGridSpec(
        num_scalar_prefetch=5,
        grid=(N_EXPERTS,),
        in_specs=[pl.BlockSpec(memory_space=pl.ANY), wspec, wspec],
        out_specs=pl.BlockSpec(memory_space=pl.ANY),
        scratch_shapes=[
            pltpu.VMEM((D_MODEL, D_EXPERT), BF16),
            pltpu.VMEM((D_MODEL, D_EXPERT), BF16),
            pltpu.VMEM((UP_X_SLOTS, EXPERT_BLOCK, D_MODEL), F32),
            pltpu.VMEM((EXPERT_BLOCK, D_MODEL), BF16),
            pltpu.VMEM((2, EXPERT_BLOCK, D_EXPERT), BF16),
            pltpu.SemaphoreType.DMA((UP_X_SLOTS,)),
            pltpu.SemaphoreType.DMA((2,)),
        ],
    )
    return pl.pallas_call(
        _moe_up_kernel,
        grid_spec=grid_spec,
        out_shape=jax.ShapeDtypeStruct((N_ROWS, D_EXPERT), BF16),
        compiler_params=_cparams(("arbitrary",), vmem=MOE_VMEM_LIMIT),
        name="moe_up",
    )(blk_start, n_blk, n_used, row_src, n_valid, h1, w_gate, w_up)


DOWN_IN_SLOTS = 4
DOWN_OUT_SLOTS = 3


def _moe_down_kernel(bs_ref, nb_ref, nu_ref, x_hbm, wd_ref, y_hbm, wd_sc, xbuf, obuf, xsem, osem):
    e = pl.program_id(0)
    nb = nb_ref[e]
    g0 = bs_ref[e]
    nu = nu_ref[0]

    def in_copy(g):
        s = g % DOWN_IN_SLOTS
        return pltpu.make_async_copy(x_hbm.at[_block_rows(g), :], xbuf.at[s], xsem.at[s])

    def out_copy(g, slot):
        return pltpu.make_async_copy(obuf.at[slot], y_hbm.at[_block_rows(g), :], osem.at[slot])

    @pl.when(e == 0)
    def _():
        for g in range(DOWN_IN_SLOTS - 1):
            @pl.when(g < nu)
            def _():
                in_copy(g).start()

    @pl.when(nb > 0)
    def _():
        wd_sc[...] = wd_ref[...].astype(BF16)

        def block(j, carry):
            g = g0 + j
            slot = g % DOWN_OUT_SLOTS

            @pl.when(g >= DOWN_OUT_SLOTS)
            def _():
                out_copy(g - DOWN_OUT_SLOTS, slot).wait()

            in_copy(g).wait()

            @pl.when(g + DOWN_IN_SLOTS - 1 < nu)
            def _():
                in_copy(g + DOWN_IN_SLOTS - 1).start()

            obuf[slot] = jnp.dot(xbuf[g % DOWN_IN_SLOTS], wd_sc[...], preferred_element_type=F32)
            out_copy(g, slot).start()
            return carry

        lax.fori_loop(0, nb, block, 0)

    @pl.when(e == N_EXPERTS - 1)
    def _():
        _finish_writes(out_copy, obuf, nu)


def _moe_down(blk_start, n_blk, n_used, hmid, w_down, l):
    grid_spec = pltpu.PrefetchScalarGridSpec(
        num_scalar_prefetch=3,
        grid=(N_EXPERTS,),
        in_specs=[
            pl.BlockSpec(memory_space=pl.ANY),
            pl.BlockSpec((None, None, D_EXPERT, D_MODEL), lambda e, *_: (l, e, 0, 0)),
        ],
        out_specs=pl.BlockSpec(memory_space=pl.ANY),
        scratch_shapes=[
            pltpu.VMEM((D_EXPERT, D_MODEL), BF16),
            pltpu.VMEM((DOWN_IN_SLOTS, EXPERT_BLOCK, D_EXPERT), BF16),
            pltpu.VMEM((DOWN_OUT_SLOTS, EXPERT_BLOCK, D_MODEL), F32),
            pltpu.SemaphoreType.DMA((DOWN_IN_SLOTS,)),
            pltpu.SemaphoreType.DMA((DOWN_OUT_SLOTS,)),
        ],
    )
    return pl.pallas_call(
        _moe_down_kernel,
        grid_spec=grid_spec,
        out_shape=jax.ShapeDtypeStruct((N_ROWS, D_MODEL), F32),
        compiler_params=_cparams(("arbitrary",), vmem=MOE_VMEM_LIMIT),
        name="moe_down",
    )(blk_start, n_blk, n_used, hmid, w_down)


def _combine_kernel(d0_ref, d1_ref, y_hbm, h1_ref, mf_ref, g_ref, b_ref, h2_ref, h2b_ref,
                    buf0, buf1, sem):
    i = pl.program_id(0)
    slot = i % 2

    def issue(tile, s):
        base = tile * LANE

        def body(r, carry):
            _row_copy(y_hbm, d0_ref[base + r], buf0.at[s], r, sem.at[s]).start(priority=0)
            _row_copy(y_hbm, d1_ref[base + r], buf1.at[s], r, sem.at[s]).start(priority=1)
            return carry

        lax.fori_loop(0, LANE, body, 0)

    @pl.when(i == 0)
    def _():
        issue(0, 0)

    @pl.when(i + 1 < pl.num_programs(0))
    def _():
        issue(i + 1, 1 - slot)

    def drain(r, carry):
        _row_copy(y_hbm, 0, buf0.at[slot], r, sem.at[slot]).wait()
        _row_copy(y_hbm, 0, buf1.at[slot], r, sem.at[slot]).wait()
        return carry

    lax.fori_loop(0, LANE, drain, 0)
    wt = mf_ref[...].T
    ffn = wt[:, 0:1] * buf0[slot] + wt[:, 1:2] * buf1[slot]
    y = _post_ln(h1_ref[...], ffn, g_ref[...], b_ref[...], i * LANE)
    h2_ref[...] = y
    if h2b_ref is not None:
        h2b_ref[...] = y.astype(BF16)


def _combine_last_kernel(d0_ref, d1_ref, y_hbm, h1_ref, mf_ref, g_ref, b_ref, out_ref, buf0, buf1, sem):
    _combine_kernel(d0_ref, d1_ref, y_hbm, h1_ref, mf_ref, g_ref, b_ref, out_ref, None, buf0, buf1, sem)


def _combine_ln2(d0, d1, y_rows, h1, mf, g, b, last):
    row = lambda i, d0, d1: (i, 0)
    vec = pl.BlockSpec((1, D_MODEL), lambda i, d0, d1: (0, 0))
    if last:
        frames = lambda i, d0, d1: (jnp.maximum(i - 1, 0), 0)
        out_specs = [pl.BlockSpec((LANE, D_MODEL), frames)]
        out_shape = [jax.ShapeDtypeStruct((SEQ, D_MODEL), F32)]
    else:
        out_specs = [pl.BlockSpec((LANE, D_MODEL), row), pl.BlockSpec((LANE, D_MODEL), row)]
        out_shape = [jax.ShapeDtypeStruct((R, D_MODEL), F32), jax.ShapeDtypeStruct((R, D_MODEL), BF16)]
    grid_spec = pltpu.PrefetchScalarGridSpec(
        num_scalar_prefetch=2,
        grid=(R // LANE,),
        in_specs=[
            pl.BlockSpec(memory_space=pl.ANY),
            pl.BlockSpec((LANE, D_MODEL), row),
            pl.BlockSpec((8, LANE), lambda i, d0, d1: (0, i)),
            vec, vec,
        ],
        out_specs=out_specs,
        scratch_shapes=[
            pltpu.VMEM((2, LANE, D_MODEL), F32),
            pltpu.VMEM((2, LANE, D_MODEL), F32),
            pltpu.SemaphoreType.DMA((2,)),
        ],
    )
    return pl.pallas_call(
        _combine_last_kernel if last else _combine_kernel,
        grid_spec=grid_spec,
        out_shape=out_shape,
        compiler_params=_cparams(("arbitrary",)),
        name="moe_combine_ln2",
    )(d0, d1, y_rows, h1, mf, g.reshape(1, -1), b.reshape(1, -1))


def kernel(x, meta_tokens, ln_in_g, ln_in_b, w_in, fox_f_bias, gla_wa2, gla_ba, gla_norm_g, conv_w, pool_w, pool_scale, gate_b, w_branch, w_out, ln1_g, ln1_b, router_w, router_b, w_gate, w_up, w_down, ln2_g, ln2_b):
    assert x.shape == (1, SEQ, D_MODEL)
    h, hb = _ln_in(x.reshape(SEQ, D_MODEL), meta_tokens, ln_in_g, ln_in_b)
    router_wt = router_w.T.reshape(N_GROUPS, EXPERTS_PER_GROUP, D_MODEL).transpose(1, 0, 2).reshape(
        N_EXPERTS, D_MODEL)
    router_bc = router_b.astype(F32).reshape(N_GROUPS, EXPERTS_PER_GROUP).T.reshape(N_EXPERTS, 1)

    wb_bf = w_branch.astype(BF16)
    wo_bf = w_out.astype(BF16)

    w_all = _transposed_weights(w_in)

    for l in range(DEPTH):
        zf = _matmul_nt(hb, w_all, l, W_FOX, 2 * MIX_W, BF16, TM_PROJ, 512, "proj_fox")
        vt = _matmul_tt(w_all, l, W_FOX + 2 * MIX_W, MIX_W, hb, BF16, TM_PROJ, "proj_fox_vt")
        z = _matmul_nt(hb, w_all, l, W_MIX, N_MIXC + N_SMALL, F32, TM_PROJ, 768, "proj_mix")

        bias_row = jnp.zeros((1, LANE), F32).at[0, SM_FF:SM_FF + FOX_HEADS].set(fox_f_bias[l])
        c = _fox_gate(z, bias_row)
        o_a = _fox_attention(zf, vt, c)

        wa2p = jnp.zeros((LANE, GLA_HEADS * GLA_DK), F32).at[SM_GA:SM_GA + GLA_RANK].set(gla_wa2[l])
        o_b = _gla(z, wa2p, gla_ba[l].reshape(1, -1), gla_norm_g[l].reshape(1, -1))

        o_c, o_d = _local_mixers(z, conv_w[l], pool_w[l].astype(BF16), pool_scale[l])

        mix = _merge(hb, o_a, o_b, o_c, o_d, w_all, gate_b[l], wb_bf, wo_bf, l)
        h1, mi, mf, counts = _ln1_route(h, mix, ln1_g[l], ln1_b[l], router_wt, router_bc)

        d0, d1, blk_start, n_blk, n_used, row_src, n_valid = _dispatch_tables(mi, counts[:, 0])
        hmid = _moe_up(blk_start, n_blk, n_used, row_src, n_valid, h1, w_gate, w_up, l)
        y_rows = _moe_down(blk_start, n_blk, n_used, hmid, w_down, l)
        if l + 1 < DEPTH:
            h, hb = _combine_ln2(d0, d1, y_rows, h1, mf, ln2_g[l], ln2_b[l], last=False)
        else:
            (out,) = _combine_ln2(d0, d1, y_rows, h1, mf, ln2_g[l], ln2_b[l], last=True)

    return out.reshape(1, SEQ, D_MODEL)
```

```python
import jax
import jax.numpy as jnp
import numpy as np
from jax import lax
from jax.experimental import pallas as pl
from jax.experimental.pallas import tpu as pltpu

F32 = jnp.float32
BF16 = jnp.bfloat16
I32 = jnp.int32
HIGHEST = lax.Precision.HIGHEST

D_MODEL = 2048
SEQ = 8192
DEPTH = 2
N_META = 16
N_BRANCH = 4
MIX_W = 512
FOX_HEADS = 4
FOX_HD = 128
GLA_HEADS = 4
GLA_DK = 64
GLA_DV = 128
GLA_RANK = 16
GLA_TAU = 16.0
CONV_K = 3
POOL_WINDOWS = (2, 4, 8, 16)
POOL_GW = 128
N_EXPERTS = 32
N_GROUPS = 8
EXPERTS_PER_GROUP = 4
TOP_K = 2
D_EXPERT = 1024
LN_EPS = 1e-5
DEEPNORM_ALPHA = (2 * DEPTH) ** 0.25

_SPLITS = (512, 512, 512, 4, 256, 256, 512, 16, 512, 512, 512, 512, 512, 8192)
_OFFS = [int(o) for o in np.concatenate([[0], np.cumsum(_SPLITS)])]
(O_FQ, O_FK, O_FV, O_FF, O_GQ, O_GK, O_GV, O_GA, O_GR, O_CB, O_CC, O_CV, O_PZ, O_GZ, P_IN) = _OFFS

LANE = 128
PAD_ROWS = LANE - N_META
T0 = PAD_ROWS
N_TOK = N_META + SEQ
R = PAD_ROWS + N_TOK
TM = 640
TM_PROJ = 1664
HALO = 16

WT_TILE = 128
W_FOX, W_MIX, W_SMALL, W_GATES = 0, 1536, 5120, 5376
N_FOX, N_MIXC, N_SMALL, N_GATES = 1536, 3584, 2 * LANE, 8192
N_WT = W_GATES + N_GATES
C_GQ, C_GK, C_GV, C_GR, C_CB, C_CC, C_CV, C_PZ = 0, 256, 512, 1024, 1536, 2048, 2560, 3072
C_SM = N_MIXC
SM_FF_TILE, SM_GA_TILE = O_FF // LANE, O_GA // LANE
SM_FF = O_FF - SM_FF_TILE * LANE
SM_GA = O_GA - SM_GA_TILE * LANE


def _wt_sources():
    src = []
    for r in range(0, N_WT, WT_TILE):
        if r < N_FOX:
            src.append(O_FQ + r)
        elif r < W_MIX + 1024:
            src.append(O_GQ + r - W_MIX)
        elif r < W_SMALL:
            src.append(O_GR + r - (W_MIX + 1024))
        elif r < W_GATES:
            src.append((SM_FF_TILE, SM_GA_TILE)[(r - W_SMALL) // LANE] * LANE)
        else:
            src.append(O_GZ + r - W_GATES)
    assert all(0 <= s and s + WT_TILE <= P_IN for s in src)
    return src


WT_SRC = _wt_sources()

EXPERT_BLOCK = 128
N_FLAT = N_TOK * TOP_K
N_BLOCKS = -(-N_FLAT // EXPERT_BLOCK) + N_EXPERTS
N_ROWS = N_BLOCKS * EXPERT_BLOCK
UP_AHEAD = 2
UP_X_SLOTS = UP_AHEAD + 1

NEG = -1e30
VMEM_LIMIT = 48 * 1024 * 1024
BIG_VMEM_LIMIT = 56 * 1024 * 1024
MOE_VMEM_LIMIT = BIG_VMEM_LIMIT


def _cparams(sem, vmem=VMEM_LIMIT):
    return pltpu.CompilerParams(dimension_semantics=sem, vmem_limit_bytes=vmem)


def _log_sigmoid(x):
    return jnp.minimum(x, 0.0) - jnp.log1p(jnp.exp(-jnp.abs(x)))


def _sigmoid(x):
    return 1.0 / (1.0 + jnp.exp(-x))


def _layer_norm_rows(x, g, b):
    mu = jnp.mean(x, axis=-1, keepdims=True)
    xc = x - mu
    var = jnp.mean(xc * xc, axis=-1, keepdims=True)
    return xc * lax.rsqrt(var + LN_EPS) * g + b


def _ln_in_kernel(x_ref, meta_ref, g_ref, b_ref, h_ref, hb_ref):
    i = pl.program_id(0)

    @pl.when(i == 0)
    def _():
        h_ref[...] = jnp.zeros_like(h_ref)
        hb_ref[...] = jnp.zeros_like(hb_ref)
        m = _layer_norm_rows(meta_ref[...], g_ref[...], b_ref[...])
        h_ref[PAD_ROWS:, :] = m
        hb_ref[PAD_ROWS:, :] = m.astype(BF16)

    @pl.when(i > 0)
    def _():
        y = _layer_norm_rows(x_ref[...], g_ref[...], b_ref[...])
        h_ref[...] = y
        hb_ref[...] = y.astype(BF16)


def _ln_in(x2d, meta, g, b):
    nb = R // LANE
    return pl.pallas_call(
        _ln_in_kernel,
        grid=(nb,),
        in_specs=[
            pl.BlockSpec((LANE, D_MODEL), lambda i: (jnp.maximum(i - 1, 0), 0)),
            pl.BlockSpec((N_META, D_MODEL), lambda i: (0, 0)),
            pl.BlockSpec((1, D_MODEL), lambda i: (0, 0)),
            pl.BlockSpec((1, D_MODEL), lambda i: (0, 0)),
        ],
        out_specs=[
            pl.BlockSpec((LANE, D_MODEL), lambda i: (i, 0)),
            pl.BlockSpec((LANE, D_MODEL), lambda i: (i, 0)),
        ],
        out_shape=[
            jax.ShapeDtypeStruct((R, D_MODEL), F32),
            jax.ShapeDtypeStruct((R, D_MODEL), BF16),
        ],
        compiler_params=_cparams(("arbitrary",)),
        name="ln_in",
    )(x2d, meta, g.reshape(1, -1), b.reshape(1, -1))


D_CHUNKS = D_MODEL // LANE
FLAT_ROWS = D_CHUNKS * DEPTH


def _wt_kernel(src_ref, w_hbm, o_ref, buf, sem):
    j = pl.program_id(0)
    slot = j % 2

    def tile_copy(t, s):
        row0 = pl.multiple_of(src_ref[t] * FLAT_ROWS, FLAT_ROWS)
        return pltpu.make_async_copy(w_hbm.at[pl.ds(row0, WT_TILE * FLAT_ROWS), :], buf.at[s],
                                     sem.at[s])

    @pl.when(j == 0)
    def _():
        tile_copy(0, 0).start()

    @pl.when(j + 1 < pl.num_programs(0))
    def _():
        tile_copy(j + 1, 1 - slot).start()

    tile_copy(j, slot).wait()
    for l in range(DEPTH):
        for c in range(D_CHUNKS):
            o_ref[l, :, c * LANE:(c + 1) * LANE] = buf[
                slot, pl.ds(c * DEPTH + l, WT_TILE, stride=FLAT_ROWS), :].astype(BF16)


def _transposed_weights(w_in):
    flat = w_in.reshape(DEPTH, D_CHUNKS, LANE, P_IN).transpose(3, 1, 0, 2).reshape(
        P_IN * FLAT_ROWS, LANE)
    grid_spec = pltpu.PrefetchScalarGridSpec(
        num_scalar_prefetch=1,
        grid=(N_WT // WT_TILE,),
        in_specs=[pl.BlockSpec(memory_space=pl.ANY)],
        out_specs=pl.BlockSpec((DEPTH, WT_TILE, D_MODEL), lambda j, src: (0, j, 0)),
        scratch_shapes=[
            pltpu.VMEM((2, WT_TILE * FLAT_ROWS, LANE), F32),
            pltpu.SemaphoreType.DMA((2,)),
        ],
    )
    return pl.pallas_call(
        _wt_kernel,
        grid_spec=grid_spec,
        out_shape=jax.ShapeDtypeStruct((DEPTH, N_WT, D_MODEL), BF16),
        compiler_params=_cparams(("arbitrary",)),
        name="transposed_weights",
    )(jnp.asarray(WT_SRC, I32), flat)


_NT = (((1,), (1,)), ((), ()))


def _mm_nt_kernel(a_ref, wt_ref, o_ref):
    o_ref[...] = lax.dot_general(a_ref[...], wt_ref[...], _NT,
                                 preferred_element_type=F32).astype(o_ref.dtype)


def _matmul_nt(a, wt, l, row0, n, out_dtype, tm, tn, name):
    m, k = a.shape
    return pl.pallas_call(
        _mm_nt_kernel,
        grid=(m // tm, n // tn),
        in_specs=[
            pl.BlockSpec((tm, k), lambda i, j: (i, 0)),
            pl.BlockSpec((None, tn, k), lambda i, j: (l, row0 // tn + j, 0)),
        ],
        out_specs=pl.BlockSpec((tm, tn), lambda i, j: (i, j)),
        out_shape=jax.ShapeDtypeStruct((m, n), out_dtype),
        compiler_params=_cparams(("parallel", "arbitrary")),
        name=name,
    )(a, wt)


def _matmul_tt(wt, l, row0, n, a, out_dtype, tm, name):
    m, k = a.shape
    return pl.pallas_call(
        _mm_nt_kernel,
        grid=(m // tm,),
        in_specs=[
            pl.BlockSpec((None, n, k), lambda i: (l, row0 // n, 0)),
            pl.BlockSpec((tm, k), lambda i: (i, 0)),
        ],
        out_specs=pl.BlockSpec((n, tm), lambda i: (0, i)),
        out_shape=jax.ShapeDtypeStruct((n, m), out_dtype),
        compiler_params=_cparams(("parallel",)),
        name=name,
    )(wt, a)


def _fox_gate_kernel(zs_ref, bias_ref, c_ref, carry_ref):
    i = pl.program_id(0)

    @pl.when(i == 0)
    def _():
        carry_ref[...] = jnp.zeros_like(carry_ref)

    rows = i * TM + lax.broadcasted_iota(I32, (TM, LANE), 0)
    lf = _log_sigmoid(zs_ref[...] + bias_ref[...])
    lf = jnp.where(rows >= T0, lf, 0.0)
    tri = (lax.broadcasted_iota(I32, (TM, TM), 0)
           >= lax.broadcasted_iota(I32, (TM, TM), 1)).astype(F32)
    c = jnp.dot(tri, lf, precision=HIGHEST, preferred_element_type=F32) + carry_ref[...]
    c_ref[...] = c
    carry_ref[...] = c[TM - 1:TM, :]


def _fox_gate(z, bias_row):
    return pl.pallas_call(
        _fox_gate_kernel,
        grid=(R // TM,),
        in_specs=[
            pl.BlockSpec((TM, LANE), lambda i: (i, C_SM // LANE)),
            pl.BlockSpec((1, LANE), lambda i: (0, 0)),
        ],
        out_specs=pl.BlockSpec((TM, LANE), lambda i: (i, 0)),
        out_shape=jax.ShapeDtypeStruct((R, LANE), F32),
        scratch_shapes=[pltpu.VMEM((1, LANE), F32)],
        compiler_params=_cparams(("arbitrary",)),
        name="fox_gate",
    )(z, bias_row)


TQ = TM
N_QB = R // TQ
_PAIRS = [(qi, kj) for qi in range(N_QB) for kj in range(qi + 1)]
N_PAIRS = len(_PAIRS)


LOG2E = 1.4426950408889634
FOX_HPS = FOX_HEADS


def _fox_kernel(qi_tab, kj_tab, q_ref, k_ref, vt_ref, ck_ref, o_ref, m_sc, l_sc, acc_sc):
    p = pl.program_id(1)
    qi = qi_tab[p]
    kj = kj_tab[p]

    @pl.when(kj == 0)
    def _():
        m_sc[...] = jnp.full_like(m_sc, NEG)
        l_sc[...] = jnp.zeros_like(l_sc)
        acc_sc[...] = jnp.zeros_like(acc_sc)

    kpos = kj * TQ + lax.broadcasted_iota(I32, (TQ, 1), 0)
    c1 = FOX_HD ** -0.5 * LOG2E

    def step(causal):
        for hh in range(FOX_HPS):
            lanes = slice(hh * FOX_HD, (hh + 1) * FOX_HD)
            ck = ck_ref[:, SM_FF + hh:SM_FF + hh + 1]
            ckl = jnp.where(kpos >= T0, ck * LOG2E, -NEG)
            t = lax.dot_general(k_ref[:, lanes], q_ref[:, lanes], (((1,), (1,)), ((), ())),
                                preferred_element_type=F32) * c1 - ckl
            if causal:
                ahead = (lax.broadcasted_iota(I32, (TQ, TQ), 0)
                         - lax.broadcasted_iota(I32, (TQ, TQ), 1))
                t = jnp.where(ahead <= 0, t, NEG)
            m_prev = m_sc[hh]
            m_new = jnp.maximum(m_prev, jnp.max(t, axis=0, keepdims=True))
            alpha = jnp.exp2(m_prev - m_new)
            pr = jnp.exp2(t - m_new)
            l_sc[hh] = alpha * l_sc[hh] + jnp.sum(pr, axis=0, keepdims=True)
            acc_sc[hh] = alpha * acc_sc[hh] + jnp.dot(vt_ref[lanes, :], pr.astype(BF16),
                                                      preferred_element_type=F32)
            m_sc[hh] = m_new

    @pl.when(kj == qi)
    def _():
        step(True)

    @pl.when(kj != qi)
    def _():
        step(False)

    @pl.when(kj == qi)
    def _():
        for hh in range(FOX_HPS):
            o_ref[:, hh * FOX_HD:(hh + 1) * FOX_HD] = (acc_sc[hh] / l_sc[hh]).T.astype(o_ref.dtype)


def _fox_attention(zf, vt, c_col):
    qi_tab = jnp.asarray([p[0] for p in _PAIRS], I32)
    kj_tab = jnp.asarray([p[1] for p in _PAIRS], I32)
    hw = FOX_HPS * FOX_HD
    grid_spec = pltpu.PrefetchScalarGridSpec(
        num_scalar_prefetch=2,
        grid=(FOX_HEADS // FOX_HPS, N_PAIRS),
        in_specs=[
            pl.BlockSpec((TQ, hw), lambda h, p, qt, kt: (qt[p], h)),
            pl.BlockSpec((TQ, hw), lambda h, p, qt, kt: (kt[p], FOX_HEADS // FOX_HPS + h)),
            pl.BlockSpec((hw, TQ), lambda h, p, qt, kt: (h, kt[p])),
            pl.BlockSpec((TQ, LANE), lambda h, p, qt, kt: (kt[p], 0)),
        ],
        out_specs=pl.BlockSpec((TQ, hw), lambda h, p, qt, kt: (qt[p], h)),
        scratch_shapes=[
            pltpu.VMEM((FOX_HPS, 1, TQ), F32),
            pltpu.VMEM((FOX_HPS, 1, TQ), F32),
            pltpu.VMEM((FOX_HPS, FOX_HD, TQ), F32),
        ],
    )
    return pl.pallas_call(
        _fox_kernel,
        grid_spec=grid_spec,
        out_shape=jax.ShapeDtypeStruct((R, MIX_W), BF16),
        compiler_params=_cparams(("parallel", "arbitrary")),
        name="fox_attention",
    )(qi_tab, kj_tab, zf, zf, vt, c_col)


GLA_CHUNK = 64
GLA_UNROLL = 5


def _gla_kernel(q_ref, k_ref, v_ref, gr_ref, zs_ref, wa2_ref, ba_ref, gn_ref, o_ref, st_ref, la_ref):
    i = pl.program_id(0)

    @pl.when(i == 0)
    def _():
        st_ref[...] = jnp.zeros_like(st_ref)

    la = jnp.dot(zs_ref[...], wa2_ref[...], precision=HIGHEST, preferred_element_type=F32)
    la_ref[...] = _log_sigmoid(la + ba_ref[...]) * (1.0 / GLA_TAU)

    c_r = lax.broadcasted_iota(I32, (GLA_CHUNK, GLA_CHUNK), 0)
    c_c = lax.broadcasted_iota(I32, (GLA_CHUNK, GLA_CHUNK), 1)
    tri_b = c_r >= c_c
    tri = jnp.where(tri_b, 1.0, 0.0).astype(BF16)

    def chunk(c, carry):
        r0 = pl.multiple_of(c * GLA_CHUNK, GLA_CHUNK)
        rows = pl.ds(r0, GLA_CHUNK)
        g = la_ref[rows, :]
        g_hi = g.astype(BF16)
        g_lo = (g - g_hi.astype(F32)).astype(BF16)
        b = (jnp.dot(tri, g_hi, preferred_element_type=F32)
             + jnp.dot(tri, g_lo, preferred_element_type=F32))
        b_last = b[GLA_CHUNK - 1:GLA_CHUNK, :]
        e_last = jnp.exp(b_last)
        qt = q_ref[rows, :] * (GLA_DK ** -0.5) * jnp.exp(b)
        kt = k_ref[rows, :] * jnp.exp(-b)
        kh = kt * e_last
        for h in range(GLA_HEADS):
            ks = slice(h * GLA_DK, (h + 1) * GLA_DK)
            vs = slice(h * GLA_DV, (h + 1) * GLA_DV)
            q_h = qt[:, ks].astype(BF16)
            k_h = kt[:, ks].astype(BF16)
            kh_h = kh[:, ks].astype(BF16)
            v_h = v_ref[rows, vs]
            att = lax.dot_general(q_h, k_h, (((1,), (1,)), ((), ())), preferred_element_type=F32)
            att = jnp.where(tri_b, att, 0.0)
            st = st_ref[h]
            o = jnp.dot(att.astype(BF16), v_h.astype(BF16), preferred_element_type=F32)
            o = o + lax.dot_general(q_h, st.astype(BF16), (((1,), (1,)), ((), ())),
                                    preferred_element_type=F32)
            st_ref[h] = st * e_last[:, ks] + jnp.dot(v_h.T.astype(BF16), kh_h,
                                                     preferred_element_type=F32)
            ms = jnp.mean(o * o, axis=-1, keepdims=True)
            on = o * lax.rsqrt(ms + LN_EPS) * gn_ref[:, vs]
            gate = gr_ref[rows, vs]
            o_ref[rows, vs] = (on * (gate * _sigmoid(gate))).astype(o_ref.dtype)
        return carry

    lax.fori_loop(0, TM // GLA_CHUNK, chunk, 0, unroll=GLA_UNROLL)


def _gla(z, wa2p, ba, gn):
    return pl.pallas_call(
        _gla_kernel,
        grid=(R // TM,),
        in_specs=[
            pl.BlockSpec((TM, 256), lambda i: (i, C_GQ // 256)),
            pl.BlockSpec((TM, 256), lambda i: (i, C_GK // 256)),
            pl.BlockSpec((TM, 512), lambda i: (i, C_GV // 512)),
            pl.BlockSpec((TM, 512), lambda i: (i, C_GR // 512)),
            pl.BlockSpec((TM, LANE), lambda i: (i, C_SM // LANE + 1)),
            pl.BlockSpec((LANE, 256), lambda i: (0, 0)),
            pl.BlockSpec((1, 256), lambda i: (0, 0)),
            pl.BlockSpec((1, 512), lambda i: (0, 0)),
        ],
        out_specs=pl.BlockSpec((TM, MIX_W), lambda i: (i, 0)),
        out_shape=jax.ShapeDtypeStruct((R, MIX_W), BF16),
        scratch_shapes=[
            pltpu.VMEM((GLA_HEADS, GLA_DV, GLA_DK), F32),
            pltpu.VMEM((TM, GLA_HEADS * GLA_DK), F32),
        ],
        compiler_params=_cparams(("arbitrary",)),
        name="gla",
    )(z, z, z, z, z, wa2p, ba, gn)


def _local_kernel(cb_ref, cc_ref, cv_ref, pz_ref, cw_ref, pw_ref, ps_ref, oc_ref, od_ref, u_sc, p_sc):
    i = pl.program_id(0)

    @pl.when(i == 0)
    def _():
        u_sc[0:HALO, :] = jnp.zeros((HALO, MIX_W), F32)
        p_sc[0:HALO, :] = jnp.zeros((HALO, MIX_W), F32)

    @pl.when(i > 0)
    def _():
        u_sc[0:HALO, :] = u_sc[TM:TM + HALO, :]
        p_sc[0:HALO, :] = p_sc[TM:TM + HALO, :]

    u = cc_ref[...] * cv_ref[...]
    pz = pz_ref[...]
    u_sc[HALO:, :] = u
    p_sc[HALO:, :] = pz

    y = (cw_ref[2:3, :] * u + cw_ref[1:2, :] * u_sc[HALO - 1:HALO - 1 + TM, :]
         + cw_ref[0:1, :] * u_sc[HALO - 2:HALO - 2 + TM, :])
    oc_ref[...] = (cb_ref[...] * y).astype(oc_ref.dtype)

    tok = i * TM - T0 + lax.broadcasted_iota(I32, (TM, 1), 0)
    cnt_small = jnp.maximum(tok + 1, 1).astype(F32)
    for g, w in enumerate(POOL_WINDOWS):
        cols = slice(g * POOL_GW, (g + 1) * POOL_GW)
        x = pz[:, cols]
        s = x
        for j in range(1, w):
            s = s + p_sc[HALO - j:HALO - j + TM, cols]
        inv_cnt = jnp.where(tok + 1 >= w, 1.0 / w, 1.0 / cnt_small)
        pooled = s * inv_cnt - x
        od = jnp.dot(pooled.astype(BF16), pw_ref[g], preferred_element_type=F32)
        od_ref[:, cols] = (od * ps_ref[:, cols]).astype(od_ref.dtype)


def _local_mixers(z, conv_w, pool_w_bf, pool_scale):
    cw = jnp.zeros((8, MIX_W), F32).at[:CONV_K].set(conv_w)
    blk = lambda c: pl.BlockSpec((TM, MIX_W), lambda i, c=c: (i, c // MIX_W))
    return pl.pallas_call(
        _local_kernel,
        grid=(R // TM,),
        in_specs=[
            blk(C_CB), blk(C_CC), blk(C_CV), blk(C_PZ),
            pl.BlockSpec((8, MIX_W), lambda i: (0, 0)),
            pl.BlockSpec((len(POOL_WINDOWS), POOL_GW, POOL_GW), lambda i: (0, 0, 0)),
            pl.BlockSpec((1, MIX_W), lambda i: (0, 0)),
        ],
        out_specs=[
            pl.BlockSpec((TM, MIX_W), lambda i: (i, 0)),
            pl.BlockSpec((TM, MIX_W), lambda i: (i, 0)),
        ],
        out_shape=[
            jax.ShapeDtypeStruct((R, MIX_W), BF16),
            jax.ShapeDtypeStruct((R, MIX_W), BF16),
        ],
        scratch_shapes=[
            pltpu.VMEM((TM + HALO, MIX_W), F32),
            pltpu.VMEM((TM + HALO, MIX_W), F32),
        ],
        compiler_params=_cparams(("arbitrary",)),
        name="conv_pool",
    )(z, z, z, z, cw, pool_w_bf, pool_scale.reshape(1, -1))


TN_MERGE = 256


def _merge_kernel(hb_ref, oa_ref, ob_ref, oc_ref, od_ref, wg0_ref, wg1_ref, wg2_ref, wg3_ref,
                  gb_ref, wb_ref, wo_ref, out_ref):
    j = pl.program_id(1)

    @pl.when(j == 0)
    def _():
        out_ref[...] = jnp.zeros_like(out_ref)

    hb = hb_ref[...]
    mixed = None
    for b, (o_ref, wg_ref) in enumerate(((oa_ref, wg0_ref), (ob_ref, wg1_ref),
                                         (oc_ref, wg2_ref), (od_ref, wg3_ref))):
        gate = _sigmoid(lax.dot_general(hb, wg_ref[...], _NT, preferred_element_type=F32)
                        + gb_ref[b:b + 1, :])
        proj = jnp.dot(o_ref[...], wb_ref[b], preferred_element_type=F32)
        term = gate * proj
        mixed = term if mixed is None else mixed + term
    out_ref[...] += jnp.dot(mixed.astype(BF16), wo_ref[...], preferred_element_type=F32)


def _merge(hb, o_a, o_b, o_c, o_d, w_all, gate_b, wb_bf, wo_bf, l):
    tn = TN_MERGE
    nj = D_MODEL // tn
    row = lambda w: pl.BlockSpec((TM, w), lambda i, j: (i, 0))
    wg = lambda b: pl.BlockSpec((None, tn, D_MODEL),
                                lambda i, j, b=b: (l, W_GATES // tn + b * nj + j, 0))
    return pl.pallas_call(
        _merge_kernel,
        grid=(R // TM, nj),
        in_specs=[
            row(D_MODEL), row(MIX_W), row(MIX_W), row(MIX_W), row(MIX_W),
            wg(0), wg(1), wg(2), wg(3),
            pl.BlockSpec((N_BRANCH, tn), lambda i, j: (0, j)),
            pl.BlockSpec((None, N_BRANCH, MIX_W, tn), lambda i, j: (l, 0, 0, j)),
            pl.BlockSpec((None, tn, D_MODEL), lambda i, j: (l, j, 0)),
        ],
        out_specs=pl.BlockSpec((TM, D_MODEL), lambda i, j: (i, 0)),
        out_shape=jax.ShapeDtypeStruct((R, D_MODEL), F32),
        compiler_params=_cparams(("parallel", "arbitrary"), vmem=BIG_VMEM_LIMIT),
        name="merge",
    )(hb, o_a, o_b, o_c, o_d, w_all, w_all, w_all, w_all, gate_b, wb_bf, wo_bf)


def _post_ln(h, delta, g, b, row0):
    y = _layer_norm_rows(DEEPNORM_ALPHA * h + delta, g, b)
    rows = row0 + lax.broadcasted_iota(I32, (y.shape[0], 1), 0)
    return jnp.where(rows >= T0, y, 0.0)


def _first_of(cands, target):
    idx = jnp.full(target.shape, len(cands) - 1, I32)
    for j in range(len(cands) - 2, -1, -1):
        idx = jnp.where(cands[j] == target, j, idx)
    return idx


def _pick(cands, idx):
    out = cands[-1]
    for j in range(len(cands) - 2, -1, -1):
        out = jnp.where(idx == j, cands[j], out)
    return out


def _ln1_route_kernel(h_ref, mix_ref, g_ref, b_ref, rwt_ref, rb_ref,
                      h1_ref, mi_ref, mf_ref, cnt_ref, carry_sc):
    i = pl.program_id(0)

    @pl.when(i == 0)
    def _():
        carry_sc[...] = jnp.zeros_like(carry_sc)

    y = _post_ln(h_ref[...], mix_ref[...], g_ref[...], b_ref[...], i * TM)
    h1_ref[...] = y

    logits = lax.dot_general(rwt_ref[...], y, (((1,), (1,)), ((), ())), precision=HIGHEST,
                             preferred_element_type=F32)
    aff = _sigmoid(logits)
    sel = aff + rb_ref[...]
    xs = [sel[j * N_GROUPS:(j + 1) * N_GROUPS, :] for j in range(EXPERTS_PER_GROUP)]
    afs = [aff[j * N_GROUPS:(j + 1) * N_GROUPS, :] for j in range(EXPERTS_PER_GROUP)]

    score = None
    for a in range(EXPERTS_PER_GROUP):
        for bb in range(a + 1, EXPERTS_PER_GROUP):
            pair = xs[a] + xs[bb]
            score = pair if score is None else jnp.maximum(score, pair)
    giota = lax.broadcasted_iota(I32, (N_GROUPS, TM), 0)
    gmax = jnp.max(score, axis=0, keepdims=True)
    grp = jnp.min(jnp.where(score == gmax, giota, N_GROUPS), axis=0, keepdims=True)
    gsel = giota == grp
    cs = [jnp.max(jnp.where(gsel, x, -jnp.inf), axis=0, keepdims=True) for x in xs]
    acs = [jnp.sum(jnp.where(gsel, a, 0.0), axis=0, keepdims=True) for a in afs]

    m1 = jnp.maximum(jnp.maximum(cs[0], cs[1]), jnp.maximum(cs[2], cs[3]))
    i0 = _first_of(cs, m1)
    ds = [jnp.where(i0 == j, -jnp.inf, cs[j]) for j in range(EXPERTS_PER_GROUP)]
    m2 = jnp.maximum(jnp.maximum(ds[0], ds[1]), jnp.maximum(ds[2], ds[3]))
    i1 = _first_of(ds, m2)
    a0 = _pick(acs, i0)
    a1 = _pick(acs, i1)
    denom = a0 + a1

    pos = i * TM + lax.broadcasted_iota(I32, (1, TM), 1)
    valid = pos >= T0
    riota = lax.broadcasted_iota(I32, (N_EXPERTS, TM), 0)
    oh0 = (riota == i0 * N_GROUPS + grp) & valid
    oh1 = (riota == i1 * N_GROUPS + grp) & valid
    ohf = jnp.where(oh0 | oh1, 1.0, 0.0)
    before = (lax.broadcasted_iota(I32, (TM, TM), 0)
              < lax.broadcasted_iota(I32, (TM, TM), 1)).astype(BF16)
    cum = jnp.dot(ohf.astype(BF16), before, preferred_element_type=F32) + carry_sc[...]
    rank0 = jnp.sum(jnp.where(oh0, cum, 0.0), axis=0, keepdims=True)
    rank1 = jnp.sum(jnp.where(oh1, cum, 0.0), axis=0, keepdims=True)
    carry = carry_sc[...] + jnp.sum(ohf, axis=1, keepdims=True)
    carry_sc[...] = carry
    cnt_ref[...] = jnp.broadcast_to(carry, cnt_ref.shape)

    zi = jnp.zeros((1, TM), I32)
    mi_ref[...] = jnp.concatenate(
        [grp * EXPERTS_PER_GROUP + i0, grp * EXPERTS_PER_GROUP + i1,
         rank0.astype(I32), rank1.astype(I32), zi, zi, zi, zi], axis=0)
    zf = jnp.zeros((1, TM), F32)
    mf_ref[...] = jnp.concatenate([a0 / denom, a1 / denom, zf, zf, zf, zf, zf, zf], axis=0)


def _ln1_route(h, mix, g, b, router_wt, router_bc):
    row = pl.BlockSpec((TM, D_MODEL), lambda i: (i, 0))
    vec = pl.BlockSpec((1, D_MODEL), lambda i: (0, 0))
    meta = pl.BlockSpec((8, TM), lambda i: (0, i))
    return pl.pallas_call(
        _ln1_route_kernel,
        grid=(R // TM,),
        in_specs=[row, row, vec, vec,
                  pl.BlockSpec((N_EXPERTS, D_MODEL), lambda i: (0, 0)),
                  pl.BlockSpec((N_EXPERTS, 1), lambda i: (0, 0))],
        out_specs=[row, meta, meta, pl.BlockSpec((N_EXPERTS, LANE), lambda i: (0, 0))],
        out_shape=[
            jax.ShapeDtypeStruct((R, D_MODEL), F32),
            jax.ShapeDtypeStruct((8, R), I32),
            jax.ShapeDtypeStruct((8, R), F32),
            jax.ShapeDtypeStruct((N_EXPERTS, LANE), F32),
        ],
        scratch_shapes=[pltpu.VMEM((N_EXPERTS, 1), F32)],
        compiler_params=_cparams(("arbitrary",)),
        name="ln1_route",
    )(h, mix, g.reshape(1, -1), b.reshape(1, -1), router_wt, router_bc)


N_ROW_SRC = N_ROWS + UP_AHEAD * EXPERT_BLOCK


def _row_src_kernel(d0_ref, d1_ref, out_ref):
    def clear(i, carry):
        out_ref[i] = 0
        return carry

    lax.fori_loop(0, N_ROW_SRC, clear, 0, unroll=16)

    def place(t, carry):
        out_ref[d0_ref[t]] = t
        out_ref[d1_ref[t]] = t
        return carry

    lax.fori_loop(T0, R, place, 0, unroll=8)


def _row_sources(d0, d1):
    smem = pl.BlockSpec(memory_space=pltpu.SMEM)
    return pl.pallas_call(
        _row_src_kernel,
        in_specs=[smem, smem],
        out_specs=smem,
        out_shape=jax.ShapeDtypeStruct((N_ROW_SRC,), I32),
        name="row_sources",
    )(d0, d1)


def _dispatch_tables(mi, counts_slot_major):
    counts = counts_slot_major.reshape(EXPERTS_PER_GROUP, N_GROUPS).T.reshape(N_EXPERTS).astype(I32)
    padded = (counts + EXPERT_BLOCK - 1) // EXPERT_BLOCK * EXPERT_BLOCK
    pad_end = jnp.cumsum(padded)
    pad_start = pad_end - padded
    e_iota = jnp.arange(N_EXPERTS, dtype=I32)
    rows_ok = jnp.arange(R) >= T0

    def dest(eid, rank):
        start = jnp.sum(jnp.where(eid[:, None] == e_iota[None, :], pad_start[None, :], 0), axis=1)
        return jnp.where(rows_ok, start + rank, 0).astype(I32)

    d0 = dest(mi[0], mi[2])
    d1 = dest(mi[1], mi[3])
    blk_start = (pad_start // EXPERT_BLOCK).astype(I32)
    n_blk = (padded // EXPERT_BLOCK).astype(I32)
    n_used = (pad_end[-1] // EXPERT_BLOCK).astype(I32).reshape(1)
    row_src = _row_sources(d0, d1)
    blk = jnp.arange(N_BLOCKS + UP_AHEAD, dtype=I32)
    owner = (blk[:, None] >= blk_start[None, :]) & (blk[:, None] < (blk_start + n_blk)[None, :])
    left = counts[None, :] - (blk[:, None] - blk_start[None, :]) * EXPERT_BLOCK
    n_valid = jnp.sum(jnp.where(owner, jnp.clip(left, 0, EXPERT_BLOCK), 0), axis=1).astype(I32)
    return d0, d1, blk_start, n_blk, n_used, row_src, n_valid


def _row_copy(src, src_row, dst, dst_row, sem):
    return pltpu.make_async_copy(src.at[pl.ds(src_row, 1), :], dst.at[pl.ds(dst_row, 1), :], sem)


def _issue_row_gather(rs_ref, nv_ref, g, h_hbm, buf, sem, full=False):
    base = g * EXPERT_BLOCK
    nv = nv_ref[g]
    for r in range(EXPERT_BLOCK):
        if full:
            _row_copy(h_hbm, rs_ref[base + r], buf, r, sem).start(priority=r % 2)
        else:
            @pl.when(r < nv)
            def _():
                _row_copy(h_hbm, rs_ref[base + r], buf, r, sem).start(priority=r % 2)


def _wait_row_gather(nv_ref, g, h_hbm, buf, sem, full=False):
    if full:
        pltpu.make_async_copy(h_hbm.at[pl.ds(0, EXPERT_BLOCK), :], buf, sem).wait()
        return
    nv = nv_ref[g]
    k = EXPERT_BLOCK
    while k >= 1:
        @pl.when((nv & k) != 0)
        def _():
            pltpu.make_async_copy(h_hbm.at[pl.ds(0, k), :], buf.at[pl.ds(0, k), :], sem).wait()
        k //= 2


def _block_rows(g):
    return pl.ds(pl.multiple_of(g * EXPERT_BLOCK, EXPERT_BLOCK), EXPERT_BLOCK)


def _finish_writes(out_copy, obuf, nu):
    n_slots = obuf.shape[0]
    for back in range(1, n_slots + 1):
        @pl.when(nu >= back)
        def _():
            out_copy(nu - back, (nu - back) % n_slots).wait()

    obuf[0] = jnp.zeros(obuf.shape[1:], obuf.dtype)

    def zero_block(g, carry):
        cp = out_copy(g, 0)
        cp.start()
        cp.wait()
        return carry

    lax.fori_loop(nu, N_BLOCKS, zero_block, 0)


def _moe_up_kernel(bs_ref, nb_ref, nu_ref, rs_ref, nv_ref, h_hbm, wg_ref, wu_ref, o_hbm,
                   wg_sc, wu_sc, xbuf, xb_sc, obuf, xsem, osem):
    e = pl.program_id(0)
    nb = nb_ref[e]
    g0 = bs_ref[e]

    def out_copy(g, slot):
        return pltpu.make_async_copy(obuf.at[slot], o_hbm.at[_block_rows(g), :], osem.at[slot])

    @pl.when(e == 0)
    def _():
        xbuf[...] = jnp.zeros_like(xbuf)
        for g in range(UP_AHEAD):
            _issue_row_gather(rs_ref, nv_ref, g, h_hbm, xbuf.at[g], xsem.at[g])

    @pl.when(nb > 0)
    def _():
        wg_sc[...] = wg_ref[...].astype(BF16)
        wu_sc[...] = wu_ref[...].astype(BF16)

        def block(j, carry, full):
            g = g0 + j
            slot = g % 2
            xs = g % UP_X_SLOTS
            xs_next = (g + UP_AHEAD) % UP_X_SLOTS

            @pl.when(g >= 2)
            def _():
                out_copy(g - 2, slot).wait()

            _wait_row_gather(nv_ref, g, h_hbm, xbuf.at[xs], xsem.at[xs], full)
            xb_sc[...] = xbuf[xs].astype(BF16)
            _issue_row_gather(rs_ref, nv_ref, g + UP_AHEAD, h_hbm, xbuf.at[xs_next],
                              xsem.at[xs_next], full)
            x = xb_sc[...]
            gate = jnp.dot(x, wg_sc[...], preferred_element_type=F32)
            up = jnp.dot(x, wu_sc[...], preferred_element_type=F32)
            obuf[slot] = (gate * _sigmoid(gate) * up).astype(BF16)
            out_copy(g, slot).start()
            return carry

        n_full = jnp.maximum(nb - 1 - UP_AHEAD, 0)
        lax.fori_loop(0, n_full, lambda j, c: block(j, c, True), 0)
        lax.fori_loop(n_full, nb, lambda j, c: block(j, c, False), 0)

    @pl.when(e == N_EXPERTS - 1)
    def _():
        _finish_writes(out_copy, obuf, nu_ref[0])


def _moe_up(blk_start, n_blk, n_used, row_src, n_valid, h1, w_gate, w_up, l):
    wspec = pl.BlockSpec((None, None, D_MODEL, D_EXPERT), lambda e, *_: (l, e, 0, 0))
    grid_spec = pltpu.PrefetchScalarGridSpec(
        num_scalar_prefetch=5,
        grid=(N_EXPERTS,),
        in_specs=[pl.BlockSpec(memory_space=pl.ANY), wspec, wspec],
        out_specs=pl.BlockSpec(memory_space=pl.ANY),
        scratch_shapes=[
            pltpu.VMEM((D_MODEL, D_EXPERT), BF16),
            pltpu.VMEM((D_MODEL, D_EXPERT), BF16),
            pltpu.VMEM((UP_X_SLOTS, EXPERT_BLOCK, D_MODEL), F32),
            pltpu.VMEM((EXPERT_BLOCK, D_MODEL), BF16),
            pltpu.VMEM((2, EXPERT_BLOCK, D_EXPERT), BF16),
            pltpu.SemaphoreType.DMA((UP_X_SLOTS,)),
            pltpu.SemaphoreType.DMA((2,)),
        ],
    )
    return pl.pallas_call(
        _moe_up_kernel,
        grid_spec=grid_spec,
        out_shape=jax.ShapeDtypeStruct((N_ROWS, D_EXPERT), BF16),
        compiler_params=_cparams(("arbitrary",), vmem=MOE_VMEM_LIMIT),
        name="moe_up",
    )(blk_start, n_blk, n_used, row_src, n_valid, h1, w_gate, w_up)


DOWN_IN_SLOTS = 4
DOWN_OUT_SLOTS = 3


def _moe_down_kernel(bs_ref, nb_ref, nu_ref, x_hbm, wd_ref, y_hbm, wd_sc, xbuf, obuf, xsem, osem):
    e = pl.program_id(0)
    nb = nb_ref[e]
    g0 = bs_ref[e]
    nu = nu_ref[0]

    def in_copy(g):
        s = g % DOWN_IN_SLOTS
        return pltpu.make_async_copy(x_hbm.at[_block_rows(g), :], xbuf.at[s], xsem.at[s])

    def out_copy(g, slot):
        return pltpu.make_async_copy(obuf.at[slot], y_hbm.at[_block_rows(g), :], osem.at[slot])

    @pl.when(e == 0)
    def _():
        for g in range(DOWN_IN_SLOTS - 1):
            @pl.when(g < nu)
            def _():
                in_copy(g).start()

    @pl.when(nb > 0)
    def _():
        wd_sc[...] = wd_ref[...].astype(BF16)

        def block(j, carry):
            g = g0 + j
            slot = g % DOWN_OUT_SLOTS

            @pl.when(g >= DOWN_OUT_SLOTS)
            def _():
                out_copy(g - DOWN_OUT_SLOTS, slot).wait()

            in_copy(g).wait()

            @pl.when(g + DOWN_IN_SLOTS - 1 < nu)
            def _():
                in_copy(g + DOWN_IN_SLOTS - 1).start()

            obuf[slot] = jnp.dot(xbuf[g % DOWN_IN_SLOTS], wd_sc[...], preferred_element_type=F32)
            out_copy(g, slot).start()
            return carry

        lax.fori_loop(0, nb, block, 0)

    @pl.when(e == N_EXPERTS - 1)
    def _():
        _finish_writes(out_copy, obuf, nu)


def _moe_down(blk_start, n_blk, n_used, hmid, w_down, l):
    grid_spec = pltpu.PrefetchScalarGridSpec(
        num_scalar_prefetch=3,
        grid=(N_EXPERTS,),
        in_specs=[
            pl.BlockSpec(memory_space=pl.ANY),
            pl.BlockSpec((None, None, D_EXPERT, D_MODEL), lambda e, *_: (l, e, 0, 0)),
        ],
        out_specs=pl.BlockSpec(memory_space=pl.ANY),
        scratch_shapes=[
            pltpu.VMEM((D_EXPERT, D_MODEL), BF16),
            pltpu.VMEM((DOWN_IN_SLOTS, EXPERT_BLOCK, D_EXPERT), BF16),
            pltpu.VMEM((DOWN_OUT_SLOTS, EXPERT_BLOCK, D_MODEL), F32),
            pltpu.SemaphoreType.DMA((DOWN_IN_SLOTS,)),
            pltpu.SemaphoreType.DMA((DOWN_OUT_SLOTS,)),
        ],
    )
    return pl.pallas_call(
        _moe_down_kernel,
        grid_spec=grid_spec,
        out_shape=jax.ShapeDtypeStruct((N_ROWS, D_MODEL), F32),
        compiler_params=_cparams(("arbitrary",), vmem=MOE_VMEM_LIMIT),
        name="moe_down",
    )(blk_start, n_blk, n_used, hmid, w_down)


def _combine_kernel(d0_ref, d1_ref, y_hbm, h1_ref, mf_ref, g_ref, b_ref, h2_ref, h2b_ref,
                    buf0, buf1, sem):
    i = pl.program_id(0)
    slot = i % 2

    def issue(tile, s):
        base = tile * LANE

        def body(r, carry):
            _row_copy(y_hbm, d0_ref[base + r], buf0.at[s], r, sem.at[s]).start(priority=0)
            _row_copy(y_hbm, d1_ref[base + r], buf1.at[s], r, sem.at[s]).start(priority=1)
            return carry

        lax.fori_loop(0, LANE, body, 0)

    @pl.when(i == 0)
    def _():
        issue(0, 0)

    @pl.when(i + 1 < pl.num_programs(0))
    def _():
        issue(i + 1, 1 - slot)

    def drain(r, carry):
        _row_copy(y_hbm, 0, buf0.at[slot], r, sem.at[slot]).wait()
        _row_copy(y_hbm, 0, buf1.at[slot], r, sem.at[slot]).wait()
        return carry

    lax.fori_loop(0, LANE, drain, 0)
    wt = mf_ref[...].T
    ffn = wt[:, 0:1] * buf0[slot] + wt[:, 1:2] * buf1[slot]
    y = _post_ln(h1_ref[...], ffn, g_ref[...], b_ref[...], i * LANE)
    h2_ref[...] = y
    if h2b_ref is not None:
        h2b_ref[...] = y.astype(BF16)


def _combine_last_kernel(d0_ref, d1_ref, y_hbm, h1_ref, mf_ref, g_ref, b_ref, out_ref, buf0, buf1, sem):
    _combine_kernel(d0_ref, d1_ref, y_hbm, h1_ref, mf_ref, g_ref, b_ref, out_ref, None, buf0, buf1, sem)


def _combine_ln2(d0, d1, y_rows, h1, mf, g, b, last):
    row = lambda i, d0, d1: (i, 0)
    vec = pl.BlockSpec((1, D_MODEL), lambda i, d0, d1: (0, 0))
    if last:
        frames = lambda i, d0, d1: (jnp.maximum(i - 1, 0), 0)
        out_specs = [pl.BlockSpec((LANE, D_MODEL), frames)]
        out_shape = [jax.ShapeDtypeStruct((SEQ, D_MODEL), F32)]
    else:
        out_specs = [pl.BlockSpec((LANE, D_MODEL), row), pl.BlockSpec((LANE, D_MODEL), row)]
        out_shape = [jax.ShapeDtypeStruct((R, D_MODEL), F32), jax.ShapeDtypeStruct((R, D_MODEL), BF16)]
    grid_spec = pltpu.PrefetchScalarGridSpec(
        num_scalar_prefetch=2,
        grid=(R // LANE,),
        in_specs=[
            pl.BlockSpec(memory_space=pl.ANY),
            pl.BlockSpec((LANE, D_MODEL), row),
            pl.BlockSpec((8, LANE), lambda i, d0, d1: (0, i)),
            vec, vec,
        ],
        out_specs=out_specs,
        scratch_shapes=[
            pltpu.VMEM((2, LANE, D_MODEL), F32),
            pltpu.VMEM((2, LANE, D_MODEL), F32),
            pltpu.SemaphoreType.DMA((2,)),
        ],
    )
    return pl.pallas_call(
        _combine_last_kernel if last else _combine_kernel,
        grid_spec=grid_spec,
        out_shape=out_shape,
        compiler_params=_cparams(("arbitrary",)),
        name="moe_combine_ln2",
    )(d0, d1, y_rows, h1, mf, g.reshape(1, -1), b.reshape(1, -1))


def kernel(x, meta_tokens, ln_in_g, ln_in_b, w_in, fox_f_bias, gla_wa2, gla_ba, gla_norm_g, conv_w, pool_w, pool_scale, gate_b, w_branch, w_out, ln1_g, ln1_b, router_w, router_b, w_gate, w_up, w_down, ln2_g, ln2_b):
    assert x.shape == (1, SEQ, D_MODEL)
    h, hb = _ln_in(x.reshape(SEQ, D_MODEL), meta_tokens, ln_in_g, ln_in_b)
    router_wt = router_w.T.reshape(N_GROUPS, EXPERTS_PER_GROUP, D_MODEL).transpose(1, 0, 2).reshape(
        N_EXPERTS, D_MODEL)
    router_bc = router_b.astype(F32).reshape(N_GROUPS, EXPERTS_PER_GROUP).T.reshape(N_EXPERTS, 1)

    wb_bf = w_branch.astype(BF16)
    wo_bf = w_out.astype(BF16)

    w_all = _transposed_weights(w_in)

    for l in range(DEPTH):
        zf = _matmul_nt(hb, w_all, l, W_FOX, 2 * MIX_W, BF16, TM_PROJ, 512, "proj_fox")
        vt = _matmul_tt(w_all, l, W_FOX + 2 * MIX_W, MIX_W, hb, BF16, TM_PROJ, "proj_fox_vt")
        z = _matmul_nt(hb, w_all, l, W_MIX, N_MIXC + N_SMALL, F32, TM_PROJ, 768, "proj_mix")

        bias_row = jnp.zeros((1, LANE), F32).at[0, SM_FF:SM_FF + FOX_HEADS].set(fox_f_bias[l])
        c = _fox_gate(z, bias_row)
        o_a = _fox_attention(zf, vt, c)

        wa2p = jnp.zeros((LANE, GLA_HEADS * GLA_DK), F32).at[SM_GA:SM_GA + GLA_RANK].set(gla_wa2[l])
        o_b = _gla(z, wa2p, gla_ba[l].reshape(1, -1), gla_norm_g[l].reshape(1, -1))

        o_c, o_d = _local_mixers(z, conv_w[l], pool_w[l].astype(BF16), pool_scale[l])

        mix = _merge(hb, o_a, o_b, o_c, o_d, w_all, gate_b[l], wb_bf, wo_bf, l)
        h1, mi, mf, counts = _ln1_route(h, mix, ln1_g[l], ln1_b[l], router_wt, router_bc)

        d0, d1, blk_start, n_blk, n_used, row_src, n_valid = _dispatch_tables(mi, counts[:, 0])
        hmid = _moe_up(blk_start, n_blk, n_used, row_src, n_valid, h1, w_gate, w_up, l)
        y_rows = _moe_down(blk_start, n_blk, n_used, hmid, w_down, l)
        if l + 1 < DEPTH:
            h, hb = _combine_ln2(d0, d1, y_rows, h1, mf, ln2_g[l], ln2_b[l], last=False)
        else:
            (out,) = _combine_ln2(d0, d1, y_rows, h1, mf, ln2_g[l], ln2_b[l], last=True)

    return out.reshape(1, SEQ, D_MODEL)
```

```python
import jax
import jax.numpy as jnp
import numpy as np
from jax import lax
from jax.experimental import pallas as pl
from jax.experimental.pallas import tpu as pltpu

F32 = jnp.float32
BF16 = jnp.bfloat16
I32 = jnp.int32
HIGHEST = lax.Precision.HIGHEST

D_MODEL = 2048
SEQ = 8192
DEPTH = 2
N_META = 16
N_BRANCH = 4
MIX_W = 512
FOX_HEADS = 4
FOX_HD = 128
GLA_HEADS = 4
GLA_DK = 64
GLA_DV = 128
GLA_RANK = 16
GLA_TAU = 16.0
CONV_K = 3
POOL_WINDOWS = (2, 4, 8, 16)
POOL_GW = 128
N_EXPERTS = 32
N_GROUPS = 8
EXPERTS_PER_GROUP = 4
TOP_K = 2
D_EXPERT = 1024
LN_EPS = 1e-5
DEEPNORM_ALPHA = (2 * DEPTH) ** 0.25

_SPLITS = (512, 512, 512, 4, 256, 256, 512, 16, 512, 512, 512, 512, 512, 8192)
_OFFS = [int(o) for o in np.concatenate([[0], np.cumsum(_SPLITS)])]
(O_FQ, O_FK, O_FV, O_FF, O_GQ, O_GK, O_GV, O_GA, O_GR, O_CB, O_CC, O_CV, O_PZ, O_GZ, P_IN) = _OFFS

LANE = 128
PAD_ROWS = LANE - N_META
T0 = PAD_ROWS
N_TOK = N_META + SEQ
R = PAD_ROWS + N_TOK
TM = 640
TM_PROJ = 1664
HALO = 16

WT_TILE = 128
W_FOX, W_MIX, W_SMALL, W_GATES = 0, 1536, 5120, 5376
N_FOX, N_MIXC, N_SMALL, N_GATES = 1536, 3584, 2 * LANE, 8192
N_WT = W_GATES + N_GATES
C_GQ, C_GK, C_GV, C_GR, C_CB, C_CC, C_CV, C_PZ = 0, 256, 512, 1024, 1536, 2048, 2560, 3072
C_SM = N_MIXC
SM_FF_TILE, SM_GA_TILE = O_FF // LANE, O_GA // LANE
SM_FF = O_FF - SM_FF_TILE * LANE
SM_GA = O_GA - SM_GA_TILE * LANE


def _wt_sources():
    src = []
    for r in range(0, N_WT, WT_TILE):
        if r < N_FOX:
            src.append(O_FQ + r)
        elif r < W_MIX + 1024:
            src.append(O_GQ + r - W_MIX)
        elif r < W_SMALL:
            src.append(O_GR + r - (W_MIX + 1024))
        elif r < W_GATES:
            src.append((SM_FF_TILE, SM_GA_TILE)[(r - W_SMALL) // LANE] * LANE)
        else:
            src.append(O_GZ + r - W_GATES)
    assert all(0 <= s and s + WT_TILE <= P_IN for s in src)
    return src


WT_SRC = _wt_sources()

EXPERT_BLOCK = 128
N_FLAT = N_TOK * TOP_K
N_BLOCKS = -(-N_FLAT // EXPERT_BLOCK) + N_EXPERTS
N_ROWS = N_BLOCKS * EXPERT_BLOCK
UP_AHEAD = 2
UP_X_SLOTS = UP_AHEAD + 1

NEG = -1e30
VMEM_LIMIT = 48 * 1024 * 1024
BIG_VMEM_LIMIT = 56 * 1024 * 1024
MOE_VMEM_LIMIT = BIG_VMEM_LIMIT


def _cparams(sem, vmem=VMEM_LIMIT):
    return pltpu.CompilerParams(dimension_semantics=sem, vmem_limit_bytes=vmem)


def _log_sigmoid(x):
    return jnp.minimum(x, 0.0) - jnp.log1p(jnp.exp(-jnp.abs(x)))


def _sigmoid(x):
    return 1.0 / (1.0 + jnp.exp(-x))


def _layer_norm_rows(x, g, b):
    mu = jnp.mean(x, axis=-1, keepdims=True)
    xc = x - mu
    var = jnp.mean(xc * xc, axis=-1, keepdims=True)
    return xc * lax.rsqrt(var + LN_EPS) * g + b


def _ln_in_kernel(x_ref, meta_ref, g_ref, b_ref, h_ref, hb_ref):
    i = pl.program_id(0)

    @pl.when(i == 0)
    def _():
        h_ref[...] = jnp.zeros_like(h_ref)
        hb_ref[...] = jnp.zeros_like(hb_ref)
        m = _layer_norm_rows(meta_ref[...], g_ref[...], b_ref[...])
        h_ref[PAD_ROWS:, :] = m
        hb_ref[PAD_ROWS:, :] = m.astype(BF16)

    @pl.when(i > 0)
    def _():
        y = _layer_norm_rows(x_ref[...], g_ref[...], b_ref[...])
        h_ref[...] = y
        hb_ref[...] = y.astype(BF16)


def _ln_in(x2d, meta, g, b):
    nb = R // LANE
    return pl.pallas_call(
        _ln_in_kernel,
        grid=(nb,),
        in_specs=[
            pl.BlockSpec((LANE, D_MODEL), lambda i: (jnp.maximum(i - 1, 0), 0)),
            pl.BlockSpec((N_META, D_MODEL), lambda i: (0, 0)),
            pl.BlockSpec((1, D_MODEL), lambda i: (0, 0)),
            pl.BlockSpec((1, D_MODEL), lambda i: (0, 0)),
        ],
        out_specs=[
            pl.BlockSpec((LANE, D_MODEL), lambda i: (i, 0)),
            pl.BlockSpec((LANE, D_MODEL), lambda i: (i, 0)),
        ],
        out_shape=[
            jax.ShapeDtypeStruct((R, D_MODEL), F32),
            jax.ShapeDtypeStruct((R, D_MODEL), BF16),
        ],
        compiler_params=_cparams(("arbitrary",)),
        name="ln_in",
    )(x2d, meta, g.reshape(1, -1), b.reshape(1, -1))


D_CHUNKS = D_MODEL // LANE
FLAT_ROWS = D_CHUNKS * DEPTH


def _wt_kernel(src_ref, w_hbm, o_ref, buf, sem):
    j = pl.program_id(0)
    slot = j % 2

    def tile_copy(t, s):
        row0 = pl.multiple_of(src_ref[t] * FLAT_ROWS, FLAT_ROWS)
        return pltpu.make_async_copy(w_hbm.at[pl.ds(row0, WT_TILE * FLAT_ROWS), :], buf.at[s],
                                     sem.at[s])

    @pl.when(j == 0)
    def _():
        tile_copy(0, 0).start()

    @pl.when(j + 1 < pl.num_programs(0))
    def _():
        tile_copy(j + 1, 1 - slot).start()

    tile_copy(j, slot).wait()
    for l in range(DEPTH):
        for c in range(D_CHUNKS):
            o_ref[l, :, c * LANE:(c + 1) * LANE] = buf[
                slot, pl.ds(c * DEPTH + l, WT_TILE, stride=FLAT_ROWS), :].astype(BF16)


def _transposed_weights(w_in):
    flat = w_in.reshape(DEPTH, D_CHUNKS, LANE, P_IN).transpose(3, 1, 0, 2).reshape(
        P_IN * FLAT_ROWS, LANE)
    grid_spec = pltpu.PrefetchScalarGridSpec(
        num_scalar_prefetch=1,
        grid=(N_WT // WT_TILE,),
        in_specs=[pl.BlockSpec(memory_space=pl.ANY)],
        out_specs=pl.BlockSpec((DEPTH, WT_TILE, D_MODEL), lambda j, src: (0, j, 0)),
        scratch_shapes=[
            pltpu.VMEM((2, WT_TILE * FLAT_ROWS, LANE), F32),
            pltpu.SemaphoreType.DMA((2,)),
        ],
    )
    return pl.pallas_call(
        _wt_kernel,
        grid_spec=grid_spec,
        out_shape=jax.ShapeDtypeStruct((DEPTH, N_WT, D_MODEL), BF16),
        compiler_params=_cparams(("arbitrary",)),
        name="transposed_weights",
    )(jnp.asarray(WT_SRC, I32), flat)


_NT = (((1,), (1,)), ((), ()))


def _mm_nt_kernel(a_ref, wt_ref, o_ref):
    o_ref[...] = lax.dot_general(a_ref[...], wt_ref[...], _NT,
                                 preferred_element_type=F32).astype(o_ref.dtype)


def _matmul_nt(a, wt, l, row0, n, out_dtype, tm, tn, name):
    m, k = a.shape
    return pl.pallas_call(
        _mm_nt_kernel,
        grid=(m // tm, n // tn),
        in_specs=[
            pl.BlockSpec((tm, k), lambda i, j: (i, 0)),
            pl.BlockSpec((None, tn, k), lambda i, j: (l, row0 // tn + j, 0)),
        ],
        out_specs=pl.BlockSpec((tm, tn), lambda i, j: (i, j)),
        out_shape=jax.ShapeDtypeStruct((m, n), out_dtype),
        compiler_params=_cparams(("parallel", "arbitrary")),
        name=name,
    )(a, wt)


def _matmul_tt(wt, l, row0, n, a, out_dtype, tm, name):
    m, k = a.shape
    return pl.pallas_call(
        _mm_nt_kernel,
        grid=(m // tm,),
        in_specs=[
            pl.BlockSpec((None, n, k), lambda i: (l, row0 // n, 0)),
            pl.BlockSpec((tm, k), lambda i: (i, 0)),
        ],
        out_specs=pl.BlockSpec((n, tm), lambda i: (0, i)),
        out_shape=jax.ShapeDtypeStruct((n, m), out_dtype),
        compiler_params=_cparams(("parallel",)),
        name=name,
    )(wt, a)


def _fox_gate_kernel(zs_ref, bias_ref, c_ref, carry_ref):
    i = pl.program_id(0)

    @pl.when(i == 0)
    def _():
        carry_ref[...] = jnp.zeros_like(carry_ref)

    rows = i * TM + lax.broadcasted_iota(I32, (TM, LANE), 0)
    lf = _log_sigmoid(zs_ref[...] + bias_ref[...])
    lf = jnp.where(rows >= T0, lf, 0.0)
    tri = (lax.broadcasted_iota(I32, (TM, TM), 0)
           >= lax.broadcasted_iota(I32, (TM, TM), 1)).astype(F32)
    c = jnp.dot(tri, lf, precision=HIGHEST, preferred_element_type=F32) + carry_ref[...]
    c_ref[...] = c
    carry_ref[...] = c[TM - 1:TM, :]


def _fox_gate(z, bias_row):
    return pl.pallas_call(
        _fox_gate_kernel,
        grid=(R // TM,),
        in_specs=[
            pl.BlockSpec((TM, LANE), lambda i: (i, C_SM // LANE)),
            pl.BlockSpec((1, LANE), lambda i: (0, 0)),
        ],
        out_specs=pl.BlockSpec((TM, LANE), lambda i: (i, 0)),
        out_shape=jax.ShapeDtypeStruct((R, LANE), F32),
        scratch_shapes=[pltpu.VMEM((1, LANE), F32)],
        compiler_params=_cparams(("arbitrary",)),
        name="fox_gate",
    )(z, bias_row)


TQ = TM
N_QB = R // TQ
_PAIRS = [(qi, kj) for qi in range(N_QB) for kj in range(qi + 1)]
N_PAIRS = len(_PAIRS)


LOG2E = 1.4426950408889634
FOX_HPS = FOX_HEADS


def _fox_kernel(qi_tab, kj_tab, q_ref, k_ref, vt_ref, ck_ref, o_ref, m_sc, l_sc, acc_sc):
    p = pl.program_id(1)
    qi = qi_tab[p]
    kj = kj_tab[p]

    @pl.when(kj == 0)
    def _():
        m_sc[...] = jnp.full_like(m_sc, NEG)
        l_sc[...] = jnp.zeros_like(l_sc)
        acc_sc[...] = jnp.zeros_like(acc_sc)

    kpos = kj * TQ + lax.broadcasted_iota(I32, (TQ, 1), 0)
    c1 = FOX_HD ** -0.5 * LOG2E

    def step(causal):
        for hh in range(FOX_HPS):
            lanes = slice(hh * FOX_HD, (hh + 1) * FOX_HD)
            ck = ck_ref[:, SM_FF + hh:SM_FF + hh + 1]
            ckl = jnp.where(kpos >= T0, ck * LOG2E, -NEG)
            t = lax.dot_general(k_ref[:, lanes], q_ref[:, lanes], (((1,), (1,)), ((), ())),
                                preferred_element_type=F32) * c1 - ckl
            if causal:
                ahead = (lax.broadcasted_iota(I32, (TQ, TQ), 0)
                         - lax.broadcasted_iota(I32, (TQ, TQ), 1))
                t = jnp.where(ahead <= 0, t, NEG)
            m_prev = m_sc[hh]
            m_new = jnp.maximum(m_prev, jnp.max(t, axis=0, keepdims=True))
            alpha = jnp.exp2(m_prev - m_new)
            pr = jnp.exp2(t - m_new)
            l_sc[hh] = alpha * l_sc[hh] + jnp.sum(pr, axis=0, keepdims=True)
            acc_sc[hh] = alpha * acc_sc[hh] + jnp.dot(vt_ref[lanes, :], pr.astype(BF16),
                                                      preferred_element_type=F32)
            m_sc[hh] = m_new

    @pl.when(kj == qi)
    def _():
        step(True)

    @pl.when(kj != qi)
    def _():
        step(False)

    @pl.when(kj == qi)
    def _():
        for hh in range(FOX_HPS):
            o_ref[:, hh * FOX_HD:(hh + 1) * FOX_HD] = (acc_sc[hh] / l_sc[hh]).T.astype(o_ref.dtype)


def _fox_attention(zf, vt, c_col):
    qi_tab = jnp.asarray([p[0] for p in _PAIRS], I32)
    kj_tab = jnp.asarray([p[1] for p in _PAIRS], I32)
    hw = FOX_HPS * FOX_HD
    grid_spec = pltpu.PrefetchScalarGridSpec(
        num_scalar_prefetch=2,
        grid=(FOX_HEADS // FOX_HPS, N_PAIRS),
        in_specs=[
            pl.BlockSpec((TQ, hw), lambda h, p, qt, kt: (qt[p], h)),
            pl.BlockSpec((TQ, hw), lambda h, p, qt, kt: (kt[p], FOX_HEADS // FOX_HPS + h)),
            pl.BlockSpec((hw, TQ), lambda h, p, qt, kt: (h, kt[p])),
            pl.BlockSpec((TQ, LANE), lambda h, p, qt, kt: (kt[p], 0)),
        ],
        out_specs=pl.BlockSpec((TQ, hw), lambda h, p, qt, kt: (qt[p], h)),
        scratch_shapes=[
            pltpu.VMEM((FOX_HPS, 1, TQ), F32),
            pltpu.VMEM((FOX_HPS, 1, TQ), F32),
            pltpu.VMEM((FOX_HPS, FOX_HD, TQ), F32),
        ],
    )
    return pl.pallas_call(
        _fox_kernel,
        grid_spec=grid_spec,
        out_shape=jax.ShapeDtypeStruct((R, MIX_W), BF16),
        compiler_params=_cparams(("parallel", "arbitrary")),
        name="fox_attention",
    )(qi_tab, kj_tab, zf, zf, vt, c_col)


GLA_CHUNK = 64
GLA_UNROLL = 5


def _gla_kernel(q_ref, k_ref, v_ref, gr_ref, zs_ref, wa2_ref, ba_ref, gn_ref, o_ref, st_ref, la_ref):
    i = pl.program_id(0)

    @pl.when(i == 0)
    def _():
        st_ref[...] = jnp.zeros_like(st_ref)

    la = jnp.dot(zs_ref[...], wa2_ref[...], precision=HIGHEST, preferred_element_type=F32)
    la_ref[...] = _log_sigmoid(la + ba_ref[...]) * (1.0 / GLA_TAU)

    c_r = lax.broadcasted_iota(I32, (GLA_CHUNK, GLA_CHUNK), 0)
    c_c = lax.broadcasted_iota(I32, (GLA_CHUNK, GLA_CHUNK), 1)
    tri_b = c_r >= c_c
    tri = jnp.where(tri_b, 1.0, 0.0).astype(BF16)

    def chunk(c, carry):
        r0 = pl.multiple_of(c * GLA_CHUNK, GLA_CHUNK)
        rows = pl.ds(r0, GLA_CHUNK)
        g = la_ref[rows, :]
        g_hi = g.astype(BF16)
        g_lo = (g - g_hi.astype(F32)).astype(BF16)
        b = (jnp.dot(tri, g_hi, preferred_element_type=F32)
             + jnp.dot(tri, g_lo, preferred_element_type=F32))
        b_last = b[GLA_CHUNK - 1:GLA_CHUNK, :]
        b_mid = b[GLA_CHUNK // 2 - 1:GLA_CHUNK // 2, :]
        e_last = jnp.exp(b_last)
        q = q_ref[rows, :] * (GLA_DK ** -0.5)
        k = k_ref[rows, :]
        qm = q * jnp.exp(b - b_mid)
        km = k * jnp.exp(b_mid - b)
        qs = q * jnp.exp(b)
        kh = k * jnp.exp(b_last - b)
        for h in range(GLA_HEADS):
            ks = slice(h * GLA_DK, (h + 1) * GLA_DK)
            vs = slice(h * GLA_DV, (h + 1) * GLA_DV)
            kh_h = kh[:, ks].astype(BF16)
            v_h = v_ref[rows, vs]
            att = lax.dot_general(qm[:, ks].astype(BF16), km[:, ks].astype(BF16),
                                  (((1,), (1,)), ((), ())), preferred_element_type=F32)
            att = jnp.where(tri_b, att, 0.0)
            st = st_ref[h]
            o = jnp.dot(att.astype(BF16), v_h.astype(BF16), preferred_element_type=F32)
            o = o + lax.dot_general(qs[:, ks].astype(BF16), st.astype(BF16),
                                    (((1,), (1,)), ((), ())), preferred_element_type=F32)
            st_ref[h] = st * e_last[:, ks] + jnp.dot(v_h.T.astype(BF16), kh_h,
                                                     preferred_element_type=F32)
            ms = jnp.mean(o * o, axis=-1, keepdims=True)
            on = o * lax.rsqrt(ms + LN_EPS) * gn_ref[:, vs]
            gate = gr_ref[rows, vs]
            o_ref[rows, vs] = (on * (gate * _sigmoid(gate))).astype(o_ref.dtype)
        return carry

    lax.fori_loop(0, TM // GLA_CHUNK, chunk, 0, unroll=GLA_UNROLL)


def _gla(z, wa2p, ba, gn):
    return pl.pallas_call(
        _gla_kernel,
        grid=(R // TM,),
        in_specs=[
            pl.BlockSpec((TM, 256), lambda i: (i, C_GQ // 256)),
            pl.BlockSpec((TM, 256), lambda i: (i, C_GK // 256)),
            pl.BlockSpec((TM, 512), lambda i: (i, C_GV // 512)),
            pl.BlockSpec((TM, 512), lambda i: (i, C_GR // 512)),
            pl.BlockSpec((TM, LANE), lambda i: (i, C_SM // LANE + 1)),
            pl.BlockSpec((LANE, 256), lambda i: (0, 0)),
            pl.BlockSpec((1, 256), lambda i: (0, 0)),
            pl.BlockSpec((1, 512), lambda i: (0, 0)),
        ],
        out_specs=pl.BlockSpec((TM, MIX_W), lambda i: (i, 0)),
        out_shape=jax.ShapeDtypeStruct((R, MIX_W), BF16),
        scratch_shapes=[
            pltpu.VMEM((GLA_HEADS, GLA_DV, GLA_DK), F32),
            pltpu.VMEM((TM, GLA_HEADS * GLA_DK), F32),
        ],
        compiler_params=_cparams(("arbitrary",)),
        name="gla",
    )(z, z, z, z, z, wa2p, ba, gn)


def _local_kernel(cb_ref, cc_ref, cv_ref, pz_ref, cw_ref, pw_ref, ps_ref, oc_ref, od_ref, u_sc, p_sc):
    i = pl.program_id(0)

    @pl.when(i == 0)
    def _():
        u_sc[0:HALO, :] = jnp.zeros((HALO, MIX_W), F32)
        p_sc[0:HALO, :] = jnp.zeros((HALO, MIX_W), F32)

    @pl.when(i > 0)
    def _():
        u_sc[0:HALO, :] = u_sc[TM:TM + HALO, :]
        p_sc[0:HALO, :] = p_sc[TM:TM + HALO, :]

    u = cc_ref[...] * cv_ref[...]
    pz = pz_ref[...]
    u_sc[HALO:, :] = u
    p_sc[HALO:, :] = pz

    y = (cw_ref[2:3, :] * u + cw_ref[1:2, :] * u_sc[HALO - 1:HALO - 1 + TM, :]
         + cw_ref[0:1, :] * u_sc[HALO - 2:HALO - 2 + TM, :])
    oc_ref[...] = (cb_ref[...] * y).astype(oc_ref.dtype)

    tok = i * TM - T0 + lax.broadcasted_iota(I32, (TM, 1), 0)
    cnt_small = jnp.maximum(tok + 1, 1).astype(F32)
    for g, w in enumerate(POOL_WINDOWS):
        cols = slice(g * POOL_GW, (g + 1) * POOL_GW)
        x = pz[:, cols]
        s = x
        for j in range(1, w):
            s = s + p_sc[HALO - j:HALO - j + TM, cols]
        inv_cnt = jnp.where(tok + 1 >= w, 1.0 / w, 1.0 / cnt_small)
        pooled = s * inv_cnt - x
        od = jnp.dot(pooled.astype(BF16), pw_ref[g], preferred_element_type=F32)
        od_ref[:, cols] = (od * ps_ref[:, cols]).astype(od_ref.dtype)


def _local_mixers(z, conv_w, pool_w_bf, pool_scale):
    cw = jnp.zeros((8, MIX_W), F32).at[:CONV_K].set(conv_w)
    blk = lambda c: pl.BlockSpec((TM, MIX_W), lambda i, c=c: (i, c // MIX_W))
    return pl.pallas_call(
        _local_kernel,
        grid=(R // TM,),
        in_specs=[
            blk(C_CB), blk(C_CC), blk(C_CV), blk(C_PZ),
            pl.BlockSpec((8, MIX_W), lambda i: (0, 0)),
            pl.BlockSpec((len(POOL_WINDOWS), POOL_GW, POOL_GW), lambda i: (0, 0, 0)),
            pl.BlockSpec((1, MIX_W), lambda i: (0, 0)),
        ],
        out_specs=[
            pl.BlockSpec((TM, MIX_W), lambda i: (i, 0)),
            pl.BlockSpec((TM, MIX_W), lambda i: (i, 0)),
        ],
        out_shape=[
            jax.ShapeDtypeStruct((R, MIX_W), BF16),
            jax.ShapeDtypeStruct((R, MIX_W), BF16),
        ],
        scratch_shapes=[
            pltpu.VMEM((TM + HALO, MIX_W), F32),
            pltpu.VMEM((TM + HALO, MIX_W), F32),
        ],
        compiler_params=_cparams(("arbitrary",)),
        name="conv_pool",
    )(z, z, z, z, cw, pool_w_bf, pool_scale.reshape(1, -1))


TN_MERGE = 256


def _merge_kernel(hb_ref, oa_ref, ob_ref, oc_ref, od_ref, wg0_ref, wg1_ref, wg2_ref, wg3_ref,
                  gb_ref, wb_ref, wo_ref, out_ref):
    j = pl.program_id(1)

    @pl.when(j == 0)
    def _():
        out_ref[...] = jnp.zeros_like(out_ref)

    hb = hb_ref[...]
    mixed = None
    for b, (o_ref, wg_ref) in enumerate(((oa_ref, wg0_ref), (ob_ref, wg1_ref),
                                         (oc_ref, wg2_ref), (od_ref, wg3_ref))):
        gate = _sigmoid(lax.dot_general(hb, wg_ref[...], _NT, preferred_element_type=F32)
                        + gb_ref[b:b + 1, :])
        proj = jnp.dot(o_ref[...], wb_ref[b], preferred_element_type=F32)
        term = gate * proj
        mixed = term if mixed is None else mixed + term
    out_ref[...] += jnp.dot(mixed.astype(BF16), wo_ref[...], preferred_element_type=F32)


def _merge(hb, o_a, o_b, o_c, o_d, w_all, gate_b, wb_bf, wo_bf, l):
    tn = TN_MERGE
    nj = D_MODEL // tn
    row = lambda w: pl.BlockSpec((TM, w), lambda i, j: (i, 0))
    wg = lambda b: pl.BlockSpec((None, tn, D_MODEL),
                                lambda i, j, b=b: (l, W_GATES // tn + b * nj + j, 0))
    return pl.pallas_call(
        _merge_kernel,
        grid=(R // TM, nj),
        in_specs=[
            row(D_MODEL), row(MIX_W), row(MIX_W), row(MIX_W), row(MIX_W),
            wg(0), wg(1), wg(2), wg(3),
            pl.BlockSpec((N_BRANCH, tn), lambda i, j: (0, j)),
            pl.BlockSpec((None, N_BRANCH, MIX_W, tn), lambda i, j: (l, 0, 0, j)),
            pl.BlockSpec((None, tn, D_MODEL), lambda i, j: (l, j, 0)),
        ],
        out_specs=pl.BlockSpec((TM, D_MODEL), lambda i, j: (i, 0)),
        out_shape=jax.ShapeDtypeStruct((R, D_MODEL), F32),
        compiler_params=_cparams(("parallel", "arbitrary"), vmem=BIG_VMEM_LIMIT),
        name="merge",
    )(hb, o_a, o_b, o_c, o_d, w_all, w_all, w_all, w_all, gate_b, wb_bf, wo_bf)


def _post_ln(h, delta, g, b, row0):
    y = _layer_norm_rows(DEEPNORM_ALPHA * h + delta, g, b)
    rows = row0 + lax.broadcasted_iota(I32, (y.shape[0], 1), 0)
    return jnp.where(rows >= T0, y, 0.0)


def _first_of(cands, target):
    idx = jnp.full(target.shape, len(cands) - 1, I32)
    for j in range(len(cands) - 2, -1, -1):
        idx = jnp.where(cands[j] == target, j, idx)
    return idx


def _pick(cands, idx):
    out = cands[-1]
    for j in range(len(cands) - 2, -1, -1):
        out = jnp.where(idx == j, cands[j], out)
    return out


def _ln1_route_kernel(h_ref, mix_ref, g_ref, b_ref, rwt_ref, rb_ref,
                      h1_ref, mi_ref, mf_ref, cnt_ref, carry_sc):
    i = pl.program_id(0)

    @pl.when(i == 0)
    def _():
        carry_sc[...] = jnp.zeros_like(carry_sc)

    y = _post_ln(h_ref[...], mix_ref[...], g_ref[...], b_ref[...], i * TM)
    h1_ref[...] = y

    logits = lax.dot_general(rwt_ref[...], y, (((1,), (1,)), ((), ())), precision=HIGHEST,
                             preferred_element_type=F32)
    aff = _sigmoid(logits)
    sel = aff + rb_ref[...]
    xs = [sel[j * N_GROUPS:(j + 1) * N_GROUPS, :] for j in range(EXPERTS_PER_GROUP)]
    afs = [aff[j * N_GROUPS:(j + 1) * N_GROUPS, :] for j in range(EXPERTS_PER_GROUP)]

    score = None
    for a in range(EXPERTS_PER_GROUP):
        for bb in range(a + 1, EXPERTS_PER_GROUP):
            pair = xs[a] + xs[bb]
            score = pair if score is None else jnp.maximum(score, pair)
    giota = lax.broadcasted_iota(I32, (N_GROUPS, TM), 0)
    gmax = jnp.max(score, axis=0, keepdims=True)
    grp = jnp.min(jnp.where(score == gmax, giota, N_GROUPS), axis=0, keepdims=True)
    gsel = giota == grp
    cs = [jnp.max(jnp.where(gsel, x, -jnp.inf), axis=0, keepdims=True) for x in xs]
    acs = [jnp.sum(jnp.where(gsel, a, 0.0), axis=0, keepdims=True) for a in afs]

    m1 = jnp.maximum(jnp.maximum(cs[0], cs[1]), jnp.maximum(cs[2], cs[3]))
    i0 = _first_of(cs, m1)
    ds = [jnp.where(i0 == j, -jnp.inf, cs[j]) for j in range(EXPERTS_PER_GROUP)]
    m2 = jnp.maximum(jnp.maximum(ds[0], ds[1]), jnp.maximum(ds[2], ds[3]))
    i1 = _first_of(ds, m2)
    a0 = _pick(acs, i0)
    a1 = _pick(acs, i1)
    denom = a0 + a1

    pos = i * TM + lax.broadcasted_iota(I32, (1, TM), 1)
    valid = pos >= T0
    riota = lax.broadcasted_iota(I32, (N_EXPERTS, TM), 0)
    oh0 = (riota == i0 * N_GROUPS + grp) & valid
    oh1 = (riota == i1 * N_GROUPS + grp) & valid
    ohf = jnp.where(oh0 | oh1, 1.0, 0.0)
    before = (lax.broadcasted_iota(I32, (TM, TM), 0)
              < lax.broadcasted_iota(I32, (TM, TM), 1)).astype(BF16)
    cum = jnp.dot(ohf.astype(BF16), before, preferred_element_type=F32) + carry_sc[...]
    rank0 = jnp.sum(jnp.where(oh0, cum, 0.0), axis=0, keepdims=True)
    rank1 = jnp.sum(jnp.where(oh1, cum, 0.0), axis=0, keepdims=True)
    carry = carry_sc[...] + jnp.sum(ohf, axis=1, keepdims=True)
    carry_sc[...] = carry
    cnt_ref[...] = jnp.broadcast_to(carry, cnt_ref.shape)

    zi = jnp.zeros((1, TM), I32)
    mi_ref[...] = jnp.concatenate(
        [grp * EXPERTS_PER_GROUP + i0, grp * EXPERTS_PER_GROUP + i1,
         rank0.astype(I32), rank1.astype(I32), zi, zi, zi, zi], axis=0)
    zf = jnp.zeros((1, TM), F32)
    mf_ref[...] = jnp.concatenate([a0 / denom, a1 / denom, zf, zf, zf, zf, zf, zf], axis=0)


def _ln1_route(h, mix, g, b, router_wt, router_bc):
    row = pl.BlockSpec((TM, D_MODEL), lambda i: (i, 0))
    vec = pl.BlockSpec((1, D_MODEL), lambda i: (0, 0))
    meta = pl.BlockSpec((8, TM), lambda i: (0, i))
    return pl.pallas_call(
        _ln1_route_kernel,
        grid=(R // TM,),
        in_specs=[row, row, vec, vec,
                  pl.BlockSpec((N_EXPERTS, D_MODEL), lambda i: (0, 0)),
                  pl.BlockSpec((N_EXPERTS, 1), lambda i: (0, 0))],
        out_specs=[row, meta, meta, pl.BlockSpec((N_EXPERTS, LANE), lambda i: (0, 0))],
        out_shape=[
            jax.ShapeDtypeStruct((R, D_MODEL), F32),
            jax.ShapeDtypeStruct((8, R), I32),
            jax.ShapeDtypeStruct((8, R), F32),
            jax.ShapeDtypeStruct((N_EXPERTS, LANE), F32),
        ],
        scratch_shapes=[pltpu.VMEM((N_EXPERTS, 1), F32)],
        compiler_params=_cparams(("arbitrary",)),
        name="ln1_route",
    )(h, mix, g.reshape(1, -1), b.reshape(1, -1), router_wt, router_bc)


N_ROW_SRC = N_ROWS + UP_AHEAD * EXPERT_BLOCK


def _row_src_kernel(d0_ref, d1_ref, out_ref):
    def clear(i, carry):
        out_ref[i] = 0
        return carry

    lax.fori_loop(0, N_ROW_SRC, clear, 0, unroll=16)

    def place(t, carry):
        out_ref[d0_ref[t]] = t
        out_ref[d1_ref[t]] = t
        return carry

    lax.fori_loop(T0, R, place, 0, unroll=8)


def _row_sources(d0, d1):
    smem = pl.BlockSpec(memory_space=pltpu.SMEM)
    return pl.pallas_call(
        _row_src_kernel,
        in_specs=[smem, smem],
        out_specs=smem,
        out_shape=jax.ShapeDtypeStruct((N_ROW_SRC,), I32),
        name="row_sources",
    )(d0, d1)


def _dispatch_tables(mi, counts_slot_major):
    counts = counts_slot_major.reshape(EXPERTS_PER_GROUP, N_GROUPS).T.reshape(N_EXPERTS).astype(I32)
    padded = (counts + EXPERT_BLOCK - 1) // EXPERT_BLOCK * EXPERT_BLOCK
    pad_end = jnp.cumsum(padded)
    pad_start = pad_end - padded
    e_iota = jnp.arange(N_EXPERTS, dtype=I32)
    rows_ok = jnp.arange(R) >= T0

    def dest(eid, rank):
        start = jnp.sum(jnp.where(eid[:, None] == e_iota[None, :], pad_start[None, :], 0), axis=1)
        return jnp.where(rows_ok, start + rank, 0).astype(I32)

    d0 = dest(mi[0], mi[2])
    d1 = dest(mi[1], mi[3])
    blk_start = (pad_start // EXPERT_BLOCK).astype(I32)
    n_blk = (padded // EXPERT_BLOCK).astype(I32)
    n_used = (pad_end[-1] // EXPERT_BLOCK).astype(I32).reshape(1)
    row_src = _row_sources(d0, d1)
    blk = jnp.arange(N_BLOCKS + UP_AHEAD, dtype=I32)
    owner = (blk[:, None] >= blk_start[None, :]) & (blk[:, None] < (blk_start + n_blk)[None, :])
    left = counts[None, :] - (blk[:, None] - blk_start[None, :]) * EXPERT_BLOCK
    n_valid = jnp.sum(jnp.where(owner, jnp.clip(left, 0, EXPERT_BLOCK), 0), axis=1).astype(I32)
    return d0, d1, blk_start, n_blk, n_used, row_src, n_valid


def _row_copy(src, src_row, dst, dst_row, sem):
    return pltpu.make_async_copy(src.at[pl.ds(src_row, 1), :], dst.at[pl.ds(dst_row, 1), :], sem)


def _issue_row_gather(rs_ref, nv_ref, g, h_hbm, buf, sem, full=False):
    base = g * EXPERT_BLOCK
    nv = nv_ref[g]
    for r in range(EXPERT_BLOCK):
        if full:
            _row_copy(h_hbm, rs_ref[base + r], buf, r, sem).start(priority=r % 2)
        else:
            @pl.when(r < nv)
            def _():
                _row_copy(h_hbm, rs_ref[base + r], buf, r, sem).start(priority=r % 2)


def _wait_row_gather(nv_ref, g, h_hbm, buf, sem, full=False):
    if full:
        pltpu.make_async_copy(h_hbm.at[pl.ds(0, EXPERT_BLOCK), :], buf, sem).wait()
        return
    nv = nv_ref[g]
    k = EXPERT_BLOCK
    while k >= 1:
        @pl.when((nv & k) != 0)
        def _():
            pltpu.make_async_copy(h_hbm.at[pl.ds(0, k), :], buf.at[pl.ds(0, k), :], sem).wait()
        k //= 2


def _block_rows(g):
    return pl.ds(pl.multiple_of(g * EXPERT_BLOCK, EXPERT_BLOCK), EXPERT_BLOCK)


def _finish_writes(out_copy, obuf, nu):
    n_slots = obuf.shape[0]
    for back in range(1, n_slots + 1):
        @pl.when(nu >= back)
        def _():
            out_copy(nu - back, (nu - back) % n_slots).wait()

    obuf[0] = jnp.zeros(obuf.shape[1:], obuf.dtype)

    def zero_block(g, carry):
        cp = out_copy(g, 0)
        cp.start()
        cp.wait()
        return carry

    lax.fori_loop(nu, N_BLOCKS, zero_block, 0)


def _moe_up_kernel(bs_ref, nb_ref, nu_ref, rs_ref, nv_ref, h_hbm, wg_ref, wu_ref, o_hbm,
                   wg_sc, wu_sc, xbuf, xb_sc, obuf, xsem, osem):
    e = pl.program_id(0)
    nb = nb_ref[e]
    g0 = bs_ref[e]

    def out_copy(g, slot):
        return pltpu.make_async_copy(obuf.at[slot], o_hbm.at[_block_rows(g), :], osem.at[slot])

    @pl.when(e == 0)
    def _():
        xbuf[...] = jnp.zeros_like(xbuf)
        for g in range(UP_AHEAD):
            _issue_row_gather(rs_ref, nv_ref, g, h_hbm, xbuf.at[g], xsem.at[g])

    @pl.when(nb > 0)
    def _():
        wg_sc[...] = wg_ref[...].astype(BF16)
        wu_sc[...] = wu_ref[...].astype(BF16)

        def block(j, carry, full):
            g = g0 + j
            slot = g % 2
            xs = g % UP_X_SLOTS
            xs_next = (g + UP_AHEAD) % UP_X_SLOTS

            @pl.when(g >= 2)
            def _():
                out_copy(g - 2, slot).wait()

            _wait_row_gather(nv_ref, g, h_hbm, xbuf.at[xs], xsem.at[xs], full)
            xb_sc[...] = xbuf[xs].astype(BF16)
            _issue_row_gather(rs_ref, nv_ref, g + UP_AHEAD, h_hbm, xbuf.at[xs_next],
                              xsem.at[xs_next], full)
            x = xb_sc[...]
            gate = jnp.dot(x, wg_sc[...], preferred_element_type=F32)
            up = jnp.dot(x, wu_sc[...], preferred_element_type=F32)
            obuf[slot] = (gate * _sigmoid(gate) * up).astype(BF16)
            out_copy(g, slot).start()
            return carry

        n_full = jnp.maximum(nb - 1 - UP_AHEAD, 0)
        lax.fori_loop(0, n_full, lambda j, c: block(j, c, True), 0)
        lax.fori_loop(n_full, nb, lambda j, c: block(j, c, False), 0)

    @pl.when(e == N_EXPERTS - 1)
    def _():
        _finish_writes(out_copy, obuf, nu_ref[0])


def _moe_up(blk_start, n_blk, n_used, row_src, n_valid, h1, w_gate, w_up, l):
    wspec = pl.BlockSpec((None, None, D_MODEL, D_EXPERT), lambda e, *_: (l, e, 0, 0))
    grid_spec = pltpu.PrefetchScalarGridSpec(
        num_scalar_prefetch=5,
        grid=(N_EXPERTS,),
        in_specs=[pl.BlockSpec(memory_space=pl.ANY), wspec, wspec],
        out_specs=pl.BlockSpec(memory_space=pl.ANY),
        scratch_shapes=[
            pltpu.VMEM((D_MODEL, D_EXPERT), BF16),
            pltpu.VMEM((D_MODEL, D_EXPERT), BF16),
            pltpu.VMEM((UP_X_SLOTS, EXPERT_BLOCK, D_MODEL), F32),
            pltpu.VMEM((EXPERT_BLOCK, D_MODEL), BF16),
            pltpu.VMEM((2, EXPERT_BLOCK, D_EXPERT), BF16),
            pltpu.SemaphoreType.DMA((UP_X_SLOTS,)),
            pltpu.SemaphoreType.DMA((2,)),
        ],
    )
    return pl.pallas_call(
        _moe_up_kernel,
        grid_spec=grid_spec,
        out_shape=jax.ShapeDtypeStruct((N_ROWS, D_EXPERT), BF16),
        compiler_params=_cparams(("arbitrary",), vmem=MOE_VMEM_LIMIT),
        name="moe_up",
    )(blk_start, n_blk, n_used, row_src, n_valid, h1, w_gate, w_up)


DOWN_IN_SLOTS = 4
DOWN_OUT_SLOTS = 3


def _moe_down_kernel(bs_ref, nb_ref, nu_ref, x_hbm, wd_ref, y_hbm, wd_sc, xbuf, obuf, xsem, osem):
    e = pl.program_id(0)
    nb = nb_ref[e]
    g0 = bs_ref[e]
    nu = nu_ref[0]

    def in_copy(g):
        s = g % DOWN_IN_SLOTS
        return pltpu.make_async_copy(x_hbm.at[_block_rows(g), :], xbuf.at[s], xsem.at[s])

    def out_copy(g, slot):
        return pltpu.make_async_copy(obuf.at[slot], y_hbm.at[_block_rows(g), :], osem.at[slot])

    @pl.when(e == 0)
    def _():
        for g in range(DOWN_IN_SLOTS - 1):
            @pl.when(g < nu)
            def _():
                in_copy(g).start()

    @pl.when(nb > 0)
    def _():
        wd_sc[...] = wd_ref[...].astype(BF16)

        def block(j, carry):
            g = g0 + j
            slot = g % DOWN_OUT_SLOTS

            @pl.when(g >= DOWN_OUT_SLOTS)
            def _():
                out_copy(g - DOWN_OUT_SLOTS, slot).wait()

            in_copy(g).wait()

            @pl.when(g + DOWN_IN_SLOTS - 1 < nu)
            def _():
                in_copy(g + DOWN_IN_SLOTS - 1).start()

            obuf[slot] = jnp.dot(xbuf[g % DOWN_IN_SLOTS], wd_sc[...], preferred_element_type=F32)
            out_copy(g, slot).start()
            return carry

        lax.fori_loop(0, nb, block, 0)

    @pl.when(e == N_EXPERTS - 1)
    def _():
        _finish_writes(out_copy, obuf, nu)


def _moe_down(blk_start, n_blk, n_used, hmid, w_down, l):
    grid_spec = pltpu.PrefetchScalarGridSpec(
        num_scalar_prefetch=3,
        grid=(N_EXPERTS,),
        in_specs=[
            pl.BlockSpec(memory_space=pl.ANY),
            pl.BlockSpec((None, None, D_EXPERT, D_MODEL), lambda e, *_: (l, e, 0, 0)),
        ],
        out_specs=pl.BlockSpec(memory_space=pl.ANY),
        scratch_shapes=[
            pltpu.VMEM((D_EXPERT, D_MODEL), BF16),
            pltpu.VMEM((DOWN_IN_SLOTS, EXPERT_BLOCK, D_EXPERT), BF16),
            pltpu.VMEM((DOWN_OUT_SLOTS, EXPERT_BLOCK, D_MODEL), F32),
            pltpu.SemaphoreType.DMA((DOWN_IN_SLOTS,)),
            pltpu.SemaphoreType.DMA((DOWN_OUT_SLOTS,)),
        ],
    )
    return pl.pallas_call(
        _moe_down_kernel,
        grid_spec=grid_spec,
        out_shape=jax.ShapeDtypeStruct((N_ROWS, D_MODEL), F32),
        compiler_params=_cparams(("arbitrary",), vmem=MOE_VMEM_LIMIT),
        name="moe_down",
    )(blk_start, n_blk, n_used, hmid, w_down)


def _combine_kernel(d0_ref, d1_ref, y_hbm, h1_ref, mf_ref, g_ref, b_ref, h2_ref, h2b_ref,
                    buf0, buf1, sem):
    i = pl.program_id(0)
    slot = i % 2

    def issue(tile, s):
        base = tile * LANE

        def body(r, carry):
            _row_copy(y_hbm, d0_ref[base + r], buf0.at[s], r, sem.at[s]).start(priority=0)
            _row_copy(y_hbm, d1_ref[base + r], buf1.at[s], r, sem.at[s]).start(priority=1)
            return carry

        lax.fori_loop(0, LANE, body, 0)

    @pl.when(i == 0)
    def _():
        issue(0, 0)

    @pl.when(i + 1 < pl.num_programs(0))
    def _():
        issue(i + 1, 1 - slot)

    def drain(r, carry):
        _row_copy(y_hbm, 0, buf0.at[slot], r, sem.at[slot]).wait()
        _row_copy(y_hbm, 0, buf1.at[slot], r, sem.at[slot]).wait()
        return carry

    lax.fori_loop(0, LANE, drain, 0)
    wt = mf_ref[...].T
    ffn = wt[:, 0:1] * buf0[slot] + wt[:, 1:2] * buf1[slot]
    y = _post_ln(h1_ref[...], ffn, g_ref[...], b_ref[...], i * LANE)
    h2_ref[...] = y
    if h2b_ref is not None:
        h2b_ref[...] = y.astype(BF16)


def _combine_last_kernel(d0_ref, d1_ref, y_hbm, h1_ref, mf_ref, g_ref, b_ref, out_ref, buf0, buf1, sem):
    _combine_kernel(d0_ref, d1_ref, y_hbm, h1_ref, mf_ref, g_ref, b_ref, out_ref, None, buf0, buf1, sem)


def _combine_ln2(d0, d1, y_rows, h1, mf, g, b, last):
    row = lambda i, d0, d1: (i, 0)
    vec = pl.BlockSpec((1, D_MODEL), lambda i, d0, d1: (0, 0))
    if last:
        frames = lambda i, d0, d1: (jnp.maximum(i - 1, 0), 0)
        out_specs = [pl.BlockSpec((LANE, D_MODEL), frames)]
        out_shape = [jax.ShapeDtypeStruct((SEQ, D_MODEL), F32)]
    else:
        out_specs = [pl.BlockSpec((LANE, D_MODEL), row), pl.BlockSpec((LANE, D_MODEL), row)]
        out_shape = [jax.ShapeDtypeStruct((R, D_MODEL), F32), jax.ShapeDtypeStruct((R, D_MODEL), BF16)]
    grid_spec = pltpu.PrefetchScalarGridSpec(
        num_scalar_prefetch=2,
        grid=(R // LANE,),
        in_specs=[
            pl.BlockSpec(memory_space=pl.ANY),
            pl.BlockSpec((LANE, D_MODEL), row),
            pl.BlockSpec((8, LANE), lambda i, d0, d1: (0, i)),
            vec, vec,
        ],
        out_specs=out_specs,
        scratch_shapes=[
            pltpu.VMEM((2, LANE, D_MODEL), F32),
            pltpu.VMEM((2, LANE, D_MODEL), F32),
            pltpu.SemaphoreType.DMA((2,)),
        ],
    )
    return pl.pallas_call(
        _combine_last_kernel if last else _combine_kernel,
        grid_spec=grid_spec,
        out_shape=out_shape,
        compiler_params=_cparams(("arbitrary",)),
        name="moe_combine_ln2",
    )(d0, d1, y_rows, h1, mf, g.reshape(1, -1), b.reshape(1, -1))


def kernel(x, meta_tokens, ln_in_g, ln_in_b, w_in, fox_f_bias, gla_wa2, gla_ba, gla_norm_g, conv_w, pool_w, pool_scale, gate_b, w_branch, w_out, ln1_g, ln1_b, router_w, router_b, w_gate, w_up, w_down, ln2_g, ln2_b):
    assert x.shape == (1, SEQ, D_MODEL)
    h, hb = _ln_in(x.reshape(SEQ, D_MODEL), meta_tokens, ln_in_g, ln_in_b)
    router_wt = router_w.T.reshape(N_GROUPS, EXPERTS_PER_GROUP, D_MODEL).transpose(1, 0, 2).reshape(
        N_EXPERTS, D_MODEL)
    router_bc = router_b.astype(F32).reshape(N_GROUPS, EXPERTS_PER_GROUP).T.reshape(N_EXPERTS, 1)

    wb_bf = w_branch.astype(BF16)
    wo_bf = w_out.astype(BF16)

    w_all = _transposed_weights(w_in)

    for l in range(DEPTH):
        zf = _matmul_nt(hb, w_all, l, W_FOX, 2 * MIX_W, BF16, TM_PROJ, 512, "proj_fox")
        vt = _matmul_tt(w_all, l, W_FOX + 2 * MIX_W, MIX_W, hb, BF16, TM_PROJ, "proj_fox_vt")
        z = _matmul_nt(hb, w_all, l, W_MIX, N_MIXC + N_SMALL, F32, TM_PROJ, 768, "proj_mix")

        bias_row = jnp.zeros((1, LANE), F32).at[0, SM_FF:SM_FF + FOX_HEADS].set(fox_f_bias[l])
        c = _fox_gate(z, bias_row)
        o_a = _fox_attention(zf, vt, c)

        wa2p = jnp.zeros((LANE, GLA_HEADS * GLA_DK), F32).at[SM_GA:SM_GA + GLA_RANK].set(gla_wa2[l])
        o_b = _gla(z, wa2p, gla_ba[l].reshape(1, -1), gla_norm_g[l].reshape(1, -1))

        o_c, o_d = _local_mixers(z, conv_w[l], pool_w[l].astype(BF16), pool_scale[l])

        mix = _merge(hb, o_a, o_b, o_c, o_d, w_all, gate_b[l], wb_bf, wo_bf, l)
        h1, mi, mf, counts = _ln1_route(h, mix, ln1_g[l], ln1_b[l], router_wt, router_bc)

        d0, d1, blk_start, n_blk, n_used, row_src, n_valid = _dispatch_tables(mi, counts[:, 0])
        hmid = _moe_up(blk_start, n_blk, n_used, row_src, n_valid, h1, w_gate, w_up, l)
        y_rows = _moe_down(blk_start, n_blk, n_used, hmid, w_down, l)
        if l + 1 < DEPTH:
            h, hb = _combine_ln2(d0, d1, y_rows, h1, mf, ln2_g[l], ln2_b[l], last=False)
        else:
            (out,) = _combine_ln2(d0, d1, y_rows, h1, mf, ln2_g[l], ln2_b[l], last=True)

    return out.reshape(1, SEQ, D_MODEL)
```

```python
import jax
import jax.numpy as jnp
import numpy as np
from jax import lax
from jax.experimental import pallas as pl
from jax.experimental.pallas import tpu as pltpu

F32 = jnp.float32
BF16 = jnp.bfloat16
I32 = jnp.int32
HIGHEST = lax.Precision.HIGHEST

D_MODEL = 2048
SEQ = 8192
DEPTH = 2
N_META = 16
N_BRANCH = 4
MIX_W = 512
FOX_HEADS = 4
FOX_HD = 128
GLA_HEADS = 4
GLA_DK = 64
GLA_DV = 128
GLA_RANK = 16
GLA_TAU = 16.0
CONV_K = 3
POOL_WINDOWS = (2, 4, 8, 16)
POOL_GW = 128
N_EXPERTS = 32
N_GROUPS = 8
EXPERTS_PER_GROUP = 4
TOP_K = 2
D_EXPERT = 1024
LN_EPS = 1e-5
DEEPNORM_ALPHA = (2 * DEPTH) ** 0.25

_SPLITS = (512, 512, 512, 4, 256, 256, 512, 16, 512, 512, 512, 512, 512, 8192)
_OFFS = [int(o) for o in np.concatenate([[0], np.cumsum(_SPLITS)])]
(O_FQ, O_FK, O_FV, O_FF, O_GQ, O_GK, O_GV, O_GA, O_GR, O_CB, O_CC, O_CV, O_PZ, O_GZ, P_IN) = _OFFS

LANE = 128
PAD_ROWS = LANE - N_META
T0 = PAD_ROWS
N_TOK = N_META + SEQ
R = PAD_ROWS + N_TOK
TM = 640
TM_PROJ = 1664
HALO = 16

WT_TILE = 128
W_FOX, W_MIX, W_SMALL, W_GATES = 0, 1536, 5120, 5376
N_FOX, N_MIXC, N_SMALL, N_GATES = 1536, 3584, 2 * LANE, 8192
N_WT = W_GATES + N_GATES
C_GQ, C_GK, C_GV, C_GR, C_CB, C_CC, C_CV, C_PZ = 0, 256, 512, 1024, 1536, 2048, 2560, 3072
C_SM = N_MIXC
SM_FF_TILE, SM_GA_TILE = O_FF // LANE, O_GA // LANE
SM_FF = O_FF - SM_FF_TILE * LANE
SM_GA = O_GA - SM_GA_TILE * LANE


def _wt_sources():
    src = []
    for r in range(0, N_WT, WT_TILE):
        if r < N_FOX:
            src.append(O_FQ + r)
        elif r < W_MIX + 1024:
            src.append(O_GQ + r - W_MIX)
        elif r < W_SMALL:
            src.append(O_GR + r - (W_MIX + 1024))
        elif r < W_GATES:
            src.append((SM_FF_TILE, SM_GA_TILE)[(r - W_SMALL) // LANE] * LANE)
        else:
            src.append(O_GZ + r - W_GATES)
    assert all(0 <= s and s + WT_TILE <= P_IN for s in src)
    return src


WT_SRC = _wt_sources()

EXPERT_BLOCK = 128
N_FLAT = N_TOK * TOP_K
N_BLOCKS = -(-N_FLAT // EXPERT_BLOCK) + N_EXPERTS
N_ROWS = N_BLOCKS * EXPERT_BLOCK
UP_AHEAD = 2
UP_X_SLOTS = UP_AHEAD + 1

NEG = -1e30
VMEM_LIMIT = 48 * 1024 * 1024
BIG_VMEM_LIMIT = 56 * 1024 * 1024
MOE_VMEM_LIMIT = BIG_VMEM_LIMIT


def _cparams(sem, vmem=VMEM_LIMIT):
    return pltpu.CompilerParams(dimension_semantics=sem, vmem_limit_bytes=vmem)


def _log_sigmoid(x):
    return jnp.minimum(x, 0.0) - jnp.log1p(jnp.exp(-jnp.abs(x)))


def _sigmoid(x):
    return 1.0 / (1.0 + jnp.exp(-x))


def _layer_norm_rows(x, g, b):
    mu = jnp.mean(x, axis=-1, keepdims=True)
    xc = x - mu
    var = jnp.mean(xc * xc, axis=-1, keepdims=True)
    return xc * lax.rsqrt(var + LN_EPS) * g + b


def _ln_in_kernel(x_ref, meta_ref, g_ref, b_ref, h_ref, hb_ref):
    i = pl.program_id(0)

    @pl.when(i == 0)
    def _():
        h_ref[...] = jnp.zeros_like(h_ref)
        hb_ref[...] = jnp.zeros_like(hb_ref)
        m = _layer_norm_rows(meta_ref[...], g_ref[...], b_ref[...])
        h_ref[PAD_ROWS:, :] = m
        hb_ref[PAD_ROWS:, :] = m.astype(BF16)

    @pl.when(i > 0)
    def _():
        y = _layer_norm_rows(x_ref[...], g_ref[...], b_ref[...])
        h_ref[...] = y
        hb_ref[...] = y.astype(BF16)


def _ln_in(x2d, meta, g, b):
    nb = R // LANE
    return pl.pallas_call(
        _ln_in_kernel,
        grid=(nb,),
        in_specs=[
            pl.BlockSpec((LANE, D_MODEL), lambda i: (jnp.maximum(i - 1, 0), 0)),
            pl.BlockSpec((N_META, D_MODEL), lambda i: (0, 0)),
            pl.BlockSpec((1, D_MODEL), lambda i: (0, 0)),
            pl.BlockSpec((1, D_MODEL), lambda i: (0, 0)),
        ],
        out_specs=[
            pl.BlockSpec((LANE, D_MODEL), lambda i: (i, 0)),
            pl.BlockSpec((LANE, D_MODEL), lambda i: (i, 0)),
        ],
        out_shape=[
            jax.ShapeDtypeStruct((R, D_MODEL), F32),
            jax.ShapeDtypeStruct((R, D_MODEL), BF16),
        ],
        compiler_params=_cparams(("arbitrary",)),
        name="ln_in",
    )(x2d, meta, g.reshape(1, -1), b.reshape(1, -1))


D_CHUNKS = D_MODEL // LANE
FLAT_ROWS = D_CHUNKS * DEPTH


def _wt_kernel(src_ref, w_hbm, o_ref, buf, sem):
    j = pl.program_id(0)
    slot = j % 2

    def tile_copy(t, s):
        row0 = pl.multiple_of(src_ref[t] * FLAT_ROWS, FLAT_ROWS)
        return pltpu.make_async_copy(w_hbm.at[pl.ds(row0, WT_TILE * FLAT_ROWS), :], buf.at[s],
                                     sem.at[s])

    @pl.when(j == 0)
    def _():
        tile_copy(0, 0).start()

    @pl.when(j + 1 < pl.num_programs(0))
    def _():
        tile_copy(j + 1, 1 - slot).start()

    tile_copy(j, slot).wait()
    for l in range(DEPTH):
        for c in range(D_CHUNKS):
            o_ref[l, :, c * LANE:(c + 1) * LANE] = buf[
                slot, pl.ds(c * DEPTH + l, WT_TILE, stride=FLAT_ROWS), :].astype(BF16)


def _transposed_weights(w_in):
    flat = w_in.reshape(DEPTH, D_CHUNKS, LANE, P_IN).transpose(3, 1, 0, 2).reshape(
        P_IN * FLAT_ROWS, LANE)
    grid_spec = pltpu.PrefetchScalarGridSpec(
        num_scalar_prefetch=1,
        grid=(N_WT // WT_TILE,),
        in_specs=[pl.BlockSpec(memory_space=pl.ANY)],
        out_specs=pl.BlockSpec((DEPTH, WT_TILE, D_MODEL), lambda j, src: (0, j, 0)),
        scratch_shapes=[
            pltpu.VMEM((2, WT_TILE * FLAT_ROWS, LANE), F32),
            pltpu.SemaphoreType.DMA((2,)),
        ],
    )
    return pl.pallas_call(
        _wt_kernel,
        grid_spec=grid_spec,
        out_shape=jax.ShapeDtypeStruct((DEPTH, N_WT, D_MODEL), BF16),
        compiler_params=_cparams(("arbitrary",)),
        name="transposed_weights",
    )(jnp.asarray(WT_SRC, I32), flat)


_NT = (((1,), (1,)), ((), ()))


def _mm_nt_kernel(a_ref, wt_ref, o_ref):
    o_ref[...] = lax.dot_general(a_ref[...], wt_ref[...], _NT,
                                 preferred_element_type=F32).astype(o_ref.dtype)


def _matmul_nt(a, wt, l, row0, n, out_dtype, tm, tn, name):
    m, k = a.shape
    return pl.pallas_call(
        _mm_nt_kernel,
        grid=(m // tm, n // tn),
        in_specs=[
            pl.BlockSpec((tm, k), lambda i, j: (i, 0)),
            pl.BlockSpec((None, tn, k), lambda i, j: (l, row0 // tn + j, 0)),
        ],
        out_specs=pl.BlockSpec((tm, tn), lambda i, j: (i, j)),
        out_shape=jax.ShapeDtypeStruct((m, n), out_dtype),
        compiler_params=_cparams(("parallel", "arbitrary")),
        name=name,
    )(a, wt)


def _matmul_tt(wt, l, row0, n, a, out_dtype, tm, name):
    m, k = a.shape
    return pl.pallas_call(
        _mm_nt_kernel,
        grid=(m // tm,),
        in_specs=[
            pl.BlockSpec((None, n, k), lambda i: (l, row0 // n, 0)),
            pl.BlockSpec((tm, k), lambda i: (i, 0)),
        ],
        out_specs=pl.BlockSpec((n, tm), lambda i: (0, i)),
        out_shape=jax.ShapeDtypeStruct((n, m), out_dtype),
        compiler_params=_cparams(("parallel",)),
        name=name,
    )(wt, a)


def _fox_gate_kernel(zs_ref, bias_ref, c_ref, carry_ref):
    i = pl.program_id(0)

    @pl.when(i == 0)
    def _():
        carry_ref[...] = jnp.zeros_like(carry_ref)

    rows = i * TM + lax.broadcasted_iota(I32, (TM, LANE), 0)
    lf = _log_sigmoid(zs_ref[...] + bias_ref[...])
    lf = jnp.where(rows >= T0, lf, 0.0)
    tri = (lax.broadcasted_iota(I32, (TM, TM), 0)
           >= lax.broadcasted_iota(I32, (TM, TM), 1)).astype(F32)
    c = jnp.dot(tri, lf, precision=HIGHEST, preferred_element_type=F32) + carry_ref[...]
    c_ref[...] = c
    carry_ref[...] = c[TM - 1:TM, :]


def _fox_gate(z, bias_row):
    return pl.pallas_call(
        _fox_gate_kernel,
        grid=(R // TM,),
        in_specs=[
            pl.BlockSpec((TM, LANE), lambda i: (i, C_SM // LANE)),
            pl.BlockSpec((1, LANE), lambda i: (0, 0)),
        ],
        out_specs=pl.BlockSpec((TM, LANE), lambda i: (i, 0)),
        out_shape=jax.ShapeDtypeStruct((R, LANE), F32),
        scratch_shapes=[pltpu.VMEM((1, LANE), F32)],
        compiler_params=_cparams(("arbitrary",)),
        name="fox_gate",
    )(z, bias_row)


TQ = TM
N_QB = R // TQ
_PAIRS = [(qi, kj) for qi in range(N_QB) for kj in range(qi + 1)]
N_PAIRS = len(_PAIRS)


LOG2E = 1.4426950408889634
FOX_HPS = FOX_HEADS


def _fox_kernel(qi_tab, kj_tab, q_ref, k_ref, vt_ref, ck_ref, o_ref, m_sc, l_sc, acc_sc):
    p = pl.program_id(1)
    qi = qi_tab[p]
    kj = kj_tab[p]

    @pl.when(kj == 0)
    def _():
        m_sc[...] = jnp.full_like(m_sc, NEG)
        l_sc[...] = jnp.zeros_like(l_sc)
        acc_sc[...] = jnp.zeros_like(acc_sc)

    kpos = kj * TQ + lax.broadcasted_iota(I32, (TQ, 1), 0)
    c1 = FOX_HD ** -0.5 * LOG2E

    def step(causal):
        for hh in range(FOX_HPS):
            lanes = slice(hh * FOX_HD, (hh + 1) * FOX_HD)
            ck = ck_ref[:, SM_FF + hh:SM_FF + hh + 1]
            ckl = jnp.where(kpos >= T0, ck * LOG2E, -NEG)
            t = lax.dot_general(k_ref[:, lanes], q_ref[:, lanes], (((1,), (1,)), ((), ())),
                                preferred_element_type=F32) * c1 - ckl
            if causal:
                ahead = (lax.broadcasted_iota(I32, (TQ, TQ), 0)
                         - lax.broadcasted_iota(I32, (TQ, TQ), 1))
                t = jnp.where(ahead <= 0, t, NEG)
            m_prev = m_sc[hh]
            m_new = jnp.maximum(m_prev, jnp.max(t, axis=0, keepdims=True))
            alpha = jnp.exp2(m_prev - m_new)
            pr = jnp.exp2(t - m_new)
            l_sc[hh] = alpha * l_sc[hh] + jnp.sum(pr, axis=0, keepdims=True)
            acc_sc[hh] = alpha * acc_sc[hh] + jnp.dot(vt_ref[lanes, :], pr.astype(BF16),
                                                      preferred_element_type=F32)
            m_sc[hh] = m_new

    @pl.when(kj == qi)
    def _():
        step(True)

    @pl.when(kj != qi)
    def _():
        step(False)

    @pl.when(kj == qi)
    def _():
        for hh in range(FOX_HPS):
            o_ref[:, hh * FOX_HD:(hh + 1) * FOX_HD] = (acc_sc[hh] / l_sc[hh]).T.astype(o_ref.dtype)


def _fox_attention(zf, vt, c_col):
    qi_tab = jnp.asarray([p[0] for p in _PAIRS], I32)
    kj_tab = jnp.asarray([p[1] for p in _PAIRS], I32)
    hw = FOX_HPS * FOX_HD
    grid_spec = pltpu.PrefetchScalarGridSpec(
        num_scalar_prefetch=2,
        grid=(FOX_HEADS // FOX_HPS, N_PAIRS),
        in_specs=[
            pl.BlockSpec((TQ, hw), lambda h, p, qt, kt: (qt[p], h)),
            pl.BlockSpec((TQ, hw), lambda h, p, qt, kt: (kt[p], FOX_HEADS // FOX_HPS + h)),
            pl.BlockSpec((hw, TQ), lambda h, p, qt, kt: (h, kt[p])),
            pl.BlockSpec((TQ, LANE), lambda h, p, qt, kt: (kt[p], 0)),
        ],
        out_specs=pl.BlockSpec((TQ, hw), lambda h, p, qt, kt: (qt[p], h)),
        scratch_shapes=[
            pltpu.VMEM((FOX_HPS, 1, TQ), F32),
            pltpu.VMEM((FOX_HPS, 1, TQ), F32),
            pltpu.VMEM((FOX_HPS, FOX_HD, TQ), F32),
        ],
    )
    return pl.pallas_call(
        _fox_kernel,
        grid_spec=grid_spec,
        out_shape=jax.ShapeDtypeStruct((R, MIX_W), BF16),
        compiler_params=_cparams(("parallel", "arbitrary")),
        name="fox_attention",
    )(qi_tab, kj_tab, zf, zf, vt, c_col)


GLA_CHUNK = 64
GLA_UNROLL = 5


def _gla_kernel(q_ref, k_ref, v_ref, gr_ref, zs_ref, wa2_ref, ba_ref, gn_ref, o_ref, st_ref, la_ref):
    i = pl.program_id(0)

    @pl.when(i == 0)
    def _():
        st_ref[...] = jnp.zeros_like(st_ref)

    la = jnp.dot(zs_ref[...], wa2_ref[...], precision=HIGHEST, preferred_element_type=F32)
    la_ref[...] = _log_sigmoid(la + ba_ref[...]) * (1.0 / GLA_TAU)

    c_r = lax.broadcasted_iota(I32, (GLA_CHUNK, GLA_CHUNK), 0)
    c_c = lax.broadcasted_iota(I32, (GLA_CHUNK, GLA_CHUNK), 1)
    tri_b = c_r >= c_c
    tri = jnp.where(tri_b, 1.0, 0.0).astype(BF16)

    def chunk(c, carry):
        r0 = pl.multiple_of(c * GLA_CHUNK, GLA_CHUNK)
        rows = pl.ds(r0, GLA_CHUNK)
        g = la_ref[rows, :]
        g_hi = g.astype(BF16)
        g_lo = (g - g_hi.astype(F32)).astype(BF16)
        b = (jnp.dot(tri, g_hi, preferred_element_type=F32)
             + jnp.dot(tri, g_lo, preferred_element_type=F32))
        b_last = b[GLA_CHUNK - 1:GLA_CHUNK, :]
        b_mid = b[GLA_CHUNK // 2 - 1:GLA_CHUNK // 2, :]
        e_last = jnp.exp(b_last)
        q = q_ref[rows, :] * (GLA_DK ** -0.5)
        k = k_ref[rows, :]
        qm = q * jnp.exp(b - b_mid)
        km = k * jnp.exp(b_mid - b)
        qs = q * jnp.exp(b)
        kh = k * jnp.exp(b_last - b)
        for h in range(GLA_HEADS):
            ks = slice(h * GLA_DK, (h + 1) * GLA_DK)
            vs = slice(h * GLA_DV, (h + 1) * GLA_DV)
            kh_h = kh[:, ks].astype(BF16)
            v_h = v_ref[rows, vs]
            att = lax.dot_general(qm[:, ks].astype(BF16), km[:, ks].astype(BF16),
                                  (((1,), (1,)), ((), ())), preferred_element_type=F32)
            att = jnp.where(tri_b, att, 0.0)
            st = st_ref[h]
            o = jnp.dot(att.astype(BF16), v_h.astype(BF16), preferred_element_type=F32)
            o = o + lax.dot_general(qs[:, ks].astype(BF16), st.astype(BF16),
                                    (((1,), (1,)), ((), ())), preferred_element_type=F32)
            st_ref[h] = st * e_last[:, ks] + jnp.dot(v_h.T.astype(BF16), kh_h,
                                                     preferred_element_type=F32)
            ms = jnp.mean(o * o, axis=-1, keepdims=True)
            on = o * lax.rsqrt(ms + LN_EPS) * gn_ref[:, vs]
            gate = gr_ref[rows, vs]
            o_ref[rows, vs] = (on * (gate * _sigmoid(gate))).astype(o_ref.dtype)
        return carry

    lax.fori_loop(0, TM // GLA_CHUNK, chunk, 0, unroll=GLA_UNROLL)


def _gla(z, wa2p, ba, gn):
    return pl.pallas_call(
        _gla_kernel,
        grid=(R // TM,),
        in_specs=[
            pl.BlockSpec((TM, 256), lambda i: (i, C_GQ // 256)),
            pl.BlockSpec((TM, 256), lambda i: (i, C_GK // 256)),
            pl.BlockSpec((TM, 512), lambda i: (i, C_GV // 512)),
            pl.BlockSpec((TM, 512), lambda i: (i, C_GR // 512)),
            pl.BlockSpec((TM, LANE), lambda i: (i, C_SM // LANE + 1)),
            pl.BlockSpec((LANE, 256), lambda i: (0, 0)),
            pl.BlockSpec((1, 256), lambda i: (0, 0)),
            pl.BlockSpec((1, 512), lambda i: (0, 0)),
        ],
        out_specs=pl.BlockSpec((TM, MIX_W), lambda i: (i, 0)),
        out_shape=jax.ShapeDtypeStruct((R, MIX_W), BF16),
        scratch_shapes=[
            pltpu.VMEM((GLA_HEADS, GLA_DV, GLA_DK), F32),
            pltpu.VMEM((TM, GLA_HEADS * GLA_DK), F32),
        ],
        compiler_params=_cparams(("arbitrary",)),
        name="gla",
    )(z, z, z, z, z, wa2p, ba, gn)


def _local_kernel(cb_ref, cc_ref, cv_ref, pz_ref, cw_ref, pw_ref, ps_ref, oc_ref, od_ref, u_sc, p_sc):
    i = pl.program_id(0)

    @pl.when(i == 0)
    def _():
        u_sc[0:HALO, :] = jnp.zeros((HALO, MIX_W), F32)
        p_sc[0:HALO, :] = jnp.zeros((HALO, MIX_W), F32)

    @pl.when(i > 0)
    def _():
        u_sc[0:HALO, :] = u_sc[TM:TM + HALO, :]
        p_sc[0:HALO, :] = p_sc[TM:TM + HALO, :]

    u = cc_ref[...] * cv_ref[...]
    pz = pz_ref[...]
    u_sc[HALO:, :] = u
    p_sc[HALO:, :] = pz

    y = (cw_ref[2:3, :] * u + cw_ref[1:2, :] * u_sc[HALO - 1:HALO - 1 + TM, :]
         + cw_ref[0:1, :] * u_sc[HALO - 2:HALO - 2 + TM, :])
    oc_ref[...] = (cb_ref[...] * y).astype(oc_ref.dtype)

    tok = i * TM - T0 + lax.broadcasted_iota(I32, (TM, 1), 0)
    cnt_small = jnp.maximum(tok + 1, 1).astype(F32)
    for g, w in enumerate(POOL_WINDOWS):
        cols = slice(g * POOL_GW, (g + 1) * POOL_GW)
        x = pz[:, cols]
        s = x
        for j in range(1, w):
            s = s + p_sc[HALO - j:HALO - j + TM, cols]
        inv_cnt = jnp.where(tok + 1 >= w, 1.0 / w, 1.0 / cnt_small)
        pooled = s * inv_cnt - x
        od = jnp.dot(pooled.astype(BF16), pw_ref[g], preferred_element_type=F32)
        od_ref[:, cols] = (od * ps_ref[:, cols]).astype(od_ref.dtype)


def _local_mixers(z, conv_w, pool_w_bf, pool_scale):
    cw = jnp.zeros((8, MIX_W), F32).at[:CONV_K].set(conv_w)
    blk = lambda c: pl.BlockSpec((TM, MIX_W), lambda i, c=c: (i, c // MIX_W))
    return pl.pallas_call(
        _local_kernel,
        grid=(R // TM,),
        in_specs=[
            blk(C_CB), blk(C_CC), blk(C_CV), blk(C_PZ),
            pl.BlockSpec((8, MIX_W), lambda i: (0, 0)),
            pl.BlockSpec((len(POOL_WINDOWS), POOL_GW, POOL_GW), lambda i: (0, 0, 0)),
            pl.BlockSpec((1, MIX_W), lambda i: (0, 0)),
        ],
        out_specs=[
            pl.BlockSpec((TM, MIX_W), lambda i: (i, 0)),
            pl.BlockSpec((TM, MIX_W), lambda i: (i, 0)),
        ],
        out_shape=[
            jax.ShapeDtypeStruct((R, MIX_W), BF16),
            jax.ShapeDtypeStruct((R, MIX_W), BF16),
        ],
        scratch_shapes=[
            pltpu.VMEM((TM + HALO, MIX_W), F32),
            pltpu.VMEM((TM + HALO, MIX_W), F32),
        ],
        compiler_params=_cparams(("arbitrary",)),
        name="conv_pool",
    )(z, z, z, z, cw, pool_w_bf, pool_scale.reshape(1, -1))


TN_MERGE = 256


def _merge_kernel(hb_ref, oa_ref, ob_ref, oc_ref, od_ref, wg0_ref, wg1_ref, wg2_ref, wg3_ref,
                  gb_ref, wb_ref, wo_ref, out_ref):
    j = pl.program_id(1)

    @pl.when(j == 0)
    def _():
        out_ref[...] = jnp.zeros_like(out_ref)

    hb = hb_ref[...]
    mixed = None
    for b, (o_ref, wg_ref) in enumerate(((oa_ref, wg0_ref), (ob_ref, wg1_ref),
                                         (oc_ref, wg2_ref), (od_ref, wg3_ref))):
        gate = _sigmoid(lax.dot_general(hb, wg_ref[...], _NT, preferred_element_type=F32)
                        + gb_ref[b:b + 1, :])
        proj = jnp.dot(o_ref[...], wb_ref[b], preferred_element_type=F32)
        term = gate * proj
        mixed = term if mixed is None else mixed + term
    out_ref[...] += jnp.dot(mixed.astype(BF16), wo_ref[...], preferred_element_type=F32)


def _merge(hb, o_a, o_b, o_c, o_d, w_all, gate_b, wb_bf, wo_bf, l):
    tn = TN_MERGE
    nj = D_MODEL // tn
    row = lambda w: pl.BlockSpec((TM, w), lambda i, j: (i, 0))
    wg = lambda b: pl.BlockSpec((None, tn, D_MODEL),
                                lambda i, j, b=b: (l, W_GATES // tn + b * nj + j, 0))
    return pl.pallas_call(
        _merge_kernel,
        grid=(R // TM, nj),
        in_specs=[
            row(D_MODEL), row(MIX_W), row(MIX_W), row(MIX_W), row(MIX_W),
            wg(0), wg(1), wg(2), wg(3),
            pl.BlockSpec((N_BRANCH, tn), lambda i, j: (0, j)),
            pl.BlockSpec((None, N_BRANCH, MIX_W, tn), lambda i, j: (l, 0, 0, j)),
            pl.BlockSpec((None, tn, D_MODEL), lambda i, j: (l, j, 0)),
        ],
        out_specs=pl.BlockSpec((TM, D_MODEL), lambda i, j: (i, 0)),
        out_shape=jax.ShapeDtypeStruct((R, D_MODEL), F32),
        compiler_params=_cparams(("parallel", "arbitrary"), vmem=BIG_VMEM_LIMIT),
        name="merge",
    )(hb, o_a, o_b, o_c, o_d, w_all, w_all, w_all, w_all, gate_b, wb_bf, wo_bf)


def _post_ln(h, delta, g, b, row0):
    y = _layer_norm_rows(DEEPNORM_ALPHA * h + delta, g, b)
    rows = row0 + lax.broadcasted_iota(I32, (y.shape[0], 1), 0)
    return jnp.where(rows >= T0, y, 0.0)


def _first_of(cands, target):
    idx = jnp.full(target.shape, len(cands) - 1, I32)
    for j in range(len(cands) - 2, -1, -1):
        idx = jnp.where(cands[j] == target, j, idx)
    return idx


def _pick(cands, idx):
    out = cands[-1]
    for j in range(len(cands) - 2, -1, -1):
        out = jnp.where(idx == j, cands[j], out)
    return out


def _ln1_route_kernel(h_ref, mix_ref, g_ref, b_ref, rwt_ref, rb_ref,
                      h1_ref, mi_ref, mf_ref, cnt_ref, carry_sc):
    i = pl.program_id(0)

    @pl.when(i == 0)
    def _():
        carry_sc[...] = jnp.zeros_like(carry_sc)

    y = _post_ln(h_ref[...], mix_ref[...], g_ref[...], b_ref[...], i * TM)
    h1_ref[...] = y

    logits = lax.dot_general(rwt_ref[...], y, (((1,), (1,)), ((), ())), precision=HIGHEST,
                             preferred_element_type=F32)
    aff = _sigmoid(logits)
    sel = aff + rb_ref[...]
    xs = [sel[j * N_GROUPS:(j + 1) * N_GROUPS, :] for j in range(EXPERTS_PER_GROUP)]
    afs = [aff[j * N_GROUPS:(j + 1) * N_GROUPS, :] for j in range(EXPERTS_PER_GROUP)]

    score = None
    for a in range(EXPERTS_PER_GROUP):
        for bb in range(a + 1, EXPERTS_PER_GROUP):
            pair = xs[a] + xs[bb]
            score = pair if score is None else jnp.maximum(score, pair)
    giota = lax.broadcasted_iota(I32, (N_GROUPS, TM), 0)
    gmax = jnp.max(score, axis=0, keepdims=True)
    grp = jnp.min(jnp.where(score == gmax, giota, N_GROUPS), axis=0, keepdims=True)
    gsel = giota == grp
    cs = [jnp.max(jnp.where(gsel, x, -jnp.inf), axis=0, keepdims=True) for x in xs]
    acs = [jnp.sum(jnp.where(gsel, a, 0.0), axis=0, keepdims=True) for a in afs]

    m1 = jnp.maximum(jnp.maximum(cs[0], cs[1]), jnp.maximum(cs[2], cs[3]))
    i0 = _first_of(cs, m1)
    ds = [jnp.where(i0 == j, -jnp.inf, cs[j]) for j in range(EXPERTS_PER_GROUP)]
    m2 = jnp.maximum(jnp.maximum(ds[0], ds[1]), jnp.maximum(ds[2], ds[3]))
    i1 = _first_of(ds, m2)
    a0 = _pick(acs, i0)
    a1 = _pick(acs, i1)
    denom = a0 + a1

    pos = i * TM + lax.broadcasted_iota(I32, (1, TM), 1)
    valid = pos >= T0
    riota = lax.broadcasted_iota(I32, (N_EXPERTS, TM), 0)
    oh0 = (riota == i0 * N_GROUPS + grp) & valid
    oh1 = (riota == i1 * N_GROUPS + grp) & valid
    ohf = jnp.where(oh0 | oh1, 1.0, 0.0)
    before = (lax.broadcasted_iota(I32, (TM, TM), 0)
              < lax.broadcasted_iota(I32, (TM, TM), 1)).astype(BF16)
    cum = jnp.dot(ohf.astype(BF16), before, preferred_element_type=F32) + carry_sc[...]
    rank0 = jnp.sum(jnp.where(oh0, cum, 0.0), axis=0, keepdims=True)
    rank1 = jnp.sum(jnp.where(oh1, cum, 0.0), axis=0, keepdims=True)
    carry = carry_sc[...] + jnp.sum(ohf, axis=1, keepdims=True)
    carry_sc[...] = carry
    cnt_ref[...] = jnp.broadcast_to(carry, cnt_ref.shape)

    zi = jnp.zeros((1, TM), I32)
    mi_ref[...] = jnp.concatenate(
        [grp * EXPERTS_PER_GROUP + i0, grp * EXPERTS_PER_GROUP + i1,
         rank0.astype(I32), rank1.astype(I32), zi, zi, zi, zi], axis=0)
    zf = jnp.zeros((1, TM), F32)
    mf_ref[...] = jnp.concatenate([a0 / denom, a1 / denom, zf, zf, zf, zf, zf, zf], axis=0)


def _ln1_route(h, mix, g, b, router_wt, router_bc):
    row = pl.BlockSpec((TM, D_MODEL), lambda i: (i, 0))
    vec = pl.BlockSpec((1, D_MODEL), lambda i: (0, 0))
    meta = pl.BlockSpec((8, TM), lambda i: (0, i))
    return pl.pallas_call(
        _ln1_route_kernel,
        grid=(R // TM,),
        in_specs=[row, row, vec, vec,
                  pl.BlockSpec((N_EXPERTS, D_MODEL), lambda i: (0, 0)),
                  pl.BlockSpec((N_EXPERTS, 1), lambda i: (0, 0))],
        out_specs=[row, meta, meta, pl.BlockSpec((N_EXPERTS, LANE), lambda i: (0, 0))],
        out_shape=[
            jax.ShapeDtypeStruct((R, D_MODEL), F32),
            jax.ShapeDtypeStruct((8, R), I32),
            jax.ShapeDtypeStruct((8, R), F32),
            jax.ShapeDtypeStruct((N_EXPERTS, LANE), F32),
        ],
        scratch_shapes=[pltpu.VMEM((N_EXPERTS, 1), F32)],
        compiler_params=_cparams(("arbitrary",)),
        name="ln1_route",
    )(h, mix, g.reshape(1, -1), b.reshape(1, -1), router_wt, router_bc)


N_ROW_SRC = N_ROWS + UP_AHEAD * EXPERT_BLOCK


def _row_src_kernel(d0_ref, d1_ref, out_ref):
    def clear(i, carry):
        out_ref[i] = 0
        return carry

    lax.fori_loop(0, N_ROW_SRC, clear, 0, unroll=16)

    def place(t, carry):
        out_ref[d0_ref[t]] = t
        out_ref[d1_ref[t]] = t
        return carry

    lax.fori_loop(T0, R, place, 0, unroll=8)


def _row_sources(d0, d1):
    smem = pl.BlockSpec(memory_space=pltpu.SMEM)
    return pl.pallas_call(
        _row_src_kernel,
        in_specs=[smem, smem],
        out_specs=smem,
        out_shape=jax.ShapeDtypeStruct((N_ROW_SRC,), I32),
        name="row_sources",
    )(d0, d1)


def _dispatch_tables(mi, counts_slot_major):
    counts = counts_slot_major.reshape(EXPERTS_PER_GROUP, N_GROUPS).T.reshape(N_EXPERTS).astype(I32)
    padded = (counts + EXPERT_BLOCK - 1) // EXPERT_BLOCK * EXPERT_BLOCK
    pad_end = jnp.cumsum(padded)
    pad_start = pad_end - padded
    e_iota = jnp.arange(N_EXPERTS, dtype=I32)
    rows_ok = jnp.arange(R) >= T0

    def dest(eid, rank):
        start = jnp.sum(jnp.where(eid[:, None] == e_iota[None, :], pad_start[None, :], 0), axis=1)
        return jnp.where(rows_ok, start + rank, 0).astype(I32)

    d0 = dest(mi[0], mi[2])
    d1 = dest(mi[1], mi[3])
    blk_start = (pad_start // EXPERT_BLOCK).astype(I32)
    n_blk = (padded // EXPERT_BLOCK).astype(I32)
    n_used = (pad_end[-1] // EXPERT_BLOCK).astype(I32).reshape(1)
    row_src = _row_sources(d0, d1)
    blk = jnp.arange(N_BLOCKS + UP_AHEAD, dtype=I32)
    owner = (blk[:, None] >= blk_start[None, :]) & (blk[:, None] < (blk_start + n_blk)[None, :])
    left = counts[None, :] - (blk[:, None] - blk_start[None, :]) * EXPERT_BLOCK
    n_valid = jnp.sum(jnp.where(owner, jnp.clip(left, 0, EXPERT_BLOCK), 0), axis=1).astype(I32)
    return d0, d1, blk_start, n_blk, n_used, row_src, n_valid


def _row_copy(src, src_row, dst, dst_row, sem):
    return pltpu.make_async_copy(src.at[pl.ds(src_row, 1), :], dst.at[pl.ds(dst_row, 1), :], sem)


def _issue_row_gather(rs_ref, nv_ref, g, h_hbm, buf, sem, full=False):
    base = g * EXPERT_BLOCK
    nv = nv_ref[g]
    for r in range(EXPERT_BLOCK):
        if full:
            _row_copy(h_hbm, rs_ref[base + r], buf, r, sem).start(priority=r % 2)
        else:
            @pl.when(r < nv)
            def _():
                _row_copy(h_hbm, rs_ref[base + r], buf, r, sem).start(priority=r % 2)


def _wait_row_gather(nv_ref, g, h_hbm, buf, sem, full=False):
    if full:
        pltpu.make_async_copy(h_hbm.at[pl.ds(0, EXPERT_BLOCK), :], buf, sem).wait()
        return
    nv = nv_ref[g]
    k = EXPERT_BLOCK
    while k >= 1:
        @pl.when((nv & k) != 0)
        def _():
            pltpu.make_async_copy(h_hbm.at[pl.ds(0, k), :], buf.at[pl.ds(0, k), :], sem).wait()
        k //= 2


def _block_rows(g):
    return pl.ds(pl.multiple_of(g * EXPERT_BLOCK, EXPERT_BLOCK), EXPERT_BLOCK)


def _finish_writes(out_copy, obuf, nu):
    n_slots = obuf.shape[0]
    for back in range(1, n_slots + 1):
        @pl.when(nu >= back)
        def _():
            out_copy(nu - back, (nu - back) % n_slots).wait()

    obuf[0] = jnp.zeros(obuf.shape[1:], obuf.dtype)

    def zero_block(g, carry):
        cp = out_copy(g, 0)
        cp.start()
        cp.wait()
        return carry

    lax.fori_loop(nu, N_BLOCKS, zero_block, 0)


def _moe_up_kernel(bs_ref, nb_ref, nu_ref, rs_ref, nv_ref, h_hbm, wg_ref, wu_ref, o_hbm,
                   wg_sc, wu_sc, xbuf, xb_sc, obuf, xsem, osem):
    e = pl.program_id(0)
    nb = nb_ref[e]
    g0 = bs_ref[e]

    def out_copy(g, slot):
        return pltpu.make_async_copy(obuf.at[slot], o_hbm.at[_block_rows(g), :], osem.at[slot])

    @pl.when(e == 0)
    def _():
        xbuf[...] = jnp.zeros_like(xbuf)
        for g in range(UP_AHEAD):
            _issue_row_gather(rs_ref, nv_ref, g, h_hbm, xbuf.at[g], xsem.at[g])

    @pl.when(nb > 0)
    def _():
        wg_sc[...] = wg_ref[...].astype(BF16)
        wu_sc[...] = wu_ref[...].astype(BF16)

        def block(j, carry, full):
            g = g0 + j
            slot = g % 2
            xs = g % UP_X_SLOTS
            xs_next = (g + UP_AHEAD) % UP_X_SLOTS

            @pl.when(g >= 2)
            def _():
                out_copy(g - 2, slot).wait()

            _wait_row_gather(nv_ref, g, h_hbm, xbuf.at[xs], xsem.at[xs], full)
            xb_sc[...] = xbuf[xs].astype(BF16)
            _issue_row_gather(rs_ref, nv_ref, g + UP_AHEAD, h_hbm, xbuf.at[xs_next],
                              xsem.at[xs_next], full)
            x = xb_sc[...]
            gate = jnp.dot(x, wg_sc[...], preferred_element_type=F32)
            up = jnp.dot(x, wu_sc[...], preferred_element_type=F32)
            obuf[slot] = (gate * _sigmoid(gate) * up).astype(BF16)
            out_copy(g, slot).start()
            return carry

        n_full = jnp.maximum(nb - 1 - UP_AHEAD, 0)
        lax.fori_loop(0, n_full, lambda j, c: block(j, c, True), 0)
        lax.fori_loop(n_full, nb, lambda j, c: block(j, c, False), 0)

    @pl.when(e == N_EXPERTS - 1)
    def _():
        _finish_writes(out_copy, obuf, nu_ref[0])


def _moe_up(blk_start, n_blk, n_used, row_src, n_valid, h1, w_gate, w_up, l):
    wspec = pl.BlockSpec((None, None, D_MODEL, D_EXPERT), lambda e, *_: (l, e, 0, 0))
    grid_spec = pltpu.PrefetchScalarGridSpec(
        num_scalar_prefetch=5,
        grid=(N_EXPERTS,),
        in_specs=[pl.BlockSpec(memory_space=pl.ANY), wspec, wspec],
        out_specs=pl.BlockSpec(memory_space=pl.ANY),
        scratch_shapes=[
            pltpu.VMEM((D_MODEL, D_EXPERT), BF16),
            pltpu.VMEM((D_MODEL, D_EXPERT), BF16),
            pltpu.VMEM((UP_X_SLOTS, EXPERT_BLOCK, D_MODEL), F32),
            pltpu.VMEM((EXPERT_BLOCK, D_MODEL), BF16),
            pltpu.VMEM((2, EXPERT_BLOCK, D_EXPERT), BF16),
            pltpu.SemaphoreType.DMA((UP_X_SLOTS,)),
            pltpu.SemaphoreType.DMA((2,)),
        ],
    )
    return pl.pallas_call(
        _moe_up_kernel,
        grid_spec=grid_spec,
        out_shape=jax.ShapeDtypeStruct((N_ROWS, D_EXPERT), BF16),
        compiler_params=_cparams(("arbitrary",), vmem=MOE_VMEM_LIMIT),
        name="moe_up",
    )(blk_start, n_blk, n_used, row_src, n_valid, h1, w_gate, w_up)


DOWN_IN_SLOTS = 4
DOWN_OUT_SLOTS = 3


def _moe_down_kernel(bs_ref, nb_ref, nu_ref, x_hbm, wd_ref, y_hbm, wd_sc, xbuf, obuf, xsem, osem):
    e = pl.program_id(0)
    nb = nb_ref[e]
    g0 = bs_ref[e]
    nu = nu_ref[0]

    def in_copy(g):
        s = g % DOWN_IN_SLOTS
        return pltpu.make_async_copy(x_hbm.at[_block_rows(g), :], xbuf.at[s], xsem.at[s])

    def out_copy(g, slot):
        return pltpu.make_async_copy(obuf.at[slot], y_hbm.at[_block_rows(g), :], osem.at[slot])

    @pl.when(e == 0)
    def _():
        for g in range(DOWN_IN_SLOTS - 1):
            @pl.when(g < nu)
            def _():
                in_copy(g).start()

    @pl.when(nb > 0)
    def _():
        wd_sc[...] = wd_ref[...].astype(BF16)

        def block(j, carry):
            g = g0 + j
            slot = g % DOWN_OUT_SLOTS

            @pl.when(g >= DOWN_OUT_SLOTS)
            def _():
                out_copy(g - DOWN_OUT_SLOTS, slot).wait()

            in_copy(g).wait()

            @pl.when(g + DOWN_IN_SLOTS - 1 < nu)
            def _():
                in_copy(g + DOWN_IN_SLOTS - 1).start()

            obuf[slot] = jnp.dot(xbuf[g % DOWN_IN_SLOTS], wd_sc[...], preferred_element_type=F32)
            out_copy(g, slot).start()
            return carry

        lax.fori_loop(0, nb, block, 0)

    @pl.when(e == N_EXPERTS - 1)
    def _():
        _finish_writes(out_copy, obuf, nu)


def _moe_down(blk_start, n_blk, n_used, hmid, w_down, l):
    grid_spec = pltpu.PrefetchScalarGridSpec(
        num_scalar_prefetch=3,
        grid=(N_EXPERTS,),
        in_specs=[
            pl.BlockSpec(memory_space=pl.ANY),
            pl.BlockSpec((None, None, D_EXPERT, D_MODEL), lambda e, *_: (l, e, 0, 0)),
        ],
        out_specs=pl.BlockSpec(memory_space=pl.ANY),
        scratch_shapes=[
            pltpu.VMEM((D_EXPERT, D_MODEL), BF16),
            pltpu.VMEM((DOWN_IN_SLOTS, EXPERT_BLOCK, D_EXPERT), BF16),
            pltpu.VMEM((DOWN_OUT_SLOTS, EXPERT_BLOCK, D_MODEL), F32),
            pltpu.SemaphoreType.DMA((DOWN_IN_SLOTS,)),
            pltpu.SemaphoreType.DMA((DOWN_OUT_SLOTS,)),
        ],
    )
    return pl.pallas_call(
        _moe_down_kernel,
        grid_spec=grid_spec,
        out_shape=jax.ShapeDtypeStruct((N_ROWS, D_MODEL), F32),
        compiler_params=_cparams(("arbitrary",), vmem=MOE_VMEM_LIMIT),
        name="moe_down",
    )(blk_start, n_blk, n_used, hmid, w_down)


def _combine_kernel(d0_ref, d1_ref, y_hbm, h1_ref, mf_ref, g_ref, b_ref, h2_ref, h2b_ref,
                    buf0, buf1, ffn_sc, sem):
    i = pl.program_id(0)
    last = pl.num_programs(0) - 1
    slot = i % 2

    def issue(tile, s):
        base = tile * LANE
        for r in range(LANE):
            _row_copy(y_hbm, d0_ref[base + r], buf0.at[s], r, sem.at[s]).start(priority=0)
            _row_copy(y_hbm, d1_ref[base + r], buf1.at[s], r, sem.at[s]).start(priority=1)

    def wait_tile(s):
        pltpu.make_async_copy(y_hbm.at[pl.ds(0, LANE), :], buf0.at[s], sem.at[s]).wait()
        pltpu.make_async_copy(y_hbm.at[pl.ds(0, LANE), :], buf1.at[s], sem.at[s]).wait()

    @pl.when(i == 0)
    def _():
        issue(0, 0)

    wait_tile(slot)
    wt = mf_ref[...].T
    ffn_sc[...] = wt[:, 0:1] * buf0[slot] + wt[:, 1:2] * buf1[slot]
    issue(jnp.minimum(i + 1, last), 1 - slot)
    y = _post_ln(h1_ref[...], ffn_sc[...], g_ref[...], b_ref[...], i * LANE)
    h2_ref[...] = y
    if h2b_ref is not None:
        h2b_ref[...] = y.astype(BF16)

    @pl.when(i == last)
    def _():
        wait_tile(1 - slot)


def _combine_last_kernel(d0_ref, d1_ref, y_hbm, h1_ref, mf_ref, g_ref, b_ref, out_ref,
                         buf0, buf1, ffn_sc, sem):
    _combine_kernel(d0_ref, d1_ref, y_hbm, h1_ref, mf_ref, g_ref, b_ref, out_ref, None,
                    buf0, buf1, ffn_sc, sem)


def _combine_ln2(d0, d1, y_rows, h1, mf, g, b, last):
    row = lambda i, d0, d1: (i, 0)
    vec = pl.BlockSpec((1, D_MODEL), lambda i, d0, d1: (0, 0))
    if last:
        frames = lambda i, d0, d1: (jnp.maximum(i - 1, 0), 0)
        out_specs = [pl.BlockSpec((LANE, D_MODEL), frames)]
        out_shape = [jax.ShapeDtypeStruct((SEQ, D_MODEL), F32)]
    else:
        out_specs = [pl.BlockSpec((LANE, D_MODEL), row), pl.BlockSpec((LANE, D_MODEL), row)]
        out_shape = [jax.ShapeDtypeStruct((R, D_MODEL), F32), jax.ShapeDtypeStruct((R, D_MODEL), BF16)]
    grid_spec = pltpu.PrefetchScalarGridSpec(
        num_scalar_prefetch=2,
        grid=(R // LANE,),
        in_specs=[
            pl.BlockSpec(memory_space=pl.ANY),
            pl.BlockSpec((LANE, D_MODEL), row),
            pl.BlockSpec((8, LANE), lambda i, d0, d1: (0, i)),
            vec, vec,
        ],
        out_specs=out_specs,
        scratch_shapes=[
            pltpu.VMEM((2, LANE, D_MODEL), F32),
            pltpu.VMEM((2, LANE, D_MODEL), F32),
            pltpu.VMEM((LANE, D_MODEL), F32),
            pltpu.SemaphoreType.DMA((2,)),
        ],
    )
    return pl.pallas_call(
        _combine_last_kernel if last else _combine_kernel,
        grid_spec=grid_spec,
        out_shape=out_shape,
        compiler_params=_cparams(("arbitrary",)),
        name="moe_combine_ln2",
    )(d0, d1, y_rows, h1, mf, g.reshape(1, -1), b.reshape(1, -1))


def kernel(x, meta_tokens, ln_in_g, ln_in_b, w_in, fox_f_bias, gla_wa2, gla_ba, gla_norm_g, conv_w, pool_w, pool_scale, gate_b, w_branch, w_out, ln1_g, ln1_b, router_w, router_b, w_gate, w_up, w_down, ln2_g, ln2_b):
    assert x.shape == (1, SEQ, D_MODEL)
    h, hb = _ln_in(x.reshape(SEQ, D_MODEL), meta_tokens, ln_in_g, ln_in_b)
    router_wt = router_w.T.reshape(N_GROUPS, EXPERTS_PER_GROUP, D_MODEL).transpose(1, 0, 2).reshape(
        N_EXPERTS, D_MODEL)
    router_bc = router_b.astype(F32).reshape(N_GROUPS, EXPERTS_PER_GROUP).T.reshape(N_EXPERTS, 1)

    wb_bf = w_branch.astype(BF16)
    wo_bf = w_out.astype(BF16)

    w_all = _transposed_weights(w_in)

    for l in range(DEPTH):
        zf = _matmul_nt(hb, w_all, l, W_FOX, 2 * MIX_W, BF16, TM_PROJ, 512, "proj_fox")
        vt = _matmul_tt(w_all, l, W_FOX + 2 * MIX_W, MIX_W, hb, BF16, TM_PROJ, "proj_fox_vt")
        z = _matmul_nt(hb, w_all, l, W_MIX, N_MIXC + N_SMALL, F32, TM_PROJ, 768, "proj_mix")

        bias_row = jnp.zeros((1, LANE), F32).at[0, SM_FF:SM_FF + FOX_HEADS].set(fox_f_bias[l])
        c = _fox_gate(z, bias_row)
        o_a = _fox_attention(zf, vt, c)

        wa2p = jnp.zeros((LANE, GLA_HEADS * GLA_DK), F32).at[SM_GA:SM_GA + GLA_RANK].set(gla_wa2[l])
        o_b = _gla(z, wa2p, gla_ba[l].reshape(1, -1), gla_norm_g[l].reshape(1, -1))

        o_c, o_d = _local_mixers(z, conv_w[l], pool_w[l].astype(BF16), pool_scale[l])

        mix = _merge(hb, o_a, o_b, o_c, o_d, w_all, gate_b[l], wb_bf, wo_bf, l)
        h1, mi, mf, counts = _ln1_route(h, mix, ln1_g[l], ln1_b[l], router_wt, router_bc)

        d0, d1, blk_start, n_blk, n_used, row_src, n_valid = _dispatch_tables(mi, counts[:, 0])
        hmid = _moe_up(blk_start, n_blk, n_used, row_src, n_valid, h1, w_gate, w_up, l)
        y_rows = _moe_down(blk_start, n_blk, n_used, hmid, w_down, l)
        if l + 1 < DEPTH:
            h, hb = _combine_ln2(d0, d1, y_rows, h1, mf, ln2_g[l], ln2_b[l], last=False)
        else:
            (out,) = _combine_ln2(d0, d1, y_rows, h1, mf, ln2_g[l], ln2_b[l], last=True)

    return out.reshape(1, SEQ, D_MODEL)
```
